```python
import math
import jax, jax.numpy as jnp
from jax import lax
import numpy as np

D_MODEL = 1024
BATCH = 16
SEQ = 256
DEPTH = 2
DEC_BATCH = 2
DEC_SEQ = 2048
PAST_LEN = 512

GRID_W = 64
N_EVEN = (DEPTH + 1) // 2
N_ODD = DEPTH // 2
RET_HEADS = 4
RET_DK = 128
RET_DV = 128
RET_CHUNK = 128
DIFF_HEADS = 4
DIFF_DK = 64
DIFF_DV = 2 * DIFF_DK
Q_BLOCK = 128
ROPE_THETA = 10000.0
RET_QK_W = RET_HEADS * RET_DK
RET_V_W = RET_HEADS * RET_DV
DIFF_QK_W = DIFF_HEADS * 2 * DIFF_DK
DIFF_V_W = DIFF_HEADS * DIFF_DV
IN_W = 2 * RET_QK_W + 2 * RET_V_W + 2 * DIFF_QK_W + DIFF_V_W
MIX_W = RET_V_W + DIFF_V_W
CONV_K = 31
CONV_PAD = CONV_K // 2
N_GROUPS = 4
EXPERTS_PER_GROUP = 8
N_EXPERTS = N_GROUPS * EXPERTS_PER_GROUP
D_EXPERT = 512
ALPHA = (2.0 * DEPTH) ** 0.25
BETA = (8.0 * DEPTH) ** -0.25
LN_EPS = 1e-5
GN_EPS = 1e-6

kernel_name = 'hybrid_retention_diffattn_conformer_hmoe_diffusion_step'


def layer_norm(x, g, b):
    xf = x.astype(jnp.float32)
    mu = jnp.mean(xf, -1, keepdims=True)
    var = jnp.mean(jnp.square(xf - mu), -1, keepdims=True)
    y = (xf - mu) * lax.rsqrt(var + LN_EPS) * g.astype(jnp.float32) + b.astype(jnp.float32)
    return y.astype(x.dtype)


def rope_1d(x, pos):
    half = x.shape[-1] // 2
    inv = ROPE_THETA ** (-jnp.arange(half, dtype=jnp.float32) / half)
    ang = pos.astype(jnp.float32)[:, None] * inv[None, :]
    cos, sin = jnp.cos(ang), jnp.sin(ang)
    xf = x.astype(jnp.float32)
    x1, x2 = xf[..., :half], xf[..., half:]
    return jnp.concatenate([x1 * cos - x2 * sin, x1 * sin + x2 * cos], -1).astype(x.dtype)


def rope_2d(x, row, col):
    h = x.shape[-1] // 2
    return jnp.concatenate([rope_1d(x[..., :h], row), rope_1d(x[..., h:], col)], -1)


def retention_scan(q, k, v, log_gamma, s0):
    B, H, N, dk = q.shape
    dv = v.shape[-1]
    C = RET_CHUNK
    nc = N // C
    lg = log_gamma.astype(jnp.float32)
    idx = jnp.arange(C, dtype=jnp.float32)
    rel = idx[:, None] - idx[None, :]
    inner_decay = jnp.where(rel[None] >= 0, jnp.exp(jnp.maximum(rel, 0.0)[None] * lg[:, None, None]), 0.0)
    q_decay = jnp.exp((idx + 1.0)[None] * lg[:, None])[:, :, None]
    k_decay = jnp.exp((C - 1.0 - idx)[None] * lg[:, None])[:, :, None]
    chunk_decay = jnp.exp(C * lg)[:, None, None]

    def to_chunks(t):
        return jnp.moveaxis(t.astype(jnp.float32).reshape(B, H, nc, C, t.shape[-1]), 2, 0)

    def step(s, qkv):
        qc, kc, vc = qkv
        scores = jnp.einsum('bhid,bhjd->bhij', qc, kc) * inner_decay
        o = jnp.einsum('bhij,bhje->bhie', scores, vc) + jnp.einsum('bhid,bhde->bhie', qc * q_decay, s)
        s = s * chunk_decay + jnp.einsum('bhjd,bhje->bhde', kc * k_decay, vc)
        return s, o

    s, o = lax.scan(step, s0.astype(jnp.float32), (to_chunks(q), to_chunks(k), to_chunks(v)))
    o = jnp.moveaxis(o, 0, 2).reshape(B, H, N, dv)
    return o, s


def diff_attention(q, k, v, lam):
    B, H, N, dq = q.shape
    nb = N // Q_BLOCK
    scale = DIFF_DK ** -0.5
    k1, k2 = k[..., :DIFF_DK], k[..., DIFF_DK:]
    vf = v.astype(jnp.float32)
    qb = jnp.moveaxis(q.reshape(B, H, nb, Q_BLOCK, dq), 2, 0)

    def block(qblk):
        s1 = jnp.einsum('bhqd,bhkd->bhqk', qblk[..., :DIFF_DK], k1).astype(jnp.float32) * scale
        s2 = jnp.einsum('bhqd,bhkd->bhqk', qblk[..., DIFF_DK:], k2).astype(jnp.float32) * scale
        w = jax.nn.softmax(s1, -1) - lam * jax.nn.softmax(s2, -1)
        return jnp.einsum('bhqk,bhkd->bhqd', w, vf)

    o = lax.map(block, qb)
    return jnp.moveaxis(o, 0, 2).reshape(B, H, N, v.shape[-1])


def even_mixer(u, w_in, w_out, dec_f, dec_b, lq1, lk1, lq2, lk2, subln_g, lam_init, ctx, grid):
    B, N, _ = u.shape
    proj = u @ w_in
    offs = [int(o) for o in np.cumsum([RET_QK_W, RET_QK_W, RET_V_W, RET_V_W, DIFF_QK_W, DIFF_QK_W])]
    qr, kr, vr, gr, qd, kd, vd = jnp.split(proj, offs, axis=-1)

    def heads(t, h):
        return t.reshape(B, N, h, -1).transpose(0, 2, 1, 3)

    qr = heads(qr, RET_HEADS)
    kr = heads(kr, RET_HEADS) * (RET_DK ** -0.5)
    vr = heads(vr, RET_HEADS)
    if ctx is None:
        s0f = jnp.zeros((B, RET_HEADS, RET_DK, RET_DV), jnp.float32)
        s0b = s0f
    else:
        ck, cv, s0f, s0b = ctx
    of, sf = retention_scan(qr, kr, vr, -jnp.exp(dec_f.astype(jnp.float32)), s0f)
    ob, sb = retention_scan(jnp.flip(qr, 2), jnp.flip(kr, 2), jnp.flip(vr, 2), -jnp.exp(dec_b.astype(jnp.float32)), s0b)
    r = of + jnp.flip(ob, 2)
    mu = jnp.mean(r, -1, keepdims=True)
    var = jnp.mean(jnp.square(r - mu), -1, keepdims=True)
    r = ((r - mu) * lax.rsqrt(var + GN_EPS)).transpose(0, 2, 1, 3).reshape(B, N, RET_V_W).astype(u.dtype)
    r = jax.nn.silu(gr) * r

    qd = heads(qd, DIFF_HEADS)
    kd = heads(kd, DIFF_HEADS)
    vd = heads(vd, DIFF_HEADS)
    if grid is None:
        q_use, k_all, v_all = qd, kd, vd
    else:
        row, col = grid

        def rope_pair(t):
            return jnp.concatenate([rope_2d(t[..., :DIFF_DK], row, col), rope_2d(t[..., DIFF_DK:], row, col)], -1)

        q_use = rope_pair(qd)
        k_all = jnp.concatenate([rope_pair(kd), ck.astype(kd.dtype)], axis=2)
        v_all = jnp.concatenate([vd, cv.astype(vd.dtype)], axis=2)
    lam = (jnp.exp(jnp.sum(lq1.astype(jnp.float32) * lk1.astype(jnp.float32)))
           - jnp.exp(jnp.sum(lq2.astype(jnp.float32) * lk2.astype(jnp.float32))) + lam_init)
    o = diff_attention(q_use, k_all, v_all, lam)
    o = o * lax.rsqrt(jnp.mean(jnp.square(o), -1, keepdims=True) + LN_EPS) * subln_g.astype(jnp.float32) * (1.0 - lam_init)
    o = o.transpose(0, 2, 1, 3).reshape(B, N, DIFF_V_W).astype(u.dtype)

    out = jnp.concatenate([r, o], -1) @ w_out
    return out, (kd, vd, sf, sb)


def conv_mixer(u, w1, b1, dw, dwb, ln_g, ln_b, w2, b2):
    h = u @ w1 + b1
    a, gt = jnp.split(h, 2, axis=-1)
    h = a * jax.nn.sigmoid(gt)
    h = lax.conv_general_dilated(h, dw[:, None, :].astype(h.dtype), window_strides=(1,),
                                 padding=[(CONV_PAD, CONV_PAD)], dimension_numbers=('NWC', 'WIO', 'NWC'),
                                 feature_group_count=h.shape[-1]) + dwb
    h = jax.nn.silu(layer_norm(h, ln_g, ln_b))
    return h @ w2 + b2


def hier_moe(u, wg, bg, we, be, w_gate, w_up, w_down):
    B, N, D = u.shape
    t = u.reshape(-1, D)
    gl = (t @ wg + bg).astype(jnp.float32)
    gp = jax.nn.softmax(gl, -1)
    gsel = jnp.argmax(gl, -1)
    p_g = jnp.take_along_axis(gp, gsel[:, None], -1)[:, 0]
    el = (jnp.einsum('td,gde->tge', t, we) + be).astype(jnp.float32)
    el = jnp.take_along_axis(el, gsel[:, None, None], 1)[:, 0]
    top_v, top_i = lax.top_k(el, 2)
    top_w = jax.nn.softmax(top_v, -1) * p_g[:, None]
    eidx = gsel[:, None] * EXPERTS_PER_GROUP + top_i
    comb = jnp.sum(jax.nn.one_hot(eidx, N_EXPERTS, dtype=jnp.float32) * top_w[..., None], axis=1)
    hg = jnp.einsum('td,edf->tef', t, w_gate)
    hu = jnp.einsum('td,edf->tef', t, w_up)
    h = jax.nn.silu(hg) * hu * comb[..., None].astype(t.dtype)
    y = jnp.einsum('tef,efd->td', h, w_down)
    return y.reshape(B, N, D)


def setup_inputs(seed: int = 0) -> dict:
    key = jax.random.key(seed)
    keys = iter(jax.random.split(key, 64))

    def nrm(shape, scale):
        return jax.random.normal(next(keys), shape, jnp.float32) * scale

    D = D_MODEL
    base_decay = jnp.log(-jnp.log(1.0 - 2.0 ** (-5.0 - jnp.arange(RET_HEADS, dtype=jnp.float32))))
    return {
        'x_prompt': nrm((BATCH, SEQ, D), 1.0),
        'x_sample': nrm((DEC_BATCH, DEC_SEQ, D), 1.0),
        'cache_diff_k': nrm((DEC_BATCH, N_EVEN, DIFF_HEADS, PAST_LEN, 2 * DIFF_DK), 1.0),
        'cache_diff_v': nrm((DEC_BATCH, N_EVEN, DIFF_HEADS, PAST_LEN, DIFF_DV), 1.0),
        'state_ret_fwd': nrm((DEC_BATCH, N_EVEN, RET_HEADS, RET_DK, RET_DV), 0.5),
        'state_ret_bwd': nrm((DEC_BATCH, N_EVEN, RET_HEADS, RET_DK, RET_DV), 0.5),
        'c': nrm((DEC_BATCH, D), 1.0),
        'c_ctx': nrm((D,), 1.0),
        'mod_w': nrm((DEPTH, D, 6 * D), D ** -0.5),
        'mod_b': nrm((DEPTH, 6 * D), 0.01),
        'ln1_g': 1.0 + nrm((DEPTH, D), 0.02),
        'ln1_b': nrm((DEPTH, D), 0.01),
        'ln2_g': 1.0 + nrm((DEPTH, D), 0.02),
        'ln2_b': nrm((DEPTH, D), 0.01),
        'mix_w_in': nrm((N_EVEN, D, IN_W), D ** -0.5),
        'mix_w_out': nrm((N_EVEN, MIX_W, D), BETA * MIX_W ** -0.5),
        'ret_decay_fwd': base_decay[None, :] + nrm((N_EVEN, RET_HEADS), 0.05),
        'ret_decay_bwd': base_decay[None, :] + nrm((N_EVEN, RET_HEADS), 0.05),
        'diff_lq1': nrm((N_EVEN, DIFF_DK), 0.1),
        'diff_lk1': nrm((N_EVEN, DIFF_DK), 0.1),
        'diff_lq2': nrm((N_EVEN, DIFF_DK), 0.1),
        'diff_lk2': nrm((N_EVEN, DIFF_DK), 0.1),
        'diff_subln_g': 1.0 + nrm((N_EVEN, DIFF_DV), 0.02),
        'conv_w1': nrm((N_ODD, D, 2 * D), D ** -0.5),
        'conv_b1': nrm((N_ODD, 2 * D), 0.01),
        'conv_dw': nrm((N_ODD, CONV_K, D), CONV_K ** -0.5),
        'conv_dw_b': nrm((N_ODD, D), 0.01),
        'conv_ln_g': 1.0 + nrm((N_ODD, D), 0.02),
        'conv_ln_b': nrm((N_ODD, D), 0.01),
        'conv_w2': nrm((N_ODD, D, D), BETA * D ** -0.5),
        'conv_b2': nrm((N_ODD, D), 0.01),
        'router_g_w': nrm((DEPTH, D, N_GROUPS), D ** -0.5),
        'router_g_b': nrm((DEPTH, N_GROUPS), 0.01),
        'router_e_w': nrm((DEPTH, N_GROUPS, D, EXPERTS_PER_GROUP), D ** -0.5),
        'router_e_b': nrm((DEPTH, N_GROUPS, EXPERTS_PER_GROUP), 0.01),
        'moe_w_gate': nrm((DEPTH, N_EXPERTS, D, D_EXPERT), D ** -0.5),
        'moe_w_up': nrm((DEPTH, N_EXPERTS, D, D_EXPERT), D ** -0.5),
        'moe_w_down': nrm((DEPTH, N_EXPERTS, D_EXPERT, D), BETA * D_EXPERT ** -0.5),
    }


def reference(x_prompt, x_sample, cache_diff_k, cache_diff_v, state_ret_fwd, state_ret_bwd, c, c_ctx,
              mod_w, mod_b, ln1_g, ln1_b, ln2_g, ln2_b,
              mix_w_in, mix_w_out, ret_decay_fwd, ret_decay_bwd,
              diff_lq1, diff_lk1, diff_lq2, diff_lk2, diff_subln_g,
              conv_w1, conv_b1, conv_dw, conv_dw_b, conv_ln_g, conv_ln_b, conv_w2, conv_b2,
              router_g_w, router_g_b, router_e_w, router_e_b,
              moe_w_gate, moe_w_up, moe_w_down):

    def run_layer(li, x, cond, ctx, grid):
        mod = (jax.nn.silu(cond) @ mod_w[li] + mod_b[li])[:, None, :]
        sh1, sc1, g1, sh2, sc2, g2 = jnp.split(mod, 6, axis=-1)
        u = x * (1.0 + sc1) + sh1
        new_ctx = None
        if li % 2 == 0:
            e = li // 2
            lam_init = 0.8 - 0.6 * math.exp(-0.3 * li)
            out, new_ctx = even_mixer(u, mix_w_in[e], mix_w_out[e], ret_decay_fwd[e], ret_decay_bwd[e],
                                      diff_lq1[e], diff_lk1[e], diff_lq2[e], diff_lk2[e], diff_subln_g[e],
                                      lam_init, ctx, grid)
        else:
            o = li // 2
            out = conv_mixer(u, conv_w1[o], conv_b1[o], conv_dw[o], conv_dw_b[o], conv_ln_g[o], conv_ln_b[o],
                             conv_w2[o], conv_b2[o])
        x = layer_norm(ALPHA * x + g1 * out, ln1_g[li], ln1_b[li])
        u = x * (1.0 + sc2) + sh2
        f = hier_moe(u, router_g_w[li], router_g_b[li], router_e_w[li], router_e_b[li],
                     moe_w_gate[li], moe_w_up[li], moe_w_down[li])
        x = layer_norm(ALPHA * x + g2 * f, ln2_g[li], ln2_b[li])
        return x, new_ctx

    h = x_prompt
    ks, vs, sfs, sbs = [], [], [], []
    for li in range(DEPTH):
        h, nc = run_layer(li, h, c_ctx[None, :], None, None)
        if nc is not None:
            ks.append(nc[0])
            vs.append(nc[1])
            sfs.append(nc[2])
            sbs.append(nc[3])
    y_prompt = h
    new_cache_diff_k = jnp.stack(ks, axis=1)
    new_cache_diff_v = jnp.stack(vs, axis=1)
    new_state_ret_fwd = jnp.stack(sfs, axis=1)
    new_state_ret_bwd = jnp.stack(sbs, axis=1)

    n_lat = x_sample.shape[1]
    grid_rows = n_lat // GRID_W
    t = jnp.arange(grid_rows * GRID_W)
    grid = (t // GRID_W, t % GRID_W)
    h = x_sample
    for li in range(DEPTH):
        if li % 2 == 0:
            e = li // 2
            ctx = (cache_diff_k[:, e], cache_diff_v[:, e], state_ret_fwd[:, e], state_ret_bwd[:, e])
        else:
            ctx = None
        h, _ = run_layer(li, h, c, ctx, grid)
    y_sample = h

    return (y_prompt, y_sample, new_cache_diff_k, new_cache_diff_v, new_state_ret_fwd, new_state_ret_bwd)
```

```python
import functools
import math

import numpy as np
import jax
import jax.numpy as jnp
from jax import lax
from jax.experimental import pallas as pl
from jax.experimental.pallas import tpu as pltpu

F32 = jnp.float32
BF16 = jnp.bfloat16

D = 1024
BATCH = 16
SEQ = 256
DEPTH = 2
DEC_BATCH = 2
DEC_SEQ = 2048
PAST_LEN = 512
GRID_W = 64
HEADS = 4
HEAD_W = 128
RET_CHUNK = 128
DIFF_DK = 64
ROPE_THETA = 10000.0
IN_W = 7 * HEADS * HEAD_W
CONV_K = 31
CONV_PAD = CONV_K // 2
N_GROUPS = 4
EXPERTS_PER_GROUP = 8
N_EXPERTS = N_GROUPS * EXPERTS_PER_GROUP
D_EXPERT = 512
ALPHA = (2.0 * DEPTH) ** 0.25
LN_EPS = 1e-5
GN_EPS = 1e-6

T_PROMPT = BATCH * SEQ
T_SAMPLE = DEC_BATCH * DEC_SEQ
T = T_PROMPT + T_SAMPLE
TM = 256
N_TILES = T // TM
PROMPT_TILES = T_PROMPT // TM
SAMPLE_TILES_PER_SEQ = DEC_SEQ // TM
MOD_ROWS = 8
MOE_TM = 256
EXPERT_CAP = T
MOE_MAX_TILES = (2 * T) // MOE_TM + N_EXPERTS
LANES = 128
ROUTER_LANE0 = N_GROUPS
VMEM_LIMIT = 52 * 1024 * 1024


def _cparams(sem):
    return pltpu.CompilerParams(dimension_semantics=sem, vmem_limit_bytes=VMEM_LIMIT)


def _tile_cond_row(i):
    return jnp.where(i < PROMPT_TILES, 0, 1 + (i - PROMPT_TILES) // SAMPLE_TILES_PER_SEQ)


def _mod_spec(li, k):
    return pl.BlockSpec((1, 1, D), lambda i, *_: ((li * MOD_ROWS + _tile_cond_row(i)) * 6 + k, 0, 0))


def _row_spec(shape):
    return pl.BlockSpec(shape, lambda i, *_: (0,) * len(shape))


def _layer_norm(x, g, b):
    mu = jnp.mean(x, axis=-1, keepdims=True)
    xc = x - mu
    var = jnp.mean(xc * xc, axis=-1, keepdims=True)
    return xc * lax.rsqrt(var + LN_EPS) * g + b


def _silu(x):
    return x * jax.nn.sigmoid(x)


def _dot(a, b):
    return jnp.dot(a, b, preferred_element_type=F32)


def _dot_nt(a, b):
    return lax.dot_general(a, b, (((1,), (1,)), ((), ())), preferred_element_type=F32)


def _dot_tn(a, b):
    return lax.dot_general(a, b, (((0,), (0,)), ((), ())), preferred_element_type=F32)


MOD_TN = 512


def _mod_kernel(cond_ref, w_ref, b_ref, o_ref):
    s = _silu(cond_ref[...])
    o_ref[0] = jnp.dot(s, w_ref[0], precision=lax.Precision.HIGHEST,
                       preferred_element_type=F32) + b_ref[0]


def _mod_vectors(cond, mod_w, mod_b):
    return pl.pallas_call(
        _mod_kernel,
        grid=(DEPTH, 6 * D // MOD_TN),
        in_specs=[
            pl.BlockSpec((MOD_ROWS, D), lambda l, j: (0, 0)),
            pl.BlockSpec((1, D, MOD_TN), lambda l, j: (l, 0, j)),
            pl.BlockSpec((1, 1, MOD_TN), lambda l, j: (l, 0, j)),
        ],
        out_specs=pl.BlockSpec((1, MOD_ROWS, MOD_TN), lambda l, j: (l, 0, j)),
        out_shape=jax.ShapeDtypeStruct((DEPTH, MOD_ROWS, 6 * D), F32),
        compiler_params=_cparams(("arbitrary", "arbitrary")),
        name="mod_vectors",
    )(cond, mod_w, mod_b.reshape(DEPTH, 1, 6 * D))


def _in_proj_kernel(x_ref, sh_ref, sc_ref, w_ref, o_ref):
    u = x_ref[...] * (1.0 + sc_ref[0]) + sh_ref[0]
    o_ref[...] = _dot(u.astype(BF16), w_ref[...])


def _in_proj(x, modr, w_in_bf16, li):
    return pl.pallas_call(
        _in_proj_kernel,
        grid=(N_TILES,),
        in_specs=[
            pl.BlockSpec((TM, D), lambda i: (i, 0)),
            _mod_spec(li, 0),
            _mod_spec(li, 1),
            _row_spec((D, IN_W)),
        ],
        out_specs=pl.BlockSpec((TM, IN_W), lambda i: (i, 0)),
        out_shape=jax.ShapeDtypeStruct((T, IN_W), F32),
        compiler_params=_cparams(("arbitrary",)),
        name="in_proj",
    )(x, modr, modr, w_in_bf16)


COL_QR, COL_KR, COL_VR, COL_GR, COL_QD, COL_KD, COL_VD = (k * HEADS for k in range(7))


def _retention_kernel(dec_ref, q_ref, k_ref, v_ref, g_ref, *rest, n_chunks, has_state, emit_state):
    rest = list(rest)
    if has_state:
        s0f_ref, s0b_ref = rest[:2]
        rest = rest[2:]
    r_ref = rest[0]
    rest = rest[1:]
    if emit_state:
        sf_ref, sb_ref = rest[:2]
        rest = rest[2:]
    of_ref = rest[0]

    h = pl.program_id(1)
    C = RET_CHUNK
    ii = lax.broadcasted_iota(jnp.int32, (C, C), 0)
    jj = lax.broadcasted_iota(jnp.int32, (C, C), 1)
    rel = (ii - jj).astype(F32)
    idx = lax.broadcasted_iota(jnp.int32, (C, 1), 0).astype(F32)
    k_scale = HEAD_W ** -0.5

    def chunk(ref, c):
        return ref[c * C:(c + 1) * C, :]

    def run(direction):
        lg = -jnp.exp(jnp.full((1, 1), dec_ref[direction * HEADS + h], F32))
        if direction == 0:
            inner = jnp.where(rel >= 0, jnp.exp(jnp.maximum(rel, 0.0) * lg), 0.0)
            q_decay = jnp.exp((idx + 1.0) * lg)
            k_decay = jnp.exp((C - 1.0 - idx) * lg)
            order = range(n_chunks)
        else:
            inner = jnp.where(rel <= 0, jnp.exp(jnp.maximum(-rel, 0.0) * lg), 0.0)
            q_decay = jnp.exp((C - idx) * lg)
            k_decay = jnp.exp(idx * lg)
            order = range(n_chunks - 1, -1, -1)
        chunk_decay = jnp.exp(C * lg)
        if has_state:
            s = (s0f_ref if direction == 0 else s0b_ref)[0, 0, 0]
        else:
            s = jnp.zeros((HEAD_W, HEAD_W), F32)
        for c in order:
            qc = chunk(q_ref, c)
            kc = chunk(k_ref, c) * k_scale
            vc = chunk(v_ref, c).astype(BF16)
            scores = _dot_nt(qc.astype(BF16), kc.astype(BF16)) * inner
            o = _dot(scores.astype(BF16), vc) + _dot((qc * q_decay).astype(BF16), s.astype(BF16))
            s = s * chunk_decay + _dot_tn((kc * k_decay).astype(BF16), vc)
            if direction == 0:
                of_ref[c * C:(c + 1) * C, :] = o
            else:
                r = of_ref[c * C:(c + 1) * C, :] + o
                mu = jnp.mean(r, axis=-1, keepdims=True)
                rc = r - mu
                var = jnp.mean(rc * rc, axis=-1, keepdims=True)
                rn = rc * lax.rsqrt(var + GN_EPS)
                r_ref[c * C:(c + 1) * C, :] = _silu(chunk(g_ref, c)) * rn
        return s

    sf = run(0)
    sb = run(1)
    if emit_state:
        sf_ref[0, 0, 0] = sf
        sb_ref[0, 0, 0] = sb


def _retention(proj, dec, n_seq, seq_len, row_block0, s0f=None, s0b=None, e=0, emit_state=False):
    has_state = s0f is not None

    def col(base):
        return pl.BlockSpec((seq_len, HEAD_W), lambda b, h, *_: (row_block0 + b, base + h))

    state_spec = pl.BlockSpec((1, 1, 1, HEAD_W, HEAD_W), lambda b, h, *_: (b, e, h, 0, 0))
    in_specs = [pl.BlockSpec(memory_space=pltpu.SMEM), col(COL_QR), col(COL_KR), col(COL_VR), col(COL_GR)]
    args = [dec, proj, proj, proj, proj]
    if has_state:
        in_specs += [state_spec, state_spec]
        args += [s0f, s0b]
    out_specs = [pl.BlockSpec((seq_len, HEAD_W), lambda b, h, *_: (b, h))]
    out_shape = [jax.ShapeDtypeStruct((n_seq * seq_len, HEADS * HEAD_W), F32)]
    if emit_state:
        st = pl.BlockSpec((1, 1, 1, HEAD_W, HEAD_W), lambda b, h, *_: (b, 0, h, 0, 0))
        out_specs += [st, st]
        out_shape += [jax.ShapeDtypeStruct((n_seq, 1, HEADS, HEAD_W, HEAD_W), F32)] * 2
    return pl.pallas_call(
        functools.partial(_retention_kernel, n_chunks=seq_len // RET_CHUNK,
                          has_state=has_state, emit_state=emit_state),
        grid=(n_seq, HEADS),
        in_specs=in_specs,
        out_specs=out_specs,
        out_shape=out_shape,
        scratch_shapes=[pltpu.VMEM((seq_len, HEAD_W), F32)],
        compiler_params=_cparams(("arbitrary", "arbitrary")),
        name=f"retention_{seq_len}",
    )(*args)


def _diff_lambda(lq1_ref, lk1_ref, lq2_ref, lk2_ref, lam_init):
    a = jnp.sum(lq1_ref[...] * lk1_ref[...], axis=-1, keepdims=True)
    b = jnp.sum(lq2_ref[...] * lk2_ref[...], axis=-1, keepdims=True)
    return jnp.exp(a) - jnp.exp(b) + lam_init


def _diff_attend(q, k, v, lam, subln_g, lam_init):
    lane = lax.broadcasted_iota(jnp.int32, q.shape, 1)
    q1 = jnp.where(lane < DIFF_DK, q, 0.0).astype(BF16)
    q2 = jnp.where(lane >= DIFF_DK, q, 0.0).astype(BF16)

    def softmax(qz):
        s = _dot_nt(qz, k)
        p = jnp.exp(s - jnp.max(s, axis=-1, keepdims=True))
        return p * (1.0 / jnp.sum(p, axis=-1, keepdims=True))

    w = softmax(q1) - lam * softmax(q2)
    o = _dot(w.astype(BF16), v)
    o = o * lax.rsqrt(jnp.mean(o * o, axis=-1, keepdims=True) + LN_EPS)
    return o * subln_g * (1.0 - lam_init)


def _attn_prompt_kernel(q_ref, k_ref, v_ref, lq1, lk1, lq2, lk2, g_ref, o_ref, *, lam_init):
    lam = _diff_lambda(lq1, lk1, lq2, lk2, lam_init)
    scale = DIFF_DK ** -0.5
    for h in range(HEADS):
        sl = slice(h * HEAD_W, (h + 1) * HEAD_W)
        o_ref[:, sl] = _diff_attend(q_ref[:, sl] * scale, k_ref[:, sl].astype(BF16),
                                    v_ref[:, sl].astype(BF16), lam, g_ref[...], lam_init)


def _attn_prompt(proj, lq1, lk1, lq2, lk2, subln_g, lam_init):
    W = HEADS * HEAD_W

    def slab(base):
        return pl.BlockSpec((SEQ, W), lambda b: (b, base // HEADS))

    small = _row_spec((1, DIFF_DK))
    return pl.pallas_call(
        functools.partial(_attn_prompt_kernel, lam_init=lam_init),
        grid=(BATCH,),
        in_specs=[slab(COL_QD), slab(COL_KD), slab(COL_VD), small, small, small, small,
                  _row_spec((1, HEAD_W))],
        out_specs=pl.BlockSpec((SEQ, W), lambda b: (b, 0)),
        out_shape=jax.ShapeDtypeStruct((T_PROMPT, W), F32),
        compiler_params=_cparams(("arbitrary",)),
        name="diff_attn_prompt",
    )(proj, proj, proj, lq1, lk1, lq2, lk2, subln_g)


def _rope(x, cos, sin_signed):
    lane = lax.broadcasted_iota(jnp.int32, x.shape, 1)
    partner = jnp.where((lane % 32) < 16, pltpu.roll(x, LANES - 16, 1), pltpu.roll(x, 16, 1))
    return x * cos + partner * sin_signed


def _attn_sample_kernel(q_ref, k_ref, v_ref, ck_ref, cv_ref, cosq_ref, sinq_ref, cos_ref, sin_ref,
                        lq1, lk1, lq2, lk2, g_ref, o_ref, kbuf, vbuf, *, lam_init):
    @pl.when(pl.program_id(2) == 0)
    def _():
        kbuf[0:DEC_SEQ, :] = _rope(k_ref[...], cos_ref[...], sin_ref[...]).astype(BF16)
        kbuf[DEC_SEQ:, :] = ck_ref[0, 0, 0].astype(BF16)
        vbuf[0:DEC_SEQ, :] = v_ref[...].astype(BF16)
        vbuf[DEC_SEQ:, :] = cv_ref[0, 0, 0].astype(BF16)

    lam = _diff_lambda(lq1, lk1, lq2, lk2, lam_init)
    q = _rope(q_ref[...], cosq_ref[...], sinq_ref[...]) * (DIFF_DK ** -0.5)
    o_ref[...] = _diff_attend(q, kbuf[...], vbuf[...], lam, g_ref[...], lam_init)


ATTN_TQ = 256


def _attn_sample(proj, cache_k, cache_v, cos, sin_signed, lq1, lk1, lq2, lk2, subln_g, lam_init, e):
    nq = DEC_SEQ // ATTN_TQ
    row0_q = T_PROMPT // ATTN_TQ
    row0_kv = T_PROMPT // DEC_SEQ
    small = pl.BlockSpec((1, DIFF_DK), lambda b, h, t: (0, 0))
    cache = pl.BlockSpec((1, 1, 1, PAST_LEN, HEAD_W), lambda b, h, t: (b, e, h, 0, 0))
    table_q = pl.BlockSpec((ATTN_TQ, HEAD_W), lambda b, h, t: (t, 0))
    table = pl.BlockSpec((DEC_SEQ, HEAD_W), lambda b, h, t: (0, 0))
    return pl.pallas_call(
        functools.partial(_attn_sample_kernel, lam_init=lam_init),
        grid=(DEC_BATCH, HEADS, nq),
        in_specs=[
            pl.BlockSpec((ATTN_TQ, HEAD_W), lambda b, h, t: (row0_q + b * nq + t, COL_QD + h)),
            pl.BlockSpec((DEC_SEQ, HEAD_W), lambda b, h, t: (row0_kv + b, COL_KD + h)),
            pl.BlockSpec((DEC_SEQ, HEAD_W), lambda b, h, t: (row0_kv + b, COL_VD + h)),
            cache, cache, table_q, table_q, table, table,
            small, small, small, small,
            pl.BlockSpec((1, HEAD_W), lambda b, h, t: (0, 0)),
        ],
        out_specs=pl.BlockSpec((ATTN_TQ, HEAD_W), lambda b, h, t: (b * nq + t, h)),
        out_shape=jax.ShapeDtypeStruct((T_SAMPLE, HEADS * HEAD_W), F32),
        scratch_shapes=[pltpu.VMEM((DEC_SEQ + PAST_LEN, HEAD_W), BF16),
                        pltpu.VMEM((DEC_SEQ + PAST_LEN, HEAD_W), BF16)],
        compiler_params=_cparams(("arbitrary", "arbitrary", "arbitrary")),
        name="diff_attn_sample",
    )(proj, proj, proj, cache_k, cache_v, cos, sin_signed, cos, sin_signed,
      lq1, lk1, lq2, lk2, subln_g)


def _rope_tables():
    t = np.arange(DEC_SEQ)
    row, colp = t // GRID_W, t % GRID_W
    lane = np.arange(LANES)
    pos = np.where(((lane // 32) % 2 == 0)[None, :], row[:, None], colp[:, None]).astype(np.float64)
    half = 16
    inv = (np.float32(ROPE_THETA) ** (-(np.arange(half, dtype=np.float32)) / np.float32(half))).astype(np.float32)
    ang = pos.astype(np.float32) * inv[lane % half][None, :]
    cos = np.cos(ang.astype(np.float64)).astype(np.float32)
    sin = np.sin(ang.astype(np.float64)).astype(np.float32)
    sign = np.where((lane % 32) < half, -1.0, 1.0).astype(np.float32)[None, :]
    return jnp.asarray(cos), jnp.asarray(sin * sign)


def _mixer_tail(out, x_ref, g1_ref, sc2_ref, sh2_ref, lng_ref, lnb_ref, wr_ref, br_ref,
                x1_ref, u2_ref, meta_ref, cnt_ref, cnt_acc):
    i = pl.program_id(0)

    @pl.when(i == 0)
    def _():
        cnt_acc[...] = jnp.zeros_like(cnt_acc)

    x1 = _layer_norm(ALPHA * x_ref[...] + g1_ref[0] * out, lng_ref[...], lnb_ref[...])
    x1_ref[...] = x1
    u2 = x1 * (1.0 + sc2_ref[0]) + sh2_ref[0]
    u2_ref[...] = u2

    logits = jnp.dot(u2, wr_ref[...], precision=lax.Precision.HIGHEST,
                     preferred_element_type=F32) + br_ref[...]
    lane = lax.broadcasted_iota(jnp.int32, logits.shape, 1).astype(F32)
    neg = jnp.float32(-jnp.inf)
    is_g = lane < N_GROUPS
    gl = jnp.where(is_g, logits, neg)
    gmax = jnp.max(gl, axis=-1, keepdims=True)
    gsel = jnp.min(jnp.where(gl == gmax, lane, float(LANES)), axis=-1, keepdims=True)
    p_g = 1.0 / jnp.sum(jnp.where(is_g, jnp.exp(gl - gmax), 0.0), axis=-1, keepdims=True)
    lo = ROUTER_LANE0 + gsel * EXPERTS_PER_GROUP
    el = jnp.where((lane >= lo) & (lane < lo + EXPERTS_PER_GROUP), logits, neg)
    v1 = jnp.max(el, axis=-1, keepdims=True)
    i1 = jnp.min(jnp.where(el == v1, lane, float(LANES)), axis=-1, keepdims=True)
    el2 = jnp.where(lane == i1, neg, el)
    v2 = jnp.max(el2, axis=-1, keepdims=True)
    i2 = jnp.min(jnp.where(el2 == v2, lane, float(LANES)), axis=-1, keepdims=True)
    t = jnp.exp(v2 - v1)
    w1 = p_g / (1.0 + t)
    w2 = w1 * t

    oh1 = (lane == i1).astype(F32)
    oh2 = (lane == i2).astype(F32)
    oh = oh1 + oh2
    r_i = lax.broadcasted_iota(jnp.int32, (TM, TM), 0)
    c_i = lax.broadcasted_iota(jnp.int32, (TM, TM), 1)
    before = (c_i < r_i).astype(BF16)
    earlier = _dot(before, oh.astype(BF16)) + cnt_acc[...]
    pos1 = jnp.sum(earlier * oh1, axis=-1, keepdims=True)
    pos2 = jnp.sum(earlier * oh2, axis=-1, keepdims=True)
    slot1 = (i1 - ROUTER_LANE0) * EXPERT_CAP + pos1
    slot2 = (i2 - ROUTER_LANE0) * EXPERT_CAP + pos2
    cnt_acc[...] = cnt_acc[...] + jnp.sum(oh, axis=0, keepdims=True)
    cnt_ref[...] = cnt_acc[...]
    meta_ref[...] = jnp.where(lane == 0, slot1, jnp.where(lane == 1, slot2,
                              jnp.where(lane == 2, w1, jnp.where(lane == 3, w2, 0.0))))


_TAIL_OUT_SHAPES = [
    jax.ShapeDtypeStruct((T, D), F32),
    jax.ShapeDtypeStruct((T, D), F32),
    jax.ShapeDtypeStruct((T, LANES), F32),
    jax.ShapeDtypeStruct((1, LANES), F32),
]


def _tail_out_specs():
    return [
        pl.BlockSpec((TM, D), lambda i: (i, 0)),
        pl.BlockSpec((TM, D), lambda i: (i, 0)),
        pl.BlockSpec((TM, LANES), lambda i: (i, 0)),
        pl.BlockSpec((1, LANES), lambda i: (0, 0)),
    ]


def _tail_in_specs(li):
    return [
        pl.BlockSpec((TM, D), lambda i: (i, 0)),
        _mod_spec(li, 2), _mod_spec(li, 4), _mod_spec(li, 3),
        _row_spec((1, D)), _row_spec((1, D)),
        _row_spec((D, LANES)), _row_spec((1, LANES)),
    ]


def _out_proj_kernel(r_ref, o_ref, w_ref, *rest):
    half = HEADS * HEAD_W
    out = _dot(r_ref[...].astype(BF16), w_ref[0:half, :]) + _dot(o_ref[...].astype(BF16), w_ref[half:, :])
    _mixer_tail(out, *rest)


def _out_proj_tail(r, o, w_out_bf16, x, modr, ln_g, ln_b, wr, br, li):
    half = HEADS * HEAD_W
    return pl.pallas_call(
        _out_proj_kernel,
        grid=(N_TILES,),
        in_specs=[pl.BlockSpec((TM, half), lambda i: (i, 0)),
                  pl.BlockSpec((TM, half), lambda i: (i, 0)),
                  _row_spec((2 * half, D))] + _tail_in_specs(li),
        out_specs=_tail_out_specs(),
        out_shape=_TAIL_OUT_SHAPES,
        scratch_shapes=[pltpu.VMEM((1, LANES), F32)],
        compiler_params=_cparams(("arbitrary",)),
        name="out_proj_tail",
    )(r, o, w_out_bf16, x, modr, modr, modr, ln_g, ln_b, wr, br)


def _conv_glu_kernel(x_ref, sh_ref, sc_ref, w_ref, b_ref, o_ref):
    u = x_ref[...] * (1.0 + sc_ref[0]) + sh_ref[0]
    h = _dot(u.astype(BF16), w_ref[...]) + b_ref[...]
    o_ref[...] = h[:, :D] * jax.nn.sigmoid(h[:, D:])


def _conv_glu(x, modr, w1_bf16, b1, li):
    return pl.pallas_call(
        _conv_glu_kernel,
        grid=(N_TILES,),
        in_specs=[pl.BlockSpec((TM, D), lambda i: (i, 0)), _mod_spec(li, 0), _mod_spec(li, 1),
                  _row_spec((D, 2 * D)), _row_spec((1, 2 * D))],
        out_specs=pl.BlockSpec((TM, D), lambda i: (i, 0)),
        out_shape=jax.ShapeDtypeStruct((T, D), F32),
        compiler_params=_cparams(("arbitrary",)),
        name="conv_glu",
    )(x, modr, modr, w1_bf16, b1)


HALO = 16
CONV_ROWS = 64
CONV_COLS = 128


def _conv_tail_kernel(cur_ref, prev_ref, next_ref, dw_ref, dwb_ref, cg_ref, cb_ref, w2_ref, b2_ref,
                      *rest):
    tail_args, hp, conv = rest[:-2], rest[-2], rest[-1]
    i = pl.program_id(0)
    k = (i - PROMPT_TILES) % SAMPLE_TILES_PER_SEQ
    in_sample = i >= PROMPT_TILES
    left_ok = jnp.logical_and(in_sample, k != 0)
    right_ok = jnp.logical_and(in_sample, k != SAMPLE_TILES_PER_SEQ - 1)
    hp[0:HALO, :] = jnp.where(left_ok, prev_ref[...], 0.0)
    hp[HALO:HALO + TM, :] = cur_ref[...]
    hp[HALO + TM:, :] = jnp.where(right_ok, next_ref[...], 0.0)
    base = HALO - CONV_PAD
    for cb in range(D // CONV_COLS):
        cs = slice(cb * CONV_COLS, (cb + 1) * CONV_COLS)
        for rb in range(TM // CONV_ROWS):
            acc = jnp.zeros((CONV_ROWS, CONV_COLS), F32)
            for tap in range(CONV_K):
                r0 = rb * CONV_ROWS + base + tap
                acc = acc + hp[r0:r0 + CONV_ROWS, cs] * dw_ref[tap:tap + 1, cs]
            conv[rb * CONV_ROWS:(rb + 1) * CONV_ROWS, cs] = acc
    hc = _silu(_layer_norm(conv[...] + dwb_ref[...], cg_ref[...], cb_ref[...]))
    out = _dot(hc.astype(BF16), w2_ref[...]) + b2_ref[...]
    _mixer_tail(out, *tail_args)


def _conv_tail(glu, dw, dwb, cg, cb, w2_bf16, b2, x, modr, ln_g, ln_b, wr, br, li):
    per = TM // HALO
    last = T // HALO - 1
    return pl.pallas_call(
        _conv_tail_kernel,
        grid=(N_TILES,),
        in_specs=[pl.BlockSpec((TM, D), lambda i: (i, 0)),
                  pl.BlockSpec((HALO, D), lambda i: (jnp.maximum(i * per - 1, 0), 0)),
                  pl.BlockSpec((HALO, D), lambda i: (jnp.minimum((i + 1) * per, last), 0)),
                  _row_spec((CONV_K, D)), _row_spec((1, D)), _row_spec((1, D)), _row_spec((1, D)),
                  _row_spec((D, D)), _row_spec((1, D))] + _tail_in_specs(li),
        out_specs=_tail_out_specs(),
        out_shape=_TAIL_OUT_SHAPES,
        scratch_shapes=[pltpu.VMEM((1, LANES), F32), pltpu.VMEM((TM + 2 * HALO, D), F32),
                        pltpu.VMEM((TM, D), F32)],
        compiler_params=_cparams(("arbitrary",)),
        name="conv_tail",
    )(glu, glu, glu, dw, dwb, cg, cb, w2_bf16, b2, x, modr, modr, modr, ln_g, ln_b, wr, br)


def _row_copy(src, src_row, dst, dst_row, sem):
    return pltpu.make_async_copy(src.at[pl.ds(src_row, 1)], dst.at[pl.ds(dst_row, 1)], sem)


def _dispatch_kernel(slot_ref, u_ref, xs_ref, sem):
    base = pl.program_id(0) * TM

    def issue(r, carry):
        t = base + r
        _row_copy(u_ref, t, xs_ref, slot_ref[t], sem).start()
        _row_copy(u_ref, t, xs_ref, slot_ref[T + t], sem).start()
        return carry

    lax.fori_loop(0, TM, issue, 0)
    pltpu.make_async_copy(u_ref.at[pl.ds(0, 2 * TM)], xs_ref.at[pl.ds(0, 2 * TM)], sem).wait()


def _dispatch(slots, u2):
    return pl.pallas_call(
        _dispatch_kernel,
        grid_spec=pltpu.PrefetchScalarGridSpec(
            num_scalar_prefetch=1,
            grid=(N_TILES,),
            in_specs=[pl.BlockSpec(memory_space=pl.ANY)],
            out_specs=pl.BlockSpec(memory_space=pl.ANY),
            scratch_shapes=[pltpu.SemaphoreType.DMA(())],
        ),
        out_shape=jax.ShapeDtypeStruct((N_EXPERTS * EXPERT_CAP, D), F32),
        compiler_params=_cparams(("arbitrary",)),
        name="moe_dispatch",
    )(slots, u2)


def _experts_kernel(te_ref, tb_ref, nt_ref, xs_ref, wg_ref, wu_ref, wd_ref, ys_ref):
    @pl.when(pl.program_id(0) < nt_ref[0])
    def _():
        x = xs_ref[...].astype(BF16)
        hg = _dot(x, wg_ref[0].astype(BF16))
        hu = _dot(x, wu_ref[0].astype(BF16))
        h = (_silu(hg) * hu).astype(BF16)
        ys_ref[...] = _dot(h, wd_ref[0].astype(BF16))


def _experts(tile_expert, tile_block, n_tiles, xs, w_gate, w_up, w_down):
    return pl.pallas_call(
        _experts_kernel,
        grid_spec=pltpu.PrefetchScalarGridSpec(
            num_scalar_prefetch=3,
            grid=(MOE_MAX_TILES,),
            in_specs=[
                pl.BlockSpec((MOE_TM, D), lambda i, te, tb, nt: (tb[i], 0)),
                pl.BlockSpec((1, D, D_EXPERT), lambda i, te, tb, nt: (te[i], 0, 0)),
                pl.BlockSpec((1, D, D_EXPERT), lambda i, te, tb, nt: (te[i], 0, 0)),
                pl.BlockSpec((1, D_EXPERT, D), lambda i, te, tb, nt: (te[i], 0, 0)),
            ],
            out_specs=pl.BlockSpec((MOE_TM, D), lambda i, te, tb, nt: (tb[i], 0)),
        ),
        out_shape=jax.ShapeDtypeStruct((N_EXPERTS * EXPERT_CAP, D), F32),
        compiler_params=_cparams(("arbitrary",)),
        name="moe_experts",
    )(tile_expert, tile_block, n_tiles, xs, w_gate, w_up, w_down)


def _combine_kernel(slot_ref, ys_ref, x1_ref, meta_ref, g2_ref, lng_ref, lnb_ref, o_ref, ybuf, sem):
    base = pl.program_id(0) * TM

    def issue(r, carry):
        t = base + r
        _row_copy(ys_ref, slot_ref[t], ybuf.at[0], r, sem).start()
        _row_copy(ys_ref, slot_ref[T + t], ybuf.at[1], r, sem).start()
        return carry

    lax.fori_loop(0, TM, issue, 0)
    pltpu.make_async_copy(ys_ref.at[pl.ds(0, TM)], ybuf.at[0], sem).wait()
    pltpu.make_async_copy(ys_ref.at[pl.ds(0, TM)], ybuf.at[1], sem).wait()
    meta = meta_ref[...]
    f = meta[:, 2:3] * ybuf[0] + meta[:, 3:4] * ybuf[1]
    o_ref[...] = _layer_norm(ALPHA * x1_ref[...] + g2_ref[0] * f, lng_ref[...], lnb_ref[...])


def _combine(slots, ys, x1, meta, modr, ln_g, ln_b, li):
    return pl.pallas_call(
        _combine_kernel,
        grid_spec=pltpu.PrefetchScalarGridSpec(
            num_scalar_prefetch=1,
            grid=(N_TILES,),
            in_specs=[pl.BlockSpec(memory_space=pl.ANY),
                      pl.BlockSpec((TM, D), lambda i, s: (i, 0)),
                      pl.BlockSpec((TM, LANES), lambda i, s: (i, 0)),
                      _mod_spec(li, 5), _row_spec((1, D)), _row_spec((1, D))],
            out_specs=pl.BlockSpec((TM, D), lambda i, s: (i, 0)),
            scratch_shapes=[pltpu.VMEM((2, TM, D), F32), pltpu.SemaphoreType.DMA(())],
        ),
        out_shape=jax.ShapeDtypeStruct((T, D), F32),
        compiler_params=_cparams(("arbitrary",)),
        name="moe_combine",
    )(slots, ys, x1, meta, modr, ln_g, ln_b)


def _tile_schedule(counts):
    tiles = (counts + MOE_TM - 1) // MOE_TM
    ends = jnp.cumsum(tiles)
    total = ends[-1]
    i = jnp.minimum(jnp.arange(MOE_MAX_TILES, dtype=jnp.int32), total - 1)
    e = jnp.searchsorted(ends, i, side="right").astype(jnp.int32)
    k = i - (ends[e] - tiles[e])
    return e, e * (EXPERT_CAP // MOE_TM) + k, total.reshape(1)


def _moe(x1, u2, meta, cnt, modr, w_gate, w_up, w_down, ln_g, ln_b, li):
    slots = meta[:, 0:2].astype(jnp.int32).T.reshape(2 * T)
    counts = cnt[0, ROUTER_LANE0:ROUTER_LANE0 + N_EXPERTS].astype(jnp.int32)
    te, tb, nt = _tile_schedule(counts)
    xs = _dispatch(slots, u2)
    ys = _experts(te, tb, nt, xs, w_gate, w_up, w_down)
    return _combine(slots, ys, x1, meta, modr, ln_g, ln_b, li)


def _router_slab(wg, bg, we, be):
    w = jnp.concatenate([wg, we.transpose(1, 0, 2).reshape(D, N_EXPERTS)], axis=1)
    b = jnp.concatenate([bg, be.reshape(N_EXPERTS)])
    pad = LANES - w.shape[1]
    return jnp.pad(w, ((0, 0), (0, pad))), jnp.pad(b, (0, pad)).reshape(1, LANES)


def kernel(x_prompt, x_sample, cache_diff_k, cache_diff_v, state_ret_fwd, state_ret_bwd, c, c_ctx, mod_w, mod_b, ln1_g, ln1_b, ln2_g, ln2_b, mix_w_in, mix_w_out, ret_decay_fwd, ret_decay_bwd, diff_lq1, diff_lk1, diff_lq2, diff_lk2, diff_subln_g, conv_w1, conv_b1, conv_dw, conv_dw_b, conv_ln_g, conv_ln_b, conv_w2, conv_b2, router_g_w, router_g_b, router_e_w, router_e_b, moe_w_gate, moe_w_up, moe_w_down):
    x = jnp.concatenate([x_prompt.reshape(T_PROMPT, D), x_sample.reshape(T_SAMPLE, D)], axis=0)
    cond = jnp.concatenate([c_ctx[None, :], c, jnp.zeros((MOD_ROWS - 1 - DEC_BATCH, D), F32)], axis=0)
    modr = _mod_vectors(cond, mod_w, mod_b).reshape(DEPTH * MOD_ROWS * 6, 1, D)
    cos, sin_signed = _rope_tables()

    def row(v):
        return v.reshape(1, -1)

    caches = None
    for li in range(DEPTH):
        wr, br = _router_slab(router_g_w[li], router_g_b[li], router_e_w[li], router_e_b[li])
        if li % 2 == 0:
            e = li // 2
            lam_init = 0.8 - 0.6 * math.exp(-0.3 * li)
            proj = _in_proj(x, modr, mix_w_in[e].astype(BF16), li)
            dec = jnp.concatenate([ret_decay_fwd[e], ret_decay_bwd[e]])
            r_p, sf, sb = _retention(proj, dec, BATCH, SEQ, 0, emit_state=True)
            (r_s,) = _retention(proj, dec, DEC_BATCH, DEC_SEQ, T_PROMPT // DEC_SEQ,
                                s0f=state_ret_fwd, s0b=state_ret_bwd, e=e)
            lams = (row(diff_lq1[e]), row(diff_lk1[e]), row(diff_lq2[e]), row(diff_lk2[e]),
                    row(diff_subln_g[e]))
            o_p = _attn_prompt(proj, *lams, lam_init)
            o_s = _attn_sample(proj, cache_diff_k, cache_diff_v, cos, sin_signed, *lams, lam_init, e)
            r = jnp.concatenate([r_p, r_s], axis=0)
            o = jnp.concatenate([o_p, o_s], axis=0)
            x1, u2, meta, cnt = _out_proj_tail(r, o, mix_w_out[e].astype(BF16), x, modr,
                                               row(ln1_g[li]), row(ln1_b[li]), wr, br, li)
            W = HEADS * HEAD_W
            pk = proj[:T_PROMPT, COL_KD * HEAD_W:COL_KD * HEAD_W + W]
            pv = proj[:T_PROMPT, COL_VD * HEAD_W:COL_VD * HEAD_W + W]
            to_cache = lambda a: a.reshape(BATCH, SEQ, HEADS, HEAD_W).transpose(0, 2, 1, 3)
            caches = (to_cache(pk), to_cache(pv), sf, sb)
        else:
            o = li // 2
            glu = _conv_glu(x, modr, conv_w1[o].astype(BF16), row(conv_b1[o]), li)
            x1, u2, meta, cnt = _conv_tail(glu, conv_dw[o], row(conv_dw_b[o]), row(conv_ln_g[o]),
                                           row(conv_ln_b[o]), conv_w2[o].astype(BF16), row(conv_b2[o]),
                                           x, modr, row(ln1_g[li]), row(ln1_b[li]), wr, br, li)
        x = _moe(x1, u2, meta, cnt, modr, moe_w_gate[li], moe_w_up[li], moe_w_down[li],
                 row(ln2_g[li]), row(ln2_b[li]), li)

    y_prompt = x[:T_PROMPT].reshape(BATCH, SEQ, D)
    y_sample = x[T_PROMPT:].reshape(DEC_BATCH, DEC_SEQ, D)
    ck, cv, sf, sb = caches
    return (y_prompt, y_sample, ck[:, None], cv[:, None], sf, sb)
```

```python
import functools
import math

import numpy as np
import jax
import jax.numpy as jnp
from jax import lax
from jax.experimental import pallas as pl
from jax.experimental.pallas import tpu as pltpu

F32 = jnp.float32
BF16 = jnp.bfloat16

D = 1024
BATCH = 16
SEQ = 256
DEPTH = 2
DEC_BATCH = 2
DEC_SEQ = 2048
PAST_LEN = 512
GRID_W = 64
HEADS = 4
HEAD_W = 128
RET_CHUNK = 128
DIFF_DK = 64
ROPE_THETA = 10000.0
IN_W = 7 * HEADS * HEAD_W
CONV_K = 31
CONV_PAD = CONV_K // 2
N_GROUPS = 4
EXPERTS_PER_GROUP = 8
N_EXPERTS = N_GROUPS * EXPERTS_PER_GROUP
D_EXPERT = 512
ALPHA = (2.0 * DEPTH) ** 0.25
LN_EPS = 1e-5
GN_EPS = 1e-6

T_PROMPT = BATCH * SEQ
T_SAMPLE = DEC_BATCH * DEC_SEQ
T = T_PROMPT + T_SAMPLE
TM = 256
N_TILES = T // TM
PROMPT_TILES = T_PROMPT // TM
SAMPLE_TILES_PER_SEQ = DEC_SEQ // TM
MOD_ROWS = 8
MOE_TM = 256
EXPERT_CAP = T
MOE_MAX_TILES = (2 * T) // MOE_TM + N_EXPERTS
LANES = 128
ROUTER_LANE0 = N_GROUPS
VMEM_LIMIT = 52 * 1024 * 1024


def _cparams(sem):
    return pltpu.CompilerParams(dimension_semantics=sem, vmem_limit_bytes=VMEM_LIMIT)


def _tile_cond_row(i):
    return jnp.where(i < PROMPT_TILES, 0, 1 + (i - PROMPT_TILES) // SAMPLE_TILES_PER_SEQ)


def _mod_spec(li, k):
    return pl.BlockSpec((1, 1, D), lambda i, *_: ((li * MOD_ROWS + _tile_cond_row(i)) * 6 + k, 0, 0))


def _row_spec(shape):
    return pl.BlockSpec(shape, lambda i, *_: (0,) * len(shape))


def _layer_norm(x, g, b):
    mu = jnp.mean(x, axis=-1, keepdims=True)
    xc = x - mu
    var = jnp.mean(xc * xc, axis=-1, keepdims=True)
    return xc * lax.rsqrt(var + LN_EPS) * g + b


def _silu(x):
    return x * jax.nn.sigmoid(x)


def _dot(a, b):
    return jnp.dot(a, b, preferred_element_type=F32)


def _dot_nt(a, b):
    return lax.dot_general(a, b, (((1,), (1,)), ((), ())), preferred_element_type=F32)


def _dot_tn(a, b):
    return lax.dot_general(a, b, (((0,), (0,)), ((), ())), preferred_element_type=F32)


MOD_TN = 512


def _mod_kernel(cond_ref, w_ref, b_ref, o_ref):
    s = _silu(cond_ref[...])
    o_ref[0] = jnp.dot(s, w_ref[0], precision=lax.Precision.HIGHEST,
                       preferred_element_type=F32) + b_ref[0]


def _mod_vectors(cond, mod_w, mod_b):
    return pl.pallas_call(
        _mod_kernel,
        grid=(DEPTH, 6 * D // MOD_TN),
        in_specs=[
            pl.BlockSpec((MOD_ROWS, D), lambda l, j: (0, 0)),
            pl.BlockSpec((1, D, MOD_TN), lambda l, j: (l, 0, j)),
            pl.BlockSpec((1, 1, MOD_TN), lambda l, j: (l, 0, j)),
        ],
        out_specs=pl.BlockSpec((1, MOD_ROWS, MOD_TN), lambda l, j: (l, 0, j)),
        out_shape=jax.ShapeDtypeStruct((DEPTH, MOD_ROWS, 6 * D), F32),
        compiler_params=_cparams(("arbitrary", "arbitrary")),
        name="mod_vectors",
    )(cond, mod_w, mod_b.reshape(DEPTH, 1, 6 * D))


def _in_proj_kernel(x_ref, sh_ref, sc_ref, w_ref, o_ref):
    u = x_ref[...] * (1.0 + sc_ref[0]) + sh_ref[0]
    o_ref[...] = _dot(u.astype(BF16), w_ref[...])


def _in_proj(x, modr, w_in_bf16, li):
    return pl.pallas_call(
        _in_proj_kernel,
        grid=(N_TILES,),
        in_specs=[
            pl.BlockSpec((TM, D), lambda i: (i, 0)),
            _mod_spec(li, 0),
            _mod_spec(li, 1),
            _row_spec((D, IN_W)),
        ],
        out_specs=pl.BlockSpec((TM, IN_W), lambda i: (i, 0)),
        out_shape=jax.ShapeDtypeStruct((T, IN_W), F32),
        compiler_params=_cparams(("arbitrary",)),
        name="in_proj",
    )(x, modr, modr, w_in_bf16)


COL_QR, COL_KR, COL_VR, COL_GR, COL_QD, COL_KD, COL_VD = (k * HEADS for k in range(7))


def _retention_kernel(dec_ref, q_ref, k_ref, v_ref, g_ref, *rest, n_chunks, has_state, emit_state):
    rest = list(rest)
    if has_state:
        s0f_ref, s0b_ref = rest[:2]
        rest = rest[2:]
    r_ref = rest[0]
    rest = rest[1:]
    if emit_state:
        sf_ref, sb_ref = rest[:2]
        rest = rest[2:]
    of_ref = rest[0]

    h = pl.program_id(1)
    C = RET_CHUNK
    ii = lax.broadcasted_iota(jnp.int32, (C, C), 0)
    jj = lax.broadcasted_iota(jnp.int32, (C, C), 1)
    rel = (ii - jj).astype(F32)
    idx = lax.broadcasted_iota(jnp.int32, (C, 1), 0).astype(F32)
    k_scale = HEAD_W ** -0.5

    def chunk(ref, c):
        return ref[c * C:(c + 1) * C, :]

    def run(direction):
        lg = -jnp.exp(jnp.full((1, 1), dec_ref[direction * HEADS + h], F32))
        if direction == 0:
            inner = jnp.where(rel >= 0, jnp.exp(jnp.maximum(rel, 0.0) * lg), 0.0)
            q_decay = jnp.exp((idx + 1.0) * lg)
            k_decay = jnp.exp((C - 1.0 - idx) * lg)
            order = range(n_chunks)
        else:
            inner = jnp.where(rel <= 0, jnp.exp(jnp.maximum(-rel, 0.0) * lg), 0.0)
            q_decay = jnp.exp((C - idx) * lg)
            k_decay = jnp.exp(idx * lg)
            order = range(n_chunks - 1, -1, -1)
        chunk_decay = jnp.exp(C * lg)
        if has_state:
            s = (s0f_ref if direction == 0 else s0b_ref)[0, 0, 0]
        else:
            s = jnp.zeros((HEAD_W, HEAD_W), F32)
        for c in order:
            qc = chunk(q_ref, c)
            kc = chunk(k_ref, c) * k_scale
            vc = chunk(v_ref, c).astype(BF16)
            scores = _dot_nt(qc.astype(BF16), kc.astype(BF16)) * inner
            o = _dot(scores.astype(BF16), vc) + _dot((qc * q_decay).astype(BF16), s.astype(BF16))
            s = s * chunk_decay + _dot_tn((kc * k_decay).astype(BF16), vc)
            if direction == 0:
                of_ref[c * C:(c + 1) * C, :] = o
            else:
                r = of_ref[c * C:(c + 1) * C, :] + o
                mu = jnp.mean(r, axis=-1, keepdims=True)
                rc = r - mu
                var = jnp.mean(rc * rc, axis=-1, keepdims=True)
                rn = rc * lax.rsqrt(var + GN_EPS)
                r_ref[c * C:(c + 1) * C, :] = _silu(chunk(g_ref, c)) * rn
        return s

    sf = run(0)
    sb = run(1)
    if emit_state:
        sf_ref[0, 0, 0] = sf
        sb_ref[0, 0, 0] = sb


def _retention(proj, dec, n_seq, seq_len, row_block0, s0f=None, s0b=None, e=0, emit_state=False):
    has_state = s0f is not None

    def col(base):
        return pl.BlockSpec((seq_len, HEAD_W), lambda b, h, *_: (row_block0 + b, base + h))

    state_spec = pl.BlockSpec((1, 1, 1, HEAD_W, HEAD_W), lambda b, h, *_: (b, e, h, 0, 0))
    in_specs = [pl.BlockSpec(memory_space=pltpu.SMEM), col(COL_QR), col(COL_KR), col(COL_VR), col(COL_GR)]
    args = [dec, proj, proj, proj, proj]
    if has_state:
        in_specs += [state_spec, state_spec]
        args += [s0f, s0b]
    out_specs = [pl.BlockSpec((seq_len, HEAD_W), lambda b, h, *_: (b, h))]
    out_shape = [jax.ShapeDtypeStruct((n_seq * seq_len, HEADS * HEAD_W), F32)]
    if emit_state:
        st = pl.BlockSpec((1, 1, 1, HEAD_W, HEAD_W), lambda b, h, *_: (b, 0, h, 0, 0))
        out_specs += [st, st]
        out_shape += [jax.ShapeDtypeStruct((n_seq, 1, HEADS, HEAD_W, HEAD_W), F32)] * 2
    return pl.pallas_call(
        functools.partial(_retention_kernel, n_chunks=seq_len // RET_CHUNK,
                          has_state=has_state, emit_state=emit_state),
        grid=(n_seq, HEADS),
        in_specs=in_specs,
        out_specs=out_specs,
        out_shape=out_shape,
        scratch_shapes=[pltpu.VMEM((seq_len, HEAD_W), F32)],
        compiler_params=_cparams(("arbitrary", "arbitrary")),
        name=f"retention_{seq_len}",
    )(*args)


def _diff_lambda(lq1_ref, lk1_ref, lq2_ref, lk2_ref, lam_init):
    a = jnp.sum(lq1_ref[...] * lk1_ref[...], axis=-1, keepdims=True)
    b = jnp.sum(lq2_ref[...] * lk2_ref[...], axis=-1, keepdims=True)
    return jnp.exp(a) - jnp.exp(b) + lam_init


def _diff_attend(q, k, v, lam, subln_g, lam_init):
    lane = lax.broadcasted_iota(jnp.int32, q.shape, 1)
    q1 = jnp.where(lane < DIFF_DK, q, 0.0).astype(BF16)
    q2 = jnp.where(lane >= DIFF_DK, q, 0.0).astype(BF16)

    def softmax(qz):
        s = _dot_nt(qz, k)
        p = jnp.exp(s - jnp.max(s, axis=-1, keepdims=True))
        return p * (1.0 / jnp.sum(p, axis=-1, keepdims=True))

    w = softmax(q1) - lam * softmax(q2)
    o = _dot(w.astype(BF16), v)
    o = o * lax.rsqrt(jnp.mean(o * o, axis=-1, keepdims=True) + LN_EPS)
    return o * subln_g * (1.0 - lam_init)


def _attn_prompt_kernel(q_ref, k_ref, v_ref, lq1, lk1, lq2, lk2, g_ref, o_ref, *, lam_init):
    lam = _diff_lambda(lq1, lk1, lq2, lk2, lam_init)
    scale = DIFF_DK ** -0.5
    for h in range(HEADS):
        sl = slice(h * HEAD_W, (h + 1) * HEAD_W)
        o_ref[:, sl] = _diff_attend(q_ref[:, sl] * scale, k_ref[:, sl].astype(BF16),
                                    v_ref[:, sl].astype(BF16), lam, g_ref[...], lam_init)


def _attn_prompt(proj, lq1, lk1, lq2, lk2, subln_g, lam_init):
    W = HEADS * HEAD_W

    def slab(base):
        return pl.BlockSpec((SEQ, W), lambda b: (b, base // HEADS))

    small = _row_spec((1, DIFF_DK))
    return pl.pallas_call(
        functools.partial(_attn_prompt_kernel, lam_init=lam_init),
        grid=(BATCH,),
        in_specs=[slab(COL_QD), slab(COL_KD), slab(COL_VD), small, small, small, small,
                  _row_spec((1, HEAD_W))],
        out_specs=pl.BlockSpec((SEQ, W), lambda b: (b, 0)),
        out_shape=jax.ShapeDtypeStruct((T_PROMPT, W), F32),
        compiler_params=_cparams(("arbitrary",)),
        name="diff_attn_prompt",
    )(proj, proj, proj, lq1, lk1, lq2, lk2, subln_g)


def _rope(x, cos, sin_signed):
    lane = lax.broadcasted_iota(jnp.int32, x.shape, 1)
    partner = jnp.where((lane % 32) < 16, pltpu.roll(x, LANES - 16, 1), pltpu.roll(x, 16, 1))
    return x * cos + partner * sin_signed


def _attn_sample_kernel(q_ref, k_ref, v_ref, ck_ref, cv_ref, cosq_ref, sinq_ref, cos_ref, sin_ref,
                        lq1, lk1, lq2, lk2, g_ref, o_ref, kbuf, vbuf, *, lam_init):
    @pl.when(pl.program_id(2) == 0)
    def _():
        kbuf[0:DEC_SEQ, :] = _rope(k_ref[...], cos_ref[...], sin_ref[...]).astype(BF16)
        kbuf[DEC_SEQ:, :] = ck_ref[0, 0, 0].astype(BF16)
        vbuf[0:DEC_SEQ, :] = v_ref[...].astype(BF16)
        vbuf[DEC_SEQ:, :] = cv_ref[0, 0, 0].astype(BF16)

    lam = _diff_lambda(lq1, lk1, lq2, lk2, lam_init)
    q = _rope(q_ref[...], cosq_ref[...], sinq_ref[...]) * (DIFF_DK ** -0.5)
    o_ref[...] = _diff_attend(q, kbuf[...], vbuf[...], lam, g_ref[...], lam_init)


ATTN_TQ = 256


def _attn_sample(proj, cache_k, cache_v, cos, sin_signed, lq1, lk1, lq2, lk2, subln_g, lam_init, e):
    nq = DEC_SEQ // ATTN_TQ
    row0_q = T_PROMPT // ATTN_TQ
    row0_kv = T_PROMPT // DEC_SEQ
    small = pl.BlockSpec((1, DIFF_DK), lambda b, h, t: (0, 0))
    cache = pl.BlockSpec((1, 1, 1, PAST_LEN, HEAD_W), lambda b, h, t: (b, e, h, 0, 0))
    table_q = pl.BlockSpec((ATTN_TQ, HEAD_W), lambda b, h, t: (t, 0))
    table = pl.BlockSpec((DEC_SEQ, HEAD_W), lambda b, h, t: (0, 0))
    return pl.pallas_call(
        functools.partial(_attn_sample_kernel, lam_init=lam_init),
        grid=(DEC_BATCH, HEADS, nq),
        in_specs=[
            pl.BlockSpec((ATTN_TQ, HEAD_W), lambda b, h, t: (row0_q + b * nq + t, COL_QD + h)),
            pl.BlockSpec((DEC_SEQ, HEAD_W), lambda b, h, t: (row0_kv + b, COL_KD + h)),
            pl.BlockSpec((DEC_SEQ, HEAD_W), lambda b, h, t: (row0_kv + b, COL_VD + h)),
            cache, cache, table_q, table_q, table, table,
            small, small, small, small,
            pl.BlockSpec((1, HEAD_W), lambda b, h, t: (0, 0)),
        ],
        out_specs=pl.BlockSpec((ATTN_TQ, HEAD_W), lambda b, h, t: (b * nq + t, h)),
        out_shape=jax.ShapeDtypeStruct((T_SAMPLE, HEADS * HEAD_W), F32),
        scratch_shapes=[pltpu.VMEM((DEC_SEQ + PAST_LEN, HEAD_W), BF16),
                        pltpu.VMEM((DEC_SEQ + PAST_LEN, HEAD_W), BF16)],
        compiler_params=_cparams(("arbitrary", "arbitrary", "arbitrary")),
        name="diff_attn_sample",
    )(proj, proj, proj, cache_k, cache_v, cos, sin_signed, cos, sin_signed,
      lq1, lk1, lq2, lk2, subln_g)


def _rope_tables():
    t = np.arange(DEC_SEQ)
    row, colp = t // GRID_W, t % GRID_W
    lane = np.arange(LANES)
    pos = np.where(((lane // 32) % 2 == 0)[None, :], row[:, None], colp[:, None]).astype(np.float64)
    half = 16
    inv = (np.float32(ROPE_THETA) ** (-(np.arange(half, dtype=np.float32)) / np.float32(half))).astype(np.float32)
    ang = pos.astype(np.float32) * inv[lane % half][None, :]
    cos = np.cos(ang.astype(np.float64)).astype(np.float32)
    sin = np.sin(ang.astype(np.float64)).astype(np.float32)
    sign = np.where((lane % 32) < half, -1.0, 1.0).astype(np.float32)[None, :]
    return jnp.asarray(cos), jnp.asarray(sin * sign)


def _mixer_tail(out, x_ref, g1_ref, sc2_ref, sh2_ref, lng_ref, lnb_ref, wr_ref, br_ref,
                x1_ref, u2_ref, meta_ref, cnt_ref, cnt_acc):
    i = pl.program_id(0)

    @pl.when(i == 0)
    def _():
        cnt_acc[...] = jnp.zeros_like(cnt_acc)

    x1 = _layer_norm(ALPHA * x_ref[...] + g1_ref[0] * out, lng_ref[...], lnb_ref[...])
    x1_ref[...] = x1
    u2 = x1 * (1.0 + sc2_ref[0]) + sh2_ref[0]
    u2_ref[...] = u2

    logits = jnp.dot(u2, wr_ref[...], precision=lax.Precision.HIGHEST,
                     preferred_element_type=F32) + br_ref[...]
    lane = lax.broadcasted_iota(jnp.int32, logits.shape, 1).astype(F32)
    neg = jnp.float32(-jnp.inf)
    is_g = lane < N_GROUPS
    gl = jnp.where(is_g, logits, neg)
    gmax = jnp.max(gl, axis=-1, keepdims=True)
    gsel = jnp.min(jnp.where(gl == gmax, lane, float(LANES)), axis=-1, keepdims=True)
    p_g = 1.0 / jnp.sum(jnp.where(is_g, jnp.exp(gl - gmax), 0.0), axis=-1, keepdims=True)
    lo = ROUTER_LANE0 + gsel * EXPERTS_PER_GROUP
    el = jnp.where((lane >= lo) & (lane < lo + EXPERTS_PER_GROUP), logits, neg)
    v1 = jnp.max(el, axis=-1, keepdims=True)
    i1 = jnp.min(jnp.where(el == v1, lane, float(LANES)), axis=-1, keepdims=True)
    el2 = jnp.where(lane == i1, neg, el)
    v2 = jnp.max(el2, axis=-1, keepdims=True)
    i2 = jnp.min(jnp.where(el2 == v2, lane, float(LANES)), axis=-1, keepdims=True)
    t = jnp.exp(v2 - v1)
    w1 = p_g / (1.0 + t)
    w2 = w1 * t

    oh1 = (lane == i1).astype(F32)
    oh2 = (lane == i2).astype(F32)
    oh = oh1 + oh2
    r_i = lax.broadcasted_iota(jnp.int32, (TM, TM), 0)
    c_i = lax.broadcasted_iota(jnp.int32, (TM, TM), 1)
    before = (c_i < r_i).astype(BF16)
    earlier = _dot(before, oh.astype(BF16)) + cnt_acc[...]
    pos1 = jnp.sum(earlier * oh1, axis=-1, keepdims=True)
    pos2 = jnp.sum(earlier * oh2, axis=-1, keepdims=True)
    slot1 = (i1 - ROUTER_LANE0) * EXPERT_CAP + pos1
    slot2 = (i2 - ROUTER_LANE0) * EXPERT_CAP + pos2
    cnt_acc[...] = cnt_acc[...] + jnp.sum(oh, axis=0, keepdims=True)
    cnt_ref[...] = cnt_acc[...]
    meta_ref[...] = jnp.where(lane == 0, slot1, jnp.where(lane == 1, slot2,
                              jnp.where(lane == 2, w1, jnp.where(lane == 3, w2, 0.0))))


_TAIL_OUT_SHAPES = [
    jax.ShapeDtypeStruct((T, D), F32),
    jax.ShapeDtypeStruct((T, D), F32),
    jax.ShapeDtypeStruct((T, LANES), F32),
    jax.ShapeDtypeStruct((1, LANES), F32),
]


def _tail_out_specs():
    return [
        pl.BlockSpec((TM, D), lambda i: (i, 0)),
        pl.BlockSpec((TM, D), lambda i: (i, 0)),
        pl.BlockSpec((TM, LANES), lambda i: (i, 0)),
        pl.BlockSpec((1, LANES), lambda i: (0, 0)),
    ]


def _tail_in_specs(li):
    return [
        pl.BlockSpec((TM, D), lambda i: (i, 0)),
        _mod_spec(li, 2), _mod_spec(li, 4), _mod_spec(li, 3),
        _row_spec((1, D)), _row_spec((1, D)),
        _row_spec((D, LANES)), _row_spec((1, LANES)),
    ]


def _out_proj_kernel(r_ref, o_ref, w_ref, *rest):
    half = HEADS * HEAD_W
    out = _dot(r_ref[...].astype(BF16), w_ref[0:half, :]) + _dot(o_ref[...].astype(BF16), w_ref[half:, :])
    _mixer_tail(out, *rest)


def _out_proj_tail(r, o, w_out_bf16, x, modr, ln_g, ln_b, wr, br, li):
    half = HEADS * HEAD_W
    return pl.pallas_call(
        _out_proj_kernel,
        grid=(N_TILES,),
        in_specs=[pl.BlockSpec((TM, half), lambda i: (i, 0)),
                  pl.BlockSpec((TM, half), lambda i: (i, 0)),
                  _row_spec((2 * half, D))] + _tail_in_specs(li),
        out_specs=_tail_out_specs(),
        out_shape=_TAIL_OUT_SHAPES,
        scratch_shapes=[pltpu.VMEM((1, LANES), F32)],
        compiler_params=_cparams(("arbitrary",)),
        name="out_proj_tail",
    )(r, o, w_out_bf16, x, modr, modr, modr, ln_g, ln_b, wr, br)


def _conv_glu_kernel(x_ref, sh_ref, sc_ref, w_ref, b_ref, o_ref):
    u = x_ref[...] * (1.0 + sc_ref[0]) + sh_ref[0]
    h = _dot(u.astype(BF16), w_ref[...]) + b_ref[...]
    o_ref[...] = h[:, :D] * jax.nn.sigmoid(h[:, D:])


def _conv_glu(x, modr, w1_bf16, b1, li):
    return pl.pallas_call(
        _conv_glu_kernel,
        grid=(N_TILES,),
        in_specs=[pl.BlockSpec((TM, D), lambda i: (i, 0)), _mod_spec(li, 0), _mod_spec(li, 1),
                  _row_spec((D, 2 * D)), _row_spec((1, 2 * D))],
        out_specs=pl.BlockSpec((TM, D), lambda i: (i, 0)),
        out_shape=jax.ShapeDtypeStruct((T, D), F32),
        compiler_params=_cparams(("arbitrary",)),
        name="conv_glu",
    )(x, modr, modr, w1_bf16, b1)


HALO = 16
CONV_ROWS = 64
CONV_COLS = 128


def _conv_tail_kernel(cur_ref, prev_ref, next_ref, dw_ref, dwb_ref, cg_ref, cb_ref, w2_ref, b2_ref,
                      *rest):
    tail_args, hp, conv = rest[:-2], rest[-2], rest[-1]
    i = pl.program_id(0)
    k = (i - PROMPT_TILES) % SAMPLE_TILES_PER_SEQ
    in_sample = i >= PROMPT_TILES
    left_ok = jnp.logical_and(in_sample, k != 0)
    right_ok = jnp.logical_and(in_sample, k != SAMPLE_TILES_PER_SEQ - 1)
    hp[0:HALO, :] = jnp.where(left_ok, prev_ref[...], 0.0)
    hp[HALO:HALO + TM, :] = cur_ref[...]
    hp[HALO + TM:, :] = jnp.where(right_ok, next_ref[...], 0.0)
    base = HALO - CONV_PAD
    for cb in range(D // CONV_COLS):
        cs = slice(cb * CONV_COLS, (cb + 1) * CONV_COLS)
        for rb in range(TM // CONV_ROWS):
            acc = jnp.zeros((CONV_ROWS, CONV_COLS), F32)
            for tap in range(CONV_K):
                r0 = rb * CONV_ROWS + base + tap
                acc = acc + hp[r0:r0 + CONV_ROWS, cs] * dw_ref[tap:tap + 1, cs]
            conv[rb * CONV_ROWS:(rb + 1) * CONV_ROWS, cs] = acc
    hc = _silu(_layer_norm(conv[...] + dwb_ref[...], cg_ref[...], cb_ref[...]))
    out = _dot(hc.astype(BF16), w2_ref[...]) + b2_ref[...]
    _mixer_tail(out, *tail_args)


def _conv_tail(glu, dw, dwb, cg, cb, w2_bf16, b2, x, modr, ln_g, ln_b, wr, br, li):
    per = TM // HALO
    last = T // HALO - 1
    return pl.pallas_call(
        _conv_tail_kernel,
        grid=(N_TILES,),
        in_specs=[pl.BlockSpec((TM, D), lambda i: (i, 0)),
                  pl.BlockSpec((HALO, D), lambda i: (jnp.maximum(i * per - 1, 0), 0)),
                  pl.BlockSpec((HALO, D), lambda i: (jnp.minimum((i + 1) * per, last), 0)),
                  _row_spec((CONV_K, D)), _row_spec((1, D)), _row_spec((1, D)), _row_spec((1, D)),
                  _row_spec((D, D)), _row_spec((1, D))] + _tail_in_specs(li),
        out_specs=_tail_out_specs(),
        out_shape=_TAIL_OUT_SHAPES,
        scratch_shapes=[pltpu.VMEM((1, LANES), F32), pltpu.VMEM((TM + 2 * HALO, D), F32),
                        pltpu.VMEM((TM, D), F32)],
        compiler_params=_cparams(("arbitrary",)),
        name="conv_tail",
    )(glu, glu, glu, dw, dwb, cg, cb, w2_bf16, b2, x, modr, modr, modr, ln_g, ln_b, wr, br)


def _row_copy(src, src_row, dst, dst_row, sem):
    return pltpu.make_async_copy(src.at[pl.ds(src_row, 1)], dst.at[pl.ds(dst_row, 1)], sem)


def _dispatch_kernel(slot_ref, u_ref, xs_ref, sem):
    base = pl.program_id(0) * TM

    def issue(r, carry):
        t = base + r
        _row_copy(u_ref, r, xs_ref, slot_ref[t], sem).start()
        _row_copy(u_ref, r, xs_ref, slot_ref[T + t], sem).start()
        return carry

    lax.fori_loop(0, TM, issue, 0)
    pltpu.make_async_copy(u_ref, xs_ref.at[pl.ds(0, TM)], sem).wait()
    pltpu.make_async_copy(u_ref, xs_ref.at[pl.ds(0, TM)], sem).wait()


def _dispatch(slots, u2):
    return pl.pallas_call(
        _dispatch_kernel,
        grid_spec=pltpu.PrefetchScalarGridSpec(
            num_scalar_prefetch=1,
            grid=(N_TILES,),
            in_specs=[pl.BlockSpec((TM, D), lambda i, s: (i, 0))],
            out_specs=pl.BlockSpec(memory_space=pl.ANY),
            scratch_shapes=[pltpu.SemaphoreType.DMA(())],
        ),
        out_shape=jax.ShapeDtypeStruct((N_EXPERTS * EXPERT_CAP, D), F32),
        compiler_params=_cparams(("arbitrary",)),
        name="moe_dispatch",
    )(slots, u2)


def _experts_kernel(te_ref, tb_ref, nt_ref, xs_ref, wg_ref, wu_ref, wd_ref, ys_ref):
    @pl.when(pl.program_id(0) < nt_ref[0])
    def _():
        x = xs_ref[...].astype(BF16)
        hg = _dot(x, wg_ref[0, 0].astype(BF16))
        hu = _dot(x, wu_ref[0, 0].astype(BF16))
        h = (_silu(hg) * hu).astype(BF16)
        ys_ref[...] = _dot(h, wd_ref[0, 0].astype(BF16))


def _experts(tile_expert, tile_block, n_tiles, xs, w_gate, w_up, w_down, li):
    return pl.pallas_call(
        _experts_kernel,
        grid_spec=pltpu.PrefetchScalarGridSpec(
            num_scalar_prefetch=3,
            grid=(MOE_MAX_TILES,),
            in_specs=[
                pl.BlockSpec((MOE_TM, D), lambda i, te, tb, nt: (tb[i], 0)),
                pl.BlockSpec((1, 1, D, D_EXPERT), lambda i, te, tb, nt: (li, te[i], 0, 0)),
                pl.BlockSpec((1, 1, D, D_EXPERT), lambda i, te, tb, nt: (li, te[i], 0, 0)),
                pl.BlockSpec((1, 1, D_EXPERT, D), lambda i, te, tb, nt: (li, te[i], 0, 0)),
            ],
            out_specs=pl.BlockSpec((MOE_TM, D), lambda i, te, tb, nt: (tb[i], 0)),
        ),
        out_shape=jax.ShapeDtypeStruct((N_EXPERTS * EXPERT_CAP, D), F32),
        compiler_params=_cparams(("arbitrary",)),
        name="moe_experts",
    )(tile_expert, tile_block, n_tiles, xs, w_gate, w_up, w_down)


def _combine_kernel(slot_ref, ys_ref, x1_ref, meta_ref, g2_ref, lng_ref, lnb_ref, o_ref, ybuf, sem):
    base = pl.program_id(0) * TM

    def issue(r, carry):
        t = base + r
        _row_copy(ys_ref, slot_ref[t], ybuf.at[0], r, sem).start()
        _row_copy(ys_ref, slot_ref[T + t], ybuf.at[1], r, sem).start()
        return carry

    lax.fori_loop(0, TM, issue, 0)
    pltpu.make_async_copy(ys_ref.at[pl.ds(0, TM)], ybuf.at[0], sem).wait()
    pltpu.make_async_copy(ys_ref.at[pl.ds(0, TM)], ybuf.at[1], sem).wait()
    meta = meta_ref[...]
    f = meta[:, 2:3] * ybuf[0] + meta[:, 3:4] * ybuf[1]
    o_ref[...] = _layer_norm(ALPHA * x1_ref[...] + g2_ref[0] * f, lng_ref[...], lnb_ref[...])


def _combine(slots, ys, x1, meta, modr, ln_g, ln_b, li):
    return pl.pallas_call(
        _combine_kernel,
        grid_spec=pltpu.PrefetchScalarGridSpec(
            num_scalar_prefetch=1,
            grid=(N_TILES,),
            in_specs=[pl.BlockSpec(memory_space=pl.ANY),
                      pl.BlockSpec((TM, D), lambda i, s: (i, 0)),
                      pl.BlockSpec((TM, LANES), lambda i, s: (i, 0)),
                      _mod_spec(li, 5), _row_spec((1, D)), _row_spec((1, D))],
            out_specs=pl.BlockSpec((TM, D), lambda i, s: (i, 0)),
            scratch_shapes=[pltpu.VMEM((2, TM, D), F32), pltpu.SemaphoreType.DMA(())],
        ),
        out_shape=jax.ShapeDtypeStruct((T, D), F32),
        compiler_params=_cparams(("arbitrary",)),
        name="moe_combine",
    )(slots, ys, x1, meta, modr, ln_g, ln_b)


def _tile_schedule(counts):
    tiles = (counts + MOE_TM - 1) // MOE_TM
    ends = jnp.cumsum(tiles)
    total = ends[-1]
    i = jnp.minimum(jnp.arange(MOE_MAX_TILES, dtype=jnp.int32), total - 1)
    e = jnp.sum((ends[None, :] <= i[:, None]).astype(jnp.int32), axis=1)
    k = i - (ends[e] - tiles[e])
    return e, e * (EXPERT_CAP // MOE_TM) + k, total.reshape(1)


def _moe(x1, u2, meta, cnt, modr, w_gate, w_up, w_down, ln_g, ln_b, li):
    slots = meta[:, 0:2].astype(jnp.int32).T.reshape(2 * T)
    counts = cnt[0, ROUTER_LANE0:ROUTER_LANE0 + N_EXPERTS].astype(jnp.int32)
    te, tb, nt = _tile_schedule(counts)
    xs = _dispatch(slots, u2)
    ys = _experts(te, tb, nt, xs, w_gate, w_up, w_down, li)
    return _combine(slots, ys, x1, meta, modr, ln_g, ln_b, li)


def _router_slab(wg, bg, we, be):
    w = jnp.concatenate([wg, we.transpose(1, 0, 2).reshape(D, N_EXPERTS)], axis=1)
    b = jnp.concatenate([bg, be.reshape(N_EXPERTS)])
    pad = LANES - w.shape[1]
    return jnp.pad(w, ((0, 0), (0, pad))), jnp.pad(b, (0, pad)).reshape(1, LANES)


def kernel(x_prompt, x_sample, cache_diff_k, cache_diff_v, state_ret_fwd, state_ret_bwd, c, c_ctx, mod_w, mod_b, ln1_g, ln1_b, ln2_g, ln2_b, mix_w_in, mix_w_out, ret_decay_fwd, ret_decay_bwd, diff_lq1, diff_lk1, diff_lq2, diff_lk2, diff_subln_g, conv_w1, conv_b1, conv_dw, conv_dw_b, conv_ln_g, conv_ln_b, conv_w2, conv_b2, router_g_w, router_g_b, router_e_w, router_e_b, moe_w_gate, moe_w_up, moe_w_down):
    x = jnp.concatenate([x_prompt.reshape(T_PROMPT, D), x_sample.reshape(T_SAMPLE, D)], axis=0)
    cond = jnp.concatenate([c_ctx[None, :], c, jnp.zeros((MOD_ROWS - 1 - DEC_BATCH, D), F32)], axis=0)
    modr = _mod_vectors(cond, mod_w, mod_b).reshape(DEPTH * MOD_ROWS * 6, 1, D)
    cos, sin_signed = _rope_tables()

    def row(v):
        return v.reshape(1, -1)

    caches = None
    for li in range(DEPTH):
        wr, br = _router_slab(router_g_w[li], router_g_b[li], router_e_w[li], router_e_b[li])
        if li % 2 == 0:
            e = li // 2
            lam_init = 0.8 - 0.6 * math.exp(-0.3 * li)
            proj = _in_proj(x, modr, mix_w_in[e].astype(BF16), li)
            dec = jnp.concatenate([ret_decay_fwd[e], ret_decay_bwd[e]])
            r_p, sf, sb = _retention(proj, dec, BATCH, SEQ, 0, emit_state=True)
            (r_s,) = _retention(proj, dec, DEC_BATCH, DEC_SEQ, T_PROMPT // DEC_SEQ,
                                s0f=state_ret_fwd, s0b=state_ret_bwd, e=e)
            lams = (row(diff_lq1[e]), row(diff_lk1[e]), row(diff_lq2[e]), row(diff_lk2[e]),
                    row(diff_subln_g[e]))
            o_p = _attn_prompt(proj, *lams, lam_init)
            o_s = _attn_sample(proj, cache_diff_k, cache_diff_v, cos, sin_signed, *lams, lam_init, e)
            r = jnp.concatenate([r_p, r_s], axis=0)
            o = jnp.concatenate([o_p, o_s], axis=0)
            x1, u2, meta, cnt = _out_proj_tail(r, o, mix_w_out[e].astype(BF16), x, modr,
                                               row(ln1_g[li]), row(ln1_b[li]), wr, br, li)
            W = HEADS * HEAD_W
            pk = proj[:T_PROMPT, COL_KD * HEAD_W:COL_KD * HEAD_W + W]
            pv = proj[:T_PROMPT, COL_VD * HEAD_W:COL_VD * HEAD_W + W]
            to_cache = lambda a: a.reshape(BATCH, SEQ, HEADS, HEAD_W).transpose(0, 2, 1, 3)
            caches = (to_cache(pk), to_cache(pv), sf, sb)
        else:
            o = li // 2
            glu = _conv_glu(x, modr, conv_w1[o].astype(BF16), row(conv_b1[o]), li)
            x1, u2, meta, cnt = _conv_tail(glu, conv_dw[o], row(conv_dw_b[o]), row(conv_ln_g[o]),
                                           row(conv_ln_b[o]), conv_w2[o].astype(BF16), row(conv_b2[o]),
                                           x, modr, row(ln1_g[li]), row(ln1_b[li]), wr, br, li)
        x = _moe(x1, u2, meta, cnt, modr, moe_w_gate, moe_w_up, moe_w_down,
                 row(ln2_g[li]), row(ln2_b[li]), li)

    y_prompt = x[:T_PROMPT].reshape(BATCH, SEQ, D)
    y_sample = x[T_PROMPT:].reshape(DEC_BATCH, DEC_SEQ, D)
    ck, cv, sf, sb = caches
    return (y_prompt, y_sample, ck[:, None], cv[:, None], sf, sb)
```

```python
import functools
import math

import numpy as np
import jax
import jax.numpy as jnp
from jax import lax
from jax.experimental import pallas as pl
from jax.experimental.pallas import tpu as pltpu

F32 = jnp.float32
BF16 = jnp.bfloat16

D = 1024
BATCH = 16
SEQ = 256
DEPTH = 2
DEC_BATCH = 2
DEC_SEQ = 2048
PAST_LEN = 512
GRID_W = 64
HEADS = 4
HEAD_W = 128
RET_CHUNK = 128
DIFF_DK = 64
ROPE_THETA = 10000.0
IN_W = 7 * HEADS * HEAD_W
CONV_K = 31
CONV_PAD = CONV_K // 2
N_GROUPS = 4
EXPERTS_PER_GROUP = 8
N_EXPERTS = N_GROUPS * EXPERTS_PER_GROUP
D_EXPERT = 512
ALPHA = (2.0 * DEPTH) ** 0.25
LN_EPS = 1e-5
GN_EPS = 1e-6

T_PROMPT = BATCH * SEQ
T_SAMPLE = DEC_BATCH * DEC_SEQ
T = T_PROMPT + T_SAMPLE
TM = 256
N_TILES = T // TM
PROMPT_TILES = T_PROMPT // TM
SAMPLE_TILES_PER_SEQ = DEC_SEQ // TM
MOD_ROWS = 8
MOE_TM = 256
EXPERT_CAP = T
MOE_MAX_TILES = (2 * T) // MOE_TM + N_EXPERTS
LANES = 128
ROUTER_LANE0 = N_GROUPS
VMEM_LIMIT = 52 * 1024 * 1024


def _cparams(sem):
    return pltpu.CompilerParams(dimension_semantics=sem, vmem_limit_bytes=VMEM_LIMIT)


def _tile_cond_row(i):
    return jnp.where(i < PROMPT_TILES, 0, 1 + (i - PROMPT_TILES) // SAMPLE_TILES_PER_SEQ)


def _mod_spec(li, k):
    return pl.BlockSpec((1, 1, D), lambda i, *_: ((li * MOD_ROWS + _tile_cond_row(i)) * 6 + k, 0, 0))


def _row_spec(shape):
    return pl.BlockSpec(shape, lambda i, *_: (0,) * len(shape))


def _layer_norm(x, g, b):
    mu = jnp.mean(x, axis=-1, keepdims=True)
    xc = x - mu
    var = jnp.mean(xc * xc, axis=-1, keepdims=True)
    return xc * lax.rsqrt(var + LN_EPS) * g + b


def _silu(x):
    return x * jax.nn.sigmoid(x)


def _dot(a, b):
    return jnp.dot(a, b, preferred_element_type=F32)


def _dot_nt(a, b):
    return lax.dot_general(a, b, (((1,), (1,)), ((), ())), preferred_element_type=F32)


def _dot_tn(a, b):
    return lax.dot_general(a, b, (((0,), (0,)), ((), ())), preferred_element_type=F32)


MOD_TN = 512


def _mod_kernel(cond_ref, w_ref, b_ref, o_ref):
    s = _silu(cond_ref[...])
    o_ref[0] = jnp.dot(s, w_ref[0], precision=lax.Precision.HIGHEST,
                       preferred_element_type=F32) + b_ref[0]


def _mod_vectors(cond, mod_w, mod_b):
    return pl.pallas_call(
        _mod_kernel,
        grid=(DEPTH, 6 * D // MOD_TN),
        in_specs=[
            pl.BlockSpec((MOD_ROWS, D), lambda l, j: (0, 0)),
            pl.BlockSpec((1, D, MOD_TN), lambda l, j: (l, 0, j)),
            pl.BlockSpec((1, 1, MOD_TN), lambda l, j: (l, 0, j)),
        ],
        out_specs=pl.BlockSpec((1, MOD_ROWS, MOD_TN), lambda l, j: (l, 0, j)),
        out_shape=jax.ShapeDtypeStruct((DEPTH, MOD_ROWS, 6 * D), F32),
        compiler_params=_cparams(("arbitrary", "arbitrary")),
        name="mod_vectors",
    )(cond, mod_w, mod_b.reshape(DEPTH, 1, 6 * D))


def _prompt_tile_spec(width):
    return pl.BlockSpec((TM, width), lambda i, *_: (jnp.minimum(i, PROMPT_TILES - 1), 0))


def _sample_tile_spec(width):
    return pl.BlockSpec((TM, width), lambda i, *_: (jnp.maximum(i - PROMPT_TILES, 0), 0))


def _pick_tile(prompt_ref, sample_ref):
    return jnp.where(pl.program_id(0) < PROMPT_TILES, prompt_ref[...], sample_ref[...])


def _in_proj_kernel(xp_ref, xs_ref, sh_ref, sc_ref, w_ref, o_ref, ck_ref, cv_ref):
    u = _pick_tile(xp_ref, xs_ref) * (1.0 + sc_ref[0]) + sh_ref[0]
    proj = _dot(u.astype(BF16), w_ref[...])
    o_ref[...] = proj

    @pl.when(pl.program_id(0) < PROMPT_TILES)
    def _():
        for h in range(HEADS):
            ck_ref[0, 0, h] = proj[:, (COL_KD + h) * HEAD_W:(COL_KD + h + 1) * HEAD_W]
            cv_ref[0, 0, h] = proj[:, (COL_VD + h) * HEAD_W:(COL_VD + h + 1) * HEAD_W]


def _in_proj(x_prompt, x_sample, modr, w_in_bf16, li):
    cache_spec = pl.BlockSpec((1, 1, HEADS, SEQ, HEAD_W),
                              lambda i: (jnp.minimum(i, PROMPT_TILES - 1), 0, 0, 0, 0))
    cache_shape = jax.ShapeDtypeStruct((BATCH, 1, HEADS, SEQ, HEAD_W), F32)
    return pl.pallas_call(
        _in_proj_kernel,
        grid=(N_TILES,),
        in_specs=[
            _prompt_tile_spec(D), _sample_tile_spec(D),
            _mod_spec(li, 0),
            _mod_spec(li, 1),
            _row_spec((D, IN_W)),
        ],
        out_specs=[pl.BlockSpec((TM, IN_W), lambda i: (i, 0)), cache_spec, cache_spec],
        out_shape=[jax.ShapeDtypeStruct((T, IN_W), F32), cache_shape, cache_shape],
        compiler_params=_cparams(("arbitrary",)),
        name="in_proj",
    )(x_prompt, x_sample, modr, modr, w_in_bf16)


COL_QR, COL_KR, COL_VR, COL_GR, COL_QD, COL_KD, COL_VD = (k * HEADS for k in range(7))


def _retention_kernel(dec_ref, q_ref, k_ref, v_ref, g_ref, *rest, n_chunks, has_state, emit_state):
    rest = list(rest)
    if has_state:
        s0f_ref, s0b_ref = rest[:2]
        rest = rest[2:]
    r_ref = rest[0]
    rest = rest[1:]
    if emit_state:
        sf_ref, sb_ref = rest[:2]
        rest = rest[2:]
    of_ref = rest[0]

    h = pl.program_id(1)
    C = RET_CHUNK
    ii = lax.broadcasted_iota(jnp.int32, (C, C), 0)
    jj = lax.broadcasted_iota(jnp.int32, (C, C), 1)
    rel = (ii - jj).astype(F32)
    idx = lax.broadcasted_iota(jnp.int32, (C, 1), 0).astype(F32)
    k_scale = HEAD_W ** -0.5

    def chunk(ref, c):
        return ref[c * C:(c + 1) * C, :]

    def run(direction):
        lg = -jnp.exp(jnp.full((1, 1), dec_ref[direction * HEADS + h], F32))
        if direction == 0:
            inner = jnp.where(rel >= 0, jnp.exp(jnp.maximum(rel, 0.0) * lg), 0.0)
            q_decay = jnp.exp((idx + 1.0) * lg)
            k_decay = jnp.exp((C - 1.0 - idx) * lg)
            order = range(n_chunks)
        else:
            inner = jnp.where(rel <= 0, jnp.exp(jnp.maximum(-rel, 0.0) * lg), 0.0)
            q_decay = jnp.exp((C - idx) * lg)
            k_decay = jnp.exp(idx * lg)
            order = range(n_chunks - 1, -1, -1)
        chunk_decay = jnp.exp(C * lg)
        if has_state:
            s = (s0f_ref if direction == 0 else s0b_ref)[0, 0, 0]
        else:
            s = jnp.zeros((HEAD_W, HEAD_W), F32)
        for c in order:
            qc = chunk(q_ref, c)
            kc = chunk(k_ref, c) * k_scale
            vc = chunk(v_ref, c).astype(BF16)
            scores = _dot_nt(qc.astype(BF16), kc.astype(BF16)) * inner
            o = _dot(scores.astype(BF16), vc) + _dot((qc * q_decay).astype(BF16), s.astype(BF16))
            s = s * chunk_decay + _dot_tn((kc * k_decay).astype(BF16), vc)
            if direction == 0:
                of_ref[c * C:(c + 1) * C, :] = o
            else:
                r = of_ref[c * C:(c + 1) * C, :] + o
                mu = jnp.mean(r, axis=-1, keepdims=True)
                rc = r - mu
                var = jnp.mean(rc * rc, axis=-1, keepdims=True)
                rn = rc * lax.rsqrt(var + GN_EPS)
                r_ref[c * C:(c + 1) * C, :] = _silu(chunk(g_ref, c)) * rn
        return s

    sf = run(0)
    sb = run(1)
    if emit_state:
        sf_ref[0, 0, 0] = sf
        sb_ref[0, 0, 0] = sb


def _retention(proj, dec, n_seq, seq_len, row_block0, s0f=None, s0b=None, e=0, emit_state=False):
    has_state = s0f is not None

    def col(base):
        return pl.BlockSpec((seq_len, HEAD_W), lambda b, h, *_: (row_block0 + b, base + h))

    state_spec = pl.BlockSpec((1, 1, 1, HEAD_W, HEAD_W), lambda b, h, *_: (b, e, h, 0, 0))
    in_specs = [pl.BlockSpec(memory_space=pltpu.SMEM), col(COL_QR), col(COL_KR), col(COL_VR), col(COL_GR)]
    args = [dec, proj, proj, proj, proj]
    if has_state:
        in_specs += [state_spec, state_spec]
        args += [s0f, s0b]
    out_specs = [pl.BlockSpec((seq_len, HEAD_W), lambda b, h, *_: (b, h))]
    out_shape = [jax.ShapeDtypeStruct((n_seq * seq_len, HEADS * HEAD_W), F32)]
    if emit_state:
        st = pl.BlockSpec((1, 1, 1, HEAD_W, HEAD_W), lambda b, h, *_: (b, 0, h, 0, 0))
        out_specs += [st, st]
        out_shape += [jax.ShapeDtypeStruct((n_seq, 1, HEADS, HEAD_W, HEAD_W), F32)] * 2
    return pl.pallas_call(
        functools.partial(_retention_kernel, n_chunks=seq_len // RET_CHUNK,
                          has_state=has_state, emit_state=emit_state),
        grid=(n_seq, HEADS),
        in_specs=in_specs,
        out_specs=out_specs,
        out_shape=out_shape,
        scratch_shapes=[pltpu.VMEM((seq_len, HEAD_W), F32)],
        compiler_params=_cparams(("arbitrary", "arbitrary")),
        name=f"retention_{seq_len}",
    )(*args)


def _diff_lambda(lq1_ref, lk1_ref, lq2_ref, lk2_ref, lam_init):
    a = jnp.sum(lq1_ref[...] * lk1_ref[...], axis=-1, keepdims=True)
    b = jnp.sum(lq2_ref[...] * lk2_ref[...], axis=-1, keepdims=True)
    return jnp.exp(a) - jnp.exp(b) + lam_init


LOG2E = 1.4426950408889634


def _diff_attend(q, k, v, lam, subln_g, lam_init):
    lane = lax.broadcasted_iota(jnp.int32, q.shape, 1)
    q1 = jnp.where(lane < DIFF_DK, q, 0.0).astype(BF16)
    q2 = jnp.where(lane >= DIFF_DK, q, 0.0).astype(BF16)

    def softmax_times_v(qz):
        s = _dot_nt(qz, k)
        p = jnp.exp2(s - jnp.max(s, axis=-1, keepdims=True))
        return _dot(p.astype(BF16), v) * (1.0 / jnp.sum(p, axis=-1, keepdims=True))

    o = softmax_times_v(q1) - lam * softmax_times_v(q2)
    o = o * lax.rsqrt(jnp.mean(o * o, axis=-1, keepdims=True) + LN_EPS)
    return o * subln_g * (1.0 - lam_init)


def _attn_prompt_kernel(q_ref, k_ref, v_ref, lq1, lk1, lq2, lk2, g_ref, o_ref, *, lam_init):
    lam = _diff_lambda(lq1, lk1, lq2, lk2, lam_init)
    scale = DIFF_DK ** -0.5 * LOG2E
    for h in range(HEADS):
        sl = slice(h * HEAD_W, (h + 1) * HEAD_W)
        o_ref[:, sl] = _diff_attend(q_ref[:, sl] * scale, k_ref[:, sl].astype(BF16),
                                    v_ref[:, sl].astype(BF16), lam, g_ref[...], lam_init)


def _attn_prompt(proj, lq1, lk1, lq2, lk2, subln_g, lam_init):
    W = HEADS * HEAD_W

    def slab(base):
        return pl.BlockSpec((SEQ, W), lambda b: (b, base // HEADS))

    small = _row_spec((1, DIFF_DK))
    return pl.pallas_call(
        functools.partial(_attn_prompt_kernel, lam_init=lam_init),
        grid=(BATCH,),
        in_specs=[slab(COL_QD), slab(COL_KD), slab(COL_VD), small, small, small, small,
                  _row_spec((1, HEAD_W))],
        out_specs=pl.BlockSpec((SEQ, W), lambda b: (b, 0)),
        out_shape=jax.ShapeDtypeStruct((T_PROMPT, W), F32),
        compiler_params=_cparams(("arbitrary",)),
        name="diff_attn_prompt",
    )(proj, proj, proj, lq1, lk1, lq2, lk2, subln_g)


def _rope(x, cos, sin_signed):
    lane = lax.broadcasted_iota(jnp.int32, x.shape, 1)
    partner = jnp.where((lane % 32) < 16, pltpu.roll(x, LANES - 16, 1), pltpu.roll(x, 16, 1))
    return x * cos + partner * sin_signed


def _attn_sample_kernel(q_ref, k_ref, v_ref, ck_ref, cv_ref, cosq_ref, sinq_ref, cos_ref, sin_ref,
                        lq1, lk1, lq2, lk2, g_ref, o_ref, kbuf, vbuf, *, lam_init):
    @pl.when(pl.program_id(2) == 0)
    def _():
        kbuf[0:DEC_SEQ, :] = _rope(k_ref[...], cos_ref[...], sin_ref[...]).astype(BF16)
        kbuf[DEC_SEQ:, :] = ck_ref[0, 0, 0].astype(BF16)
        vbuf[0:DEC_SEQ, :] = v_ref[...].astype(BF16)
        vbuf[DEC_SEQ:, :] = cv_ref[0, 0, 0].astype(BF16)

    lam = _diff_lambda(lq1, lk1, lq2, lk2, lam_init)
    q = _rope(q_ref[...], cosq_ref[...], sinq_ref[...]) * (DIFF_DK ** -0.5 * LOG2E)
    o_ref[...] = _diff_attend(q, kbuf[...], vbuf[...], lam, g_ref[...], lam_init)


ATTN_TQ = 256


def _attn_sample(proj, cache_k, cache_v, cos, sin_signed, lq1, lk1, lq2, lk2, subln_g, lam_init, e):
    nq = DEC_SEQ // ATTN_TQ
    row0_q = T_PROMPT // ATTN_TQ
    row0_kv = T_PROMPT // DEC_SEQ
    small = pl.BlockSpec((1, DIFF_DK), lambda b, h, t: (0, 0))
    cache = pl.BlockSpec((1, 1, 1, PAST_LEN, HEAD_W), lambda b, h, t: (b, e, h, 0, 0))
    table_q = pl.BlockSpec((ATTN_TQ, HEAD_W), lambda b, h, t: (t, 0))
    table = pl.BlockSpec((DEC_SEQ, HEAD_W), lambda b, h, t: (0, 0))
    return pl.pallas_call(
        functools.partial(_attn_sample_kernel, lam_init=lam_init),
        grid=(DEC_BATCH, HEADS, nq),
        in_specs=[
            pl.BlockSpec((ATTN_TQ, HEAD_W), lambda b, h, t: (row0_q + b * nq + t, COL_QD + h)),
            pl.BlockSpec((DEC_SEQ, HEAD_W), lambda b, h, t: (row0_kv + b, COL_KD + h)),
            pl.BlockSpec((DEC_SEQ, HEAD_W), lambda b, h, t: (row0_kv + b, COL_VD + h)),
            cache, cache, table_q, table_q, table, table,
            small, small, small, small,
            pl.BlockSpec((1, HEAD_W), lambda b, h, t: (0, 0)),
        ],
        out_specs=pl.BlockSpec((ATTN_TQ, HEAD_W), lambda b, h, t: (b * nq + t, h)),
        out_shape=jax.ShapeDtypeStruct((T_SAMPLE, HEADS * HEAD_W), F32),
        scratch_shapes=[pltpu.VMEM((DEC_SEQ + PAST_LEN, HEAD_W), BF16),
                        pltpu.VMEM((DEC_SEQ + PAST_LEN, HEAD_W), BF16)],
        compiler_params=_cparams(("arbitrary", "arbitrary", "arbitrary")),
        name="diff_attn_sample",
    )(proj, proj, proj, cache_k, cache_v, cos, sin_signed, cos, sin_signed,
      lq1, lk1, lq2, lk2, subln_g)


def _rope_tables():
    t = np.arange(DEC_SEQ)
    row, colp = t // GRID_W, t % GRID_W
    lane = np.arange(LANES)
    pos = np.where(((lane // 32) % 2 == 0)[None, :], row[:, None], colp[:, None]).astype(np.float64)
    half = 16
    inv = (np.float32(ROPE_THETA) ** (-(np.arange(half, dtype=np.float32)) / np.float32(half))).astype(np.float32)
    ang = pos.astype(np.float32) * inv[lane % half][None, :]
    cos = np.cos(ang.astype(np.float64)).astype(np.float32)
    sin = np.sin(ang.astype(np.float64)).astype(np.float32)
    sign = np.where((lane % 32) < half, -1.0, 1.0).astype(np.float32)[None, :]
    return jnp.asarray(cos), jnp.asarray(sin * sign)


def _split_bf16(a):
    hi = a.astype(BF16)
    return hi, (a - hi.astype(F32)).astype(BF16)


def _mixer_tail(out, x, g1_ref, sc2_ref, sh2_ref, lng_ref, lnb_ref, wr_ref, br_ref,
                x1_ref, u2_ref, meta_ref, cnt_ref, cnt_acc):
    i = pl.program_id(0)

    @pl.when(i == 0)
    def _():
        cnt_acc[...] = jnp.zeros_like(cnt_acc)

    x1 = _layer_norm(ALPHA * x + g1_ref[0] * out, lng_ref[...], lnb_ref[...])
    x1_ref[...] = x1
    u2 = x1 * (1.0 + sc2_ref[0]) + sh2_ref[0]
    u2_ref[...] = u2

    u_hi, u_lo = _split_bf16(u2)
    w_hi, w_lo = _split_bf16(wr_ref[...])
    logits = _dot(u_hi, w_hi) + (_dot(u_hi, w_lo) + _dot(u_lo, w_hi)) + br_ref[...]
    lane = lax.broadcasted_iota(jnp.int32, logits.shape, 1).astype(F32)
    neg = jnp.float32(-jnp.inf)
    is_g = lane < N_GROUPS
    gl = jnp.where(is_g, logits, neg)
    gmax = jnp.max(gl, axis=-1, keepdims=True)
    gsel = jnp.min(jnp.where(gl == gmax, lane, float(LANES)), axis=-1, keepdims=True)
    p_g = 1.0 / jnp.sum(jnp.where(is_g, jnp.exp(gl - gmax), 0.0), axis=-1, keepdims=True)
    lo = ROUTER_LANE0 + gsel * EXPERTS_PER_GROUP
    el = jnp.where((lane >= lo) & (lane < lo + EXPERTS_PER_GROUP), logits, neg)
    v1 = jnp.max(el, axis=-1, keepdims=True)
    i1 = jnp.min(jnp.where(el == v1, lane, float(LANES)), axis=-1, keepdims=True)
    el2 = jnp.where(lane == i1, neg, el)
    v2 = jnp.max(el2, axis=-1, keepdims=True)
    i2 = jnp.min(jnp.where(el2 == v2, lane, float(LANES)), axis=-1, keepdims=True)
    t = jnp.exp(v2 - v1)
    w1 = p_g / (1.0 + t)
    w2 = w1 * t

    oh1 = (lane == i1).astype(F32)
    oh2 = (lane == i2).astype(F32)
    oh = oh1 + oh2
    r_i = lax.broadcasted_iota(jnp.int32, (TM, TM), 0)
    c_i = lax.broadcasted_iota(jnp.int32, (TM, TM), 1)
    before = (c_i < r_i).astype(BF16)
    earlier = _dot(before, oh.astype(BF16)) + cnt_acc[...]
    pos1 = jnp.sum(earlier * oh1, axis=-1, keepdims=True)
    pos2 = jnp.sum(earlier * oh2, axis=-1, keepdims=True)
    slot1 = (i1 - ROUTER_LANE0) * EXPERT_CAP + pos1
    slot2 = (i2 - ROUTER_LANE0) * EXPERT_CAP + pos2
    cnt_acc[...] = cnt_acc[...] + jnp.sum(oh, axis=0, keepdims=True)
    cnt_ref[...] = cnt_acc[...]
    meta_ref[...] = jnp.where(lane == 0, slot1, jnp.where(lane == 1, slot2,
                              jnp.where(lane == 2, w1, jnp.where(lane == 3, w2, 0.0))))


_TAIL_OUT_SHAPES = [
    jax.ShapeDtypeStruct((T, D), F32),
    jax.ShapeDtypeStruct((T, D), F32),
    jax.ShapeDtypeStruct((T, LANES), F32),
    jax.ShapeDtypeStruct((1, LANES), F32),
]


def _tail_out_specs():
    return [
        pl.BlockSpec((TM, D), lambda i: (i, 0)),
        pl.BlockSpec((TM, D), lambda i: (i, 0)),
        pl.BlockSpec((TM, LANES), lambda i: (i, 0)),
        pl.BlockSpec((1, LANES), lambda i: (0, 0)),
    ]


def _tail_in_specs(li):
    return [
        _mod_spec(li, 2), _mod_spec(li, 4), _mod_spec(li, 3),
        _row_spec((1, D)), _row_spec((1, D)),
        _row_spec((D, LANES)), _row_spec((1, LANES)),
    ]


def _out_proj_kernel(rp_ref, rs_ref, op_ref, os_ref, w_ref, xp_ref, xs_ref, *rest):
    half = HEADS * HEAD_W
    r = _pick_tile(rp_ref, rs_ref).astype(BF16)
    o = _pick_tile(op_ref, os_ref).astype(BF16)
    out = _dot(r, w_ref[0:half, :]) + _dot(o, w_ref[half:, :])
    _mixer_tail(out, _pick_tile(xp_ref, xs_ref), *rest)


def _out_proj_tail(r_p, r_s, o_p, o_s, w_out_bf16, x_prompt, x_sample, modr, ln_g, ln_b, wr, br, li):
    half = HEADS * HEAD_W
    return pl.pallas_call(
        _out_proj_kernel,
        grid=(N_TILES,),
        in_specs=[_prompt_tile_spec(half), _sample_tile_spec(half),
                  _prompt_tile_spec(half), _sample_tile_spec(half),
                  _row_spec((2 * half, D)),
                  _prompt_tile_spec(D), _sample_tile_spec(D)] + _tail_in_specs(li),
        out_specs=_tail_out_specs(),
        out_shape=_TAIL_OUT_SHAPES,
        scratch_shapes=[pltpu.VMEM((1, LANES), F32)],
        compiler_params=_cparams(("arbitrary",)),
        name="out_proj_tail",
    )(r_p, r_s, o_p, o_s, w_out_bf16, x_prompt, x_sample, modr, modr, modr, ln_g, ln_b, wr, br)


def _conv_glu_kernel(x_ref, sh_ref, sc_ref, w_ref, b_ref, o_ref):
    u = x_ref[...] * (1.0 + sc_ref[0]) + sh_ref[0]
    h = _dot(u.astype(BF16), w_ref[...]) + b_ref[...]
    o_ref[...] = h[:, :D] * jax.nn.sigmoid(h[:, D:])


def _conv_glu(x, modr, w1_bf16, b1, li):
    return pl.pallas_call(
        _conv_glu_kernel,
        grid=(N_TILES,),
        in_specs=[pl.BlockSpec((TM, D), lambda i: (i, 0)), _mod_spec(li, 0), _mod_spec(li, 1),
                  _row_spec((D, 2 * D)), _row_spec((1, 2 * D))],
        out_specs=pl.BlockSpec((TM, D), lambda i: (i, 0)),
        out_shape=jax.ShapeDtypeStruct((T, D), F32),
        compiler_params=_cparams(("arbitrary",)),
        name="conv_glu",
    )(x, modr, modr, w1_bf16, b1)


HALO = 16
CONV_ROWS = 64
CONV_COLS = 128
SUBLANES = 8


def _depthwise_conv(hp, dw_ref, conv):
    base = HALO - CONV_PAD
    for cb in range(D // CONV_COLS):
        cs = slice(cb * CONV_COLS, (cb + 1) * CONV_COLS)
        for rb in range(TM // CONV_ROWS):
            r0 = rb * CONV_ROWS
            acc = None
            for shift in range(SUBLANES):
                part = None
                for tap in range(CONV_K):
                    off = base + tap
                    if off % SUBLANES != shift:
                        continue
                    a0 = r0 + off - shift
                    term = hp[a0:a0 + CONV_ROWS + SUBLANES, cs] * dw_ref[tap:tap + 1, cs]
                    part = term if part is None else part + term
                part = part[shift:shift + CONV_ROWS, :]
                acc = part if acc is None else acc + part
            conv[r0:r0 + CONV_ROWS, cs] = acc


def _conv_tail_kernel(cur_ref, prev_ref, next_ref, dw_ref, dwb_ref, cg_ref, cb_ref, w2_ref, b2_ref,
                      x_ref, *rest):
    tail_args, hp, conv = rest[:-2], rest[-2], rest[-1]
    i = pl.program_id(0)
    k = (i - PROMPT_TILES) % SAMPLE_TILES_PER_SEQ
    in_sample = i >= PROMPT_TILES
    left_ok = jnp.logical_and(in_sample, k != 0)
    right_ok = jnp.logical_and(in_sample, k != SAMPLE_TILES_PER_SEQ - 1)
    hp[0:HALO, :] = jnp.where(left_ok, prev_ref[...], 0.0)
    hp[HALO:HALO + TM, :] = cur_ref[...]
    hp[HALO + TM:HALO + TM + HALO, :] = jnp.where(right_ok, next_ref[...], 0.0)
    _depthwise_conv(hp, dw_ref, conv)
    hc = _silu(_layer_norm(conv[...] + dwb_ref[...], cg_ref[...], cb_ref[...]))
    out = _dot(hc.astype(BF16), w2_ref[...]) + b2_ref[...]
    _mixer_tail(out, x_ref[...], *tail_args)


def _conv_tail(glu, dw, dwb, cg, cb, w2_bf16, b2, x, modr, ln_g, ln_b, wr, br, li):
    per = TM // HALO
    last = T // HALO - 1
    return pl.pallas_call(
        _conv_tail_kernel,
        grid=(N_TILES,),
        in_specs=[pl.BlockSpec((TM, D), lambda i: (i, 0)),
                  pl.BlockSpec((HALO, D), lambda i: (jnp.maximum(i * per - 1, 0), 0)),
                  pl.BlockSpec((HALO, D), lambda i: (jnp.minimum((i + 1) * per, last), 0)),
                  _row_spec((CONV_K, D)), _row_spec((1, D)), _row_spec((1, D)), _row_spec((1, D)),
                  _row_spec((D, D)), _row_spec((1, D)),
                  pl.BlockSpec((TM, D), lambda i: (i, 0))] + _tail_in_specs(li),
        out_specs=_tail_out_specs(),
        out_shape=_TAIL_OUT_SHAPES,
        scratch_shapes=[pltpu.VMEM((1, LANES), F32),
                        pltpu.VMEM((TM + 2 * HALO, D), F32),
                        pltpu.VMEM((TM, D), F32)],
        compiler_params=_cparams(("arbitrary",)),
        name="conv_tail",
    )(glu, glu, glu, dw, dwb, cg, cb, w2_bf16, b2, x, modr, modr, modr, ln_g, ln_b, wr, br)


def _row_copy(src, src_row, dst, dst_row, sem):
    return pltpu.make_async_copy(src.at[pl.ds(src_row, 1)], dst.at[pl.ds(dst_row, 1)], sem)


def _dispatch_kernel(slot_ref, u_ref, xs_ref, sem):
    base = pl.program_id(0) * TM

    def issue(r, carry):
        t = base + r
        _row_copy(u_ref, r, xs_ref, slot_ref[t], sem).start()
        _row_copy(u_ref, r, xs_ref, slot_ref[T + t], sem).start()
        return carry

    lax.fori_loop(0, TM, issue, 0)
    pltpu.make_async_copy(u_ref, xs_ref.at[pl.ds(0, TM)], sem).wait()
    pltpu.make_async_copy(u_ref, xs_ref.at[pl.ds(0, TM)], sem).wait()


def _dispatch(slots, u2):
    return pl.pallas_call(
        _dispatch_kernel,
        grid_spec=pltpu.PrefetchScalarGridSpec(
            num_scalar_prefetch=1,
            grid=(N_TILES,),
            in_specs=[pl.BlockSpec((TM, D), lambda i, s: (i, 0))],
            out_specs=pl.BlockSpec(memory_space=pl.ANY),
            scratch_shapes=[pltpu.SemaphoreType.DMA(())],
        ),
        out_shape=jax.ShapeDtypeStruct((N_EXPERTS * EXPERT_CAP, D), F32),
        compiler_params=_cparams(("arbitrary",)),
        name="moe_dispatch",
    )(slots, u2)


def _experts_kernel(te_ref, tb_ref, nt_ref, xs_ref, wg_ref, wu_ref, wd_ref, ys_ref):
    @pl.when(pl.program_id(0) < nt_ref[0])
    def _():
        x = xs_ref[...].astype(BF16)
        hg = _dot(x, wg_ref[0, 0].astype(BF16))
        hu = _dot(x, wu_ref[0, 0].astype(BF16))
        h = (_silu(hg) * hu).astype(BF16)
        ys_ref[...] = _dot(h, wd_ref[0, 0].astype(BF16))


def _experts(tile_expert, tile_block, n_tiles, xs, w_gate, w_up, w_down, li):
    return pl.pallas_call(
        _experts_kernel,
        grid_spec=pltpu.PrefetchScalarGridSpec(
            num_scalar_prefetch=3,
            grid=(MOE_MAX_TILES,),
            in_specs=[
                pl.BlockSpec((MOE_TM, D), lambda i, te, tb, nt: (tb[i], 0)),
                pl.BlockSpec((1, 1, D, D_EXPERT), lambda i, te, tb, nt: (li, te[i], 0, 0)),
                pl.BlockSpec((1, 1, D, D_EXPERT), lambda i, te, tb, nt: (li, te[i], 0, 0)),
                pl.BlockSpec((1, 1, D_EXPERT, D), lambda i, te, tb, nt: (li, te[i], 0, 0)),
            ],
            out_specs=pl.BlockSpec((MOE_TM, D), lambda i, te, tb, nt: (tb[i], 0)),
        ),
        out_shape=jax.ShapeDtypeStruct((N_EXPERTS * EXPERT_CAP, D), F32),
        compiler_params=_cparams(("arbitrary",)),
        name="moe_experts",
    )(tile_expert, tile_block, n_tiles, xs, w_gate, w_up, w_down)


def _combine_kernel(slot_ref, ys_ref, x1_ref, meta_ref, g2_ref, lng_ref, lnb_ref, *rest, split):
    outs, (ybuf, sem) = rest[:-2], rest[-2:]
    i = pl.program_id(0)
    base = i * TM

    def issue(r, carry):
        t = base + r
        _row_copy(ys_ref, slot_ref[t], ybuf.at[0], r, sem).start()
        _row_copy(ys_ref, slot_ref[T + t], ybuf.at[1], r, sem).start()
        return carry

    lax.fori_loop(0, TM, issue, 0)
    pltpu.make_async_copy(ys_ref.at[pl.ds(0, TM)], ybuf.at[0], sem).wait()
    pltpu.make_async_copy(ys_ref.at[pl.ds(0, TM)], ybuf.at[1], sem).wait()
    meta = meta_ref[...]
    f = meta[:, 2:3] * ybuf[0] + meta[:, 3:4] * ybuf[1]
    y = _layer_norm(ALPHA * x1_ref[...] + g2_ref[0] * f, lng_ref[...], lnb_ref[...])
    if split:
        @pl.when(i < PROMPT_TILES)
        def _():
            outs[0][...] = y

        @pl.when(i >= PROMPT_TILES)
        def _():
            outs[1][...] = y
    else:
        outs[0][...] = y


def _combine(slots, ys, x1, meta, modr, ln_g, ln_b, li, split):
    if split:
        out_specs = [_prompt_tile_spec(D), _sample_tile_spec(D)]
        out_shape = [jax.ShapeDtypeStruct((T_PROMPT, D), F32), jax.ShapeDtypeStruct((T_SAMPLE, D), F32)]
    else:
        out_specs = [pl.BlockSpec((TM, D), lambda i, s: (i, 0))]
        out_shape = [jax.ShapeDtypeStruct((T, D), F32)]
    return pl.pallas_call(
        functools.partial(_combine_kernel, split=split),
        grid_spec=pltpu.PrefetchScalarGridSpec(
            num_scalar_prefetch=1,
            grid=(N_TILES,),
            in_specs=[pl.BlockSpec(memory_space=pl.ANY),
                      pl.BlockSpec((TM, D), lambda i, s: (i, 0)),
                      pl.BlockSpec((TM, LANES), lambda i, s: (i, 0)),
                      _mod_spec(li, 5), _row_spec((1, D)), _row_spec((1, D))],
            out_specs=out_specs,
            scratch_shapes=[pltpu.VMEM((2, TM, D), F32), pltpu.SemaphoreType.DMA(())],
        ),
        out_shape=out_shape,
        compiler_params=_cparams(("arbitrary",)),
        name="moe_combine",
    )(slots, ys, x1, meta, modr, ln_g, ln_b)


def _tile_schedule(counts):
    tiles = (counts + MOE_TM - 1) // MOE_TM
    ends = jnp.cumsum(tiles)
    total = ends[-1]
    i = jnp.minimum(jnp.arange(MOE_MAX_TILES, dtype=jnp.int32), jnp.maximum(total - 1, 0))
    e = jnp.sum((ends[None, :] <= i[:, None]).astype(jnp.int32), axis=1)
    k = i - (ends[e] - tiles[e])
    return e, e * (EXPERT_CAP // MOE_TM) + k, total.reshape(1)


def _moe(x1, u2, meta, cnt, modr, w_gate, w_up, w_down, ln_g, ln_b, li, split):
    slots = meta[:, 0:2].astype(jnp.int32).T.reshape(2 * T)
    counts = cnt[0, ROUTER_LANE0:ROUTER_LANE0 + N_EXPERTS].astype(jnp.int32)
    te, tb, nt = _tile_schedule(counts)
    xs = _dispatch(slots, u2)
    ys = _experts(te, tb, nt, xs, w_gate, w_up, w_down, li)
    return _combine(slots, ys, x1, meta, modr, ln_g, ln_b, li, split)


def _router_slab(wg, bg, we, be):
    w = jnp.concatenate([wg, we.transpose(1, 0, 2).reshape(D, N_EXPERTS)], axis=1)
    b = jnp.concatenate([bg, be.reshape(N_EXPERTS)])
    pad = LANES - w.shape[1]
    return jnp.pad(w, ((0, 0), (0, pad))), jnp.pad(b, (0, pad)).reshape(1, LANES)


def kernel(x_prompt, x_sample, cache_diff_k, cache_diff_v, state_ret_fwd, state_ret_bwd, c, c_ctx, mod_w, mod_b, ln1_g, ln1_b, ln2_g, ln2_b, mix_w_in, mix_w_out, ret_decay_fwd, ret_decay_bwd, diff_lq1, diff_lk1, diff_lq2, diff_lk2, diff_subln_g, conv_w1, conv_b1, conv_dw, conv_dw_b, conv_ln_g, conv_ln_b, conv_w2, conv_b2, router_g_w, router_g_b, router_e_w, router_e_b, moe_w_gate, moe_w_up, moe_w_down):
    xp = x_prompt.reshape(T_PROMPT, D)
    xs = x_sample.reshape(T_SAMPLE, D)
    cond = jnp.concatenate([c_ctx[None, :], c, jnp.zeros((MOD_ROWS - 1 - DEC_BATCH, D), F32)], axis=0)
    modr = _mod_vectors(cond, mod_w, mod_b).reshape(DEPTH * MOD_ROWS * 6, 1, D)
    cos, sin_signed = _rope_tables()

    def row(v):
        return v.reshape(1, -1)

    x = None
    caches = None
    for li in range(DEPTH):
        wr, br = _router_slab(router_g_w[li], router_g_b[li], router_e_w[li], router_e_b[li])
        if li % 2 == 0:
            assert li == 0, "the even mixer reads the kernel inputs directly"
            e = li // 2
            lam_init = 0.8 - 0.6 * math.exp(-0.3 * li)
            proj, ck, cv = _in_proj(xp, xs, modr, mix_w_in[e].astype(BF16), li)
            dec = jnp.concatenate([ret_decay_fwd[e], ret_decay_bwd[e]])
            r_p, sf, sb = _retention(proj, dec, BATCH, SEQ, 0, emit_state=True)
            (r_s,) = _retention(proj, dec, DEC_BATCH, DEC_SEQ, T_PROMPT // DEC_SEQ,
                                s0f=state_ret_fwd, s0b=state_ret_bwd, e=e)
            lams = (row(diff_lq1[e]), row(diff_lk1[e]), row(diff_lq2[e]), row(diff_lk2[e]),
                    row(diff_subln_g[e]))
            o_p = _attn_prompt(proj, *lams, lam_init)
            o_s = _attn_sample(proj, cache_diff_k, cache_diff_v, cos, sin_signed, *lams, lam_init, e)
            x1, u2, meta, cnt = _out_proj_tail(r_p, r_s, o_p, o_s, mix_w_out[e].astype(BF16), xp, xs, modr,
                                               row(ln1_g[li]), row(ln1_b[li]), wr, br, li)
            caches = (ck, cv, sf, sb)
        else:
            o = li // 2
            glu = _conv_glu(x, modr, conv_w1[o].astype(BF16), row(conv_b1[o]), li)
            x1, u2, meta, cnt = _conv_tail(glu, conv_dw[o], row(conv_dw_b[o]), row(conv_ln_g[o]),
                                           row(conv_ln_b[o]), conv_w2[o].astype(BF16), row(conv_b2[o]),
                                           x, modr, row(ln1_g[li]), row(ln1_b[li]), wr, br, li)
        outs = _moe(x1, u2, meta, cnt, modr, moe_w_gate, moe_w_up, moe_w_down,
                    row(ln2_g[li]), row(ln2_b[li]), li, split=(li == DEPTH - 1))
        x = outs[0]

    y_prompt = outs[0].reshape(BATCH, SEQ, D)
    y_sample = outs[1].reshape(DEC_BATCH, DEC_SEQ, D)
    return (y_prompt, y_sample) + caches
```

```python
import functools
import math

import numpy as np
import jax
import jax.numpy as jnp
from jax import lax
from jax.experimental import pallas as pl
from jax.experimental.pallas import tpu as pltpu

F32 = jnp.float32
BF16 = jnp.bfloat16

D = 1024
BATCH = 16
SEQ = 256
DEPTH = 2
DEC_BATCH = 2
DEC_SEQ = 2048
PAST_LEN = 512
GRID_W = 64
HEADS = 4
HEAD_W = 128
RET_CHUNK = 128
DIFF_DK = 64
ROPE_THETA = 10000.0
IN_W = 7 * HEADS * HEAD_W
CONV_K = 31
CONV_PAD = CONV_K // 2
N_GROUPS = 4
EXPERTS_PER_GROUP = 8
N_EXPERTS = N_GROUPS * EXPERTS_PER_GROUP
D_EXPERT = 512
ALPHA = (2.0 * DEPTH) ** 0.25
LN_EPS = 1e-5
GN_EPS = 1e-6

T_PROMPT = BATCH * SEQ
T_SAMPLE = DEC_BATCH * DEC_SEQ
T = T_PROMPT + T_SAMPLE
TM = 256
N_TILES = T // TM
PROMPT_TILES = T_PROMPT // TM
SAMPLE_TILES_PER_SEQ = DEC_SEQ // TM
MOD_ROWS = 8
MOE_TM = 256
LANES = 128
SUBLANES = 8
RUN_ALIGN = SUBLANES
SORT_ROWS = -(-(2 * TM + N_EXPERTS * (RUN_ALIGN - 1)) // TM) * TM
RUN_BITS = tuple(1 << b for b in range((2 * TM).bit_length() - 1, RUN_ALIGN.bit_length() - 2, -1))
MOE_MAX_TILES = -(-(2 * T + N_TILES * N_EXPERTS * (RUN_ALIGN - 1) + N_EXPERTS * (MOE_TM - RUN_ALIGN)) // MOE_TM)
MOE_ROWS = MOE_MAX_TILES * MOE_TM
ROUTER_LANE0 = N_GROUPS
VMEM_LIMIT = 52 * 1024 * 1024


def _cparams(sem):
    return pltpu.CompilerParams(dimension_semantics=sem, vmem_limit_bytes=VMEM_LIMIT)


def _tile_cond_row(i):
    return jnp.where(i < PROMPT_TILES, 0, 1 + (i - PROMPT_TILES) // SAMPLE_TILES_PER_SEQ)


def _mod_spec(li, k):
    return pl.BlockSpec((1, 1, D), lambda i, *_: ((li * MOD_ROWS + _tile_cond_row(i)) * 6 + k, 0, 0))


def _row_spec(shape):
    return pl.BlockSpec(shape, lambda i, *_: (0,) * len(shape))


def _layer_norm(x, g, b):
    mu = jnp.mean(x, axis=-1, keepdims=True)
    xc = x - mu
    var = jnp.mean(xc * xc, axis=-1, keepdims=True)
    return xc * lax.rsqrt(var + LN_EPS) * g + b


def _silu(x):
    return x * jax.nn.sigmoid(x)


def _dot(a, b):
    return jnp.dot(a, b, preferred_element_type=F32)


def _dot_nt(a, b):
    return lax.dot_general(a, b, (((1,), (1,)), ((), ())), preferred_element_type=F32)


def _dot_tn(a, b):
    return lax.dot_general(a, b, (((0,), (0,)), ((), ())), preferred_element_type=F32)


MOD_TN = 512


def _mod_kernel(cond_ref, w_ref, b_ref, o_ref):
    s = _silu(cond_ref[...])
    o_ref[0] = jnp.dot(s, w_ref[0], precision=lax.Precision.HIGHEST,
                       preferred_element_type=F32) + b_ref[0]


def _mod_vectors(cond, mod_w, mod_b):
    return pl.pallas_call(
        _mod_kernel,
        grid=(DEPTH, 6 * D // MOD_TN),
        in_specs=[
            pl.BlockSpec((MOD_ROWS, D), lambda l, j: (0, 0)),
            pl.BlockSpec((1, D, MOD_TN), lambda l, j: (l, 0, j)),
            pl.BlockSpec((1, 1, MOD_TN), lambda l, j: (l, 0, j)),
        ],
        out_specs=pl.BlockSpec((1, MOD_ROWS, MOD_TN), lambda l, j: (l, 0, j)),
        out_shape=jax.ShapeDtypeStruct((DEPTH, MOD_ROWS, 6 * D), F32),
        compiler_params=_cparams(("arbitrary", "arbitrary")),
        name="mod_vectors",
    )(cond, mod_w, mod_b.reshape(DEPTH, 1, 6 * D))


def _prompt_tile_spec(width):
    return pl.BlockSpec((TM, width), lambda i, *_: (jnp.minimum(i, PROMPT_TILES - 1), 0))


def _sample_tile_spec(width):
    return pl.BlockSpec((TM, width), lambda i, *_: (jnp.maximum(i - PROMPT_TILES, 0), 0))


def _pick_tile(prompt_ref, sample_ref):
    return jnp.where(pl.program_id(0) < PROMPT_TILES, prompt_ref[...], sample_ref[...])


def _in_proj_kernel(xp_ref, xs_ref, sh_ref, sc_ref, w_ref, o_ref, ck_ref, cv_ref):
    u = _pick_tile(xp_ref, xs_ref) * (1.0 + sc_ref[0]) + sh_ref[0]
    proj = _dot(u.astype(BF16), w_ref[...])
    o_ref[...] = proj

    @pl.when(pl.program_id(0) < PROMPT_TILES)
    def _():
        for h in range(HEADS):
            ck_ref[0, 0, h] = proj[:, (COL_KD + h) * HEAD_W:(COL_KD + h + 1) * HEAD_W]
            cv_ref[0, 0, h] = proj[:, (COL_VD + h) * HEAD_W:(COL_VD + h + 1) * HEAD_W]


def _in_proj(x_prompt, x_sample, modr, w_in_bf16, li):
    cache_spec = pl.BlockSpec((1, 1, HEADS, SEQ, HEAD_W),
                              lambda i: (jnp.minimum(i, PROMPT_TILES - 1), 0, 0, 0, 0))
    cache_shape = jax.ShapeDtypeStruct((BATCH, 1, HEADS, SEQ, HEAD_W), F32)
    return pl.pallas_call(
        _in_proj_kernel,
        grid=(N_TILES,),
        in_specs=[
            _prompt_tile_spec(D), _sample_tile_spec(D),
            _mod_spec(li, 0),
            _mod_spec(li, 1),
            _row_spec((D, IN_W)),
        ],
        out_specs=[pl.BlockSpec((TM, IN_W), lambda i: (i, 0)), cache_spec, cache_spec],
        out_shape=[jax.ShapeDtypeStruct((T, IN_W), F32), cache_shape, cache_shape],
        compiler_params=_cparams(("arbitrary",)),
        name="in_proj",
    )(x_prompt, x_sample, modr, modr, w_in_bf16)


COL_QR, COL_KR, COL_VR, COL_GR, COL_QD, COL_KD, COL_VD = (k * HEADS for k in range(7))


def _retention_kernel(dec_ref, q_ref, k_ref, v_ref, g_ref, *rest, n_chunks, has_state, emit_state):
    rest = list(rest)
    if has_state:
        s0f_ref, s0b_ref = rest[:2]
        rest = rest[2:]
    r_ref = rest[0]
    rest = rest[1:]
    if emit_state:
        sf_ref, sb_ref = rest[:2]
        rest = rest[2:]
    of_ref = rest[0]

    h = pl.program_id(1)
    C = RET_CHUNK
    ii = lax.broadcasted_iota(jnp.int32, (C, C), 0)
    jj = lax.broadcasted_iota(jnp.int32, (C, C), 1)
    rel = (ii - jj).astype(F32)
    idx = lax.broadcasted_iota(jnp.int32, (C, 1), 0).astype(F32)
    k_scale = HEAD_W ** -0.5

    def chunk(ref, c):
        return ref[c * C:(c + 1) * C, :]

    def run(direction):
        lg = -jnp.exp(jnp.full((1, 1), dec_ref[direction * HEADS + h], F32))
        if direction == 0:
            inner = jnp.where(rel >= 0, jnp.exp(jnp.maximum(rel, 0.0) * lg), 0.0)
            q_decay = jnp.exp((idx + 1.0) * lg)
            k_decay = jnp.exp((C - 1.0 - idx) * lg)
            order = range(n_chunks)
        else:
            inner = jnp.where(rel <= 0, jnp.exp(jnp.maximum(-rel, 0.0) * lg), 0.0)
            q_decay = jnp.exp((C - idx) * lg)
            k_decay = jnp.exp(idx * lg)
            order = range(n_chunks - 1, -1, -1)
        chunk_decay = jnp.exp(C * lg)
        if has_state:
            s = (s0f_ref if direction == 0 else s0b_ref)[0, 0, 0]
        else:
            s = jnp.zeros((HEAD_W, HEAD_W), F32)
        for c in order:
            qc = chunk(q_ref, c)
            kc = chunk(k_ref, c) * k_scale
            vc = chunk(v_ref, c).astype(BF16)
            scores = _dot_nt(qc.astype(BF16), kc.astype(BF16)) * inner
            o = _dot(scores.astype(BF16), vc) + _dot((qc * q_decay).astype(BF16), s.astype(BF16))
            s = s * chunk_decay + _dot_tn((kc * k_decay).astype(BF16), vc)
            if direction == 0:
                of_ref[c * C:(c + 1) * C, :] = o
            else:
                r = of_ref[c * C:(c + 1) * C, :] + o
                mu = jnp.mean(r, axis=-1, keepdims=True)
                rc = r - mu
                var = jnp.mean(rc * rc, axis=-1, keepdims=True)
                rn = rc * lax.rsqrt(var + GN_EPS)
                r_ref[c * C:(c + 1) * C, :] = _silu(chunk(g_ref, c)) * rn
        return s

    sf = run(0)
    sb = run(1)
    if emit_state:
        sf_ref[0, 0, 0] = sf
        sb_ref[0, 0, 0] = sb


def _retention(proj, dec, n_seq, seq_len, row_block0, s0f=None, s0b=None, e=0, emit_state=False):
    has_state = s0f is not None

    def col(base):
        return pl.BlockSpec((seq_len, HEAD_W), lambda b, h, *_: (row_block0 + b, base + h))

    state_spec = pl.BlockSpec((1, 1, 1, HEAD_W, HEAD_W), lambda b, h, *_: (b, e, h, 0, 0))
    in_specs = [pl.BlockSpec(memory_space=pltpu.SMEM), col(COL_QR), col(COL_KR), col(COL_VR), col(COL_GR)]
    args = [dec, proj, proj, proj, proj]
    if has_state:
        in_specs += [state_spec, state_spec]
        args += [s0f, s0b]
    out_specs = [pl.BlockSpec((seq_len, HEAD_W), lambda b, h, *_: (b, h))]
    out_shape = [jax.ShapeDtypeStruct((n_seq * seq_len, HEADS * HEAD_W), F32)]
    if emit_state:
        st = pl.BlockSpec((1, 1, 1, HEAD_W, HEAD_W), lambda b, h, *_: (b, 0, h, 0, 0))
        out_specs += [st, st]
        out_shape += [jax.ShapeDtypeStruct((n_seq, 1, HEADS, HEAD_W, HEAD_W), F32)] * 2
    return pl.pallas_call(
        functools.partial(_retention_kernel, n_chunks=seq_len // RET_CHUNK,
                          has_state=has_state, emit_state=emit_state),
        grid=(n_seq, HEADS),
        in_specs=in_specs,
        out_specs=out_specs,
        out_shape=out_shape,
        scratch_shapes=[pltpu.VMEM((seq_len, HEAD_W), F32)],
        compiler_params=_cparams(("arbitrary", "arbitrary")),
        name=f"retention_{seq_len}",
    )(*args)


def _diff_lambda(lq1_ref, lk1_ref, lq2_ref, lk2_ref, lam_init):
    a = jnp.sum(lq1_ref[...] * lk1_ref[...], axis=-1, keepdims=True)
    b = jnp.sum(lq2_ref[...] * lk2_ref[...], axis=-1, keepdims=True)
    return jnp.exp(a) - jnp.exp(b) + lam_init


LOG2E = 1.4426950408889634


def _diff_attend(q, k, v, lam, subln_g, lam_init):
    lane = lax.broadcasted_iota(jnp.int32, q.shape, 1)
    q1 = jnp.where(lane < DIFF_DK, q, 0.0).astype(BF16)
    q2 = jnp.where(lane >= DIFF_DK, q, 0.0).astype(BF16)

    def softmax_times_v(qz):
        s = _dot_nt(qz, k)
        p = jnp.exp2(s - jnp.max(s, axis=-1, keepdims=True))
        return _dot(p.astype(BF16), v) * (1.0 / jnp.sum(p, axis=-1, keepdims=True))

    o = softmax_times_v(q1) - lam * softmax_times_v(q2)
    o = o * lax.rsqrt(jnp.mean(o * o, axis=-1, keepdims=True) + LN_EPS)
    return o * subln_g * (1.0 - lam_init)


def _attn_prompt_kernel(q_ref, k_ref, v_ref, lq1, lk1, lq2, lk2, g_ref, o_ref, *, lam_init):
    lam = _diff_lambda(lq1, lk1, lq2, lk2, lam_init)
    scale = DIFF_DK ** -0.5 * LOG2E
    for h in range(HEADS):
        sl = slice(h * HEAD_W, (h + 1) * HEAD_W)
        o_ref[:, sl] = _diff_attend(q_ref[:, sl] * scale, k_ref[:, sl].astype(BF16),
                                    v_ref[:, sl].astype(BF16), lam, g_ref[...], lam_init)


def _attn_prompt(proj, lq1, lk1, lq2, lk2, subln_g, lam_init):
    W = HEADS * HEAD_W

    def slab(base):
        return pl.BlockSpec((SEQ, W), lambda b: (b, base // HEADS))

    small = _row_spec((1, DIFF_DK))
    return pl.pallas_call(
        functools.partial(_attn_prompt_kernel, lam_init=lam_init),
        grid=(BATCH,),
        in_specs=[slab(COL_QD), slab(COL_KD), slab(COL_VD), small, small, small, small,
                  _row_spec((1, HEAD_W))],
        out_specs=pl.BlockSpec((SEQ, W), lambda b: (b, 0)),
        out_shape=jax.ShapeDtypeStruct((T_PROMPT, W), F32),
        compiler_params=_cparams(("arbitrary",)),
        name="diff_attn_prompt",
    )(proj, proj, proj, lq1, lk1, lq2, lk2, subln_g)


def _rope(x, cos, sin_signed):
    lane = lax.broadcasted_iota(jnp.int32, x.shape, 1)
    partner = jnp.where((lane % 32) < 16, pltpu.roll(x, LANES - 16, 1), pltpu.roll(x, 16, 1))
    return x * cos + partner * sin_signed


def _attn_sample_kernel(q_ref, k_ref, v_ref, ck_ref, cv_ref, cosq_ref, sinq_ref, cos_ref, sin_ref,
                        lq1, lk1, lq2, lk2, g_ref, o_ref, kbuf, vbuf, *, lam_init):
    @pl.when(pl.program_id(2) == 0)
    def _():
        kbuf[0:DEC_SEQ, :] = _rope(k_ref[...], cos_ref[...], sin_ref[...]).astype(BF16)
        kbuf[DEC_SEQ:, :] = ck_ref[0, 0, 0].astype(BF16)
        vbuf[0:DEC_SEQ, :] = v_ref[...].astype(BF16)
        vbuf[DEC_SEQ:, :] = cv_ref[0, 0, 0].astype(BF16)

    lam = _diff_lambda(lq1, lk1, lq2, lk2, lam_init)
    q = _rope(q_ref[...], cosq_ref[...], sinq_ref[...]) * (DIFF_DK ** -0.5 * LOG2E)
    o_ref[...] = _diff_attend(q, kbuf[...], vbuf[...], lam, g_ref[...], lam_init)


ATTN_TQ = 256


def _attn_sample(proj, cache_k, cache_v, cos, sin_signed, lq1, lk1, lq2, lk2, subln_g, lam_init, e):
    nq = DEC_SEQ // ATTN_TQ
    row0_q = T_PROMPT // ATTN_TQ
    row0_kv = T_PROMPT // DEC_SEQ
    small = pl.BlockSpec((1, DIFF_DK), lambda b, h, t: (0, 0))
    cache = pl.BlockSpec((1, 1, 1, PAST_LEN, HEAD_W), lambda b, h, t: (b, e, h, 0, 0))
    table_q = pl.BlockSpec((ATTN_TQ, HEAD_W), lambda b, h, t: (t, 0))
    table = pl.BlockSpec((DEC_SEQ, HEAD_W), lambda b, h, t: (0, 0))
    return pl.pallas_call(
        functools.partial(_attn_sample_kernel, lam_init=lam_init),
        grid=(DEC_BATCH, HEADS, nq),
        in_specs=[
            pl.BlockSpec((ATTN_TQ, HEAD_W), lambda b, h, t: (row0_q + b * nq + t, COL_QD + h)),
            pl.BlockSpec((DEC_SEQ, HEAD_W), lambda b, h, t: (row0_kv + b, COL_KD + h)),
            pl.BlockSpec((DEC_SEQ, HEAD_W), lambda b, h, t: (row0_kv + b, COL_VD + h)),
            cache, cache, table_q, table_q, table, table,
            small, small, small, small,
            pl.BlockSpec((1, HEAD_W), lambda b, h, t: (0, 0)),
        ],
        out_specs=pl.BlockSpec((ATTN_TQ, HEAD_W), lambda b, h, t: (b * nq + t, h)),
        out_shape=jax.ShapeDtypeStruct((T_SAMPLE, HEADS * HEAD_W), F32),
        scratch_shapes=[pltpu.VMEM((DEC_SEQ + PAST_LEN, HEAD_W), BF16),
                        pltpu.VMEM((DEC_SEQ + PAST_LEN, HEAD_W), BF16)],
        compiler_params=_cparams(("arbitrary", "arbitrary", "arbitrary")),
        name="diff_attn_sample",
    )(proj, proj, proj, cache_k, cache_v, cos, sin_signed, cos, sin_signed,
      lq1, lk1, lq2, lk2, subln_g)


def _rope_tables():
    t = np.arange(DEC_SEQ)
    row, colp = t // GRID_W, t % GRID_W
    lane = np.arange(LANES)
    pos = np.where(((lane // 32) % 2 == 0)[None, :], row[:, None], colp[:, None]).astype(np.float64)
    half = 16
    inv = (np.float32(ROPE_THETA) ** (-(np.arange(half, dtype=np.float32)) / np.float32(half))).astype(np.float32)
    ang = pos.astype(np.float32) * inv[lane % half][None, :]
    cos = np.cos(ang.astype(np.float64)).astype(np.float32)
    sin = np.sin(ang.astype(np.float64)).astype(np.float32)
    sign = np.where((lane % 32) < half, -1.0, 1.0).astype(np.float32)[None, :]
    return jnp.asarray(cos), jnp.asarray(sin * sign)


def _split_bf16(a):
    hi = a.astype(BF16)
    return hi, (a - hi.astype(F32)).astype(BF16)


def _mixer_tail(out, x, g1_ref, sc2_ref, sh2_ref, lng_ref, lnb_ref, wr_ref, br_ref,
                x1_ref, u2_ref, meta_ref, cnt_ref):
    x1 = _layer_norm(ALPHA * x + g1_ref[0] * out, lng_ref[...], lnb_ref[...])
    x1_ref[...] = x1
    u2 = x1 * (1.0 + sc2_ref[0]) + sh2_ref[0]
    u2_ref[...] = u2.astype(BF16)

    u_hi, u_lo = _split_bf16(u2)
    w_hi, w_lo = _split_bf16(wr_ref[...])
    logits = _dot(u_hi, w_hi) + (_dot(u_hi, w_lo) + _dot(u_lo, w_hi)) + br_ref[...]
    lane = lax.broadcasted_iota(jnp.int32, logits.shape, 1).astype(F32)
    neg = jnp.float32(-jnp.inf)
    is_g = lane < N_GROUPS
    gl = jnp.where(is_g, logits, neg)
    gmax = jnp.max(gl, axis=-1, keepdims=True)
    gsel = jnp.min(jnp.where(gl == gmax, lane, float(LANES)), axis=-1, keepdims=True)
    p_g = 1.0 / jnp.sum(jnp.where(is_g, jnp.exp(gl - gmax), 0.0), axis=-1, keepdims=True)
    lo = ROUTER_LANE0 + gsel * EXPERTS_PER_GROUP
    el = jnp.where((lane >= lo) & (lane < lo + EXPERTS_PER_GROUP), logits, neg)
    v1 = jnp.max(el, axis=-1, keepdims=True)
    i1 = jnp.min(jnp.where(el == v1, lane, float(LANES)), axis=-1, keepdims=True)
    el2 = jnp.where(lane == i1, neg, el)
    v2 = jnp.max(el2, axis=-1, keepdims=True)
    i2 = jnp.min(jnp.where(el2 == v2, lane, float(LANES)), axis=-1, keepdims=True)
    t = jnp.exp(v2 - v1)
    w1 = p_g / (1.0 + t)
    w2 = w1 * t

    oh1 = (lane == i1).astype(F32)
    oh2 = (lane == i2).astype(F32)
    oh = oh1 + oh2
    r_i = lax.broadcasted_iota(jnp.int32, (TM, TM), 0)
    c_i = lax.broadcasted_iota(jnp.int32, (TM, TM), 1)
    before = (c_i < r_i).astype(BF16)
    earlier = _dot(before, oh.astype(BF16))
    rank1 = jnp.sum(earlier * oh1, axis=-1, keepdims=True)
    rank2 = jnp.sum(earlier * oh2, axis=-1, keepdims=True)
    cnt_ref[0] = jnp.sum(oh, axis=0, keepdims=True)
    cols = (i1, i2, w1, w2, rank1, rank2)
    meta = jnp.zeros_like(logits)
    for k, col in enumerate(cols):
        meta = jnp.where(lane == k, col, meta)
    meta_ref[...] = meta


META_E1, META_E2, META_W1, META_W2, META_RANK1, META_RANK2 = range(6)

_TAIL_OUT_SHAPES = [
    jax.ShapeDtypeStruct((T, D), F32),
    jax.ShapeDtypeStruct((T, D), BF16),
    jax.ShapeDtypeStruct((T, LANES), F32),
    jax.ShapeDtypeStruct((N_TILES, 1, LANES), F32),
]


def _tail_out_specs():
    return [
        pl.BlockSpec((TM, D), lambda i: (i, 0)),
        pl.BlockSpec((TM, D), lambda i: (i, 0)),
        pl.BlockSpec((TM, LANES), lambda i: (i, 0)),
        pl.BlockSpec((1, 1, LANES), lambda i: (i, 0, 0)),
    ]


def _tail_in_specs(li):
    return [
        _mod_spec(li, 2), _mod_spec(li, 4), _mod_spec(li, 3),
        _row_spec((1, D)), _row_spec((1, D)),
        _row_spec((D, LANES)), _row_spec((1, LANES)),
    ]


def _out_proj_kernel(rp_ref, rs_ref, op_ref, os_ref, w_ref, xp_ref, xs_ref, *rest):
    half = HEADS * HEAD_W
    r = _pick_tile(rp_ref, rs_ref).astype(BF16)
    o = _pick_tile(op_ref, os_ref).astype(BF16)
    out = _dot(r, w_ref[0:half, :]) + _dot(o, w_ref[half:, :])
    _mixer_tail(out, _pick_tile(xp_ref, xs_ref), *rest)


def _out_proj_tail(r_p, r_s, o_p, o_s, w_out_bf16, x_prompt, x_sample, modr, ln_g, ln_b, wr, br, li):
    half = HEADS * HEAD_W
    return pl.pallas_call(
        _out_proj_kernel,
        grid=(N_TILES,),
        in_specs=[_prompt_tile_spec(half), _sample_tile_spec(half),
                  _prompt_tile_spec(half), _sample_tile_spec(half),
                  _row_spec((2 * half, D)),
                  _prompt_tile_spec(D), _sample_tile_spec(D)] + _tail_in_specs(li),
        out_specs=_tail_out_specs(),
        out_shape=_TAIL_OUT_SHAPES,
        compiler_params=_cparams(("arbitrary",)),
        name="out_proj_tail",
    )(r_p, r_s, o_p, o_s, w_out_bf16, x_prompt, x_sample, modr, modr, modr, ln_g, ln_b, wr, br)


def _conv_glu_kernel(x_ref, sh_ref, sc_ref, w_ref, b_ref, o_ref):
    u = x_ref[...] * (1.0 + sc_ref[0]) + sh_ref[0]
    h = _dot(u.astype(BF16), w_ref[...]) + b_ref[...]
    o_ref[...] = h[:, :D] * jax.nn.sigmoid(h[:, D:])


def _conv_glu(x, modr, w1_bf16, b1, li):
    return pl.pallas_call(
        _conv_glu_kernel,
        grid=(N_TILES,),
        in_specs=[pl.BlockSpec((TM, D), lambda i: (i, 0)), _mod_spec(li, 0), _mod_spec(li, 1),
                  _row_spec((D, 2 * D)), _row_spec((1, 2 * D))],
        out_specs=pl.BlockSpec((TM, D), lambda i: (i, 0)),
        out_shape=jax.ShapeDtypeStruct((T, D), F32),
        compiler_params=_cparams(("arbitrary",)),
        name="conv_glu",
    )(x, modr, modr, w1_bf16, b1)


HALO = 16
CONV_ROWS = 64
CONV_COLS = 128


def _depthwise_conv(hp, dw_ref, conv):
    base = HALO - CONV_PAD
    for cb in range(D // CONV_COLS):
        cs = slice(cb * CONV_COLS, (cb + 1) * CONV_COLS)
        for rb in range(TM // CONV_ROWS):
            r0 = rb * CONV_ROWS
            acc = None
            for shift in range(SUBLANES):
                part = None
                for tap in range(CONV_K):
                    off = base + tap
                    if off % SUBLANES != shift:
                        continue
                    a0 = r0 + off - shift
                    term = hp[a0:a0 + CONV_ROWS + SUBLANES, cs] * dw_ref[tap:tap + 1, cs]
                    part = term if part is None else part + term
                part = part[shift:shift + CONV_ROWS, :]
                acc = part if acc is None else acc + part
            conv[r0:r0 + CONV_ROWS, cs] = acc


def _conv_tail_kernel(cur_ref, prev_ref, next_ref, dw_ref, dwb_ref, cg_ref, cb_ref, w2_ref, b2_ref,
                      x_ref, *rest):
    tail_args, hp, conv = rest[:-2], rest[-2], rest[-1]
    i = pl.program_id(0)
    k = (i - PROMPT_TILES) % SAMPLE_TILES_PER_SEQ
    in_sample = i >= PROMPT_TILES
    left_ok = jnp.logical_and(in_sample, k != 0)
    right_ok = jnp.logical_and(in_sample, k != SAMPLE_TILES_PER_SEQ - 1)
    hp[0:HALO, :] = jnp.where(left_ok, prev_ref[...], 0.0)
    hp[HALO:HALO + TM, :] = cur_ref[...]
    hp[HALO + TM:HALO + TM + HALO, :] = jnp.where(right_ok, next_ref[...], 0.0)
    _depthwise_conv(hp, dw_ref, conv)
    hc = _silu(_layer_norm(conv[...] + dwb_ref[...], cg_ref[...], cb_ref[...]))
    out = _dot(hc.astype(BF16), w2_ref[...]) + b2_ref[...]
    _mixer_tail(out, x_ref[...], *tail_args)


def _conv_tail(glu, dw, dwb, cg, cb, w2_bf16, b2, x, modr, ln_g, ln_b, wr, br, li):
    per = TM // HALO
    last = T // HALO - 1
    return pl.pallas_call(
        _conv_tail_kernel,
        grid=(N_TILES,),
        in_specs=[pl.BlockSpec((TM, D), lambda i: (i, 0)),
                  pl.BlockSpec((HALO, D), lambda i: (jnp.maximum(i * per - 1, 0), 0)),
                  pl.BlockSpec((HALO, D), lambda i: (jnp.minimum((i + 1) * per, last), 0)),
                  _row_spec((CONV_K, D)), _row_spec((1, D)), _row_spec((1, D)), _row_spec((1, D)),
                  _row_spec((D, D)), _row_spec((1, D)),
                  pl.BlockSpec((TM, D), lambda i: (i, 0))] + _tail_in_specs(li),
        out_specs=_tail_out_specs(),
        out_shape=_TAIL_OUT_SHAPES,
        scratch_shapes=[pltpu.VMEM((TM + 2 * HALO, D), F32), pltpu.VMEM((TM, D), F32)],
        compiler_params=_cparams(("arbitrary",)),
        name="conv_tail",
    )(glu, glu, glu, dw, dwb, cg, cb, w2_bf16, b2, x, modr, modr, modr, ln_g, ln_b, wr, br)


def _sorted_positions(meta, srcv):
    lane = lax.broadcasted_iota(jnp.int32, meta.shape, 1).astype(F32)

    def pos(e_col, r_col):
        start = jnp.sum(jnp.where(lane == meta[:, e_col:e_col + 1], srcv, 0.0), axis=-1, keepdims=True)
        return start + meta[:, r_col:r_col + 1]

    return pos(META_E1, META_RANK1), pos(META_E2, META_RANK2)


def _one_hot_rows(pos):
    col = lax.broadcasted_iota(jnp.int32, (TM, SORT_ROWS), 1).astype(F32)
    return col == pos


def _for_each_run_piece(tile, src_ref, dst_ref, n_ref, fn):
    for e in range(N_EXPERTS):
        k = tile * N_EXPERTS + e
        n, src, dst = n_ref[k], src_ref[k], dst_ref[k]
        done = 0
        for bit in RUN_BITS:
            piece = n & bit

            @pl.when(piece != 0)
            def _(done=done, bit=bit):
                fn(pl.multiple_of(src + done, RUN_ALIGN), pl.multiple_of(dst + done, RUN_ALIGN), bit)

            done = done + piece


def _wait_rows(total, make_copy):
    for bit in RUN_BITS:
        @pl.when((total & bit) != 0)
        def _(bit=bit):
            make_copy(bit).wait()


def _dispatch_kernel(src_ref, dst_ref, n_ref, tot_ref, u_ref, meta_ref, srcv_ref, xs_ref, sorted_ref, sem):
    i = pl.program_id(0)
    pos1, pos2 = _sorted_positions(meta_ref[...], srcv_ref[0])
    select = jnp.logical_or(_one_hot_rows(pos1), _one_hot_rows(pos2)).astype(BF16)
    sorted_ref[...] = _dot_tn(select, u_ref[...])

    def start(src, dst, rows):
        pltpu.make_async_copy(sorted_ref.at[pl.ds(src, rows)], xs_ref.at[pl.ds(dst, rows)], sem).start()

    _for_each_run_piece(i, src_ref, dst_ref, n_ref, start)
    _wait_rows(tot_ref[i], lambda rows: pltpu.make_async_copy(
        sorted_ref.at[pl.ds(0, rows)], xs_ref.at[pl.ds(0, rows)], sem))


def _dispatch(sched, u2, meta, srcv):
    return pl.pallas_call(
        _dispatch_kernel,
        grid_spec=pltpu.PrefetchScalarGridSpec(
            num_scalar_prefetch=4,
            grid=(N_TILES,),
            in_specs=[pl.BlockSpec((TM, D), lambda i, *_: (i, 0)),
                      pl.BlockSpec((TM, LANES), lambda i, *_: (i, 0)),
                      pl.BlockSpec((1, 1, LANES), lambda i, *_: (i, 0, 0))],
            out_specs=pl.BlockSpec(memory_space=pl.ANY),
            scratch_shapes=[pltpu.VMEM((SORT_ROWS, D), F32), pltpu.SemaphoreType.DMA(())],
        ),
        out_shape=jax.ShapeDtypeStruct((MOE_ROWS, D), F32),
        compiler_params=_cparams(("arbitrary",)),
        name="moe_dispatch",
    )(*sched, u2, meta, srcv)


def _experts_kernel(te_ref, tb_ref, nt_ref, xs_ref, wg_ref, wu_ref, wd_ref, ys_ref):
    @pl.when(pl.program_id(0) < nt_ref[0])
    def _():
        x = xs_ref[...].astype(BF16)
        hg = _dot(x, wg_ref[0, 0].astype(BF16))
        hu = _dot(x, wu_ref[0, 0].astype(BF16))
        h = (_silu(hg) * hu).astype(BF16)
        ys_ref[...] = _dot(h, wd_ref[0, 0].astype(BF16))


def _experts(tile_expert, tile_block, n_tiles, xs, w_gate, w_up, w_down, li):
    return pl.pallas_call(
        _experts_kernel,
        grid_spec=pltpu.PrefetchScalarGridSpec(
            num_scalar_prefetch=3,
            grid=(MOE_MAX_TILES,),
            in_specs=[
                pl.BlockSpec((MOE_TM, D), lambda i, te, tb, nt: (tb[i], 0)),
                pl.BlockSpec((1, 1, D, D_EXPERT), lambda i, te, tb, nt: (li, te[i], 0, 0)),
                pl.BlockSpec((1, 1, D, D_EXPERT), lambda i, te, tb, nt: (li, te[i], 0, 0)),
                pl.BlockSpec((1, 1, D_EXPERT, D), lambda i, te, tb, nt: (li, te[i], 0, 0)),
            ],
            out_specs=pl.BlockSpec((MOE_TM, D), lambda i, te, tb, nt: (tb[i], 0)),
        ),
        out_shape=jax.ShapeDtypeStruct((MOE_ROWS, D), F32),
        compiler_params=_cparams(("arbitrary",)),
        name="moe_experts",
    )(tile_expert, tile_block, n_tiles, xs, w_gate, w_up, w_down)


def _combine_kernel(src_ref, dst_ref, n_ref, tot_ref, ys_ref, x1_ref, meta_ref, srcv_ref, g2_ref,
                    lng_ref, lnb_ref, *rest, split):
    outs, (sorted_ref, sem) = rest[:-2], rest[-2:]
    i = pl.program_id(0)

    @pl.when(i == 0)
    def _():
        sorted_ref[...] = jnp.zeros_like(sorted_ref)

    def start(src, dst, rows):
        pltpu.make_async_copy(ys_ref.at[pl.ds(dst, rows)], sorted_ref.at[pl.ds(src, rows)], sem).start()

    _for_each_run_piece(i, src_ref, dst_ref, n_ref, start)
    meta = meta_ref[...]
    pos1, pos2 = _sorted_positions(meta, srcv_ref[0])
    sel1 = _one_hot_rows(pos1).astype(BF16)
    sel2 = _one_hot_rows(pos2).astype(BF16)
    _wait_rows(tot_ref[i], lambda rows: pltpu.make_async_copy(
        ys_ref.at[pl.ds(0, rows)], sorted_ref.at[pl.ds(0, rows)], sem))
    ysort = sorted_ref[...].astype(BF16)
    f = (meta[:, META_W1:META_W1 + 1] * _dot(sel1, ysort)
         + meta[:, META_W2:META_W2 + 1] * _dot(sel2, ysort))
    y = _layer_norm(ALPHA * x1_ref[...] + g2_ref[0] * f, lng_ref[...], lnb_ref[...])
    if split:
        @pl.when(i < PROMPT_TILES)
        def _():
            outs[0][...] = y

        @pl.when(i >= PROMPT_TILES)
        def _():
            outs[1][...] = y
    else:
        outs[0][...] = y


def _combine(sched, ys, x1, meta, srcv, modr, ln_g, ln_b, li, split):
    if split:
        out_specs = [_prompt_tile_spec(D), _sample_tile_spec(D)]
        out_shape = [jax.ShapeDtypeStruct((T_PROMPT, D), F32), jax.ShapeDtypeStruct((T_SAMPLE, D), F32)]
    else:
        out_specs = [pl.BlockSpec((TM, D), lambda i, *_: (i, 0))]
        out_shape = [jax.ShapeDtypeStruct((T, D), F32)]
    return pl.pallas_call(
        functools.partial(_combine_kernel, split=split),
        grid_spec=pltpu.PrefetchScalarGridSpec(
            num_scalar_prefetch=4,
            grid=(N_TILES,),
            in_specs=[pl.BlockSpec(memory_space=pl.ANY),
                      pl.BlockSpec((TM, D), lambda i, *_: (i, 0)),
                      pl.BlockSpec((TM, LANES), lambda i, *_: (i, 0)),
                      pl.BlockSpec((1, 1, LANES), lambda i, *_: (i, 0, 0)),
                      _mod_spec(li, 5), _row_spec((1, D)), _row_spec((1, D))],
            out_specs=out_specs,
            scratch_shapes=[pltpu.VMEM((SORT_ROWS, D), F32), pltpu.SemaphoreType.DMA(())],
        ),
        out_shape=out_shape,
        compiler_params=_cparams(("arbitrary",)),
        name="moe_combine",
    )(*sched, ys, x1, meta, srcv, modr, ln_g, ln_b)


def _moe_schedule(tile_counts):
    n = (tile_counts + RUN_ALIGN - 1) // RUN_ALIGN * RUN_ALIGN
    src = jnp.cumsum(n, axis=1) - n
    per_expert = jnp.sum(n, axis=0)
    seg = (per_expert + MOE_TM - 1) // MOE_TM * MOE_TM
    seg_start = jnp.cumsum(seg) - seg
    dst = seg_start[None, :] + jnp.cumsum(n, axis=0) - n
    tiles = seg // MOE_TM
    ends = jnp.cumsum(tiles)
    total = ends[-1]
    i = jnp.minimum(jnp.arange(MOE_MAX_TILES, dtype=jnp.int32), jnp.maximum(total - 1, 0))
    e = jnp.sum((ends[None, :] <= i[:, None]).astype(jnp.int32), axis=1)
    block = seg_start[e] // MOE_TM + (i - (ends[e] - tiles[e]))
    runs = (src.reshape(-1), dst.reshape(-1), n.reshape(-1), jnp.sum(n, axis=1))
    srcv = jnp.pad(src.astype(F32), ((0, 0), (ROUTER_LANE0, LANES - ROUTER_LANE0 - N_EXPERTS)))
    return runs, srcv.reshape(N_TILES, 1, LANES), (e, block, total.reshape(1))


def _moe(x1, u2, meta, cnt, modr, w_gate, w_up, w_down, ln_g, ln_b, li, split):
    tile_counts = cnt[:, 0, ROUTER_LANE0:ROUTER_LANE0 + N_EXPERTS].astype(jnp.int32)
    runs, srcv, (te, tb, nt) = _moe_schedule(tile_counts)
    xs = _dispatch(runs, u2, meta, srcv)
    ys = _experts(te, tb, nt, xs, w_gate, w_up, w_down, li)
    return _combine(runs, ys, x1, meta, srcv, modr, ln_g, ln_b, li, split)


def _router_slab(wg, bg, we, be):
    w = jnp.concatenate([wg, we.transpose(1, 0, 2).reshape(D, N_EXPERTS)], axis=1)
    b = jnp.concatenate([bg, be.reshape(N_EXPERTS)])
    pad = LANES - w.shape[1]
    return jnp.pad(w, ((0, 0), (0, pad))), jnp.pad(b, (0, pad)).reshape(1, LANES)


def kernel(x_prompt, x_sample, cache_diff_k, cache_diff_v, state_ret_fwd, state_ret_bwd, c, c_ctx, mod_w, mod_b, ln1_g, ln1_b, ln2_g, ln2_b, mix_w_in, mix_w_out, ret_decay_fwd, ret_decay_bwd, diff_lq1, diff_lk1, diff_lq2, diff_lk2, diff_subln_g, conv_w1, conv_b1, conv_dw, conv_dw_b, conv_ln_g, conv_ln_b, conv_w2, conv_b2, router_g_w, router_g_b, router_e_w, router_e_b, moe_w_gate, moe_w_up, moe_w_down):
    xp = x_prompt.reshape(T_PROMPT, D)
    xs = x_sample.reshape(T_SAMPLE, D)
    cond = jnp.concatenate([c_ctx[None, :], c, jnp.zeros((MOD_ROWS - 1 - DEC_BATCH, D), F32)], axis=0)
    modr = _mod_vectors(cond, mod_w, mod_b).reshape(DEPTH * MOD_ROWS * 6, 1, D)
    cos, sin_signed = _rope_tables()

    def row(v):
        return v.reshape(1, -1)

    x = None
    caches = None
    for li in range(DEPTH):
        wr, br = _router_slab(router_g_w[li], router_g_b[li], router_e_w[li], router_e_b[li])
        if li % 2 == 0:
            assert li == 0, "the even mixer reads the kernel inputs directly"
            e = li // 2
            lam_init = 0.8 - 0.6 * math.exp(-0.3 * li)
            proj, ck, cv = _in_proj(xp, xs, modr, mix_w_in[e].astype(BF16), li)
            dec = jnp.concatenate([ret_decay_fwd[e], ret_decay_bwd[e]])
            r_p, sf, sb = _retention(proj, dec, BATCH, SEQ, 0, emit_state=True)
            (r_s,) = _retention(proj, dec, DEC_BATCH, DEC_SEQ, T_PROMPT // DEC_SEQ,
                                s0f=state_ret_fwd, s0b=state_ret_bwd, e=e)
            lams = (row(diff_lq1[e]), row(diff_lk1[e]), row(diff_lq2[e]), row(diff_lk2[e]),
                    row(diff_subln_g[e]))
            o_p = _attn_prompt(proj, *lams, lam_init)
            o_s = _attn_sample(proj, cache_diff_k, cache_diff_v, cos, sin_signed, *lams, lam_init, e)
            x1, u2, meta, cnt = _out_proj_tail(r_p, r_s, o_p, o_s, mix_w_out[e].astype(BF16), xp, xs, modr,
                                               row(ln1_g[li]), row(ln1_b[li]), wr, br, li)
            caches = (ck, cv, sf, sb)
        else:
            o = li // 2
            glu = _conv_glu(x, modr, conv_w1[o].astype(BF16), row(conv_b1[o]), li)
            x1, u2, meta, cnt = _conv_tail(glu, conv_dw[o], row(conv_dw_b[o]), row(conv_ln_g[o]),
                                           row(conv_ln_b[o]), conv_w2[o].astype(BF16), row(conv_b2[o]),
                                           x, modr, row(ln1_g[li]), row(ln1_b[li]), wr, br, li)
        outs = _moe(x1, u2, meta, cnt, modr, moe_w_gate, moe_w_up, moe_w_down,
                    row(ln2_g[li]), row(ln2_b[li]), li, split=(li == DEPTH - 1))
        x = outs[0]

    y_prompt = outs[0].reshape(BATCH, SEQ, D)
    y_sample = outs[1].reshape(DEC_BATCH, DEC_SEQ, D)
    return (y_prompt, y_sample) + caches
```

```python
import functools
import math

import numpy as np
import jax
import jax.numpy as jnp
from jax import lax
from jax.experimental import pallas as pl
from jax.experimental.pallas import tpu as pltpu

F32 = jnp.float32
BF16 = jnp.bfloat16

D = 1024
BATCH = 16
SEQ = 256
DEPTH = 2
DEC_BATCH = 2
DEC_SEQ = 2048
PAST_LEN = 512
GRID_W = 64
HEADS = 4
HEAD_W = 128
RET_CHUNK = 128
DIFF_DK = 64
ROPE_THETA = 10000.0
IN_W = 7 * HEADS * HEAD_W
CONV_K = 31
CONV_PAD = CONV_K // 2
N_GROUPS = 4
EXPERTS_PER_GROUP = 8
N_EXPERTS = N_GROUPS * EXPERTS_PER_GROUP
D_EXPERT = 512
ALPHA = (2.0 * DEPTH) ** 0.25
LN_EPS = 1e-5
GN_EPS = 1e-6

T_PROMPT = BATCH * SEQ
T_SAMPLE = DEC_BATCH * DEC_SEQ
T = T_PROMPT + T_SAMPLE
TM = 256
N_TILES = T // TM
PROMPT_TILES = T_PROMPT // TM
SAMPLE_TILES_PER_SEQ = DEC_SEQ // TM
MOD_ROWS = 8
MOE_TM = 256
LANES = 128
SUBLANES = 8
RUN_ALIGN = SUBLANES
SORT_ROWS = -(-(2 * TM + N_EXPERTS * (RUN_ALIGN - 1)) // TM) * TM
RUN_BITS = tuple(1 << b for b in range((2 * TM).bit_length() - 1, RUN_ALIGN.bit_length() - 2, -1))
MOE_MAX_TILES = -(-(2 * T + N_TILES * N_EXPERTS * (RUN_ALIGN - 1) + N_EXPERTS * (MOE_TM - RUN_ALIGN)) // MOE_TM)
MOE_ROWS = MOE_MAX_TILES * MOE_TM
ROUTER_LANE0 = N_GROUPS
VMEM_LIMIT = 52 * 1024 * 1024


def _cparams(sem):
    return pltpu.CompilerParams(dimension_semantics=sem, vmem_limit_bytes=VMEM_LIMIT)


def _tile_cond_row(i):
    return jnp.where(i < PROMPT_TILES, 0, 1 + (i - PROMPT_TILES) // SAMPLE_TILES_PER_SEQ)


def _mod_spec(li, k):
    return pl.BlockSpec((1, 1, D), lambda i, *_: ((li * MOD_ROWS + _tile_cond_row(i)) * 6 + k, 0, 0))


def _row_spec(shape):
    return pl.BlockSpec(shape, lambda i, *_: (0,) * len(shape))


def _layer_norm(x, g, b):
    mu = jnp.mean(x, axis=-1, keepdims=True)
    xc = x - mu
    var = jnp.mean(xc * xc, axis=-1, keepdims=True)
    return xc * lax.rsqrt(var + LN_EPS) * g + b


def _silu(x):
    return x * jax.nn.sigmoid(x)


def _dot(a, b):
    return jnp.dot(a, b, preferred_element_type=F32)


def _dot_nt(a, b):
    return lax.dot_general(a, b, (((1,), (1,)), ((), ())), preferred_element_type=F32)


def _dot_tn(a, b):
    return lax.dot_general(a, b, (((0,), (0,)), ((), ())), preferred_element_type=F32)


MOD_TN = 512


def _mod_kernel(cond_ref, w_ref, b_ref, o_ref):
    s = _silu(cond_ref[...])
    o_ref[0] = jnp.dot(s, w_ref[0], precision=lax.Precision.HIGHEST,
                       preferred_element_type=F32) + b_ref[0]


def _mod_vectors(cond, mod_w, mod_b):
    return pl.pallas_call(
        _mod_kernel,
        grid=(DEPTH, 6 * D // MOD_TN),
        in_specs=[
            pl.BlockSpec((MOD_ROWS, D), lambda l, j: (0, 0)),
            pl.BlockSpec((1, D, MOD_TN), lambda l, j: (l, 0, j)),
            pl.BlockSpec((1, 1, MOD_TN), lambda l, j: (l, 0, j)),
        ],
        out_specs=pl.BlockSpec((1, MOD_ROWS, MOD_TN), lambda l, j: (l, 0, j)),
        out_shape=jax.ShapeDtypeStruct((DEPTH, MOD_ROWS, 6 * D), F32),
        compiler_params=_cparams(("arbitrary", "arbitrary")),
        name="mod_vectors",
    )(cond, mod_w, mod_b.reshape(DEPTH, 1, 6 * D))


def _prompt_tile_spec(width):
    return pl.BlockSpec((TM, width), lambda i, *_: (jnp.minimum(i, PROMPT_TILES - 1), 0))


def _sample_tile_spec(width):
    return pl.BlockSpec((TM, width), lambda i, *_: (jnp.maximum(i - PROMPT_TILES, 0), 0))


def _pick_tile(prompt_ref, sample_ref):
    return jnp.where(pl.program_id(0) < PROMPT_TILES, prompt_ref[...], sample_ref[...])


def _in_proj_kernel(xp_ref, xs_ref, sh_ref, sc_ref, w_ref, o_ref, ck_ref, cv_ref):
    u = _pick_tile(xp_ref, xs_ref) * (1.0 + sc_ref[0]) + sh_ref[0]
    proj = _dot(u.astype(BF16), w_ref[...])
    o_ref[...] = proj

    @pl.when(pl.program_id(0) < PROMPT_TILES)
    def _():
        for h in range(HEADS):
            ck_ref[0, 0, h] = proj[:, (COL_KD + h) * HEAD_W:(COL_KD + h + 1) * HEAD_W]
            cv_ref[0, 0, h] = proj[:, (COL_VD + h) * HEAD_W:(COL_VD + h + 1) * HEAD_W]


def _in_proj(x_prompt, x_sample, modr, w_in_bf16, li):
    cache_spec = pl.BlockSpec((1, 1, HEADS, SEQ, HEAD_W),
                              lambda i: (jnp.minimum(i, PROMPT_TILES - 1), 0, 0, 0, 0))
    cache_shape = jax.ShapeDtypeStruct((BATCH, 1, HEADS, SEQ, HEAD_W), F32)
    return pl.pallas_call(
        _in_proj_kernel,
        grid=(N_TILES,),
        in_specs=[
            _prompt_tile_spec(D), _sample_tile_spec(D),
            _mod_spec(li, 0),
            _mod_spec(li, 1),
            _row_spec((D, IN_W)),
        ],
        out_specs=[pl.BlockSpec((TM, IN_W), lambda i: (i, 0)), cache_spec, cache_spec],
        out_shape=[jax.ShapeDtypeStruct((T, IN_W), F32), cache_shape, cache_shape],
        compiler_params=_cparams(("arbitrary",)),
        name="in_proj",
    )(x_prompt, x_sample, modr, modr, w_in_bf16)


COL_QR, COL_KR, COL_VR, COL_GR, COL_QD, COL_KD, COL_VD = (k * HEADS for k in range(7))


def _retention_kernel(dec_ref, q_ref, k_ref, v_ref, g_ref, *rest, n_chunks, has_state, emit_state):
    rest = list(rest)
    if has_state:
        s0f_ref, s0b_ref = rest[:2]
        rest = rest[2:]
    r_ref = rest[0]
    rest = rest[1:]
    if emit_state:
        sf_ref, sb_ref = rest[:2]
        rest = rest[2:]
    of_ref = rest[0]

    h = pl.program_id(1)
    C = RET_CHUNK
    ii = lax.broadcasted_iota(jnp.int32, (C, C), 0)
    jj = lax.broadcasted_iota(jnp.int32, (C, C), 1)
    rel = (ii - jj).astype(F32)
    idx = lax.broadcasted_iota(jnp.int32, (C, 1), 0).astype(F32)
    k_scale = HEAD_W ** -0.5

    def chunk(ref, c):
        return ref[c * C:(c + 1) * C, :]

    def run(direction):
        lg = -jnp.exp(jnp.full((1, 1), dec_ref[direction * HEADS + h], F32))
        if direction == 0:
            inner = jnp.where(rel >= 0, jnp.exp(jnp.maximum(rel, 0.0) * lg), 0.0)
            q_decay = jnp.exp((idx + 1.0) * lg)
            k_decay = jnp.exp((C - 1.0 - idx) * lg)
            order = range(n_chunks)
        else:
            inner = jnp.where(rel <= 0, jnp.exp(jnp.maximum(-rel, 0.0) * lg), 0.0)
            q_decay = jnp.exp((C - idx) * lg)
            k_decay = jnp.exp(idx * lg)
            order = range(n_chunks - 1, -1, -1)
        chunk_decay = jnp.exp(C * lg)
        if has_state:
            s = (s0f_ref if direction == 0 else s0b_ref)[0, 0, 0]
        else:
            s = jnp.zeros((HEAD_W, HEAD_W), F32)
        for c in order:
            qc = chunk(q_ref, c)
            kc = chunk(k_ref, c) * k_scale
            vc = chunk(v_ref, c).astype(BF16)
            scores = _dot_nt(qc.astype(BF16), kc.astype(BF16)) * inner
            o = _dot(scores.astype(BF16), vc) + _dot((qc * q_decay).astype(BF16), s.astype(BF16))
            s = s * chunk_decay + _dot_tn((kc * k_decay).astype(BF16), vc)
            if direction == 0:
                of_ref[c * C:(c + 1) * C, :] = o
            else:
                r = of_ref[c * C:(c + 1) * C, :] + o
                mu = jnp.mean(r, axis=-1, keepdims=True)
                rc = r - mu
                var = jnp.mean(rc * rc, axis=-1, keepdims=True)
                rn = rc * lax.rsqrt(var + GN_EPS)
                r_ref[c * C:(c + 1) * C, :] = _silu(chunk(g_ref, c)) * rn
        return s

    sf = run(0)
    sb = run(1)
    if emit_state:
        sf_ref[0, 0, 0] = sf
        sb_ref[0, 0, 0] = sb


def _retention(proj, dec, n_seq, seq_len, row_block0, s0f=None, s0b=None, e=0, emit_state=False):
    has_state = s0f is not None

    def col(base):
        return pl.BlockSpec((seq_len, HEAD_W), lambda b, h, *_: (row_block0 + b, base + h))

    state_spec = pl.BlockSpec((1, 1, 1, HEAD_W, HEAD_W), lambda b, h, *_: (b, e, h, 0, 0))
    in_specs = [pl.BlockSpec(memory_space=pltpu.SMEM), col(COL_QR), col(COL_KR), col(COL_VR), col(COL_GR)]
    args = [dec, proj, proj, proj, proj]
    if has_state:
        in_specs += [state_spec, state_spec]
        args += [s0f, s0b]
    out_specs = [pl.BlockSpec((seq_len, HEAD_W), lambda b, h, *_: (b, h))]
    out_shape = [jax.ShapeDtypeStruct((n_seq * seq_len, HEADS * HEAD_W), F32)]
    if emit_state:
        st = pl.BlockSpec((1, 1, 1, HEAD_W, HEAD_W), lambda b, h, *_: (b, 0, h, 0, 0))
        out_specs += [st, st]
        out_shape += [jax.ShapeDtypeStruct((n_seq, 1, HEADS, HEAD_W, HEAD_W), F32)] * 2
    return pl.pallas_call(
        functools.partial(_retention_kernel, n_chunks=seq_len // RET_CHUNK,
                          has_state=has_state, emit_state=emit_state),
        grid=(n_seq, HEADS),
        in_specs=in_specs,
        out_specs=out_specs,
        out_shape=out_shape,
        scratch_shapes=[pltpu.VMEM((seq_len, HEAD_W), F32)],
        compiler_params=_cparams(("arbitrary", "arbitrary")),
        name=f"retention_{seq_len}",
    )(*args)


def _diff_lambda(lq1_ref, lk1_ref, lq2_ref, lk2_ref, lam_init):
    a = jnp.sum(lq1_ref[...] * lk1_ref[...], axis=-1, keepdims=True)
    b = jnp.sum(lq2_ref[...] * lk2_ref[...], axis=-1, keepdims=True)
    return jnp.exp(a) - jnp.exp(b) + lam_init


LOG2E = 1.4426950408889634


def _diff_attend(q, k, v, lam, subln_g, lam_init):
    lane = lax.broadcasted_iota(jnp.int32, q.shape, 1)
    q1 = jnp.where(lane < DIFF_DK, q, 0.0).astype(BF16)
    q2 = jnp.where(lane >= DIFF_DK, q, 0.0).astype(BF16)

    def softmax_times_v(qz):
        s = _dot_nt(qz, k)
        p = jnp.exp2(s - jnp.max(s, axis=-1, keepdims=True))
        return _dot(p.astype(BF16), v) * (1.0 / jnp.sum(p, axis=-1, keepdims=True))

    o = softmax_times_v(q1) - lam * softmax_times_v(q2)
    o = o * lax.rsqrt(jnp.mean(o * o, axis=-1, keepdims=True) + LN_EPS)
    return o * subln_g * (1.0 - lam_init)


def _attn_prompt_kernel(q_ref, k_ref, v_ref, lq1, lk1, lq2, lk2, g_ref, o_ref, *, lam_init):
    lam = _diff_lambda(lq1, lk1, lq2, lk2, lam_init)
    scale = DIFF_DK ** -0.5 * LOG2E
    for h in range(HEADS):
        sl = slice(h * HEAD_W, (h + 1) * HEAD_W)
        o_ref[:, sl] = _diff_attend(q_ref[:, sl] * scale, k_ref[:, sl].astype(BF16),
                                    v_ref[:, sl].astype(BF16), lam, g_ref[...], lam_init)


def _attn_prompt(proj, lq1, lk1, lq2, lk2, subln_g, lam_init):
    W = HEADS * HEAD_W

    def slab(base):
        return pl.BlockSpec((SEQ, W), lambda b: (b, base // HEADS))

    small = _row_spec((1, DIFF_DK))
    return pl.pallas_call(
        functools.partial(_attn_prompt_kernel, lam_init=lam_init),
        grid=(BATCH,),
        in_specs=[slab(COL_QD), slab(COL_KD), slab(COL_VD), small, small, small, small,
                  _row_spec((1, HEAD_W))],
        out_specs=pl.BlockSpec((SEQ, W), lambda b: (b, 0)),
        out_shape=jax.ShapeDtypeStruct((T_PROMPT, W), F32),
        compiler_params=_cparams(("arbitrary",)),
        name="diff_attn_prompt",
    )(proj, proj, proj, lq1, lk1, lq2, lk2, subln_g)


def _rope(x, cos, sin_signed):
    lane = lax.broadcasted_iota(jnp.int32, x.shape, 1)
    partner = jnp.where((lane % 32) < 16, pltpu.roll(x, LANES - 16, 1), pltpu.roll(x, 16, 1))
    return x * cos + partner * sin_signed


def _attn_sample_kernel(q_ref, k_ref, v_ref, ck_ref, cv_ref, cosq_ref, sinq_ref, cos_ref, sin_ref,
                        lq1, lk1, lq2, lk2, g_ref, o_ref, kbuf, vbuf, *, lam_init):
    @pl.when(pl.program_id(2) == 0)
    def _():
        kbuf[0:DEC_SEQ, :] = _rope(k_ref[...], cos_ref[...], sin_ref[...]).astype(BF16)
        kbuf[DEC_SEQ:, :] = ck_ref[0, 0, 0].astype(BF16)
        vbuf[0:DEC_SEQ, :] = v_ref[...].astype(BF16)
        vbuf[DEC_SEQ:, :] = cv_ref[0, 0, 0].astype(BF16)

    lam = _diff_lambda(lq1, lk1, lq2, lk2, lam_init)
    q = _rope(q_ref[...], cosq_ref[...], sinq_ref[...]) * (DIFF_DK ** -0.5 * LOG2E)
    o_ref[...] = _diff_attend(q, kbuf[...], vbuf[...], lam, g_ref[...], lam_init)


ATTN_TQ = 256


def _attn_sample(proj, cache_k, cache_v, cos, sin_signed, lq1, lk1, lq2, lk2, subln_g, lam_init, e):
    nq = DEC_SEQ // ATTN_TQ
    row0_q = T_PROMPT // ATTN_TQ
    row0_kv = T_PROMPT // DEC_SEQ
    small = pl.BlockSpec((1, DIFF_DK), lambda b, h, t: (0, 0))
    cache = pl.BlockSpec((1, 1, 1, PAST_LEN, HEAD_W), lambda b, h, t: (b, e, h, 0, 0))
    table_q = pl.BlockSpec((ATTN_TQ, HEAD_W), lambda b, h, t: (t, 0))
    table = pl.BlockSpec((DEC_SEQ, HEAD_W), lambda b, h, t: (0, 0))
    return pl.pallas_call(
        functools.partial(_attn_sample_kernel, lam_init=lam_init),
        grid=(DEC_BATCH, HEADS, nq),
        in_specs=[
            pl.BlockSpec((ATTN_TQ, HEAD_W), lambda b, h, t: (row0_q + b * nq + t, COL_QD + h)),
            pl.BlockSpec((DEC_SEQ, HEAD_W), lambda b, h, t: (row0_kv + b, COL_KD + h)),
            pl.BlockSpec((DEC_SEQ, HEAD_W), lambda b, h, t: (row0_kv + b, COL_VD + h)),
            cache, cache, table_q, table_q, table, table,
            small, small, small, small,
            pl.BlockSpec((1, HEAD_W), lambda b, h, t: (0, 0)),
        ],
        out_specs=pl.BlockSpec((ATTN_TQ, HEAD_W), lambda b, h, t: (b * nq + t, h)),
        out_shape=jax.ShapeDtypeStruct((T_SAMPLE, HEADS * HEAD_W), F32),
        scratch_shapes=[pltpu.VMEM((DEC_SEQ + PAST_LEN, HEAD_W), BF16),
                        pltpu.VMEM((DEC_SEQ + PAST_LEN, HEAD_W), BF16)],
        compiler_params=_cparams(("arbitrary", "arbitrary", "arbitrary")),
        name="diff_attn_sample",
    )(proj, proj, proj, cache_k, cache_v, cos, sin_signed, cos, sin_signed,
      lq1, lk1, lq2, lk2, subln_g)


def _rope_tables():
    t = np.arange(DEC_SEQ)
    row, colp = t // GRID_W, t % GRID_W
    lane = np.arange(LANES)
    pos = np.where(((lane // 32) % 2 == 0)[None, :], row[:, None], colp[:, None]).astype(np.float64)
    half = 16
    inv = (np.float32(ROPE_THETA) ** (-(np.arange(half, dtype=np.float32)) / np.float32(half))).astype(np.float32)
    ang = pos.astype(np.float32) * inv[lane % half][None, :]
    cos = np.cos(ang.astype(np.float64)).astype(np.float32)
    sin = np.sin(ang.astype(np.float64)).astype(np.float32)
    sign = np.where((lane % 32) < half, -1.0, 1.0).astype(np.float32)[None, :]
    return jnp.asarray(cos), jnp.asarray(sin * sign)


def _split_bf16(a):
    hi = a.astype(BF16)
    return hi, (a - hi.astype(F32)).astype(BF16)


def _mixer_tail(out, x, g1_ref, sc2_ref, sh2_ref, lng_ref, lnb_ref, wr_ref, br_ref,
                x1_ref, u2_ref, meta_ref, cnt_ref):
    x1 = _layer_norm(ALPHA * x + g1_ref[0] * out, lng_ref[...], lnb_ref[...])
    x1_ref[...] = x1
    u2 = x1 * (1.0 + sc2_ref[0]) + sh2_ref[0]
    u2_ref[...] = u2.astype(BF16)

    u_hi, u_lo = _split_bf16(u2)
    w_hi, w_lo = _split_bf16(wr_ref[...])
    logits = _dot(u_hi, w_hi) + (_dot(u_hi, w_lo) + _dot(u_lo, w_hi)) + br_ref[...]
    lane = lax.broadcasted_iota(jnp.int32, logits.shape, 1).astype(F32)
    neg = jnp.float32(-jnp.inf)
    is_g = lane < N_GROUPS
    gl = jnp.where(is_g, logits, neg)
    gmax = jnp.max(gl, axis=-1, keepdims=True)
    gsel = jnp.min(jnp.where(gl == gmax, lane, float(LANES)), axis=-1, keepdims=True)
    p_g = 1.0 / jnp.sum(jnp.where(is_g, jnp.exp(gl - gmax), 0.0), axis=-1, keepdims=True)
    lo = ROUTER_LANE0 + gsel * EXPERTS_PER_GROUP
    el = jnp.where((lane >= lo) & (lane < lo + EXPERTS_PER_GROUP), logits, neg)
    v1 = jnp.max(el, axis=-1, keepdims=True)
    i1 = jnp.min(jnp.where(el == v1, lane, float(LANES)), axis=-1, keepdims=True)
    el2 = jnp.where(lane == i1, neg, el)
    v2 = jnp.max(el2, axis=-1, keepdims=True)
    i2 = jnp.min(jnp.where(el2 == v2, lane, float(LANES)), axis=-1, keepdims=True)
    t = jnp.exp(v2 - v1)
    w1 = p_g / (1.0 + t)
    w2 = w1 * t

    oh1 = (lane == i1).astype(F32)
    oh2 = (lane == i2).astype(F32)
    oh = oh1 + oh2
    r_i = lax.broadcasted_iota(jnp.int32, (TM, TM), 0)
    c_i = lax.broadcasted_iota(jnp.int32, (TM, TM), 1)
    before = (c_i < r_i).astype(BF16)
    earlier = _dot(before, oh.astype(BF16))
    rank1 = jnp.sum(earlier * oh1, axis=-1, keepdims=True)
    rank2 = jnp.sum(earlier * oh2, axis=-1, keepdims=True)
    cnt_ref[0] = jnp.sum(oh, axis=0, keepdims=True)
    cols = (i1, i2, w1, w2, rank1, rank2)
    meta = jnp.zeros_like(logits)
    for k, col in enumerate(cols):
        meta = jnp.where(lane == k, col, meta)
    meta_ref[...] = meta


META_E1, META_E2, META_W1, META_W2, META_RANK1, META_RANK2 = range(6)

_TAIL_OUT_SHAPES = [
    jax.ShapeDtypeStruct((T, D), F32),
    jax.ShapeDtypeStruct((T, D), BF16),
    jax.ShapeDtypeStruct((T, LANES), F32),
    jax.ShapeDtypeStruct((N_TILES, 1, LANES), F32),
]


def _tail_out_specs():
    return [
        pl.BlockSpec((TM, D), lambda i: (i, 0)),
        pl.BlockSpec((TM, D), lambda i: (i, 0)),
        pl.BlockSpec((TM, LANES), lambda i: (i, 0)),
        pl.BlockSpec((1, 1, LANES), lambda i: (i, 0, 0)),
    ]


def _tail_in_specs(li):
    return [
        _mod_spec(li, 2), _mod_spec(li, 4), _mod_spec(li, 3),
        _row_spec((1, D)), _row_spec((1, D)),
        _row_spec((D, LANES)), _row_spec((1, LANES)),
    ]


def _out_proj_kernel(rp_ref, rs_ref, op_ref, os_ref, w_ref, xp_ref, xs_ref, *rest):
    half = HEADS * HEAD_W
    r = _pick_tile(rp_ref, rs_ref).astype(BF16)
    o = _pick_tile(op_ref, os_ref).astype(BF16)
    out = _dot(r, w_ref[0:half, :]) + _dot(o, w_ref[half:, :])
    _mixer_tail(out, _pick_tile(xp_ref, xs_ref), *rest)


def _out_proj_tail(r_p, r_s, o_p, o_s, w_out_bf16, x_prompt, x_sample, modr, ln_g, ln_b, wr, br, li):
    half = HEADS * HEAD_W
    return pl.pallas_call(
        _out_proj_kernel,
        grid=(N_TILES,),
        in_specs=[_prompt_tile_spec(half), _sample_tile_spec(half),
                  _prompt_tile_spec(half), _sample_tile_spec(half),
                  _row_spec((2 * half, D)),
                  _prompt_tile_spec(D), _sample_tile_spec(D)] + _tail_in_specs(li),
        out_specs=_tail_out_specs(),
        out_shape=_TAIL_OUT_SHAPES,
        compiler_params=_cparams(("arbitrary",)),
        name="out_proj_tail",
    )(r_p, r_s, o_p, o_s, w_out_bf16, x_prompt, x_sample, modr, modr, modr, ln_g, ln_b, wr, br)


def _conv_glu_kernel(x_ref, sh_ref, sc_ref, w_ref, b_ref, o_ref):
    u = x_ref[...] * (1.0 + sc_ref[0]) + sh_ref[0]
    h = _dot(u.astype(BF16), w_ref[...]) + b_ref[...]
    o_ref[...] = h[:, :D] * jax.nn.sigmoid(h[:, D:])


def _conv_glu(x, modr, w1_bf16, b1, li):
    return pl.pallas_call(
        _conv_glu_kernel,
        grid=(N_TILES,),
        in_specs=[pl.BlockSpec((TM, D), lambda i: (i, 0)), _mod_spec(li, 0), _mod_spec(li, 1),
                  _row_spec((D, 2 * D)), _row_spec((1, 2 * D))],
        out_specs=pl.BlockSpec((TM, D), lambda i: (i, 0)),
        out_shape=jax.ShapeDtypeStruct((T, D), F32),
        compiler_params=_cparams(("arbitrary",)),
        name="conv_glu",
    )(x, modr, modr, w1_bf16, b1)


HALO = 16
CONV_ROWS = 64
CONV_COLS = 128


def _depthwise_conv(hp, dw_ref, conv):
    base = HALO - CONV_PAD
    for cb in range(D // CONV_COLS):
        cs = slice(cb * CONV_COLS, (cb + 1) * CONV_COLS)
        for rb in range(TM // CONV_ROWS):
            r0 = rb * CONV_ROWS
            acc = None
            for shift in range(SUBLANES):
                part = None
                for tap in range(CONV_K):
                    off = base + tap
                    if off % SUBLANES != shift:
                        continue
                    a0 = r0 + off - shift
                    term = hp[a0:a0 + CONV_ROWS + SUBLANES, cs] * dw_ref[tap:tap + 1, cs]
                    part = term if part is None else part + term
                part = part[shift:shift + CONV_ROWS, :]
                acc = part if acc is None else acc + part
            conv[r0:r0 + CONV_ROWS, cs] = acc


def _conv_tail_kernel(cur_ref, prev_ref, next_ref, dw_ref, dwb_ref, cg_ref, cb_ref, w2_ref, b2_ref,
                      x_ref, *rest):
    tail_args, hp, conv = rest[:-2], rest[-2], rest[-1]
    i = pl.program_id(0)
    k = (i - PROMPT_TILES) % SAMPLE_TILES_PER_SEQ
    in_sample = i >= PROMPT_TILES
    left_ok = jnp.logical_and(in_sample, k != 0)
    right_ok = jnp.logical_and(in_sample, k != SAMPLE_TILES_PER_SEQ - 1)
    hp[0:HALO, :] = jnp.where(left_ok, prev_ref[...], 0.0)
    hp[HALO:HALO + TM, :] = cur_ref[...]
    hp[HALO + TM:HALO + TM + HALO, :] = jnp.where(right_ok, next_ref[...], 0.0)
    _depthwise_conv(hp, dw_ref, conv)
    hc = _silu(_layer_norm(conv[...] + dwb_ref[...], cg_ref[...], cb_ref[...]))
    out = _dot(hc.astype(BF16), w2_ref[...]) + b2_ref[...]
    _mixer_tail(out, x_ref[...], *tail_args)


def _conv_tail(glu, dw, dwb, cg, cb, w2_bf16, b2, x, modr, ln_g, ln_b, wr, br, li):
    per = TM // HALO
    last = T // HALO - 1
    return pl.pallas_call(
        _conv_tail_kernel,
        grid=(N_TILES,),
        in_specs=[pl.BlockSpec((TM, D), lambda i: (i, 0)),
                  pl.BlockSpec((HALO, D), lambda i: (jnp.maximum(i * per - 1, 0), 0)),
                  pl.BlockSpec((HALO, D), lambda i: (jnp.minimum((i + 1) * per, last), 0)),
                  _row_spec((CONV_K, D)), _row_spec((1, D)), _row_spec((1, D)), _row_spec((1, D)),
                  _row_spec((D, D)), _row_spec((1, D)),
                  pl.BlockSpec((TM, D), lambda i: (i, 0))] + _tail_in_specs(li),
        out_specs=_tail_out_specs(),
        out_shape=_TAIL_OUT_SHAPES,
        scratch_shapes=[pltpu.VMEM((TM + 2 * HALO, D), F32), pltpu.VMEM((TM, D), F32)],
        compiler_params=_cparams(("arbitrary",)),
        name="conv_tail",
    )(glu, glu, glu, dw, dwb, cg, cb, w2_bf16, b2, x, modr, modr, modr, ln_g, ln_b, wr, br)


def _sorted_positions(meta, srcv):
    lane = lax.broadcasted_iota(jnp.int32, meta.shape, 1).astype(F32)

    def pos(e_col, r_col):
        start = jnp.sum(jnp.where(lane == meta[:, e_col:e_col + 1], srcv, 0.0), axis=-1, keepdims=True)
        return start + meta[:, r_col:r_col + 1]

    return pos(META_E1, META_RANK1), pos(META_E2, META_RANK2)


def _one_hot_rows(pos):
    col = lax.broadcasted_iota(jnp.int32, (TM, SORT_ROWS), 1).astype(F32)
    return col == pos


def _for_each_run_piece(tile, src_ref, dst_ref, n_ref, fn):
    for e in range(N_EXPERTS):
        k = tile * N_EXPERTS + e
        n, src, dst = n_ref[k], src_ref[k], dst_ref[k]
        done = 0
        for bit in RUN_BITS:
            piece = n & bit

            @pl.when(piece != 0)
            def _(done=done, bit=bit):
                fn(pl.multiple_of(src + done, RUN_ALIGN), pl.multiple_of(dst + done, RUN_ALIGN), bit)

            done = done + piece


def _wait_rows(total, make_copy):
    for bit in RUN_BITS:
        @pl.when((total & bit) != 0)
        def _(bit=bit):
            make_copy(bit).wait()


def _dispatch_kernel(src_ref, dst_ref, n_ref, tot_ref, u_ref, meta_ref, srcv_ref, xs_ref, sorted_ref, sems):
    i = pl.program_id(0)
    slot = i % 2

    def wait_tile(tile, slot):
        buf = sorted_ref.at[slot]
        _wait_rows(tot_ref[tile], lambda rows: pltpu.make_async_copy(
            buf.at[pl.ds(0, rows)], xs_ref.at[pl.ds(0, rows)], sems.at[slot]))

    @pl.when(i >= 2)
    def _():
        wait_tile(i - 2, slot)

    pos1, pos2 = _sorted_positions(meta_ref[...], srcv_ref[0])
    select = jnp.logical_or(_one_hot_rows(pos1), _one_hot_rows(pos2)).astype(BF16)
    sorted_ref[slot] = _dot_tn(select, u_ref[...])
    buf = sorted_ref.at[slot]

    def start(src, dst, rows):
        pltpu.make_async_copy(buf.at[pl.ds(src, rows)], xs_ref.at[pl.ds(dst, rows)], sems.at[slot]).start()

    _for_each_run_piece(i, src_ref, dst_ref, n_ref, start)

    @pl.when(i == N_TILES - 1)
    def _():
        wait_tile(i - 1, 1 - slot)
        wait_tile(i, slot)


def _dispatch(sched, u2, meta, srcv):
    return pl.pallas_call(
        _dispatch_kernel,
        grid_spec=pltpu.PrefetchScalarGridSpec(
            num_scalar_prefetch=4,
            grid=(N_TILES,),
            in_specs=[pl.BlockSpec((TM, D), lambda i, *_: (i, 0)),
                      pl.BlockSpec((TM, LANES), lambda i, *_: (i, 0)),
                      pl.BlockSpec((1, 1, LANES), lambda i, *_: (i, 0, 0))],
            out_specs=pl.BlockSpec(memory_space=pl.ANY),
            scratch_shapes=[pltpu.VMEM((2, SORT_ROWS, D), F32), pltpu.SemaphoreType.DMA((2,))],
        ),
        out_shape=jax.ShapeDtypeStruct((MOE_ROWS, D), F32),
        compiler_params=_cparams(("arbitrary",)),
        name="moe_dispatch",
    )(*sched, u2, meta, srcv)


def _experts_kernel(start_ref, chunks_ref, xs_ref, wg_ref, wu_ref, wd_ref, ys_ref,
                    wg_bf, wu_bf, wd_bf, xbuf, ybuf, in_sems, out_sems):
    e = pl.program_id(0)
    n = chunks_ref[e]
    base = start_ref[e]

    def rows(k):
        return pl.ds(pl.multiple_of(base + k * MOE_TM, MOE_TM), MOE_TM)

    def load(k, slot):
        return pltpu.make_async_copy(xs_ref.at[rows(k)], xbuf.at[slot], in_sems.at[slot])

    def store(k, slot):
        return pltpu.make_async_copy(ybuf.at[slot], ys_ref.at[rows(k)], out_sems.at[slot])

    @pl.when(n > 0)
    def _():
        load(0, 0).start()
        wg_bf[...] = wg_ref[0, 0].astype(BF16)
        wu_bf[...] = wu_ref[0, 0].astype(BF16)
        wd_bf[...] = wd_ref[0, 0].astype(BF16)

        def chunk(k, carry):
            slot = k % 2
            load(k, slot).wait()

            @pl.when(k + 1 < n)
            def _():
                load(k + 1, 1 - slot).start()

            @pl.when(k >= 2)
            def _():
                store(k - 2, slot).wait()

            x = xbuf[slot].astype(BF16)
            h = (_silu(_dot(x, wg_bf[...])) * _dot(x, wu_bf[...])).astype(BF16)
            ybuf[slot] = _dot(h, wd_bf[...])
            store(k, slot).start()
            return carry

        lax.fori_loop(0, n, chunk, 0)

        @pl.when(n >= 2)
        def _():
            store(n - 2, n % 2).wait()

        store(n - 1, (n - 1) % 2).wait()


def _experts(seg_start, seg_chunks, xs, w_gate, w_up, w_down, li):
    def weight(shape):
        return pl.BlockSpec((1, 1) + shape, lambda e, *_: (li, e, 0, 0))

    return pl.pallas_call(
        _experts_kernel,
        grid_spec=pltpu.PrefetchScalarGridSpec(
            num_scalar_prefetch=2,
            grid=(N_EXPERTS,),
            in_specs=[pl.BlockSpec(memory_space=pl.ANY),
                      weight((D, D_EXPERT)), weight((D, D_EXPERT)), weight((D_EXPERT, D))],
            out_specs=pl.BlockSpec(memory_space=pl.ANY),
            scratch_shapes=[pltpu.VMEM((D, D_EXPERT), BF16), pltpu.VMEM((D, D_EXPERT), BF16),
                            pltpu.VMEM((D_EXPERT, D), BF16),
                            pltpu.VMEM((2, MOE_TM, D), F32), pltpu.VMEM((2, MOE_TM, D), F32),
                            pltpu.SemaphoreType.DMA((2,)), pltpu.SemaphoreType.DMA((2,))],
        ),
        out_shape=jax.ShapeDtypeStruct((MOE_ROWS, D), F32),
        compiler_params=_cparams(("arbitrary",)),
        name="moe_experts",
    )(seg_start, seg_chunks, xs, w_gate, w_up, w_down)


def _combine_kernel(src_ref, dst_ref, n_ref, tot_ref, ys_ref, x1_ref, meta_ref, srcv_ref, g2_ref,
                    lng_ref, lnb_ref, *rest, split):
    outs, (sorted_ref, sems) = rest[:-2], rest[-2:]
    i = pl.program_id(0)
    slot = i % 2

    def fetch(tile, slot):
        buf = sorted_ref.at[slot]

        def start(src, dst, rows):
            pltpu.make_async_copy(ys_ref.at[pl.ds(dst, rows)], buf.at[pl.ds(src, rows)], sems.at[slot]).start()

        _for_each_run_piece(tile, src_ref, dst_ref, n_ref, start)

    @pl.when(i == 0)
    def _():
        sorted_ref[...] = jnp.zeros_like(sorted_ref)
        fetch(0, 0)

    @pl.when(i + 1 < N_TILES)
    def _():
        fetch(i + 1, 1 - slot)

    meta = meta_ref[...]
    pos1, pos2 = _sorted_positions(meta, srcv_ref[0])
    sel1 = _one_hot_rows(pos1).astype(BF16)
    sel2 = _one_hot_rows(pos2).astype(BF16)
    buf = sorted_ref.at[slot]
    _wait_rows(tot_ref[i], lambda rows: pltpu.make_async_copy(
        ys_ref.at[pl.ds(0, rows)], buf.at[pl.ds(0, rows)], sems.at[slot]))
    ysort = sorted_ref[slot].astype(BF16)
    f = (meta[:, META_W1:META_W1 + 1] * _dot(sel1, ysort)
         + meta[:, META_W2:META_W2 + 1] * _dot(sel2, ysort))
    y = _layer_norm(ALPHA * x1_ref[...] + g2_ref[0] * f, lng_ref[...], lnb_ref[...])
    if split:
        @pl.when(i < PROMPT_TILES)
        def _():
            outs[0][...] = y

        @pl.when(i >= PROMPT_TILES)
        def _():
            outs[1][...] = y
    else:
        outs[0][...] = y


def _combine(sched, ys, x1, meta, srcv, modr, ln_g, ln_b, li, split):
    if split:
        out_specs = [_prompt_tile_spec(D), _sample_tile_spec(D)]
        out_shape = [jax.ShapeDtypeStruct((T_PROMPT, D), F32), jax.ShapeDtypeStruct((T_SAMPLE, D), F32)]
    else:
        out_specs = [pl.BlockSpec((TM, D), lambda i, *_: (i, 0))]
        out_shape = [jax.ShapeDtypeStruct((T, D), F32)]
    return pl.pallas_call(
        functools.partial(_combine_kernel, split=split),
        grid_spec=pltpu.PrefetchScalarGridSpec(
            num_scalar_prefetch=4,
            grid=(N_TILES,),
            in_specs=[pl.BlockSpec(memory_space=pl.ANY),
                      pl.BlockSpec((TM, D), lambda i, *_: (i, 0)),
                      pl.BlockSpec((TM, LANES), lambda i, *_: (i, 0)),
                      pl.BlockSpec((1, 1, LANES), lambda i, *_: (i, 0, 0)),
                      _mod_spec(li, 5), _row_spec((1, D)), _row_spec((1, D))],
            out_specs=out_specs,
            scratch_shapes=[pltpu.VMEM((2, SORT_ROWS, D), F32), pltpu.SemaphoreType.DMA((2,))],
        ),
        out_shape=out_shape,
        compiler_params=_cparams(("arbitrary",)),
        name="moe_combine",
    )(*sched, ys, x1, meta, srcv, modr, ln_g, ln_b)


def _moe_schedule(tile_counts):
    n = (tile_counts + RUN_ALIGN - 1) // RUN_ALIGN * RUN_ALIGN
    src = jnp.cumsum(n, axis=1) - n
    per_expert = jnp.sum(n, axis=0)
    seg = (per_expert + MOE_TM - 1) // MOE_TM * MOE_TM
    seg_start = jnp.cumsum(seg) - seg
    dst = seg_start[None, :] + jnp.cumsum(n, axis=0) - n
    runs = (src.reshape(-1), dst.reshape(-1), n.reshape(-1), jnp.sum(n, axis=1))
    srcv = jnp.pad(src.astype(F32), ((0, 0), (ROUTER_LANE0, LANES - ROUTER_LANE0 - N_EXPERTS)))
    return runs, srcv.reshape(N_TILES, 1, LANES), (seg_start, seg // MOE_TM)


def _moe(x1, u2, meta, cnt, modr, w_gate, w_up, w_down, ln_g, ln_b, li, split):
    tile_counts = cnt[:, 0, ROUTER_LANE0:ROUTER_LANE0 + N_EXPERTS].astype(jnp.int32)
    runs, srcv, (seg_start, seg_chunks) = _moe_schedule(tile_counts)
    xs = _dispatch(runs, u2, meta, srcv)
    ys = _experts(seg_start, seg_chunks, xs, w_gate, w_up, w_down, li)
    return _combine(runs, ys, x1, meta, srcv, modr, ln_g, ln_b, li, split)


def _router_slab(wg, bg, we, be):
    w = jnp.concatenate([wg, we.transpose(1, 0, 2).reshape(D, N_EXPERTS)], axis=1)
    b = jnp.concatenate([bg, be.reshape(N_EXPERTS)])
    pad = LANES - w.shape[1]
    return jnp.pad(w, ((0, 0), (0, pad))), jnp.pad(b, (0, pad)).reshape(1, LANES)


def kernel(x_prompt, x_sample, cache_diff_k, cache_diff_v, state_ret_fwd, state_ret_bwd, c, c_ctx, mod_w, mod_b, ln1_g, ln1_b, ln2_g, ln2_b, mix_w_in, mix_w_out, ret_decay_fwd, ret_decay_bwd, diff_lq1, diff_lk1, diff_lq2, diff_lk2, diff_subln_g, conv_w1, conv_b1, conv_dw, conv_dw_b, conv_ln_g, conv_ln_b, conv_w2, conv_b2, router_g_w, router_g_b, router_e_w, router_e_b, moe_w_gate, moe_w_up, moe_w_down):
    xp = x_prompt.reshape(T_PROMPT, D)
    xs = x_sample.reshape(T_SAMPLE, D)
    cond = jnp.concatenate([c_ctx[None, :], c, jnp.zeros((MOD_ROWS - 1 - DEC_BATCH, D), F32)], axis=0)
    modr = _mod_vectors(cond, mod_w, mod_b).reshape(DEPTH * MOD_ROWS * 6, 1, D)
    cos, sin_signed = _rope_tables()

    def row(v):
        return v.reshape(1, -1)

    x = None
    caches = None
    for li in range(DEPTH):
        wr, br = _router_slab(router_g_w[li], router_g_b[li], router_e_w[li], router_e_b[li])
        if li % 2 == 0:
            assert li == 0, "the even mixer reads the kernel inputs directly"
            e = li // 2
            lam_init = 0.8 - 0.6 * math.exp(-0.3 * li)
            proj, ck, cv = _in_proj(xp, xs, modr, mix_w_in[e].astype(BF16), li)
            dec = jnp.concatenate([ret_decay_fwd[e], ret_decay_bwd[e]])
            r_p, sf, sb = _retention(proj, dec, BATCH, SEQ, 0, emit_state=True)
            (r_s,) = _retention(proj, dec, DEC_BATCH, DEC_SEQ, T_PROMPT // DEC_SEQ,
                                s0f=state_ret_fwd, s0b=state_ret_bwd, e=e)
            lams = (row(diff_lq1[e]), row(diff_lk1[e]), row(diff_lq2[e]), row(diff_lk2[e]),
                    row(diff_subln_g[e]))
            o_p = _attn_prompt(proj, *lams, lam_init)
            o_s = _attn_sample(proj, cache_diff_k, cache_diff_v, cos, sin_signed, *lams, lam_init, e)
            x1, u2, meta, cnt = _out_proj_tail(r_p, r_s, o_p, o_s, mix_w_out[e].astype(BF16), xp, xs, modr,
                                               row(ln1_g[li]), row(ln1_b[li]), wr, br, li)
            caches = (ck, cv, sf, sb)
        else:
            o = li // 2
            glu = _conv_glu(x, modr, conv_w1[o].astype(BF16), row(conv_b1[o]), li)
            x1, u2, meta, cnt = _conv_tail(glu, conv_dw[o], row(conv_dw_b[o]), row(conv_ln_g[o]),
                                           row(conv_ln_b[o]), conv_w2[o].astype(BF16), row(conv_b2[o]),
                                           x, modr, row(ln1_g[li]), row(ln1_b[li]), wr, br, li)
        outs = _moe(x1, u2, meta, cnt, modr, moe_w_gate, moe_w_up, moe_w_down,
                    row(ln2_g[li]), row(ln2_b[li]), li, split=(li == DEPTH - 1))
        x = outs[0]

    y_prompt = outs[0].reshape(BATCH, SEQ, D)
    y_sample = outs[1].reshape(DEC_BATCH, DEC_SEQ, D)
    return (y_prompt, y_sample) + caches
```

```python
import functools
import math

import numpy as np
import jax
import jax.numpy as jnp
from jax import lax
from jax.experimental import pallas as pl
from jax.experimental.pallas import tpu as pltpu

F32 = jnp.float32
BF16 = jnp.bfloat16

D = 1024
BATCH = 16
SEQ = 256
DEPTH = 2
DEC_BATCH = 2
DEC_SEQ = 2048
PAST_LEN = 512
GRID_W = 64
HEADS = 4
HEAD_W = 128
RET_CHUNK = 128
DIFF_DK = 64
ROPE_THETA = 10000.0
IN_W = 7 * HEADS * HEAD_W
CONV_K = 31
CONV_PAD = CONV_K // 2
N_GROUPS = 4
EXPERTS_PER_GROUP = 8
N_EXPERTS = N_GROUPS * EXPERTS_PER_GROUP
D_EXPERT = 512
ALPHA = (2.0 * DEPTH) ** 0.25
LN_EPS = 1e-5
GN_EPS = 1e-6

T_PROMPT = BATCH * SEQ
T_SAMPLE = DEC_BATCH * DEC_SEQ
T = T_PROMPT + T_SAMPLE
TM = 256
N_TILES = T // TM
PROMPT_TILES = T_PROMPT // TM
SAMPLE_TILES_PER_SEQ = DEC_SEQ // TM
MOD_ROWS = 8
MOE_TM = 256
LANES = 128
SUBLANES = 8
RUN_ALIGN = SUBLANES
SORT_ROWS = -(-(2 * TM + N_EXPERTS * (RUN_ALIGN - 1)) // TM) * TM
RUN_BITS = tuple(1 << b for b in range((2 * TM).bit_length() - 1, RUN_ALIGN.bit_length() - 2, -1))
MOE_MAX_TILES = -(-(2 * T + N_TILES * N_EXPERTS * (RUN_ALIGN - 1) + N_EXPERTS * (MOE_TM - RUN_ALIGN)) // MOE_TM)
MOE_ROWS = MOE_MAX_TILES * MOE_TM
MOE_AHEAD = 2
MOE_IN_SLOTS = MOE_AHEAD + 1
MOE_OUT_SLOTS = 2
ROUTER_LANE0 = N_GROUPS
VMEM_LIMIT = 52 * 1024 * 1024


def _cparams(sem):
    return pltpu.CompilerParams(dimension_semantics=sem, vmem_limit_bytes=VMEM_LIMIT)


def _tile_cond_row(i):
    return jnp.where(i < PROMPT_TILES, 0, 1 + (i - PROMPT_TILES) // SAMPLE_TILES_PER_SEQ)


def _mod_spec(li, k):
    return pl.BlockSpec((1, 1, D), lambda i, *_: ((li * MOD_ROWS + _tile_cond_row(i)) * 6 + k, 0, 0))


def _row_spec(shape):
    return pl.BlockSpec(shape, lambda i, *_: (0,) * len(shape))


def _layer_norm(x, g, b):
    mu = jnp.mean(x, axis=-1, keepdims=True)
    xc = x - mu
    var = jnp.mean(xc * xc, axis=-1, keepdims=True)
    return xc * lax.rsqrt(var + LN_EPS) * g + b


def _silu(x):
    return x * jax.nn.sigmoid(x)


def _dot(a, b):
    return jnp.dot(a, b, preferred_element_type=F32)


def _dot_nt(a, b):
    return lax.dot_general(a, b, (((1,), (1,)), ((), ())), preferred_element_type=F32)


def _dot_tn(a, b):
    return lax.dot_general(a, b, (((0,), (0,)), ((), ())), preferred_element_type=F32)


MOD_TN = 512


def _mod_kernel(cond_ref, w_ref, b_ref, o_ref):
    s = _silu(cond_ref[...])
    o_ref[0] = jnp.dot(s, w_ref[0], precision=lax.Precision.HIGHEST,
                       preferred_element_type=F32) + b_ref[0]


def _mod_vectors(cond, mod_w, mod_b):
    return pl.pallas_call(
        _mod_kernel,
        grid=(DEPTH, 6 * D // MOD_TN),
        in_specs=[
            pl.BlockSpec((MOD_ROWS, D), lambda l, j: (0, 0)),
            pl.BlockSpec((1, D, MOD_TN), lambda l, j: (l, 0, j)),
            pl.BlockSpec((1, 1, MOD_TN), lambda l, j: (l, 0, j)),
        ],
        out_specs=pl.BlockSpec((1, MOD_ROWS, MOD_TN), lambda l, j: (l, 0, j)),
        out_shape=jax.ShapeDtypeStruct((DEPTH, MOD_ROWS, 6 * D), F32),
        compiler_params=_cparams(("arbitrary", "arbitrary")),
        name="mod_vectors",
    )(cond, mod_w, mod_b.reshape(DEPTH, 1, 6 * D))


def _prompt_tile_spec(width):
    return pl.BlockSpec((TM, width), lambda i, *_: (jnp.minimum(i, PROMPT_TILES - 1), 0))


def _sample_tile_spec(width):
    return pl.BlockSpec((TM, width), lambda i, *_: (jnp.maximum(i - PROMPT_TILES, 0), 0))


def _pick_tile(prompt_ref, sample_ref):
    return jnp.where(pl.program_id(0) < PROMPT_TILES, prompt_ref[...], sample_ref[...])


def _in_proj_kernel(xp_ref, xs_ref, sh_ref, sc_ref, w_ref, o_ref, ck_ref, cv_ref):
    u = _pick_tile(xp_ref, xs_ref) * (1.0 + sc_ref[0]) + sh_ref[0]
    proj = _dot(u.astype(BF16), w_ref[...])
    o_ref[...] = proj

    @pl.when(pl.program_id(0) < PROMPT_TILES)
    def _():
        for h in range(HEADS):
            ck_ref[0, 0, h] = proj[:, (COL_KD + h) * HEAD_W:(COL_KD + h + 1) * HEAD_W]
            cv_ref[0, 0, h] = proj[:, (COL_VD + h) * HEAD_W:(COL_VD + h + 1) * HEAD_W]


def _in_proj(x_prompt, x_sample, modr, w_in_bf16, li):
    cache_spec = pl.BlockSpec((1, 1, HEADS, SEQ, HEAD_W),
                              lambda i: (jnp.minimum(i, PROMPT_TILES - 1), 0, 0, 0, 0))
    cache_shape = jax.ShapeDtypeStruct((BATCH, 1, HEADS, SEQ, HEAD_W), F32)
    return pl.pallas_call(
        _in_proj_kernel,
        grid=(N_TILES,),
        in_specs=[
            _prompt_tile_spec(D), _sample_tile_spec(D),
            _mod_spec(li, 0),
            _mod_spec(li, 1),
            _row_spec((D, IN_W)),
        ],
        out_specs=[pl.BlockSpec((TM, IN_W), lambda i: (i, 0)), cache_spec, cache_spec],
        out_shape=[jax.ShapeDtypeStruct((T, IN_W), F32), cache_shape, cache_shape],
        compiler_params=_cparams(("arbitrary",)),
        name="in_proj",
    )(x_prompt, x_sample, modr, modr, w_in_bf16)


COL_QR, COL_KR, COL_VR, COL_GR, COL_QD, COL_KD, COL_VD = (k * HEADS for k in range(7))


def _retention_kernel(dec_ref, q_ref, k_ref, v_ref, g_ref, *rest, n_chunks, has_state, emit_state):
    rest = list(rest)
    if has_state:
        s0f_ref, s0b_ref = rest[:2]
        rest = rest[2:]
    r_ref = rest[0]
    rest = rest[1:]
    if emit_state:
        sf_ref, sb_ref = rest[:2]
        rest = rest[2:]
    of_ref = rest[0]

    h = pl.program_id(1)
    C = RET_CHUNK
    ii = lax.broadcasted_iota(jnp.int32, (C, C), 0)
    jj = lax.broadcasted_iota(jnp.int32, (C, C), 1)
    rel = (ii - jj).astype(F32)
    idx = lax.broadcasted_iota(jnp.int32, (C, 1), 0).astype(F32)
    k_scale = HEAD_W ** -0.5

    def chunk(ref, c):
        return ref[c * C:(c + 1) * C, :]

    def run(direction):
        lg = -jnp.exp(jnp.full((1, 1), dec_ref[direction * HEADS + h], F32))
        if direction == 0:
            inner = jnp.where(rel >= 0, jnp.exp(jnp.maximum(rel, 0.0) * lg), 0.0)
            q_decay = jnp.exp((idx + 1.0) * lg)
            k_decay = jnp.exp((C - 1.0 - idx) * lg)
            order = range(n_chunks)
        else:
            inner = jnp.where(rel <= 0, jnp.exp(jnp.maximum(-rel, 0.0) * lg), 0.0)
            q_decay = jnp.exp((C - idx) * lg)
            k_decay = jnp.exp(idx * lg)
            order = range(n_chunks - 1, -1, -1)
        chunk_decay = jnp.exp(C * lg)
        if has_state:
            s = (s0f_ref if direction == 0 else s0b_ref)[0, 0, 0]
        else:
            s = jnp.zeros((HEAD_W, HEAD_W), F32)
        for c in order:
            qc = chunk(q_ref, c)
            kc = chunk(k_ref, c) * k_scale
            vc = chunk(v_ref, c).astype(BF16)
            scores = _dot_nt(qc.astype(BF16), kc.astype(BF16)) * inner
            o = _dot(scores.astype(BF16), vc) + _dot((qc * q_decay).astype(BF16), s.astype(BF16))
            s = s * chunk_decay + _dot_tn((kc * k_decay).astype(BF16), vc)
            if direction == 0:
                of_ref[c * C:(c + 1) * C, :] = o
            else:
                r = of_ref[c * C:(c + 1) * C, :] + o
                mu = jnp.mean(r, axis=-1, keepdims=True)
                rc = r - mu
                var = jnp.mean(rc * rc, axis=-1, keepdims=True)
                rn = rc * lax.rsqrt(var + GN_EPS)
                r_ref[c * C:(c + 1) * C, :] = _silu(chunk(g_ref, c)) * rn
        return s

    sf = run(0)
    sb = run(1)
    if emit_state:
        sf_ref[0, 0, 0] = sf
        sb_ref[0, 0, 0] = sb


def _retention(proj, dec, n_seq, seq_len, row_block0, s0f=None, s0b=None, e=0, emit_state=False):
    has_state = s0f is not None

    def col(base):
        return pl.BlockSpec((seq_len, HEAD_W), lambda b, h, *_: (row_block0 + b, base + h))

    state_spec = pl.BlockSpec((1, 1, 1, HEAD_W, HEAD_W), lambda b, h, *_: (b, e, h, 0, 0))
    in_specs = [pl.BlockSpec(memory_space=pltpu.SMEM), col(COL_QR), col(COL_KR), col(COL_VR), col(COL_GR)]
    args = [dec, proj, proj, proj, proj]
    if has_state:
        in_specs += [state_spec, state_spec]
        args += [s0f, s0b]
    out_specs = [pl.BlockSpec((seq_len, HEAD_W), lambda b, h, *_: (b, h))]
    out_shape = [jax.ShapeDtypeStruct((n_seq * seq_len, HEADS * HEAD_W), F32)]
    if emit_state:
        st = pl.BlockSpec((1, 1, 1, HEAD_W, HEAD_W), lambda b, h, *_: (b, 0, h, 0, 0))
        out_specs += [st, st]
        out_shape += [jax.ShapeDtypeStruct((n_seq, 1, HEADS, HEAD_W, HEAD_W), F32)] * 2
    return pl.pallas_call(
        functools.partial(_retention_kernel, n_chunks=seq_len // RET_CHUNK,
                          has_state=has_state, emit_state=emit_state),
        grid=(n_seq, HEADS),
        in_specs=in_specs,
        out_specs=out_specs,
        out_shape=out_shape,
        scratch_shapes=[pltpu.VMEM((seq_len, HEAD_W), F32)],
        compiler_params=_cparams(("arbitrary", "arbitrary")),
        name=f"retention_{seq_len}",
    )(*args)


def _diff_lambda(lq1_ref, lk1_ref, lq2_ref, lk2_ref, lam_init):
    a = jnp.sum(lq1_ref[...] * lk1_ref[...], axis=-1, keepdims=True)
    b = jnp.sum(lq2_ref[...] * lk2_ref[...], axis=-1, keepdims=True)
    return jnp.exp(a) - jnp.exp(b) + lam_init


LOG2E = 1.4426950408889634


def _diff_attend(q, k, v, lam, subln_g, lam_init):
    lane = lax.broadcasted_iota(jnp.int32, q.shape, 1)
    q1 = jnp.where(lane < DIFF_DK, q, 0.0).astype(BF16)
    q2 = jnp.where(lane >= DIFF_DK, q, 0.0).astype(BF16)

    def softmax_times_v(qz):
        s = _dot_nt(qz, k)
        p = jnp.exp2(s - jnp.max(s, axis=-1, keepdims=True))
        return _dot(p.astype(BF16), v) * (1.0 / jnp.sum(p, axis=-1, keepdims=True))

    o = softmax_times_v(q1) - lam * softmax_times_v(q2)
    o = o * lax.rsqrt(jnp.mean(o * o, axis=-1, keepdims=True) + LN_EPS)
    return o * subln_g * (1.0 - lam_init)


def _attn_prompt_kernel(q_ref, k_ref, v_ref, lq1, lk1, lq2, lk2, g_ref, o_ref, *, lam_init):
    lam = _diff_lambda(lq1, lk1, lq2, lk2, lam_init)
    scale = DIFF_DK ** -0.5 * LOG2E
    for h in range(HEADS):
        sl = slice(h * HEAD_W, (h + 1) * HEAD_W)
        o_ref[:, sl] = _diff_attend(q_ref[:, sl] * scale, k_ref[:, sl].astype(BF16),
                                    v_ref[:, sl].astype(BF16), lam, g_ref[...], lam_init)


def _attn_prompt(proj, lq1, lk1, lq2, lk2, subln_g, lam_init):
    W = HEADS * HEAD_W

    def slab(base):
        return pl.BlockSpec((SEQ, W), lambda b: (b, base // HEADS))

    small = _row_spec((1, DIFF_DK))
    return pl.pallas_call(
        functools.partial(_attn_prompt_kernel, lam_init=lam_init),
        grid=(BATCH,),
        in_specs=[slab(COL_QD), slab(COL_KD), slab(COL_VD), small, small, small, small,
                  _row_spec((1, HEAD_W))],
        out_specs=pl.BlockSpec((SEQ, W), lambda b: (b, 0)),
        out_shape=jax.ShapeDtypeStruct((T_PROMPT, W), F32),
        compiler_params=_cparams(("arbitrary",)),
        name="diff_attn_prompt",
    )(proj, proj, proj, lq1, lk1, lq2, lk2, subln_g)


def _rope(x, cos, sin_signed):
    lane = lax.broadcasted_iota(jnp.int32, x.shape, 1)
    partner = jnp.where((lane % 32) < 16, pltpu.roll(x, LANES - 16, 1), pltpu.roll(x, 16, 1))
    return x * cos + partner * sin_signed


def _attn_sample_kernel(q_ref, k_ref, v_ref, ck_ref, cv_ref, cosq_ref, sinq_ref, cos_ref, sin_ref,
                        lq1, lk1, lq2, lk2, g_ref, o_ref, kbuf, vbuf, *, lam_init):
    @pl.when(pl.program_id(2) == 0)
    def _():
        kbuf[0:DEC_SEQ, :] = _rope(k_ref[...], cos_ref[...], sin_ref[...]).astype(BF16)
        kbuf[DEC_SEQ:, :] = ck_ref[0, 0, 0].astype(BF16)
        vbuf[0:DEC_SEQ, :] = v_ref[...].astype(BF16)
        vbuf[DEC_SEQ:, :] = cv_ref[0, 0, 0].astype(BF16)

    lam = _diff_lambda(lq1, lk1, lq2, lk2, lam_init)
    q = _rope(q_ref[...], cosq_ref[...], sinq_ref[...]) * (DIFF_DK ** -0.5 * LOG2E)
    o_ref[...] = _diff_attend(q, kbuf[...], vbuf[...], lam, g_ref[...], lam_init)


ATTN_TQ = 256


def _attn_sample(proj, cache_k, cache_v, cos, sin_signed, lq1, lk1, lq2, lk2, subln_g, lam_init, e):
    nq = DEC_SEQ // ATTN_TQ
    row0_q = T_PROMPT // ATTN_TQ
    row0_kv = T_PROMPT // DEC_SEQ
    small = pl.BlockSpec((1, DIFF_DK), lambda b, h, t: (0, 0))
    cache = pl.BlockSpec((1, 1, 1, PAST_LEN, HEAD_W), lambda b, h, t: (b, e, h, 0, 0))
    table_q = pl.BlockSpec((ATTN_TQ, HEAD_W), lambda b, h, t: (t, 0))
    table = pl.BlockSpec((DEC_SEQ, HEAD_W), lambda b, h, t: (0, 0))
    return pl.pallas_call(
        functools.partial(_attn_sample_kernel, lam_init=lam_init),
        grid=(DEC_BATCH, HEADS, nq),
        in_specs=[
            pl.BlockSpec((ATTN_TQ, HEAD_W), lambda b, h, t: (row0_q + b * nq + t, COL_QD + h)),
            pl.BlockSpec((DEC_SEQ, HEAD_W), lambda b, h, t: (row0_kv + b, COL_KD + h)),
            pl.BlockSpec((DEC_SEQ, HEAD_W), lambda b, h, t: (row0_kv + b, COL_VD + h)),
            cache, cache, table_q, table_q, table, table,
            small, small, small, small,
            pl.BlockSpec((1, HEAD_W), lambda b, h, t: (0, 0)),
        ],
        out_specs=pl.BlockSpec((ATTN_TQ, HEAD_W), lambda b, h, t: (b * nq + t, h)),
        out_shape=jax.ShapeDtypeStruct((T_SAMPLE, HEADS * HEAD_W), F32),
        scratch_shapes=[pltpu.VMEM((DEC_SEQ + PAST_LEN, HEAD_W), BF16),
                        pltpu.VMEM((DEC_SEQ + PAST_LEN, HEAD_W), BF16)],
        compiler_params=_cparams(("arbitrary", "arbitrary", "arbitrary")),
        name="diff_attn_sample",
    )(proj, proj, proj, cache_k, cache_v, cos, sin_signed, cos, sin_signed,
      lq1, lk1, lq2, lk2, subln_g)


def _rope_tables():
    t = np.arange(DEC_SEQ)
    row, colp = t // GRID_W, t % GRID_W
    lane = np.arange(LANES)
    pos = np.where(((lane // 32) % 2 == 0)[None, :], row[:, None], colp[:, None]).astype(np.float64)
    half = 16
    inv = (np.float32(ROPE_THETA) ** (-(np.arange(half, dtype=np.float32)) / np.float32(half))).astype(np.float32)
    ang = pos.astype(np.float32) * inv[lane % half][None, :]
    cos = np.cos(ang.astype(np.float64)).astype(np.float32)
    sin = np.sin(ang.astype(np.float64)).astype(np.float32)
    sign = np.where((lane % 32) < half, -1.0, 1.0).astype(np.float32)[None, :]
    return jnp.asarray(cos), jnp.asarray(sin * sign)


def _split_bf16(a):
    hi = a.astype(BF16)
    return hi, (a - hi.astype(F32)).astype(BF16)


def _mixer_tail(out, x, g1_ref, sc2_ref, sh2_ref, lng_ref, lnb_ref, wr_ref, br_ref,
                x1_ref, u2_ref, meta_ref, cnt_ref):
    x1 = _layer_norm(ALPHA * x + g1_ref[0] * out, lng_ref[...], lnb_ref[...])
    x1_ref[...] = x1
    u2 = x1 * (1.0 + sc2_ref[0]) + sh2_ref[0]
    u2_ref[...] = u2.astype(BF16)

    u_hi, u_lo = _split_bf16(u2)
    w_hi, w_lo = _split_bf16(wr_ref[...])
    logits = _dot(u_hi, w_hi) + (_dot(u_hi, w_lo) + _dot(u_lo, w_hi)) + br_ref[...]
    lane = lax.broadcasted_iota(jnp.int32, logits.shape, 1).astype(F32)
    neg = jnp.float32(-jnp.inf)
    is_g = lane < N_GROUPS
    gl = jnp.where(is_g, logits, neg)
    gmax = jnp.max(gl, axis=-1, keepdims=True)
    gsel = jnp.min(jnp.where(gl == gmax, lane, float(LANES)), axis=-1, keepdims=True)
    p_g = 1.0 / jnp.sum(jnp.where(is_g, jnp.exp(gl - gmax), 0.0), axis=-1, keepdims=True)
    lo = ROUTER_LANE0 + gsel * EXPERTS_PER_GROUP
    el = jnp.where((lane >= lo) & (lane < lo + EXPERTS_PER_GROUP), logits, neg)
    v1 = jnp.max(el, axis=-1, keepdims=True)
    i1 = jnp.min(jnp.where(el == v1, lane, float(LANES)), axis=-1, keepdims=True)
    el2 = jnp.where(lane == i1, neg, el)
    v2 = jnp.max(el2, axis=-1, keepdims=True)
    i2 = jnp.min(jnp.where(el2 == v2, lane, float(LANES)), axis=-1, keepdims=True)
    t = jnp.exp(v2 - v1)
    w1 = p_g / (1.0 + t)
    w2 = w1 * t

    oh1 = (lane == i1).astype(F32)
    oh2 = (lane == i2).astype(F32)
    oh = oh1 + oh2
    r_i = lax.broadcasted_iota(jnp.int32, (TM, TM), 0)
    c_i = lax.broadcasted_iota(jnp.int32, (TM, TM), 1)
    before = (c_i < r_i).astype(BF16)
    earlier = _dot(before, oh.astype(BF16))
    rank1 = jnp.sum(earlier * oh1, axis=-1, keepdims=True)
    rank2 = jnp.sum(earlier * oh2, axis=-1, keepdims=True)
    cnt_ref[0] = jnp.sum(oh, axis=0, keepdims=True)
    cols = (i1, i2, w1, w2, rank1, rank2)
    meta = jnp.zeros_like(logits)
    for k, col in enumerate(cols):
        meta = jnp.where(lane == k, col, meta)
    meta_ref[...] = meta


META_E1, META_E2, META_W1, META_W2, META_RANK1, META_RANK2 = range(6)

_TAIL_OUT_SHAPES = [
    jax.ShapeDtypeStruct((T, D), F32),
    jax.ShapeDtypeStruct((T, D), BF16),
    jax.ShapeDtypeStruct((T, LANES), F32),
    jax.ShapeDtypeStruct((N_TILES, 1, LANES), F32),
]


def _tail_out_specs():
    return [
        pl.BlockSpec((TM, D), lambda i: (i, 0)),
        pl.BlockSpec((TM, D), lambda i: (i, 0)),
        pl.BlockSpec((TM, LANES), lambda i: (i, 0)),
        pl.BlockSpec((1, 1, LANES), lambda i: (i, 0, 0)),
    ]


def _tail_in_specs(li):
    return [
        _mod_spec(li, 2), _mod_spec(li, 4), _mod_spec(li, 3),
        _row_spec((1, D)), _row_spec((1, D)),
        _row_spec((D, LANES)), _row_spec((1, LANES)),
    ]


def _out_proj_kernel(rp_ref, rs_ref, op_ref, os_ref, w_ref, xp_ref, xs_ref, *rest):
    half = HEADS * HEAD_W
    r = _pick_tile(rp_ref, rs_ref).astype(BF16)
    o = _pick_tile(op_ref, os_ref).astype(BF16)
    out = _dot(r, w_ref[0:half, :]) + _dot(o, w_ref[half:, :])
    _mixer_tail(out, _pick_tile(xp_ref, xs_ref), *rest)


def _out_proj_tail(r_p, r_s, o_p, o_s, w_out_bf16, x_prompt, x_sample, modr, ln_g, ln_b, wr, br, li):
    half = HEADS * HEAD_W
    return pl.pallas_call(
        _out_proj_kernel,
        grid=(N_TILES,),
        in_specs=[_prompt_tile_spec(half), _sample_tile_spec(half),
                  _prompt_tile_spec(half), _sample_tile_spec(half),
                  _row_spec((2 * half, D)),
                  _prompt_tile_spec(D), _sample_tile_spec(D)] + _tail_in_specs(li),
        out_specs=_tail_out_specs(),
        out_shape=_TAIL_OUT_SHAPES,
        compiler_params=_cparams(("arbitrary",)),
        name="out_proj_tail",
    )(r_p, r_s, o_p, o_s, w_out_bf16, x_prompt, x_sample, modr, modr, modr, ln_g, ln_b, wr, br)


def _conv_glu_kernel(x_ref, sh_ref, sc_ref, w_ref, b_ref, o_ref):
    u = x_ref[...] * (1.0 + sc_ref[0]) + sh_ref[0]
    h = _dot(u.astype(BF16), w_ref[...]) + b_ref[...]
    o_ref[...] = h[:, :D] * jax.nn.sigmoid(h[:, D:])


def _conv_glu(x, modr, w1_bf16, b1, li):
    return pl.pallas_call(
        _conv_glu_kernel,
        grid=(N_TILES,),
        in_specs=[pl.BlockSpec((TM, D), lambda i: (i, 0)), _mod_spec(li, 0), _mod_spec(li, 1),
                  _row_spec((D, 2 * D)), _row_spec((1, 2 * D))],
        out_specs=pl.BlockSpec((TM, D), lambda i: (i, 0)),
        out_shape=jax.ShapeDtypeStruct((T, D), F32),
        compiler_params=_cparams(("arbitrary",)),
        name="conv_glu",
    )(x, modr, modr, w1_bf16, b1)


HALO = 16
CONV_ROWS = 64
CONV_COLS = 128


def _depthwise_conv(hp, dw_ref, conv):
    base = HALO - CONV_PAD
    for cb in range(D // CONV_COLS):
        cs = slice(cb * CONV_COLS, (cb + 1) * CONV_COLS)
        for rb in range(TM // CONV_ROWS):
            r0 = rb * CONV_ROWS
            acc = None
            for shift in range(SUBLANES):
                part = None
                for tap in range(CONV_K):
                    off = base + tap
                    if off % SUBLANES != shift:
                        continue
                    a0 = r0 + off - shift
                    term = hp[a0:a0 + CONV_ROWS + SUBLANES, cs] * dw_ref[tap:tap + 1, cs]
                    part = term if part is None else part + term
                part = part[shift:shift + CONV_ROWS, :]
                acc = part if acc is None else acc + part
            conv[r0:r0 + CONV_ROWS, cs] = acc


def _conv_tail_kernel(cur_ref, prev_ref, next_ref, dw_ref, dwb_ref, cg_ref, cb_ref, w2_ref, b2_ref,
                      x_ref, *rest):
    tail_args, hp, conv = rest[:-2], rest[-2], rest[-1]
    i = pl.program_id(0)
    k = (i - PROMPT_TILES) % SAMPLE_TILES_PER_SEQ
    in_sample = i >= PROMPT_TILES
    left_ok = jnp.logical_and(in_sample, k != 0)
    right_ok = jnp.logical_and(in_sample, k != SAMPLE_TILES_PER_SEQ - 1)
    hp[0:HALO, :] = jnp.where(left_ok, prev_ref[...], 0.0)
    hp[HALO:HALO + TM, :] = cur_ref[...]
    hp[HALO + TM:HALO + TM + HALO, :] = jnp.where(right_ok, next_ref[...], 0.0)
    _depthwise_conv(hp, dw_ref, conv)
    hc = _silu(_layer_norm(conv[...] + dwb_ref[...], cg_ref[...], cb_ref[...]))
    out = _dot(hc.astype(BF16), w2_ref[...]) + b2_ref[...]
    _mixer_tail(out, x_ref[...], *tail_args)


def _conv_tail(glu, dw, dwb, cg, cb, w2_bf16, b2, x, modr, ln_g, ln_b, wr, br, li):
    per = TM // HALO
    last = T // HALO - 1
    return pl.pallas_call(
        _conv_tail_kernel,
        grid=(N_TILES,),
        in_specs=[pl.BlockSpec((TM, D), lambda i: (i, 0)),
                  pl.BlockSpec((HALO, D), lambda i: (jnp.maximum(i * per - 1, 0), 0)),
                  pl.BlockSpec((HALO, D), lambda i: (jnp.minimum((i + 1) * per, last), 0)),
                  _row_spec((CONV_K, D)), _row_spec((1, D)), _row_spec((1, D)), _row_spec((1, D)),
                  _row_spec((D, D)), _row_spec((1, D)),
                  pl.BlockSpec((TM, D), lambda i: (i, 0))] + _tail_in_specs(li),
        out_specs=_tail_out_specs(),
        out_shape=_TAIL_OUT_SHAPES,
        scratch_shapes=[pltpu.VMEM((TM + 2 * HALO, D), F32), pltpu.VMEM((TM, D), F32)],
        compiler_params=_cparams(("arbitrary",)),
        name="conv_tail",
    )(glu, glu, glu, dw, dwb, cg, cb, w2_bf16, b2, x, modr, modr, modr, ln_g, ln_b, wr, br)


def _sorted_positions(meta, srcv):
    lane = lax.broadcasted_iota(jnp.int32, meta.shape, 1).astype(F32)

    def pos(e_col, r_col):
        start = jnp.sum(jnp.where(lane == meta[:, e_col:e_col + 1], srcv, 0.0), axis=-1, keepdims=True)
        return start + meta[:, r_col:r_col + 1]

    return pos(META_E1, META_RANK1), pos(META_E2, META_RANK2)


def _one_hot_rows(pos):
    col = lax.broadcasted_iota(jnp.int32, (TM, SORT_ROWS), 1).astype(F32)
    return col == pos


def _for_each_run_piece(tile, src_ref, dst_ref, n_ref, fn):
    for e in range(N_EXPERTS):
        k = tile * N_EXPERTS + e
        n, src, dst = n_ref[k], src_ref[k], dst_ref[k]
        done = 0
        for bit in RUN_BITS:
            piece = n & bit

            @pl.when(piece != 0)
            def _(done=done, bit=bit):
                fn(pl.multiple_of(src + done, RUN_ALIGN), pl.multiple_of(dst + done, RUN_ALIGN), bit)

            done = done + piece


def _wait_rows(total, make_copy):
    for bit in RUN_BITS:
        @pl.when((total & bit) != 0)
        def _(bit=bit):
            make_copy(bit).wait()


def _dispatch_kernel(src_ref, dst_ref, n_ref, tot_ref, u_ref, meta_ref, srcv_ref, xs_ref, sorted_ref, sems):
    i = pl.program_id(0)
    slot = i % 2

    def wait_tile(tile, slot):
        buf = sorted_ref.at[slot]
        _wait_rows(tot_ref[tile], lambda rows: pltpu.make_async_copy(
            buf.at[pl.ds(0, rows)], xs_ref.at[pl.ds(0, rows)], sems.at[slot]))

    @pl.when(i >= 2)
    def _():
        wait_tile(i - 2, slot)

    pos1, pos2 = _sorted_positions(meta_ref[...], srcv_ref[0])
    select = jnp.logical_or(_one_hot_rows(pos1), _one_hot_rows(pos2)).astype(BF16)
    sorted_ref[slot] = _dot_tn(select, u_ref[...])
    buf = sorted_ref.at[slot]

    def start(src, dst, rows):
        pltpu.make_async_copy(buf.at[pl.ds(src, rows)], xs_ref.at[pl.ds(dst, rows)], sems.at[slot]).start()

    _for_each_run_piece(i, src_ref, dst_ref, n_ref, start)

    @pl.when(i == N_TILES - 1)
    def _():
        wait_tile(i - 1, 1 - slot)
        wait_tile(i, slot)


def _dispatch(sched, u2, meta, srcv):
    return pl.pallas_call(
        _dispatch_kernel,
        grid_spec=pltpu.PrefetchScalarGridSpec(
            num_scalar_prefetch=4,
            grid=(N_TILES,),
            in_specs=[pl.BlockSpec((TM, D), lambda i, *_: (i, 0)),
                      pl.BlockSpec((TM, LANES), lambda i, *_: (i, 0)),
                      pl.BlockSpec((1, 1, LANES), lambda i, *_: (i, 0, 0))],
            out_specs=pl.BlockSpec(memory_space=pl.ANY),
            scratch_shapes=[pltpu.VMEM((2, SORT_ROWS, D), F32), pltpu.SemaphoreType.DMA((2,))],
        ),
        out_shape=jax.ShapeDtypeStruct((MOE_ROWS, D), F32),
        compiler_params=_cparams(("arbitrary",)),
        name="moe_dispatch",
    )(*sched, u2, meta, srcv)


def _experts_kernel(start_ref, chunks_ref, xs_ref, wg_ref, wu_ref, wd_ref, ys_ref,
                    wg_bf, wu_bf, wd_bf, xbuf, ybuf, in_sems, out_sems):
    e = pl.program_id(0)
    n = chunks_ref[e]
    first = start_ref[e] // MOE_TM
    total = start_ref[N_EXPERTS - 1] // MOE_TM + chunks_ref[N_EXPERTS - 1]

    def rows(g):
        return pl.ds(pl.multiple_of(g * MOE_TM, MOE_TM), MOE_TM)

    def load(g):
        slot = g % MOE_IN_SLOTS
        return pltpu.make_async_copy(xs_ref.at[rows(g)], xbuf.at[slot], in_sems.at[slot])

    def store(g):
        slot = g % MOE_OUT_SLOTS
        return pltpu.make_async_copy(ybuf.at[slot], ys_ref.at[rows(g)], out_sems.at[slot])

    @pl.when(e == 0)
    def _():
        for g in range(MOE_AHEAD):
            @pl.when(g < total)
            def _(g=g):
                load(g).start()

    @pl.when(n > 0)
    def _():
        wg_bf[...] = wg_ref[0, 0].astype(BF16)
        wu_bf[...] = wu_ref[0, 0].astype(BF16)
        wd_bf[...] = wd_ref[0, 0].astype(BF16)

        def tile(g, carry):
            load(g).wait()

            @pl.when(g + MOE_AHEAD < total)
            def _():
                load(g + MOE_AHEAD).start()

            @pl.when(g >= MOE_OUT_SLOTS)
            def _():
                store(g - MOE_OUT_SLOTS).wait()

            x = xbuf[g % MOE_IN_SLOTS].astype(BF16)
            h = (_silu(_dot(x, wg_bf[...])) * _dot(x, wu_bf[...])).astype(BF16)
            ybuf[g % MOE_OUT_SLOTS] = _dot(h, wd_bf[...])
            store(g).start()
            return carry

        lax.fori_loop(first, first + n, tile, 0)

    @pl.when(e == N_EXPERTS - 1)
    def _():
        for back in range(MOE_OUT_SLOTS, 0, -1):
            @pl.when(total >= back)
            def _(back=back):
                store(total - back).wait()


def _experts(seg_start, seg_chunks, xs, w_gate, w_up, w_down, li):
    def weight(shape):
        return pl.BlockSpec((1, 1) + shape, lambda e, *_: (li, e, 0, 0))

    return pl.pallas_call(
        _experts_kernel,
        grid_spec=pltpu.PrefetchScalarGridSpec(
            num_scalar_prefetch=2,
            grid=(N_EXPERTS,),
            in_specs=[pl.BlockSpec(memory_space=pl.ANY),
                      weight((D, D_EXPERT)), weight((D, D_EXPERT)), weight((D_EXPERT, D))],
            out_specs=pl.BlockSpec(memory_space=pl.ANY),
            scratch_shapes=[pltpu.VMEM((D, D_EXPERT), BF16), pltpu.VMEM((D, D_EXPERT), BF16),
                            pltpu.VMEM((D_EXPERT, D), BF16),
                            pltpu.VMEM((MOE_IN_SLOTS, MOE_TM, D), F32),
                            pltpu.VMEM((MOE_OUT_SLOTS, MOE_TM, D), F32),
                            pltpu.SemaphoreType.DMA((MOE_IN_SLOTS,)),
                            pltpu.SemaphoreType.DMA((MOE_OUT_SLOTS,))],
        ),
        out_shape=jax.ShapeDtypeStruct((MOE_ROWS, D), F32),
        compiler_params=_cparams(("arbitrary",)),
        name="moe_experts",
    )(seg_start, seg_chunks, xs, w_gate, w_up, w_down)


def _combine_kernel(src_ref, dst_ref, n_ref, tot_ref, ys_ref, x1_ref, meta_ref, srcv_ref, g2_ref,
                    lng_ref, lnb_ref, *rest, split):
    outs, (sorted_ref, sems) = rest[:-2], rest[-2:]
    i = pl.program_id(0)
    slot = i % 2

    def fetch(tile, slot):
        buf = sorted_ref.at[slot]

        def start(src, dst, rows):
            pltpu.make_async_copy(ys_ref.at[pl.ds(dst, rows)], buf.at[pl.ds(src, rows)], sems.at[slot]).start()

        _for_each_run_piece(tile, src_ref, dst_ref, n_ref, start)

    @pl.when(i == 0)
    def _():
        sorted_ref[...] = jnp.zeros_like(sorted_ref)
        fetch(0, 0)

    @pl.when(i + 1 < N_TILES)
    def _():
        fetch(i + 1, 1 - slot)

    meta = meta_ref[...]
    pos1, pos2 = _sorted_positions(meta, srcv_ref[0])
    sel1 = _one_hot_rows(pos1).astype(BF16)
    sel2 = _one_hot_rows(pos2).astype(BF16)
    buf = sorted_ref.at[slot]
    _wait_rows(tot_ref[i], lambda rows: pltpu.make_async_copy(
        ys_ref.at[pl.ds(0, rows)], buf.at[pl.ds(0, rows)], sems.at[slot]))
    ysort = sorted_ref[slot].astype(BF16)
    f = (meta[:, META_W1:META_W1 + 1] * _dot(sel1, ysort)
         + meta[:, META_W2:META_W2 + 1] * _dot(sel2, ysort))
    y = _layer_norm(ALPHA * x1_ref[...] + g2_ref[0] * f, lng_ref[...], lnb_ref[...])
    if split:
        @pl.when(i < PROMPT_TILES)
        def _():
            outs[0][...] = y

        @pl.when(i >= PROMPT_TILES)
        def _():
            outs[1][...] = y
    else:
        outs[0][...] = y


def _combine(sched, ys, x1, meta, srcv, modr, ln_g, ln_b, li, split):
    if split:
        out_specs = [_prompt_tile_spec(D), _sample_tile_spec(D)]
        out_shape = [jax.ShapeDtypeStruct((T_PROMPT, D), F32), jax.ShapeDtypeStruct((T_SAMPLE, D), F32)]
    else:
        out_specs = [pl.BlockSpec((TM, D), lambda i, *_: (i, 0))]
        out_shape = [jax.ShapeDtypeStruct((T, D), F32)]
    return pl.pallas_call(
        functools.partial(_combine_kernel, split=split),
        grid_spec=pltpu.PrefetchScalarGridSpec(
            num_scalar_prefetch=4,
            grid=(N_TILES,),
            in_specs=[pl.BlockSpec(memory_space=pl.ANY),
                      pl.BlockSpec((TM, D), lambda i, *_: (i, 0)),
                      pl.BlockSpec((TM, LANES), lambda i, *_: (i, 0)),
                      pl.BlockSpec((1, 1, LANES), lambda i, *_: (i, 0, 0)),
                      _mod_spec(li, 5), _row_spec((1, D)), _row_spec((1, D))],
            out_specs=out_specs,
            scratch_shapes=[pltpu.VMEM((2, SORT_ROWS, D), F32), pltpu.SemaphoreType.DMA((2,))],
        ),
        out_shape=out_shape,
        compiler_params=_cparams(("arbitrary",)),
        name="moe_combine",
    )(*sched, ys, x1, meta, srcv, modr, ln_g, ln_b)


def _moe_schedule(tile_counts):
    n = (tile_counts + RUN_ALIGN - 1) // RUN_ALIGN * RUN_ALIGN
    src = jnp.cumsum(n, axis=1) - n
    per_expert = jnp.sum(n, axis=0)
    seg = (per_expert + MOE_TM - 1) // MOE_TM * MOE_TM
    seg_start = jnp.cumsum(seg) - seg
    dst = seg_start[None, :] + jnp.cumsum(n, axis=0) - n
    runs = (src.reshape(-1), dst.reshape(-1), n.reshape(-1), jnp.sum(n, axis=1))
    srcv = jnp.pad(src.astype(F32), ((0, 0), (ROUTER_LANE0, LANES - ROUTER_LANE0 - N_EXPERTS)))
    return runs, srcv.reshape(N_TILES, 1, LANES), (seg_start, seg // MOE_TM)


def _moe(x1, u2, meta, cnt, modr, w_gate, w_up, w_down, ln_g, ln_b, li, split):
    tile_counts = cnt[:, 0, ROUTER_LANE0:ROUTER_LANE0 + N_EXPERTS].astype(jnp.int32)
    runs, srcv, (seg_start, seg_chunks) = _moe_schedule(tile_counts)
    xs = _dispatch(runs, u2, meta, srcv)
    ys = _experts(seg_start, seg_chunks, xs, w_gate, w_up, w_down, li)
    return _combine(runs, ys, x1, meta, srcv, modr, ln_g, ln_b, li, split)


def _router_slab(wg, bg, we, be):
    w = jnp.concatenate([wg, we.transpose(1, 0, 2).reshape(D, N_EXPERTS)], axis=1)
    b = jnp.concatenate([bg, be.reshape(N_EXPERTS)])
    pad = LANES - w.shape[1]
    return jnp.pad(w, ((0, 0), (0, pad))), jnp.pad(b, (0, pad)).reshape(1, LANES)


def kernel(x_prompt, x_sample, cache_diff_k, cache_diff_v, state_ret_fwd, state_ret_bwd, c, c_ctx, mod_w, mod_b, ln1_g, ln1_b, ln2_g, ln2_b, mix_w_in, mix_w_out, ret_decay_fwd, ret_decay_bwd, diff_lq1, diff_lk1, diff_lq2, diff_lk2, diff_subln_g, conv_w1, conv_b1, conv_dw, conv_dw_b, conv_ln_g, conv_ln_b, conv_w2, conv_b2, router_g_w, router_g_b, router_e_w, router_e_b, moe_w_gate, moe_w_up, moe_w_down):
    xp = x_prompt.reshape(T_PROMPT, D)
    xs = x_sample.reshape(T_SAMPLE, D)
    cond = jnp.concatenate([c_ctx[None, :], c, jnp.zeros((MOD_ROWS - 1 - DEC_BATCH, D), F32)], axis=0)
    modr = _mod_vectors(cond, mod_w, mod_b).reshape(DEPTH * MOD_ROWS * 6, 1, D)
    cos, sin_signed = _rope_tables()

    def row(v):
        return v.reshape(1, -1)

    x = None
    caches = None
    for li in range(DEPTH):
        wr, br = _router_slab(router_g_w[li], router_g_b[li], router_e_w[li], router_e_b[li])
        if li % 2 == 0:
            assert li == 0, "the even mixer reads the kernel inputs directly"
            e = li // 2
            lam_init = 0.8 - 0.6 * math.exp(-0.3 * li)
            proj, ck, cv = _in_proj(xp, xs, modr, mix_w_in[e].astype(BF16), li)
            dec = jnp.concatenate([ret_decay_fwd[e], ret_decay_bwd[e]])
            r_p, sf, sb = _retention(proj, dec, BATCH, SEQ, 0, emit_state=True)
            (r_s,) = _retention(proj, dec, DEC_BATCH, DEC_SEQ, T_PROMPT // DEC_SEQ,
                                s0f=state_ret_fwd, s0b=state_ret_bwd, e=e)
            lams = (row(diff_lq1[e]), row(diff_lk1[e]), row(diff_lq2[e]), row(diff_lk2[e]),
                    row(diff_subln_g[e]))
            o_p = _attn_prompt(proj, *lams, lam_init)
            o_s = _attn_sample(proj, cache_diff_k, cache_diff_v, cos, sin_signed, *lams, lam_init, e)
            x1, u2, meta, cnt = _out_proj_tail(r_p, r_s, o_p, o_s, mix_w_out[e].astype(BF16), xp, xs, modr,
                                               row(ln1_g[li]), row(ln1_b[li]), wr, br, li)
            caches = (ck, cv, sf, sb)
        else:
            o = li // 2
            glu = _conv_glu(x, modr, conv_w1[o].astype(BF16), row(conv_b1[o]), li)
            x1, u2, meta, cnt = _conv_tail(glu, conv_dw[o], row(conv_dw_b[o]), row(conv_ln_g[o]),
                                           row(conv_ln_b[o]), conv_w2[o].astype(BF16), row(conv_b2[o]),
                                           x, modr, row(ln1_g[li]), row(ln1_b[li]), wr, br, li)
        outs = _moe(x1, u2, meta, cnt, modr, moe_w_gate, moe_w_up, moe_w_down,
                    row(ln2_g[li]), row(ln2_b[li]), li, split=(li == DEPTH - 1))
        x = outs[0]

    y_prompt = outs[0].reshape(BATCH, SEQ, D)
    y_sample = outs[1].reshape(DEC_BATCH, DEC_SEQ, D)
    return (y_prompt, y_sample) + caches
```

```python
import functools
import math

import numpy as np
import jax
import jax.numpy as jnp
from jax import lax
from jax.experimental import pallas as pl
from jax.experimental.pallas import tpu as pltpu

F32 = jnp.float32
BF16 = jnp.bfloat16

D = 1024
BATCH = 16
SEQ = 256
DEPTH = 2
DEC_BATCH = 2
DEC_SEQ = 2048
PAST_LEN = 512
GRID_W = 64
HEADS = 4
HEAD_W = 128
RET_CHUNK = 128
DIFF_DK = 64
ROPE_THETA = 10000.0
IN_W = 7 * HEADS * HEAD_W
CONV_K = 31
CONV_PAD = CONV_K // 2
N_GROUPS = 4
EXPERTS_PER_GROUP = 8
N_EXPERTS = N_GROUPS * EXPERTS_PER_GROUP
D_EXPERT = 512
ALPHA = (2.0 * DEPTH) ** 0.25
LN_EPS = 1e-5
GN_EPS = 1e-6

T_PROMPT = BATCH * SEQ
T_SAMPLE = DEC_BATCH * DEC_SEQ
T = T_PROMPT + T_SAMPLE
TM = 256
N_TILES = T // TM
PROMPT_TILES = T_PROMPT // TM
SAMPLE_TILES_PER_SEQ = DEC_SEQ // TM
MOD_ROWS = 8
MOE_TM = 256
LANES = 128
SUBLANES = 8
RUN_ALIGN = SUBLANES
SORT_ROWS = -(-(2 * TM + N_EXPERTS * (RUN_ALIGN - 1)) // TM) * TM
RUN_BITS = tuple(1 << b for b in range((2 * TM).bit_length() - 1, RUN_ALIGN.bit_length() - 2, -1))
MOE_MAX_TILES = -(-(2 * T + N_TILES * N_EXPERTS * (RUN_ALIGN - 1) + N_EXPERTS * (MOE_TM - RUN_ALIGN)) // MOE_TM)
MOE_ROWS = MOE_MAX_TILES * MOE_TM
SORT_GROUPS = SORT_ROWS // RUN_ALIGN
MOE_AHEAD = 2
MOE_IN_SLOTS = MOE_AHEAD + 1
MOE_OUT_SLOTS = 2
ROUTER_LANE0 = N_GROUPS
VMEM_LIMIT = 52 * 1024 * 1024


def _cparams(sem):
    return pltpu.CompilerParams(dimension_semantics=sem, vmem_limit_bytes=VMEM_LIMIT)


def _tile_cond_row(i):
    return jnp.where(i < PROMPT_TILES, 0, 1 + (i - PROMPT_TILES) // SAMPLE_TILES_PER_SEQ)


def _mod_spec(li, k):
    return pl.BlockSpec((1, 1, D), lambda i, *_: ((li * MOD_ROWS + _tile_cond_row(i)) * 6 + k, 0, 0))


def _row_spec(shape):
    return pl.BlockSpec(shape, lambda i, *_: (0,) * len(shape))


def _layer_norm(x, g, b):
    mu = jnp.mean(x, axis=-1, keepdims=True)
    xc = x - mu
    var = jnp.mean(xc * xc, axis=-1, keepdims=True)
    return xc * lax.rsqrt(var + LN_EPS) * g + b


def _silu(x):
    return x * jax.nn.sigmoid(x)


def _dot(a, b):
    return jnp.dot(a, b, preferred_element_type=F32)


def _dot_nt(a, b):
    return lax.dot_general(a, b, (((1,), (1,)), ((), ())), preferred_element_type=F32)


def _dot_tn(a, b):
    return lax.dot_general(a, b, (((0,), (0,)), ((), ())), preferred_element_type=F32)


MOD_TN = 512


def _mod_kernel(cond_ref, w_ref, b_ref, o_ref):
    s = _silu(cond_ref[...])
    o_ref[0] = jnp.dot(s, w_ref[0], precision=lax.Precision.HIGHEST,
                       preferred_element_type=F32) + b_ref[0]


def _mod_vectors(cond, mod_w, mod_b):
    return pl.pallas_call(
        _mod_kernel,
        grid=(DEPTH, 6 * D // MOD_TN),
        in_specs=[
            pl.BlockSpec((MOD_ROWS, D), lambda l, j: (0, 0)),
            pl.BlockSpec((1, D, MOD_TN), lambda l, j: (l, 0, j)),
            pl.BlockSpec((1, 1, MOD_TN), lambda l, j: (l, 0, j)),
        ],
        out_specs=pl.BlockSpec((1, MOD_ROWS, MOD_TN), lambda l, j: (l, 0, j)),
        out_shape=jax.ShapeDtypeStruct((DEPTH, MOD_ROWS, 6 * D), F32),
        compiler_params=_cparams(("arbitrary", "arbitrary")),
        name="mod_vectors",
    )(cond, mod_w, mod_b.reshape(DEPTH, 1, 6 * D))


def _prompt_tile_spec(width):
    return pl.BlockSpec((TM, width), lambda i, *_: (jnp.minimum(i, PROMPT_TILES - 1), 0))


def _sample_tile_spec(width):
    return pl.BlockSpec((TM, width), lambda i, *_: (jnp.maximum(i - PROMPT_TILES, 0), 0))


def _pick_tile(prompt_ref, sample_ref):
    return jnp.where(pl.program_id(0) < PROMPT_TILES, prompt_ref[...], sample_ref[...])


def _in_proj_kernel(xp_ref, xs_ref, sh_ref, sc_ref, w_ref, o_ref, ck_ref, cv_ref):
    u = _pick_tile(xp_ref, xs_ref) * (1.0 + sc_ref[0]) + sh_ref[0]
    proj = _dot(u.astype(BF16), w_ref[...])
    o_ref[...] = proj

    @pl.when(pl.program_id(0) < PROMPT_TILES)
    def _():
        for h in range(HEADS):
            ck_ref[0, 0, h] = proj[:, (COL_KD + h) * HEAD_W:(COL_KD + h + 1) * HEAD_W]
            cv_ref[0, 0, h] = proj[:, (COL_VD + h) * HEAD_W:(COL_VD + h + 1) * HEAD_W]


def _in_proj(x_prompt, x_sample, modr, w_in_bf16, li):
    cache_spec = pl.BlockSpec((1, 1, HEADS, SEQ, HEAD_W),
                              lambda i: (jnp.minimum(i, PROMPT_TILES - 1), 0, 0, 0, 0))
    cache_shape = jax.ShapeDtypeStruct((BATCH, 1, HEADS, SEQ, HEAD_W), F32)
    return pl.pallas_call(
        _in_proj_kernel,
        grid=(N_TILES,),
        in_specs=[
            _prompt_tile_spec(D), _sample_tile_spec(D),
            _mod_spec(li, 0),
            _mod_spec(li, 1),
            _row_spec((D, IN_W)),
        ],
        out_specs=[pl.BlockSpec((TM, IN_W), lambda i: (i, 0)), cache_spec, cache_spec],
        out_shape=[jax.ShapeDtypeStruct((T, IN_W), F32), cache_shape, cache_shape],
        compiler_params=_cparams(("arbitrary",)),
        name="in_proj",
    )(x_prompt, x_sample, modr, modr, w_in_bf16)


COL_QR, COL_KR, COL_VR, COL_GR, COL_QD, COL_KD, COL_VD = (k * HEADS for k in range(7))


def _retention_kernel(dec_ref, q_ref, k_ref, v_ref, g_ref, *rest, n_chunks, has_state, emit_state):
    rest = list(rest)
    if has_state:
        s0f_ref, s0b_ref = rest[:2]
        rest = rest[2:]
    r_ref = rest[0]
    rest = rest[1:]
    if emit_state:
        sf_ref, sb_ref = rest[:2]
        rest = rest[2:]
    of_ref = rest[0]

    h = pl.program_id(1)
    C = RET_CHUNK
    ii = lax.broadcasted_iota(jnp.int32, (C, C), 0)
    jj = lax.broadcasted_iota(jnp.int32, (C, C), 1)
    rel = (ii - jj).astype(F32)
    idx = lax.broadcasted_iota(jnp.int32, (C, 1), 0).astype(F32)
    k_scale = HEAD_W ** -0.5

    def chunk(ref, c):
        return ref[c * C:(c + 1) * C, :]

    def run(direction):
        lg = -jnp.exp(jnp.full((1, 1), dec_ref[direction * HEADS + h], F32))
        if direction == 0:
            inner = jnp.where(rel >= 0, jnp.exp(jnp.maximum(rel, 0.0) * lg), 0.0)
            q_decay = jnp.exp((idx + 1.0) * lg)
            k_decay = jnp.exp((C - 1.0 - idx) * lg)
            order = range(n_chunks)
        else:
            inner = jnp.where(rel <= 0, jnp.exp(jnp.maximum(-rel, 0.0) * lg), 0.0)
            q_decay = jnp.exp((C - idx) * lg)
            k_decay = jnp.exp(idx * lg)
            order = range(n_chunks - 1, -1, -1)
        chunk_decay = jnp.exp(C * lg)
        if has_state:
            s = (s0f_ref if direction == 0 else s0b_ref)[0, 0, 0]
        else:
            s = jnp.zeros((HEAD_W, HEAD_W), F32)
        for c in order:
            qc = chunk(q_ref, c)
            kc = chunk(k_ref, c) * k_scale
            vc = chunk(v_ref, c).astype(BF16)
            scores = _dot_nt(qc.astype(BF16), kc.astype(BF16)) * inner
            o = _dot(scores.astype(BF16), vc) + _dot((qc * q_decay).astype(BF16), s.astype(BF16))
            s = s * chunk_decay + _dot_tn((kc * k_decay).astype(BF16), vc)
            if direction == 0:
                of_ref[c * C:(c + 1) * C, :] = o
            else:
                r = of_ref[c * C:(c + 1) * C, :] + o
                mu = jnp.mean(r, axis=-1, keepdims=True)
                rc = r - mu
                var = jnp.mean(rc * rc, axis=-1, keepdims=True)
                rn = rc * lax.rsqrt(var + GN_EPS)
                r_ref[c * C:(c + 1) * C, :] = _silu(chunk(g_ref, c)) * rn
        return s

    sf = run(0)
    sb = run(1)
    if emit_state:
        sf_ref[0, 0, 0] = sf
        sb_ref[0, 0, 0] = sb


def _retention(proj, dec, n_seq, seq_len, row_block0, s0f=None, s0b=None, e=0, emit_state=False):
    has_state = s0f is not None

    def col(base):
        return pl.BlockSpec((seq_len, HEAD_W), lambda b, h, *_: (row_block0 + b, base + h))

    state_spec = pl.BlockSpec((1, 1, 1, HEAD_W, HEAD_W), lambda b, h, *_: (b, e, h, 0, 0))
    in_specs = [pl.BlockSpec(memory_space=pltpu.SMEM), col(COL_QR), col(COL_KR), col(COL_VR), col(COL_GR)]
    args = [dec, proj, proj, proj, proj]
    if has_state:
        in_specs += [state_spec, state_spec]
        args += [s0f, s0b]
    out_specs = [pl.BlockSpec((seq_len, HEAD_W), lambda b, h, *_: (b, h))]
    out_shape = [jax.ShapeDtypeStruct((n_seq * seq_len, HEADS * HEAD_W), F32)]
    if emit_state:
        st = pl.BlockSpec((1, 1, 1, HEAD_W, HEAD_W), lambda b, h, *_: (b, 0, h, 0, 0))
        out_specs += [st, st]
        out_shape += [jax.ShapeDtypeStruct((n_seq, 1, HEADS, HEAD_W, HEAD_W), F32)] * 2
    return pl.pallas_call(
        functools.partial(_retention_kernel, n_chunks=seq_len // RET_CHUNK,
                          has_state=has_state, emit_state=emit_state),
        grid=(n_seq, HEADS),
        in_specs=in_specs,
        out_specs=out_specs,
        out_shape=out_shape,
        scratch_shapes=[pltpu.VMEM((seq_len, HEAD_W), F32)],
        compiler_params=_cparams(("arbitrary", "arbitrary")),
        name=f"retention_{seq_len}",
    )(*args)


def _diff_lambda(lq1_ref, lk1_ref, lq2_ref, lk2_ref, lam_init):
    a = jnp.sum(lq1_ref[...] * lk1_ref[...], axis=-1, keepdims=True)
    b = jnp.sum(lq2_ref[...] * lk2_ref[...], axis=-1, keepdims=True)
    return jnp.exp(a) - jnp.exp(b) + lam_init


LOG2E = 1.4426950408889634


def _diff_attend(q, k, v, lam, subln_g, lam_init):
    lane = lax.broadcasted_iota(jnp.int32, q.shape, 1)
    q1 = jnp.where(lane < DIFF_DK, q, 0.0).astype(BF16)
    q2 = jnp.where(lane >= DIFF_DK, q, 0.0).astype(BF16)

    def softmax_times_v(qz):
        s = _dot_nt(qz, k)
        p = jnp.exp2(s - jnp.max(s, axis=-1, keepdims=True))
        return _dot(p.astype(BF16), v) * (1.0 / jnp.sum(p, axis=-1, keepdims=True))

    o = softmax_times_v(q1) - lam * softmax_times_v(q2)
    o = o * lax.rsqrt(jnp.mean(o * o, axis=-1, keepdims=True) + LN_EPS)
    return o * subln_g * (1.0 - lam_init)


def _attn_prompt_kernel(q_ref, k_ref, v_ref, lq1, lk1, lq2, lk2, g_ref, o_ref, *, lam_init):
    lam = _diff_lambda(lq1, lk1, lq2, lk2, lam_init)
    scale = DIFF_DK ** -0.5 * LOG2E
    for h in range(HEADS):
        sl = slice(h * HEAD_W, (h + 1) * HEAD_W)
        o_ref[:, sl] = _diff_attend(q_ref[:, sl] * scale, k_ref[:, sl].astype(BF16),
                                    v_ref[:, sl].astype(BF16), lam, g_ref[...], lam_init)


def _attn_prompt(proj, lq1, lk1, lq2, lk2, subln_g, lam_init):
    W = HEADS * HEAD_W

    def slab(base):
        return pl.BlockSpec((SEQ, W), lambda b: (b, base // HEADS))

    small = _row_spec((1, DIFF_DK))
    return pl.pallas_call(
        functools.partial(_attn_prompt_kernel, lam_init=lam_init),
        grid=(BATCH,),
        in_specs=[slab(COL_QD), slab(COL_KD), slab(COL_VD), small, small, small, small,
                  _row_spec((1, HEAD_W))],
        out_specs=pl.BlockSpec((SEQ, W), lambda b: (b, 0)),
        out_shape=jax.ShapeDtypeStruct((T_PROMPT, W), F32),
        compiler_params=_cparams(("arbitrary",)),
        name="diff_attn_prompt",
    )(proj, proj, proj, lq1, lk1, lq2, lk2, subln_g)


def _rope(x, cos, sin_signed):
    lane = lax.broadcasted_iota(jnp.int32, x.shape, 1)
    partner = jnp.where((lane % 32) < 16, pltpu.roll(x, LANES - 16, 1), pltpu.roll(x, 16, 1))
    return x * cos + partner * sin_signed


def _attn_sample_kernel(q_ref, k_ref, v_ref, ck_ref, cv_ref, cosq_ref, sinq_ref, cos_ref, sin_ref,
                        lq1, lk1, lq2, lk2, g_ref, o_ref, kbuf, vbuf, *, lam_init):
    @pl.when(pl.program_id(2) == 0)
    def _():
        kbuf[0:DEC_SEQ, :] = _rope(k_ref[...], cos_ref[...], sin_ref[...]).astype(BF16)
        kbuf[DEC_SEQ:, :] = ck_ref[0, 0, 0].astype(BF16)
        vbuf[0:DEC_SEQ, :] = v_ref[...].astype(BF16)
        vbuf[DEC_SEQ:, :] = cv_ref[0, 0, 0].astype(BF16)

    lam = _diff_lambda(lq1, lk1, lq2, lk2, lam_init)
    q = _rope(q_ref[...], cosq_ref[...], sinq_ref[...]) * (DIFF_DK ** -0.5 * LOG2E)
    o_ref[...] = _diff_attend(q, kbuf[...], vbuf[...], lam, g_ref[...], lam_init)


ATTN_TQ = 256


def _attn_sample(proj, cache_k, cache_v, cos, sin_signed, lq1, lk1, lq2, lk2, subln_g, lam_init, e):
    nq = DEC_SEQ // ATTN_TQ
    row0_q = T_PROMPT // ATTN_TQ
    row0_kv = T_PROMPT // DEC_SEQ
    small = pl.BlockSpec((1, DIFF_DK), lambda b, h, t: (0, 0))
    cache = pl.BlockSpec((1, 1, 1, PAST_LEN, HEAD_W), lambda b, h, t: (b, e, h, 0, 0))
    table_q = pl.BlockSpec((ATTN_TQ, HEAD_W), lambda b, h, t: (t, 0))
    table = pl.BlockSpec((DEC_SEQ, HEAD_W), lambda b, h, t: (0, 0))
    return pl.pallas_call(
        functools.partial(_attn_sample_kernel, lam_init=lam_init),
        grid=(DEC_BATCH, HEADS, nq),
        in_specs=[
            pl.BlockSpec((ATTN_TQ, HEAD_W), lambda b, h, t: (row0_q + b * nq + t, COL_QD + h)),
            pl.BlockSpec((DEC_SEQ, HEAD_W), lambda b, h, t: (row0_kv + b, COL_KD + h)),
            pl.BlockSpec((DEC_SEQ, HEAD_W), lambda b, h, t: (row0_kv + b, COL_VD + h)),
            cache, cache, table_q, table_q, table, table,
            small, small, small, small,
            pl.BlockSpec((1, HEAD_W), lambda b, h, t: (0, 0)),
        ],
        out_specs=pl.BlockSpec((ATTN_TQ, HEAD_W), lambda b, h, t: (b * nq + t, h)),
        out_shape=jax.ShapeDtypeStruct((T_SAMPLE, HEADS * HEAD_W), F32),
        scratch_shapes=[pltpu.VMEM((DEC_SEQ + PAST_LEN, HEAD_W), BF16),
                        pltpu.VMEM((DEC_SEQ + PAST_LEN, HEAD_W), BF16)],
        compiler_params=_cparams(("arbitrary", "arbitrary", "arbitrary")),
        name="diff_attn_sample",
    )(proj, proj, proj, cache_k, cache_v, cos, sin_signed, cos, sin_signed,
      lq1, lk1, lq2, lk2, subln_g)


def _rope_tables():
    t = np.arange(DEC_SEQ)
    row, colp = t // GRID_W, t % GRID_W
    lane = np.arange(LANES)
    pos = np.where(((lane // 32) % 2 == 0)[None, :], row[:, None], colp[:, None]).astype(np.float64)
    half = 16
    inv = (np.float32(ROPE_THETA) ** (-(np.arange(half, dtype=np.float32)) / np.float32(half))).astype(np.float32)
    ang = pos.astype(np.float32) * inv[lane % half][None, :]
    cos = np.cos(ang.astype(np.float64)).astype(np.float32)
    sin = np.sin(ang.astype(np.float64)).astype(np.float32)
    sign = np.where((lane % 32) < half, -1.0, 1.0).astype(np.float32)[None, :]
    return jnp.asarray(cos), jnp.asarray(sin * sign)


def _split_bf16(a):
    hi = a.astype(BF16)
    return hi, (a - hi.astype(F32)).astype(BF16)


def _mixer_tail(out, x, g1_ref, sc2_ref, sh2_ref, lng_ref, lnb_ref, wr_ref, br_ref,
                x1_ref, u2_ref, meta_ref, cnt_ref):
    x1 = _layer_norm(ALPHA * x + g1_ref[0] * out, lng_ref[...], lnb_ref[...])
    x1_ref[...] = x1
    u2 = x1 * (1.0 + sc2_ref[0]) + sh2_ref[0]
    u2_ref[...] = u2.astype(BF16)

    u_hi, u_lo = _split_bf16(u2)
    w_hi, w_lo = _split_bf16(wr_ref[...])
    logits = _dot(u_hi, w_hi) + (_dot(u_hi, w_lo) + _dot(u_lo, w_hi)) + br_ref[...]
    lane = lax.broadcasted_iota(jnp.int32, logits.shape, 1).astype(F32)
    neg = jnp.float32(-jnp.inf)
    is_g = lane < N_GROUPS
    gl = jnp.where(is_g, logits, neg)
    gmax = jnp.max(gl, axis=-1, keepdims=True)
    gsel = jnp.min(jnp.where(gl == gmax, lane, float(LANES)), axis=-1, keepdims=True)
    p_g = 1.0 / jnp.sum(jnp.where(is_g, jnp.exp(gl - gmax), 0.0), axis=-1, keepdims=True)
    lo = ROUTER_LANE0 + gsel * EXPERTS_PER_GROUP
    el = jnp.where((lane >= lo) & (lane < lo + EXPERTS_PER_GROUP), logits, neg)
    v1 = jnp.max(el, axis=-1, keepdims=True)
    i1 = jnp.min(jnp.where(el == v1, lane, float(LANES)), axis=-1, keepdims=True)
    el2 = jnp.where(lane == i1, neg, el)
    v2 = jnp.max(el2, axis=-1, keepdims=True)
    i2 = jnp.min(jnp.where(el2 == v2, lane, float(LANES)), axis=-1, keepdims=True)
    t = jnp.exp(v2 - v1)
    w1 = p_g / (1.0 + t)
    w2 = w1 * t

    oh1 = (lane == i1).astype(F32)
    oh2 = (lane == i2).astype(F32)
    oh = oh1 + oh2
    r_i = lax.broadcasted_iota(jnp.int32, (TM, TM), 0)
    c_i = lax.broadcasted_iota(jnp.int32, (TM, TM), 1)
    before = (c_i < r_i).astype(BF16)
    earlier = _dot(before, oh.astype(BF16))
    rank1 = jnp.sum(earlier * oh1, axis=-1, keepdims=True)
    rank2 = jnp.sum(earlier * oh2, axis=-1, keepdims=True)
    cnt_ref[0] = jnp.sum(oh, axis=0, keepdims=True)
    cols = (i1, i2, w1, w2, rank1, rank2)
    meta = jnp.zeros_like(logits)
    for k, col in enumerate(cols):
        meta = jnp.where(lane == k, col, meta)
    meta_ref[...] = meta


META_E1, META_E2, META_W1, META_W2, META_RANK1, META_RANK2 = range(6)

_TAIL_OUT_SHAPES = [
    jax.ShapeDtypeStruct((T, D), F32),
    jax.ShapeDtypeStruct((T, D), BF16),
    jax.ShapeDtypeStruct((T, LANES), F32),
    jax.ShapeDtypeStruct((N_TILES, 1, LANES), F32),
]


def _tail_out_specs():
    return [
        pl.BlockSpec((TM, D), lambda i: (i, 0)),
        pl.BlockSpec((TM, D), lambda i: (i, 0)),
        pl.BlockSpec((TM, LANES), lambda i: (i, 0)),
        pl.BlockSpec((1, 1, LANES), lambda i: (i, 0, 0)),
    ]


def _tail_in_specs(li):
    return [
        _mod_spec(li, 2), _mod_spec(li, 4), _mod_spec(li, 3),
        _row_spec((1, D)), _row_spec((1, D)),
        _row_spec((D, LANES)), _row_spec((1, LANES)),
    ]


def _out_proj_kernel(rp_ref, rs_ref, op_ref, os_ref, w_ref, xp_ref, xs_ref, *rest):
    half = HEADS * HEAD_W
    r = _pick_tile(rp_ref, rs_ref).astype(BF16)
    o = _pick_tile(op_ref, os_ref).astype(BF16)
    out = _dot(r, w_ref[0:half, :]) + _dot(o, w_ref[half:, :])
    _mixer_tail(out, _pick_tile(xp_ref, xs_ref), *rest)


def _out_proj_tail(r_p, r_s, o_p, o_s, w_out_bf16, x_prompt, x_sample, modr, ln_g, ln_b, wr, br, li):
    half = HEADS * HEAD_W
    return pl.pallas_call(
        _out_proj_kernel,
        grid=(N_TILES,),
        in_specs=[_prompt_tile_spec(half), _sample_tile_spec(half),
                  _prompt_tile_spec(half), _sample_tile_spec(half),
                  _row_spec((2 * half, D)),
                  _prompt_tile_spec(D), _sample_tile_spec(D)] + _tail_in_specs(li),
        out_specs=_tail_out_specs(),
        out_shape=_TAIL_OUT_SHAPES,
        compiler_params=_cparams(("arbitrary",)),
        name="out_proj_tail",
    )(r_p, r_s, o_p, o_s, w_out_bf16, x_prompt, x_sample, modr, modr, modr, ln_g, ln_b, wr, br)


def _conv_glu_kernel(x_ref, sh_ref, sc_ref, w_ref, b_ref, o_ref):
    u = x_ref[...] * (1.0 + sc_ref[0]) + sh_ref[0]
    h = _dot(u.astype(BF16), w_ref[...]) + b_ref[...]
    o_ref[...] = h[:, :D] * jax.nn.sigmoid(h[:, D:])


def _conv_glu(x, modr, w1_bf16, b1, li):
    return pl.pallas_call(
        _conv_glu_kernel,
        grid=(N_TILES,),
        in_specs=[pl.BlockSpec((TM, D), lambda i: (i, 0)), _mod_spec(li, 0), _mod_spec(li, 1),
                  _row_spec((D, 2 * D)), _row_spec((1, 2 * D))],
        out_specs=pl.BlockSpec((TM, D), lambda i: (i, 0)),
        out_shape=jax.ShapeDtypeStruct((T, D), F32),
        compiler_params=_cparams(("arbitrary",)),
        name="conv_glu",
    )(x, modr, modr, w1_bf16, b1)


HALO = 16
CONV_ROWS = 64
CONV_COLS = 128


def _depthwise_conv(hp, dw_ref, conv):
    base = HALO - CONV_PAD
    for cb in range(D // CONV_COLS):
        cs = slice(cb * CONV_COLS, (cb + 1) * CONV_COLS)
        for rb in range(TM // CONV_ROWS):
            r0 = rb * CONV_ROWS
            acc = None
            for shift in range(SUBLANES):
                part = None
                for tap in range(CONV_K):
                    off = base + tap
                    if off % SUBLANES != shift:
                        continue
                    a0 = r0 + off - shift
                    term = hp[a0:a0 + CONV_ROWS + SUBLANES, cs] * dw_ref[tap:tap + 1, cs]
                    part = term if part is None else part + term
                part = part[shift:shift + CONV_ROWS, :]
                acc = part if acc is None else acc + part
            conv[r0:r0 + CONV_ROWS, cs] = acc


def _conv_tail_kernel(cur_ref, prev_ref, next_ref, dw_ref, dwb_ref, cg_ref, cb_ref, w2_ref, b2_ref,
                      x_ref, *rest):
    tail_args, hp, conv = rest[:-2], rest[-2], rest[-1]
    i = pl.program_id(0)
    k = (i - PROMPT_TILES) % SAMPLE_TILES_PER_SEQ
    in_sample = i >= PROMPT_TILES
    left_ok = jnp.logical_and(in_sample, k != 0)
    right_ok = jnp.logical_and(in_sample, k != SAMPLE_TILES_PER_SEQ - 1)
    hp[0:HALO, :] = jnp.where(left_ok, prev_ref[...], 0.0)
    hp[HALO:HALO + TM, :] = cur_ref[...]
    hp[HALO + TM:HALO + TM + HALO, :] = jnp.where(right_ok, next_ref[...], 0.0)
    _depthwise_conv(hp, dw_ref, conv)
    hc = _silu(_layer_norm(conv[...] + dwb_ref[...], cg_ref[...], cb_ref[...]))
    out = _dot(hc.astype(BF16), w2_ref[...]) + b2_ref[...]
    _mixer_tail(out, x_ref[...], *tail_args)


def _conv_tail(glu, dw, dwb, cg, cb, w2_bf16, b2, x, modr, ln_g, ln_b, wr, br, li):
    per = TM // HALO
    last = T // HALO - 1
    return pl.pallas_call(
        _conv_tail_kernel,
        grid=(N_TILES,),
        in_specs=[pl.BlockSpec((TM, D), lambda i: (i, 0)),
                  pl.BlockSpec((HALO, D), lambda i: (jnp.maximum(i * per - 1, 0), 0)),
                  pl.BlockSpec((HALO, D), lambda i: (jnp.minimum((i + 1) * per, last), 0)),
                  _row_spec((CONV_K, D)), _row_spec((1, D)), _row_spec((1, D)), _row_spec((1, D)),
                  _row_spec((D, D)), _row_spec((1, D)),
                  pl.BlockSpec((TM, D), lambda i: (i, 0))] + _tail_in_specs(li),
        out_specs=_tail_out_specs(),
        out_shape=_TAIL_OUT_SHAPES,
        scratch_shapes=[pltpu.VMEM((TM + 2 * HALO, D), F32), pltpu.VMEM((TM, D), F32)],
        compiler_params=_cparams(("arbitrary",)),
        name="conv_tail",
    )(glu, glu, glu, dw, dwb, cg, cb, w2_bf16, b2, x, modr, modr, modr, ln_g, ln_b, wr, br)


def _sorted_positions(meta, srcv):
    lane = lax.broadcasted_iota(jnp.int32, meta.shape, 1).astype(F32)

    def pos(e_col, r_col):
        start = jnp.sum(jnp.where(lane == meta[:, e_col:e_col + 1], srcv, 0.0), axis=-1, keepdims=True)
        return start + meta[:, r_col:r_col + 1]

    return pos(META_E1, META_RANK1), pos(META_E2, META_RANK2)


def _one_hot_rows(pos):
    col = lax.broadcasted_iota(jnp.int32, (TM, SORT_ROWS), 1).astype(F32)
    return col == pos


def _for_each_row_group(tile, tot_ref, dstg_ref, fn):
    def body(g, carry):
        fn(pl.multiple_of(g * RUN_ALIGN, RUN_ALIGN),
           pl.multiple_of(dstg_ref[tile * SORT_GROUPS + g], RUN_ALIGN))
        return carry

    lax.fori_loop(0, tot_ref[tile] // RUN_ALIGN, body, 0)


def _wait_rows(total, make_copy):
    for bit in RUN_BITS:
        @pl.when((total & bit) != 0)
        def _(bit=bit):
            make_copy(bit).wait()


def _dispatch_kernel(tot_ref, dstg_ref, u_ref, meta_ref, srcv_ref, xs_ref, sorted_ref, sems):
    i = pl.program_id(0)
    slot = i % 2

    def wait_tile(tile, slot):
        buf = sorted_ref.at[slot]
        _wait_rows(tot_ref[tile], lambda rows: pltpu.make_async_copy(
            buf.at[pl.ds(0, rows)], xs_ref.at[pl.ds(0, rows)], sems.at[slot]))

    @pl.when(i >= 2)
    def _():
        wait_tile(i - 2, slot)

    pos1, pos2 = _sorted_positions(meta_ref[...], srcv_ref[0])
    select = jnp.logical_or(_one_hot_rows(pos1), _one_hot_rows(pos2)).astype(BF16)
    sorted_ref[slot] = _dot_tn(select, u_ref[...])
    buf = sorted_ref.at[slot]

    def start(src, dst):
        pltpu.make_async_copy(buf.at[pl.ds(src, RUN_ALIGN)], xs_ref.at[pl.ds(dst, RUN_ALIGN)],
                              sems.at[slot]).start()

    _for_each_row_group(i, tot_ref, dstg_ref, start)

    @pl.when(i == N_TILES - 1)
    def _():
        wait_tile(i - 1, 1 - slot)
        wait_tile(i, slot)


def _dispatch(sched, u2, meta, srcv):
    return pl.pallas_call(
        _dispatch_kernel,
        grid_spec=pltpu.PrefetchScalarGridSpec(
            num_scalar_prefetch=2,
            grid=(N_TILES,),
            in_specs=[pl.BlockSpec((TM, D), lambda i, *_: (i, 0)),
                      pl.BlockSpec((TM, LANES), lambda i, *_: (i, 0)),
                      pl.BlockSpec((1, 1, LANES), lambda i, *_: (i, 0, 0))],
            out_specs=pl.BlockSpec(memory_space=pl.ANY),
            scratch_shapes=[pltpu.VMEM((2, SORT_ROWS, D), F32), pltpu.SemaphoreType.DMA((2,))],
        ),
        out_shape=jax.ShapeDtypeStruct((MOE_ROWS, D), F32),
        compiler_params=_cparams(("arbitrary",)),
        name="moe_dispatch",
    )(*sched, u2, meta, srcv)


def _experts_kernel(start_ref, chunks_ref, xs_ref, wg_ref, wu_ref, wd_ref, ys_ref,
                    wg_bf, wu_bf, wd_bf, xbuf, ybuf, in_sems, out_sems):
    e = pl.program_id(0)
    n = chunks_ref[e]
    first = start_ref[e] // MOE_TM
    total = start_ref[N_EXPERTS - 1] // MOE_TM + chunks_ref[N_EXPERTS - 1]

    def rows(g):
        return pl.ds(pl.multiple_of(g * MOE_TM, MOE_TM), MOE_TM)

    def load(g):
        slot = g % MOE_IN_SLOTS
        return pltpu.make_async_copy(xs_ref.at[rows(g)], xbuf.at[slot], in_sems.at[slot])

    def store(g):
        slot = g % MOE_OUT_SLOTS
        return pltpu.make_async_copy(ybuf.at[slot], ys_ref.at[rows(g)], out_sems.at[slot])

    @pl.when(e == 0)
    def _():
        for g in range(MOE_AHEAD):
            @pl.when(g < total)
            def _(g=g):
                load(g).start()

    @pl.when(n > 0)
    def _():
        wg_bf[...] = wg_ref[0, 0].astype(BF16)
        wu_bf[...] = wu_ref[0, 0].astype(BF16)
        wd_bf[...] = wd_ref[0, 0].astype(BF16)

        def tile(g, carry):
            load(g).wait()

            @pl.when(g + MOE_AHEAD < total)
            def _():
                load(g + MOE_AHEAD).start()

            @pl.when(g >= MOE_OUT_SLOTS)
            def _():
                store(g - MOE_OUT_SLOTS).wait()

            x = xbuf[g % MOE_IN_SLOTS].astype(BF16)
            h = (_silu(_dot(x, wg_bf[...])) * _dot(x, wu_bf[...])).astype(BF16)
            ybuf[g % MOE_OUT_SLOTS] = _dot(h, wd_bf[...])
            store(g).start()
            return carry

        lax.fori_loop(first, first + n, tile, 0)

    @pl.when(e == N_EXPERTS - 1)
    def _():
        for back in range(MOE_OUT_SLOTS, 0, -1):
            @pl.when(total >= back)
            def _(back=back):
                store(total - back).wait()


def _experts(seg_start, seg_chunks, xs, w_gate, w_up, w_down, li):
    def weight(shape):
        return pl.BlockSpec((1, 1) + shape, lambda e, *_: (li, e, 0, 0))

    return pl.pallas_call(
        _experts_kernel,
        grid_spec=pltpu.PrefetchScalarGridSpec(
            num_scalar_prefetch=2,
            grid=(N_EXPERTS,),
            in_specs=[pl.BlockSpec(memory_space=pl.ANY),
                      weight((D, D_EXPERT)), weight((D, D_EXPERT)), weight((D_EXPERT, D))],
            out_specs=pl.BlockSpec(memory_space=pl.ANY),
            scratch_shapes=[pltpu.VMEM((D, D_EXPERT), BF16), pltpu.VMEM((D, D_EXPERT), BF16),
                            pltpu.VMEM((D_EXPERT, D), BF16),
                            pltpu.VMEM((MOE_IN_SLOTS, MOE_TM, D), F32),
                            pltpu.VMEM((MOE_OUT_SLOTS, MOE_TM, D), F32),
                            pltpu.SemaphoreType.DMA((MOE_IN_SLOTS,)),
                            pltpu.SemaphoreType.DMA((MOE_OUT_SLOTS,))],
        ),
        out_shape=jax.ShapeDtypeStruct((MOE_ROWS, D), F32),
        compiler_params=_cparams(("arbitrary",)),
        name="moe_experts",
    )(seg_start, seg_chunks, xs, w_gate, w_up, w_down)


def _combine_kernel(tot_ref, dstg_ref, ys_ref, x1_ref, meta_ref, srcv_ref, g2_ref,
                    lng_ref, lnb_ref, *rest, split):
    outs, (sorted_ref, sems) = rest[:-2], rest[-2:]
    i = pl.program_id(0)
    slot = i % 2

    def fetch(tile, slot):
        buf = sorted_ref.at[slot]

        def start(src, dst):
            pltpu.make_async_copy(ys_ref.at[pl.ds(dst, RUN_ALIGN)], buf.at[pl.ds(src, RUN_ALIGN)],
                                  sems.at[slot]).start()

        _for_each_row_group(tile, tot_ref, dstg_ref, start)

    @pl.when(i == 0)
    def _():
        sorted_ref[...] = jnp.zeros_like(sorted_ref)
        fetch(0, 0)

    @pl.when(i + 1 < N_TILES)
    def _():
        fetch(i + 1, 1 - slot)

    meta = meta_ref[...]
    pos1, pos2 = _sorted_positions(meta, srcv_ref[0])
    sel1 = _one_hot_rows(pos1).astype(BF16)
    sel2 = _one_hot_rows(pos2).astype(BF16)
    buf = sorted_ref.at[slot]
    _wait_rows(tot_ref[i], lambda rows: pltpu.make_async_copy(
        ys_ref.at[pl.ds(0, rows)], buf.at[pl.ds(0, rows)], sems.at[slot]))
    ysort = sorted_ref[slot].astype(BF16)
    f = (meta[:, META_W1:META_W1 + 1] * _dot(sel1, ysort)
         + meta[:, META_W2:META_W2 + 1] * _dot(sel2, ysort))
    y = _layer_norm(ALPHA * x1_ref[...] + g2_ref[0] * f, lng_ref[...], lnb_ref[...])
    if split:
        @pl.when(i < PROMPT_TILES)
        def _():
            outs[0][...] = y

        @pl.when(i >= PROMPT_TILES)
        def _():
            outs[1][...] = y
    else:
        outs[0][...] = y


def _combine(sched, ys, x1, meta, srcv, modr, ln_g, ln_b, li, split):
    if split:
        out_specs = [_prompt_tile_spec(D), _sample_tile_spec(D)]
        out_shape = [jax.ShapeDtypeStruct((T_PROMPT, D), F32), jax.ShapeDtypeStruct((T_SAMPLE, D), F32)]
    else:
        out_specs = [pl.BlockSpec((TM, D), lambda i, *_: (i, 0))]
        out_shape = [jax.ShapeDtypeStruct((T, D), F32)]
    return pl.pallas_call(
        functools.partial(_combine_kernel, split=split),
        grid_spec=pltpu.PrefetchScalarGridSpec(
            num_scalar_prefetch=2,
            grid=(N_TILES,),
            in_specs=[pl.BlockSpec(memory_space=pl.ANY),
                      pl.BlockSpec((TM, D), lambda i, *_: (i, 0)),
                      pl.BlockSpec((TM, LANES), lambda i, *_: (i, 0)),
                      pl.BlockSpec((1, 1, LANES), lambda i, *_: (i, 0, 0)),
                      _mod_spec(li, 5), _row_spec((1, D)), _row_spec((1, D))],
            out_specs=out_specs,
            scratch_shapes=[pltpu.VMEM((2, SORT_ROWS, D), F32), pltpu.SemaphoreType.DMA((2,))],
        ),
        out_shape=out_shape,
        compiler_params=_cparams(("arbitrary",)),
        name="moe_combine",
    )(*sched, ys, x1, meta, srcv, modr, ln_g, ln_b)


def _moe_schedule(tile_counts):
    n = (tile_counts + RUN_ALIGN - 1) // RUN_ALIGN * RUN_ALIGN
    src = jnp.cumsum(n, axis=1) - n
    per_expert = jnp.sum(n, axis=0)
    seg = (per_expert + MOE_TM - 1) // MOE_TM * MOE_TM
    seg_start = jnp.cumsum(seg) - seg
    dst = seg_start[None, :] + jnp.cumsum(n, axis=0) - n
    g_row = jnp.arange(SORT_GROUPS, dtype=jnp.int32) * RUN_ALIGN
    run_of_g = jnp.sum(((src + n)[:, None, :] <= g_row[None, :, None]).astype(jnp.int32), axis=2)
    run_of_g = jnp.minimum(run_of_g, N_EXPERTS - 1)
    dst_g = (jnp.take_along_axis(dst, run_of_g, axis=1) + g_row[None, :]
             - jnp.take_along_axis(src, run_of_g, axis=1))
    runs = (jnp.sum(n, axis=1), dst_g.reshape(-1))
    srcv = jnp.pad(src.astype(F32), ((0, 0), (ROUTER_LANE0, LANES - ROUTER_LANE0 - N_EXPERTS)))
    return runs, srcv.reshape(N_TILES, 1, LANES), (seg_start, seg // MOE_TM)


def _moe(x1, u2, meta, cnt, modr, w_gate, w_up, w_down, ln_g, ln_b, li, split):
    tile_counts = cnt[:, 0, ROUTER_LANE0:ROUTER_LANE0 + N_EXPERTS].astype(jnp.int32)
    runs, srcv, (seg_start, seg_chunks) = _moe_schedule(tile_counts)
    xs = _dispatch(runs, u2, meta, srcv)
    ys = _experts(seg_start, seg_chunks, xs, w_gate, w_up, w_down, li)
    return _combine(runs, ys, x1, meta, srcv, modr, ln_g, ln_b, li, split)


def _router_slab(wg, bg, we, be):
    w = jnp.concatenate([wg, we.transpose(1, 0, 2).reshape(D, N_EXPERTS)], axis=1)
    b = jnp.concatenate([bg, be.reshape(N_EXPERTS)])
    pad = LANES - w.shape[1]
    return jnp.pad(w, ((0, 0), (0, pad))), jnp.pad(b, (0, pad)).reshape(1, LANES)


def kernel(x_prompt, x_sample, cache_diff_k, cache_diff_v, state_ret_fwd, state_ret_bwd, c, c_ctx, mod_w, mod_b, ln1_g, ln1_b, ln2_g, ln2_b, mix_w_in, mix_w_out, ret_decay_fwd, ret_decay_bwd, diff_lq1, diff_lk1, diff_lq2, diff_lk2, diff_subln_g, conv_w1, conv_b1, conv_dw, conv_dw_b, conv_ln_g, conv_ln_b, conv_w2, conv_b2, router_g_w, router_g_b, router_e_w, router_e_b, moe_w_gate, moe_w_up, moe_w_down):
    xp = x_prompt.reshape(T_PROMPT, D)
    xs = x_sample.reshape(T_SAMPLE, D)
    cond = jnp.concatenate([c_ctx[None, :], c, jnp.zeros((MOD_ROWS - 1 - DEC_BATCH, D), F32)], axis=0)
    modr = _mod_vectors(cond, mod_w, mod_b).reshape(DEPTH * MOD_ROWS * 6, 1, D)
    cos, sin_signed = _rope_tables()

    def row(v):
        return v.reshape(1, -1)

    x = None
    caches = None
    for li in range(DEPTH):
        wr, br = _router_slab(router_g_w[li], router_g_b[li], router_e_w[li], router_e_b[li])
        if li % 2 == 0:
            assert li == 0, "the even mixer reads the kernel inputs directly"
            e = li // 2
            lam_init = 0.8 - 0.6 * math.exp(-0.3 * li)
            proj, ck, cv = _in_proj(xp, xs, modr, mix_w_in[e].astype(BF16), li)
            dec = jnp.concatenate([ret_decay_fwd[e], ret_decay_bwd[e]])
            r_p, sf, sb = _retention(proj, dec, BATCH, SEQ, 0, emit_state=True)
            (r_s,) = _retention(proj, dec, DEC_BATCH, DEC_SEQ, T_PROMPT // DEC_SEQ,
                                s0f=state_ret_fwd, s0b=state_ret_bwd, e=e)
            lams = (row(diff_lq1[e]), row(diff_lk1[e]), row(diff_lq2[e]), row(diff_lk2[e]),
                    row(diff_subln_g[e]))
            o_p = _attn_prompt(proj, *lams, lam_init)
            o_s = _attn_sample(proj, cache_diff_k, cache_diff_v, cos, sin_signed, *lams, lam_init, e)
            x1, u2, meta, cnt = _out_proj_tail(r_p, r_s, o_p, o_s, mix_w_out[e].astype(BF16), xp, xs, modr,
                                               row(ln1_g[li]), row(ln1_b[li]), wr, br, li)
            caches = (ck, cv, sf, sb)
        else:
            o = li // 2
            glu = _conv_glu(x, modr, conv_w1[o].astype(BF16), row(conv_b1[o]), li)
            x1, u2, meta, cnt = _conv_tail(glu, conv_dw[o], row(conv_dw_b[o]), row(conv_ln_g[o]),
                                           row(conv_ln_b[o]), conv_w2[o].astype(BF16), row(conv_b2[o]),
                                           x, modr, row(ln1_g[li]), row(ln1_b[li]), wr, br, li)
        outs = _moe(x1, u2, meta, cnt, modr, moe_w_gate, moe_w_up, moe_w_down,
                    row(ln2_g[li]), row(ln2_b[li]), li, split=(li == DEPTH - 1))
        x = outs[0]

    y_prompt = outs[0].reshape(BATCH, SEQ, D)
    y_sample = outs[1].reshape(DEC_BATCH, DEC_SEQ, D)
    return (y_prompt, y_sample) + caches
```

```python
import functools
import math

import numpy as np
import jax
import jax.numpy as jnp
from jax import lax
from jax.experimental import pallas as pl
from jax.experimental.pallas import tpu as pltpu

F32 = jnp.float32
BF16 = jnp.bfloat16

D = 1024
BATCH = 16
SEQ = 256
DEPTH = 2
DEC_BATCH = 2
DEC_SEQ = 2048
PAST_LEN = 512
GRID_W = 64
HEADS = 4
HEAD_W = 128
RET_CHUNK = 128
DIFF_DK = 64
ROPE_THETA = 10000.0
IN_W = 7 * HEADS * HEAD_W
CONV_K = 31
CONV_PAD = CONV_K // 2
N_GROUPS = 4
EXPERTS_PER_GROUP = 8
N_EXPERTS = N_GROUPS * EXPERTS_PER_GROUP
D_EXPERT = 512
ALPHA = (2.0 * DEPTH) ** 0.25
LN_EPS = 1e-5
GN_EPS = 1e-6

T_PROMPT = BATCH * SEQ
T_SAMPLE = DEC_BATCH * DEC_SEQ
T = T_PROMPT + T_SAMPLE
TM = 256
N_TILES = T // TM
PROMPT_TILES = T_PROMPT // TM
SAMPLE_TILES_PER_SEQ = DEC_SEQ // TM
MOD_ROWS = 8
MOE_TM = 256
LANES = 128
SUBLANES = 8
RUN_ALIGN = SUBLANES
SORT_ROWS = -(-(2 * TM + N_EXPERTS * (RUN_ALIGN - 1)) // TM) * TM
RUN_BITS = tuple(1 << b for b in range((2 * TM).bit_length() - 1, RUN_ALIGN.bit_length() - 2, -1))
MOE_MAX_TILES = -(-(2 * T + N_TILES * N_EXPERTS * (RUN_ALIGN - 1) + N_EXPERTS * (MOE_TM - RUN_ALIGN)) // MOE_TM)
MOE_ROWS = MOE_MAX_TILES * MOE_TM
SORT_GROUPS = SORT_ROWS // RUN_ALIGN
MOE_AHEAD = 2
MOE_IN_SLOTS = MOE_AHEAD + 1
MOE_OUT_SLOTS = 2
ROUTER_LANE0 = N_GROUPS
VMEM_LIMIT = 52 * 1024 * 1024


def _cparams(sem):
    return pltpu.CompilerParams(dimension_semantics=sem, vmem_limit_bytes=VMEM_LIMIT)


def _tile_cond_row(i):
    return jnp.where(i < PROMPT_TILES, 0, 1 + (i - PROMPT_TILES) // SAMPLE_TILES_PER_SEQ)


def _mod_spec(li, k):
    return pl.BlockSpec((1, 1, D), lambda i, *_: ((li * MOD_ROWS + _tile_cond_row(i)) * 6 + k, 0, 0))


def _row_spec(shape):
    return pl.BlockSpec(shape, lambda i, *_: (0,) * len(shape))


def _layer_norm(x, g, b):
    mu = jnp.mean(x, axis=-1, keepdims=True)
    xc = x - mu
    var = jnp.mean(xc * xc, axis=-1, keepdims=True)
    return xc * lax.rsqrt(var + LN_EPS) * g + b


def _silu(x):
    return x * jax.nn.sigmoid(x)


def _dot(a, b):
    return jnp.dot(a, b, preferred_element_type=F32)


def _dot_nt(a, b):
    return lax.dot_general(a, b, (((1,), (1,)), ((), ())), preferred_element_type=F32)


def _dot_tn(a, b):
    return lax.dot_general(a, b, (((0,), (0,)), ((), ())), preferred_element_type=F32)


MOD_TN = 512


def _mod_kernel(cond_ref, w_ref, b_ref, o_ref):
    s = _silu(cond_ref[...])
    o_ref[0] = jnp.dot(s, w_ref[0], precision=lax.Precision.HIGHEST,
                       preferred_element_type=F32) + b_ref[0]


def _mod_vectors(cond, mod_w, mod_b):
    return pl.pallas_call(
        _mod_kernel,
        grid=(DEPTH, 6 * D // MOD_TN),
        in_specs=[
            pl.BlockSpec((MOD_ROWS, D), lambda l, j: (0, 0)),
            pl.BlockSpec((1, D, MOD_TN), lambda l, j: (l, 0, j)),
            pl.BlockSpec((1, 1, MOD_TN), lambda l, j: (l, 0, j)),
        ],
        out_specs=pl.BlockSpec((1, MOD_ROWS, MOD_TN), lambda l, j: (l, 0, j)),
        out_shape=jax.ShapeDtypeStruct((DEPTH, MOD_ROWS, 6 * D), F32),
        compiler_params=_cparams(("arbitrary", "arbitrary")),
        name="mod_vectors",
    )(cond, mod_w, mod_b.reshape(DEPTH, 1, 6 * D))


def _prompt_tile_spec(width):
    return pl.BlockSpec((TM, width), lambda i, *_: (jnp.minimum(i, PROMPT_TILES - 1), 0))


def _sample_tile_spec(width):
    return pl.BlockSpec((TM, width), lambda i, *_: (jnp.maximum(i - PROMPT_TILES, 0), 0))


def _pick_tile(prompt_ref, sample_ref):
    return jnp.where(pl.program_id(0) < PROMPT_TILES, prompt_ref[...], sample_ref[...])


def _in_proj_kernel(xp_ref, xs_ref, sh_ref, sc_ref, w_ref, o_ref, ck_ref, cv_ref):
    u = _pick_tile(xp_ref, xs_ref) * (1.0 + sc_ref[0]) + sh_ref[0]
    proj = _dot(u.astype(BF16), w_ref[...])
    o_ref[...] = proj

    @pl.when(pl.program_id(0) < PROMPT_TILES)
    def _():
        for h in range(HEADS):
            ck_ref[0, 0, h] = proj[:, (COL_KD + h) * HEAD_W:(COL_KD + h + 1) * HEAD_W]
            cv_ref[0, 0, h] = proj[:, (COL_VD + h) * HEAD_W:(COL_VD + h + 1) * HEAD_W]


def _in_proj(x_prompt, x_sample, modr, w_in_bf16, li):
    cache_spec = pl.BlockSpec((1, 1, HEADS, SEQ, HEAD_W),
                              lambda i: (jnp.minimum(i, PROMPT_TILES - 1), 0, 0, 0, 0))
    cache_shape = jax.ShapeDtypeStruct((BATCH, 1, HEADS, SEQ, HEAD_W), F32)
    return pl.pallas_call(
        _in_proj_kernel,
        grid=(N_TILES,),
        in_specs=[
            _prompt_tile_spec(D), _sample_tile_spec(D),
            _mod_spec(li, 0),
            _mod_spec(li, 1),
            _row_spec((D, IN_W)),
        ],
        out_specs=[pl.BlockSpec((TM, IN_W), lambda i: (i, 0)), cache_spec, cache_spec],
        out_shape=[jax.ShapeDtypeStruct((T, IN_W), F32), cache_shape, cache_shape],
        compiler_params=_cparams(("arbitrary",)),
        name="in_proj",
    )(x_prompt, x_sample, modr, modr, w_in_bf16)


COL_QR, COL_KR, COL_VR, COL_GR, COL_QD, COL_KD, COL_VD = (k * HEADS for k in range(7))


def _retention_kernel(dec_ref, q_ref, k_ref, v_ref, g_ref, *rest, n_chunks, has_state, emit_state):
    rest = list(rest)
    if has_state:
        s0f_ref, s0b_ref = rest[:2]
        rest = rest[2:]
    r_ref = rest[0]
    rest = rest[1:]
    if emit_state:
        sf_ref, sb_ref = rest[:2]
        rest = rest[2:]
    of_ref = rest[0]

    h = pl.program_id(1)
    C = RET_CHUNK
    ii = lax.broadcasted_iota(jnp.int32, (C, C), 0)
    jj = lax.broadcasted_iota(jnp.int32, (C, C), 1)
    rel = (ii - jj).astype(F32)
    idx = lax.broadcasted_iota(jnp.int32, (C, 1), 0).astype(F32)
    k_scale = HEAD_W ** -0.5

    def chunk(ref, c):
        return ref[c * C:(c + 1) * C, :]

    def run(direction):
        lg = -jnp.exp(jnp.full((1, 1), dec_ref[direction * HEADS + h], F32))
        if direction == 0:
            inner = jnp.where(rel >= 0, jnp.exp(jnp.maximum(rel, 0.0) * lg), 0.0)
            q_decay = jnp.exp((idx + 1.0) * lg)
            k_decay = jnp.exp((C - 1.0 - idx) * lg)
            order = range(n_chunks)
        else:
            inner = jnp.where(rel <= 0, jnp.exp(jnp.maximum(-rel, 0.0) * lg), 0.0)
            q_decay = jnp.exp((C - idx) * lg)
            k_decay = jnp.exp(idx * lg)
            order = range(n_chunks - 1, -1, -1)
        chunk_decay = jnp.exp(C * lg)
        if has_state:
            s = (s0f_ref if direction == 0 else s0b_ref)[0, 0, 0]
        else:
            s = jnp.zeros((HEAD_W, HEAD_W), F32)
        for c in order:
            qc = chunk(q_ref, c)
            kc = chunk(k_ref, c) * k_scale
            vc = chunk(v_ref, c).astype(BF16)
            scores = _dot_nt(qc.astype(BF16), kc.astype(BF16)) * inner
            o = _dot(scores.astype(BF16), vc) + _dot((qc * q_decay).astype(BF16), s.astype(BF16))
            s = s * chunk_decay + _dot_tn((kc * k_decay).astype(BF16), vc)
            if direction == 0:
                of_ref[c * C:(c + 1) * C, :] = o
            else:
                r = of_ref[c * C:(c + 1) * C, :] + o
                mu = jnp.mean(r, axis=-1, keepdims=True)
                rc = r - mu
                var = jnp.mean(rc * rc, axis=-1, keepdims=True)
                rn = rc * lax.rsqrt(var + GN_EPS)
                r_ref[c * C:(c + 1) * C, :] = _silu(chunk(g_ref, c)) * rn
        return s

    sf = run(0)
    sb = run(1)
    if emit_state:
        sf_ref[0, 0, 0] = sf
        sb_ref[0, 0, 0] = sb


def _retention(proj, dec, n_seq, seq_len, row_block0, s0f=None, s0b=None, e=0, emit_state=False):
    has_state = s0f is not None

    def col(base):
        return pl.BlockSpec((seq_len, HEAD_W), lambda b, h, *_: (row_block0 + b, base + h))

    state_spec = pl.BlockSpec((1, 1, 1, HEAD_W, HEAD_W), lambda b, h, *_: (b, e, h, 0, 0))
    in_specs = [pl.BlockSpec(memory_space=pltpu.SMEM), col(COL_QR), col(COL_KR), col(COL_VR), col(COL_GR)]
    args = [dec, proj, proj, proj, proj]
    if has_state:
        in_specs += [state_spec, state_spec]
        args += [s0f, s0b]
    out_specs = [pl.BlockSpec((seq_len, HEAD_W), lambda b, h, *_: (b, h))]
    out_shape = [jax.ShapeDtypeStruct((n_seq * seq_len, HEADS * HEAD_W), F32)]
    if emit_state:
        st = pl.BlockSpec((1, 1, 1, HEAD_W, HEAD_W), lambda b, h, *_: (b, 0, h, 0, 0))
        out_specs += [st, st]
        out_shape += [jax.ShapeDtypeStruct((n_seq, 1, HEADS, HEAD_W, HEAD_W), F32)] * 2
    return pl.pallas_call(
        functools.partial(_retention_kernel, n_chunks=seq_len // RET_CHUNK,
                          has_state=has_state, emit_state=emit_state),
        grid=(n_seq, HEADS),
        in_specs=in_specs,
        out_specs=out_specs,
        out_shape=out_shape,
        scratch_shapes=[pltpu.VMEM((seq_len, HEAD_W), F32)],
        compiler_params=_cparams(("arbitrary", "arbitrary")),
        name=f"retention_{seq_len}",
    )(*args)


def _diff_lambda(lq1_ref, lk1_ref, lq2_ref, lk2_ref, lam_init):
    a = jnp.sum(lq1_ref[...] * lk1_ref[...], axis=-1, keepdims=True)
    b = jnp.sum(lq2_ref[...] * lk2_ref[...], axis=-1, keepdims=True)
    return jnp.exp(a) - jnp.exp(b) + lam_init


LOG2E = 1.4426950408889634


def _diff_attend(q, k, v, lam, subln_g, lam_init):
    lane = lax.broadcasted_iota(jnp.int32, q.shape, 1)
    q1 = jnp.where(lane < DIFF_DK, q, 0.0).astype(BF16)
    q2 = jnp.where(lane >= DIFF_DK, q, 0.0).astype(BF16)

    def softmax_times_v(qz):
        s = _dot_nt(qz, k)
        p = jnp.exp2(s - jnp.max(s, axis=-1, keepdims=True))
        return _dot(p.astype(BF16), v) * (1.0 / jnp.sum(p, axis=-1, keepdims=True))

    o = softmax_times_v(q1) - lam * softmax_times_v(q2)
    o = o * lax.rsqrt(jnp.mean(o * o, axis=-1, keepdims=True) + LN_EPS)
    return o * subln_g * (1.0 - lam_init)


def _attn_prompt_kernel(q_ref, k_ref, v_ref, lq1, lk1, lq2, lk2, g_ref, o_ref, *, lam_init):
    lam = _diff_lambda(lq1, lk1, lq2, lk2, lam_init)
    scale = DIFF_DK ** -0.5 * LOG2E
    for h in range(HEADS):
        sl = slice(h * HEAD_W, (h + 1) * HEAD_W)
        o_ref[:, sl] = _diff_attend(q_ref[:, sl] * scale, k_ref[:, sl].astype(BF16),
                                    v_ref[:, sl].astype(BF16), lam, g_ref[...], lam_init)


def _attn_prompt(proj, lq1, lk1, lq2, lk2, subln_g, lam_init):
    W = HEADS * HEAD_W

    def slab(base):
        return pl.BlockSpec((SEQ, W), lambda b: (b, base // HEADS))

    small = _row_spec((1, DIFF_DK))
    return pl.pallas_call(
        functools.partial(_attn_prompt_kernel, lam_init=lam_init),
        grid=(BATCH,),
        in_specs=[slab(COL_QD), slab(COL_KD), slab(COL_VD), small, small, small, small,
                  _row_spec((1, HEAD_W))],
        out_specs=pl.BlockSpec((SEQ, W), lambda b: (b, 0)),
        out_shape=jax.ShapeDtypeStruct((T_PROMPT, W), F32),
        compiler_params=_cparams(("arbitrary",)),
        name="diff_attn_prompt",
    )(proj, proj, proj, lq1, lk1, lq2, lk2, subln_g)


def _rope(x, cos, sin_signed):
    lane = lax.broadcasted_iota(jnp.int32, x.shape, 1)
    partner = jnp.where((lane % 32) < 16, pltpu.roll(x, LANES - 16, 1), pltpu.roll(x, 16, 1))
    return x * cos + partner * sin_signed


def _attn_sample_kernel(q_ref, k_ref, v_ref, ck_ref, cv_ref, cosq_ref, sinq_ref, cos_ref, sin_ref,
                        lq1, lk1, lq2, lk2, g_ref, o_ref, kbuf, vbuf, *, lam_init):
    @pl.when(pl.program_id(2) == 0)
    def _():
        kbuf[0:DEC_SEQ, :] = _rope(k_ref[...], cos_ref[...], sin_ref[...]).astype(BF16)
        kbuf[DEC_SEQ:, :] = ck_ref[0, 0, 0].astype(BF16)
        vbuf[0:DEC_SEQ, :] = v_ref[...].astype(BF16)
        vbuf[DEC_SEQ:, :] = cv_ref[0, 0, 0].astype(BF16)

    lam = _diff_lambda(lq1, lk1, lq2, lk2, lam_init)
    q = _rope(q_ref[...], cosq_ref[...], sinq_ref[...]) * (DIFF_DK ** -0.5 * LOG2E)
    o_ref[...] = _diff_attend(q, kbuf[...], vbuf[...], lam, g_ref[...], lam_init)


ATTN_TQ = 256


def _attn_sample(proj, cache_k, cache_v, cos, sin_signed, lq1, lk1, lq2, lk2, subln_g, lam_init, e):
    nq = DEC_SEQ // ATTN_TQ
    row0_q = T_PROMPT // ATTN_TQ
    row0_kv = T_PROMPT // DEC_SEQ
    small = pl.BlockSpec((1, DIFF_DK), lambda b, h, t: (0, 0))
    cache = pl.BlockSpec((1, 1, 1, PAST_LEN, HEAD_W), lambda b, h, t: (b, e, h, 0, 0))
    table_q = pl.BlockSpec((ATTN_TQ, HEAD_W), lambda b, h, t: (t, 0))
    table = pl.BlockSpec((DEC_SEQ, HEAD_W), lambda b, h, t: (0, 0))
    return pl.pallas_call(
        functools.partial(_attn_sample_kernel, lam_init=lam_init),
        grid=(DEC_BATCH, HEADS, nq),
        in_specs=[
            pl.BlockSpec((ATTN_TQ, HEAD_W), lambda b, h, t: (row0_q + b * nq + t, COL_QD + h)),
            pl.BlockSpec((DEC_SEQ, HEAD_W), lambda b, h, t: (row0_kv + b, COL_KD + h)),
            pl.BlockSpec((DEC_SEQ, HEAD_W), lambda b, h, t: (row0_kv + b, COL_VD + h)),
            cache, cache, table_q, table_q, table, table,
            small, small, small, small,
            pl.BlockSpec((1, HEAD_W), lambda b, h, t: (0, 0)),
        ],
        out_specs=pl.BlockSpec((ATTN_TQ, HEAD_W), lambda b, h, t: (b * nq + t, h)),
        out_shape=jax.ShapeDtypeStruct((T_SAMPLE, HEADS * HEAD_W), F32),
        scratch_shapes=[pltpu.VMEM((DEC_SEQ + PAST_LEN, HEAD_W), BF16),
                        pltpu.VMEM((DEC_SEQ + PAST_LEN, HEAD_W), BF16)],
        compiler_params=_cparams(("arbitrary", "arbitrary", "arbitrary")),
        name="diff_attn_sample",
    )(proj, proj, proj, cache_k, cache_v, cos, sin_signed, cos, sin_signed,
      lq1, lk1, lq2, lk2, subln_g)


def _rope_tables():
    t = np.arange(DEC_SEQ)
    row, colp = t // GRID_W, t % GRID_W
    lane = np.arange(LANES)
    pos = np.where(((lane // 32) % 2 == 0)[None, :], row[:, None], colp[:, None]).astype(np.float64)
    half = 16
    inv = (np.float32(ROPE_THETA) ** (-(np.arange(half, dtype=np.float32)) / np.float32(half))).astype(np.float32)
    ang = pos.astype(np.float32) * inv[lane % half][None, :]
    cos = np.cos(ang.astype(np.float64)).astype(np.float32)
    sin = np.sin(ang.astype(np.float64)).astype(np.float32)
    sign = np.where((lane % 32) < half, -1.0, 1.0).astype(np.float32)[None, :]
    return jnp.asarray(cos), jnp.asarray(sin * sign)


def _split_bf16(a):
    hi = a.astype(BF16)
    return hi, (a - hi.astype(F32)).astype(BF16)


def _mixer_tail(out, x, g1_ref, sc2_ref, sh2_ref, lng_ref, lnb_ref, wr_ref, br_ref,
                x1_ref, u2_ref, meta_ref, cnt_ref):
    x1 = _layer_norm(ALPHA * x + g1_ref[0] * out, lng_ref[...], lnb_ref[...])
    x1_ref[...] = x1
    u2 = x1 * (1.0 + sc2_ref[0]) + sh2_ref[0]
    u2_ref[...] = u2.astype(BF16)

    u_hi, u_lo = _split_bf16(u2)
    w_hi, w_lo = _split_bf16(wr_ref[...])
    logits = _dot(u_hi, w_hi) + (_dot(u_hi, w_lo) + _dot(u_lo, w_hi)) + br_ref[...]
    lane = lax.broadcasted_iota(jnp.int32, logits.shape, 1).astype(F32)
    neg = jnp.float32(-jnp.inf)
    is_g = lane < N_GROUPS
    gl = jnp.where(is_g, logits, neg)
    gmax = jnp.max(gl, axis=-1, keepdims=True)
    gsel = jnp.min(jnp.where(gl == gmax, lane, float(LANES)), axis=-1, keepdims=True)
    p_g = 1.0 / jnp.sum(jnp.where(is_g, jnp.exp(gl - gmax), 0.0), axis=-1, keepdims=True)
    lo = ROUTER_LANE0 + gsel * EXPERTS_PER_GROUP
    el = jnp.where((lane >= lo) & (lane < lo + EXPERTS_PER_GROUP), logits, neg)
    v1 = jnp.max(el, axis=-1, keepdims=True)
    i1 = jnp.min(jnp.where(el == v1, lane, float(LANES)), axis=-1, keepdims=True)
    el2 = jnp.where(lane == i1, neg, el)
    v2 = jnp.max(el2, axis=-1, keepdims=True)
    i2 = jnp.min(jnp.where(el2 == v2, lane, float(LANES)), axis=-1, keepdims=True)
    t = jnp.exp(v2 - v1)
    w1 = p_g / (1.0 + t)
    w2 = w1 * t

    oh1 = (lane == i1).astype(F32)
    oh2 = (lane == i2).astype(F32)
    oh = oh1 + oh2
    r_i = lax.broadcasted_iota(jnp.int32, (TM, TM), 0)
    c_i = lax.broadcasted_iota(jnp.int32, (TM, TM), 1)
    before = (c_i < r_i).astype(BF16)
    earlier = _dot(before, oh.astype(BF16))
    rank1 = jnp.sum(earlier * oh1, axis=-1, keepdims=True)
    rank2 = jnp.sum(earlier * oh2, axis=-1, keepdims=True)
    cnt_ref[0] = jnp.sum(oh, axis=0, keepdims=True)
    cols = (i1, i2, w1, w2, rank1, rank2)
    meta = jnp.zeros_like(logits)
    for k, col in enumerate(cols):
        meta = jnp.where(lane == k, col, meta)
    meta_ref[...] = meta


META_E1, META_E2, META_W1, META_W2, META_RANK1, META_RANK2 = range(6)

_TAIL_OUT_SHAPES = [
    jax.ShapeDtypeStruct((T, D), F32),
    jax.ShapeDtypeStruct((T, D), BF16),
    jax.ShapeDtypeStruct((T, LANES), F32),
    jax.ShapeDtypeStruct((N_TILES, 1, LANES), F32),
]


def _tail_out_specs():
    return [
        pl.BlockSpec((TM, D), lambda i: (i, 0)),
        pl.BlockSpec((TM, D), lambda i: (i, 0)),
        pl.BlockSpec((TM, LANES), lambda i: (i, 0)),
        pl.BlockSpec((1, 1, LANES), lambda i: (i, 0, 0)),
    ]


def _tail_in_specs(li):
    return [
        _mod_spec(li, 2), _mod_spec(li, 4), _mod_spec(li, 3),
        _row_spec((1, D)), _row_spec((1, D)),
        _row_spec((D, LANES)), _row_spec((1, LANES)),
    ]


def _out_proj_kernel(rp_ref, rs_ref, op_ref, os_ref, w_ref, xp_ref, xs_ref, *rest):
    half = HEADS * HEAD_W
    r = _pick_tile(rp_ref, rs_ref).astype(BF16)
    o = _pick_tile(op_ref, os_ref).astype(BF16)
    out = _dot(r, w_ref[0:half, :]) + _dot(o, w_ref[half:, :])
    _mixer_tail(out, _pick_tile(xp_ref, xs_ref), *rest)


def _out_proj_tail(r_p, r_s, o_p, o_s, w_out_bf16, x_prompt, x_sample, modr, ln_g, ln_b, wr, br, li):
    half = HEADS * HEAD_W
    return pl.pallas_call(
        _out_proj_kernel,
        grid=(N_TILES,),
        in_specs=[_prompt_tile_spec(half), _sample_tile_spec(half),
                  _prompt_tile_spec(half), _sample_tile_spec(half),
                  _row_spec((2 * half, D)),
                  _prompt_tile_spec(D), _sample_tile_spec(D)] + _tail_in_specs(li),
        out_specs=_tail_out_specs(),
        out_shape=_TAIL_OUT_SHAPES,
        compiler_params=_cparams(("arbitrary",)),
        name="out_proj_tail",
    )(r_p, r_s, o_p, o_s, w_out_bf16, x_prompt, x_sample, modr, modr, modr, ln_g, ln_b, wr, br)


def _conv_glu_kernel(x_ref, sh_ref, sc_ref, w_ref, b_ref, o_ref):
    u = x_ref[...] * (1.0 + sc_ref[0]) + sh_ref[0]
    h = _dot(u.astype(BF16), w_ref[...]) + b_ref[...]
    o_ref[...] = h[:, :D] * jax.nn.sigmoid(h[:, D:])


def _conv_glu(x, modr, w1_bf16, b1, li):
    return pl.pallas_call(
        _conv_glu_kernel,
        grid=(N_TILES,),
        in_specs=[pl.BlockSpec((TM, D), lambda i: (i, 0)), _mod_spec(li, 0), _mod_spec(li, 1),
                  _row_spec((D, 2 * D)), _row_spec((1, 2 * D))],
        out_specs=pl.BlockSpec((TM, D), lambda i: (i, 0)),
        out_shape=jax.ShapeDtypeStruct((T, D), F32),
        compiler_params=_cparams(("arbitrary",)),
        name="conv_glu",
    )(x, modr, modr, w1_bf16, b1)


HALO = 16
CONV_ROWS = 64
CONV_COLS = 128


def _depthwise_conv(hp, dw_ref, conv):
    base = HALO - CONV_PAD
    for cb in range(D // CONV_COLS):
        cs = slice(cb * CONV_COLS, (cb + 1) * CONV_COLS)
        for rb in range(TM // CONV_ROWS):
            r0 = rb * CONV_ROWS
            acc = None
            for shift in range(SUBLANES):
                part = None
                for tap in range(CONV_K):
                    off = base + tap
                    if off % SUBLANES != shift:
                        continue
                    a0 = r0 + off - shift
                    term = hp[a0:a0 + CONV_ROWS + SUBLANES, cs] * dw_ref[tap:tap + 1, cs]
                    part = term if part is None else part + term
                part = part[shift:shift + CONV_ROWS, :]
                acc = part if acc is None else acc + part
            conv[r0:r0 + CONV_ROWS, cs] = acc


def _conv_tail_kernel(cur_ref, prev_ref, next_ref, dw_ref, dwb_ref, cg_ref, cb_ref, w2_ref, b2_ref,
                      x_ref, *rest):
    tail_args, hp, conv = rest[:-2], rest[-2], rest[-1]
    i = pl.program_id(0)
    k = (i - PROMPT_TILES) % SAMPLE_TILES_PER_SEQ
    in_sample = i >= PROMPT_TILES
    left_ok = jnp.logical_and(in_sample, k != 0)
    right_ok = jnp.logical_and(in_sample, k != SAMPLE_TILES_PER_SEQ - 1)
    hp[0:HALO, :] = jnp.where(left_ok, prev_ref[...], 0.0)
    hp[HALO:HALO + TM, :] = cur_ref[...]
    hp[HALO + TM:HALO + TM + HALO, :] = jnp.where(right_ok, next_ref[...], 0.0)
    _depthwise_conv(hp, dw_ref, conv)
    hc = _silu(_layer_norm(conv[...] + dwb_ref[...], cg_ref[...], cb_ref[...]))
    out = _dot(hc.astype(BF16), w2_ref[...]) + b2_ref[...]
    _mixer_tail(out, x_ref[...], *tail_args)


def _conv_tail(glu, dw, dwb, cg, cb, w2_bf16, b2, x, modr, ln_g, ln_b, wr, br, li):
    per = TM // HALO
    last = T // HALO - 1
    return pl.pallas_call(
        _conv_tail_kernel,
        grid=(N_TILES,),
        in_specs=[pl.BlockSpec((TM, D), lambda i: (i, 0)),
                  pl.BlockSpec((HALO, D), lambda i: (jnp.maximum(i * per - 1, 0), 0)),
                  pl.BlockSpec((HALO, D), lambda i: (jnp.minimum((i + 1) * per, last), 0)),
                  _row_spec((CONV_K, D)), _row_spec((1, D)), _row_spec((1, D)), _row_spec((1, D)),
                  _row_spec((D, D)), _row_spec((1, D)),
                  pl.BlockSpec((TM, D), lambda i: (i, 0))] + _tail_in_specs(li),
        out_specs=_tail_out_specs(),
        out_shape=_TAIL_OUT_SHAPES,
        scratch_shapes=[pltpu.VMEM((TM + 2 * HALO, D), F32), pltpu.VMEM((TM, D), F32)],
        compiler_params=_cparams(("arbitrary",)),
        name="conv_tail",
    )(glu, glu, glu, dw, dwb, cg, cb, w2_bf16, b2, x, modr, modr, modr, ln_g, ln_b, wr, br)


def _sorted_positions(meta, srcv):
    lane = lax.broadcasted_iota(jnp.int32, meta.shape, 1).astype(F32)

    def pos(e_col, r_col):
        start = jnp.sum(jnp.where(lane == meta[:, e_col:e_col + 1], srcv, 0.0), axis=-1, keepdims=True)
        return start + meta[:, r_col:r_col + 1]

    return pos(META_E1, META_RANK1), pos(META_E2, META_RANK2)


def _one_hot_rows(pos):
    col = lax.broadcasted_iota(jnp.int32, (TM, SORT_ROWS), 1).astype(F32)
    return col == pos


def _for_each_row_group(tile, tot_ref, dstg_ref, fn):
    def body(g, carry):
        fn(pl.multiple_of(g * RUN_ALIGN, RUN_ALIGN),
           pl.multiple_of(dstg_ref[tile * SORT_GROUPS + g], RUN_ALIGN))
        return carry

    lax.fori_loop(0, tot_ref[tile] // RUN_ALIGN, body, 0)


def _wait_rows(total, make_copy):
    for bit in RUN_BITS:
        @pl.when((total & bit) != 0)
        def _(bit=bit):
            make_copy(bit).wait()


def _dispatch_kernel(tot_ref, dstg_ref, u_ref, meta_ref, srcv_ref, xs_ref, sorted_ref, sems):
    i = pl.program_id(0)
    slot = i % 2

    def wait_tile(tile, slot):
        buf = sorted_ref.at[slot]
        _wait_rows(tot_ref[tile], lambda rows: pltpu.make_async_copy(
            buf.at[pl.ds(0, rows)], xs_ref.at[pl.ds(0, rows)], sems.at[slot]))

    @pl.when(i >= 2)
    def _():
        wait_tile(i - 2, slot)

    pos1, pos2 = _sorted_positions(meta_ref[...], srcv_ref[0])
    select = jnp.logical_or(_one_hot_rows(pos1), _one_hot_rows(pos2)).astype(BF16)
    sorted_ref[slot] = _dot_tn(select, u_ref[...])
    buf = sorted_ref.at[slot]

    def start(src, dst):
        pltpu.make_async_copy(buf.at[pl.ds(src, RUN_ALIGN)], xs_ref.at[pl.ds(dst, RUN_ALIGN)],
                              sems.at[slot]).start()

    _for_each_row_group(i, tot_ref, dstg_ref, start)

    @pl.when(i == N_TILES - 1)
    def _():
        wait_tile(i - 1, 1 - slot)
        wait_tile(i, slot)


def _dispatch(sched, u2, meta, srcv):
    return pl.pallas_call(
        _dispatch_kernel,
        grid_spec=pltpu.PrefetchScalarGridSpec(
            num_scalar_prefetch=2,
            grid=(N_TILES,),
            in_specs=[pl.BlockSpec((TM, D), lambda i, *_: (i, 0)),
                      pl.BlockSpec((TM, LANES), lambda i, *_: (i, 0)),
                      pl.BlockSpec((1, 1, LANES), lambda i, *_: (i, 0, 0))],
            out_specs=pl.BlockSpec(memory_space=pl.ANY),
            scratch_shapes=[pltpu.VMEM((2, SORT_ROWS, D), F32), pltpu.SemaphoreType.DMA((2,))],
        ),
        out_shape=jax.ShapeDtypeStruct((MOE_ROWS, D), F32),
        compiler_params=_cparams(("arbitrary",)),
        name="moe_dispatch",
    )(*sched, u2, meta, srcv)


def _experts_kernel(start_ref, chunks_ref, xs_ref, wg_ref, wu_ref, wd_ref, ys_ref,
                    wg_bf, wu_bf, wd_bf, xbuf, ybuf, in_sems, out_sems):
    e = pl.program_id(0)
    n = chunks_ref[e]
    first = start_ref[e] // MOE_TM
    total = start_ref[N_EXPERTS - 1] // MOE_TM + chunks_ref[N_EXPERTS - 1]

    def rows(g):
        return pl.ds(pl.multiple_of(g * MOE_TM, MOE_TM), MOE_TM)

    def load(g):
        slot = g % MOE_IN_SLOTS
        return pltpu.make_async_copy(xs_ref.at[rows(g)], xbuf.at[slot], in_sems.at[slot])

    def store(g):
        slot = g % MOE_OUT_SLOTS
        return pltpu.make_async_copy(ybuf.at[slot], ys_ref.at[rows(g)], out_sems.at[slot])

    @pl.when(e == 0)
    def _():
        for g in range(MOE_AHEAD):
            @pl.when(g < total)
            def _(g=g):
                load(g).start()

    @pl.when(n > 0)
    def _():
        wg_bf[...] = wg_ref[0, 0].astype(BF16)
        wu_bf[...] = wu_ref[0, 0].astype(BF16)
        wd_bf[...] = wd_ref[0, 0].astype(BF16)

        def tile(g, carry):
            load(g).wait()

            @pl.when(g + MOE_AHEAD < total)
            def _():
                load(g + MOE_AHEAD).start()

            @pl.when(g >= MOE_OUT_SLOTS)
            def _():
                store(g - MOE_OUT_SLOTS).wait()

            x = xbuf[g % MOE_IN_SLOTS].astype(BF16)
            h = (_silu(_dot(x, wg_bf[...])) * _dot(x, wu_bf[...])).astype(BF16)
            ybuf[g % MOE_OUT_SLOTS] = _dot(h, wd_bf[...])
            store(g).start()
            return carry

        lax.fori_loop(first, first + n, tile, 0)

    @pl.when(e == N_EXPERTS - 1)
    def _():
        for back in range(MOE_OUT_SLOTS, 0, -1):
            @pl.when(total >= back)
            def _(back=back):
                store(total - back).wait()


def _experts(seg_start, seg_chunks, xs, w_gate, w_up, w_down, li):
    def weight(shape):
        return pl.BlockSpec((1, 1) + shape, lambda e, *_: (li, e, 0, 0))

    return pl.pallas_call(
        _experts_kernel,
        grid_spec=pltpu.PrefetchScalarGridSpec(
            num_scalar_prefetch=2,
            grid=(N_EXPERTS,),
            in_specs=[pl.BlockSpec(memory_space=pl.ANY),
                      weight((D, D_EXPERT)), weight((D, D_EXPERT)), weight((D_EXPERT, D))],
            out_specs=pl.BlockSpec(memory_space=pl.ANY),
            scratch_shapes=[pltpu.VMEM((D, D_EXPERT), BF16), pltpu.VMEM((D, D_EXPERT), BF16),
                            pltpu.VMEM((D_EXPERT, D), BF16),
                            pltpu.VMEM((MOE_IN_SLOTS, MOE_TM, D), F32),
                            pltpu.VMEM((MOE_OUT_SLOTS, MOE_TM, D), F32),
                            pltpu.SemaphoreType.DMA((MOE_IN_SLOTS,)),
                            pltpu.SemaphoreType.DMA((MOE_OUT_SLOTS,))],
        ),
        out_shape=jax.ShapeDtypeStruct((MOE_ROWS, D), F32),
        compiler_params=_cparams(("arbitrary",)),
        name="moe_experts",
    )(seg_start, seg_chunks, xs, w_gate, w_up, w_down)


def _combine_kernel(tot_ref, dstg_ref, ys_ref, x1_ref, meta_ref, srcv_ref, g2_ref,
                    lng_ref, lnb_ref, *rest, split):
    outs, (sorted_ref, sems) = rest[:-2], rest[-2:]
    i = pl.program_id(0)
    slot = i % 2

    def fetch(tile, slot):
        buf = sorted_ref.at[slot]

        def start(src, dst):
            pltpu.make_async_copy(ys_ref.at[pl.ds(dst, RUN_ALIGN)], buf.at[pl.ds(src, RUN_ALIGN)],
                                  sems.at[slot]).start()

        _for_each_row_group(tile, tot_ref, dstg_ref, start)

    @pl.when(i == 0)
    def _():
        sorted_ref[...] = jnp.zeros_like(sorted_ref)
        fetch(0, 0)

    @pl.when(i + 1 < N_TILES)
    def _():
        fetch(i + 1, 1 - slot)

    meta = meta_ref[...]
    pos1, pos2 = _sorted_positions(meta, srcv_ref[0])
    sel1 = _one_hot_rows(pos1).astype(BF16)
    sel2 = _one_hot_rows(pos2).astype(BF16)
    buf = sorted_ref.at[slot]
    _wait_rows(tot_ref[i], lambda rows: pltpu.make_async_copy(
        ys_ref.at[pl.ds(0, rows)], buf.at[pl.ds(0, rows)], sems.at[slot]))
    ysort = sorted_ref[slot].astype(BF16)
    f = (meta[:, META_W1:META_W1 + 1] * _dot(sel1, ysort)
         + meta[:, META_W2:META_W2 + 1] * _dot(sel2, ysort))
    y = _layer_norm(ALPHA * x1_ref[...] + g2_ref[0] * f, lng_ref[...], lnb_ref[...])
    if split:
        @pl.when(i < PROMPT_TILES)
        def _():
            outs[0][...] = y

        @pl.when(i >= PROMPT_TILES)
        def _():
            outs[1][...] = y
    else:
        outs[0][...] = y


def _combine(sched, ys, x1, meta, srcv, modr, ln_g, ln_b, li, split):
    if split:
        out_specs = [_prompt_tile_spec(D), _sample_tile_spec(D)]
        out_shape = [jax.ShapeDtypeStruct((T_PROMPT, D), F32), jax.ShapeDtypeStruct((T_SAMPLE, D), F32)]
    else:
        out_specs = [pl.BlockSpec((TM, D), lambda i, *_: (i, 0))]
        out_shape = [jax.ShapeDtypeStruct((T, D), F32)]
    return pl.pallas_call(
        functools.partial(_combine_kernel, split=split),
        grid_spec=pltpu.PrefetchScalarGridSpec(
            num_scalar_prefetch=2,
            grid=(N_TILES,),
            in_specs=[pl.BlockSpec(memory_space=pl.ANY),
                      pl.BlockSpec((TM, D), lambda i, *_: (i, 0)),
                      pl.BlockSpec((TM, LANES), lambda i, *_: (i, 0)),
                      pl.BlockSpec((1, 1, LANES), lambda i, *_: (i, 0, 0)),
                      _mod_spec(li, 5), _row_spec((1, D)), _row_spec((1, D))],
            out_specs=out_specs,
            scratch_shapes=[pltpu.VMEM((2, SORT_ROWS, D), F32), pltpu.SemaphoreType.DMA((2,))],
        ),
        out_shape=out_shape,
        compiler_params=_cparams(("arbitrary",)),
        name="moe_combine",
    )(*sched, ys, x1, meta, srcv, modr, ln_g, ln_b)


def _moe_schedule(tile_counts):
    n = (tile_counts + RUN_ALIGN - 1) // RUN_ALIGN * RUN_ALIGN
    src = jnp.cumsum(n, axis=1) - n
    per_expert = jnp.sum(n, axis=0)
    seg = (per_expert + MOE_TM - 1) // MOE_TM * MOE_TM
    seg_start = jnp.cumsum(seg) - seg
    dst = seg_start[None, :] + jnp.cumsum(n, axis=0) - n
    g_row = jnp.arange(SORT_GROUPS, dtype=jnp.int32) * RUN_ALIGN
    in_run = jnp.logical_and(src[:, None, :] <= g_row[None, :, None],
                             g_row[None, :, None] < (src + n)[:, None, :])
    dst_g = g_row[None, :] + jnp.sum(jnp.where(in_run, (dst - src)[:, None, :], 0), axis=2)
    runs = (jnp.sum(n, axis=1), dst_g.reshape(-1))
    srcv = jnp.pad(src.astype(F32), ((0, 0), (ROUTER_LANE0, LANES - ROUTER_LANE0 - N_EXPERTS)))
    return runs, srcv.reshape(N_TILES, 1, LANES), (seg_start, seg // MOE_TM)


def _moe(x1, u2, meta, cnt, modr, w_gate, w_up, w_down, ln_g, ln_b, li, split):
    tile_counts = cnt[:, 0, ROUTER_LANE0:ROUTER_LANE0 + N_EXPERTS].astype(jnp.int32)
    runs, srcv, (seg_start, seg_chunks) = _moe_schedule(tile_counts)
    xs = _dispatch(runs, u2, meta, srcv)
    ys = _experts(seg_start, seg_chunks, xs, w_gate, w_up, w_down, li)
    return _combine(runs, ys, x1, meta, srcv, modr, ln_g, ln_b, li, split)


def _router_slab(wg, bg, we, be):
    w = jnp.concatenate([wg, we.transpose(1, 0, 2).reshape(D, N_EXPERTS)], axis=1)
    b = jnp.concatenate([bg, be.reshape(N_EXPERTS)])
    pad = LANES - w.shape[1]
    return jnp.pad(w, ((0, 0), (0, pad))), jnp.pad(b, (0, pad)).reshape(1, LANES)


def kernel(x_prompt, x_sample, cache_diff_k, cache_diff_v, state_ret_fwd, state_ret_bwd, c, c_ctx, mod_w, mod_b, ln1_g, ln1_b, ln2_g, ln2_b, mix_w_in, mix_w_out, ret_decay_fwd, ret_decay_bwd, diff_lq1, diff_lk1, diff_lq2, diff_lk2, diff_subln_g, conv_w1, conv_b1, conv_dw, conv_dw_b, conv_ln_g, conv_ln_b, conv_w2, conv_b2, router_g_w, router_g_b, router_e_w, router_e_b, moe_w_gate, moe_w_up, moe_w_down):
    xp = x_prompt.reshape(T_PROMPT, D)
    xs = x_sample.reshape(T_SAMPLE, D)
    cond = jnp.concatenate([c_ctx[None, :], c, jnp.zeros((MOD_ROWS - 1 - DEC_BATCH, D), F32)], axis=0)
    modr = _mod_vectors(cond, mod_w, mod_b).reshape(DEPTH * MOD_ROWS * 6, 1, D)
    cos, sin_signed = _rope_tables()

    def row(v):
        return v.reshape(1, -1)

    x = None
    caches = None
    for li in range(DEPTH):
        wr, br = _router_slab(router_g_w[li], router_g_b[li], router_e_w[li], router_e_b[li])
        if li % 2 == 0:
            assert li == 0, "the even mixer reads the kernel inputs directly"
            e = li // 2
            lam_init = 0.8 - 0.6 * math.exp(-0.3 * li)
            proj, ck, cv = _in_proj(xp, xs, modr, mix_w_in[e].astype(BF16), li)
            dec = jnp.concatenate([ret_decay_fwd[e], ret_decay_bwd[e]])
            r_p, sf, sb = _retention(proj, dec, BATCH, SEQ, 0, emit_state=True)
            (r_s,) = _retention(proj, dec, DEC_BATCH, DEC_SEQ, T_PROMPT // DEC_SEQ,
                                s0f=state_ret_fwd, s0b=state_ret_bwd, e=e)
            lams = (row(diff_lq1[e]), row(diff_lk1[e]), row(diff_lq2[e]), row(diff_lk2[e]),
                    row(diff_subln_g[e]))
            o_p = _attn_prompt(proj, *lams, lam_init)
            o_s = _attn_sample(proj, cache_diff_k, cache_diff_v, cos, sin_signed, *lams, lam_init, e)
            x1, u2, meta, cnt = _out_proj_tail(r_p, r_s, o_p, o_s, mix_w_out[e].astype(BF16), xp, xs, modr,
                                               row(ln1_g[li]), row(ln1_b[li]), wr, br, li)
            caches = (ck, cv, sf, sb)
        else:
            o = li // 2
            glu = _conv_glu(x, modr, conv_w1[o].astype(BF16), row(conv_b1[o]), li)
            x1, u2, meta, cnt = _conv_tail(glu, conv_dw[o], row(conv_dw_b[o]), row(conv_ln_g[o]),
                                           row(conv_ln_b[o]), conv_w2[o].astype(BF16), row(conv_b2[o]),
                                           x, modr, row(ln1_g[li]), row(ln1_b[li]), wr, br, li)
        outs = _moe(x1, u2, meta, cnt, modr, moe_w_gate, moe_w_up, moe_w_down,
                    row(ln2_g[li]), row(ln2_b[li]), li, split=(li == DEPTH - 1))
        x = outs[0]

    y_prompt = outs[0].reshape(BATCH, SEQ, D)
    y_sample = outs[1].reshape(DEC_BATCH, DEC_SEQ, D)
    return (y_prompt, y_sample) + caches
```

```python
import functools
import math

import numpy as np
import jax
import jax.numpy as jnp
from jax import lax
from jax.experimental import pallas as pl
from jax.experimental.pallas import tpu as pltpu

F32 = jnp.float32
BF16 = jnp.bfloat16

D = 1024
BATCH = 16
SEQ = 256
DEPTH = 2
DEC_BATCH = 2
DEC_SEQ = 2048
PAST_LEN = 512
GRID_W = 64
HEADS = 4
HEAD_W = 128
RET_CHUNK = 128
DIFF_DK = 64
ROPE_THETA = 10000.0
IN_W = 7 * HEADS * HEAD_W
CONV_K = 31
CONV_PAD = CONV_K // 2
N_GROUPS = 4
EXPERTS_PER_GROUP = 8
N_EXPERTS = N_GROUPS * EXPERTS_PER_GROUP
D_EXPERT = 512
ALPHA = (2.0 * DEPTH) ** 0.25
LN_EPS = 1e-5
GN_EPS = 1e-6

T_PROMPT = BATCH * SEQ
T_SAMPLE = DEC_BATCH * DEC_SEQ
T = T_PROMPT + T_SAMPLE
TM = 256
N_TILES = T // TM
PROMPT_TILES = T_PROMPT // TM
SAMPLE_TILES_PER_SEQ = DEC_SEQ // TM
MOD_ROWS = 8
MOE_TM = 256
LANES = 128
SUBLANES = 8
RUN_ALIGN = SUBLANES
SORT_ROWS = -(-(2 * TM + N_EXPERTS * (RUN_ALIGN - 1)) // TM) * TM
RUN_BITS = tuple(1 << b for b in range((2 * TM).bit_length() - 1, RUN_ALIGN.bit_length() - 2, -1))
MOE_MAX_TILES = -(-(2 * T + N_TILES * N_EXPERTS * (RUN_ALIGN - 1) + N_EXPERTS * (MOE_TM - RUN_ALIGN)) // MOE_TM)
MOE_ROWS = MOE_MAX_TILES * MOE_TM
SORT_GROUPS = SORT_ROWS // RUN_ALIGN
MOE_AHEAD = 2
MOE_IN_SLOTS = MOE_AHEAD + 1
MOE_OUT_SLOTS = 2
ROUTER_LANE0 = N_GROUPS
VMEM_LIMIT = 52 * 1024 * 1024


def _cparams(sem):
    return pltpu.CompilerParams(dimension_semantics=sem, vmem_limit_bytes=VMEM_LIMIT)


def _tile_cond_row(i):
    return jnp.where(i < PROMPT_TILES, 0, 1 + (i - PROMPT_TILES) // SAMPLE_TILES_PER_SEQ)


def _mod_spec(li, k):
    return pl.BlockSpec((1, 1, D), lambda i, *_: ((li * MOD_ROWS + _tile_cond_row(i)) * 6 + k, 0, 0))


def _row_spec(shape):
    return pl.BlockSpec(shape, lambda i, *_: (0,) * len(shape))


def _layer_norm(x, g, b):
    mu = jnp.mean(x, axis=-1, keepdims=True)
    xc = x - mu
    var = jnp.mean(xc * xc, axis=-1, keepdims=True)
    return xc * lax.rsqrt(var + LN_EPS) * g + b


def _silu(x):
    return x * jax.nn.sigmoid(x)


def _dot(a, b):
    return jnp.dot(a, b, preferred_element_type=F32)


def _dot_nt(a, b):
    return lax.dot_general(a, b, (((1,), (1,)), ((), ())), preferred_element_type=F32)


def _dot_tn(a, b):
    return lax.dot_general(a, b, (((0,), (0,)), ((), ())), preferred_element_type=F32)


MOD_TN = 1024
MOD_USED_ROWS = 1 + DEC_BATCH


def _mod_kernel(cond_t_ref, w_ref, b_ref, o_ref):
    s = _silu(cond_t_ref[...])
    w = w_ref[0]
    o_ref[0] = jnp.zeros((MOD_ROWS, MOD_TN), F32) + b_ref[0]
    for r in range(MOD_USED_ROWS):
        o_ref[0, r:r + 1, :] = jnp.sum(w * s[:, r:r + 1], axis=0, keepdims=True) + b_ref[0]


def _mod_vectors(cond, mod_w, mod_b):
    return pl.pallas_call(
        _mod_kernel,
        grid=(DEPTH, 6 * D // MOD_TN),
        in_specs=[
            pl.BlockSpec((D, MOD_ROWS), lambda l, j: (0, 0)),
            pl.BlockSpec((1, D, MOD_TN), lambda l, j: (l, 0, j)),
            pl.BlockSpec((1, 1, MOD_TN), lambda l, j: (l, 0, j)),
        ],
        out_specs=pl.BlockSpec((1, MOD_ROWS, MOD_TN), lambda l, j: (l, 0, j)),
        out_shape=jax.ShapeDtypeStruct((DEPTH, MOD_ROWS, 6 * D), F32),
        compiler_params=_cparams(("arbitrary", "arbitrary")),
        name="mod_vectors",
    )(cond.T, mod_w, mod_b.reshape(DEPTH, 1, 6 * D))


def _prompt_tile_spec(width):
    return pl.BlockSpec((TM, width), lambda i, *_: (jnp.minimum(i, PROMPT_TILES - 1), 0))


def _sample_tile_spec(width):
    return pl.BlockSpec((TM, width), lambda i, *_: (jnp.maximum(i - PROMPT_TILES, 0), 0))


def _pick_tile(prompt_ref, sample_ref):
    return jnp.where(pl.program_id(0) < PROMPT_TILES, prompt_ref[...], sample_ref[...])


def _in_proj_kernel(xp_ref, xs_ref, sh_ref, sc_ref, w_ref, o_ref, ck_ref, cv_ref):
    u = _pick_tile(xp_ref, xs_ref) * (1.0 + sc_ref[0]) + sh_ref[0]
    proj = _dot(u.astype(BF16), w_ref[...])
    o_ref[...] = proj

    @pl.when(pl.program_id(0) < PROMPT_TILES)
    def _():
        for h in range(HEADS):
            ck_ref[0, 0, h] = proj[:, (COL_KD + h) * HEAD_W:(COL_KD + h + 1) * HEAD_W]
            cv_ref[0, 0, h] = proj[:, (COL_VD + h) * HEAD_W:(COL_VD + h + 1) * HEAD_W]


def _in_proj(x_prompt, x_sample, modr, w_in_bf16, li):
    cache_spec = pl.BlockSpec((1, 1, HEADS, SEQ, HEAD_W),
                              lambda i: (jnp.minimum(i, PROMPT_TILES - 1), 0, 0, 0, 0))
    cache_shape = jax.ShapeDtypeStruct((BATCH, 1, HEADS, SEQ, HEAD_W), F32)
    return pl.pallas_call(
        _in_proj_kernel,
        grid=(N_TILES,),
        in_specs=[
            _prompt_tile_spec(D), _sample_tile_spec(D),
            _mod_spec(li, 0),
            _mod_spec(li, 1),
            _row_spec((D, IN_W)),
        ],
        out_specs=[pl.BlockSpec((TM, IN_W), lambda i: (i, 0)), cache_spec, cache_spec],
        out_shape=[jax.ShapeDtypeStruct((T, IN_W), F32), cache_shape, cache_shape],
        compiler_params=_cparams(("arbitrary",)),
        name="in_proj",
    )(x_prompt, x_sample, modr, modr, w_in_bf16)


COL_QR, COL_KR, COL_VR, COL_GR, COL_QD, COL_KD, COL_VD = (k * HEADS for k in range(7))


def _retention_kernel(dec_ref, q_ref, k_ref, v_ref, g_ref, *rest, n_chunks, n_heads, has_state, emit_state):
    rest = list(rest)
    if has_state:
        s0f_ref, s0b_ref = rest[:2]
        rest = rest[2:]
    r_ref = rest[0]
    rest = rest[1:]
    if emit_state:
        sf_ref, sb_ref = rest[:2]
        rest = rest[2:]
    of_ref = rest[0]

    head0 = pl.program_id(1) * n_heads
    C = RET_CHUNK
    ii = lax.broadcasted_iota(jnp.int32, (C, C), 0)
    jj = lax.broadcasted_iota(jnp.int32, (C, C), 1)
    rel = (ii - jj).astype(F32)
    idx = lax.broadcasted_iota(jnp.int32, (C, 1), 0).astype(F32)
    k_scale = HEAD_W ** -0.5

    def chunk(ref, c, h):
        return ref[c * C:(c + 1) * C, h * HEAD_W:(h + 1) * HEAD_W]

    def decays(direction, h):
        lg = -jnp.exp(jnp.full((1, 1), dec_ref[direction * HEADS + head0 + h], F32))
        if direction == 0:
            inner = jnp.where(rel >= 0, jnp.exp(jnp.maximum(rel, 0.0) * lg), 0.0)
            return inner, jnp.exp((idx + 1.0) * lg), jnp.exp((C - 1.0 - idx) * lg), jnp.exp(C * lg)
        inner = jnp.where(rel <= 0, jnp.exp(jnp.maximum(-rel, 0.0) * lg), 0.0)
        return inner, jnp.exp((C - idx) * lg), jnp.exp(idx * lg), jnp.exp(C * lg)

    def run(direction):
        dec = [decays(direction, h) for h in range(n_heads)]
        if has_state:
            s0_ref = s0f_ref if direction == 0 else s0b_ref
            states = [s0_ref[0, 0, h] for h in range(n_heads)]
        else:
            states = [jnp.zeros((HEAD_W, HEAD_W), F32) for _ in range(n_heads)]
        order = range(n_chunks) if direction == 0 else range(n_chunks - 1, -1, -1)
        for c in order:
            rows = slice(c * C, (c + 1) * C)
            for h in range(n_heads):
                inner, q_decay, k_decay, chunk_decay = dec[h]
                cols = slice(h * HEAD_W, (h + 1) * HEAD_W)
                s = states[h]
                qc = chunk(q_ref, c, h)
                kc = chunk(k_ref, c, h) * k_scale
                vc = chunk(v_ref, c, h).astype(BF16)
                scores = _dot_nt(qc.astype(BF16), kc.astype(BF16)) * inner
                o = _dot(scores.astype(BF16), vc) + _dot((qc * q_decay).astype(BF16), s.astype(BF16))
                states[h] = s * chunk_decay + _dot_tn((kc * k_decay).astype(BF16), vc)
                if direction == 0:
                    of_ref[rows, cols] = o
                else:
                    r = of_ref[rows, cols] + o
                    mu = jnp.mean(r, axis=-1, keepdims=True)
                    rc = r - mu
                    var = jnp.mean(rc * rc, axis=-1, keepdims=True)
                    rn = rc * lax.rsqrt(var + GN_EPS)
                    r_ref[rows, cols] = _silu(chunk(g_ref, c, h)) * rn
        return states

    sf = run(0)
    sb = run(1)
    if emit_state:
        for h in range(n_heads):
            sf_ref[0, 0, h] = sf[h]
            sb_ref[0, 0, h] = sb[h]


def _retention(proj, dec, n_seq, seq_len, row_block0, n_heads, s0f=None, s0b=None, e=0, emit_state=False):
    has_state = s0f is not None
    width = n_heads * HEAD_W

    def col(base):
        return pl.BlockSpec((seq_len, width), lambda b, h, *_: (row_block0 + b, base // n_heads + h))

    state_spec = pl.BlockSpec((1, 1, n_heads, HEAD_W, HEAD_W), lambda b, h, *_: (b, e, h, 0, 0))
    in_specs = [pl.BlockSpec(memory_space=pltpu.SMEM), col(COL_QR), col(COL_KR), col(COL_VR), col(COL_GR)]
    args = [dec, proj, proj, proj, proj]
    if has_state:
        in_specs += [state_spec, state_spec]
        args += [s0f, s0b]
    out_specs = [pl.BlockSpec((seq_len, width), lambda b, h, *_: (b, h))]
    out_shape = [jax.ShapeDtypeStruct((n_seq * seq_len, HEADS * HEAD_W), F32)]
    if emit_state:
        st = pl.BlockSpec((1, 1, n_heads, HEAD_W, HEAD_W), lambda b, h, *_: (b, 0, h, 0, 0))
        out_specs += [st, st]
        out_shape += [jax.ShapeDtypeStruct((n_seq, 1, HEADS, HEAD_W, HEAD_W), F32)] * 2
    return pl.pallas_call(
        functools.partial(_retention_kernel, n_chunks=seq_len // RET_CHUNK, n_heads=n_heads,
                          has_state=has_state, emit_state=emit_state),
        grid=(n_seq, HEADS // n_heads),
        in_specs=in_specs,
        out_specs=out_specs,
        out_shape=out_shape,
        scratch_shapes=[pltpu.VMEM((seq_len, width), F32)],
        compiler_params=_cparams(("arbitrary", "arbitrary")),
        name=f"retention_{seq_len}",
    )(*args)


def _diff_lambda(lq1_ref, lk1_ref, lq2_ref, lk2_ref, lam_init):
    a = jnp.sum(lq1_ref[...] * lk1_ref[...], axis=-1, keepdims=True)
    b = jnp.sum(lq2_ref[...] * lk2_ref[...], axis=-1, keepdims=True)
    return jnp.exp(a) - jnp.exp(b) + lam_init


LOG2E = 1.4426950408889634


def _diff_attend(q, k, v, lam, subln_g, lam_init):
    lane = lax.broadcasted_iota(jnp.int32, q.shape, 1)
    q1 = jnp.where(lane < DIFF_DK, q, 0.0).astype(BF16)
    q2 = jnp.where(lane >= DIFF_DK, q, 0.0).astype(BF16)

    def softmax_times_v(qz):
        s = _dot_nt(qz, k)
        p = jnp.exp2(s - jnp.max(s, axis=-1, keepdims=True))
        return _dot(p.astype(BF16), v) * (1.0 / jnp.sum(p, axis=-1, keepdims=True))

    o = softmax_times_v(q1) - lam * softmax_times_v(q2)
    o = o * lax.rsqrt(jnp.mean(o * o, axis=-1, keepdims=True) + LN_EPS)
    return o * subln_g * (1.0 - lam_init)


def _attn_prompt_kernel(q_ref, k_ref, v_ref, lq1, lk1, lq2, lk2, g_ref, o_ref, *, lam_init):
    lam = _diff_lambda(lq1, lk1, lq2, lk2, lam_init)
    scale = DIFF_DK ** -0.5 * LOG2E
    for h in range(HEADS):
        sl = slice(h * HEAD_W, (h + 1) * HEAD_W)
        o_ref[:, sl] = _diff_attend(q_ref[:, sl] * scale, k_ref[:, sl].astype(BF16),
                                    v_ref[:, sl].astype(BF16), lam, g_ref[...], lam_init)


def _attn_prompt(proj, lq1, lk1, lq2, lk2, subln_g, lam_init):
    W = HEADS * HEAD_W

    def slab(base):
        return pl.BlockSpec((SEQ, W), lambda b: (b, base // HEADS))

    small = _row_spec((1, DIFF_DK))
    return pl.pallas_call(
        functools.partial(_attn_prompt_kernel, lam_init=lam_init),
        grid=(BATCH,),
        in_specs=[slab(COL_QD), slab(COL_KD), slab(COL_VD), small, small, small, small,
                  _row_spec((1, HEAD_W))],
        out_specs=pl.BlockSpec((SEQ, W), lambda b: (b, 0)),
        out_shape=jax.ShapeDtypeStruct((T_PROMPT, W), F32),
        compiler_params=_cparams(("arbitrary",)),
        name="diff_attn_prompt",
    )(proj, proj, proj, lq1, lk1, lq2, lk2, subln_g)


def _rope(x, cos, sin_signed):
    lane = lax.broadcasted_iota(jnp.int32, x.shape, 1)
    partner = jnp.where((lane % 32) < 16, pltpu.roll(x, LANES - 16, 1), pltpu.roll(x, 16, 1))
    return x * cos + partner * sin_signed


def _attn_sample_kernel(q_ref, k_ref, v_ref, ck_ref, cv_ref, cosq_ref, sinq_ref, cos_ref, sin_ref,
                        lq1, lk1, lq2, lk2, g_ref, o_ref, kbuf, vbuf, *, lam_init):
    @pl.when(pl.program_id(2) == 0)
    def _():
        kbuf[0:DEC_SEQ, :] = _rope(k_ref[...], cos_ref[...], sin_ref[...]).astype(BF16)
        kbuf[DEC_SEQ:, :] = ck_ref[0, 0, 0].astype(BF16)
        vbuf[0:DEC_SEQ, :] = v_ref[...].astype(BF16)
        vbuf[DEC_SEQ:, :] = cv_ref[0, 0, 0].astype(BF16)

    lam = _diff_lambda(lq1, lk1, lq2, lk2, lam_init)
    q = _rope(q_ref[...], cosq_ref[...], sinq_ref[...]) * (DIFF_DK ** -0.5 * LOG2E)
    o_ref[...] = _diff_attend(q, kbuf[...], vbuf[...], lam, g_ref[...], lam_init)


ATTN_TQ = 256


def _attn_sample(proj, cache_k, cache_v, cos, sin_signed, lq1, lk1, lq2, lk2, subln_g, lam_init, e):
    nq = DEC_SEQ // ATTN_TQ
    row0_q = T_PROMPT // ATTN_TQ
    row0_kv = T_PROMPT // DEC_SEQ
    small = pl.BlockSpec((1, DIFF_DK), lambda b, h, t: (0, 0))
    cache = pl.BlockSpec((1, 1, 1, PAST_LEN, HEAD_W), lambda b, h, t: (b, e, h, 0, 0))
    table_q = pl.BlockSpec((ATTN_TQ, HEAD_W), lambda b, h, t: (t, 0))
    table = pl.BlockSpec((DEC_SEQ, HEAD_W), lambda b, h, t: (0, 0))
    return pl.pallas_call(
        functools.partial(_attn_sample_kernel, lam_init=lam_init),
        grid=(DEC_BATCH, HEADS, nq),
        in_specs=[
            pl.BlockSpec((ATTN_TQ, HEAD_W), lambda b, h, t: (row0_q + b * nq + t, COL_QD + h)),
            pl.BlockSpec((DEC_SEQ, HEAD_W), lambda b, h, t: (row0_kv + b, COL_KD + h)),
            pl.BlockSpec((DEC_SEQ, HEAD_W), lambda b, h, t: (row0_kv + b, COL_VD + h)),
            cache, cache, table_q, table_q, table, table,
            small, small, small, small,
            pl.BlockSpec((1, HEAD_W), lambda b, h, t: (0, 0)),
        ],
        out_specs=pl.BlockSpec((ATTN_TQ, HEAD_W), lambda b, h, t: (b * nq + t, h)),
        out_shape=jax.ShapeDtypeStruct((T_SAMPLE, HEADS * HEAD_W), F32),
        scratch_shapes=[pltpu.VMEM((DEC_SEQ + PAST_LEN, HEAD_W), BF16),
                        pltpu.VMEM((DEC_SEQ + PAST_LEN, HEAD_W), BF16)],
        compiler_params=_cparams(("arbitrary", "arbitrary", "arbitrary")),
        name="diff_attn_sample",
    )(proj, proj, proj, cache_k, cache_v, cos, sin_signed, cos, sin_signed,
      lq1, lk1, lq2, lk2, subln_g)


def _rope_tables():
    t = np.arange(DEC_SEQ)
    row, colp = t // GRID_W, t % GRID_W
    lane = np.arange(LANES)
    pos = np.where(((lane // 32) % 2 == 0)[None, :], row[:, None], colp[:, None]).astype(np.float64)
    half = 16
    inv = (np.float32(ROPE_THETA) ** (-(np.arange(half, dtype=np.float32)) / np.float32(half))).astype(np.float32)
    ang = pos.astype(np.float32) * inv[lane % half][None, :]
    cos = np.cos(ang.astype(np.float64)).astype(np.float32)
    sin = np.sin(ang.astype(np.float64)).astype(np.float32)
    sign = np.where((lane % 32) < half, -1.0, 1.0).astype(np.float32)[None, :]
    return jnp.asarray(cos), jnp.asarray(sin * sign)


def _split_bf16(a):
    hi = a.astype(BF16)
    return hi, (a - hi.astype(F32)).astype(BF16)


def _mixer_tail(out, x, g1_ref, sc2_ref, sh2_ref, lng_ref, lnb_ref, wr_ref, br_ref,
                x1_ref, u2_ref, meta_ref, cnt_ref):
    x1 = _layer_norm(ALPHA * x + g1_ref[0] * out, lng_ref[...], lnb_ref[...])
    x1_ref[...] = x1
    u2 = x1 * (1.0 + sc2_ref[0]) + sh2_ref[0]
    u2_ref[...] = u2.astype(BF16)

    u_hi, u_lo = _split_bf16(u2)
    w_hi, w_lo = _split_bf16(wr_ref[...])
    logits = _dot(u_hi, w_hi) + (_dot(u_hi, w_lo) + _dot(u_lo, w_hi)) + br_ref[...]
    lane = lax.broadcasted_iota(jnp.int32, logits.shape, 1).astype(F32)
    neg = jnp.float32(-jnp.inf)
    is_g = lane < N_GROUPS
    gl = jnp.where(is_g, logits, neg)
    gmax = jnp.max(gl, axis=-1, keepdims=True)
    gsel = jnp.min(jnp.where(gl == gmax, lane, float(LANES)), axis=-1, keepdims=True)
    p_g = 1.0 / jnp.sum(jnp.where(is_g, jnp.exp(gl - gmax), 0.0), axis=-1, keepdims=True)
    lo = ROUTER_LANE0 + gsel * EXPERTS_PER_GROUP
    el = jnp.where((lane >= lo) & (lane < lo + EXPERTS_PER_GROUP), logits, neg)
    v1 = jnp.max(el, axis=-1, keepdims=True)
    i1 = jnp.min(jnp.where(el == v1, lane, float(LANES)), axis=-1, keepdims=True)
    el2 = jnp.where(lane == i1, neg, el)
    v2 = jnp.max(el2, axis=-1, keepdims=True)
    i2 = jnp.min(jnp.where(el2 == v2, lane, float(LANES)), axis=-1, keepdims=True)
    t = jnp.exp(v2 - v1)
    w1 = p_g / (1.0 + t)
    w2 = w1 * t

    oh1 = (lane == i1).astype(F32)
    oh2 = (lane == i2).astype(F32)
    oh = oh1 + oh2
    r_i = lax.broadcasted_iota(jnp.int32, (TM, TM), 0)
    c_i = lax.broadcasted_iota(jnp.int32, (TM, TM), 1)
    before = (c_i < r_i).astype(BF16)
    earlier = _dot(before, oh.astype(BF16))
    rank1 = jnp.sum(earlier * oh1, axis=-1, keepdims=True)
    rank2 = jnp.sum(earlier * oh2, axis=-1, keepdims=True)
    cnt_ref[0] = jnp.sum(oh, axis=0, keepdims=True)
    cols = (i1, i2, w1, w2, rank1, rank2)
    meta = jnp.zeros_like(logits)
    for k, col in enumerate(cols):
        meta = jnp.where(lane == k, col, meta)
    meta_ref[...] = meta


META_E1, META_E2, META_W1, META_W2, META_RANK1, META_RANK2 = range(6)

_TAIL_OUT_SHAPES = [
    jax.ShapeDtypeStruct((T, D), F32),
    jax.ShapeDtypeStruct((T, D), BF16),
    jax.ShapeDtypeStruct((T, LANES), F32),
    jax.ShapeDtypeStruct((N_TILES, 1, LANES), F32),
]


def _tail_out_specs():
    return [
        pl.BlockSpec((TM, D), lambda i: (i, 0)),
        pl.BlockSpec((TM, D), lambda i: (i, 0)),
        pl.BlockSpec((TM, LANES), lambda i: (i, 0)),
        pl.BlockSpec((1, 1, LANES), lambda i: (i, 0, 0)),
    ]


def _tail_in_specs(li):
    return [
        _mod_spec(li, 2), _mod_spec(li, 4), _mod_spec(li, 3),
        _row_spec((1, D)), _row_spec((1, D)),
        _row_spec((D, LANES)), _row_spec((1, LANES)),
    ]


def _out_proj_kernel(rp_ref, rs_ref, op_ref, os_ref, w_ref, xp_ref, xs_ref, *rest):
    half = HEADS * HEAD_W
    r = _pick_tile(rp_ref, rs_ref).astype(BF16)
    o = _pick_tile(op_ref, os_ref).astype(BF16)
    out = _dot(r, w_ref[0:half, :]) + _dot(o, w_ref[half:, :])
    _mixer_tail(out, _pick_tile(xp_ref, xs_ref), *rest)


def _out_proj_tail(r_p, r_s, o_p, o_s, w_out_bf16, x_prompt, x_sample, modr, ln_g, ln_b, wr, br, li):
    half = HEADS * HEAD_W
    return pl.pallas_call(
        _out_proj_kernel,
        grid=(N_TILES,),
        in_specs=[_prompt_tile_spec(half), _sample_tile_spec(half),
                  _prompt_tile_spec(half), _sample_tile_spec(half),
                  _row_spec((2 * half, D)),
                  _prompt_tile_spec(D), _sample_tile_spec(D)] + _tail_in_specs(li),
        out_specs=_tail_out_specs(),
        out_shape=_TAIL_OUT_SHAPES,
        compiler_params=_cparams(("arbitrary",)),
        name="out_proj_tail",
    )(r_p, r_s, o_p, o_s, w_out_bf16, x_prompt, x_sample, modr, modr, modr, ln_g, ln_b, wr, br)


def _conv_glu_kernel(x_ref, sh_ref, sc_ref, w_ref, b_ref, o_ref):
    u = x_ref[...] * (1.0 + sc_ref[0]) + sh_ref[0]
    h = _dot(u.astype(BF16), w_ref[...]) + b_ref[...]
    o_ref[...] = h[:, :D] * jax.nn.sigmoid(h[:, D:])


def _conv_glu(x, modr, w1_bf16, b1, li):
    return pl.pallas_call(
        _conv_glu_kernel,
        grid=(N_TILES,),
        in_specs=[pl.BlockSpec((TM, D), lambda i: (i, 0)), _mod_spec(li, 0), _mod_spec(li, 1),
                  _row_spec((D, 2 * D)), _row_spec((1, 2 * D))],
        out_specs=pl.BlockSpec((TM, D), lambda i: (i, 0)),
        out_shape=jax.ShapeDtypeStruct((T, D), F32),
        compiler_params=_cparams(("arbitrary",)),
        name="conv_glu",
    )(x, modr, modr, w1_bf16, b1)


HALO = 16
CONV_ROWS = 64
CONV_COLS = 128


def _depthwise_conv(hp, dw_ref, conv):
    base = HALO - CONV_PAD
    for cb in range(D // CONV_COLS):
        cs = slice(cb * CONV_COLS, (cb + 1) * CONV_COLS)
        for rb in range(TM // CONV_ROWS):
            r0 = rb * CONV_ROWS
            acc = None
            for shift in range(SUBLANES):
                part = None
                for tap in range(CONV_K):
                    off = base + tap
                    if off % SUBLANES != shift:
                        continue
                    a0 = r0 + off - shift
                    term = hp[a0:a0 + CONV_ROWS + SUBLANES, cs] * dw_ref[tap:tap + 1, cs]
                    part = term if part is None else part + term
                part = part[shift:shift + CONV_ROWS, :]
                acc = part if acc is None else acc + part
            conv[r0:r0 + CONV_ROWS, cs] = acc


def _conv_tail_kernel(cur_ref, prev_ref, next_ref, dw_ref, dwb_ref, cg_ref, cb_ref, w2_ref, b2_ref,
                      x_ref, *rest):
    tail_args, hp, conv = rest[:-2], rest[-2], rest[-1]
    i = pl.program_id(0)
    k = (i - PROMPT_TILES) % SAMPLE_TILES_PER_SEQ
    in_sample = i >= PROMPT_TILES
    left_ok = jnp.logical_and(in_sample, k != 0)
    right_ok = jnp.logical_and(in_sample, k != SAMPLE_TILES_PER_SEQ - 1)
    hp[0:HALO, :] = jnp.where(left_ok, prev_ref[...], 0.0)
    hp[HALO:HALO + TM, :] = cur_ref[...]
    hp[HALO + TM:HALO + TM + HALO, :] = jnp.where(right_ok, next_ref[...], 0.0)
    _depthwise_conv(hp, dw_ref, conv)
    hc = _silu(_layer_norm(conv[...] + dwb_ref[...], cg_ref[...], cb_ref[...]))
    out = _dot(hc.astype(BF16), w2_ref[...]) + b2_ref[...]
    _mixer_tail(out, x_ref[...], *tail_args)


def _conv_tail(glu, dw, dwb, cg, cb, w2_bf16, b2, x, modr, ln_g, ln_b, wr, br, li):
    per = TM // HALO
    last = T // HALO - 1
    return pl.pallas_call(
        _conv_tail_kernel,
        grid=(N_TILES,),
        in_specs=[pl.BlockSpec((TM, D), lambda i: (i, 0)),
                  pl.BlockSpec((HALO, D), lambda i: (jnp.maximum(i * per - 1, 0), 0)),
                  pl.BlockSpec((HALO, D), lambda i: (jnp.minimum((i + 1) * per, last), 0)),
                  _row_spec((CONV_K, D)), _row_spec((1, D)), _row_spec((1, D)), _row_spec((1, D)),
                  _row_spec((D, D)), _row_spec((1, D)),
                  pl.BlockSpec((TM, D), lambda i: (i, 0))] + _tail_in_specs(li),
        out_specs=_tail_out_specs(),
        out_shape=_TAIL_OUT_SHAPES,
        scratch_shapes=[pltpu.VMEM((TM + 2 * HALO, D), F32), pltpu.VMEM((TM, D), F32)],
        compiler_params=_cparams(("arbitrary",)),
        name="conv_tail",
    )(glu, glu, glu, dw, dwb, cg, cb, w2_bf16, b2, x, modr, modr, modr, ln_g, ln_b, wr, br)


def _sorted_positions(meta, srcv):
    lane = lax.broadcasted_iota(jnp.int32, meta.shape, 1).astype(F32)

    def pos(e_col, r_col):
        start = jnp.sum(jnp.where(lane == meta[:, e_col:e_col + 1], srcv, 0.0), axis=-1, keepdims=True)
        return start + meta[:, r_col:r_col + 1]

    return pos(META_E1, META_RANK1), pos(META_E2, META_RANK2)


def _one_hot_rows(pos):
    col = lax.broadcasted_iota(jnp.int32, (TM, SORT_ROWS), 1).astype(F32)
    return col == pos


def _for_each_row_group(tile, tot_ref, dstg_ref, fn):
    def body(g, carry):
        fn(pl.multiple_of(g * RUN_ALIGN, RUN_ALIGN),
           pl.multiple_of(dstg_ref[tile * SORT_GROUPS + g], RUN_ALIGN))
        return carry

    lax.fori_loop(0, tot_ref[tile] // RUN_ALIGN, body, 0)


def _wait_rows(total, make_copy):
    for bit in RUN_BITS:
        @pl.when((total & bit) != 0)
        def _(bit=bit):
            make_copy(bit).wait()


def _dispatch_kernel(tot_ref, dstg_ref, u_ref, meta_ref, srcv_ref, xs_ref, sorted_ref, sems):
    i = pl.program_id(0)
    slot = i % 2

    def wait_tile(tile, slot):
        buf = sorted_ref.at[slot]
        _wait_rows(tot_ref[tile], lambda rows: pltpu.make_async_copy(
            buf.at[pl.ds(0, rows)], xs_ref.at[pl.ds(0, rows)], sems.at[slot]))

    @pl.when(i >= 2)
    def _():
        wait_tile(i - 2, slot)

    pos1, pos2 = _sorted_positions(meta_ref[...], srcv_ref[0])
    select = jnp.logical_or(_one_hot_rows(pos1), _one_hot_rows(pos2)).astype(BF16)
    sorted_ref[slot] = _dot_tn(select, u_ref[...])
    buf = sorted_ref.at[slot]

    def start(src, dst):
        pltpu.make_async_copy(buf.at[pl.ds(src, RUN_ALIGN)], xs_ref.at[pl.ds(dst, RUN_ALIGN)],
                              sems.at[slot]).start()

    _for_each_row_group(i, tot_ref, dstg_ref, start)

    @pl.when(i == N_TILES - 1)
    def _():
        wait_tile(i - 1, 1 - slot)
        wait_tile(i, slot)


def _dispatch(sched, u2, meta, srcv):
    return pl.pallas_call(
        _dispatch_kernel,
        grid_spec=pltpu.PrefetchScalarGridSpec(
            num_scalar_prefetch=2,
            grid=(N_TILES,),
            in_specs=[pl.BlockSpec((TM, D), lambda i, *_: (i, 0)),
                      pl.BlockSpec((TM, LANES), lambda i, *_: (i, 0)),
                      pl.BlockSpec((1, 1, LANES), lambda i, *_: (i, 0, 0))],
            out_specs=pl.BlockSpec(memory_space=pl.ANY),
            scratch_shapes=[pltpu.VMEM((2, SORT_ROWS, D), F32), pltpu.SemaphoreType.DMA((2,))],
        ),
        out_shape=jax.ShapeDtypeStruct((MOE_ROWS, D), F32),
        compiler_params=_cparams(("arbitrary",)),
        name="moe_dispatch",
    )(*sched, u2, meta, srcv)


def _experts_kernel(start_ref, chunks_ref, xs_ref, wg_ref, wu_ref, wd_ref, ys_ref,
                    wg_bf, wu_bf, wd_bf, xbuf, ybuf, in_sems, out_sems):
    e = pl.program_id(0)
    n = chunks_ref[e]
    first = start_ref[e] // MOE_TM
    total = start_ref[N_EXPERTS - 1] // MOE_TM + chunks_ref[N_EXPERTS - 1]

    def rows(g):
        return pl.ds(pl.multiple_of(g * MOE_TM, MOE_TM), MOE_TM)

    def load(g):
        slot = g % MOE_IN_SLOTS
        return pltpu.make_async_copy(xs_ref.at[rows(g)], xbuf.at[slot], in_sems.at[slot])

    def store(g):
        slot = g % MOE_OUT_SLOTS
        return pltpu.make_async_copy(ybuf.at[slot], ys_ref.at[rows(g)], out_sems.at[slot])

    @pl.when(e == 0)
    def _():
        for g in range(MOE_AHEAD):
            @pl.when(g < total)
            def _(g=g):
                load(g).start()

    @pl.when(n > 0)
    def _():
        wg_bf[...] = wg_ref[0, 0].astype(BF16)
        wu_bf[...] = wu_ref[0, 0].astype(BF16)
        wd_bf[...] = wd_ref[0, 0].astype(BF16)

        def tile(g, carry):
            load(g).wait()

            @pl.when(g + MOE_AHEAD < total)
            def _():
                load(g + MOE_AHEAD).start()

            @pl.when(g >= MOE_OUT_SLOTS)
            def _():
                store(g - MOE_OUT_SLOTS).wait()

            x = xbuf[g % MOE_IN_SLOTS].astype(BF16)
            h = (_silu(_dot(x, wg_bf[...])) * _dot(x, wu_bf[...])).astype(BF16)
            ybuf[g % MOE_OUT_SLOTS] = _dot(h, wd_bf[...])
            store(g).start()
            return carry

        lax.fori_loop(first, first + n, tile, 0)

    @pl.when(e == N_EXPERTS - 1)
    def _():
        for back in range(MOE_OUT_SLOTS, 0, -1):
            @pl.when(total >= back)
            def _(back=back):
                store(total - back).wait()


def _experts(seg_start, seg_chunks, xs, w_gate, w_up, w_down, li):
    def weight(shape):
        return pl.BlockSpec((1, 1) + shape, lambda e, *_: (li, e, 0, 0))

    return pl.pallas_call(
        _experts_kernel,
        grid_spec=pltpu.PrefetchScalarGridSpec(
            num_scalar_prefetch=2,
            grid=(N_EXPERTS,),
            in_specs=[pl.BlockSpec(memory_space=pl.ANY),
                      weight((D, D_EXPERT)), weight((D, D_EXPERT)), weight((D_EXPERT, D))],
            out_specs=pl.BlockSpec(memory_space=pl.ANY),
            scratch_shapes=[pltpu.VMEM((D, D_EXPERT), BF16), pltpu.VMEM((D, D_EXPERT), BF16),
                            pltpu.VMEM((D_EXPERT, D), BF16),
                            pltpu.VMEM((MOE_IN_SLOTS, MOE_TM, D), F32),
                            pltpu.VMEM((MOE_OUT_SLOTS, MOE_TM, D), F32),
                            pltpu.SemaphoreType.DMA((MOE_IN_SLOTS,)),
                            pltpu.SemaphoreType.DMA((MOE_OUT_SLOTS,))],
        ),
        out_shape=jax.ShapeDtypeStruct((MOE_ROWS, D), F32),
        compiler_params=_cparams(("arbitrary",)),
        name="moe_experts",
    )(seg_start, seg_chunks, xs, w_gate, w_up, w_down)


def _combine_kernel(tot_ref, dstg_ref, ys_ref, x1_ref, meta_ref, srcv_ref, g2_ref,
                    lng_ref, lnb_ref, *rest, split):
    outs, (sorted_ref, sems) = rest[:-2], rest[-2:]
    i = pl.program_id(0)
    slot = i % 2

    def fetch(tile, slot):
        buf = sorted_ref.at[slot]

        def start(src, dst):
            pltpu.make_async_copy(ys_ref.at[pl.ds(dst, RUN_ALIGN)], buf.at[pl.ds(src, RUN_ALIGN)],
                                  sems.at[slot]).start()

        _for_each_row_group(tile, tot_ref, dstg_ref, start)

    @pl.when(i == 0)
    def _():
        sorted_ref[...] = jnp.zeros_like(sorted_ref)
        fetch(0, 0)

    @pl.when(i + 1 < N_TILES)
    def _():
        fetch(i + 1, 1 - slot)

    meta = meta_ref[...]
    pos1, pos2 = _sorted_positions(meta, srcv_ref[0])
    sel1 = _one_hot_rows(pos1).astype(BF16)
    sel2 = _one_hot_rows(pos2).astype(BF16)
    buf = sorted_ref.at[slot]
    _wait_rows(tot_ref[i], lambda rows: pltpu.make_async_copy(
        ys_ref.at[pl.ds(0, rows)], buf.at[pl.ds(0, rows)], sems.at[slot]))
    ysort = sorted_ref[slot].astype(BF16)
    f = (meta[:, META_W1:META_W1 + 1] * _dot(sel1, ysort)
         + meta[:, META_W2:META_W2 + 1] * _dot(sel2, ysort))
    y = _layer_norm(ALPHA * x1_ref[...] + g2_ref[0] * f, lng_ref[...], lnb_ref[...])
    if split:
        @pl.when(i < PROMPT_TILES)
        def _():
            outs[0][...] = y

        @pl.when(i >= PROMPT_TILES)
        def _():
            outs[1][...] = y
    else:
        outs[0][...] = y


def _combine(sched, ys, x1, meta, srcv, modr, ln_g, ln_b, li, split):
    if split:
        out_specs = [_prompt_tile_spec(D), _sample_tile_spec(D)]
        out_shape = [jax.ShapeDtypeStruct((T_PROMPT, D), F32), jax.ShapeDtypeStruct((T_SAMPLE, D), F32)]
    else:
        out_specs = [pl.BlockSpec((TM, D), lambda i, *_: (i, 0))]
        out_shape = [jax.ShapeDtypeStruct((T, D), F32)]
    return pl.pallas_call(
        functools.partial(_combine_kernel, split=split),
        grid_spec=pltpu.PrefetchScalarGridSpec(
            num_scalar_prefetch=2,
            grid=(N_TILES,),
            in_specs=[pl.BlockSpec(memory_space=pl.ANY),
                      pl.BlockSpec((TM, D), lambda i, *_: (i, 0)),
                      pl.BlockSpec((TM, LANES), lambda i, *_: (i, 0)),
                      pl.BlockSpec((1, 1, LANES), lambda i, *_: (i, 0, 0)),
                      _mod_spec(li, 5), _row_spec((1, D)), _row_spec((1, D))],
            out_specs=out_specs,
            scratch_shapes=[pltpu.VMEM((2, SORT_ROWS, D), F32), pltpu.SemaphoreType.DMA((2,))],
        ),
        out_shape=out_shape,
        compiler_params=_cparams(("arbitrary",)),
        name="moe_combine",
    )(*sched, ys, x1, meta, srcv, modr, ln_g, ln_b)


def _moe_schedule(tile_counts):
    n = (tile_counts + RUN_ALIGN - 1) // RUN_ALIGN * RUN_ALIGN
    src = jnp.cumsum(n, axis=1) - n
    per_expert = jnp.sum(n, axis=0)
    seg = (per_expert + MOE_TM - 1) // MOE_TM * MOE_TM
    seg_start = jnp.cumsum(seg) - seg
    dst = seg_start[None, :] + jnp.cumsum(n, axis=0) - n
    g_row = jnp.arange(SORT_GROUPS, dtype=jnp.int32) * RUN_ALIGN
    in_run = jnp.logical_and(src[:, None, :] <= g_row[None, :, None],
                             g_row[None, :, None] < (src + n)[:, None, :])
    dst_g = g_row[None, :] + jnp.sum(jnp.where(in_run, (dst - src)[:, None, :], 0), axis=2)
    runs = (jnp.sum(n, axis=1), dst_g.reshape(-1))
    srcv = jnp.pad(src.astype(F32), ((0, 0), (ROUTER_LANE0, LANES - ROUTER_LANE0 - N_EXPERTS)))
    return runs, srcv.reshape(N_TILES, 1, LANES), (seg_start, seg // MOE_TM)


def _moe(x1, u2, meta, cnt, modr, w_gate, w_up, w_down, ln_g, ln_b, li, split):
    tile_counts = cnt[:, 0, ROUTER_LANE0:ROUTER_LANE0 + N_EXPERTS].astype(jnp.int32)
    runs, srcv, (seg_start, seg_chunks) = _moe_schedule(tile_counts)
    xs = _dispatch(runs, u2, meta, srcv)
    ys = _experts(seg_start, seg_chunks, xs, w_gate, w_up, w_down, li)
    return _combine(runs, ys, x1, meta, srcv, modr, ln_g, ln_b, li, split)


def _router_slab(wg, bg, we, be):
    w = jnp.concatenate([wg, we.transpose(1, 0, 2).reshape(D, N_EXPERTS)], axis=1)
    b = jnp.concatenate([bg, be.reshape(N_EXPERTS)])
    pad = LANES - w.shape[1]
    return jnp.pad(w, ((0, 0), (0, pad))), jnp.pad(b, (0, pad)).reshape(1, LANES)


def kernel(x_prompt, x_sample, cache_diff_k, cache_diff_v, state_ret_fwd, state_ret_bwd, c, c_ctx, mod_w, mod_b, ln1_g, ln1_b, ln2_g, ln2_b, mix_w_in, mix_w_out, ret_decay_fwd, ret_decay_bwd, diff_lq1, diff_lk1, diff_lq2, diff_lk2, diff_subln_g, conv_w1, conv_b1, conv_dw, conv_dw_b, conv_ln_g, conv_ln_b, conv_w2, conv_b2, router_g_w, router_g_b, router_e_w, router_e_b, moe_w_gate, moe_w_up, moe_w_down):
    xp = x_prompt.reshape(T_PROMPT, D)
    xs = x_sample.reshape(T_SAMPLE, D)
    cond = jnp.concatenate([c_ctx[None, :], c, jnp.zeros((MOD_ROWS - 1 - DEC_BATCH, D), F32)], axis=0)
    modr = _mod_vectors(cond, mod_w, mod_b).reshape(DEPTH * MOD_ROWS * 6, 1, D)
    cos, sin_signed = _rope_tables()

    def row(v):
        return v.reshape(1, -1)

    x = None
    caches = None
    for li in range(DEPTH):
        wr, br = _router_slab(router_g_w[li], router_g_b[li], router_e_w[li], router_e_b[li])
        if li % 2 == 0:
            assert li == 0, "the even mixer reads the kernel inputs directly"
            e = li // 2
            lam_init = 0.8 - 0.6 * math.exp(-0.3 * li)
            proj, ck, cv = _in_proj(xp, xs, modr, mix_w_in[e].astype(BF16), li)
            dec = jnp.concatenate([ret_decay_fwd[e], ret_decay_bwd[e]])
            r_p, sf, sb = _retention(proj, dec, BATCH, SEQ, 0, HEADS, emit_state=True)
            (r_s,) = _retention(proj, dec, DEC_BATCH, DEC_SEQ, T_PROMPT // DEC_SEQ, 2,
                                s0f=state_ret_fwd, s0b=state_ret_bwd, e=e)
            lams = (row(diff_lq1[e]), row(diff_lk1[e]), row(diff_lq2[e]), row(diff_lk2[e]),
                    row(diff_subln_g[e]))
            o_p = _attn_prompt(proj, *lams, lam_init)
            o_s = _attn_sample(proj, cache_diff_k, cache_diff_v, cos, sin_signed, *lams, lam_init, e)
            x1, u2, meta, cnt = _out_proj_tail(r_p, r_s, o_p, o_s, mix_w_out[e].astype(BF16), xp, xs, modr,
                                               row(ln1_g[li]), row(ln1_b[li]), wr, br, li)
            caches = (ck, cv, sf, sb)
        else:
            o = li // 2
            glu = _conv_glu(x, modr, conv_w1[o].astype(BF16), row(conv_b1[o]), li)
            x1, u2, meta, cnt = _conv_tail(glu, conv_dw[o], row(conv_dw_b[o]), row(conv_ln_g[o]),
                                           row(conv_ln_b[o]), conv_w2[o].astype(BF16), row(conv_b2[o]),
                                           x, modr, row(ln1_g[li]), row(ln1_b[li]), wr, br, li)
        outs = _moe(x1, u2, meta, cnt, modr, moe_w_gate, moe_w_up, moe_w_down,
                    row(ln2_g[li]), row(ln2_b[li]), li, split=(li == DEPTH - 1))
        x = outs[0]

    y_prompt = outs[0].reshape(BATCH, SEQ, D)
    y_sample = outs[1].reshape(DEC_BATCH, DEC_SEQ, D)
    return (y_prompt, y_sample) + caches
```

```python
import functools
import math

import numpy as np
import jax
import jax.numpy as jnp
from jax import lax
from jax.experimental import pallas as pl
from jax.experimental.pallas import tpu as pltpu

F32 = jnp.float32
BF16 = jnp.bfloat16

D = 1024
BATCH = 16
SEQ = 256
DEPTH = 2
DEC_BATCH = 2
DEC_SEQ = 2048
PAST_LEN = 512
GRID_W = 64
HEADS = 4
HEAD_W = 128
RET_CHUNK = 128
DIFF_DK = 64
ROPE_THETA = 10000.0
IN_W = 7 * HEADS * HEAD_W
CONV_K = 31
CONV_PAD = CONV_K // 2
N_GROUPS = 4
EXPERTS_PER_GROUP = 8
N_EXPERTS = N_GROUPS * EXPERTS_PER_GROUP
D_EXPERT = 512
ALPHA = (2.0 * DEPTH) ** 0.25
LN_EPS = 1e-5
GN_EPS = 1e-6

T_PROMPT = BATCH * SEQ
T_SAMPLE = DEC_BATCH * DEC_SEQ
T = T_PROMPT + T_SAMPLE
TM = 256
N_TILES = T // TM
PROMPT_TILES = T_PROMPT // TM
SAMPLE_TILES_PER_SEQ = DEC_SEQ // TM
MOD_ROWS = 8
MOE_TM = 256
LANES = 128
SUBLANES = 8
RUN_ALIGN = SUBLANES
SORT_ROWS = -(-(2 * TM + N_EXPERTS * (RUN_ALIGN - 1)) // TM) * TM
RUN_BITS = tuple(1 << b for b in range((2 * TM).bit_length() - 1, RUN_ALIGN.bit_length() - 2, -1))
MOE_MAX_TILES = -(-(2 * T + N_TILES * N_EXPERTS * (RUN_ALIGN - 1) + N_EXPERTS * (MOE_TM - RUN_ALIGN)) // MOE_TM)
MOE_ROWS = MOE_MAX_TILES * MOE_TM
SORT_GROUPS = SORT_ROWS // RUN_ALIGN
MOE_AHEAD = 2
MOE_IN_SLOTS = MOE_AHEAD + 1
MOE_OUT_SLOTS = 2
ROUTER_LANE0 = N_GROUPS
VMEM_LIMIT = 52 * 1024 * 1024


def _cparams(sem):
    return pltpu.CompilerParams(dimension_semantics=sem, vmem_limit_bytes=VMEM_LIMIT)


def _tile_cond_row(i):
    return jnp.where(i < PROMPT_TILES, 0, 1 + (i - PROMPT_TILES) // SAMPLE_TILES_PER_SEQ)


def _mod_spec(li, k):
    return pl.BlockSpec((1, 1, D), lambda i, *_: ((li * MOD_ROWS + _tile_cond_row(i)) * 6 + k, 0, 0))


def _row_spec(shape):
    return pl.BlockSpec(shape, lambda i, *_: (0,) * len(shape))


def _resident_f32_weight(shape):
    return pl.BlockSpec(shape, lambda i, *_: (0,) * len(shape), pipeline_mode=pl.Buffered(1))


def _cast_weight_once(w_ref, w_bf):
    @pl.when(pl.program_id(0) == 0)
    def _():
        w_bf[...] = w_ref[...].astype(BF16)


def _layer_norm(x, g, b):
    mu = jnp.mean(x, axis=-1, keepdims=True)
    xc = x - mu
    var = jnp.mean(xc * xc, axis=-1, keepdims=True)
    return xc * lax.rsqrt(var + LN_EPS) * g + b


def _silu(x):
    return x * jax.nn.sigmoid(x)


def _dot(a, b):
    return jnp.dot(a, b, preferred_element_type=F32)


def _dot_nt(a, b):
    return lax.dot_general(a, b, (((1,), (1,)), ((), ())), preferred_element_type=F32)


def _dot_tn(a, b):
    return lax.dot_general(a, b, (((0,), (0,)), ((), ())), preferred_element_type=F32)


MOD_TN = 1024
MOD_USED_ROWS = 1 + DEC_BATCH


def _mod_kernel(cond_t_ref, w_ref, b_ref, o_ref):
    s = _silu(cond_t_ref[...])
    w = w_ref[0]
    o_ref[0] = jnp.zeros((MOD_ROWS, MOD_TN), F32) + b_ref[0]
    for r in range(MOD_USED_ROWS):
        o_ref[0, r:r + 1, :] = jnp.sum(w * s[:, r:r + 1], axis=0, keepdims=True) + b_ref[0]


def _mod_vectors(cond, mod_w, mod_b):
    return pl.pallas_call(
        _mod_kernel,
        grid=(DEPTH, 6 * D // MOD_TN),
        in_specs=[
            pl.BlockSpec((D, MOD_ROWS), lambda l, j: (0, 0)),
            pl.BlockSpec((1, D, MOD_TN), lambda l, j: (l, 0, j)),
            pl.BlockSpec((1, 1, MOD_TN), lambda l, j: (l, 0, j)),
        ],
        out_specs=pl.BlockSpec((1, MOD_ROWS, MOD_TN), lambda l, j: (l, 0, j)),
        out_shape=jax.ShapeDtypeStruct((DEPTH, MOD_ROWS, 6 * D), F32),
        compiler_params=_cparams(("arbitrary", "arbitrary")),
        name="mod_vectors",
    )(cond.T, mod_w, mod_b.reshape(DEPTH, 1, 6 * D))


def _prompt_tile_spec(width):
    return pl.BlockSpec((TM, width), lambda i, *_: (jnp.minimum(i, PROMPT_TILES - 1), 0))


def _sample_tile_spec(width):
    return pl.BlockSpec((TM, width), lambda i, *_: (jnp.maximum(i - PROMPT_TILES, 0), 0))


def _pick_tile(prompt_ref, sample_ref):
    return jnp.where(pl.program_id(0) < PROMPT_TILES, prompt_ref[...], sample_ref[...])


def _in_proj_kernel(xp_ref, xs_ref, sh_ref, sc_ref, w_ref, o_ref, ck_ref, cv_ref, w_bf):
    _cast_weight_once(w_ref, w_bf)
    u = _pick_tile(xp_ref, xs_ref) * (1.0 + sc_ref[0]) + sh_ref[0]
    proj = _dot(u.astype(BF16), w_bf[...])
    o_ref[...] = proj

    @pl.when(pl.program_id(0) < PROMPT_TILES)
    def _():
        for h in range(HEADS):
            ck_ref[0, 0, h] = proj[:, (COL_KD + h) * HEAD_W:(COL_KD + h + 1) * HEAD_W]
            cv_ref[0, 0, h] = proj[:, (COL_VD + h) * HEAD_W:(COL_VD + h + 1) * HEAD_W]


def _in_proj(x_prompt, x_sample, modr, w_in, li):
    cache_spec = pl.BlockSpec((1, 1, HEADS, SEQ, HEAD_W),
                              lambda i: (jnp.minimum(i, PROMPT_TILES - 1), 0, 0, 0, 0))
    cache_shape = jax.ShapeDtypeStruct((BATCH, 1, HEADS, SEQ, HEAD_W), F32)
    return pl.pallas_call(
        _in_proj_kernel,
        grid=(N_TILES,),
        in_specs=[
            _prompt_tile_spec(D), _sample_tile_spec(D),
            _mod_spec(li, 0),
            _mod_spec(li, 1),
            _resident_f32_weight((D, IN_W)),
        ],
        out_specs=[pl.BlockSpec((TM, IN_W), lambda i: (i, 0)), cache_spec, cache_spec],
        out_shape=[jax.ShapeDtypeStruct((T, IN_W), F32), cache_shape, cache_shape],
        scratch_shapes=[pltpu.VMEM((D, IN_W), BF16)],
        compiler_params=_cparams(("arbitrary",)),
        name="in_proj",
    )(x_prompt, x_sample, modr, modr, w_in)


COL_QR, COL_KR, COL_VR, COL_GR, COL_QD, COL_KD, COL_VD = (k * HEADS for k in range(7))


def _retention_kernel(dec_ref, q_ref, k_ref, v_ref, g_ref, *rest, n_chunks, n_heads, has_state, emit_state):
    rest = list(rest)
    if has_state:
        s0f_ref, s0b_ref = rest[:2]
        rest = rest[2:]
    r_ref = rest[0]
    rest = rest[1:]
    if emit_state:
        sf_ref, sb_ref = rest[:2]
        rest = rest[2:]
    of_ref = rest[0]

    head0 = pl.program_id(1) * n_heads
    C = RET_CHUNK
    ii = lax.broadcasted_iota(jnp.int32, (C, C), 0)
    jj = lax.broadcasted_iota(jnp.int32, (C, C), 1)
    rel = (ii - jj).astype(F32)
    idx = lax.broadcasted_iota(jnp.int32, (C, 1), 0).astype(F32)
    k_scale = HEAD_W ** -0.5

    def chunk(ref, c, h):
        return ref[c * C:(c + 1) * C, h * HEAD_W:(h + 1) * HEAD_W]

    def decays(direction, h):
        lg = -jnp.exp(jnp.full((1, 1), dec_ref[direction * HEADS + head0 + h], F32))
        if direction == 0:
            inner = jnp.where(rel >= 0, jnp.exp(jnp.maximum(rel, 0.0) * lg), 0.0)
            return inner, jnp.exp((idx + 1.0) * lg), jnp.exp((C - 1.0 - idx) * lg), jnp.exp(C * lg)
        inner = jnp.where(rel <= 0, jnp.exp(jnp.maximum(-rel, 0.0) * lg), 0.0)
        return inner, jnp.exp((C - idx) * lg), jnp.exp(idx * lg), jnp.exp(C * lg)

    def run(direction):
        dec = [decays(direction, h) for h in range(n_heads)]
        if has_state:
            s0_ref = s0f_ref if direction == 0 else s0b_ref
            states = [s0_ref[0, 0, h] for h in range(n_heads)]
        else:
            states = [jnp.zeros((HEAD_W, HEAD_W), F32) for _ in range(n_heads)]
        order = range(n_chunks) if direction == 0 else range(n_chunks - 1, -1, -1)
        for c in order:
            rows = slice(c * C, (c + 1) * C)
            for h in range(n_heads):
                inner, q_decay, k_decay, chunk_decay = dec[h]
                cols = slice(h * HEAD_W, (h + 1) * HEAD_W)
                s = states[h]
                qc = chunk(q_ref, c, h)
                kc = chunk(k_ref, c, h) * k_scale
                vc = chunk(v_ref, c, h).astype(BF16)
                scores = _dot_nt(qc.astype(BF16), kc.astype(BF16)) * inner
                o = _dot(scores.astype(BF16), vc) + _dot((qc * q_decay).astype(BF16), s.astype(BF16))
                states[h] = s * chunk_decay + _dot_tn((kc * k_decay).astype(BF16), vc)
                if direction == 0:
                    of_ref[rows, cols] = o
                else:
                    r = of_ref[rows, cols] + o
                    mu = jnp.mean(r, axis=-1, keepdims=True)
                    rc = r - mu
                    var = jnp.mean(rc * rc, axis=-1, keepdims=True)
                    rn = rc * lax.rsqrt(var + GN_EPS)
                    r_ref[rows, cols] = _silu(chunk(g_ref, c, h)) * rn
        return states

    sf = run(0)
    sb = run(1)
    if emit_state:
        for h in range(n_heads):
            sf_ref[0, 0, h] = sf[h]
            sb_ref[0, 0, h] = sb[h]


def _retention(proj, dec, n_seq, seq_len, row_block0, n_heads, s0f=None, s0b=None, e=0, emit_state=False):
    has_state = s0f is not None
    width = n_heads * HEAD_W

    def col(base):
        return pl.BlockSpec((seq_len, width), lambda b, h, *_: (row_block0 + b, base // n_heads + h))

    state_spec = pl.BlockSpec((1, 1, n_heads, HEAD_W, HEAD_W), lambda b, h, *_: (b, e, h, 0, 0))
    in_specs = [pl.BlockSpec(memory_space=pltpu.SMEM), col(COL_QR), col(COL_KR), col(COL_VR), col(COL_GR)]
    args = [dec, proj, proj, proj, proj]
    if has_state:
        in_specs += [state_spec, state_spec]
        args += [s0f, s0b]
    out_specs = [pl.BlockSpec((seq_len, width), lambda b, h, *_: (b, h))]
    out_shape = [jax.ShapeDtypeStruct((n_seq * seq_len, HEADS * HEAD_W), F32)]
    if emit_state:
        st = pl.BlockSpec((1, 1, n_heads, HEAD_W, HEAD_W), lambda b, h, *_: (b, 0, h, 0, 0))
        out_specs += [st, st]
        out_shape += [jax.ShapeDtypeStruct((n_seq, 1, HEADS, HEAD_W, HEAD_W), F32)] * 2
    return pl.pallas_call(
        functools.partial(_retention_kernel, n_chunks=seq_len // RET_CHUNK, n_heads=n_heads,
                          has_state=has_state, emit_state=emit_state),
        grid=(n_seq, HEADS // n_heads),
        in_specs=in_specs,
        out_specs=out_specs,
        out_shape=out_shape,
        scratch_shapes=[pltpu.VMEM((seq_len, width), F32)],
        compiler_params=_cparams(("arbitrary", "arbitrary")),
        name=f"retention_{seq_len}",
    )(*args)


def _diff_lambda(lq1_ref, lk1_ref, lq2_ref, lk2_ref, lam_init):
    a = jnp.sum(lq1_ref[...] * lk1_ref[...], axis=-1, keepdims=True)
    b = jnp.sum(lq2_ref[...] * lk2_ref[...], axis=-1, keepdims=True)
    return jnp.exp(a) - jnp.exp(b) + lam_init


LOG2E = 1.4426950408889634


def _diff_attend(q, k, v, lam, subln_g, lam_init):
    lane = lax.broadcasted_iota(jnp.int32, q.shape, 1)
    q1 = jnp.where(lane < DIFF_DK, q, 0.0).astype(BF16)
    q2 = jnp.where(lane >= DIFF_DK, q, 0.0).astype(BF16)

    def softmax_times_v(qz):
        s = _dot_nt(qz, k)
        p = jnp.exp2(s - jnp.max(s, axis=-1, keepdims=True))
        return _dot(p.astype(BF16), v) * (1.0 / jnp.sum(p, axis=-1, keepdims=True))

    o = softmax_times_v(q1) - lam * softmax_times_v(q2)
    o = o * lax.rsqrt(jnp.mean(o * o, axis=-1, keepdims=True) + LN_EPS)
    return o * subln_g * (1.0 - lam_init)


def _attn_prompt_kernel(q_ref, k_ref, v_ref, lq1, lk1, lq2, lk2, g_ref, o_ref, *, lam_init):
    lam = _diff_lambda(lq1, lk1, lq2, lk2, lam_init)
    scale = DIFF_DK ** -0.5 * LOG2E
    for h in range(HEADS):
        sl = slice(h * HEAD_W, (h + 1) * HEAD_W)
        o_ref[:, sl] = _diff_attend(q_ref[:, sl] * scale, k_ref[:, sl].astype(BF16),
                                    v_ref[:, sl].astype(BF16), lam, g_ref[...], lam_init)


def _attn_prompt(proj, lq1, lk1, lq2, lk2, subln_g, lam_init):
    W = HEADS * HEAD_W

    def slab(base):
        return pl.BlockSpec((SEQ, W), lambda b: (b, base // HEADS))

    small = _row_spec((1, DIFF_DK))
    return pl.pallas_call(
        functools.partial(_attn_prompt_kernel, lam_init=lam_init),
        grid=(BATCH,),
        in_specs=[slab(COL_QD), slab(COL_KD), slab(COL_VD), small, small, small, small,
                  _row_spec((1, HEAD_W))],
        out_specs=pl.BlockSpec((SEQ, W), lambda b: (b, 0)),
        out_shape=jax.ShapeDtypeStruct((T_PROMPT, W), F32),
        compiler_params=_cparams(("arbitrary",)),
        name="diff_attn_prompt",
    )(proj, proj, proj, lq1, lk1, lq2, lk2, subln_g)


def _rope(x, cos, sin_signed):
    lane = lax.broadcasted_iota(jnp.int32, x.shape, 1)
    partner = jnp.where((lane % 32) < 16, pltpu.roll(x, LANES - 16, 1), pltpu.roll(x, 16, 1))
    return x * cos + partner * sin_signed


def _attn_sample_kernel(q_ref, k_ref, v_ref, ck_ref, cv_ref, cosq_ref, sinq_ref, cos_ref, sin_ref,
                        lq1, lk1, lq2, lk2, g_ref, o_ref, kbuf, vbuf, *, lam_init):
    @pl.when(pl.program_id(2) == 0)
    def _():
        kbuf[0:DEC_SEQ, :] = _rope(k_ref[...], cos_ref[...], sin_ref[...]).astype(BF16)
        kbuf[DEC_SEQ:, :] = ck_ref[0, 0, 0].astype(BF16)
        vbuf[0:DEC_SEQ, :] = v_ref[...].astype(BF16)
        vbuf[DEC_SEQ:, :] = cv_ref[0, 0, 0].astype(BF16)

    lam = _diff_lambda(lq1, lk1, lq2, lk2, lam_init)
    q = _rope(q_ref[...], cosq_ref[...], sinq_ref[...]) * (DIFF_DK ** -0.5 * LOG2E)
    o_ref[...] = _diff_attend(q, kbuf[...], vbuf[...], lam, g_ref[...], lam_init)


ATTN_TQ = 512


def _attn_sample(proj, cache_k, cache_v, cos, sin_signed, lq1, lk1, lq2, lk2, subln_g, lam_init, e):
    nq = DEC_SEQ // ATTN_TQ
    row0_q = T_PROMPT // ATTN_TQ
    row0_kv = T_PROMPT // DEC_SEQ
    small = pl.BlockSpec((1, DIFF_DK), lambda b, h, t: (0, 0))
    cache = pl.BlockSpec((1, 1, 1, PAST_LEN, HEAD_W), lambda b, h, t: (b, e, h, 0, 0))
    table_q = pl.BlockSpec((ATTN_TQ, HEAD_W), lambda b, h, t: (t, 0))
    table = pl.BlockSpec((DEC_SEQ, HEAD_W), lambda b, h, t: (0, 0))
    return pl.pallas_call(
        functools.partial(_attn_sample_kernel, lam_init=lam_init),
        grid=(DEC_BATCH, HEADS, nq),
        in_specs=[
            pl.BlockSpec((ATTN_TQ, HEAD_W), lambda b, h, t: (row0_q + b * nq + t, COL_QD + h)),
            pl.BlockSpec((DEC_SEQ, HEAD_W), lambda b, h, t: (row0_kv + b, COL_KD + h)),
            pl.BlockSpec((DEC_SEQ, HEAD_W), lambda b, h, t: (row0_kv + b, COL_VD + h)),
            cache, cache, table_q, table_q, table, table,
            small, small, small, small,
            pl.BlockSpec((1, HEAD_W), lambda b, h, t: (0, 0)),
        ],
        out_specs=pl.BlockSpec((ATTN_TQ, HEAD_W), lambda b, h, t: (b * nq + t, h)),
        out_shape=jax.ShapeDtypeStruct((T_SAMPLE, HEADS * HEAD_W), F32),
        scratch_shapes=[pltpu.VMEM((DEC_SEQ + PAST_LEN, HEAD_W), BF16),
                        pltpu.VMEM((DEC_SEQ + PAST_LEN, HEAD_W), BF16)],
        compiler_params=_cparams(("arbitrary", "arbitrary", "arbitrary")),
        name="diff_attn_sample",
    )(proj, proj, proj, cache_k, cache_v, cos, sin_signed, cos, sin_signed,
      lq1, lk1, lq2, lk2, subln_g)


def _rope_tables():
    t = np.arange(DEC_SEQ)
    row, colp = t // GRID_W, t % GRID_W
    lane = np.arange(LANES)
    pos = np.where(((lane // 32) % 2 == 0)[None, :], row[:, None], colp[:, None]).astype(np.float64)
    half = 16
    inv = (np.float32(ROPE_THETA) ** (-(np.arange(half, dtype=np.float32)) / np.float32(half))).astype(np.float32)
    ang = pos.astype(np.float32) * inv[lane % half][None, :]
    cos = np.cos(ang.astype(np.float64)).astype(np.float32)
    sin = np.sin(ang.astype(np.float64)).astype(np.float32)
    sign = np.where((lane % 32) < half, -1.0, 1.0).astype(np.float32)[None, :]
    return jnp.asarray(cos), jnp.asarray(sin * sign)


def _split_bf16(a):
    hi = a.astype(BF16)
    return hi, (a - hi.astype(F32)).astype(BF16)


def _mixer_tail(out, x, g1_ref, sc2_ref, sh2_ref, lng_ref, lnb_ref, wr_ref, br_ref,
                x1_ref, u2_ref, meta_ref, cnt_ref):
    x1 = _layer_norm(ALPHA * x + g1_ref[0] * out, lng_ref[...], lnb_ref[...])
    x1_ref[...] = x1
    u2 = x1 * (1.0 + sc2_ref[0]) + sh2_ref[0]
    u2_ref[...] = u2.astype(BF16)

    u_hi, u_lo = _split_bf16(u2)
    w_hi, w_lo = _split_bf16(wr_ref[...])
    logits = _dot(u_hi, w_hi) + (_dot(u_hi, w_lo) + _dot(u_lo, w_hi)) + br_ref[...]
    lane = lax.broadcasted_iota(jnp.int32, logits.shape, 1).astype(F32)
    neg = jnp.float32(-jnp.inf)
    is_g = lane < N_GROUPS
    gl = jnp.where(is_g, logits, neg)
    gmax = jnp.max(gl, axis=-1, keepdims=True)
    gsel = jnp.min(jnp.where(gl == gmax, lane, float(LANES)), axis=-1, keepdims=True)
    p_g = 1.0 / jnp.sum(jnp.where(is_g, jnp.exp(gl - gmax), 0.0), axis=-1, keepdims=True)
    lo = ROUTER_LANE0 + gsel * EXPERTS_PER_GROUP
    el = jnp.where((lane >= lo) & (lane < lo + EXPERTS_PER_GROUP), logits, neg)
    v1 = jnp.max(el, axis=-1, keepdims=True)
    i1 = jnp.min(jnp.where(el == v1, lane, float(LANES)), axis=-1, keepdims=True)
    el2 = jnp.where(lane == i1, neg, el)
    v2 = jnp.max(el2, axis=-1, keepdims=True)
    i2 = jnp.min(jnp.where(el2 == v2, lane, float(LANES)), axis=-1, keepdims=True)
    t = jnp.exp(v2 - v1)
    w1 = p_g / (1.0 + t)
    w2 = w1 * t

    oh1 = (lane == i1).astype(F32)
    oh2 = (lane == i2).astype(F32)
    oh = oh1 + oh2
    r_i = lax.broadcasted_iota(jnp.int32, (TM, TM), 0)
    c_i = lax.broadcasted_iota(jnp.int32, (TM, TM), 1)
    before = (c_i < r_i).astype(BF16)
    earlier = _dot(before, oh.astype(BF16))
    rank1 = jnp.sum(earlier * oh1, axis=-1, keepdims=True)
    rank2 = jnp.sum(earlier * oh2, axis=-1, keepdims=True)
    cnt_ref[0] = jnp.sum(oh, axis=0, keepdims=True)
    cols = (i1, i2, w1, w2, rank1, rank2)
    meta = jnp.zeros_like(logits)
    for k, col in enumerate(cols):
        meta = jnp.where(lane == k, col, meta)
    meta_ref[...] = meta


META_E1, META_E2, META_W1, META_W2, META_RANK1, META_RANK2 = range(6)

_TAIL_OUT_SHAPES = [
    jax.ShapeDtypeStruct((T, D), F32),
    jax.ShapeDtypeStruct((T, D), BF16),
    jax.ShapeDtypeStruct((T, LANES), F32),
    jax.ShapeDtypeStruct((N_TILES, 1, LANES), F32),
]


def _tail_out_specs():
    return [
        pl.BlockSpec((TM, D), lambda i: (i, 0)),
        pl.BlockSpec((TM, D), lambda i: (i, 0)),
        pl.BlockSpec((TM, LANES), lambda i: (i, 0)),
        pl.BlockSpec((1, 1, LANES), lambda i: (i, 0, 0)),
    ]


def _tail_in_specs(li):
    return [
        _mod_spec(li, 2), _mod_spec(li, 4), _mod_spec(li, 3),
        _row_spec((1, D)), _row_spec((1, D)),
        _row_spec((D, LANES)), _row_spec((1, LANES)),
    ]


def _out_proj_kernel(rp_ref, rs_ref, op_ref, os_ref, w_ref, xp_ref, xs_ref, *rest):
    tail_args, w_bf = rest[:-1], rest[-1]
    _cast_weight_once(w_ref, w_bf)
    half = HEADS * HEAD_W
    r = _pick_tile(rp_ref, rs_ref).astype(BF16)
    o = _pick_tile(op_ref, os_ref).astype(BF16)
    out = _dot(r, w_bf[0:half, :]) + _dot(o, w_bf[half:, :])
    _mixer_tail(out, _pick_tile(xp_ref, xs_ref), *tail_args)


def _out_proj_tail(r_p, r_s, o_p, o_s, w_out, x_prompt, x_sample, modr, ln_g, ln_b, wr, br, li):
    half = HEADS * HEAD_W
    return pl.pallas_call(
        _out_proj_kernel,
        grid=(N_TILES,),
        in_specs=[_prompt_tile_spec(half), _sample_tile_spec(half),
                  _prompt_tile_spec(half), _sample_tile_spec(half),
                  _resident_f32_weight((2 * half, D)),
                  _prompt_tile_spec(D), _sample_tile_spec(D)] + _tail_in_specs(li),
        out_specs=_tail_out_specs(),
        out_shape=_TAIL_OUT_SHAPES,
        scratch_shapes=[pltpu.VMEM((2 * half, D), BF16)],
        compiler_params=_cparams(("arbitrary",)),
        name="out_proj_tail",
    )(r_p, r_s, o_p, o_s, w_out, x_prompt, x_sample, modr, modr, modr, ln_g, ln_b, wr, br)


def _conv_glu_kernel(x_ref, sh_ref, sc_ref, w_ref, b_ref, o_ref, w_bf):
    _cast_weight_once(w_ref, w_bf)
    u = x_ref[...] * (1.0 + sc_ref[0]) + sh_ref[0]
    h = _dot(u.astype(BF16), w_bf[...]) + b_ref[...]
    o_ref[...] = h[:, :D] * jax.nn.sigmoid(h[:, D:])


def _conv_glu(x, modr, w1, b1, li):
    return pl.pallas_call(
        _conv_glu_kernel,
        grid=(N_TILES,),
        in_specs=[pl.BlockSpec((TM, D), lambda i: (i, 0)), _mod_spec(li, 0), _mod_spec(li, 1),
                  _resident_f32_weight((D, 2 * D)), _row_spec((1, 2 * D))],
        out_specs=pl.BlockSpec((TM, D), lambda i: (i, 0)),
        out_shape=jax.ShapeDtypeStruct((T, D), F32),
        scratch_shapes=[pltpu.VMEM((D, 2 * D), BF16)],
        compiler_params=_cparams(("arbitrary",)),
        name="conv_glu",
    )(x, modr, modr, w1, b1)


HALO = 16
CONV_ROWS = 64
CONV_COLS = 128


def _depthwise_conv(hp, dw_ref, conv):
    base = HALO - CONV_PAD
    for cb in range(D // CONV_COLS):
        cs = slice(cb * CONV_COLS, (cb + 1) * CONV_COLS)
        for rb in range(TM // CONV_ROWS):
            r0 = rb * CONV_ROWS
            acc = None
            for shift in range(SUBLANES):
                part = None
                for tap in range(CONV_K):
                    off = base + tap
                    if off % SUBLANES != shift:
                        continue
                    a0 = r0 + off - shift
                    term = hp[a0:a0 + CONV_ROWS + SUBLANES, cs] * dw_ref[tap:tap + 1, cs]
                    part = term if part is None else part + term
                part = part[shift:shift + CONV_ROWS, :]
                acc = part if acc is None else acc + part
            conv[r0:r0 + CONV_ROWS, cs] = acc


def _conv_tail_kernel(cur_ref, prev_ref, next_ref, dw_ref, dwb_ref, cg_ref, cb_ref, w2_ref, b2_ref,
                      x_ref, *rest):
    tail_args, (hp, conv, w2_bf) = rest[:-3], rest[-3:]
    _cast_weight_once(w2_ref, w2_bf)
    i = pl.program_id(0)
    k = (i - PROMPT_TILES) % SAMPLE_TILES_PER_SEQ
    in_sample = i >= PROMPT_TILES
    left_ok = jnp.logical_and(in_sample, k != 0)
    right_ok = jnp.logical_and(in_sample, k != SAMPLE_TILES_PER_SEQ - 1)
    hp[0:HALO, :] = jnp.where(left_ok, prev_ref[...], 0.0)
    hp[HALO:HALO + TM, :] = cur_ref[...]
    hp[HALO + TM:HALO + TM + HALO, :] = jnp.where(right_ok, next_ref[...], 0.0)
    _depthwise_conv(hp, dw_ref, conv)
    hc = _silu(_layer_norm(conv[...] + dwb_ref[...], cg_ref[...], cb_ref[...]))
    out = _dot(hc.astype(BF16), w2_bf[...]) + b2_ref[...]
    _mixer_tail(out, x_ref[...], *tail_args)


def _conv_tail(glu, dw, dwb, cg, cb, w2, b2, x, modr, ln_g, ln_b, wr, br, li):
    per = TM // HALO
    last = T // HALO - 1
    return pl.pallas_call(
        _conv_tail_kernel,
        grid=(N_TILES,),
        in_specs=[pl.BlockSpec((TM, D), lambda i: (i, 0)),
                  pl.BlockSpec((HALO, D), lambda i: (jnp.maximum(i * per - 1, 0), 0)),
                  pl.BlockSpec((HALO, D), lambda i: (jnp.minimum((i + 1) * per, last), 0)),
                  _row_spec((CONV_K, D)), _row_spec((1, D)), _row_spec((1, D)), _row_spec((1, D)),
                  _resident_f32_weight((D, D)), _row_spec((1, D)),
                  pl.BlockSpec((TM, D), lambda i: (i, 0))] + _tail_in_specs(li),
        out_specs=_tail_out_specs(),
        out_shape=_TAIL_OUT_SHAPES,
        scratch_shapes=[pltpu.VMEM((TM + 2 * HALO, D), F32), pltpu.VMEM((TM, D), F32),
                        pltpu.VMEM((D, D), BF16)],
        compiler_params=_cparams(("arbitrary",)),
        name="conv_tail",
    )(glu, glu, glu, dw, dwb, cg, cb, w2, b2, x, modr, modr, modr, ln_g, ln_b, wr, br)


def _sorted_positions(meta, srcv):
    lane = lax.broadcasted_iota(jnp.int32, meta.shape, 1).astype(F32)

    def pos(e_col, r_col):
        start = jnp.sum(jnp.where(lane == meta[:, e_col:e_col + 1], srcv, 0.0), axis=-1, keepdims=True)
        return start + meta[:, r_col:r_col + 1]

    return pos(META_E1, META_RANK1), pos(META_E2, META_RANK2)


def _one_hot_rows(pos):
    col = lax.broadcasted_iota(jnp.int32, (TM, SORT_ROWS), 1).astype(F32)
    return col == pos


def _for_each_row_group(tile, tot_ref, dstg_ref, fn):
    def body(g, carry):
        fn(pl.multiple_of(g * RUN_ALIGN, RUN_ALIGN),
           pl.multiple_of(dstg_ref[tile * SORT_GROUPS + g], RUN_ALIGN))
        return carry

    lax.fori_loop(0, tot_ref[tile] // RUN_ALIGN, body, 0)


def _wait_rows(total, make_copy):
    for bit in RUN_BITS:
        @pl.when((total & bit) != 0)
        def _(bit=bit):
            make_copy(bit).wait()


def _dispatch_kernel(tot_ref, dstg_ref, u_ref, meta_ref, srcv_ref, xs_ref, sorted_ref, sems):
    i = pl.program_id(0)
    slot = i % 2

    def wait_tile(tile, slot):
        buf = sorted_ref.at[slot]
        _wait_rows(tot_ref[tile], lambda rows: pltpu.make_async_copy(
            buf.at[pl.ds(0, rows)], xs_ref.at[pl.ds(0, rows)], sems.at[slot]))

    @pl.when(i >= 2)
    def _():
        wait_tile(i - 2, slot)

    pos1, pos2 = _sorted_positions(meta_ref[...], srcv_ref[0])
    select = jnp.logical_or(_one_hot_rows(pos1), _one_hot_rows(pos2)).astype(BF16)
    sorted_ref[slot] = _dot_tn(select, u_ref[...])
    buf = sorted_ref.at[slot]

    def start(src, dst):
        pltpu.make_async_copy(buf.at[pl.ds(src, RUN_ALIGN)], xs_ref.at[pl.ds(dst, RUN_ALIGN)],
                              sems.at[slot]).start()

    _for_each_row_group(i, tot_ref, dstg_ref, start)

    @pl.when(i == N_TILES - 1)
    def _():
        wait_tile(i - 1, 1 - slot)
        wait_tile(i, slot)


def _dispatch(sched, u2, meta, srcv):
    return pl.pallas_call(
        _dispatch_kernel,
        grid_spec=pltpu.PrefetchScalarGridSpec(
            num_scalar_prefetch=2,
            grid=(N_TILES,),
            in_specs=[pl.BlockSpec((TM, D), lambda i, *_: (i, 0)),
                      pl.BlockSpec((TM, LANES), lambda i, *_: (i, 0)),
                      pl.BlockSpec((1, 1, LANES), lambda i, *_: (i, 0, 0))],
            out_specs=pl.BlockSpec(memory_space=pl.ANY),
            scratch_shapes=[pltpu.VMEM((2, SORT_ROWS, D), F32), pltpu.SemaphoreType.DMA((2,))],
        ),
        out_shape=jax.ShapeDtypeStruct((MOE_ROWS, D), F32),
        compiler_params=_cparams(("arbitrary",)),
        name="moe_dispatch",
    )(*sched, u2, meta, srcv)


def _experts_kernel(start_ref, chunks_ref, xs_ref, wg_ref, wu_ref, wd_ref, ys_ref,
                    wg_bf, wu_bf, wd_bf, xbuf, ybuf, in_sems, out_sems):
    e = pl.program_id(0)
    n = chunks_ref[e]
    first = start_ref[e] // MOE_TM
    total = start_ref[N_EXPERTS - 1] // MOE_TM + chunks_ref[N_EXPERTS - 1]

    def rows(g):
        return pl.ds(pl.multiple_of(g * MOE_TM, MOE_TM), MOE_TM)

    def load(g):
        slot = g % MOE_IN_SLOTS
        return pltpu.make_async_copy(xs_ref.at[rows(g)], xbuf.at[slot], in_sems.at[slot])

    def store(g):
        slot = g % MOE_OUT_SLOTS
        return pltpu.make_async_copy(ybuf.at[slot], ys_ref.at[rows(g)], out_sems.at[slot])

    @pl.when(e == 0)
    def _():
        for g in range(MOE_AHEAD):
            @pl.when(g < total)
            def _(g=g):
                load(g).start()

    @pl.when(n > 0)
    def _():
        wg_bf[...] = wg_ref[0, 0].astype(BF16)
        wu_bf[...] = wu_ref[0, 0].astype(BF16)
        wd_bf[...] = wd_ref[0, 0].astype(BF16)

        def tile(g, carry):
            load(g).wait()

            @pl.when(g + MOE_AHEAD < total)
            def _():
                load(g + MOE_AHEAD).start()

            @pl.when(g >= MOE_OUT_SLOTS)
            def _():
                store(g - MOE_OUT_SLOTS).wait()

            x = xbuf[g % MOE_IN_SLOTS].astype(BF16)
            h = (_silu(_dot(x, wg_bf[...])) * _dot(x, wu_bf[...])).astype(BF16)
            ybuf[g % MOE_OUT_SLOTS] = _dot(h, wd_bf[...])
            store(g).start()
            return carry

        lax.fori_loop(first, first + n, tile, 0)

    @pl.when(e == N_EXPERTS - 1)
    def _():
        for back in range(MOE_OUT_SLOTS, 0, -1):
            @pl.when(total >= back)
            def _(back=back):
                store(total - back).wait()


def _experts(seg_start, seg_chunks, xs, w_gate, w_up, w_down, li):
    def weight(shape):
        return pl.BlockSpec((1, 1) + shape, lambda e, *_: (li, e, 0, 0))

    return pl.pallas_call(
        _experts_kernel,
        grid_spec=pltpu.PrefetchScalarGridSpec(
            num_scalar_prefetch=2,
            grid=(N_EXPERTS,),
            in_specs=[pl.BlockSpec(memory_space=pl.ANY),
                      weight((D, D_EXPERT)), weight((D, D_EXPERT)), weight((D_EXPERT, D))],
            out_specs=pl.BlockSpec(memory_space=pl.ANY),
            scratch_shapes=[pltpu.VMEM((D, D_EXPERT), BF16), pltpu.VMEM((D, D_EXPERT), BF16),
                            pltpu.VMEM((D_EXPERT, D), BF16),
                            pltpu.VMEM((MOE_IN_SLOTS, MOE_TM, D), F32),
                            pltpu.VMEM((MOE_OUT_SLOTS, MOE_TM, D), F32),
                            pltpu.SemaphoreType.DMA((MOE_IN_SLOTS,)),
                            pltpu.SemaphoreType.DMA((MOE_OUT_SLOTS,))],
        ),
        out_shape=jax.ShapeDtypeStruct((MOE_ROWS, D), F32),
        compiler_params=_cparams(("arbitrary",)),
        name="moe_experts",
    )(seg_start, seg_chunks, xs, w_gate, w_up, w_down)


def _combine_kernel(tot_ref, dstg_ref, ys_ref, x1_ref, meta_ref, srcv_ref, g2_ref,
                    lng_ref, lnb_ref, *rest, split):
    outs, (sorted_ref, sems) = rest[:-2], rest[-2:]
    i = pl.program_id(0)
    slot = i % 2

    def fetch(tile, slot):
        buf = sorted_ref.at[slot]

        def start(src, dst):
            pltpu.make_async_copy(ys_ref.at[pl.ds(dst, RUN_ALIGN)], buf.at[pl.ds(src, RUN_ALIGN)],
                                  sems.at[slot]).start()

        _for_each_row_group(tile, tot_ref, dstg_ref, start)

    @pl.when(i == 0)
    def _():
        sorted_ref[...] = jnp.zeros_like(sorted_ref)
        fetch(0, 0)

    @pl.when(i + 1 < N_TILES)
    def _():
        fetch(i + 1, 1 - slot)

    meta = meta_ref[...]
    pos1, pos2 = _sorted_positions(meta, srcv_ref[0])
    sel1 = _one_hot_rows(pos1).astype(BF16)
    sel2 = _one_hot_rows(pos2).astype(BF16)
    buf = sorted_ref.at[slot]
    _wait_rows(tot_ref[i], lambda rows: pltpu.make_async_copy(
        ys_ref.at[pl.ds(0, rows)], buf.at[pl.ds(0, rows)], sems.at[slot]))
    ysort = sorted_ref[slot].astype(BF16)
    f = (meta[:, META_W1:META_W1 + 1] * _dot(sel1, ysort)
         + meta[:, META_W2:META_W2 + 1] * _dot(sel2, ysort))
    y = _layer_norm(ALPHA * x1_ref[...] + g2_ref[0] * f, lng_ref[...], lnb_ref[...])
    if split:
        @pl.when(i < PROMPT_TILES)
        def _():
            outs[0][...] = y

        @pl.when(i >= PROMPT_TILES)
        def _():
            outs[1][...] = y
    else:
        outs[0][...] = y


def _combine(sched, ys, x1, meta, srcv, modr, ln_g, ln_b, li, split):
    if split:
        out_specs = [_prompt_tile_spec(D), _sample_tile_spec(D)]
        out_shape = [jax.ShapeDtypeStruct((T_PROMPT, D), F32), jax.ShapeDtypeStruct((T_SAMPLE, D), F32)]
    else:
        out_specs = [pl.BlockSpec((TM, D), lambda i, *_: (i, 0))]
        out_shape = [jax.ShapeDtypeStruct((T, D), F32)]
    return pl.pallas_call(
        functools.partial(_combine_kernel, split=split),
        grid_spec=pltpu.PrefetchScalarGridSpec(
            num_scalar_prefetch=2,
            grid=(N_TILES,),
            in_specs=[pl.BlockSpec(memory_space=pl.ANY),
                      pl.BlockSpec((TM, D), lambda i, *_: (i, 0)),
                      pl.BlockSpec((TM, LANES), lambda i, *_: (i, 0)),
                      pl.BlockSpec((1, 1, LANES), lambda i, *_: (i, 0, 0)),
                      _mod_spec(li, 5), _row_spec((1, D)), _row_spec((1, D))],
            out_specs=out_specs,
            scratch_shapes=[pltpu.VMEM((2, SORT_ROWS, D), F32), pltpu.SemaphoreType.DMA((2,))],
        ),
        out_shape=out_shape,
        compiler_params=_cparams(("arbitrary",)),
        name="moe_combine",
    )(*sched, ys, x1, meta, srcv, modr, ln_g, ln_b)


def _moe_schedule(tile_counts):
    n = (tile_counts + RUN_ALIGN - 1) // RUN_ALIGN * RUN_ALIGN
    src = jnp.cumsum(n, axis=1) - n
    per_expert = jnp.sum(n, axis=0)
    seg = (per_expert + MOE_TM - 1) // MOE_TM * MOE_TM
    seg_start = jnp.cumsum(seg) - seg
    dst = seg_start[None, :] + jnp.cumsum(n, axis=0) - n
    g_row = jnp.arange(SORT_GROUPS, dtype=jnp.int32) * RUN_ALIGN
    in_run = jnp.logical_and(src[:, None, :] <= g_row[None, :, None],
                             g_row[None, :, None] < (src + n)[:, None, :])
    dst_g = g_row[None, :] + jnp.sum(jnp.where(in_run, (dst - src)[:, None, :], 0), axis=2)
    runs = (jnp.sum(n, axis=1), dst_g.reshape(-1))
    srcv = jnp.pad(src.astype(F32), ((0, 0), (ROUTER_LANE0, LANES - ROUTER_LANE0 - N_EXPERTS)))
    return runs, srcv.reshape(N_TILES, 1, LANES), (seg_start, seg // MOE_TM)


def _moe(x1, u2, meta, cnt, modr, w_gate, w_up, w_down, ln_g, ln_b, li, split):
    tile_counts = cnt[:, 0, ROUTER_LANE0:ROUTER_LANE0 + N_EXPERTS].astype(jnp.int32)
    runs, srcv, (seg_start, seg_chunks) = _moe_schedule(tile_counts)
    xs = _dispatch(runs, u2, meta, srcv)
    ys = _experts(seg_start, seg_chunks, xs, w_gate, w_up, w_down, li)
    return _combine(runs, ys, x1, meta, srcv, modr, ln_g, ln_b, li, split)


def _router_slab(wg, bg, we, be):
    w = jnp.concatenate([wg, we.transpose(1, 0, 2).reshape(D, N_EXPERTS)], axis=1)
    b = jnp.concatenate([bg, be.reshape(N_EXPERTS)])
    pad = LANES - w.shape[1]
    return jnp.pad(w, ((0, 0), (0, pad))), jnp.pad(b, (0, pad)).reshape(1, LANES)


def kernel(x_prompt, x_sample, cache_diff_k, cache_diff_v, state_ret_fwd, state_ret_bwd, c, c_ctx, mod_w, mod_b, ln1_g, ln1_b, ln2_g, ln2_b, mix_w_in, mix_w_out, ret_decay_fwd, ret_decay_bwd, diff_lq1, diff_lk1, diff_lq2, diff_lk2, diff_subln_g, conv_w1, conv_b1, conv_dw, conv_dw_b, conv_ln_g, conv_ln_b, conv_w2, conv_b2, router_g_w, router_g_b, router_e_w, router_e_b, moe_w_gate, moe_w_up, moe_w_down):
    xp = x_prompt.reshape(T_PROMPT, D)
    xs = x_sample.reshape(T_SAMPLE, D)
    cond = jnp.concatenate([c_ctx[None, :], c, jnp.zeros((MOD_ROWS - 1 - DEC_BATCH, D), F32)], axis=0)
    modr = _mod_vectors(cond, mod_w, mod_b).reshape(DEPTH * MOD_ROWS * 6, 1, D)
    cos, sin_signed = _rope_tables()

    def row(v):
        return v.reshape(1, -1)

    x = None
    caches = None
    for li in range(DEPTH):
        wr, br = _router_slab(router_g_w[li], router_g_b[li], router_e_w[li], router_e_b[li])
        if li % 2 == 0:
            assert li == 0, "the even mixer reads the kernel inputs directly"
            e = li // 2
            lam_init = 0.8 - 0.6 * math.exp(-0.3 * li)
            proj, ck, cv = _in_proj(xp, xs, modr, mix_w_in[e], li)
            dec = jnp.concatenate([ret_decay_fwd[e], ret_decay_bwd[e]])
            r_p, sf, sb = _retention(proj, dec, BATCH, SEQ, 0, HEADS, emit_state=True)
            (r_s,) = _retention(proj, dec, DEC_BATCH, DEC_SEQ, T_PROMPT // DEC_SEQ, 2,
                                s0f=state_ret_fwd, s0b=state_ret_bwd, e=e)
            lams = (row(diff_lq1[e]), row(diff_lk1[e]), row(diff_lq2[e]), row(diff_lk2[e]),
                    row(diff_subln_g[e]))
            o_p = _attn_prompt(proj, *lams, lam_init)
            o_s = _attn_sample(proj, cache_diff_k, cache_diff_v, cos, sin_signed, *lams, lam_init, e)
            x1, u2, meta, cnt = _out_proj_tail(r_p, r_s, o_p, o_s, mix_w_out[e], xp, xs, modr,
                                               row(ln1_g[li]), row(ln1_b[li]), wr, br, li)
            caches = (ck, cv, sf, sb)
        else:
            o = li // 2
            glu = _conv_glu(x, modr, conv_w1[o], row(conv_b1[o]), li)
            x1, u2, meta, cnt = _conv_tail(glu, conv_dw[o], row(conv_dw_b[o]), row(conv_ln_g[o]),
                                           row(conv_ln_b[o]), conv_w2[o], row(conv_b2[o]),
                                           x, modr, row(ln1_g[li]), row(ln1_b[li]), wr, br, li)
        outs = _moe(x1, u2, meta, cnt, modr, moe_w_gate, moe_w_up, moe_w_down,
                    row(ln2_g[li]), row(ln2_b[li]), li, split=(li == DEPTH - 1))
        x = outs[0]

    y_prompt = outs[0].reshape(BATCH, SEQ, D)
    y_sample = outs[1].reshape(DEC_BATCH, DEC_SEQ, D)
    return (y_prompt, y_sample) + caches
```

```python
import functools
import math

import numpy as np
import jax
import jax.numpy as jnp
from jax import lax
from jax.experimental import pallas as pl
from jax.experimental.pallas import tpu as pltpu

F32 = jnp.float32
BF16 = jnp.bfloat16

D = 1024
BATCH = 16
SEQ = 256
DEPTH = 2
DEC_BATCH = 2
DEC_SEQ = 2048
PAST_LEN = 512
GRID_W = 64
HEADS = 4
HEAD_W = 128
RET_CHUNK = 128
DIFF_DK = 64
ROPE_THETA = 10000.0
IN_W = 7 * HEADS * HEAD_W
CONV_K = 31
CONV_PAD = CONV_K // 2
N_GROUPS = 4
EXPERTS_PER_GROUP = 8
N_EXPERTS = N_GROUPS * EXPERTS_PER_GROUP
D_EXPERT = 512
ALPHA = (2.0 * DEPTH) ** 0.25
LN_EPS = 1e-5
GN_EPS = 1e-6

T_PROMPT = BATCH * SEQ
T_SAMPLE = DEC_BATCH * DEC_SEQ
T = T_PROMPT + T_SAMPLE
TM = 256
N_TILES = T // TM
PROMPT_TILES = T_PROMPT // TM
SAMPLE_TILES_PER_SEQ = DEC_SEQ // TM
MOD_ROWS = 8
MOE_TM = 256
LANES = 128
SUBLANES = 8
RUN_ALIGN = SUBLANES
SORT_ROWS = -(-(2 * TM + N_EXPERTS * (RUN_ALIGN - 1)) // TM) * TM
RUN_BITS = tuple(1 << b for b in range((2 * TM).bit_length() - 1, RUN_ALIGN.bit_length() - 2, -1))
MOE_MAX_TILES = -(-(2 * T + N_TILES * N_EXPERTS * (RUN_ALIGN - 1) + N_EXPERTS * (MOE_TM - RUN_ALIGN)) // MOE_TM)
MOE_ROWS = MOE_MAX_TILES * MOE_TM
SORT_GROUPS = SORT_ROWS // RUN_ALIGN
MOE_AHEAD = 2
MOE_IN_SLOTS = MOE_AHEAD + 1
MOE_OUT_SLOTS = 2
ROUTER_LANE0 = N_GROUPS
VMEM_LIMIT = 52 * 1024 * 1024


def _cparams(sem):
    return pltpu.CompilerParams(dimension_semantics=sem, vmem_limit_bytes=VMEM_LIMIT)


def _tile_cond_row(i):
    return jnp.where(i < PROMPT_TILES, 0, 1 + (i - PROMPT_TILES) // SAMPLE_TILES_PER_SEQ)


def _mod_spec(li, k):
    return pl.BlockSpec((1, 1, D), lambda i, *_: ((li * MOD_ROWS + _tile_cond_row(i)) * 6 + k, 0, 0))


def _row_spec(shape):
    return pl.BlockSpec(shape, lambda i, *_: (0,) * len(shape))


def _resident_f32_weight(shape):
    return pl.BlockSpec(shape, lambda i, *_: (0,) * len(shape), pipeline_mode=pl.Buffered(1))


def _cast_weight_once(w_ref, w_bf):
    @pl.when(pl.program_id(0) == 0)
    def _():
        w_bf[...] = w_ref[...].astype(BF16)


def _layer_norm(x, g, b):
    mu = jnp.mean(x, axis=-1, keepdims=True)
    xc = x - mu
    var = jnp.mean(xc * xc, axis=-1, keepdims=True)
    return xc * lax.rsqrt(var + LN_EPS) * g + b


def _silu(x):
    return x * jax.nn.sigmoid(x)


def _dot(a, b):
    return jnp.dot(a, b, preferred_element_type=F32)


def _dot_nt(a, b):
    return lax.dot_general(a, b, (((1,), (1,)), ((), ())), preferred_element_type=F32)


def _dot_tn(a, b):
    return lax.dot_general(a, b, (((0,), (0,)), ((), ())), preferred_element_type=F32)


MOD_TN = 1024
MOD_USED_ROWS = 1 + DEC_BATCH


def _mod_kernel(cond_t_ref, w_ref, b_ref, o_ref):
    s = _silu(cond_t_ref[...])
    w = w_ref[0]
    o_ref[0] = jnp.zeros((MOD_ROWS, MOD_TN), F32) + b_ref[0]
    for r in range(MOD_USED_ROWS):
        o_ref[0, r:r + 1, :] = jnp.sum(w * s[:, r:r + 1], axis=0, keepdims=True) + b_ref[0]


def _mod_vectors(cond, mod_w, mod_b):
    return pl.pallas_call(
        _mod_kernel,
        grid=(DEPTH, 6 * D // MOD_TN),
        in_specs=[
            pl.BlockSpec((D, MOD_ROWS), lambda l, j: (0, 0)),
            pl.BlockSpec((1, D, MOD_TN), lambda l, j: (l, 0, j)),
            pl.BlockSpec((1, 1, MOD_TN), lambda l, j: (l, 0, j)),
        ],
        out_specs=pl.BlockSpec((1, MOD_ROWS, MOD_TN), lambda l, j: (l, 0, j)),
        out_shape=jax.ShapeDtypeStruct((DEPTH, MOD_ROWS, 6 * D), F32),
        compiler_params=_cparams(("arbitrary", "arbitrary")),
        name="mod_vectors",
    )(cond.T, mod_w, mod_b.reshape(DEPTH, 1, 6 * D))


def _prompt_tile_spec(width):
    return pl.BlockSpec((TM, width), lambda i, *_: (jnp.minimum(i, PROMPT_TILES - 1), 0))


def _sample_tile_spec(width):
    return pl.BlockSpec((TM, width), lambda i, *_: (jnp.maximum(i - PROMPT_TILES, 0), 0))


def _pick_tile(prompt_ref, sample_ref):
    return jnp.where(pl.program_id(0) < PROMPT_TILES, prompt_ref[...], sample_ref[...])


def _in_proj_kernel(xp_ref, xs_ref, sh_ref, sc_ref, w_ref, o_ref, ck_ref, cv_ref, w_bf):
    _cast_weight_once(w_ref, w_bf)
    u = _pick_tile(xp_ref, xs_ref) * (1.0 + sc_ref[0]) + sh_ref[0]
    proj = _dot(u.astype(BF16), w_bf[...])
    o_ref[...] = proj

    @pl.when(pl.program_id(0) < PROMPT_TILES)
    def _():
        for h in range(HEADS):
            ck_ref[0, 0, h] = proj[:, (COL_KD + h) * HEAD_W:(COL_KD + h + 1) * HEAD_W]
            cv_ref[0, 0, h] = proj[:, (COL_VD + h) * HEAD_W:(COL_VD + h + 1) * HEAD_W]


def _in_proj(x_prompt, x_sample, modr, w_in, li):
    cache_spec = pl.BlockSpec((1, 1, HEADS, SEQ, HEAD_W),
                              lambda i: (jnp.minimum(i, PROMPT_TILES - 1), 0, 0, 0, 0))
    cache_shape = jax.ShapeDtypeStruct((BATCH, 1, HEADS, SEQ, HEAD_W), F32)
    return pl.pallas_call(
        _in_proj_kernel,
        grid=(N_TILES,),
        in_specs=[
            _prompt_tile_spec(D), _sample_tile_spec(D),
            _mod_spec(li, 0),
            _mod_spec(li, 1),
            _resident_f32_weight((D, IN_W)),
        ],
        out_specs=[pl.BlockSpec((TM, IN_W), lambda i: (i, 0)), cache_spec, cache_spec],
        out_shape=[jax.ShapeDtypeStruct((T, IN_W), F32), cache_shape, cache_shape],
        scratch_shapes=[pltpu.VMEM((D, IN_W), BF16)],
        compiler_params=_cparams(("arbitrary",)),
        name="in_proj",
    )(x_prompt, x_sample, modr, modr, w_in)


COL_QR, COL_KR, COL_VR, COL_GR, COL_QD, COL_KD, COL_VD = (k * HEADS for k in range(7))


def _retention_kernel(dec_ref, q_ref, k_ref, v_ref, g_ref, *rest, n_chunks, n_heads, has_state, emit_state):
    rest = list(rest)
    if has_state:
        s0f_ref, s0b_ref = rest[:2]
        rest = rest[2:]
    r_ref = rest[0]
    rest = rest[1:]
    if emit_state:
        sf_ref, sb_ref = rest[:2]
        rest = rest[2:]
    of_ref = rest[0]

    head0 = pl.program_id(1) * n_heads
    C = RET_CHUNK
    ii = lax.broadcasted_iota(jnp.int32, (C, C), 0)
    jj = lax.broadcasted_iota(jnp.int32, (C, C), 1)
    rel = (ii - jj).astype(F32)
    idx = lax.broadcasted_iota(jnp.int32, (C, 1), 0).astype(F32)
    k_scale = HEAD_W ** -0.5

    def chunk(ref, c, h):
        return ref[c * C:(c + 1) * C, h * HEAD_W:(h + 1) * HEAD_W]

    def decays(direction, h):
        lg = -jnp.exp(jnp.full((1, 1), dec_ref[direction * HEADS + head0 + h], F32))
        if direction == 0:
            inner = jnp.where(rel >= 0, jnp.exp(jnp.maximum(rel, 0.0) * lg), 0.0)
            return inner, jnp.exp((idx + 1.0) * lg), jnp.exp((C - 1.0 - idx) * lg), jnp.exp(C * lg)
        inner = jnp.where(rel <= 0, jnp.exp(jnp.maximum(-rel, 0.0) * lg), 0.0)
        return inner, jnp.exp((C - idx) * lg), jnp.exp(idx * lg), jnp.exp(C * lg)

    def run(direction):
        dec = [decays(direction, h) for h in range(n_heads)]
        if has_state:
            s0_ref = s0f_ref if direction == 0 else s0b_ref
            states = [s0_ref[0, 0, h] for h in range(n_heads)]
        else:
            states = [jnp.zeros((HEAD_W, HEAD_W), F32) for _ in range(n_heads)]
        order = range(n_chunks) if direction == 0 else range(n_chunks - 1, -1, -1)
        for c in order:
            rows = slice(c * C, (c + 1) * C)
            for h in range(n_heads):
                inner, q_decay, k_decay, chunk_decay = dec[h]
                cols = slice(h * HEAD_W, (h + 1) * HEAD_W)
                s = states[h]
                qc = chunk(q_ref, c, h)
                kc = chunk(k_ref, c, h) * k_scale
                vc = chunk(v_ref, c, h).astype(BF16)
                scores = _dot_nt(qc.astype(BF16), kc.astype(BF16)) * inner
                o = _dot(scores.astype(BF16), vc) + _dot((qc * q_decay).astype(BF16), s.astype(BF16))
                states[h] = s * chunk_decay + _dot_tn((kc * k_decay).astype(BF16), vc)
                if direction == 0:
                    of_ref[rows, cols] = o
                else:
                    r = of_ref[rows, cols] + o
                    mu = jnp.mean(r, axis=-1, keepdims=True)
                    rc = r - mu
                    var = jnp.mean(rc * rc, axis=-1, keepdims=True)
                    rn = rc * lax.rsqrt(var + GN_EPS)
                    r_ref[rows, cols] = _silu(chunk(g_ref, c, h)) * rn
        return states

    sf = run(0)
    sb = run(1)
    if emit_state:
        for h in range(n_heads):
            sf_ref[0, 0, h] = sf[h]
            sb_ref[0, 0, h] = sb[h]


def _retention(proj, dec, n_seq, seq_len, row_block0, n_heads, s0f=None, s0b=None, e=0, emit_state=False):
    has_state = s0f is not None
    width = n_heads * HEAD_W

    def col(base):
        return pl.BlockSpec((seq_len, width), lambda b, h, *_: (row_block0 + b, base // n_heads + h))

    state_spec = pl.BlockSpec((1, 1, n_heads, HEAD_W, HEAD_W), lambda b, h, *_: (b, e, h, 0, 0))
    in_specs = [pl.BlockSpec(memory_space=pltpu.SMEM), col(COL_QR), col(COL_KR), col(COL_VR), col(COL_GR)]
    args = [dec, proj, proj, proj, proj]
    if has_state:
        in_specs += [state_spec, state_spec]
        args += [s0f, s0b]
    out_specs = [pl.BlockSpec((seq_len, width), lambda b, h, *_: (b, h))]
    out_shape = [jax.ShapeDtypeStruct((n_seq * seq_len, HEADS * HEAD_W), F32)]
    if emit_state:
        st = pl.BlockSpec((1, 1, n_heads, HEAD_W, HEAD_W), lambda b, h, *_: (b, 0, h, 0, 0))
        out_specs += [st, st]
        out_shape += [jax.ShapeDtypeStruct((n_seq, 1, HEADS, HEAD_W, HEAD_W), F32)] * 2
    return pl.pallas_call(
        functools.partial(_retention_kernel, n_chunks=seq_len // RET_CHUNK, n_heads=n_heads,
                          has_state=has_state, emit_state=emit_state),
        grid=(n_seq, HEADS // n_heads),
        in_specs=in_specs,
        out_specs=out_specs,
        out_shape=out_shape,
        scratch_shapes=[pltpu.VMEM((seq_len, width), F32)],
        compiler_params=_cparams(("arbitrary", "arbitrary")),
        name=f"retention_{seq_len}",
    )(*args)


def _diff_lambda(lq1_ref, lk1_ref, lq2_ref, lk2_ref, lam_init):
    a = jnp.sum(lq1_ref[...] * lk1_ref[...], axis=-1, keepdims=True)
    b = jnp.sum(lq2_ref[...] * lk2_ref[...], axis=-1, keepdims=True)
    return jnp.exp(a) - jnp.exp(b) + lam_init


LOG2E = 1.4426950408889634


def _diff_attend(q, k, v, lam, subln_g, lam_init):
    lane = lax.broadcasted_iota(jnp.int32, q.shape, 1)
    q1 = jnp.where(lane < DIFF_DK, q, 0.0).astype(BF16)
    q2 = jnp.where(lane >= DIFF_DK, q, 0.0).astype(BF16)

    def softmax_times_v(qz):
        s = _dot_nt(qz, k)
        p = jnp.exp2(s - jnp.max(s, axis=-1, keepdims=True))
        return _dot(p.astype(BF16), v) * (1.0 / jnp.sum(p, axis=-1, keepdims=True))

    o = softmax_times_v(q1) - lam * softmax_times_v(q2)
    o = o * lax.rsqrt(jnp.mean(o * o, axis=-1, keepdims=True) + LN_EPS)
    return o * subln_g * (1.0 - lam_init)


def _attn_prompt_kernel(q_ref, k_ref, v_ref, lq1, lk1, lq2, lk2, g_ref, o_ref, *, lam_init):
    lam = _diff_lambda(lq1, lk1, lq2, lk2, lam_init)
    scale = DIFF_DK ** -0.5 * LOG2E
    for h in range(HEADS):
        sl = slice(h * HEAD_W, (h + 1) * HEAD_W)
        o_ref[:, sl] = _diff_attend(q_ref[:, sl] * scale, k_ref[:, sl].astype(BF16),
                                    v_ref[:, sl].astype(BF16), lam, g_ref[...], lam_init)


def _attn_prompt(proj, lq1, lk1, lq2, lk2, subln_g, lam_init):
    W = HEADS * HEAD_W

    def slab(base):
        return pl.BlockSpec((SEQ, W), lambda b: (b, base // HEADS))

    small = _row_spec((1, DIFF_DK))
    return pl.pallas_call(
        functools.partial(_attn_prompt_kernel, lam_init=lam_init),
        grid=(BATCH,),
        in_specs=[slab(COL_QD), slab(COL_KD), slab(COL_VD), small, small, small, small,
                  _row_spec((1, HEAD_W))],
        out_specs=pl.BlockSpec((SEQ, W), lambda b: (b, 0)),
        out_shape=jax.ShapeDtypeStruct((T_PROMPT, W), F32),
        compiler_params=_cparams(("arbitrary",)),
        name="diff_attn_prompt",
    )(proj, proj, proj, lq1, lk1, lq2, lk2, subln_g)


def _rope(x, cos, sin_signed):
    lane = lax.broadcasted_iota(jnp.int32, x.shape, 1)
    partner = jnp.where((lane % 32) < 16, pltpu.roll(x, LANES - 16, 1), pltpu.roll(x, 16, 1))
    return x * cos + partner * sin_signed


def _attn_sample_kernel(q_ref, k_ref, v_ref, ck_ref, cv_ref, cosq_ref, sinq_ref, cos_ref, sin_ref,
                        lq1, lk1, lq2, lk2, g_ref, o_ref, kbuf, vbuf, *, lam_init):
    @pl.when(pl.program_id(2) == 0)
    def _():
        kbuf[0:DEC_SEQ, :] = _rope(k_ref[...], cos_ref[...], sin_ref[...]).astype(BF16)
        kbuf[DEC_SEQ:, :] = ck_ref[0, 0, 0].astype(BF16)
        vbuf[0:DEC_SEQ, :] = v_ref[...].astype(BF16)
        vbuf[DEC_SEQ:, :] = cv_ref[0, 0, 0].astype(BF16)

    lam = _diff_lambda(lq1, lk1, lq2, lk2, lam_init)
    q = _rope(q_ref[...], cosq_ref[...], sinq_ref[...]) * (DIFF_DK ** -0.5 * LOG2E)
    o_ref[...] = _diff_attend(q, kbuf[...], vbuf[...], lam, g_ref[...], lam_init)


ATTN_TQ = 256


def _attn_sample(proj, cache_k, cache_v, cos, sin_signed, lq1, lk1, lq2, lk2, subln_g, lam_init, e):
    nq = DEC_SEQ // ATTN_TQ
    row0_q = T_PROMPT // ATTN_TQ
    row0_kv = T_PROMPT // DEC_SEQ
    small = pl.BlockSpec((1, DIFF_DK), lambda b, h, t: (0, 0))
    cache = pl.BlockSpec((1, 1, 1, PAST_LEN, HEAD_W), lambda b, h, t: (b, e, h, 0, 0))
    table_q = pl.BlockSpec((ATTN_TQ, HEAD_W), lambda b, h, t: (t, 0))
    table = pl.BlockSpec((DEC_SEQ, HEAD_W), lambda b, h, t: (0, 0))
    return pl.pallas_call(
        functools.partial(_attn_sample_kernel, lam_init=lam_init),
        grid=(DEC_BATCH, HEADS, nq),
        in_specs=[
            pl.BlockSpec((ATTN_TQ, HEAD_W), lambda b, h, t: (row0_q + b * nq + t, COL_QD + h)),
            pl.BlockSpec((DEC_SEQ, HEAD_W), lambda b, h, t: (row0_kv + b, COL_KD + h)),
            pl.BlockSpec((DEC_SEQ, HEAD_W), lambda b, h, t: (row0_kv + b, COL_VD + h)),
            cache, cache, table_q, table_q, table, table,
            small, small, small, small,
            pl.BlockSpec((1, HEAD_W), lambda b, h, t: (0, 0)),
        ],
        out_specs=pl.BlockSpec((ATTN_TQ, HEAD_W), lambda b, h, t: (b * nq + t, h)),
        out_shape=jax.ShapeDtypeStruct((T_SAMPLE, HEADS * HEAD_W), F32),
        scratch_shapes=[pltpu.VMEM((DEC_SEQ + PAST_LEN, HEAD_W), BF16),
                        pltpu.VMEM((DEC_SEQ + PAST_LEN, HEAD_W), BF16)],
        compiler_params=_cparams(("arbitrary", "arbitrary", "arbitrary")),
        name="diff_attn_sample",
    )(proj, proj, proj, cache_k, cache_v, cos, sin_signed, cos, sin_signed,
      lq1, lk1, lq2, lk2, subln_g)


def _rope_tables():
    t = np.arange(DEC_SEQ)
    row, colp = t // GRID_W, t % GRID_W
    lane = np.arange(LANES)
    pos = np.where(((lane // 32) % 2 == 0)[None, :], row[:, None], colp[:, None]).astype(np.float64)
    half = 16
    inv = (np.float32(ROPE_THETA) ** (-(np.arange(half, dtype=np.float32)) / np.float32(half))).astype(np.float32)
    ang = pos.astype(np.float32) * inv[lane % half][None, :]
    cos = np.cos(ang.astype(np.float64)).astype(np.float32)
    sin = np.sin(ang.astype(np.float64)).astype(np.float32)
    sign = np.where((lane % 32) < half, -1.0, 1.0).astype(np.float32)[None, :]
    return jnp.asarray(cos), jnp.asarray(sin * sign)


def _split_bf16(a):
    hi = a.astype(BF16)
    return hi, (a - hi.astype(F32)).astype(BF16)


def _mixer_tail(out, x, g1_ref, sc2_ref, sh2_ref, lng_ref, lnb_ref, wr_ref, br_ref,
                x1_ref, u2_ref, meta_ref, cnt_ref):
    x1 = _layer_norm(ALPHA * x + g1_ref[0] * out, lng_ref[...], lnb_ref[...])
    x1_ref[...] = x1
    u2 = x1 * (1.0 + sc2_ref[0]) + sh2_ref[0]
    u2_ref[...] = u2.astype(BF16)

    u_hi, u_lo = _split_bf16(u2)
    w_hi, w_lo = _split_bf16(wr_ref[...])
    logits = _dot(u_hi, w_hi) + (_dot(u_hi, w_lo) + _dot(u_lo, w_hi)) + br_ref[...]
    lane = lax.broadcasted_iota(jnp.int32, logits.shape, 1).astype(F32)
    neg = jnp.float32(-jnp.inf)
    is_g = lane < N_GROUPS
    gl = jnp.where(is_g, logits, neg)
    gmax = jnp.max(gl, axis=-1, keepdims=True)
    gsel = jnp.min(jnp.where(gl == gmax, lane, float(LANES)), axis=-1, keepdims=True)
    p_g = 1.0 / jnp.sum(jnp.where(is_g, jnp.exp(gl - gmax), 0.0), axis=-1, keepdims=True)
    lo = ROUTER_LANE0 + gsel * EXPERTS_PER_GROUP
    el = jnp.where((lane >= lo) & (lane < lo + EXPERTS_PER_GROUP), logits, neg)
    v1 = jnp.max(el, axis=-1, keepdims=True)
    i1 = jnp.min(jnp.where(el == v1, lane, float(LANES)), axis=-1, keepdims=True)
    el2 = jnp.where(lane == i1, neg, el)
    v2 = jnp.max(el2, axis=-1, keepdims=True)
    i2 = jnp.min(jnp.where(el2 == v2, lane, float(LANES)), axis=-1, keepdims=True)
    t = jnp.exp(v2 - v1)
    w1 = p_g / (1.0 + t)
    w2 = w1 * t

    oh1 = (lane == i1).astype(F32)
    oh2 = (lane == i2).astype(F32)
    oh = oh1 + oh2
    r_i = lax.broadcasted_iota(jnp.int32, (TM, TM), 0)
    c_i = lax.broadcasted_iota(jnp.int32, (TM, TM), 1)
    before = (c_i < r_i).astype(BF16)
    earlier = _dot(before, oh.astype(BF16))
    rank1 = jnp.sum(earlier * oh1, axis=-1, keepdims=True)
    rank2 = jnp.sum(earlier * oh2, axis=-1, keepdims=True)
    cnt_ref[0] = jnp.sum(oh, axis=0, keepdims=True)
    cols = (i1, i2, w1, w2, rank1, rank2)
    meta = jnp.zeros_like(logits)
    for k, col in enumerate(cols):
        meta = jnp.where(lane == k, col, meta)
    meta_ref[...] = meta


META_E1, META_E2, META_W1, META_W2, META_RANK1, META_RANK2 = range(6)

_TAIL_OUT_SHAPES = [
    jax.ShapeDtypeStruct((T, D), F32),
    jax.ShapeDtypeStruct((T, D), BF16),
    jax.ShapeDtypeStruct((T, LANES), F32),
    jax.ShapeDtypeStruct((N_TILES, 1, LANES), F32),
]


def _tail_out_specs():
    return [
        pl.BlockSpec((TM, D), lambda i: (i, 0)),
        pl.BlockSpec((TM, D), lambda i: (i, 0)),
        pl.BlockSpec((TM, LANES), lambda i: (i, 0)),
        pl.BlockSpec((1, 1, LANES), lambda i: (i, 0, 0)),
    ]


def _tail_in_specs(li):
    return [
        _mod_spec(li, 2), _mod_spec(li, 4), _mod_spec(li, 3),
        _row_spec((1, D)), _row_spec((1, D)),
        _row_spec((D, LANES)), _row_spec((1, LANES)),
    ]


def _out_proj_kernel(rp_ref, rs_ref, op_ref, os_ref, w_ref, xp_ref, xs_ref, *rest):
    tail_args, w_bf = rest[:-1], rest[-1]
    _cast_weight_once(w_ref, w_bf)
    half = HEADS * HEAD_W
    r = _pick_tile(rp_ref, rs_ref).astype(BF16)
    o = _pick_tile(op_ref, os_ref).astype(BF16)
    out = _dot(r, w_bf[0:half, :]) + _dot(o, w_bf[half:, :])
    _mixer_tail(out, _pick_tile(xp_ref, xs_ref), *tail_args)


def _out_proj_tail(r_p, r_s, o_p, o_s, w_out, x_prompt, x_sample, modr, ln_g, ln_b, wr, br, li):
    half = HEADS * HEAD_W
    return pl.pallas_call(
        _out_proj_kernel,
        grid=(N_TILES,),
        in_specs=[_prompt_tile_spec(half), _sample_tile_spec(half),
                  _prompt_tile_spec(half), _sample_tile_spec(half),
                  _resident_f32_weight((2 * half, D)),
                  _prompt_tile_spec(D), _sample_tile_spec(D)] + _tail_in_specs(li),
        out_specs=_tail_out_specs(),
        out_shape=_TAIL_OUT_SHAPES,
        scratch_shapes=[pltpu.VMEM((2 * half, D), BF16)],
        compiler_params=_cparams(("arbitrary",)),
        name="out_proj_tail",
    )(r_p, r_s, o_p, o_s, w_out, x_prompt, x_sample, modr, modr, modr, ln_g, ln_b, wr, br)


def _conv_glu(y, sh_ref, sc_ref, w_bf, b_ref):
    u = y * (1.0 + sc_ref[0]) + sh_ref[0]
    h = _dot(u.astype(BF16), w_bf[...]) + b_ref[...]
    return h[:, :D] * jax.nn.sigmoid(h[:, D:])


HALO = 16
CONV_ROWS = 64
CONV_COLS = 128


def _depthwise_conv(hp, dw_ref, conv):
    base = HALO - CONV_PAD
    for cb in range(D // CONV_COLS):
        cs = slice(cb * CONV_COLS, (cb + 1) * CONV_COLS)
        for rb in range(TM // CONV_ROWS):
            r0 = rb * CONV_ROWS
            acc = None
            for shift in range(SUBLANES):
                part = None
                for tap in range(CONV_K):
                    off = base + tap
                    if off % SUBLANES != shift:
                        continue
                    a0 = r0 + off - shift
                    term = hp[a0:a0 + CONV_ROWS + SUBLANES, cs] * dw_ref[tap:tap + 1, cs]
                    part = term if part is None else part + term
                part = part[shift:shift + CONV_ROWS, :]
                acc = part if acc is None else acc + part
            conv[r0:r0 + CONV_ROWS, cs] = acc


def _conv_tail_kernel(cur_ref, prev_ref, next_ref, dw_ref, dwb_ref, cg_ref, cb_ref, w2_ref, b2_ref,
                      x_ref, *rest):
    tail_args, (hp, conv, w2_bf) = rest[:-3], rest[-3:]
    _cast_weight_once(w2_ref, w2_bf)
    i = pl.program_id(0)
    k = (i - PROMPT_TILES) % SAMPLE_TILES_PER_SEQ
    in_sample = i >= PROMPT_TILES
    left_ok = jnp.logical_and(in_sample, k != 0)
    right_ok = jnp.logical_and(in_sample, k != SAMPLE_TILES_PER_SEQ - 1)
    hp[0:HALO, :] = jnp.where(left_ok, prev_ref[...], 0.0)
    hp[HALO:HALO + TM, :] = cur_ref[...]
    hp[HALO + TM:HALO + TM + HALO, :] = jnp.where(right_ok, next_ref[...], 0.0)
    _depthwise_conv(hp, dw_ref, conv)
    hc = _silu(_layer_norm(conv[...] + dwb_ref[...], cg_ref[...], cb_ref[...]))
    out = _dot(hc.astype(BF16), w2_bf[...]) + b2_ref[...]
    _mixer_tail(out, x_ref[...], *tail_args)


def _conv_tail(glu, dw, dwb, cg, cb, w2, b2, x, modr, ln_g, ln_b, wr, br, li):
    per = TM // HALO
    last = T // HALO - 1
    return pl.pallas_call(
        _conv_tail_kernel,
        grid=(N_TILES,),
        in_specs=[pl.BlockSpec((TM, D), lambda i: (i, 0)),
                  pl.BlockSpec((HALO, D), lambda i: (jnp.maximum(i * per - 1, 0), 0)),
                  pl.BlockSpec((HALO, D), lambda i: (jnp.minimum((i + 1) * per, last), 0)),
                  _row_spec((CONV_K, D)), _row_spec((1, D)), _row_spec((1, D)), _row_spec((1, D)),
                  _resident_f32_weight((D, D)), _row_spec((1, D)),
                  pl.BlockSpec((TM, D), lambda i: (i, 0))] + _tail_in_specs(li),
        out_specs=_tail_out_specs(),
        out_shape=_TAIL_OUT_SHAPES,
        scratch_shapes=[pltpu.VMEM((TM + 2 * HALO, D), F32), pltpu.VMEM((TM, D), F32),
                        pltpu.VMEM((D, D), BF16)],
        compiler_params=_cparams(("arbitrary",)),
        name="conv_tail",
    )(glu, glu, glu, dw, dwb, cg, cb, w2, b2, x, modr, modr, modr, ln_g, ln_b, wr, br)


def _sorted_positions(meta, srcv):
    lane = lax.broadcasted_iota(jnp.int32, meta.shape, 1).astype(F32)

    def pos(e_col, r_col):
        start = jnp.sum(jnp.where(lane == meta[:, e_col:e_col + 1], srcv, 0.0), axis=-1, keepdims=True)
        return start + meta[:, r_col:r_col + 1]

    return pos(META_E1, META_RANK1), pos(META_E2, META_RANK2)


def _one_hot_rows(pos):
    col = lax.broadcasted_iota(jnp.int32, (TM, SORT_ROWS), 1).astype(F32)
    return col == pos


def _for_each_row_group(tile, tot_ref, dstg_ref, fn):
    def body(g, carry):
        fn(pl.multiple_of(g * RUN_ALIGN, RUN_ALIGN),
           pl.multiple_of(dstg_ref[tile * SORT_GROUPS + g], RUN_ALIGN))
        return carry

    lax.fori_loop(0, tot_ref[tile] // RUN_ALIGN, body, 0)


def _wait_rows(total, make_copy):
    for bit in RUN_BITS:
        @pl.when((total & bit) != 0)
        def _(bit=bit):
            make_copy(bit).wait()


def _dispatch_kernel(tot_ref, dstg_ref, u_ref, meta_ref, srcv_ref, xs_ref, sorted_ref, sems):
    i = pl.program_id(0)
    slot = i % 2

    def wait_tile(tile, slot):
        buf = sorted_ref.at[slot]
        _wait_rows(tot_ref[tile], lambda rows: pltpu.make_async_copy(
            buf.at[pl.ds(0, rows)], xs_ref.at[pl.ds(0, rows)], sems.at[slot]))

    @pl.when(i >= 2)
    def _():
        wait_tile(i - 2, slot)

    pos1, pos2 = _sorted_positions(meta_ref[...], srcv_ref[0])
    select = jnp.logical_or(_one_hot_rows(pos1), _one_hot_rows(pos2)).astype(BF16)
    sorted_ref[slot] = _dot_tn(select, u_ref[...])
    buf = sorted_ref.at[slot]

    def start(src, dst):
        pltpu.make_async_copy(buf.at[pl.ds(src, RUN_ALIGN)], xs_ref.at[pl.ds(dst, RUN_ALIGN)],
                              sems.at[slot]).start()

    _for_each_row_group(i, tot_ref, dstg_ref, start)

    @pl.when(i == N_TILES - 1)
    def _():
        wait_tile(i - 1, 1 - slot)
        wait_tile(i, slot)


def _dispatch(sched, u2, meta, srcv):
    return pl.pallas_call(
        _dispatch_kernel,
        grid_spec=pltpu.PrefetchScalarGridSpec(
            num_scalar_prefetch=2,
            grid=(N_TILES,),
            in_specs=[pl.BlockSpec((TM, D), lambda i, *_: (i, 0)),
                      pl.BlockSpec((TM, LANES), lambda i, *_: (i, 0)),
                      pl.BlockSpec((1, 1, LANES), lambda i, *_: (i, 0, 0))],
            out_specs=pl.BlockSpec(memory_space=pl.ANY),
            scratch_shapes=[pltpu.VMEM((2, SORT_ROWS, D), F32), pltpu.SemaphoreType.DMA((2,))],
        ),
        out_shape=jax.ShapeDtypeStruct((MOE_ROWS, D), F32),
        compiler_params=_cparams(("arbitrary",)),
        name="moe_dispatch",
    )(*sched, u2, meta, srcv)


def _experts_kernel(start_ref, chunks_ref, xs_ref, wg_ref, wu_ref, wd_ref, ys_ref,
                    wg_bf, wu_bf, wd_bf, xbuf, ybuf, in_sems, out_sems):
    e = pl.program_id(0)
    n = chunks_ref[e]
    first = start_ref[e] // MOE_TM
    total = start_ref[N_EXPERTS - 1] // MOE_TM + chunks_ref[N_EXPERTS - 1]

    def rows(g):
        return pl.ds(pl.multiple_of(g * MOE_TM, MOE_TM), MOE_TM)

    def load(g):
        slot = g % MOE_IN_SLOTS
        return pltpu.make_async_copy(xs_ref.at[rows(g)], xbuf.at[slot], in_sems.at[slot])

    def store(g):
        slot = g % MOE_OUT_SLOTS
        return pltpu.make_async_copy(ybuf.at[slot], ys_ref.at[rows(g)], out_sems.at[slot])

    @pl.when(e == 0)
    def _():
        for g in range(MOE_AHEAD):
            @pl.when(g < total)
            def _(g=g):
                load(g).start()

    @pl.when(n > 0)
    def _():
        wg_bf[...] = wg_ref[0, 0].astype(BF16)
        wu_bf[...] = wu_ref[0, 0].astype(BF16)
        wd_bf[...] = wd_ref[0, 0].astype(BF16)

        def tile(g, carry):
            load(g).wait()

            @pl.when(g + MOE_AHEAD < total)
            def _():
                load(g + MOE_AHEAD).start()

            @pl.when(g >= MOE_OUT_SLOTS)
            def _():
                store(g - MOE_OUT_SLOTS).wait()

            x = xbuf[g % MOE_IN_SLOTS].astype(BF16)
            h = (_silu(_dot(x, wg_bf[...])) * _dot(x, wu_bf[...])).astype(BF16)
            ybuf[g % MOE_OUT_SLOTS] = _dot(h, wd_bf[...])
            store(g).start()
            return carry

        lax.fori_loop(first, first + n, tile, 0)

    @pl.when(e == N_EXPERTS - 1)
    def _():
        for back in range(MOE_OUT_SLOTS, 0, -1):
            @pl.when(total >= back)
            def _(back=back):
                store(total - back).wait()


def _experts(seg_start, seg_chunks, xs, w_gate, w_up, w_down, li):
    def weight(shape):
        return pl.BlockSpec((1, 1) + shape, lambda e, *_: (li, e, 0, 0))

    return pl.pallas_call(
        _experts_kernel,
        grid_spec=pltpu.PrefetchScalarGridSpec(
            num_scalar_prefetch=2,
            grid=(N_EXPERTS,),
            in_specs=[pl.BlockSpec(memory_space=pl.ANY),
                      weight((D, D_EXPERT)), weight((D, D_EXPERT)), weight((D_EXPERT, D))],
            out_specs=pl.BlockSpec(memory_space=pl.ANY),
            scratch_shapes=[pltpu.VMEM((D, D_EXPERT), BF16), pltpu.VMEM((D, D_EXPERT), BF16),
                            pltpu.VMEM((D_EXPERT, D), BF16),
                            pltpu.VMEM((MOE_IN_SLOTS, MOE_TM, D), F32),
                            pltpu.VMEM((MOE_OUT_SLOTS, MOE_TM, D), F32),
                            pltpu.SemaphoreType.DMA((MOE_IN_SLOTS,)),
                            pltpu.SemaphoreType.DMA((MOE_OUT_SLOTS,))],
        ),
        out_shape=jax.ShapeDtypeStruct((MOE_ROWS, D), F32),
        compiler_params=_cparams(("arbitrary",)),
        name="moe_experts",
    )(seg_start, seg_chunks, xs, w_gate, w_up, w_down)


def _combine_kernel(tot_ref, dstg_ref, ys_ref, x1_ref, meta_ref, srcv_ref, g2_ref,
                    lng_ref, lnb_ref, *rest, split, feeds_conv):
    if feeds_conv:
        (sh_ref, sc_ref, w1_ref, b1_ref), rest, w1_bf = rest[:4], rest[4:-1], rest[-1]
        _cast_weight_once(w1_ref, w1_bf)
    outs, (sorted_ref, sems) = rest[:-2], rest[-2:]
    i = pl.program_id(0)
    slot = i % 2

    def fetch(tile, slot):
        buf = sorted_ref.at[slot]

        def start(src, dst):
            pltpu.make_async_copy(ys_ref.at[pl.ds(dst, RUN_ALIGN)], buf.at[pl.ds(src, RUN_ALIGN)],
                                  sems.at[slot]).start()

        _for_each_row_group(tile, tot_ref, dstg_ref, start)

    @pl.when(i == 0)
    def _():
        sorted_ref[...] = jnp.zeros_like(sorted_ref)
        fetch(0, 0)

    @pl.when(i + 1 < N_TILES)
    def _():
        fetch(i + 1, 1 - slot)

    meta = meta_ref[...]
    pos1, pos2 = _sorted_positions(meta, srcv_ref[0])
    sel1 = _one_hot_rows(pos1).astype(BF16)
    sel2 = _one_hot_rows(pos2).astype(BF16)
    buf = sorted_ref.at[slot]
    _wait_rows(tot_ref[i], lambda rows: pltpu.make_async_copy(
        ys_ref.at[pl.ds(0, rows)], buf.at[pl.ds(0, rows)], sems.at[slot]))
    ysort = sorted_ref[slot].astype(BF16)
    f = (meta[:, META_W1:META_W1 + 1] * _dot(sel1, ysort)
         + meta[:, META_W2:META_W2 + 1] * _dot(sel2, ysort))
    y = _layer_norm(ALPHA * x1_ref[...] + g2_ref[0] * f, lng_ref[...], lnb_ref[...])
    if split:
        @pl.when(i < PROMPT_TILES)
        def _():
            outs[0][...] = y

        @pl.when(i >= PROMPT_TILES)
        def _():
            outs[1][...] = y
    else:
        outs[0][...] = y
    if feeds_conv:
        outs[-1][...] = _conv_glu(y, sh_ref, sc_ref, w1_bf, b1_ref)


def _combine(sched, ys, x1, meta, srcv, modr, ln_g, ln_b, li, split, conv_w1=None, conv_b1=None):
    feeds_conv = conv_w1 is not None
    tile = pl.BlockSpec((TM, D), lambda i, *_: (i, 0))
    if split:
        out_specs = [_prompt_tile_spec(D), _sample_tile_spec(D)]
        out_shape = [jax.ShapeDtypeStruct((T_PROMPT, D), F32), jax.ShapeDtypeStruct((T_SAMPLE, D), F32)]
    else:
        out_specs = [tile]
        out_shape = [jax.ShapeDtypeStruct((T, D), F32)]
    extra_specs, extra_args, extra_scratch = [], [], []
    if feeds_conv:
        extra_specs = [_mod_spec(li + 1, 0), _mod_spec(li + 1, 1),
                       _resident_f32_weight((D, 2 * D)), _row_spec((1, 2 * D))]
        extra_args = [modr, modr, conv_w1, conv_b1]
        extra_scratch = [pltpu.VMEM((D, 2 * D), BF16)]
        out_specs = out_specs + [tile]
        out_shape = out_shape + [jax.ShapeDtypeStruct((T, D), F32)]
    return pl.pallas_call(
        functools.partial(_combine_kernel, split=split, feeds_conv=feeds_conv),
        grid_spec=pltpu.PrefetchScalarGridSpec(
            num_scalar_prefetch=2,
            grid=(N_TILES,),
            in_specs=[pl.BlockSpec(memory_space=pl.ANY),
                      pl.BlockSpec((TM, D), lambda i, *_: (i, 0)),
                      pl.BlockSpec((TM, LANES), lambda i, *_: (i, 0)),
                      pl.BlockSpec((1, 1, LANES), lambda i, *_: (i, 0, 0)),
                      _mod_spec(li, 5), _row_spec((1, D)), _row_spec((1, D))] + extra_specs,
            out_specs=out_specs,
            scratch_shapes=[pltpu.VMEM((2, SORT_ROWS, D), F32), pltpu.SemaphoreType.DMA((2,))] + extra_scratch,
        ),
        out_shape=out_shape,
        compiler_params=_cparams(("arbitrary",)),
        name="moe_combine",
    )(*sched, ys, x1, meta, srcv, modr, ln_g, ln_b, *extra_args)


def _moe_schedule(tile_counts):
    n = (tile_counts + RUN_ALIGN - 1) // RUN_ALIGN * RUN_ALIGN
    src = jnp.cumsum(n, axis=1) - n
    per_expert = jnp.sum(n, axis=0)
    seg = (per_expert + MOE_TM - 1) // MOE_TM * MOE_TM
    seg_start = jnp.cumsum(seg) - seg
    dst = seg_start[None, :] + jnp.cumsum(n, axis=0) - n
    g_row = jnp.arange(SORT_GROUPS, dtype=jnp.int32) * RUN_ALIGN
    in_run = jnp.logical_and(src[:, None, :] <= g_row[None, :, None],
                             g_row[None, :, None] < (src + n)[:, None, :])
    dst_g = g_row[None, :] + jnp.sum(jnp.where(in_run, (dst - src)[:, None, :], 0), axis=2)
    runs = (jnp.sum(n, axis=1), dst_g.reshape(-1))
    srcv = jnp.pad(src.astype(F32), ((0, 0), (ROUTER_LANE0, LANES - ROUTER_LANE0 - N_EXPERTS)))
    return runs, srcv.reshape(N_TILES, 1, LANES), (seg_start, seg // MOE_TM)


def _moe(x1, u2, meta, cnt, modr, w_gate, w_up, w_down, ln_g, ln_b, li, split, **next_conv):
    tile_counts = cnt[:, 0, ROUTER_LANE0:ROUTER_LANE0 + N_EXPERTS].astype(jnp.int32)
    runs, srcv, (seg_start, seg_chunks) = _moe_schedule(tile_counts)
    xs = _dispatch(runs, u2, meta, srcv)
    ys = _experts(seg_start, seg_chunks, xs, w_gate, w_up, w_down, li)
    return _combine(runs, ys, x1, meta, srcv, modr, ln_g, ln_b, li, split, **next_conv)


def _router_slab(wg, bg, we, be):
    w = jnp.concatenate([wg, we.transpose(1, 0, 2).reshape(D, N_EXPERTS)], axis=1)
    b = jnp.concatenate([bg, be.reshape(N_EXPERTS)])
    pad = LANES - w.shape[1]
    return jnp.pad(w, ((0, 0), (0, pad))), jnp.pad(b, (0, pad)).reshape(1, LANES)


def kernel(x_prompt, x_sample, cache_diff_k, cache_diff_v, state_ret_fwd, state_ret_bwd, c, c_ctx, mod_w, mod_b, ln1_g, ln1_b, ln2_g, ln2_b, mix_w_in, mix_w_out, ret_decay_fwd, ret_decay_bwd, diff_lq1, diff_lk1, diff_lq2, diff_lk2, diff_subln_g, conv_w1, conv_b1, conv_dw, conv_dw_b, conv_ln_g, conv_ln_b, conv_w2, conv_b2, router_g_w, router_g_b, router_e_w, router_e_b, moe_w_gate, moe_w_up, moe_w_down):
    xp = x_prompt.reshape(T_PROMPT, D)
    xs = x_sample.reshape(T_SAMPLE, D)
    cond = jnp.concatenate([c_ctx[None, :], c, jnp.zeros((MOD_ROWS - 1 - DEC_BATCH, D), F32)], axis=0)
    modr = _mod_vectors(cond, mod_w, mod_b).reshape(DEPTH * MOD_ROWS * 6, 1, D)
    cos, sin_signed = _rope_tables()

    def row(v):
        return v.reshape(1, -1)

    x = None
    caches = None
    for li in range(DEPTH):
        wr, br = _router_slab(router_g_w[li], router_g_b[li], router_e_w[li], router_e_b[li])
        if li % 2 == 0:
            assert li == 0, "the even mixer reads the kernel inputs directly"
            e = li // 2
            lam_init = 0.8 - 0.6 * math.exp(-0.3 * li)
            proj, ck, cv = _in_proj(xp, xs, modr, mix_w_in[e], li)
            dec = jnp.concatenate([ret_decay_fwd[e], ret_decay_bwd[e]])
            r_p, sf, sb = _retention(proj, dec, BATCH, SEQ, 0, HEADS, emit_state=True)
            (r_s,) = _retention(proj, dec, DEC_BATCH, DEC_SEQ, T_PROMPT // DEC_SEQ, 2,
                                s0f=state_ret_fwd, s0b=state_ret_bwd, e=e)
            lams = (row(diff_lq1[e]), row(diff_lk1[e]), row(diff_lq2[e]), row(diff_lk2[e]),
                    row(diff_subln_g[e]))
            o_p = _attn_prompt(proj, *lams, lam_init)
            o_s = _attn_sample(proj, cache_diff_k, cache_diff_v, cos, sin_signed, *lams, lam_init, e)
            x1, u2, meta, cnt = _out_proj_tail(r_p, r_s, o_p, o_s, mix_w_out[e], xp, xs, modr,
                                               row(ln1_g[li]), row(ln1_b[li]), wr, br, li)
            caches = (ck, cv, sf, sb)
        else:
            o = li // 2
            x1, u2, meta, cnt = _conv_tail(glu, conv_dw[o], row(conv_dw_b[o]), row(conv_ln_g[o]),
                                           row(conv_ln_b[o]), conv_w2[o], row(conv_b2[o]),
                                           x, modr, row(ln1_g[li]), row(ln1_b[li]), wr, br, li)
        next_conv = {}
        if li + 1 < DEPTH and (li + 1) % 2 == 1:
            next_conv = dict(conv_w1=conv_w1[(li + 1) // 2], conv_b1=row(conv_b1[(li + 1) // 2]))
        outs = _moe(x1, u2, meta, cnt, modr, moe_w_gate, moe_w_up, moe_w_down,
                    row(ln2_g[li]), row(ln2_b[li]), li, split=(li == DEPTH - 1), **next_conv)
        x, glu = outs[0], outs[-1]

    y_prompt = outs[0].reshape(BATCH, SEQ, D)
    y_sample = outs[1].reshape(DEC_BATCH, DEC_SEQ, D)
    return (y_prompt, y_sample) + caches
```

```python
import functools
import math

import numpy as np
import jax
import jax.numpy as jnp
from jax import lax
from jax.experimental import pallas as pl
from jax.experimental.pallas import tpu as pltpu

F32 = jnp.float32
BF16 = jnp.bfloat16

D = 1024
BATCH = 16
SEQ = 256
DEPTH = 2
DEC_BATCH = 2
DEC_SEQ = 2048
PAST_LEN = 512
GRID_W = 64
HEADS = 4
HEAD_W = 128
RET_CHUNK = 128
DIFF_DK = 64
ROPE_THETA = 10000.0
IN_W = 7 * HEADS * HEAD_W
CONV_K = 31
CONV_PAD = CONV_K // 2
N_GROUPS = 4
EXPERTS_PER_GROUP = 8
N_EXPERTS = N_GROUPS * EXPERTS_PER_GROUP
D_EXPERT = 512
ALPHA = (2.0 * DEPTH) ** 0.25
LN_EPS = 1e-5
GN_EPS = 1e-6

T_PROMPT = BATCH * SEQ
T_SAMPLE = DEC_BATCH * DEC_SEQ
T = T_PROMPT + T_SAMPLE
TM = 256
N_TILES = T // TM
PROMPT_TILES = T_PROMPT // TM
SAMPLE_TILES_PER_SEQ = DEC_SEQ // TM
MOD_ROWS = 8
MOE_TM = 256
LANES = 128
SUBLANES = 8
RUN_ALIGN = SUBLANES
SORT_ROWS = -(-(2 * TM + N_EXPERTS * (RUN_ALIGN - 1)) // TM) * TM
RUN_BITS = tuple(1 << b for b in range((2 * TM).bit_length() - 1, RUN_ALIGN.bit_length() - 2, -1))
MOE_MAX_TILES = -(-(2 * T + N_TILES * N_EXPERTS * (RUN_ALIGN - 1) + N_EXPERTS * (MOE_TM - RUN_ALIGN)) // MOE_TM)
MOE_ROWS = MOE_MAX_TILES * MOE_TM
SORT_GROUPS = SORT_ROWS // RUN_ALIGN
MOE_AHEAD = 2
MOE_IN_SLOTS = MOE_AHEAD + 1
MOE_OUT_SLOTS = 2
ROUTER_LANE0 = N_GROUPS
VMEM_LIMIT = 52 * 1024 * 1024


def _cparams(sem):
    return pltpu.CompilerParams(dimension_semantics=sem, vmem_limit_bytes=VMEM_LIMIT)


def _tile_cond_row(i):
    return jnp.where(i < PROMPT_TILES, 0, 1 + (i - PROMPT_TILES) // SAMPLE_TILES_PER_SEQ)


def _mod_spec(li, k):
    return pl.BlockSpec((1, 1, D), lambda i, *_: ((li * MOD_ROWS + _tile_cond_row(i)) * 6 + k, 0, 0))


def _row_spec(shape):
    return pl.BlockSpec(shape, lambda i, *_: (0,) * len(shape))


def _resident_f32_weight(shape):
    return pl.BlockSpec(shape, lambda i, *_: (0,) * len(shape), pipeline_mode=pl.Buffered(1))


def _cast_weight_once(w_ref, w_bf):
    @pl.when(pl.program_id(0) == 0)
    def _():
        w_bf[...] = w_ref[...].astype(BF16)


def _layer_norm(x, g, b):
    mu = jnp.mean(x, axis=-1, keepdims=True)
    xc = x - mu
    var = jnp.mean(xc * xc, axis=-1, keepdims=True)
    return xc * lax.rsqrt(var + LN_EPS) * g + b


def _silu(x):
    return x * jax.nn.sigmoid(x)


def _dot(a, b):
    return jnp.dot(a, b, preferred_element_type=F32)


def _dot_nt(a, b):
    return lax.dot_general(a, b, (((1,), (1,)), ((), ())), preferred_element_type=F32)


def _dot_tn(a, b):
    return lax.dot_general(a, b, (((0,), (0,)), ((), ())), preferred_element_type=F32)


MOD_TN = 1024
MOD_USED_ROWS = 1 + DEC_BATCH


def _mod_kernel(cond_t_ref, w_ref, b_ref, o_ref):
    s = _silu(cond_t_ref[...])
    w = w_ref[0]
    o_ref[0] = jnp.zeros((MOD_ROWS, MOD_TN), F32) + b_ref[0]
    for r in range(MOD_USED_ROWS):
        o_ref[0, r:r + 1, :] = jnp.sum(w * s[:, r:r + 1], axis=0, keepdims=True) + b_ref[0]


def _mod_vectors(cond, mod_w, mod_b):
    return pl.pallas_call(
        _mod_kernel,
        grid=(DEPTH, 6 * D // MOD_TN),
        in_specs=[
            pl.BlockSpec((D, MOD_ROWS), lambda l, j: (0, 0)),
            pl.BlockSpec((1, D, MOD_TN), lambda l, j: (l, 0, j)),
            pl.BlockSpec((1, 1, MOD_TN), lambda l, j: (l, 0, j)),
        ],
        out_specs=pl.BlockSpec((1, MOD_ROWS, MOD_TN), lambda l, j: (l, 0, j)),
        out_shape=jax.ShapeDtypeStruct((DEPTH, MOD_ROWS, 6 * D), F32),
        compiler_params=_cparams(("arbitrary", "arbitrary")),
        name="mod_vectors",
    )(cond.T, mod_w, mod_b.reshape(DEPTH, 1, 6 * D))


def _prompt_tile_spec(width):
    return pl.BlockSpec((TM, width), lambda i, *_: (jnp.minimum(i, PROMPT_TILES - 1), 0))


def _sample_tile_spec(width):
    return pl.BlockSpec((TM, width), lambda i, *_: (jnp.maximum(i - PROMPT_TILES, 0), 0))


def _pick_tile(prompt_ref, sample_ref):
    return jnp.where(pl.program_id(0) < PROMPT_TILES, prompt_ref[...], sample_ref[...])


def _in_proj_kernel(xp_ref, xs_ref, sh_ref, sc_ref, w_ref, o_ref, ck_ref, cv_ref, w_bf):
    _cast_weight_once(w_ref, w_bf)
    u = _pick_tile(xp_ref, xs_ref) * (1.0 + sc_ref[0]) + sh_ref[0]
    proj = _dot(u.astype(BF16), w_bf[...])
    o_ref[...] = proj.astype(BF16)

    @pl.when(pl.program_id(0) < PROMPT_TILES)
    def _():
        for h in range(HEADS):
            ck_ref[0, 0, h] = proj[:, (COL_KD + h) * HEAD_W:(COL_KD + h + 1) * HEAD_W]
            cv_ref[0, 0, h] = proj[:, (COL_VD + h) * HEAD_W:(COL_VD + h + 1) * HEAD_W]


def _in_proj(x_prompt, x_sample, modr, w_in, li):
    cache_spec = pl.BlockSpec((1, 1, HEADS, SEQ, HEAD_W),
                              lambda i: (jnp.minimum(i, PROMPT_TILES - 1), 0, 0, 0, 0))
    cache_shape = jax.ShapeDtypeStruct((BATCH, 1, HEADS, SEQ, HEAD_W), F32)
    return pl.pallas_call(
        _in_proj_kernel,
        grid=(N_TILES,),
        in_specs=[
            _prompt_tile_spec(D), _sample_tile_spec(D),
            _mod_spec(li, 0),
            _mod_spec(li, 1),
            _resident_f32_weight((D, IN_W)),
        ],
        out_specs=[pl.BlockSpec((TM, IN_W), lambda i: (i, 0)), cache_spec, cache_spec],
        out_shape=[jax.ShapeDtypeStruct((T, IN_W), BF16), cache_shape, cache_shape],
        scratch_shapes=[pltpu.VMEM((D, IN_W), BF16)],
        compiler_params=_cparams(("arbitrary",)),
        name="in_proj",
    )(x_prompt, x_sample, modr, modr, w_in)


COL_QR, COL_KR, COL_VR, COL_GR, COL_QD, COL_KD, COL_VD = (k * HEADS for k in range(7))


def _retention_kernel(dec_ref, q_ref, k_ref, v_ref, g_ref, *rest, n_chunks, n_heads, has_state, emit_state):
    rest = list(rest)
    if has_state:
        s0f_ref, s0b_ref = rest[:2]
        rest = rest[2:]
    r_ref = rest[0]
    rest = rest[1:]
    if emit_state:
        sf_ref, sb_ref = rest[:2]
        rest = rest[2:]
    of_ref = rest[0]

    head0 = pl.program_id(1) * n_heads
    C = RET_CHUNK
    ii = lax.broadcasted_iota(jnp.int32, (C, C), 0)
    jj = lax.broadcasted_iota(jnp.int32, (C, C), 1)
    rel = (ii - jj).astype(F32)
    idx = lax.broadcasted_iota(jnp.int32, (C, 1), 0).astype(F32)
    k_scale = HEAD_W ** -0.5

    def chunk(ref, c, h):
        return ref[c * C:(c + 1) * C, h * HEAD_W:(h + 1) * HEAD_W].astype(F32)

    def decays(direction, h):
        lg = -jnp.exp(jnp.full((1, 1), dec_ref[direction * HEADS + head0 + h], F32))
        if direction == 0:
            inner = jnp.where(rel >= 0, jnp.exp(jnp.maximum(rel, 0.0) * lg), 0.0)
            return inner, jnp.exp((idx + 1.0) * lg), jnp.exp((C - 1.0 - idx) * lg), jnp.exp(C * lg)
        inner = jnp.where(rel <= 0, jnp.exp(jnp.maximum(-rel, 0.0) * lg), 0.0)
        return inner, jnp.exp((C - idx) * lg), jnp.exp(idx * lg), jnp.exp(C * lg)

    def run(direction):
        dec = [decays(direction, h) for h in range(n_heads)]
        if has_state:
            s0_ref = s0f_ref if direction == 0 else s0b_ref
            states = [s0_ref[0, 0, h] for h in range(n_heads)]
        else:
            states = [jnp.zeros((HEAD_W, HEAD_W), F32) for _ in range(n_heads)]
        order = range(n_chunks) if direction == 0 else range(n_chunks - 1, -1, -1)
        for c in order:
            rows = slice(c * C, (c + 1) * C)
            for h in range(n_heads):
                inner, q_decay, k_decay, chunk_decay = dec[h]
                cols = slice(h * HEAD_W, (h + 1) * HEAD_W)
                s = states[h]
                qc = chunk(q_ref, c, h)
                kc = chunk(k_ref, c, h) * k_scale
                vc = chunk(v_ref, c, h).astype(BF16)
                scores = _dot_nt(qc.astype(BF16), kc.astype(BF16)) * inner
                o = _dot(scores.astype(BF16), vc) + _dot((qc * q_decay).astype(BF16), s.astype(BF16))
                states[h] = s * chunk_decay + _dot_tn((kc * k_decay).astype(BF16), vc)
                if direction == 0:
                    of_ref[rows, cols] = o
                else:
                    r = of_ref[rows, cols] + o
                    mu = jnp.mean(r, axis=-1, keepdims=True)
                    rc = r - mu
                    var = jnp.mean(rc * rc, axis=-1, keepdims=True)
                    rn = rc * lax.rsqrt(var + GN_EPS)
                    r_ref[rows, cols] = _silu(chunk(g_ref, c, h)) * rn
        return states

    sf = run(0)
    sb = run(1)
    if emit_state:
        for h in range(n_heads):
            sf_ref[0, 0, h] = sf[h]
            sb_ref[0, 0, h] = sb[h]


def _retention(proj, dec, n_seq, seq_len, row_block0, n_heads, s0f=None, s0b=None, e=0, emit_state=False):
    has_state = s0f is not None
    width = n_heads * HEAD_W

    def col(base):
        return pl.BlockSpec((seq_len, width), lambda b, h, *_: (row_block0 + b, base // n_heads + h))

    state_spec = pl.BlockSpec((1, 1, n_heads, HEAD_W, HEAD_W), lambda b, h, *_: (b, e, h, 0, 0))
    in_specs = [pl.BlockSpec(memory_space=pltpu.SMEM), col(COL_QR), col(COL_KR), col(COL_VR), col(COL_GR)]
    args = [dec, proj, proj, proj, proj]
    if has_state:
        in_specs += [state_spec, state_spec]
        args += [s0f, s0b]
    out_specs = [pl.BlockSpec((seq_len, width), lambda b, h, *_: (b, h))]
    out_shape = [jax.ShapeDtypeStruct((n_seq * seq_len, HEADS * HEAD_W), F32)]
    if emit_state:
        st = pl.BlockSpec((1, 1, n_heads, HEAD_W, HEAD_W), lambda b, h, *_: (b, 0, h, 0, 0))
        out_specs += [st, st]
        out_shape += [jax.ShapeDtypeStruct((n_seq, 1, HEADS, HEAD_W, HEAD_W), F32)] * 2
    return pl.pallas_call(
        functools.partial(_retention_kernel, n_chunks=seq_len // RET_CHUNK, n_heads=n_heads,
                          has_state=has_state, emit_state=emit_state),
        grid=(n_seq, HEADS // n_heads),
        in_specs=in_specs,
        out_specs=out_specs,
        out_shape=out_shape,
        scratch_shapes=[pltpu.VMEM((seq_len, width), F32)],
        compiler_params=_cparams(("arbitrary", "arbitrary")),
        name=f"retention_{seq_len}",
    )(*args)


def _diff_lambda(lq1_ref, lk1_ref, lq2_ref, lk2_ref, lam_init):
    a = jnp.sum(lq1_ref[...] * lk1_ref[...], axis=-1, keepdims=True)
    b = jnp.sum(lq2_ref[...] * lk2_ref[...], axis=-1, keepdims=True)
    return jnp.exp(a) - jnp.exp(b) + lam_init


LOG2E = 1.4426950408889634


def _diff_attend(q, k, v, lam, subln_g, lam_init):
    lane = lax.broadcasted_iota(jnp.int32, q.shape, 1)
    q1 = jnp.where(lane < DIFF_DK, q, 0.0).astype(BF16)
    q2 = jnp.where(lane >= DIFF_DK, q, 0.0).astype(BF16)

    def softmax_times_v(qz):
        s = _dot_nt(qz, k)
        p = jnp.exp2(s - jnp.max(s, axis=-1, keepdims=True))
        return _dot(p.astype(BF16), v) * (1.0 / jnp.sum(p, axis=-1, keepdims=True))

    o = softmax_times_v(q1) - lam * softmax_times_v(q2)
    o = o * lax.rsqrt(jnp.mean(o * o, axis=-1, keepdims=True) + LN_EPS)
    return o * subln_g * (1.0 - lam_init)


def _attn_prompt_kernel(q_ref, k_ref, v_ref, lq1, lk1, lq2, lk2, g_ref, o_ref, *, lam_init):
    lam = _diff_lambda(lq1, lk1, lq2, lk2, lam_init)
    scale = DIFF_DK ** -0.5 * LOG2E
    for h in range(HEADS):
        sl = slice(h * HEAD_W, (h + 1) * HEAD_W)
        o_ref[:, sl] = _diff_attend(q_ref[:, sl].astype(F32) * scale, k_ref[:, sl], v_ref[:, sl],
                                    lam, g_ref[...], lam_init)


def _attn_prompt(proj, lq1, lk1, lq2, lk2, subln_g, lam_init):
    W = HEADS * HEAD_W

    def slab(base):
        return pl.BlockSpec((SEQ, W), lambda b: (b, base // HEADS))

    small = _row_spec((1, DIFF_DK))
    return pl.pallas_call(
        functools.partial(_attn_prompt_kernel, lam_init=lam_init),
        grid=(BATCH,),
        in_specs=[slab(COL_QD), slab(COL_KD), slab(COL_VD), small, small, small, small,
                  _row_spec((1, HEAD_W))],
        out_specs=pl.BlockSpec((SEQ, W), lambda b: (b, 0)),
        out_shape=jax.ShapeDtypeStruct((T_PROMPT, W), F32),
        compiler_params=_cparams(("arbitrary",)),
        name="diff_attn_prompt",
    )(proj, proj, proj, lq1, lk1, lq2, lk2, subln_g)


def _rope(x, cos, sin_signed):
    lane = lax.broadcasted_iota(jnp.int32, x.shape, 1)
    partner = jnp.where((lane % 32) < 16, pltpu.roll(x, LANES - 16, 1), pltpu.roll(x, 16, 1))
    return x * cos + partner * sin_signed


def _attn_sample_kernel(q_ref, k_ref, v_ref, ck_ref, cv_ref, cosq_ref, sinq_ref, cos_ref, sin_ref,
                        lq1, lk1, lq2, lk2, g_ref, o_ref, kbuf, vbuf, *, lam_init):
    @pl.when(pl.program_id(2) == 0)
    def _():
        kbuf[0:DEC_SEQ, :] = _rope(k_ref[...].astype(F32), cos_ref[...], sin_ref[...]).astype(BF16)
        kbuf[DEC_SEQ:, :] = ck_ref[0, 0, 0].astype(BF16)
        vbuf[0:DEC_SEQ, :] = v_ref[...]
        vbuf[DEC_SEQ:, :] = cv_ref[0, 0, 0].astype(BF16)

    lam = _diff_lambda(lq1, lk1, lq2, lk2, lam_init)
    q = _rope(q_ref[...].astype(F32), cosq_ref[...], sinq_ref[...]) * (DIFF_DK ** -0.5 * LOG2E)
    o_ref[...] = _diff_attend(q, kbuf[...], vbuf[...], lam, g_ref[...], lam_init)


ATTN_TQ = 256


def _attn_sample(proj, cache_k, cache_v, cos, sin_signed, lq1, lk1, lq2, lk2, subln_g, lam_init, e):
    nq = DEC_SEQ // ATTN_TQ
    row0_q = T_PROMPT // ATTN_TQ
    row0_kv = T_PROMPT // DEC_SEQ
    small = pl.BlockSpec((1, DIFF_DK), lambda b, h, t: (0, 0))
    cache = pl.BlockSpec((1, 1, 1, PAST_LEN, HEAD_W), lambda b, h, t: (b, e, h, 0, 0))
    table_q = pl.BlockSpec((ATTN_TQ, HEAD_W), lambda b, h, t: (t, 0))
    table = pl.BlockSpec((DEC_SEQ, HEAD_W), lambda b, h, t: (0, 0))
    return pl.pallas_call(
        functools.partial(_attn_sample_kernel, lam_init=lam_init),
        grid=(DEC_BATCH, HEADS, nq),
        in_specs=[
            pl.BlockSpec((ATTN_TQ, HEAD_W), lambda b, h, t: (row0_q + b * nq + t, COL_QD + h)),
            pl.BlockSpec((DEC_SEQ, HEAD_W), lambda b, h, t: (row0_kv + b, COL_KD + h)),
            pl.BlockSpec((DEC_SEQ, HEAD_W), lambda b, h, t: (row0_kv + b, COL_VD + h)),
            cache, cache, table_q, table_q, table, table,
            small, small, small, small,
            pl.BlockSpec((1, HEAD_W), lambda b, h, t: (0, 0)),
        ],
        out_specs=pl.BlockSpec((ATTN_TQ, HEAD_W), lambda b, h, t: (b * nq + t, h)),
        out_shape=jax.ShapeDtypeStruct((T_SAMPLE, HEADS * HEAD_W), F32),
        scratch_shapes=[pltpu.VMEM((DEC_SEQ + PAST_LEN, HEAD_W), BF16),
                        pltpu.VMEM((DEC_SEQ + PAST_LEN, HEAD_W), BF16)],
        compiler_params=_cparams(("arbitrary", "arbitrary", "arbitrary")),
        name="diff_attn_sample",
    )(proj, proj, proj, cache_k, cache_v, cos, sin_signed, cos, sin_signed,
      lq1, lk1, lq2, lk2, subln_g)


def _rope_tables():
    t = np.arange(DEC_SEQ)
    row, colp = t // GRID_W, t % GRID_W
    lane = np.arange(LANES)
    pos = np.where(((lane // 32) % 2 == 0)[None, :], row[:, None], colp[:, None]).astype(np.float64)
    half = 16
    inv = (np.float32(ROPE_THETA) ** (-(np.arange(half, dtype=np.float32)) / np.float32(half))).astype(np.float32)
    ang = pos.astype(np.float32) * inv[lane % half][None, :]
    cos = np.cos(ang.astype(np.float64)).astype(np.float32)
    sin = np.sin(ang.astype(np.float64)).astype(np.float32)
    sign = np.where((lane % 32) < half, -1.0, 1.0).astype(np.float32)[None, :]
    return jnp.asarray(cos), jnp.asarray(sin * sign)


def _split_bf16(a):
    hi = a.astype(BF16)
    return hi, (a - hi.astype(F32)).astype(BF16)


def _mixer_tail(out, x, g1_ref, sc2_ref, sh2_ref, lng_ref, lnb_ref, wr_ref, br_ref,
                x1_ref, u2_ref, meta_ref, cnt_ref):
    x1 = _layer_norm(ALPHA * x + g1_ref[0] * out, lng_ref[...], lnb_ref[...])
    x1_ref[...] = x1
    u2 = x1 * (1.0 + sc2_ref[0]) + sh2_ref[0]
    u2_ref[...] = u2.astype(BF16)

    u_hi, u_lo = _split_bf16(u2)
    w_hi, w_lo = _split_bf16(wr_ref[...])
    logits = _dot(u_hi, w_hi) + (_dot(u_hi, w_lo) + _dot(u_lo, w_hi)) + br_ref[...]
    lane = lax.broadcasted_iota(jnp.int32, logits.shape, 1).astype(F32)
    neg = jnp.float32(-jnp.inf)
    is_g = lane < N_GROUPS
    gl = jnp.where(is_g, logits, neg)
    gmax = jnp.max(gl, axis=-1, keepdims=True)
    gsel = jnp.min(jnp.where(gl == gmax, lane, float(LANES)), axis=-1, keepdims=True)
    p_g = 1.0 / jnp.sum(jnp.where(is_g, jnp.exp(gl - gmax), 0.0), axis=-1, keepdims=True)
    lo = ROUTER_LANE0 + gsel * EXPERTS_PER_GROUP
    el = jnp.where((lane >= lo) & (lane < lo + EXPERTS_PER_GROUP), logits, neg)
    v1 = jnp.max(el, axis=-1, keepdims=True)
    i1 = jnp.min(jnp.where(el == v1, lane, float(LANES)), axis=-1, keepdims=True)
    el2 = jnp.where(lane == i1, neg, el)
    v2 = jnp.max(el2, axis=-1, keepdims=True)
    i2 = jnp.min(jnp.where(el2 == v2, lane, float(LANES)), axis=-1, keepdims=True)
    t = jnp.exp(v2 - v1)
    w1 = p_g / (1.0 + t)
    w2 = w1 * t

    oh1 = (lane == i1).astype(F32)
    oh2 = (lane == i2).astype(F32)
    oh = oh1 + oh2
    r_i = lax.broadcasted_iota(jnp.int32, (TM, TM), 0)
    c_i = lax.broadcasted_iota(jnp.int32, (TM, TM), 1)
    before = (c_i < r_i).astype(BF16)
    earlier = _dot(before, oh.astype(BF16))
    rank1 = jnp.sum(earlier * oh1, axis=-1, keepdims=True)
    rank2 = jnp.sum(earlier * oh2, axis=-1, keepdims=True)
    cnt_ref[0] = jnp.sum(oh, axis=0, keepdims=True)
    cols = (i1, i2, w1, w2, rank1, rank2)
    meta = jnp.zeros_like(logits)
    for k, col in enumerate(cols):
        meta = jnp.where(lane == k, col, meta)
    meta_ref[...] = meta


META_E1, META_E2, META_W1, META_W2, META_RANK1, META_RANK2 = range(6)

_TAIL_OUT_SHAPES = [
    jax.ShapeDtypeStruct((T, D), F32),
    jax.ShapeDtypeStruct((T, D), BF16),
    jax.ShapeDtypeStruct((T, LANES), F32),
    jax.ShapeDtypeStruct((N_TILES, 1, LANES), F32),
]


def _tail_out_specs():
    return [
        pl.BlockSpec((TM, D), lambda i: (i, 0)),
        pl.BlockSpec((TM, D), lambda i: (i, 0)),
        pl.BlockSpec((TM, LANES), lambda i: (i, 0)),
        pl.BlockSpec((1, 1, LANES), lambda i: (i, 0, 0)),
    ]


def _tail_in_specs(li):
    return [
        _mod_spec(li, 2), _mod_spec(li, 4), _mod_spec(li, 3),
        _row_spec((1, D)), _row_spec((1, D)),
        _row_spec((D, LANES)), _row_spec((1, LANES)),
    ]


def _out_proj_kernel(rp_ref, rs_ref, op_ref, os_ref, w_ref, xp_ref, xs_ref, *rest):
    tail_args, w_bf = rest[:-1], rest[-1]
    _cast_weight_once(w_ref, w_bf)
    half = HEADS * HEAD_W
    r = _pick_tile(rp_ref, rs_ref).astype(BF16)
    o = _pick_tile(op_ref, os_ref).astype(BF16)
    out = _dot(r, w_bf[0:half, :]) + _dot(o, w_bf[half:, :])
    _mixer_tail(out, _pick_tile(xp_ref, xs_ref), *tail_args)


def _out_proj_tail(r_p, r_s, o_p, o_s, w_out, x_prompt, x_sample, modr, ln_g, ln_b, wr, br, li):
    half = HEADS * HEAD_W
    return pl.pallas_call(
        _out_proj_kernel,
        grid=(N_TILES,),
        in_specs=[_prompt_tile_spec(half), _sample_tile_spec(half),
                  _prompt_tile_spec(half), _sample_tile_spec(half),
                  _resident_f32_weight((2 * half, D)),
                  _prompt_tile_spec(D), _sample_tile_spec(D)] + _tail_in_specs(li),
        out_specs=_tail_out_specs(),
        out_shape=_TAIL_OUT_SHAPES,
        scratch_shapes=[pltpu.VMEM((2 * half, D), BF16)],
        compiler_params=_cparams(("arbitrary",)),
        name="out_proj_tail",
    )(r_p, r_s, o_p, o_s, w_out, x_prompt, x_sample, modr, modr, modr, ln_g, ln_b, wr, br)


def _conv_glu(y, sh_ref, sc_ref, w_bf, b_ref):
    u = y * (1.0 + sc_ref[0]) + sh_ref[0]
    h = _dot(u.astype(BF16), w_bf[...]) + b_ref[...]
    return h[:, :D] * jax.nn.sigmoid(h[:, D:])


HALO = 16
CONV_ROWS = 64
CONV_COLS = 128


def _depthwise_conv(hp, dw_ref, conv):
    base = HALO - CONV_PAD
    for cb in range(D // CONV_COLS):
        cs = slice(cb * CONV_COLS, (cb + 1) * CONV_COLS)
        for rb in range(TM // CONV_ROWS):
            r0 = rb * CONV_ROWS
            acc = None
            for shift in range(SUBLANES):
                part = None
                for tap in range(CONV_K):
                    off = base + tap
                    if off % SUBLANES != shift:
                        continue
                    a0 = r0 + off - shift
                    term = hp[a0:a0 + CONV_ROWS + SUBLANES, cs] * dw_ref[tap:tap + 1, cs]
                    part = term if part is None else part + term
                part = part[shift:shift + CONV_ROWS, :]
                acc = part if acc is None else acc + part
            conv[r0:r0 + CONV_ROWS, cs] = acc


def _conv_tail_kernel(cur_ref, prev_ref, next_ref, dw_ref, dwb_ref, cg_ref, cb_ref, w2_ref, b2_ref,
                      x_ref, *rest):
    tail_args, (hp, conv, w2_bf) = rest[:-3], rest[-3:]
    _cast_weight_once(w2_ref, w2_bf)
    i = pl.program_id(0)
    k = (i - PROMPT_TILES) % SAMPLE_TILES_PER_SEQ
    in_sample = i >= PROMPT_TILES
    left_ok = jnp.logical_and(in_sample, k != 0)
    right_ok = jnp.logical_and(in_sample, k != SAMPLE_TILES_PER_SEQ - 1)
    hp[0:HALO, :] = jnp.where(left_ok, prev_ref[...], 0.0)
    hp[HALO:HALO + TM, :] = cur_ref[...]
    hp[HALO + TM:HALO + TM + HALO, :] = jnp.where(right_ok, next_ref[...], 0.0)
    _depthwise_conv(hp, dw_ref, conv)
    hc = _silu(_layer_norm(conv[...] + dwb_ref[...], cg_ref[...], cb_ref[...]))
    out = _dot(hc.astype(BF16), w2_bf[...]) + b2_ref[...]
    _mixer_tail(out, x_ref[...], *tail_args)


def _conv_tail(glu, dw, dwb, cg, cb, w2, b2, x, modr, ln_g, ln_b, wr, br, li):
    per = TM // HALO
    last = T // HALO - 1
    return pl.pallas_call(
        _conv_tail_kernel,
        grid=(N_TILES,),
        in_specs=[pl.BlockSpec((TM, D), lambda i: (i, 0)),
                  pl.BlockSpec((HALO, D), lambda i: (jnp.maximum(i * per - 1, 0), 0)),
                  pl.BlockSpec((HALO, D), lambda i: (jnp.minimum((i + 1) * per, last), 0)),
                  _row_spec((CONV_K, D)), _row_spec((1, D)), _row_spec((1, D)), _row_spec((1, D)),
                  _resident_f32_weight((D, D)), _row_spec((1, D)),
                  pl.BlockSpec((TM, D), lambda i: (i, 0))] + _tail_in_specs(li),
        out_specs=_tail_out_specs(),
        out_shape=_TAIL_OUT_SHAPES,
        scratch_shapes=[pltpu.VMEM((TM + 2 * HALO, D), F32), pltpu.VMEM((TM, D), F32),
                        pltpu.VMEM((D, D), BF16)],
        compiler_params=_cparams(("arbitrary",)),
        name="conv_tail",
    )(glu, glu, glu, dw, dwb, cg, cb, w2, b2, x, modr, modr, modr, ln_g, ln_b, wr, br)


def _sorted_positions(meta, srcv):
    lane = lax.broadcasted_iota(jnp.int32, meta.shape, 1).astype(F32)

    def pos(e_col, r_col):
        start = jnp.sum(jnp.where(lane == meta[:, e_col:e_col + 1], srcv, 0.0), axis=-1, keepdims=True)
        return start + meta[:, r_col:r_col + 1]

    return pos(META_E1, META_RANK1), pos(META_E2, META_RANK2)


def _one_hot_rows(pos):
    col = lax.broadcasted_iota(jnp.int32, (TM, SORT_ROWS), 1).astype(F32)
    return col == pos


def _for_each_row_group(tile, tot_ref, dstg_ref, fn):
    def body(g, carry):
        fn(pl.multiple_of(g * RUN_ALIGN, RUN_ALIGN),
           pl.multiple_of(dstg_ref[tile * SORT_GROUPS + g], RUN_ALIGN))
        return carry

    lax.fori_loop(0, tot_ref[tile] // RUN_ALIGN, body, 0)


def _wait_rows(total, make_copy):
    for bit in RUN_BITS:
        @pl.when((total & bit) != 0)
        def _(bit=bit):
            make_copy(bit).wait()


def _dispatch_kernel(tot_ref, dstg_ref, u_ref, meta_ref, srcv_ref, xs_ref, sorted_ref, sems):
    i = pl.program_id(0)
    slot = i % 2

    def wait_tile(tile, slot):
        buf = sorted_ref.at[slot]
        _wait_rows(tot_ref[tile], lambda rows: pltpu.make_async_copy(
            buf.at[pl.ds(0, rows)], xs_ref.at[pl.ds(0, rows)], sems.at[slot]))

    @pl.when(i >= 2)
    def _():
        wait_tile(i - 2, slot)

    pos1, pos2 = _sorted_positions(meta_ref[...], srcv_ref[0])
    select = jnp.logical_or(_one_hot_rows(pos1), _one_hot_rows(pos2)).astype(BF16)
    sorted_ref[slot] = _dot_tn(select, u_ref[...])
    buf = sorted_ref.at[slot]

    def start(src, dst):
        pltpu.make_async_copy(buf.at[pl.ds(src, RUN_ALIGN)], xs_ref.at[pl.ds(dst, RUN_ALIGN)],
                              sems.at[slot]).start()

    _for_each_row_group(i, tot_ref, dstg_ref, start)

    @pl.when(i == N_TILES - 1)
    def _():
        wait_tile(i - 1, 1 - slot)
        wait_tile(i, slot)


def _dispatch(sched, u2, meta, srcv):
    return pl.pallas_call(
        _dispatch_kernel,
        grid_spec=pltpu.PrefetchScalarGridSpec(
            num_scalar_prefetch=2,
            grid=(N_TILES,),
            in_specs=[pl.BlockSpec((TM, D), lambda i, *_: (i, 0)),
                      pl.BlockSpec((TM, LANES), lambda i, *_: (i, 0)),
                      pl.BlockSpec((1, 1, LANES), lambda i, *_: (i, 0, 0))],
            out_specs=pl.BlockSpec(memory_space=pl.ANY),
            scratch_shapes=[pltpu.VMEM((2, SORT_ROWS, D), F32), pltpu.SemaphoreType.DMA((2,))],
        ),
        out_shape=jax.ShapeDtypeStruct((MOE_ROWS, D), F32),
        compiler_params=_cparams(("arbitrary",)),
        name="moe_dispatch",
    )(*sched, u2, meta, srcv)


def _experts_kernel(start_ref, chunks_ref, xs_ref, wg_ref, wu_ref, wd_ref, ys_ref,
                    wg_bf, wu_bf, wd_bf, xbuf, ybuf, in_sems, out_sems):
    e = pl.program_id(0)
    n = chunks_ref[e]
    first = start_ref[e] // MOE_TM
    total = start_ref[N_EXPERTS - 1] // MOE_TM + chunks_ref[N_EXPERTS - 1]

    def rows(g):
        return pl.ds(pl.multiple_of(g * MOE_TM, MOE_TM), MOE_TM)

    def load(g):
        slot = g % MOE_IN_SLOTS
        return pltpu.make_async_copy(xs_ref.at[rows(g)], xbuf.at[slot], in_sems.at[slot])

    def store(g):
        slot = g % MOE_OUT_SLOTS
        return pltpu.make_async_copy(ybuf.at[slot], ys_ref.at[rows(g)], out_sems.at[slot])

    @pl.when(e == 0)
    def _():
        for g in range(MOE_AHEAD):
            @pl.when(g < total)
            def _(g=g):
                load(g).start()

    @pl.when(n > 0)
    def _():
        wg_bf[...] = wg_ref[0, 0].astype(BF16)
        wu_bf[...] = wu_ref[0, 0].astype(BF16)
        wd_bf[...] = wd_ref[0, 0].astype(BF16)

        def tile(g, carry):
            load(g).wait()

            @pl.when(g + MOE_AHEAD < total)
            def _():
                load(g + MOE_AHEAD).start()

            @pl.when(g >= MOE_OUT_SLOTS)
            def _():
                store(g - MOE_OUT_SLOTS).wait()

            x = xbuf[g % MOE_IN_SLOTS].astype(BF16)
            h = (_silu(_dot(x, wg_bf[...])) * _dot(x, wu_bf[...])).astype(BF16)
            ybuf[g % MOE_OUT_SLOTS] = _dot(h, wd_bf[...])
            store(g).start()
            return carry

        lax.fori_loop(first, first + n, tile, 0)

    @pl.when(e == N_EXPERTS - 1)
    def _():
        for back in range(MOE_OUT_SLOTS, 0, -1):
            @pl.when(total >= back)
            def _(back=back):
                store(total - back).wait()


def _experts(seg_start, seg_chunks, xs, w_gate, w_up, w_down, li):
    def weight(shape):
        return pl.BlockSpec((1, 1) + shape, lambda e, *_: (li, e, 0, 0))

    return pl.pallas_call(
        _experts_kernel,
        grid_spec=pltpu.PrefetchScalarGridSpec(
            num_scalar_prefetch=2,
            grid=(N_EXPERTS,),
            in_specs=[pl.BlockSpec(memory_space=pl.ANY),
                      weight((D, D_EXPERT)), weight((D, D_EXPERT)), weight((D_EXPERT, D))],
            out_specs=pl.BlockSpec(memory_space=pl.ANY),
            scratch_shapes=[pltpu.VMEM((D, D_EXPERT), BF16), pltpu.VMEM((D, D_EXPERT), BF16),
                            pltpu.VMEM((D_EXPERT, D), BF16),
                            pltpu.VMEM((MOE_IN_SLOTS, MOE_TM, D), F32),
                            pltpu.VMEM((MOE_OUT_SLOTS, MOE_TM, D), F32),
                            pltpu.SemaphoreType.DMA((MOE_IN_SLOTS,)),
                            pltpu.SemaphoreType.DMA((MOE_OUT_SLOTS,))],
        ),
        out_shape=jax.ShapeDtypeStruct((MOE_ROWS, D), F32),
        compiler_params=_cparams(("arbitrary",)),
        name="moe_experts",
    )(seg_start, seg_chunks, xs, w_gate, w_up, w_down)


def _combine_kernel(tot_ref, dstg_ref, ys_ref, x1_ref, meta_ref, srcv_ref, g2_ref,
                    lng_ref, lnb_ref, *rest, split, feeds_conv):
    if feeds_conv:
        (sh_ref, sc_ref, w1_ref, b1_ref), rest, w1_bf = rest[:4], rest[4:-1], rest[-1]
        _cast_weight_once(w1_ref, w1_bf)
    outs, (sorted_ref, sems) = rest[:-2], rest[-2:]
    i = pl.program_id(0)
    slot = i % 2

    def fetch(tile, slot):
        buf = sorted_ref.at[slot]

        def start(src, dst):
            pltpu.make_async_copy(ys_ref.at[pl.ds(dst, RUN_ALIGN)], buf.at[pl.ds(src, RUN_ALIGN)],
                                  sems.at[slot]).start()

        _for_each_row_group(tile, tot_ref, dstg_ref, start)

    @pl.when(i == 0)
    def _():
        sorted_ref[...] = jnp.zeros_like(sorted_ref)
        fetch(0, 0)

    @pl.when(i + 1 < N_TILES)
    def _():
        fetch(i + 1, 1 - slot)

    meta = meta_ref[...]
    pos1, pos2 = _sorted_positions(meta, srcv_ref[0])
    sel1 = _one_hot_rows(pos1).astype(BF16)
    sel2 = _one_hot_rows(pos2).astype(BF16)
    buf = sorted_ref.at[slot]
    _wait_rows(tot_ref[i], lambda rows: pltpu.make_async_copy(
        ys_ref.at[pl.ds(0, rows)], buf.at[pl.ds(0, rows)], sems.at[slot]))
    ysort = sorted_ref[slot].astype(BF16)
    f = (meta[:, META_W1:META_W1 + 1] * _dot(sel1, ysort)
         + meta[:, META_W2:META_W2 + 1] * _dot(sel2, ysort))
    y = _layer_norm(ALPHA * x1_ref[...] + g2_ref[0] * f, lng_ref[...], lnb_ref[...])
    if split:
        @pl.when(i < PROMPT_TILES)
        def _():
            outs[0][...] = y

        @pl.when(i >= PROMPT_TILES)
        def _():
            outs[1][...] = y
    else:
        outs[0][...] = y
    if feeds_conv:
        outs[-1][...] = _conv_glu(y, sh_ref, sc_ref, w1_bf, b1_ref)


def _combine(sched, ys, x1, meta, srcv, modr, ln_g, ln_b, li, split, conv_w1=None, conv_b1=None):
    feeds_conv = conv_w1 is not None
    tile = pl.BlockSpec((TM, D), lambda i, *_: (i, 0))
    if split:
        out_specs = [_prompt_tile_spec(D), _sample_tile_spec(D)]
        out_shape = [jax.ShapeDtypeStruct((T_PROMPT, D), F32), jax.ShapeDtypeStruct((T_SAMPLE, D), F32)]
    else:
        out_specs = [tile]
        out_shape = [jax.ShapeDtypeStruct((T, D), F32)]
    extra_specs, extra_args, extra_scratch = [], [], []
    if feeds_conv:
        extra_specs = [_mod_spec(li + 1, 0), _mod_spec(li + 1, 1),
                       _resident_f32_weight((D, 2 * D)), _row_spec((1, 2 * D))]
        extra_args = [modr, modr, conv_w1, conv_b1]
        extra_scratch = [pltpu.VMEM((D, 2 * D), BF16)]
        out_specs = out_specs + [tile]
        out_shape = out_shape + [jax.ShapeDtypeStruct((T, D), F32)]
    return pl.pallas_call(
        functools.partial(_combine_kernel, split=split, feeds_conv=feeds_conv),
        grid_spec=pltpu.PrefetchScalarGridSpec(
            num_scalar_prefetch=2,
            grid=(N_TILES,),
            in_specs=[pl.BlockSpec(memory_space=pl.ANY),
                      pl.BlockSpec((TM, D), lambda i, *_: (i, 0)),
                      pl.BlockSpec((TM, LANES), lambda i, *_: (i, 0)),
                      pl.BlockSpec((1, 1, LANES), lambda i, *_: (i, 0, 0)),
                      _mod_spec(li, 5), _row_spec((1, D)), _row_spec((1, D))] + extra_specs,
            out_specs=out_specs,
            scratch_shapes=[pltpu.VMEM((2, SORT_ROWS, D), F32), pltpu.SemaphoreType.DMA((2,))] + extra_scratch,
        ),
        out_shape=out_shape,
        compiler_params=_cparams(("arbitrary",)),
        name="moe_combine",
    )(*sched, ys, x1, meta, srcv, modr, ln_g, ln_b, *extra_args)


def _moe_schedule(tile_counts):
    n = (tile_counts + RUN_ALIGN - 1) // RUN_ALIGN * RUN_ALIGN
    src = jnp.cumsum(n, axis=1) - n
    per_expert = jnp.sum(n, axis=0)
    seg = (per_expert + MOE_TM - 1) // MOE_TM * MOE_TM
    seg_start = jnp.cumsum(seg) - seg
    dst = seg_start[None, :] + jnp.cumsum(n, axis=0) - n
    g_row = jnp.arange(SORT_GROUPS, dtype=jnp.int32) * RUN_ALIGN
    in_run = jnp.logical_and(src[:, None, :] <= g_row[None, :, None],
                             g_row[None, :, None] < (src + n)[:, None, :])
    dst_g = g_row[None, :] + jnp.sum(jnp.where(in_run, (dst - src)[:, None, :], 0), axis=2)
    runs = (jnp.sum(n, axis=1), dst_g.reshape(-1))
    srcv = jnp.pad(src.astype(F32), ((0, 0), (ROUTER_LANE0, LANES - ROUTER_LANE0 - N_EXPERTS)))
    return runs, srcv.reshape(N_TILES, 1, LANES), (seg_start, seg // MOE_TM)


def _moe(x1, u2, meta, cnt, modr, w_gate, w_up, w_down, ln_g, ln_b, li, split, **next_conv):
    tile_counts = cnt[:, 0, ROUTER_LANE0:ROUTER_LANE0 + N_EXPERTS].astype(jnp.int32)
    runs, srcv, (seg_start, seg_chunks) = _moe_schedule(tile_counts)
    xs = _dispatch(runs, u2, meta, srcv)
    ys = _experts(seg_start, seg_chunks, xs, w_gate, w_up, w_down, li)
    return _combine(runs, ys, x1, meta, srcv, modr, ln_g, ln_b, li, split, **next_conv)


def _router_slab(wg, bg, we, be):
    w = jnp.concatenate([wg, we.transpose(1, 0, 2).reshape(D, N_EXPERTS)], axis=1)
    b = jnp.concatenate([bg, be.reshape(N_EXPERTS)])
    pad = LANES - w.shape[1]
    return jnp.pad(w, ((0, 0), (0, pad))), jnp.pad(b, (0, pad)).reshape(1, LANES)


def kernel(x_prompt, x_sample, cache_diff_k, cache_diff_v, state_ret_fwd, state_ret_bwd, c, c_ctx, mod_w, mod_b, ln1_g, ln1_b, ln2_g, ln2_b, mix_w_in, mix_w_out, ret_decay_fwd, ret_decay_bwd, diff_lq1, diff_lk1, diff_lq2, diff_lk2, diff_subln_g, conv_w1, conv_b1, conv_dw, conv_dw_b, conv_ln_g, conv_ln_b, conv_w2, conv_b2, router_g_w, router_g_b, router_e_w, router_e_b, moe_w_gate, moe_w_up, moe_w_down):
    xp = x_prompt.reshape(T_PROMPT, D)
    xs = x_sample.reshape(T_SAMPLE, D)
    cond = jnp.concatenate([c_ctx[None, :], c, jnp.zeros((MOD_ROWS - 1 - DEC_BATCH, D), F32)], axis=0)
    modr = _mod_vectors(cond, mod_w, mod_b).reshape(DEPTH * MOD_ROWS * 6, 1, D)
    cos, sin_signed = _rope_tables()

    def row(v):
        return v.reshape(1, -1)

    x = None
    caches = None
    for li in range(DEPTH):
        wr, br = _router_slab(router_g_w[li], router_g_b[li], router_e_w[li], router_e_b[li])
        if li % 2 == 0:
            assert li == 0, "the even mixer reads the kernel inputs directly"
            e = li // 2
            lam_init = 0.8 - 0.6 * math.exp(-0.3 * li)
            proj, ck, cv = _in_proj(xp, xs, modr, mix_w_in[e], li)
            dec = jnp.concatenate([ret_decay_fwd[e], ret_decay_bwd[e]])
            r_p, sf, sb = _retention(proj, dec, BATCH, SEQ, 0, HEADS, emit_state=True)
            (r_s,) = _retention(proj, dec, DEC_BATCH, DEC_SEQ, T_PROMPT // DEC_SEQ, 2,
                                s0f=state_ret_fwd, s0b=state_ret_bwd, e=e)
            lams = (row(diff_lq1[e]), row(diff_lk1[e]), row(diff_lq2[e]), row(diff_lk2[e]),
                    row(diff_subln_g[e]))
            o_p = _attn_prompt(proj, *lams, lam_init)
            o_s = _attn_sample(proj, cache_diff_k, cache_diff_v, cos, sin_signed, *lams, lam_init, e)
            x1, u2, meta, cnt = _out_proj_tail(r_p, r_s, o_p, o_s, mix_w_out[e], xp, xs, modr,
                                               row(ln1_g[li]), row(ln1_b[li]), wr, br, li)
            caches = (ck, cv, sf, sb)
        else:
            o = li // 2
            x1, u2, meta, cnt = _conv_tail(glu, conv_dw[o], row(conv_dw_b[o]), row(conv_ln_g[o]),
                                           row(conv_ln_b[o]), conv_w2[o], row(conv_b2[o]),
                                           x, modr, row(ln1_g[li]), row(ln1_b[li]), wr, br, li)
        next_conv = {}
        if li + 1 < DEPTH and (li + 1) % 2 == 1:
            next_conv = dict(conv_w1=conv_w1[(li + 1) // 2], conv_b1=row(conv_b1[(li + 1) // 2]))
        outs = _moe(x1, u2, meta, cnt, modr, moe_w_gate, moe_w_up, moe_w_down,
                    row(ln2_g[li]), row(ln2_b[li]), li, split=(li == DEPTH - 1), **next_conv)
        x, glu = outs[0], outs[-1]

    y_prompt = outs[0].reshape(BATCH, SEQ, D)
    y_sample = outs[1].reshape(DEC_BATCH, DEC_SEQ, D)
    return (y_prompt, y_sample) + caches
```

```python
import functools
import math

import numpy as np
import jax
import jax.numpy as jnp
from jax import lax
from jax.experimental import pallas as pl
from jax.experimental.pallas import tpu as pltpu

F32 = jnp.float32
BF16 = jnp.bfloat16

D = 1024
BATCH = 16
SEQ = 256
DEPTH = 2
DEC_BATCH = 2
DEC_SEQ = 2048
PAST_LEN = 512
GRID_W = 64
HEADS = 4
HEAD_W = 128
RET_CHUNK = 128
DIFF_DK = 64
ROPE_THETA = 10000.0
IN_W = 7 * HEADS * HEAD_W
CONV_K = 31
CONV_PAD = CONV_K // 2
N_GROUPS = 4
EXPERTS_PER_GROUP = 8
N_EXPERTS = N_GROUPS * EXPERTS_PER_GROUP
D_EXPERT = 512
ALPHA = (2.0 * DEPTH) ** 0.25
LN_EPS = 1e-5
GN_EPS = 1e-6

T_PROMPT = BATCH * SEQ
T_SAMPLE = DEC_BATCH * DEC_SEQ
T = T_PROMPT + T_SAMPLE
TM = 256
N_TILES = T // TM
PROMPT_TILES = T_PROMPT // TM
SAMPLE_TILES_PER_SEQ = DEC_SEQ // TM
MOD_ROWS = 8
MOE_TM = 256
LANES = 128
SUBLANES = 8
RUN_ALIGN = SUBLANES
SORT_ROWS = -(-(2 * TM + N_EXPERTS * (RUN_ALIGN - 1)) // TM) * TM
RUN_BITS = tuple(1 << b for b in range((2 * TM).bit_length() - 1, RUN_ALIGN.bit_length() - 2, -1))
MOE_MAX_TILES = -(-(2 * T + N_TILES * N_EXPERTS * (RUN_ALIGN - 1) + N_EXPERTS * (MOE_TM - RUN_ALIGN)) // MOE_TM)
MOE_ROWS = MOE_MAX_TILES * MOE_TM
SORT_GROUPS = SORT_ROWS // RUN_ALIGN
MOE_AHEAD = 4
MOE_IN_SLOTS = MOE_AHEAD + 1
MOE_OUT_SLOTS = 3
ROUTER_LANE0 = N_GROUPS
VMEM_LIMIT = 52 * 1024 * 1024


def _cparams(sem):
    return pltpu.CompilerParams(dimension_semantics=sem, vmem_limit_bytes=VMEM_LIMIT)


def _tile_cond_row(i):
    return jnp.where(i < PROMPT_TILES, 0, 1 + (i - PROMPT_TILES) // SAMPLE_TILES_PER_SEQ)


def _mod_spec(li, k):
    return pl.BlockSpec((1, 1, D), lambda i, *_: ((li * MOD_ROWS + _tile_cond_row(i)) * 6 + k, 0, 0))


def _row_spec(shape):
    return pl.BlockSpec(shape, lambda i, *_: (0,) * len(shape))


def _resident_f32_weight(shape):
    return pl.BlockSpec(shape, lambda i, *_: (0,) * len(shape), pipeline_mode=pl.Buffered(1))


def _cast_weight_once(w_ref, w_bf):
    @pl.when(pl.program_id(0) == 0)
    def _():
        w_bf[...] = w_ref[...].astype(BF16)


def _layer_norm(x, g, b):
    mu = jnp.mean(x, axis=-1, keepdims=True)
    xc = x - mu
    var = jnp.mean(xc * xc, axis=-1, keepdims=True)
    return xc * lax.rsqrt(var + LN_EPS) * g + b


def _silu(x):
    return x * jax.nn.sigmoid(x)


def _dot(a, b):
    return jnp.dot(a, b, preferred_element_type=F32)


def _dot_nt(a, b):
    return lax.dot_general(a, b, (((1,), (1,)), ((), ())), preferred_element_type=F32)


def _dot_tn(a, b):
    return lax.dot_general(a, b, (((0,), (0,)), ((), ())), preferred_element_type=F32)


MOD_TN = 1024
MOD_USED_ROWS = 1 + DEC_BATCH


def _mod_kernel(cond_t_ref, w_ref, b_ref, o_ref):
    s = _silu(cond_t_ref[...])
    w = w_ref[0]
    o_ref[0] = jnp.zeros((MOD_ROWS, MOD_TN), F32) + b_ref[0]
    for r in range(MOD_USED_ROWS):
        o_ref[0, r:r + 1, :] = jnp.sum(w * s[:, r:r + 1], axis=0, keepdims=True) + b_ref[0]


def _mod_vectors(cond, mod_w, mod_b):
    return pl.pallas_call(
        _mod_kernel,
        grid=(DEPTH, 6 * D // MOD_TN),
        in_specs=[
            pl.BlockSpec((D, MOD_ROWS), lambda l, j: (0, 0)),
            pl.BlockSpec((1, D, MOD_TN), lambda l, j: (l, 0, j)),
            pl.BlockSpec((1, 1, MOD_TN), lambda l, j: (l, 0, j)),
        ],
        out_specs=pl.BlockSpec((1, MOD_ROWS, MOD_TN), lambda l, j: (l, 0, j)),
        out_shape=jax.ShapeDtypeStruct((DEPTH, MOD_ROWS, 6 * D), F32),
        compiler_params=_cparams(("arbitrary", "arbitrary")),
        name="mod_vectors",
    )(cond.T, mod_w, mod_b.reshape(DEPTH, 1, 6 * D))


def _prompt_tile_spec(width):
    return pl.BlockSpec((TM, width), lambda i, *_: (jnp.minimum(i, PROMPT_TILES - 1), 0))


def _sample_tile_spec(width):
    return pl.BlockSpec((TM, width), lambda i, *_: (jnp.maximum(i - PROMPT_TILES, 0), 0))


def _pick_tile(prompt_ref, sample_ref):
    return jnp.where(pl.program_id(0) < PROMPT_TILES, prompt_ref[...], sample_ref[...])


def _in_proj_kernel(xp_ref, xs_ref, sh_ref, sc_ref, w_ref, o_ref, ck_ref, cv_ref, w_bf):
    _cast_weight_once(w_ref, w_bf)
    u = _pick_tile(xp_ref, xs_ref) * (1.0 + sc_ref[0]) + sh_ref[0]
    proj = _dot(u.astype(BF16), w_bf[...])
    o_ref[...] = proj.astype(BF16)

    @pl.when(pl.program_id(0) < PROMPT_TILES)
    def _():
        for h in range(HEADS):
            ck_ref[0, 0, h] = proj[:, (COL_KD + h) * HEAD_W:(COL_KD + h + 1) * HEAD_W]
            cv_ref[0, 0, h] = proj[:, (COL_VD + h) * HEAD_W:(COL_VD + h + 1) * HEAD_W]


def _in_proj(x_prompt, x_sample, modr, w_in, li):
    cache_spec = pl.BlockSpec((1, 1, HEADS, SEQ, HEAD_W),
                              lambda i: (jnp.minimum(i, PROMPT_TILES - 1), 0, 0, 0, 0))
    cache_shape = jax.ShapeDtypeStruct((BATCH, 1, HEADS, SEQ, HEAD_W), F32)
    return pl.pallas_call(
        _in_proj_kernel,
        grid=(N_TILES,),
        in_specs=[
            _prompt_tile_spec(D), _sample_tile_spec(D),
            _mod_spec(li, 0),
            _mod_spec(li, 1),
            _resident_f32_weight((D, IN_W)),
        ],
        out_specs=[pl.BlockSpec((TM, IN_W), lambda i: (i, 0)), cache_spec, cache_spec],
        out_shape=[jax.ShapeDtypeStruct((T, IN_W), BF16), cache_shape, cache_shape],
        scratch_shapes=[pltpu.VMEM((D, IN_W), BF16)],
        compiler_params=_cparams(("arbitrary",)),
        name="in_proj",
    )(x_prompt, x_sample, modr, modr, w_in)


COL_QR, COL_KR, COL_VR, COL_GR, COL_QD, COL_KD, COL_VD = (k * HEADS for k in range(7))


def _retention_kernel(dec_ref, q_ref, k_ref, v_ref, g_ref, *rest, n_chunks, n_heads, has_state, emit_state):
    rest = list(rest)
    if has_state:
        s0f_ref, s0b_ref = rest[:2]
        rest = rest[2:]
    r_ref = rest[0]
    rest = rest[1:]
    if emit_state:
        sf_ref, sb_ref = rest[:2]
        rest = rest[2:]
    of_ref = rest[0]

    head0 = pl.program_id(1) * n_heads
    C = RET_CHUNK
    ii = lax.broadcasted_iota(jnp.int32, (C, C), 0)
    jj = lax.broadcasted_iota(jnp.int32, (C, C), 1)
    rel = (ii - jj).astype(F32)
    idx = lax.broadcasted_iota(jnp.int32, (C, 1), 0).astype(F32)
    k_scale = HEAD_W ** -0.5

    def chunk(ref, c, h):
        return ref[c * C:(c + 1) * C, h * HEAD_W:(h + 1) * HEAD_W].astype(F32)

    def decays(direction, h):
        lg = -jnp.exp(jnp.full((1, 1), dec_ref[direction * HEADS + head0 + h], F32))
        if direction == 0:
            inner = jnp.where(rel >= 0, jnp.exp(jnp.maximum(rel, 0.0) * lg), 0.0)
            return inner, jnp.exp((idx + 1.0) * lg), jnp.exp((C - 1.0 - idx) * lg), jnp.exp(C * lg)
        inner = jnp.where(rel <= 0, jnp.exp(jnp.maximum(-rel, 0.0) * lg), 0.0)
        return inner, jnp.exp((C - idx) * lg), jnp.exp(idx * lg), jnp.exp(C * lg)

    def run(direction):
        dec = [decays(direction, h) for h in range(n_heads)]
        if has_state:
            s0_ref = s0f_ref if direction == 0 else s0b_ref
            states = [s0_ref[0, 0, h] for h in range(n_heads)]
        else:
            states = [jnp.zeros((HEAD_W, HEAD_W), F32) for _ in range(n_heads)]
        order = range(n_chunks) if direction == 0 else range(n_chunks - 1, -1, -1)
        for c in order:
            rows = slice(c * C, (c + 1) * C)
            for h in range(n_heads):
                inner, q_decay, k_decay, chunk_decay = dec[h]
                cols = slice(h * HEAD_W, (h + 1) * HEAD_W)
                s = states[h]
                qc = chunk(q_ref, c, h)
                kc = chunk(k_ref, c, h) * k_scale
                vc = chunk(v_ref, c, h).astype(BF16)
                scores = _dot_nt(qc.astype(BF16), kc.astype(BF16)) * inner
                o = _dot(scores.astype(BF16), vc) + _dot((qc * q_decay).astype(BF16), s.astype(BF16))
                states[h] = s * chunk_decay + _dot_tn((kc * k_decay).astype(BF16), vc)
                if direction == 0:
                    of_ref[rows, cols] = o
                else:
                    r = of_ref[rows, cols] + o
                    mu = jnp.mean(r, axis=-1, keepdims=True)
                    rc = r - mu
                    var = jnp.mean(rc * rc, axis=-1, keepdims=True)
                    rn = rc * lax.rsqrt(var + GN_EPS)
                    r_ref[rows, cols] = _silu(chunk(g_ref, c, h)) * rn
        return states

    sf = run(0)
    sb = run(1)
    if emit_state:
        for h in range(n_heads):
            sf_ref[0, 0, h] = sf[h]
            sb_ref[0, 0, h] = sb[h]


def _retention(proj, dec, n_seq, seq_len, row_block0, n_heads, s0f=None, s0b=None, e=0, emit_state=False):
    has_state = s0f is not None
    width = n_heads * HEAD_W

    def col(base):
        return pl.BlockSpec((seq_len, width), lambda b, h, *_: (row_block0 + b, base // n_heads + h))

    state_spec = pl.BlockSpec((1, 1, n_heads, HEAD_W, HEAD_W), lambda b, h, *_: (b, e, h, 0, 0))
    in_specs = [pl.BlockSpec(memory_space=pltpu.SMEM), col(COL_QR), col(COL_KR), col(COL_VR), col(COL_GR)]
    args = [dec, proj, proj, proj, proj]
    if has_state:
        in_specs += [state_spec, state_spec]
        args += [s0f, s0b]
    out_specs = [pl.BlockSpec((seq_len, width), lambda b, h, *_: (b, h))]
    out_shape = [jax.ShapeDtypeStruct((n_seq * seq_len, HEADS * HEAD_W), F32)]
    if emit_state:
        st = pl.BlockSpec((1, 1, n_heads, HEAD_W, HEAD_W), lambda b, h, *_: (b, 0, h, 0, 0))
        out_specs += [st, st]
        out_shape += [jax.ShapeDtypeStruct((n_seq, 1, HEADS, HEAD_W, HEAD_W), F32)] * 2
    return pl.pallas_call(
        functools.partial(_retention_kernel, n_chunks=seq_len // RET_CHUNK, n_heads=n_heads,
                          has_state=has_state, emit_state=emit_state),
        grid=(n_seq, HEADS // n_heads),
        in_specs=in_specs,
        out_specs=out_specs,
        out_shape=out_shape,
        scratch_shapes=[pltpu.VMEM((seq_len, width), F32)],
        compiler_params=_cparams(("arbitrary", "arbitrary")),
        name=f"retention_{seq_len}",
    )(*args)


def _diff_lambda(lq1_ref, lk1_ref, lq2_ref, lk2_ref, lam_init):
    a = jnp.sum(lq1_ref[...] * lk1_ref[...], axis=-1, keepdims=True)
    b = jnp.sum(lq2_ref[...] * lk2_ref[...], axis=-1, keepdims=True)
    return jnp.exp(a) - jnp.exp(b) + lam_init


LOG2E = 1.4426950408889634


def _diff_attend(q, k, v, lam, subln_g, lam_init):
    lane = lax.broadcasted_iota(jnp.int32, q.shape, 1)
    q1 = jnp.where(lane < DIFF_DK, q, 0.0).astype(BF16)
    q2 = jnp.where(lane >= DIFF_DK, q, 0.0).astype(BF16)

    def softmax_times_v(qz):
        s = _dot_nt(qz, k)
        p = jnp.exp2(s - jnp.max(s, axis=-1, keepdims=True))
        return _dot(p.astype(BF16), v) * (1.0 / jnp.sum(p, axis=-1, keepdims=True))

    o = softmax_times_v(q1) - lam * softmax_times_v(q2)
    o = o * lax.rsqrt(jnp.mean(o * o, axis=-1, keepdims=True) + LN_EPS)
    return o * subln_g * (1.0 - lam_init)


def _attn_prompt_kernel(q_ref, k_ref, v_ref, lq1, lk1, lq2, lk2, g_ref, o_ref, *, lam_init):
    lam = _diff_lambda(lq1, lk1, lq2, lk2, lam_init)
    scale = DIFF_DK ** -0.5 * LOG2E
    for h in range(HEADS):
        sl = slice(h * HEAD_W, (h + 1) * HEAD_W)
        o_ref[:, sl] = _diff_attend(q_ref[:, sl].astype(F32) * scale, k_ref[:, sl], v_ref[:, sl],
                                    lam, g_ref[...], lam_init)


def _attn_prompt(proj, lq1, lk1, lq2, lk2, subln_g, lam_init):
    W = HEADS * HEAD_W

    def slab(base):
        return pl.BlockSpec((SEQ, W), lambda b: (b, base // HEADS))

    small = _row_spec((1, DIFF_DK))
    return pl.pallas_call(
        functools.partial(_attn_prompt_kernel, lam_init=lam_init),
        grid=(BATCH,),
        in_specs=[slab(COL_QD), slab(COL_KD), slab(COL_VD), small, small, small, small,
                  _row_spec((1, HEAD_W))],
        out_specs=pl.BlockSpec((SEQ, W), lambda b: (b, 0)),
        out_shape=jax.ShapeDtypeStruct((T_PROMPT, W), F32),
        compiler_params=_cparams(("arbitrary",)),
        name="diff_attn_prompt",
    )(proj, proj, proj, lq1, lk1, lq2, lk2, subln_g)


def _rope(x, cos, sin_signed):
    lane = lax.broadcasted_iota(jnp.int32, x.shape, 1)
    partner = jnp.where((lane % 32) < 16, pltpu.roll(x, LANES - 16, 1), pltpu.roll(x, 16, 1))
    return x * cos + partner * sin_signed


def _attn_sample_kernel(q_ref, k_ref, v_ref, ck_ref, cv_ref, cosq_ref, sinq_ref, cos_ref, sin_ref,
                        lq1, lk1, lq2, lk2, g_ref, o_ref, kbuf, vbuf, *, lam_init):
    @pl.when(pl.program_id(2) == 0)
    def _():
        kbuf[0:DEC_SEQ, :] = _rope(k_ref[...].astype(F32), cos_ref[...], sin_ref[...]).astype(BF16)
        kbuf[DEC_SEQ:, :] = ck_ref[0, 0, 0].astype(BF16)
        vbuf[0:DEC_SEQ, :] = v_ref[...]
        vbuf[DEC_SEQ:, :] = cv_ref[0, 0, 0].astype(BF16)

    lam = _diff_lambda(lq1, lk1, lq2, lk2, lam_init)
    q = _rope(q_ref[...].astype(F32), cosq_ref[...], sinq_ref[...]) * (DIFF_DK ** -0.5 * LOG2E)
    o_ref[...] = _diff_attend(q, kbuf[...], vbuf[...], lam, g_ref[...], lam_init)


ATTN_TQ = 256


def _attn_sample(proj, cache_k, cache_v, cos, sin_signed, lq1, lk1, lq2, lk2, subln_g, lam_init, e):
    nq = DEC_SEQ // ATTN_TQ
    row0_q = T_PROMPT // ATTN_TQ
    row0_kv = T_PROMPT // DEC_SEQ
    small = pl.BlockSpec((1, DIFF_DK), lambda b, h, t: (0, 0))
    cache = pl.BlockSpec((1, 1, 1, PAST_LEN, HEAD_W), lambda b, h, t: (b, e, h, 0, 0))
    table_q = pl.BlockSpec((ATTN_TQ, HEAD_W), lambda b, h, t: (t, 0))
    table = pl.BlockSpec((DEC_SEQ, HEAD_W), lambda b, h, t: (0, 0))
    return pl.pallas_call(
        functools.partial(_attn_sample_kernel, lam_init=lam_init),
        grid=(DEC_BATCH, HEADS, nq),
        in_specs=[
            pl.BlockSpec((ATTN_TQ, HEAD_W), lambda b, h, t: (row0_q + b * nq + t, COL_QD + h)),
            pl.BlockSpec((DEC_SEQ, HEAD_W), lambda b, h, t: (row0_kv + b, COL_KD + h)),
            pl.BlockSpec((DEC_SEQ, HEAD_W), lambda b, h, t: (row0_kv + b, COL_VD + h)),
            cache, cache, table_q, table_q, table, table,
            small, small, small, small,
            pl.BlockSpec((1, HEAD_W), lambda b, h, t: (0, 0)),
        ],
        out_specs=pl.BlockSpec((ATTN_TQ, HEAD_W), lambda b, h, t: (b * nq + t, h)),
        out_shape=jax.ShapeDtypeStruct((T_SAMPLE, HEADS * HEAD_W), F32),
        scratch_shapes=[pltpu.VMEM((DEC_SEQ + PAST_LEN, HEAD_W), BF16),
                        pltpu.VMEM((DEC_SEQ + PAST_LEN, HEAD_W), BF16)],
        compiler_params=_cparams(("arbitrary", "arbitrary", "arbitrary")),
        name="diff_attn_sample",
    )(proj, proj, proj, cache_k, cache_v, cos, sin_signed, cos, sin_signed,
      lq1, lk1, lq2, lk2, subln_g)


def _rope_tables():
    t = np.arange(DEC_SEQ)
    row, colp = t // GRID_W, t % GRID_W
    lane = np.arange(LANES)
    pos = np.where(((lane // 32) % 2 == 0)[None, :], row[:, None], colp[:, None]).astype(np.float64)
    half = 16
    inv = (np.float32(ROPE_THETA) ** (-(np.arange(half, dtype=np.float32)) / np.float32(half))).astype(np.float32)
    ang = pos.astype(np.float32) * inv[lane % half][None, :]
    cos = np.cos(ang.astype(np.float64)).astype(np.float32)
    sin = np.sin(ang.astype(np.float64)).astype(np.float32)
    sign = np.where((lane % 32) < half, -1.0, 1.0).astype(np.float32)[None, :]
    return jnp.asarray(cos), jnp.asarray(sin * sign)


def _split_bf16(a):
    hi = a.astype(BF16)
    return hi, (a - hi.astype(F32)).astype(BF16)


def _mixer_tail(out, x, g1_ref, sc2_ref, sh2_ref, lng_ref, lnb_ref, wr_ref, br_ref,
                x1_ref, u2_ref, meta_ref, cnt_ref):
    x1 = _layer_norm(ALPHA * x + g1_ref[0] * out, lng_ref[...], lnb_ref[...])
    x1_ref[...] = x1
    u2 = x1 * (1.0 + sc2_ref[0]) + sh2_ref[0]
    u2_ref[...] = u2.astype(BF16)

    u_hi, u_lo = _split_bf16(u2)
    w_hi, w_lo = _split_bf16(wr_ref[...])
    logits = _dot(u_hi, w_hi) + (_dot(u_hi, w_lo) + _dot(u_lo, w_hi)) + br_ref[...]
    lane = lax.broadcasted_iota(jnp.int32, logits.shape, 1).astype(F32)
    neg = jnp.float32(-jnp.inf)
    is_g = lane < N_GROUPS
    gl = jnp.where(is_g, logits, neg)
    gmax = jnp.max(gl, axis=-1, keepdims=True)
    gsel = jnp.min(jnp.where(gl == gmax, lane, float(LANES)), axis=-1, keepdims=True)
    p_g = 1.0 / jnp.sum(jnp.where(is_g, jnp.exp(gl - gmax), 0.0), axis=-1, keepdims=True)
    lo = ROUTER_LANE0 + gsel * EXPERTS_PER_GROUP
    el = jnp.where((lane >= lo) & (lane < lo + EXPERTS_PER_GROUP), logits, neg)
    v1 = jnp.max(el, axis=-1, keepdims=True)
    i1 = jnp.min(jnp.where(el == v1, lane, float(LANES)), axis=-1, keepdims=True)
    el2 = jnp.where(lane == i1, neg, el)
    v2 = jnp.max(el2, axis=-1, keepdims=True)
    i2 = jnp.min(jnp.where(el2 == v2, lane, float(LANES)), axis=-1, keepdims=True)
    t = jnp.exp(v2 - v1)
    w1 = p_g / (1.0 + t)
    w2 = w1 * t

    oh1 = (lane == i1).astype(F32)
    oh2 = (lane == i2).astype(F32)
    oh = oh1 + oh2
    r_i = lax.broadcasted_iota(jnp.int32, (TM, TM), 0)
    c_i = lax.broadcasted_iota(jnp.int32, (TM, TM), 1)
    before = (c_i < r_i).astype(BF16)
    earlier = _dot(before, oh.astype(BF16))
    rank1 = jnp.sum(earlier * oh1, axis=-1, keepdims=True)
    rank2 = jnp.sum(earlier * oh2, axis=-1, keepdims=True)
    cnt_ref[0] = jnp.sum(oh, axis=0, keepdims=True)
    cols = (i1, i2, w1, w2, rank1, rank2)
    meta = jnp.zeros_like(logits)
    for k, col in enumerate(cols):
        meta = jnp.where(lane == k, col, meta)
    meta_ref[...] = meta


META_E1, META_E2, META_W1, META_W2, META_RANK1, META_RANK2 = range(6)

_TAIL_OUT_SHAPES = [
    jax.ShapeDtypeStruct((T, D), F32),
    jax.ShapeDtypeStruct((T, D), BF16),
    jax.ShapeDtypeStruct((T, LANES), F32),
    jax.ShapeDtypeStruct((N_TILES, 1, LANES), F32),
]


def _tail_out_specs():
    return [
        pl.BlockSpec((TM, D), lambda i: (i, 0)),
        pl.BlockSpec((TM, D), lambda i: (i, 0)),
        pl.BlockSpec((TM, LANES), lambda i: (i, 0)),
        pl.BlockSpec((1, 1, LANES), lambda i: (i, 0, 0)),
    ]


def _tail_in_specs(li):
    return [
        _mod_spec(li, 2), _mod_spec(li, 4), _mod_spec(li, 3),
        _row_spec((1, D)), _row_spec((1, D)),
        _row_spec((D, LANES)), _row_spec((1, LANES)),
    ]


def _out_proj_kernel(rp_ref, rs_ref, op_ref, os_ref, w_ref, xp_ref, xs_ref, *rest):
    tail_args, w_bf = rest[:-1], rest[-1]
    _cast_weight_once(w_ref, w_bf)
    half = HEADS * HEAD_W
    r = _pick_tile(rp_ref, rs_ref).astype(BF16)
    o = _pick_tile(op_ref, os_ref).astype(BF16)
    out = _dot(r, w_bf[0:half, :]) + _dot(o, w_bf[half:, :])
    _mixer_tail(out, _pick_tile(xp_ref, xs_ref), *tail_args)


def _out_proj_tail(r_p, r_s, o_p, o_s, w_out, x_prompt, x_sample, modr, ln_g, ln_b, wr, br, li):
    half = HEADS * HEAD_W
    return pl.pallas_call(
        _out_proj_kernel,
        grid=(N_TILES,),
        in_specs=[_prompt_tile_spec(half), _sample_tile_spec(half),
                  _prompt_tile_spec(half), _sample_tile_spec(half),
                  _resident_f32_weight((2 * half, D)),
                  _prompt_tile_spec(D), _sample_tile_spec(D)] + _tail_in_specs(li),
        out_specs=_tail_out_specs(),
        out_shape=_TAIL_OUT_SHAPES,
        scratch_shapes=[pltpu.VMEM((2 * half, D), BF16)],
        compiler_params=_cparams(("arbitrary",)),
        name="out_proj_tail",
    )(r_p, r_s, o_p, o_s, w_out, x_prompt, x_sample, modr, modr, modr, ln_g, ln_b, wr, br)


def _conv_glu(y, sh_ref, sc_ref, w_bf, b_ref):
    u = y * (1.0 + sc_ref[0]) + sh_ref[0]
    h = _dot(u.astype(BF16), w_bf[...]) + b_ref[...]
    return h[:, :D] * jax.nn.sigmoid(h[:, D:])


HALO = 16
CONV_ROWS = 64
CONV_COLS = 128


def _depthwise_conv(hp, dw_ref, conv):
    base = HALO - CONV_PAD
    for cb in range(D // CONV_COLS):
        cs = slice(cb * CONV_COLS, (cb + 1) * CONV_COLS)
        for rb in range(TM // CONV_ROWS):
            r0 = rb * CONV_ROWS
            acc = None
            for shift in range(SUBLANES):
                part = None
                for tap in range(CONV_K):
                    off = base + tap
                    if off % SUBLANES != shift:
                        continue
                    a0 = r0 + off - shift
                    term = hp[a0:a0 + CONV_ROWS + SUBLANES, cs] * dw_ref[tap:tap + 1, cs]
                    part = term if part is None else part + term
                part = part[shift:shift + CONV_ROWS, :]
                acc = part if acc is None else acc + part
            conv[r0:r0 + CONV_ROWS, cs] = acc


def _conv_tail_kernel(cur_ref, prev_ref, next_ref, dw_ref, dwb_ref, cg_ref, cb_ref, w2_ref, b2_ref,
                      x_ref, *rest):
    tail_args, (hp, conv, w2_bf) = rest[:-3], rest[-3:]
    _cast_weight_once(w2_ref, w2_bf)
    i = pl.program_id(0)
    k = (i - PROMPT_TILES) % SAMPLE_TILES_PER_SEQ
    in_sample = i >= PROMPT_TILES
    left_ok = jnp.logical_and(in_sample, k != 0)
    right_ok = jnp.logical_and(in_sample, k != SAMPLE_TILES_PER_SEQ - 1)
    hp[0:HALO, :] = jnp.where(left_ok, prev_ref[...], 0.0)
    hp[HALO:HALO + TM, :] = cur_ref[...]
    hp[HALO + TM:HALO + TM + HALO, :] = jnp.where(right_ok, next_ref[...], 0.0)
    _depthwise_conv(hp, dw_ref, conv)
    hc = _silu(_layer_norm(conv[...] + dwb_ref[...], cg_ref[...], cb_ref[...]))
    out = _dot(hc.astype(BF16), w2_bf[...]) + b2_ref[...]
    _mixer_tail(out, x_ref[...], *tail_args)


def _conv_tail(glu, dw, dwb, cg, cb, w2, b2, x, modr, ln_g, ln_b, wr, br, li):
    per = TM // HALO
    last = T // HALO - 1
    return pl.pallas_call(
        _conv_tail_kernel,
        grid=(N_TILES,),
        in_specs=[pl.BlockSpec((TM, D), lambda i: (i, 0)),
                  pl.BlockSpec((HALO, D), lambda i: (jnp.maximum(i * per - 1, 0), 0)),
                  pl.BlockSpec((HALO, D), lambda i: (jnp.minimum((i + 1) * per, last), 0)),
                  _row_spec((CONV_K, D)), _row_spec((1, D)), _row_spec((1, D)), _row_spec((1, D)),
                  _resident_f32_weight((D, D)), _row_spec((1, D)),
                  pl.BlockSpec((TM, D), lambda i: (i, 0))] + _tail_in_specs(li),
        out_specs=_tail_out_specs(),
        out_shape=_TAIL_OUT_SHAPES,
        scratch_shapes=[pltpu.VMEM((TM + 2 * HALO, D), F32), pltpu.VMEM((TM, D), F32),
                        pltpu.VMEM((D, D), BF16)],
        compiler_params=_cparams(("arbitrary",)),
        name="conv_tail",
    )(glu, glu, glu, dw, dwb, cg, cb, w2, b2, x, modr, modr, modr, ln_g, ln_b, wr, br)


def _sorted_positions(meta, srcv):
    lane = lax.broadcasted_iota(jnp.int32, meta.shape, 1).astype(F32)

    def pos(e_col, r_col):
        start = jnp.sum(jnp.where(lane == meta[:, e_col:e_col + 1], srcv, 0.0), axis=-1, keepdims=True)
        return start + meta[:, r_col:r_col + 1]

    return pos(META_E1, META_RANK1), pos(META_E2, META_RANK2)


def _one_hot_rows(pos):
    col = lax.broadcasted_iota(jnp.int32, (TM, SORT_ROWS), 1).astype(F32)
    return col == pos


def _for_each_row_group(tile, tot_ref, dstg_ref, fn):
    def body(g, carry):
        fn(pl.multiple_of(g * RUN_ALIGN, RUN_ALIGN),
           pl.multiple_of(dstg_ref[tile * SORT_GROUPS + g], RUN_ALIGN))
        return carry

    lax.fori_loop(0, tot_ref[tile] // RUN_ALIGN, body, 0)


def _wait_rows(total, make_copy):
    for bit in RUN_BITS:
        @pl.when((total & bit) != 0)
        def _(bit=bit):
            make_copy(bit).wait()


def _dispatch_kernel(tot_ref, dstg_ref, u_ref, meta_ref, srcv_ref, xs_ref, sorted_ref, sems):
    i = pl.program_id(0)
    slot = i % 2

    def wait_tile(tile, slot):
        buf = sorted_ref.at[slot]
        _wait_rows(tot_ref[tile], lambda rows: pltpu.make_async_copy(
            buf.at[pl.ds(0, rows)], xs_ref.at[pl.ds(0, rows)], sems.at[slot]))

    @pl.when(i >= 2)
    def _():
        wait_tile(i - 2, slot)

    pos1, pos2 = _sorted_positions(meta_ref[...], srcv_ref[0])
    select = jnp.logical_or(_one_hot_rows(pos1), _one_hot_rows(pos2)).astype(BF16)
    sorted_ref[slot] = _dot_tn(select, u_ref[...])
    buf = sorted_ref.at[slot]

    def start(src, dst):
        pltpu.make_async_copy(buf.at[pl.ds(src, RUN_ALIGN)], xs_ref.at[pl.ds(dst, RUN_ALIGN)],
                              sems.at[slot]).start()

    _for_each_row_group(i, tot_ref, dstg_ref, start)

    @pl.when(i == N_TILES - 1)
    def _():
        wait_tile(i - 1, 1 - slot)
        wait_tile(i, slot)


def _dispatch(sched, u2, meta, srcv):
    return pl.pallas_call(
        _dispatch_kernel,
        grid_spec=pltpu.PrefetchScalarGridSpec(
            num_scalar_prefetch=2,
            grid=(N_TILES,),
            in_specs=[pl.BlockSpec((TM, D), lambda i, *_: (i, 0)),
                      pl.BlockSpec((TM, LANES), lambda i, *_: (i, 0)),
                      pl.BlockSpec((1, 1, LANES), lambda i, *_: (i, 0, 0))],
            out_specs=pl.BlockSpec(memory_space=pl.ANY),
            scratch_shapes=[pltpu.VMEM((2, SORT_ROWS, D), F32), pltpu.SemaphoreType.DMA((2,))],
        ),
        out_shape=jax.ShapeDtypeStruct((MOE_ROWS, D), F32),
        compiler_params=_cparams(("arbitrary",)),
        name="moe_dispatch",
    )(*sched, u2, meta, srcv)


def _experts_kernel(start_ref, chunks_ref, xs_ref, wg_ref, wu_ref, wd_ref, ys_ref,
                    wg_bf, wu_bf, wd_bf, xbuf, ybuf, in_sems, out_sems):
    e = pl.program_id(0)
    n = chunks_ref[e]
    first = start_ref[e] // MOE_TM
    total = start_ref[N_EXPERTS - 1] // MOE_TM + chunks_ref[N_EXPERTS - 1]

    def rows(g):
        return pl.ds(pl.multiple_of(g * MOE_TM, MOE_TM), MOE_TM)

    def load(g):
        slot = g % MOE_IN_SLOTS
        return pltpu.make_async_copy(xs_ref.at[rows(g)], xbuf.at[slot], in_sems.at[slot])

    def store(g):
        slot = g % MOE_OUT_SLOTS
        return pltpu.make_async_copy(ybuf.at[slot], ys_ref.at[rows(g)], out_sems.at[slot])

    @pl.when(e == 0)
    def _():
        for g in range(MOE_AHEAD):
            @pl.when(g < total)
            def _(g=g):
                load(g).start()

    @pl.when(n > 0)
    def _():
        wg_bf[...] = wg_ref[0, 0].astype(BF16)
        wu_bf[...] = wu_ref[0, 0].astype(BF16)
        wd_bf[...] = wd_ref[0, 0].astype(BF16)

        def tile(g, carry):
            load(g).wait()

            @pl.when(g + MOE_AHEAD < total)
            def _():
                load(g + MOE_AHEAD).start()

            @pl.when(g >= MOE_OUT_SLOTS)
            def _():
                store(g - MOE_OUT_SLOTS).wait()

            x = xbuf[g % MOE_IN_SLOTS].astype(BF16)
            h = (_silu(_dot(x, wg_bf[...])) * _dot(x, wu_bf[...])).astype(BF16)
            ybuf[g % MOE_OUT_SLOTS] = _dot(h, wd_bf[...])
            store(g).start()
            return carry

        lax.fori_loop(first, first + n, tile, 0)

    @pl.when(e == N_EXPERTS - 1)
    def _():
        for back in range(MOE_OUT_SLOTS, 0, -1):
            @pl.when(total >= back)
            def _(back=back):
                store(total - back).wait()


def _experts(seg_start, seg_chunks, xs, w_gate, w_up, w_down, li):
    def weight(shape):
        return pl.BlockSpec((1, 1) + shape, lambda e, *_: (li, e, 0, 0))

    return pl.pallas_call(
        _experts_kernel,
        grid_spec=pltpu.PrefetchScalarGridSpec(
            num_scalar_prefetch=2,
            grid=(N_EXPERTS,),
            in_specs=[pl.BlockSpec(memory_space=pl.ANY),
                      weight((D, D_EXPERT)), weight((D, D_EXPERT)), weight((D_EXPERT, D))],
            out_specs=pl.BlockSpec(memory_space=pl.ANY),
            scratch_shapes=[pltpu.VMEM((D, D_EXPERT), BF16), pltpu.VMEM((D, D_EXPERT), BF16),
                            pltpu.VMEM((D_EXPERT, D), BF16),
                            pltpu.VMEM((MOE_IN_SLOTS, MOE_TM, D), F32),
                            pltpu.VMEM((MOE_OUT_SLOTS, MOE_TM, D), F32),
                            pltpu.SemaphoreType.DMA((MOE_IN_SLOTS,)),
                            pltpu.SemaphoreType.DMA((MOE_OUT_SLOTS,))],
        ),
        out_shape=jax.ShapeDtypeStruct((MOE_ROWS, D), F32),
        compiler_params=_cparams(("arbitrary",)),
        name="moe_experts",
    )(seg_start, seg_chunks, xs, w_gate, w_up, w_down)


def _combine_kernel(tot_ref, dstg_ref, ys_ref, x1_ref, meta_ref, srcv_ref, g2_ref,
                    lng_ref, lnb_ref, *rest, split, feeds_conv):
    if feeds_conv:
        (sh_ref, sc_ref, w1_ref, b1_ref), rest, w1_bf = rest[:4], rest[4:-1], rest[-1]
        _cast_weight_once(w1_ref, w1_bf)
    outs, (sorted_ref, sems) = rest[:-2], rest[-2:]
    i = pl.program_id(0)
    slot = i % 2

    def fetch(tile, slot):
        buf = sorted_ref.at[slot]

        def start(src, dst):
            pltpu.make_async_copy(ys_ref.at[pl.ds(dst, RUN_ALIGN)], buf.at[pl.ds(src, RUN_ALIGN)],
                                  sems.at[slot]).start()

        _for_each_row_group(tile, tot_ref, dstg_ref, start)

    @pl.when(i == 0)
    def _():
        sorted_ref[...] = jnp.zeros_like(sorted_ref)
        fetch(0, 0)

    @pl.when(i + 1 < N_TILES)
    def _():
        fetch(i + 1, 1 - slot)

    meta = meta_ref[...]
    pos1, pos2 = _sorted_positions(meta, srcv_ref[0])
    sel1 = _one_hot_rows(pos1).astype(BF16)
    sel2 = _one_hot_rows(pos2).astype(BF16)
    buf = sorted_ref.at[slot]
    _wait_rows(tot_ref[i], lambda rows: pltpu.make_async_copy(
        ys_ref.at[pl.ds(0, rows)], buf.at[pl.ds(0, rows)], sems.at[slot]))
    ysort = sorted_ref[slot].astype(BF16)
    f = (meta[:, META_W1:META_W1 + 1] * _dot(sel1, ysort)
         + meta[:, META_W2:META_W2 + 1] * _dot(sel2, ysort))
    y = _layer_norm(ALPHA * x1_ref[...] + g2_ref[0] * f, lng_ref[...], lnb_ref[...])
    if split:
        @pl.when(i < PROMPT_TILES)
        def _():
            outs[0][...] = y

        @pl.when(i >= PROMPT_TILES)
        def _():
            outs[1][...] = y
    else:
        outs[0][...] = y
    if feeds_conv:
        outs[-1][...] = _conv_glu(y, sh_ref, sc_ref, w1_bf, b1_ref)


def _combine(sched, ys, x1, meta, srcv, modr, ln_g, ln_b, li, split, conv_w1=None, conv_b1=None):
    feeds_conv = conv_w1 is not None
    tile = pl.BlockSpec((TM, D), lambda i, *_: (i, 0))
    if split:
        out_specs = [_prompt_tile_spec(D), _sample_tile_spec(D)]
        out_shape = [jax.ShapeDtypeStruct((T_PROMPT, D), F32), jax.ShapeDtypeStruct((T_SAMPLE, D), F32)]
    else:
        out_specs = [tile]
        out_shape = [jax.ShapeDtypeStruct((T, D), F32)]
    extra_specs, extra_args, extra_scratch = [], [], []
    if feeds_conv:
        extra_specs = [_mod_spec(li + 1, 0), _mod_spec(li + 1, 1),
                       _resident_f32_weight((D, 2 * D)), _row_spec((1, 2 * D))]
        extra_args = [modr, modr, conv_w1, conv_b1]
        extra_scratch = [pltpu.VMEM((D, 2 * D), BF16)]
        out_specs = out_specs + [tile]
        out_shape = out_shape + [jax.ShapeDtypeStruct((T, D), F32)]
    return pl.pallas_call(
        functools.partial(_combine_kernel, split=split, feeds_conv=feeds_conv),
        grid_spec=pltpu.PrefetchScalarGridSpec(
            num_scalar_prefetch=2,
            grid=(N_TILES,),
            in_specs=[pl.BlockSpec(memory_space=pl.ANY),
                      pl.BlockSpec((TM, D), lambda i, *_: (i, 0)),
                      pl.BlockSpec((TM, LANES), lambda i, *_: (i, 0)),
                      pl.BlockSpec((1, 1, LANES), lambda i, *_: (i, 0, 0)),
                      _mod_spec(li, 5), _row_spec((1, D)), _row_spec((1, D))] + extra_specs,
            out_specs=out_specs,
            scratch_shapes=[pltpu.VMEM((2, SORT_ROWS, D), F32), pltpu.SemaphoreType.DMA((2,))] + extra_scratch,
        ),
        out_shape=out_shape,
        compiler_params=_cparams(("arbitrary",)),
        name="moe_combine",
    )(*sched, ys, x1, meta, srcv, modr, ln_g, ln_b, *extra_args)


def _moe_schedule(tile_counts):
    n = (tile_counts + RUN_ALIGN - 1) // RUN_ALIGN * RUN_ALIGN
    src = jnp.cumsum(n, axis=1) - n
    per_expert = jnp.sum(n, axis=0)
    seg = (per_expert + MOE_TM - 1) // MOE_TM * MOE_TM
    seg_start = jnp.cumsum(seg) - seg
    dst = seg_start[None, :] + jnp.cumsum(n, axis=0) - n
    g_row = jnp.arange(SORT_GROUPS, dtype=jnp.int32) * RUN_ALIGN
    in_run = jnp.logical_and(src[:, None, :] <= g_row[None, :, None],
                             g_row[None, :, None] < (src + n)[:, None, :])
    dst_g = g_row[None, :] + jnp.sum(jnp.where(in_run, (dst - src)[:, None, :], 0), axis=2)
    runs = (jnp.sum(n, axis=1), dst_g.reshape(-1))
    srcv = jnp.pad(src.astype(F32), ((0, 0), (ROUTER_LANE0, LANES - ROUTER_LANE0 - N_EXPERTS)))
    return runs, srcv.reshape(N_TILES, 1, LANES), (seg_start, seg // MOE_TM)


def _moe(x1, u2, meta, cnt, modr, w_gate, w_up, w_down, ln_g, ln_b, li, split, **next_conv):
    tile_counts = cnt[:, 0, ROUTER_LANE0:ROUTER_LANE0 + N_EXPERTS].astype(jnp.int32)
    runs, srcv, (seg_start, seg_chunks) = _moe_schedule(tile_counts)
    xs = _dispatch(runs, u2, meta, srcv)
    ys = _experts(seg_start, seg_chunks, xs, w_gate, w_up, w_down, li)
    return _combine(runs, ys, x1, meta, srcv, modr, ln_g, ln_b, li, split, **next_conv)


def _router_slab(wg, bg, we, be):
    w = jnp.concatenate([wg, we.transpose(1, 0, 2).reshape(D, N_EXPERTS)], axis=1)
    b = jnp.concatenate([bg, be.reshape(N_EXPERTS)])
    pad = LANES - w.shape[1]
    return jnp.pad(w, ((0, 0), (0, pad))), jnp.pad(b, (0, pad)).reshape(1, LANES)


def kernel(x_prompt, x_sample, cache_diff_k, cache_diff_v, state_ret_fwd, state_ret_bwd, c, c_ctx, mod_w, mod_b, ln1_g, ln1_b, ln2_g, ln2_b, mix_w_in, mix_w_out, ret_decay_fwd, ret_decay_bwd, diff_lq1, diff_lk1, diff_lq2, diff_lk2, diff_subln_g, conv_w1, conv_b1, conv_dw, conv_dw_b, conv_ln_g, conv_ln_b, conv_w2, conv_b2, router_g_w, router_g_b, router_e_w, router_e_b, moe_w_gate, moe_w_up, moe_w_down):
    xp = x_prompt.reshape(T_PROMPT, D)
    xs = x_sample.reshape(T_SAMPLE, D)
    cond = jnp.concatenate([c_ctx[None, :], c, jnp.zeros((MOD_ROWS - 1 - DEC_BATCH, D), F32)], axis=0)
    modr = _mod_vectors(cond, mod_w, mod_b).reshape(DEPTH * MOD_ROWS * 6, 1, D)
    cos, sin_signed = _rope_tables()

    def row(v):
        return v.reshape(1, -1)

    x = None
    caches = None
    for li in range(DEPTH):
        wr, br = _router_slab(router_g_w[li], router_g_b[li], router_e_w[li], router_e_b[li])
        if li % 2 == 0:
            assert li == 0, "the even mixer reads the kernel inputs directly"
            e = li // 2
            lam_init = 0.8 - 0.6 * math.exp(-0.3 * li)
            proj, ck, cv = _in_proj(xp, xs, modr, mix_w_in[e], li)
            dec = jnp.concatenate([ret_decay_fwd[e], ret_decay_bwd[e]])
            r_p, sf, sb = _retention(proj, dec, BATCH, SEQ, 0, HEADS, emit_state=True)
            (r_s,) = _retention(proj, dec, DEC_BATCH, DEC_SEQ, T_PROMPT // DEC_SEQ, 2,
                                s0f=state_ret_fwd, s0b=state_ret_bwd, e=e)
            lams = (row(diff_lq1[e]), row(diff_lk1[e]), row(diff_lq2[e]), row(diff_lk2[e]),
                    row(diff_subln_g[e]))
            o_p = _attn_prompt(proj, *lams, lam_init)
            o_s = _attn_sample(proj, cache_diff_k, cache_diff_v, cos, sin_signed, *lams, lam_init, e)
            x1, u2, meta, cnt = _out_proj_tail(r_p, r_s, o_p, o_s, mix_w_out[e], xp, xs, modr,
                                               row(ln1_g[li]), row(ln1_b[li]), wr, br, li)
            caches = (ck, cv, sf, sb)
        else:
            o = li // 2
            x1, u2, meta, cnt = _conv_tail(glu, conv_dw[o], row(conv_dw_b[o]), row(conv_ln_g[o]),
                                           row(conv_ln_b[o]), conv_w2[o], row(conv_b2[o]),
                                           x, modr, row(ln1_g[li]), row(ln1_b[li]), wr, br, li)
        next_conv = {}
        if li + 1 < DEPTH and (li + 1) % 2 == 1:
            next_conv = dict(conv_w1=conv_w1[(li + 1) // 2], conv_b1=row(conv_b1[(li + 1) // 2]))
        outs = _moe(x1, u2, meta, cnt, modr, moe_w_gate, moe_w_up, moe_w_down,
                    row(ln2_g[li]), row(ln2_b[li]), li, split=(li == DEPTH - 1), **next_conv)
        x, glu = outs[0], outs[-1]

    y_prompt = outs[0].reshape(BATCH, SEQ, D)
    y_sample = outs[1].reshape(DEC_BATCH, DEC_SEQ, D)
    return (y_prompt, y_sample) + caches
```

```python
import functools
import math

import numpy as np
import jax
import jax.numpy as jnp
from jax import lax
from jax.experimental import pallas as pl
from jax.experimental.pallas import tpu as pltpu

F32 = jnp.float32
BF16 = jnp.bfloat16

D = 1024
BATCH = 16
SEQ = 256
DEPTH = 2
DEC_BATCH = 2
DEC_SEQ = 2048
PAST_LEN = 512
GRID_W = 64
HEADS = 4
HEAD_W = 128
RET_CHUNK = 128
DIFF_DK = 64
ROPE_THETA = 10000.0
IN_W = 7 * HEADS * HEAD_W
CONV_K = 31
CONV_PAD = CONV_K // 2
N_GROUPS = 4
EXPERTS_PER_GROUP = 8
N_EXPERTS = N_GROUPS * EXPERTS_PER_GROUP
D_EXPERT = 512
ALPHA = (2.0 * DEPTH) ** 0.25
LN_EPS = 1e-5
GN_EPS = 1e-6

T_PROMPT = BATCH * SEQ
T_SAMPLE = DEC_BATCH * DEC_SEQ
T = T_PROMPT + T_SAMPLE
TM = 256
N_TILES = T // TM
PROMPT_TILES = T_PROMPT // TM
SAMPLE_TILES_PER_SEQ = DEC_SEQ // TM
MOD_ROWS = 8
MOE_TM = 256
LANES = 128
SUBLANES = 8
RUN_ALIGN = SUBLANES
SORT_ROWS = -(-(2 * TM + N_EXPERTS * (RUN_ALIGN - 1)) // TM) * TM
RUN_BITS = tuple(1 << b for b in range((2 * TM).bit_length() - 1, RUN_ALIGN.bit_length() - 2, -1))
MOE_MAX_TILES = -(-(2 * T + N_TILES * N_EXPERTS * (RUN_ALIGN - 1) + N_EXPERTS * (MOE_TM - RUN_ALIGN)) // MOE_TM)
MOE_ROWS = MOE_MAX_TILES * MOE_TM
SORT_GROUPS = SORT_ROWS // RUN_ALIGN
MOE_AHEAD = 8
MOE_IN_SLOTS = MOE_AHEAD + 1
MOE_OUT_SLOTS = 4
ROUTER_LANE0 = N_GROUPS
VMEM_LIMIT = 52 * 1024 * 1024


def _cparams(sem):
    return pltpu.CompilerParams(dimension_semantics=sem, vmem_limit_bytes=VMEM_LIMIT)


def _tile_cond_row(i):
    return jnp.where(i < PROMPT_TILES, 0, 1 + (i - PROMPT_TILES) // SAMPLE_TILES_PER_SEQ)


def _mod_spec(li, k):
    return pl.BlockSpec((1, 1, D), lambda i, *_: ((li * MOD_ROWS + _tile_cond_row(i)) * 6 + k, 0, 0))


def _row_spec(shape):
    return pl.BlockSpec(shape, lambda i, *_: (0,) * len(shape))


def _resident_f32_weight(shape):
    return pl.BlockSpec(shape, lambda i, *_: (0,) * len(shape), pipeline_mode=pl.Buffered(1))


def _cast_weight_once(w_ref, w_bf):
    @pl.when(pl.program_id(0) == 0)
    def _():
        w_bf[...] = w_ref[...].astype(BF16)


def _layer_norm(x, g, b):
    mu = jnp.mean(x, axis=-1, keepdims=True)
    xc = x - mu
    var = jnp.mean(xc * xc, axis=-1, keepdims=True)
    return xc * lax.rsqrt(var + LN_EPS) * g + b


def _silu(x):
    return x * jax.nn.sigmoid(x)


def _dot(a, b):
    return jnp.dot(a, b, preferred_element_type=F32)


def _dot_nt(a, b):
    return lax.dot_general(a, b, (((1,), (1,)), ((), ())), preferred_element_type=F32)


def _dot_tn(a, b):
    return lax.dot_general(a, b, (((0,), (0,)), ((), ())), preferred_element_type=F32)


MOD_TN = 1024
MOD_USED_ROWS = 1 + DEC_BATCH


def _mod_kernel(cond_t_ref, w_ref, b_ref, o_ref):
    s = _silu(cond_t_ref[...])
    w = w_ref[0]
    o_ref[0] = jnp.zeros((MOD_ROWS, MOD_TN), F32) + b_ref[0]
    for r in range(MOD_USED_ROWS):
        o_ref[0, r:r + 1, :] = jnp.sum(w * s[:, r:r + 1], axis=0, keepdims=True) + b_ref[0]


def _mod_vectors(cond, mod_w, mod_b):
    return pl.pallas_call(
        _mod_kernel,
        grid=(DEPTH, 6 * D // MOD_TN),
        in_specs=[
            pl.BlockSpec((D, MOD_ROWS), lambda l, j: (0, 0)),
            pl.BlockSpec((1, D, MOD_TN), lambda l, j: (l, 0, j)),
            pl.BlockSpec((1, 1, MOD_TN), lambda l, j: (l, 0, j)),
        ],
        out_specs=pl.BlockSpec((1, MOD_ROWS, MOD_TN), lambda l, j: (l, 0, j)),
        out_shape=jax.ShapeDtypeStruct((DEPTH, MOD_ROWS, 6 * D), F32),
        compiler_params=_cparams(("arbitrary", "arbitrary")),
        name="mod_vectors",
    )(cond.T, mod_w, mod_b.reshape(DEPTH, 1, 6 * D))


def _prompt_tile_spec(width):
    return pl.BlockSpec((TM, width), lambda i, *_: (jnp.minimum(i, PROMPT_TILES - 1), 0))


def _sample_tile_spec(width):
    return pl.BlockSpec((TM, width), lambda i, *_: (jnp.maximum(i - PROMPT_TILES, 0), 0))


def _pick_tile(prompt_ref, sample_ref):
    return jnp.where(pl.program_id(0) < PROMPT_TILES, prompt_ref[...], sample_ref[...])


def _in_proj_kernel(xp_ref, xs_ref, sh_ref, sc_ref, w_ref, o_ref, ck_ref, cv_ref, w_bf):
    _cast_weight_once(w_ref, w_bf)
    u = _pick_tile(xp_ref, xs_ref) * (1.0 + sc_ref[0]) + sh_ref[0]
    proj = _dot(u.astype(BF16), w_bf[...])
    o_ref[...] = proj.astype(BF16)

    @pl.when(pl.program_id(0) < PROMPT_TILES)
    def _():
        for h in range(HEADS):
            ck_ref[0, 0, h] = proj[:, (COL_KD + h) * HEAD_W:(COL_KD + h + 1) * HEAD_W]
            cv_ref[0, 0, h] = proj[:, (COL_VD + h) * HEAD_W:(COL_VD + h + 1) * HEAD_W]


def _in_proj(x_prompt, x_sample, modr, w_in, li):
    cache_spec = pl.BlockSpec((1, 1, HEADS, SEQ, HEAD_W),
                              lambda i: (jnp.minimum(i, PROMPT_TILES - 1), 0, 0, 0, 0))
    cache_shape = jax.ShapeDtypeStruct((BATCH, 1, HEADS, SEQ, HEAD_W), F32)
    return pl.pallas_call(
        _in_proj_kernel,
        grid=(N_TILES,),
        in_specs=[
            _prompt_tile_spec(D), _sample_tile_spec(D),
            _mod_spec(li, 0),
            _mod_spec(li, 1),
            _resident_f32_weight((D, IN_W)),
        ],
        out_specs=[pl.BlockSpec((TM, IN_W), lambda i: (i, 0)), cache_spec, cache_spec],
        out_shape=[jax.ShapeDtypeStruct((T, IN_W), BF16), cache_shape, cache_shape],
        scratch_shapes=[pltpu.VMEM((D, IN_W), BF16)],
        compiler_params=_cparams(("arbitrary",)),
        name="in_proj",
    )(x_prompt, x_sample, modr, modr, w_in)


COL_QR, COL_KR, COL_VR, COL_GR, COL_QD, COL_KD, COL_VD = (k * HEADS for k in range(7))


def _retention_kernel(dec_ref, q_ref, k_ref, v_ref, g_ref, *rest, n_chunks, n_heads, has_state, emit_state):
    rest = list(rest)
    if has_state:
        s0f_ref, s0b_ref = rest[:2]
        rest = rest[2:]
    r_ref = rest[0]
    rest = rest[1:]
    if emit_state:
        sf_ref, sb_ref = rest[:2]
        rest = rest[2:]
    of_ref = rest[0]

    head0 = pl.program_id(1) * n_heads
    C = RET_CHUNK
    ii = lax.broadcasted_iota(jnp.int32, (C, C), 0)
    jj = lax.broadcasted_iota(jnp.int32, (C, C), 1)
    rel = (ii - jj).astype(F32)
    idx = lax.broadcasted_iota(jnp.int32, (C, 1), 0).astype(F32)
    k_scale = HEAD_W ** -0.5

    def chunk(ref, c, h):
        return ref[c * C:(c + 1) * C, h * HEAD_W:(h + 1) * HEAD_W].astype(F32)

    def decays(direction, h):
        lg = -jnp.exp(jnp.full((1, 1), dec_ref[direction * HEADS + head0 + h], F32))
        if direction == 0:
            inner = jnp.where(rel >= 0, jnp.exp(jnp.maximum(rel, 0.0) * lg), 0.0)
            return inner, jnp.exp((idx + 1.0) * lg), jnp.exp((C - 1.0 - idx) * lg), jnp.exp(C * lg)
        inner = jnp.where(rel <= 0, jnp.exp(jnp.maximum(-rel, 0.0) * lg), 0.0)
        return inner, jnp.exp((C - idx) * lg), jnp.exp(idx * lg), jnp.exp(C * lg)

    def run(direction):
        dec = [decays(direction, h) for h in range(n_heads)]
        if has_state:
            s0_ref = s0f_ref if direction == 0 else s0b_ref
            states = [s0_ref[0, 0, h] for h in range(n_heads)]
        else:
            states = [jnp.zeros((HEAD_W, HEAD_W), F32) for _ in range(n_heads)]
        order = range(n_chunks) if direction == 0 else range(n_chunks - 1, -1, -1)
        for c in order:
            rows = slice(c * C, (c + 1) * C)
            for h in range(n_heads):
                inner, q_decay, k_decay, chunk_decay = dec[h]
                cols = slice(h * HEAD_W, (h + 1) * HEAD_W)
                s = states[h]
                qc = chunk(q_ref, c, h)
                kc = chunk(k_ref, c, h) * k_scale
                vc = chunk(v_ref, c, h).astype(BF16)
                scores = _dot_nt(qc.astype(BF16), kc.astype(BF16)) * inner
                o = _dot(scores.astype(BF16), vc) + _dot((qc * q_decay).astype(BF16), s.astype(BF16))
                states[h] = s * chunk_decay + _dot_tn((kc * k_decay).astype(BF16), vc)
                if direction == 0:
                    of_ref[rows, cols] = o
                else:
                    r = of_ref[rows, cols] + o
                    mu = jnp.mean(r, axis=-1, keepdims=True)
                    rc = r - mu
                    var = jnp.mean(rc * rc, axis=-1, keepdims=True)
                    rn = rc * lax.rsqrt(var + GN_EPS)
                    r_ref[rows, cols] = _silu(chunk(g_ref, c, h)) * rn
        return states

    sf = run(0)
    sb = run(1)
    if emit_state:
        for h in range(n_heads):
            sf_ref[0, 0, h] = sf[h]
            sb_ref[0, 0, h] = sb[h]


def _retention(proj, dec, n_seq, seq_len, row_block0, n_heads, s0f=None, s0b=None, e=0, emit_state=False):
    has_state = s0f is not None
    width = n_heads * HEAD_W

    def col(base):
        return pl.BlockSpec((seq_len, width), lambda b, h, *_: (row_block0 + b, base // n_heads + h))

    state_spec = pl.BlockSpec((1, 1, n_heads, HEAD_W, HEAD_W), lambda b, h, *_: (b, e, h, 0, 0))
    in_specs = [pl.BlockSpec(memory_space=pltpu.SMEM), col(COL_QR), col(COL_KR), col(COL_VR), col(COL_GR)]
    args = [dec, proj, proj, proj, proj]
    if has_state:
        in_specs += [state_spec, state_spec]
        args += [s0f, s0b]
    out_specs = [pl.BlockSpec((seq_len, width), lambda b, h, *_: (b, h))]
    out_shape = [jax.ShapeDtypeStruct((n_seq * seq_len, HEADS * HEAD_W), F32)]
    if emit_state:
        st = pl.BlockSpec((1, 1, n_heads, HEAD_W, HEAD_W), lambda b, h, *_: (b, 0, h, 0, 0))
        out_specs += [st, st]
        out_shape += [jax.ShapeDtypeStruct((n_seq, 1, HEADS, HEAD_W, HEAD_W), F32)] * 2
    return pl.pallas_call(
        functools.partial(_retention_kernel, n_chunks=seq_len // RET_CHUNK, n_heads=n_heads,
                          has_state=has_state, emit_state=emit_state),
        grid=(n_seq, HEADS // n_heads),
        in_specs=in_specs,
        out_specs=out_specs,
        out_shape=out_shape,
        scratch_shapes=[pltpu.VMEM((seq_len, width), F32)],
        compiler_params=_cparams(("arbitrary", "arbitrary")),
        name=f"retention_{seq_len}",
    )(*args)


def _diff_lambda(lq1_ref, lk1_ref, lq2_ref, lk2_ref, lam_init):
    a = jnp.sum(lq1_ref[...] * lk1_ref[...], axis=-1, keepdims=True)
    b = jnp.sum(lq2_ref[...] * lk2_ref[...], axis=-1, keepdims=True)
    return jnp.exp(a) - jnp.exp(b) + lam_init


LOG2E = 1.4426950408889634


def _diff_attend(q, k, v, lam, subln_g, lam_init):
    lane = lax.broadcasted_iota(jnp.int32, q.shape, 1)
    q1 = jnp.where(lane < DIFF_DK, q, 0.0).astype(BF16)
    q2 = jnp.where(lane >= DIFF_DK, q, 0.0).astype(BF16)

    def softmax_times_v(qz):
        s = _dot_nt(qz, k)
        p = jnp.exp2(s - jnp.max(s, axis=-1, keepdims=True))
        return _dot(p.astype(BF16), v) * (1.0 / jnp.sum(p, axis=-1, keepdims=True))

    o = softmax_times_v(q1) - lam * softmax_times_v(q2)
    o = o * lax.rsqrt(jnp.mean(o * o, axis=-1, keepdims=True) + LN_EPS)
    return o * subln_g * (1.0 - lam_init)


def _attn_prompt_kernel(q_ref, k_ref, v_ref, lq1, lk1, lq2, lk2, g_ref, o_ref, *, lam_init):
    lam = _diff_lambda(lq1, lk1, lq2, lk2, lam_init)
    scale = DIFF_DK ** -0.5 * LOG2E
    for h in range(HEADS):
        sl = slice(h * HEAD_W, (h + 1) * HEAD_W)
        o_ref[:, sl] = _diff_attend(q_ref[:, sl].astype(F32) * scale, k_ref[:, sl], v_ref[:, sl],
                                    lam, g_ref[...], lam_init)


def _attn_prompt(proj, lq1, lk1, lq2, lk2, subln_g, lam_init):
    W = HEADS * HEAD_W

    def slab(base):
        return pl.BlockSpec((SEQ, W), lambda b: (b, base // HEADS))

    small = _row_spec((1, DIFF_DK))
    return pl.pallas_call(
        functools.partial(_attn_prompt_kernel, lam_init=lam_init),
        grid=(BATCH,),
        in_specs=[slab(COL_QD), slab(COL_KD), slab(COL_VD), small, small, small, small,
                  _row_spec((1, HEAD_W))],
        out_specs=pl.BlockSpec((SEQ, W), lambda b: (b, 0)),
        out_shape=jax.ShapeDtypeStruct((T_PROMPT, W), F32),
        compiler_params=_cparams(("arbitrary",)),
        name="diff_attn_prompt",
    )(proj, proj, proj, lq1, lk1, lq2, lk2, subln_g)


def _rope(x, cos, sin_signed):
    lane = lax.broadcasted_iota(jnp.int32, x.shape, 1)
    partner = jnp.where((lane % 32) < 16, pltpu.roll(x, LANES - 16, 1), pltpu.roll(x, 16, 1))
    return x * cos + partner * sin_signed


def _attn_sample_kernel(q_ref, k_ref, v_ref, ck_ref, cv_ref, cosq_ref, sinq_ref, cos_ref, sin_ref,
                        lq1, lk1, lq2, lk2, g_ref, o_ref, kbuf, vbuf, *, lam_init):
    @pl.when(pl.program_id(2) == 0)
    def _():
        kbuf[0:DEC_SEQ, :] = _rope(k_ref[...].astype(F32), cos_ref[...], sin_ref[...]).astype(BF16)
        kbuf[DEC_SEQ:, :] = ck_ref[0, 0, 0].astype(BF16)
        vbuf[0:DEC_SEQ, :] = v_ref[...]
        vbuf[DEC_SEQ:, :] = cv_ref[0, 0, 0].astype(BF16)

    lam = _diff_lambda(lq1, lk1, lq2, lk2, lam_init)
    q = _rope(q_ref[...].astype(F32), cosq_ref[...], sinq_ref[...]) * (DIFF_DK ** -0.5 * LOG2E)
    o_ref[...] = _diff_attend(q, kbuf[...], vbuf[...], lam, g_ref[...], lam_init)


ATTN_TQ = 256


def _attn_sample(proj, cache_k, cache_v, cos, sin_signed, lq1, lk1, lq2, lk2, subln_g, lam_init, e):
    nq = DEC_SEQ // ATTN_TQ
    row0_q = T_PROMPT // ATTN_TQ
    row0_kv = T_PROMPT // DEC_SEQ
    small = pl.BlockSpec((1, DIFF_DK), lambda b, h, t: (0, 0))
    cache = pl.BlockSpec((1, 1, 1, PAST_LEN, HEAD_W), lambda b, h, t: (b, e, h, 0, 0))
    table_q = pl.BlockSpec((ATTN_TQ, HEAD_W), lambda b, h, t: (t, 0))
    table = pl.BlockSpec((DEC_SEQ, HEAD_W), lambda b, h, t: (0, 0))
    return pl.pallas_call(
        functools.partial(_attn_sample_kernel, lam_init=lam_init),
        grid=(DEC_BATCH, HEADS, nq),
        in_specs=[
            pl.BlockSpec((ATTN_TQ, HEAD_W), lambda b, h, t: (row0_q + b * nq + t, COL_QD + h)),
            pl.BlockSpec((DEC_SEQ, HEAD_W), lambda b, h, t: (row0_kv + b, COL_KD + h)),
            pl.BlockSpec((DEC_SEQ, HEAD_W), lambda b, h, t: (row0_kv + b, COL_VD + h)),
            cache, cache, table_q, table_q, table, table,
            small, small, small, small,
            pl.BlockSpec((1, HEAD_W), lambda b, h, t: (0, 0)),
        ],
        out_specs=pl.BlockSpec((ATTN_TQ, HEAD_W), lambda b, h, t: (b * nq + t, h)),
        out_shape=jax.ShapeDtypeStruct((T_SAMPLE, HEADS * HEAD_W), F32),
        scratch_shapes=[pltpu.VMEM((DEC_SEQ + PAST_LEN, HEAD_W), BF16),
                        pltpu.VMEM((DEC_SEQ + PAST_LEN, HEAD_W), BF16)],
        compiler_params=_cparams(("arbitrary", "arbitrary", "arbitrary")),
        name="diff_attn_sample",
    )(proj, proj, proj, cache_k, cache_v, cos, sin_signed, cos, sin_signed,
      lq1, lk1, lq2, lk2, subln_g)


def _rope_tables():
    t = np.arange(DEC_SEQ)
    row, colp = t // GRID_W, t % GRID_W
    lane = np.arange(LANES)
    pos = np.where(((lane // 32) % 2 == 0)[None, :], row[:, None], colp[:, None]).astype(np.float64)
    half = 16
    inv = (np.float32(ROPE_THETA) ** (-(np.arange(half, dtype=np.float32)) / np.float32(half))).astype(np.float32)
    ang = pos.astype(np.float32) * inv[lane % half][None, :]
    cos = np.cos(ang.astype(np.float64)).astype(np.float32)
    sin = np.sin(ang.astype(np.float64)).astype(np.float32)
    sign = np.where((lane % 32) < half, -1.0, 1.0).astype(np.float32)[None, :]
    return jnp.asarray(cos), jnp.asarray(sin * sign)


def _split_bf16(a):
    hi = a.astype(BF16)
    return hi, (a - hi.astype(F32)).astype(BF16)


def _mixer_tail(out, x, g1_ref, sc2_ref, sh2_ref, lng_ref, lnb_ref, wr_ref, br_ref,
                x1_ref, u2_ref, meta_ref, cnt_ref):
    x1 = _layer_norm(ALPHA * x + g1_ref[0] * out, lng_ref[...], lnb_ref[...])
    x1_ref[...] = x1
    u2 = x1 * (1.0 + sc2_ref[0]) + sh2_ref[0]
    u2_ref[...] = u2.astype(BF16)

    u_hi, u_lo = _split_bf16(u2)
    w_hi, w_lo = _split_bf16(wr_ref[...])
    logits = _dot(u_hi, w_hi) + (_dot(u_hi, w_lo) + _dot(u_lo, w_hi)) + br_ref[...]
    lane = lax.broadcasted_iota(jnp.int32, logits.shape, 1).astype(F32)
    neg = jnp.float32(-jnp.inf)
    is_g = lane < N_GROUPS
    gl = jnp.where(is_g, logits, neg)
    gmax = jnp.max(gl, axis=-1, keepdims=True)
    gsel = jnp.min(jnp.where(gl == gmax, lane, float(LANES)), axis=-1, keepdims=True)
    p_g = 1.0 / jnp.sum(jnp.where(is_g, jnp.exp(gl - gmax), 0.0), axis=-1, keepdims=True)
    lo = ROUTER_LANE0 + gsel * EXPERTS_PER_GROUP
    el = jnp.where((lane >= lo) & (lane < lo + EXPERTS_PER_GROUP), logits, neg)
    v1 = jnp.max(el, axis=-1, keepdims=True)
    i1 = jnp.min(jnp.where(el == v1, lane, float(LANES)), axis=-1, keepdims=True)
    el2 = jnp.where(lane == i1, neg, el)
    v2 = jnp.max(el2, axis=-1, keepdims=True)
    i2 = jnp.min(jnp.where(el2 == v2, lane, float(LANES)), axis=-1, keepdims=True)
    t = jnp.exp(v2 - v1)
    w1 = p_g / (1.0 + t)
    w2 = w1 * t

    oh1 = (lane == i1).astype(F32)
    oh2 = (lane == i2).astype(F32)
    oh = oh1 + oh2
    r_i = lax.broadcasted_iota(jnp.int32, (TM, TM), 0)
    c_i = lax.broadcasted_iota(jnp.int32, (TM, TM), 1)
    before = (c_i < r_i).astype(BF16)
    earlier = _dot(before, oh.astype(BF16))
    rank1 = jnp.sum(earlier * oh1, axis=-1, keepdims=True)
    rank2 = jnp.sum(earlier * oh2, axis=-1, keepdims=True)
    cnt_ref[0] = jnp.sum(oh, axis=0, keepdims=True)
    cols = (i1, i2, w1, w2, rank1, rank2)
    meta = jnp.zeros_like(logits)
    for k, col in enumerate(cols):
        meta = jnp.where(lane == k, col, meta)
    meta_ref[...] = meta


META_E1, META_E2, META_W1, META_W2, META_RANK1, META_RANK2 = range(6)

_TAIL_OUT_SHAPES = [
    jax.ShapeDtypeStruct((T, D), F32),
    jax.ShapeDtypeStruct((T, D), BF16),
    jax.ShapeDtypeStruct((T, LANES), F32),
    jax.ShapeDtypeStruct((N_TILES, 1, LANES), F32),
]


def _tail_out_specs():
    return [
        pl.BlockSpec((TM, D), lambda i: (i, 0)),
        pl.BlockSpec((TM, D), lambda i: (i, 0)),
        pl.BlockSpec((TM, LANES), lambda i: (i, 0)),
        pl.BlockSpec((1, 1, LANES), lambda i: (i, 0, 0)),
    ]


def _tail_in_specs(li):
    return [
        _mod_spec(li, 2), _mod_spec(li, 4), _mod_spec(li, 3),
        _row_spec((1, D)), _row_spec((1, D)),
        _row_spec((D, LANES)), _row_spec((1, LANES)),
    ]


def _out_proj_kernel(rp_ref, rs_ref, op_ref, os_ref, w_ref, xp_ref, xs_ref, *rest):
    tail_args, w_bf = rest[:-1], rest[-1]
    _cast_weight_once(w_ref, w_bf)
    half = HEADS * HEAD_W
    r = _pick_tile(rp_ref, rs_ref).astype(BF16)
    o = _pick_tile(op_ref, os_ref).astype(BF16)
    out = _dot(r, w_bf[0:half, :]) + _dot(o, w_bf[half:, :])
    _mixer_tail(out, _pick_tile(xp_ref, xs_ref), *tail_args)


def _out_proj_tail(r_p, r_s, o_p, o_s, w_out, x_prompt, x_sample, modr, ln_g, ln_b, wr, br, li):
    half = HEADS * HEAD_W
    return pl.pallas_call(
        _out_proj_kernel,
        grid=(N_TILES,),
        in_specs=[_prompt_tile_spec(half), _sample_tile_spec(half),
                  _prompt_tile_spec(half), _sample_tile_spec(half),
                  _resident_f32_weight((2 * half, D)),
                  _prompt_tile_spec(D), _sample_tile_spec(D)] + _tail_in_specs(li),
        out_specs=_tail_out_specs(),
        out_shape=_TAIL_OUT_SHAPES,
        scratch_shapes=[pltpu.VMEM((2 * half, D), BF16)],
        compiler_params=_cparams(("arbitrary",)),
        name="out_proj_tail",
    )(r_p, r_s, o_p, o_s, w_out, x_prompt, x_sample, modr, modr, modr, ln_g, ln_b, wr, br)


def _conv_glu(y, sh_ref, sc_ref, w_bf, b_ref):
    u = y * (1.0 + sc_ref[0]) + sh_ref[0]
    h = _dot(u.astype(BF16), w_bf[...]) + b_ref[...]
    return h[:, :D] * jax.nn.sigmoid(h[:, D:])


HALO = 16
CONV_ROWS = 64
CONV_COLS = 128


def _depthwise_conv(hp, dw_ref, conv):
    base = HALO - CONV_PAD
    for cb in range(D // CONV_COLS):
        cs = slice(cb * CONV_COLS, (cb + 1) * CONV_COLS)
        for rb in range(TM // CONV_ROWS):
            r0 = rb * CONV_ROWS
            acc = None
            for shift in range(SUBLANES):
                part = None
                for tap in range(CONV_K):
                    off = base + tap
                    if off % SUBLANES != shift:
                        continue
                    a0 = r0 + off - shift
                    term = hp[a0:a0 + CONV_ROWS + SUBLANES, cs] * dw_ref[tap:tap + 1, cs]
                    part = term if part is None else part + term
                part = part[shift:shift + CONV_ROWS, :]
                acc = part if acc is None else acc + part
            conv[r0:r0 + CONV_ROWS, cs] = acc


def _conv_tail_kernel(cur_ref, prev_ref, next_ref, dw_ref, dwb_ref, cg_ref, cb_ref, w2_ref, b2_ref,
                      x_ref, *rest):
    tail_args, (hp, conv, w2_bf) = rest[:-3], rest[-3:]
    _cast_weight_once(w2_ref, w2_bf)
    i = pl.program_id(0)
    k = (i - PROMPT_TILES) % SAMPLE_TILES_PER_SEQ
    in_sample = i >= PROMPT_TILES
    left_ok = jnp.logical_and(in_sample, k != 0)
    right_ok = jnp.logical_and(in_sample, k != SAMPLE_TILES_PER_SEQ - 1)
    hp[0:HALO, :] = jnp.where(left_ok, prev_ref[...], 0.0)
    hp[HALO:HALO + TM, :] = cur_ref[...]
    hp[HALO + TM:HALO + TM + HALO, :] = jnp.where(right_ok, next_ref[...], 0.0)
    _depthwise_conv(hp, dw_ref, conv)
    hc = _silu(_layer_norm(conv[...] + dwb_ref[...], cg_ref[...], cb_ref[...]))
    out = _dot(hc.astype(BF16), w2_bf[...]) + b2_ref[...]
    _mixer_tail(out, x_ref[...], *tail_args)


def _conv_tail(glu, dw, dwb, cg, cb, w2, b2, x, modr, ln_g, ln_b, wr, br, li):
    per = TM // HALO
    last = T // HALO - 1
    return pl.pallas_call(
        _conv_tail_kernel,
        grid=(N_TILES,),
        in_specs=[pl.BlockSpec((TM, D), lambda i: (i, 0)),
                  pl.BlockSpec((HALO, D), lambda i: (jnp.maximum(i * per - 1, 0), 0)),
                  pl.BlockSpec((HALO, D), lambda i: (jnp.minimum((i + 1) * per, last), 0)),
                  _row_spec((CONV_K, D)), _row_spec((1, D)), _row_spec((1, D)), _row_spec((1, D)),
                  _resident_f32_weight((D, D)), _row_spec((1, D)),
                  pl.BlockSpec((TM, D), lambda i: (i, 0))] + _tail_in_specs(li),
        out_specs=_tail_out_specs(),
        out_shape=_TAIL_OUT_SHAPES,
        scratch_shapes=[pltpu.VMEM((TM + 2 * HALO, D), F32), pltpu.VMEM((TM, D), F32),
                        pltpu.VMEM((D, D), BF16)],
        compiler_params=_cparams(("arbitrary",)),
        name="conv_tail",
    )(glu, glu, glu, dw, dwb, cg, cb, w2, b2, x, modr, modr, modr, ln_g, ln_b, wr, br)


def _sorted_positions(meta, srcv):
    lane = lax.broadcasted_iota(jnp.int32, meta.shape, 1).astype(F32)

    def pos(e_col, r_col):
        start = jnp.sum(jnp.where(lane == meta[:, e_col:e_col + 1], srcv, 0.0), axis=-1, keepdims=True)
        return start + meta[:, r_col:r_col + 1]

    return pos(META_E1, META_RANK1), pos(META_E2, META_RANK2)


def _one_hot_rows(pos):
    col = lax.broadcasted_iota(jnp.int32, (TM, SORT_ROWS), 1).astype(F32)
    return col == pos


def _for_each_row_group(tile, tot_ref, dstg_ref, fn):
    def body(g, carry):
        fn(pl.multiple_of(g * RUN_ALIGN, RUN_ALIGN),
           pl.multiple_of(dstg_ref[tile * SORT_GROUPS + g], RUN_ALIGN))
        return carry

    lax.fori_loop(0, tot_ref[tile] // RUN_ALIGN, body, 0)


def _wait_rows(total, make_copy):
    for bit in RUN_BITS:
        @pl.when((total & bit) != 0)
        def _(bit=bit):
            make_copy(bit).wait()


def _dispatch_kernel(tot_ref, dstg_ref, u_ref, meta_ref, srcv_ref, xs_ref, sorted_ref, sems):
    i = pl.program_id(0)
    slot = i % 2

    def wait_tile(tile, slot):
        buf = sorted_ref.at[slot]
        _wait_rows(tot_ref[tile], lambda rows: pltpu.make_async_copy(
            buf.at[pl.ds(0, rows)], xs_ref.at[pl.ds(0, rows)], sems.at[slot]))

    @pl.when(i >= 2)
    def _():
        wait_tile(i - 2, slot)

    pos1, pos2 = _sorted_positions(meta_ref[...], srcv_ref[0])
    select = jnp.logical_or(_one_hot_rows(pos1), _one_hot_rows(pos2)).astype(BF16)
    sorted_ref[slot] = _dot_tn(select, u_ref[...])
    buf = sorted_ref.at[slot]

    def start(src, dst):
        pltpu.make_async_copy(buf.at[pl.ds(src, RUN_ALIGN)], xs_ref.at[pl.ds(dst, RUN_ALIGN)],
                              sems.at[slot]).start()

    _for_each_row_group(i, tot_ref, dstg_ref, start)

    @pl.when(i == N_TILES - 1)
    def _():
        wait_tile(i - 1, 1 - slot)
        wait_tile(i, slot)


def _dispatch(sched, u2, meta, srcv):
    return pl.pallas_call(
        _dispatch_kernel,
        grid_spec=pltpu.PrefetchScalarGridSpec(
            num_scalar_prefetch=2,
            grid=(N_TILES,),
            in_specs=[pl.BlockSpec((TM, D), lambda i, *_: (i, 0)),
                      pl.BlockSpec((TM, LANES), lambda i, *_: (i, 0)),
                      pl.BlockSpec((1, 1, LANES), lambda i, *_: (i, 0, 0))],
            out_specs=pl.BlockSpec(memory_space=pl.ANY),
            scratch_shapes=[pltpu.VMEM((2, SORT_ROWS, D), F32), pltpu.SemaphoreType.DMA((2,))],
        ),
        out_shape=jax.ShapeDtypeStruct((MOE_ROWS, D), F32),
        compiler_params=_cparams(("arbitrary",)),
        name="moe_dispatch",
    )(*sched, u2, meta, srcv)


def _experts_kernel(start_ref, chunks_ref, xs_ref, wg_ref, wu_ref, wd_ref, ys_ref,
                    wg_bf, wu_bf, wd_bf, xbuf, ybuf, in_sems, out_sems):
    e = pl.program_id(0)
    n = chunks_ref[e]
    first = start_ref[e] // MOE_TM
    total = start_ref[N_EXPERTS - 1] // MOE_TM + chunks_ref[N_EXPERTS - 1]

    def rows(g):
        return pl.ds(pl.multiple_of(g * MOE_TM, MOE_TM), MOE_TM)

    def load(g):
        slot = g % MOE_IN_SLOTS
        return pltpu.make_async_copy(xs_ref.at[rows(g)], xbuf.at[slot], in_sems.at[slot])

    def store(g):
        slot = g % MOE_OUT_SLOTS
        return pltpu.make_async_copy(ybuf.at[slot], ys_ref.at[rows(g)], out_sems.at[slot])

    @pl.when(e == 0)
    def _():
        for g in range(MOE_AHEAD):
            @pl.when(g < total)
            def _(g=g):
                load(g).start()

    @pl.when(n > 0)
    def _():
        wg_bf[...] = wg_ref[0, 0].astype(BF16)
        wu_bf[...] = wu_ref[0, 0].astype(BF16)
        wd_bf[...] = wd_ref[0, 0].astype(BF16)

        def tile(g, carry):
            load(g).wait()

            @pl.when(g + MOE_AHEAD < total)
            def _():
                load(g + MOE_AHEAD).start()

            @pl.when(g >= MOE_OUT_SLOTS)
            def _():
                store(g - MOE_OUT_SLOTS).wait()

            x = xbuf[g % MOE_IN_SLOTS].astype(BF16)
            h = (_silu(_dot(x, wg_bf[...])) * _dot(x, wu_bf[...])).astype(BF16)
            ybuf[g % MOE_OUT_SLOTS] = _dot(h, wd_bf[...])
            store(g).start()
            return carry

        lax.fori_loop(first, first + n, tile, 0)

    @pl.when(e == N_EXPERTS - 1)
    def _():
        for back in range(MOE_OUT_SLOTS, 0, -1):
            @pl.when(total >= back)
            def _(back=back):
                store(total - back).wait()


def _experts(seg_start, seg_chunks, xs, w_gate, w_up, w_down, li):
    def weight(shape):
        return pl.BlockSpec((1, 1) + shape, lambda e, *_: (li, e, 0, 0))

    return pl.pallas_call(
        _experts_kernel,
        grid_spec=pltpu.PrefetchScalarGridSpec(
            num_scalar_prefetch=2,
            grid=(N_EXPERTS,),
            in_specs=[pl.BlockSpec(memory_space=pl.ANY),
                      weight((D, D_EXPERT)), weight((D, D_EXPERT)), weight((D_EXPERT, D))],
            out_specs=pl.BlockSpec(memory_space=pl.ANY),
            scratch_shapes=[pltpu.VMEM((D, D_EXPERT), BF16), pltpu.VMEM((D, D_EXPERT), BF16),
                            pltpu.VMEM((D_EXPERT, D), BF16),
                            pltpu.VMEM((MOE_IN_SLOTS, MOE_TM, D), F32),
                            pltpu.VMEM((MOE_OUT_SLOTS, MOE_TM, D), F32),
                            pltpu.SemaphoreType.DMA((MOE_IN_SLOTS,)),
                            pltpu.SemaphoreType.DMA((MOE_OUT_SLOTS,))],
        ),
        out_shape=jax.ShapeDtypeStruct((MOE_ROWS, D), F32),
        compiler_params=_cparams(("arbitrary",)),
        name="moe_experts",
    )(seg_start, seg_chunks, xs, w_gate, w_up, w_down)


def _combine_kernel(tot_ref, dstg_ref, ys_ref, x1_ref, meta_ref, srcv_ref, g2_ref,
                    lng_ref, lnb_ref, *rest, split, feeds_conv):
    if feeds_conv:
        (sh_ref, sc_ref, w1_ref, b1_ref), rest, w1_bf = rest[:4], rest[4:-1], rest[-1]
        _cast_weight_once(w1_ref, w1_bf)
    outs, (sorted_ref, sems) = rest[:-2], rest[-2:]
    i = pl.program_id(0)
    slot = i % 2

    def fetch(tile, slot):
        buf = sorted_ref.at[slot]

        def start(src, dst):
            pltpu.make_async_copy(ys_ref.at[pl.ds(dst, RUN_ALIGN)], buf.at[pl.ds(src, RUN_ALIGN)],
                                  sems.at[slot]).start()

        _for_each_row_group(tile, tot_ref, dstg_ref, start)

    @pl.when(i == 0)
    def _():
        sorted_ref[...] = jnp.zeros_like(sorted_ref)
        fetch(0, 0)

    @pl.when(i + 1 < N_TILES)
    def _():
        fetch(i + 1, 1 - slot)

    meta = meta_ref[...]
    pos1, pos2 = _sorted_positions(meta, srcv_ref[0])
    sel1 = _one_hot_rows(pos1).astype(BF16)
    sel2 = _one_hot_rows(pos2).astype(BF16)
    buf = sorted_ref.at[slot]
    _wait_rows(tot_ref[i], lambda rows: pltpu.make_async_copy(
        ys_ref.at[pl.ds(0, rows)], buf.at[pl.ds(0, rows)], sems.at[slot]))
    ysort = sorted_ref[slot].astype(BF16)
    f = (meta[:, META_W1:META_W1 + 1] * _dot(sel1, ysort)
         + meta[:, META_W2:META_W2 + 1] * _dot(sel2, ysort))
    y = _layer_norm(ALPHA * x1_ref[...] + g2_ref[0] * f, lng_ref[...], lnb_ref[...])
    if split:
        @pl.when(i < PROMPT_TILES)
        def _():
            outs[0][...] = y

        @pl.when(i >= PROMPT_TILES)
        def _():
            outs[1][...] = y
    else:
        outs[0][...] = y
    if feeds_conv:
        outs[-1][...] = _conv_glu(y, sh_ref, sc_ref, w1_bf, b1_ref)


def _combine(sched, ys, x1, meta, srcv, modr, ln_g, ln_b, li, split, conv_w1=None, conv_b1=None):
    feeds_conv = conv_w1 is not None
    tile = pl.BlockSpec((TM, D), lambda i, *_: (i, 0))
    if split:
        out_specs = [_prompt_tile_spec(D), _sample_tile_spec(D)]
        out_shape = [jax.ShapeDtypeStruct((T_PROMPT, D), F32), jax.ShapeDtypeStruct((T_SAMPLE, D), F32)]
    else:
        out_specs = [tile]
        out_shape = [jax.ShapeDtypeStruct((T, D), F32)]
    extra_specs, extra_args, extra_scratch = [], [], []
    if feeds_conv:
        extra_specs = [_mod_spec(li + 1, 0), _mod_spec(li + 1, 1),
                       _resident_f32_weight((D, 2 * D)), _row_spec((1, 2 * D))]
        extra_args = [modr, modr, conv_w1, conv_b1]
        extra_scratch = [pltpu.VMEM((D, 2 * D), BF16)]
        out_specs = out_specs + [tile]
        out_shape = out_shape + [jax.ShapeDtypeStruct((T, D), F32)]
    return pl.pallas_call(
        functools.partial(_combine_kernel, split=split, feeds_conv=feeds_conv),
        grid_spec=pltpu.PrefetchScalarGridSpec(
            num_scalar_prefetch=2,
            grid=(N_TILES,),
            in_specs=[pl.BlockSpec(memory_space=pl.ANY),
                      pl.BlockSpec((TM, D), lambda i, *_: (i, 0)),
                      pl.BlockSpec((TM, LANES), lambda i, *_: (i, 0)),
                      pl.BlockSpec((1, 1, LANES), lambda i, *_: (i, 0, 0)),
                      _mod_spec(li, 5), _row_spec((1, D)), _row_spec((1, D))] + extra_specs,
            out_specs=out_specs,
            scratch_shapes=[pltpu.VMEM((2, SORT_ROWS, D), F32), pltpu.SemaphoreType.DMA((2,))] + extra_scratch,
        ),
        out_shape=out_shape,
        compiler_params=_cparams(("arbitrary",)),
        name="moe_combine",
    )(*sched, ys, x1, meta, srcv, modr, ln_g, ln_b, *extra_args)


def _moe_schedule(tile_counts):
    n = (tile_counts + RUN_ALIGN - 1) // RUN_ALIGN * RUN_ALIGN
    src = jnp.cumsum(n, axis=1) - n
    per_expert = jnp.sum(n, axis=0)
    seg = (per_expert + MOE_TM - 1) // MOE_TM * MOE_TM
    seg_start = jnp.cumsum(seg) - seg
    dst = seg_start[None, :] + jnp.cumsum(n, axis=0) - n
    g_row = jnp.arange(SORT_GROUPS, dtype=jnp.int32) * RUN_ALIGN
    in_run = jnp.logical_and(src[:, None, :] <= g_row[None, :, None],
                             g_row[None, :, None] < (src + n)[:, None, :])
    dst_g = g_row[None, :] + jnp.sum(jnp.where(in_run, (dst - src)[:, None, :], 0), axis=2)
    runs = (jnp.sum(n, axis=1), dst_g.reshape(-1))
    srcv = jnp.pad(src.astype(F32), ((0, 0), (ROUTER_LANE0, LANES - ROUTER_LANE0 - N_EXPERTS)))
    return runs, srcv.reshape(N_TILES, 1, LANES), (seg_start, seg // MOE_TM)


def _moe(x1, u2, meta, cnt, modr, w_gate, w_up, w_down, ln_g, ln_b, li, split, **next_conv):
    tile_counts = cnt[:, 0, ROUTER_LANE0:ROUTER_LANE0 + N_EXPERTS].astype(jnp.int32)
    runs, srcv, (seg_start, seg_chunks) = _moe_schedule(tile_counts)
    xs = _dispatch(runs, u2, meta, srcv)
    ys = _experts(seg_start, seg_chunks, xs, w_gate, w_up, w_down, li)
    return _combine(runs, ys, x1, meta, srcv, modr, ln_g, ln_b, li, split, **next_conv)


def _router_slab(wg, bg, we, be):
    w = jnp.concatenate([wg, we.transpose(1, 0, 2).reshape(D, N_EXPERTS)], axis=1)
    b = jnp.concatenate([bg, be.reshape(N_EXPERTS)])
    pad = LANES - w.shape[1]
    return jnp.pad(w, ((0, 0), (0, pad))), jnp.pad(b, (0, pad)).reshape(1, LANES)


def kernel(x_prompt, x_sample, cache_diff_k, cache_diff_v, state_ret_fwd, state_ret_bwd, c, c_ctx, mod_w, mod_b, ln1_g, ln1_b, ln2_g, ln2_b, mix_w_in, mix_w_out, ret_decay_fwd, ret_decay_bwd, diff_lq1, diff_lk1, diff_lq2, diff_lk2, diff_subln_g, conv_w1, conv_b1, conv_dw, conv_dw_b, conv_ln_g, conv_ln_b, conv_w2, conv_b2, router_g_w, router_g_b, router_e_w, router_e_b, moe_w_gate, moe_w_up, moe_w_down):
    xp = x_prompt.reshape(T_PROMPT, D)
    xs = x_sample.reshape(T_SAMPLE, D)
    cond = jnp.concatenate([c_ctx[None, :], c, jnp.zeros((MOD_ROWS - 1 - DEC_BATCH, D), F32)], axis=0)
    modr = _mod_vectors(cond, mod_w, mod_b).reshape(DEPTH * MOD_ROWS * 6, 1, D)
    cos, sin_signed = _rope_tables()

    def row(v):
        return v.reshape(1, -1)

    x = None
    caches = None
    for li in range(DEPTH):
        wr, br = _router_slab(router_g_w[li], router_g_b[li], router_e_w[li], router_e_b[li])
        if li % 2 == 0:
            assert li == 0, "the even mixer reads the kernel inputs directly"
            e = li // 2
            lam_init = 0.8 - 0.6 * math.exp(-0.3 * li)
            proj, ck, cv = _in_proj(xp, xs, modr, mix_w_in[e], li)
            dec = jnp.concatenate([ret_decay_fwd[e], ret_decay_bwd[e]])
            r_p, sf, sb = _retention(proj, dec, BATCH, SEQ, 0, HEADS, emit_state=True)
            (r_s,) = _retention(proj, dec, DEC_BATCH, DEC_SEQ, T_PROMPT // DEC_SEQ, 2,
                                s0f=state_ret_fwd, s0b=state_ret_bwd, e=e)
            lams = (row(diff_lq1[e]), row(diff_lk1[e]), row(diff_lq2[e]), row(diff_lk2[e]),
                    row(diff_subln_g[e]))
            o_p = _attn_prompt(proj, *lams, lam_init)
            o_s = _attn_sample(proj, cache_diff_k, cache_diff_v, cos, sin_signed, *lams, lam_init, e)
            x1, u2, meta, cnt = _out_proj_tail(r_p, r_s, o_p, o_s, mix_w_out[e], xp, xs, modr,
                                               row(ln1_g[li]), row(ln1_b[li]), wr, br, li)
            caches = (ck, cv, sf, sb)
        else:
            o = li // 2
            x1, u2, meta, cnt = _conv_tail(glu, conv_dw[o], row(conv_dw_b[o]), row(conv_ln_g[o]),
                                           row(conv_ln_b[o]), conv_w2[o], row(conv_b2[o]),
                                           x, modr, row(ln1_g[li]), row(ln1_b[li]), wr, br, li)
        next_conv = {}
        if li + 1 < DEPTH and (li + 1) % 2 == 1:
            next_conv = dict(conv_w1=conv_w1[(li + 1) // 2], conv_b1=row(conv_b1[(li + 1) // 2]))
        outs = _moe(x1, u2, meta, cnt, modr, moe_w_gate, moe_w_up, moe_w_down,
                    row(ln2_g[li]), row(ln2_b[li]), li, split=(li == DEPTH - 1), **next_conv)
        x, glu = outs[0], outs[-1]

    y_prompt = outs[0].reshape(BATCH, SEQ, D)
    y_sample = outs[1].reshape(DEC_BATCH, DEC_SEQ, D)
    return (y_prompt, y_sample) + caches
```

```python
import functools
import math

import numpy as np
import jax
import jax.numpy as jnp
from jax import lax
from jax.experimental import pallas as pl
from jax.experimental.pallas import tpu as pltpu

F32 = jnp.float32
BF16 = jnp.bfloat16

D = 1024
BATCH = 16
SEQ = 256
DEPTH = 2
DEC_BATCH = 2
DEC_SEQ = 2048
PAST_LEN = 512
GRID_W = 64
HEADS = 4
HEAD_W = 128
RET_CHUNK = 128
DIFF_DK = 64
ROPE_THETA = 10000.0
IN_W = 7 * HEADS * HEAD_W
CONV_K = 31
CONV_PAD = CONV_K // 2
N_GROUPS = 4
EXPERTS_PER_GROUP = 8
N_EXPERTS = N_GROUPS * EXPERTS_PER_GROUP
D_EXPERT = 512
ALPHA = (2.0 * DEPTH) ** 0.25
LN_EPS = 1e-5
GN_EPS = 1e-6

T_PROMPT = BATCH * SEQ
T_SAMPLE = DEC_BATCH * DEC_SEQ
T = T_PROMPT + T_SAMPLE
TM = 256
N_TILES = T // TM
PROMPT_TILES = T_PROMPT // TM
SAMPLE_TILES_PER_SEQ = DEC_SEQ // TM
MOD_ROWS = 8
MOE_TM = 256
LANES = 128
SUBLANES = 8
RUN_ALIGN = SUBLANES
SORT_ROWS = -(-(2 * TM + N_EXPERTS * (RUN_ALIGN - 1)) // TM) * TM
RUN_BITS = tuple(1 << b for b in range((2 * TM).bit_length() - 1, RUN_ALIGN.bit_length() - 2, -1))
MOE_MAX_TILES = -(-(2 * T + N_TILES * N_EXPERTS * (RUN_ALIGN - 1) + N_EXPERTS * (MOE_TM - RUN_ALIGN)) // MOE_TM)
MOE_ROWS = MOE_MAX_TILES * MOE_TM
SORT_GROUPS = SORT_ROWS // RUN_ALIGN
GROUP_UNROLL = 4
DISPATCH_ROWS = MOE_ROWS + N_TILES * RUN_ALIGN * GROUP_UNROLL
MOE_AHEAD = 8
MOE_IN_SLOTS = MOE_AHEAD + 1
MOE_OUT_SLOTS = 4
ROUTER_LANE0 = N_GROUPS
VMEM_LIMIT = 52 * 1024 * 1024


def _cparams(sem):
    return pltpu.CompilerParams(dimension_semantics=sem, vmem_limit_bytes=VMEM_LIMIT)


def _tile_cond_row(i):
    return jnp.where(i < PROMPT_TILES, 0, 1 + (i - PROMPT_TILES) // SAMPLE_TILES_PER_SEQ)


def _mod_spec(li, k):
    return pl.BlockSpec((1, 1, D), lambda i, *_: ((li * MOD_ROWS + _tile_cond_row(i)) * 6 + k, 0, 0))


def _row_spec(shape):
    return pl.BlockSpec(shape, lambda i, *_: (0,) * len(shape))


def _resident_f32_weight(shape):
    return pl.BlockSpec(shape, lambda i, *_: (0,) * len(shape), pipeline_mode=pl.Buffered(1))


def _cast_weight_once(w_ref, w_bf):
    @pl.when(pl.program_id(0) == 0)
    def _():
        w_bf[...] = w_ref[...].astype(BF16)


def _layer_norm(x, g, b):
    mu = jnp.mean(x, axis=-1, keepdims=True)
    xc = x - mu
    var = jnp.mean(xc * xc, axis=-1, keepdims=True)
    return xc * lax.rsqrt(var + LN_EPS) * g + b


def _silu(x):
    return x * jax.nn.sigmoid(x)


def _dot(a, b):
    return jnp.dot(a, b, preferred_element_type=F32)


def _dot_nt(a, b):
    return lax.dot_general(a, b, (((1,), (1,)), ((), ())), preferred_element_type=F32)


def _dot_tn(a, b):
    return lax.dot_general(a, b, (((0,), (0,)), ((), ())), preferred_element_type=F32)


MOD_TN = 1024
MOD_USED_ROWS = 1 + DEC_BATCH


def _mod_kernel(cond_t_ref, w_ref, b_ref, o_ref):
    s = _silu(cond_t_ref[...])
    w = w_ref[0]
    o_ref[0] = jnp.zeros((MOD_ROWS, MOD_TN), F32) + b_ref[0]
    for r in range(MOD_USED_ROWS):
        o_ref[0, r:r + 1, :] = jnp.sum(w * s[:, r:r + 1], axis=0, keepdims=True) + b_ref[0]


def _mod_vectors(cond, mod_w, mod_b):
    return pl.pallas_call(
        _mod_kernel,
        grid=(DEPTH, 6 * D // MOD_TN),
        in_specs=[
            pl.BlockSpec((D, MOD_ROWS), lambda l, j: (0, 0)),
            pl.BlockSpec((1, D, MOD_TN), lambda l, j: (l, 0, j)),
            pl.BlockSpec((1, 1, MOD_TN), lambda l, j: (l, 0, j)),
        ],
        out_specs=pl.BlockSpec((1, MOD_ROWS, MOD_TN), lambda l, j: (l, 0, j)),
        out_shape=jax.ShapeDtypeStruct((DEPTH, MOD_ROWS, 6 * D), F32),
        compiler_params=_cparams(("arbitrary", "arbitrary")),
        name="mod_vectors",
    )(cond.T, mod_w, mod_b.reshape(DEPTH, 1, 6 * D))


def _prompt_tile_spec(width):
    return pl.BlockSpec((TM, width), lambda i, *_: (jnp.minimum(i, PROMPT_TILES - 1), 0))


def _sample_tile_spec(width):
    return pl.BlockSpec((TM, width), lambda i, *_: (jnp.maximum(i - PROMPT_TILES, 0), 0))


def _pick_tile(prompt_ref, sample_ref):
    return jnp.where(pl.program_id(0) < PROMPT_TILES, prompt_ref[...], sample_ref[...])


def _in_proj_kernel(xp_ref, xs_ref, sh_ref, sc_ref, w_ref, o_ref, ck_ref, cv_ref, w_bf):
    _cast_weight_once(w_ref, w_bf)
    u = _pick_tile(xp_ref, xs_ref) * (1.0 + sc_ref[0]) + sh_ref[0]
    proj = _dot(u.astype(BF16), w_bf[...])
    o_ref[...] = proj.astype(BF16)

    @pl.when(pl.program_id(0) < PROMPT_TILES)
    def _():
        for h in range(HEADS):
            ck_ref[0, 0, h] = proj[:, (COL_KD + h) * HEAD_W:(COL_KD + h + 1) * HEAD_W]
            cv_ref[0, 0, h] = proj[:, (COL_VD + h) * HEAD_W:(COL_VD + h + 1) * HEAD_W]


def _in_proj(x_prompt, x_sample, modr, w_in, li):
    cache_spec = pl.BlockSpec((1, 1, HEADS, SEQ, HEAD_W),
                              lambda i: (jnp.minimum(i, PROMPT_TILES - 1), 0, 0, 0, 0))
    cache_shape = jax.ShapeDtypeStruct((BATCH, 1, HEADS, SEQ, HEAD_W), F32)
    return pl.pallas_call(
        _in_proj_kernel,
        grid=(N_TILES,),
        in_specs=[
            _prompt_tile_spec(D), _sample_tile_spec(D),
            _mod_spec(li, 0),
            _mod_spec(li, 1),
            _resident_f32_weight((D, IN_W)),
        ],
        out_specs=[pl.BlockSpec((TM, IN_W), lambda i: (i, 0)), cache_spec, cache_spec],
        out_shape=[jax.ShapeDtypeStruct((T, IN_W), BF16), cache_shape, cache_shape],
        scratch_shapes=[pltpu.VMEM((D, IN_W), BF16)],
        compiler_params=_cparams(("arbitrary",)),
        name="in_proj",
    )(x_prompt, x_sample, modr, modr, w_in)


COL_QR, COL_KR, COL_VR, COL_GR, COL_QD, COL_KD, COL_VD = (k * HEADS for k in range(7))


def _retention_kernel(dec_ref, q_ref, k_ref, v_ref, g_ref, *rest, n_chunks, n_heads, has_state, emit_state):
    rest = list(rest)
    if has_state:
        s0f_ref, s0b_ref = rest[:2]
        rest = rest[2:]
    r_ref = rest[0]
    rest = rest[1:]
    if emit_state:
        sf_ref, sb_ref = rest[:2]
        rest = rest[2:]
    of_ref = rest[0]

    head0 = pl.program_id(1) * n_heads
    C = RET_CHUNK
    ii = lax.broadcasted_iota(jnp.int32, (C, C), 0)
    jj = lax.broadcasted_iota(jnp.int32, (C, C), 1)
    rel = (ii - jj).astype(F32)
    idx = lax.broadcasted_iota(jnp.int32, (C, 1), 0).astype(F32)
    k_scale = HEAD_W ** -0.5

    def chunk(ref, c, h):
        return ref[c * C:(c + 1) * C, h * HEAD_W:(h + 1) * HEAD_W].astype(F32)

    def decays(direction, h):
        lg = -jnp.exp(jnp.full((1, 1), dec_ref[direction * HEADS + head0 + h], F32))
        if direction == 0:
            inner = jnp.where(rel >= 0, jnp.exp(jnp.maximum(rel, 0.0) * lg), 0.0)
            return inner, jnp.exp((idx + 1.0) * lg), jnp.exp((C - 1.0 - idx) * lg), jnp.exp(C * lg)
        inner = jnp.where(rel <= 0, jnp.exp(jnp.maximum(-rel, 0.0) * lg), 0.0)
        return inner, jnp.exp((C - idx) * lg), jnp.exp(idx * lg), jnp.exp(C * lg)

    def run(direction):
        dec = [decays(direction, h) for h in range(n_heads)]
        if has_state:
            s0_ref = s0f_ref if direction == 0 else s0b_ref
            states = [s0_ref[0, 0, h] for h in range(n_heads)]
        else:
            states = [jnp.zeros((HEAD_W, HEAD_W), F32) for _ in range(n_heads)]
        order = range(n_chunks) if direction == 0 else range(n_chunks - 1, -1, -1)
        for c in order:
            rows = slice(c * C, (c + 1) * C)
            for h in range(n_heads):
                inner, q_decay, k_decay, chunk_decay = dec[h]
                cols = slice(h * HEAD_W, (h + 1) * HEAD_W)
                s = states[h]
                qc = chunk(q_ref, c, h)
                kc = chunk(k_ref, c, h) * k_scale
                vc = chunk(v_ref, c, h).astype(BF16)
                scores = _dot_nt(qc.astype(BF16), kc.astype(BF16)) * inner
                o = _dot(scores.astype(BF16), vc) + _dot((qc * q_decay).astype(BF16), s.astype(BF16))
                states[h] = s * chunk_decay + _dot_tn((kc * k_decay).astype(BF16), vc)
                if direction == 0:
                    of_ref[rows, cols] = o
                else:
                    r = of_ref[rows, cols] + o
                    mu = jnp.mean(r, axis=-1, keepdims=True)
                    rc = r - mu
                    var = jnp.mean(rc * rc, axis=-1, keepdims=True)
                    rn = rc * lax.rsqrt(var + GN_EPS)
                    r_ref[rows, cols] = _silu(chunk(g_ref, c, h)) * rn
        return states

    sf = run(0)
    sb = run(1)
    if emit_state:
        for h in range(n_heads):
            sf_ref[0, 0, h] = sf[h]
            sb_ref[0, 0, h] = sb[h]


def _retention(proj, dec, n_seq, seq_len, row_block0, n_heads, s0f=None, s0b=None, e=0, emit_state=False):
    has_state = s0f is not None
    width = n_heads * HEAD_W

    def col(base):
        return pl.BlockSpec((seq_len, width), lambda b, h, *_: (row_block0 + b, base // n_heads + h))

    state_spec = pl.BlockSpec((1, 1, n_heads, HEAD_W, HEAD_W), lambda b, h, *_: (b, e, h, 0, 0))
    in_specs = [pl.BlockSpec(memory_space=pltpu.SMEM), col(COL_QR), col(COL_KR), col(COL_VR), col(COL_GR)]
    args = [dec, proj, proj, proj, proj]
    if has_state:
        in_specs += [state_spec, state_spec]
        args += [s0f, s0b]
    out_specs = [pl.BlockSpec((seq_len, width), lambda b, h, *_: (b, h))]
    out_shape = [jax.ShapeDtypeStruct((n_seq * seq_len, HEADS * HEAD_W), F32)]
    if emit_state:
        st = pl.BlockSpec((1, 1, n_heads, HEAD_W, HEAD_W), lambda b, h, *_: (b, 0, h, 0, 0))
        out_specs += [st, st]
        out_shape += [jax.ShapeDtypeStruct((n_seq, 1, HEADS, HEAD_W, HEAD_W), F32)] * 2
    return pl.pallas_call(
        functools.partial(_retention_kernel, n_chunks=seq_len // RET_CHUNK, n_heads=n_heads,
                          has_state=has_state, emit_state=emit_state),
        grid=(n_seq, HEADS // n_heads),
        in_specs=in_specs,
        out_specs=out_specs,
        out_shape=out_shape,
        scratch_shapes=[pltpu.VMEM((seq_len, width), F32)],
        compiler_params=_cparams(("arbitrary", "arbitrary")),
        name=f"retention_{seq_len}",
    )(*args)


def _diff_lambda(lq1_ref, lk1_ref, lq2_ref, lk2_ref, lam_init):
    a = jnp.sum(lq1_ref[...] * lk1_ref[...], axis=-1, keepdims=True)
    b = jnp.sum(lq2_ref[...] * lk2_ref[...], axis=-1, keepdims=True)
    return jnp.exp(a) - jnp.exp(b) + lam_init


LOG2E = 1.4426950408889634


def _diff_attend(q, k, v, lam, subln_g, lam_init):
    lane = lax.broadcasted_iota(jnp.int32, q.shape, 1)
    q1 = jnp.where(lane < DIFF_DK, q, 0.0).astype(BF16)
    q2 = jnp.where(lane >= DIFF_DK, q, 0.0).astype(BF16)

    def softmax_times_v(qz):
        s = _dot_nt(qz, k)
        p = jnp.exp2(s - jnp.max(s, axis=-1, keepdims=True))
        return _dot(p.astype(BF16), v) * (1.0 / jnp.sum(p, axis=-1, keepdims=True))

    o = softmax_times_v(q1) - lam * softmax_times_v(q2)
    o = o * lax.rsqrt(jnp.mean(o * o, axis=-1, keepdims=True) + LN_EPS)
    return o * subln_g * (1.0 - lam_init)


def _attn_prompt_kernel(q_ref, k_ref, v_ref, lq1, lk1, lq2, lk2, g_ref, o_ref, *, lam_init):
    lam = _diff_lambda(lq1, lk1, lq2, lk2, lam_init)
    scale = DIFF_DK ** -0.5 * LOG2E
    for h in range(HEADS):
        sl = slice(h * HEAD_W, (h + 1) * HEAD_W)
        o_ref[:, sl] = _diff_attend(q_ref[:, sl].astype(F32) * scale, k_ref[:, sl], v_ref[:, sl],
                                    lam, g_ref[...], lam_init)


def _attn_prompt(proj, lq1, lk1, lq2, lk2, subln_g, lam_init):
    W = HEADS * HEAD_W

    def slab(base):
        return pl.BlockSpec((SEQ, W), lambda b: (b, base // HEADS))

    small = _row_spec((1, DIFF_DK))
    return pl.pallas_call(
        functools.partial(_attn_prompt_kernel, lam_init=lam_init),
        grid=(BATCH,),
        in_specs=[slab(COL_QD), slab(COL_KD), slab(COL_VD), small, small, small, small,
                  _row_spec((1, HEAD_W))],
        out_specs=pl.BlockSpec((SEQ, W), lambda b: (b, 0)),
        out_shape=jax.ShapeDtypeStruct((T_PROMPT, W), F32),
        compiler_params=_cparams(("arbitrary",)),
        name="diff_attn_prompt",
    )(proj, proj, proj, lq1, lk1, lq2, lk2, subln_g)


def _rope(x, cos, sin_signed):
    lane = lax.broadcasted_iota(jnp.int32, x.shape, 1)
    partner = jnp.where((lane % 32) < 16, pltpu.roll(x, LANES - 16, 1), pltpu.roll(x, 16, 1))
    return x * cos + partner * sin_signed


def _attn_sample_kernel(q_ref, k_ref, v_ref, ck_ref, cv_ref, cosq_ref, sinq_ref, cos_ref, sin_ref,
                        lq1, lk1, lq2, lk2, g_ref, o_ref, kbuf, vbuf, *, lam_init):
    @pl.when(pl.program_id(2) == 0)
    def _():
        kbuf[0:DEC_SEQ, :] = _rope(k_ref[...].astype(F32), cos_ref[...], sin_ref[...]).astype(BF16)
        kbuf[DEC_SEQ:, :] = ck_ref[0, 0, 0].astype(BF16)
        vbuf[0:DEC_SEQ, :] = v_ref[...]
        vbuf[DEC_SEQ:, :] = cv_ref[0, 0, 0].astype(BF16)

    lam = _diff_lambda(lq1, lk1, lq2, lk2, lam_init)
    q = _rope(q_ref[...].astype(F32), cosq_ref[...], sinq_ref[...]) * (DIFF_DK ** -0.5 * LOG2E)
    o_ref[...] = _diff_attend(q, kbuf[...], vbuf[...], lam, g_ref[...], lam_init)


ATTN_TQ = 256


def _attn_sample(proj, cache_k, cache_v, cos, sin_signed, lq1, lk1, lq2, lk2, subln_g, lam_init, e):
    nq = DEC_SEQ // ATTN_TQ
    row0_q = T_PROMPT // ATTN_TQ
    row0_kv = T_PROMPT // DEC_SEQ
    small = pl.BlockSpec((1, DIFF_DK), lambda b, h, t: (0, 0))
    cache = pl.BlockSpec((1, 1, 1, PAST_LEN, HEAD_W), lambda b, h, t: (b, e, h, 0, 0))
    table_q = pl.BlockSpec((ATTN_TQ, HEAD_W), lambda b, h, t: (t, 0))
    table = pl.BlockSpec((DEC_SEQ, HEAD_W), lambda b, h, t: (0, 0))
    return pl.pallas_call(
        functools.partial(_attn_sample_kernel, lam_init=lam_init),
        grid=(DEC_BATCH, HEADS, nq),
        in_specs=[
            pl.BlockSpec((ATTN_TQ, HEAD_W), lambda b, h, t: (row0_q + b * nq + t, COL_QD + h)),
            pl.BlockSpec((DEC_SEQ, HEAD_W), lambda b, h, t: (row0_kv + b, COL_KD + h)),
            pl.BlockSpec((DEC_SEQ, HEAD_W), lambda b, h, t: (row0_kv + b, COL_VD + h)),
            cache, cache, table_q, table_q, table, table,
            small, small, small, small,
            pl.BlockSpec((1, HEAD_W), lambda b, h, t: (0, 0)),
        ],
        out_specs=pl.BlockSpec((ATTN_TQ, HEAD_W), lambda b, h, t: (b * nq + t, h)),
        out_shape=jax.ShapeDtypeStruct((T_SAMPLE, HEADS * HEAD_W), F32),
        scratch_shapes=[pltpu.VMEM((DEC_SEQ + PAST_LEN, HEAD_W), BF16),
                        pltpu.VMEM((DEC_SEQ + PAST_LEN, HEAD_W), BF16)],
        compiler_params=_cparams(("arbitrary", "arbitrary", "arbitrary")),
        name="diff_attn_sample",
    )(proj, proj, proj, cache_k, cache_v, cos, sin_signed, cos, sin_signed,
      lq1, lk1, lq2, lk2, subln_g)


def _rope_tables():
    t = np.arange(DEC_SEQ)
    row, colp = t // GRID_W, t % GRID_W
    lane = np.arange(LANES)
    pos = np.where(((lane // 32) % 2 == 0)[None, :], row[:, None], colp[:, None]).astype(np.float64)
    half = 16
    inv = (np.float32(ROPE_THETA) ** (-(np.arange(half, dtype=np.float32)) / np.float32(half))).astype(np.float32)
    ang = pos.astype(np.float32) * inv[lane % half][None, :]
    cos = np.cos(ang.astype(np.float64)).astype(np.float32)
    sin = np.sin(ang.astype(np.float64)).astype(np.float32)
    sign = np.where((lane % 32) < half, -1.0, 1.0).astype(np.float32)[None, :]
    return jnp.asarray(cos), jnp.asarray(sin * sign)


def _split_bf16(a):
    hi = a.astype(BF16)
    return hi, (a - hi.astype(F32)).astype(BF16)


def _mixer_tail(out, x, g1_ref, sc2_ref, sh2_ref, lng_ref, lnb_ref, wr_ref, br_ref,
                x1_ref, u2_ref, meta_ref, cnt_ref):
    x1 = _layer_norm(ALPHA * x + g1_ref[0] * out, lng_ref[...], lnb_ref[...])
    x1_ref[...] = x1
    u2 = x1 * (1.0 + sc2_ref[0]) + sh2_ref[0]
    u2_ref[...] = u2.astype(BF16)

    u_hi, u_lo = _split_bf16(u2)
    w_hi, w_lo = _split_bf16(wr_ref[...])
    logits = _dot(u_hi, w_hi) + (_dot(u_hi, w_lo) + _dot(u_lo, w_hi)) + br_ref[...]
    lane = lax.broadcasted_iota(jnp.int32, logits.shape, 1).astype(F32)
    neg = jnp.float32(-jnp.inf)
    is_g = lane < N_GROUPS
    gl = jnp.where(is_g, logits, neg)
    gmax = jnp.max(gl, axis=-1, keepdims=True)
    gsel = jnp.min(jnp.where(gl == gmax, lane, float(LANES)), axis=-1, keepdims=True)
    p_g = 1.0 / jnp.sum(jnp.where(is_g, jnp.exp(gl - gmax), 0.0), axis=-1, keepdims=True)
    lo = ROUTER_LANE0 + gsel * EXPERTS_PER_GROUP
    el = jnp.where((lane >= lo) & (lane < lo + EXPERTS_PER_GROUP), logits, neg)
    v1 = jnp.max(el, axis=-1, keepdims=True)
    i1 = jnp.min(jnp.where(el == v1, lane, float(LANES)), axis=-1, keepdims=True)
    el2 = jnp.where(lane == i1, neg, el)
    v2 = jnp.max(el2, axis=-1, keepdims=True)
    i2 = jnp.min(jnp.where(el2 == v2, lane, float(LANES)), axis=-1, keepdims=True)
    t = jnp.exp(v2 - v1)
    w1 = p_g / (1.0 + t)
    w2 = w1 * t

    oh1 = (lane == i1).astype(F32)
    oh2 = (lane == i2).astype(F32)
    oh = oh1 + oh2
    r_i = lax.broadcasted_iota(jnp.int32, (TM, TM), 0)
    c_i = lax.broadcasted_iota(jnp.int32, (TM, TM), 1)
    before = (c_i < r_i).astype(BF16)
    earlier = _dot(before, oh.astype(BF16))
    rank1 = jnp.sum(earlier * oh1, axis=-1, keepdims=True)
    rank2 = jnp.sum(earlier * oh2, axis=-1, keepdims=True)
    cnt_ref[0] = jnp.sum(oh, axis=0, keepdims=True)
    cols = (i1, i2, w1, w2, rank1, rank2)
    meta = jnp.zeros_like(logits)
    for k, col in enumerate(cols):
        meta = jnp.where(lane == k, col, meta)
    meta_ref[...] = meta


META_E1, META_E2, META_W1, META_W2, META_RANK1, META_RANK2 = range(6)

_TAIL_OUT_SHAPES = [
    jax.ShapeDtypeStruct((T, D), F32),
    jax.ShapeDtypeStruct((T, D), BF16),
    jax.ShapeDtypeStruct((T, LANES), F32),
    jax.ShapeDtypeStruct((N_TILES, 1, LANES), F32),
]


def _tail_out_specs():
    return [
        pl.BlockSpec((TM, D), lambda i: (i, 0)),
        pl.BlockSpec((TM, D), lambda i: (i, 0)),
        pl.BlockSpec((TM, LANES), lambda i: (i, 0)),
        pl.BlockSpec((1, 1, LANES), lambda i: (i, 0, 0)),
    ]


def _tail_in_specs(li):
    return [
        _mod_spec(li, 2), _mod_spec(li, 4), _mod_spec(li, 3),
        _row_spec((1, D)), _row_spec((1, D)),
        _row_spec((D, LANES)), _row_spec((1, LANES)),
    ]


def _out_proj_kernel(rp_ref, rs_ref, op_ref, os_ref, w_ref, xp_ref, xs_ref, *rest):
    tail_args, w_bf = rest[:-1], rest[-1]
    _cast_weight_once(w_ref, w_bf)
    half = HEADS * HEAD_W
    r = _pick_tile(rp_ref, rs_ref).astype(BF16)
    o = _pick_tile(op_ref, os_ref).astype(BF16)
    out = _dot(r, w_bf[0:half, :]) + _dot(o, w_bf[half:, :])
    _mixer_tail(out, _pick_tile(xp_ref, xs_ref), *tail_args)


def _out_proj_tail(r_p, r_s, o_p, o_s, w_out, x_prompt, x_sample, modr, ln_g, ln_b, wr, br, li):
    half = HEADS * HEAD_W
    return pl.pallas_call(
        _out_proj_kernel,
        grid=(N_TILES,),
        in_specs=[_prompt_tile_spec(half), _sample_tile_spec(half),
                  _prompt_tile_spec(half), _sample_tile_spec(half),
                  _resident_f32_weight((2 * half, D)),
                  _prompt_tile_spec(D), _sample_tile_spec(D)] + _tail_in_specs(li),
        out_specs=_tail_out_specs(),
        out_shape=_TAIL_OUT_SHAPES,
        scratch_shapes=[pltpu.VMEM((2 * half, D), BF16)],
        compiler_params=_cparams(("arbitrary",)),
        name="out_proj_tail",
    )(r_p, r_s, o_p, o_s, w_out, x_prompt, x_sample, modr, modr, modr, ln_g, ln_b, wr, br)


def _conv_glu(y, sh_ref, sc_ref, w_bf, b_ref):
    u = y * (1.0 + sc_ref[0]) + sh_ref[0]
    h = _dot(u.astype(BF16), w_bf[...]) + b_ref[...]
    return h[:, :D] * jax.nn.sigmoid(h[:, D:])


HALO = 16
CONV_ROWS = 64
CONV_COLS = 128


def _depthwise_conv(hp, dw_ref, conv):
    base = HALO - CONV_PAD
    for cb in range(D // CONV_COLS):
        cs = slice(cb * CONV_COLS, (cb + 1) * CONV_COLS)
        for rb in range(TM // CONV_ROWS):
            r0 = rb * CONV_ROWS
            acc = None
            for shift in range(SUBLANES):
                part = None
                for tap in range(CONV_K):
                    off = base + tap
                    if off % SUBLANES != shift:
                        continue
                    a0 = r0 + off - shift
                    term = hp[a0:a0 + CONV_ROWS + SUBLANES, cs] * dw_ref[tap:tap + 1, cs]
                    part = term if part is None else part + term
                part = part[shift:shift + CONV_ROWS, :]
                acc = part if acc is None else acc + part
            conv[r0:r0 + CONV_ROWS, cs] = acc


def _conv_tail_kernel(cur_ref, prev_ref, next_ref, dw_ref, dwb_ref, cg_ref, cb_ref, w2_ref, b2_ref,
                      x_ref, *rest):
    tail_args, (hp, conv, w2_bf) = rest[:-3], rest[-3:]
    _cast_weight_once(w2_ref, w2_bf)
    i = pl.program_id(0)
    k = (i - PROMPT_TILES) % SAMPLE_TILES_PER_SEQ
    in_sample = i >= PROMPT_TILES
    left_ok = jnp.logical_and(in_sample, k != 0)
    right_ok = jnp.logical_and(in_sample, k != SAMPLE_TILES_PER_SEQ - 1)
    hp[0:HALO, :] = jnp.where(left_ok, prev_ref[...], 0.0)
    hp[HALO:HALO + TM, :] = cur_ref[...]
    hp[HALO + TM:HALO + TM + HALO, :] = jnp.where(right_ok, next_ref[...], 0.0)
    _depthwise_conv(hp, dw_ref, conv)
    hc = _silu(_layer_norm(conv[...] + dwb_ref[...], cg_ref[...], cb_ref[...]))
    out = _dot(hc.astype(BF16), w2_bf[...]) + b2_ref[...]
    _mixer_tail(out, x_ref[...], *tail_args)


def _conv_tail(glu, dw, dwb, cg, cb, w2, b2, x, modr, ln_g, ln_b, wr, br, li):
    per = TM // HALO
    last = T // HALO - 1
    return pl.pallas_call(
        _conv_tail_kernel,
        grid=(N_TILES,),
        in_specs=[pl.BlockSpec((TM, D), lambda i: (i, 0)),
                  pl.BlockSpec((HALO, D), lambda i: (jnp.maximum(i * per - 1, 0), 0)),
                  pl.BlockSpec((HALO, D), lambda i: (jnp.minimum((i + 1) * per, last), 0)),
                  _row_spec((CONV_K, D)), _row_spec((1, D)), _row_spec((1, D)), _row_spec((1, D)),
                  _resident_f32_weight((D, D)), _row_spec((1, D)),
                  pl.BlockSpec((TM, D), lambda i: (i, 0))] + _tail_in_specs(li),
        out_specs=_tail_out_specs(),
        out_shape=_TAIL_OUT_SHAPES,
        scratch_shapes=[pltpu.VMEM((TM + 2 * HALO, D), F32), pltpu.VMEM((TM, D), F32),
                        pltpu.VMEM((D, D), BF16)],
        compiler_params=_cparams(("arbitrary",)),
        name="conv_tail",
    )(glu, glu, glu, dw, dwb, cg, cb, w2, b2, x, modr, modr, modr, ln_g, ln_b, wr, br)


def _sorted_positions(meta, srcv):
    lane = lax.broadcasted_iota(jnp.int32, meta.shape, 1).astype(F32)

    def pos(e_col, r_col):
        start = jnp.sum(jnp.where(lane == meta[:, e_col:e_col + 1], srcv, 0.0), axis=-1, keepdims=True)
        return start + meta[:, r_col:r_col + 1]

    return pos(META_E1, META_RANK1), pos(META_E2, META_RANK2)


def _one_hot_rows(pos):
    col = lax.broadcasted_iota(jnp.int32, (TM, SORT_ROWS), 1).astype(F32)
    return col == pos


def _for_each_row_group(tile, tot_ref, dstg_ref, fn):
    def body(k, carry):
        for j in range(GROUP_UNROLL):
            g = k * GROUP_UNROLL + j
            fn(pl.multiple_of(g * RUN_ALIGN, RUN_ALIGN),
               pl.multiple_of(dstg_ref[tile * SORT_GROUPS + g], RUN_ALIGN))
        return carry

    lax.fori_loop(0, tot_ref[tile] // (RUN_ALIGN * GROUP_UNROLL), body, 0)


def _wait_rows(total, make_copy):
    for bit in RUN_BITS:
        @pl.when((total & bit) != 0)
        def _(bit=bit):
            make_copy(bit).wait()


def _dispatch_kernel(tot_ref, dstg_ref, u_ref, meta_ref, srcv_ref, xs_ref, sorted_ref, sems):
    i = pl.program_id(0)
    slot = i % 2

    def wait_tile(tile, slot):
        buf = sorted_ref.at[slot]
        _wait_rows(tot_ref[tile], lambda rows: pltpu.make_async_copy(
            buf.at[pl.ds(0, rows)], xs_ref.at[pl.ds(0, rows)], sems.at[slot]))

    @pl.when(i >= 2)
    def _():
        wait_tile(i - 2, slot)

    pos1, pos2 = _sorted_positions(meta_ref[...], srcv_ref[0])
    select = jnp.logical_or(_one_hot_rows(pos1), _one_hot_rows(pos2)).astype(BF16)
    sorted_ref[slot] = _dot_tn(select, u_ref[...])
    buf = sorted_ref.at[slot]

    def start(src, dst):
        pltpu.make_async_copy(buf.at[pl.ds(src, RUN_ALIGN)], xs_ref.at[pl.ds(dst, RUN_ALIGN)],
                              sems.at[slot]).start()

    _for_each_row_group(i, tot_ref, dstg_ref, start)

    @pl.when(i == N_TILES - 1)
    def _():
        wait_tile(i - 1, 1 - slot)
        wait_tile(i, slot)


def _dispatch(sched, u2, meta, srcv):
    return pl.pallas_call(
        _dispatch_kernel,
        grid_spec=pltpu.PrefetchScalarGridSpec(
            num_scalar_prefetch=2,
            grid=(N_TILES,),
            in_specs=[pl.BlockSpec((TM, D), lambda i, *_: (i, 0)),
                      pl.BlockSpec((TM, LANES), lambda i, *_: (i, 0)),
                      pl.BlockSpec((1, 1, LANES), lambda i, *_: (i, 0, 0))],
            out_specs=pl.BlockSpec(memory_space=pl.ANY),
            scratch_shapes=[pltpu.VMEM((2, SORT_ROWS, D), F32), pltpu.SemaphoreType.DMA((2,))],
        ),
        out_shape=jax.ShapeDtypeStruct((DISPATCH_ROWS, D), F32),
        compiler_params=_cparams(("arbitrary",)),
        name="moe_dispatch",
    )(*sched, u2, meta, srcv)


def _experts_kernel(start_ref, chunks_ref, xs_ref, wg_ref, wu_ref, wd_ref, ys_ref,
                    wg_bf, wu_bf, wd_bf, xbuf, ybuf, in_sems, out_sems):
    e = pl.program_id(0)
    n = chunks_ref[e]
    first = start_ref[e] // MOE_TM
    total = start_ref[N_EXPERTS - 1] // MOE_TM + chunks_ref[N_EXPERTS - 1]

    def rows(g):
        return pl.ds(pl.multiple_of(g * MOE_TM, MOE_TM), MOE_TM)

    def load(g):
        slot = g % MOE_IN_SLOTS
        return pltpu.make_async_copy(xs_ref.at[rows(g)], xbuf.at[slot], in_sems.at[slot])

    def store(g):
        slot = g % MOE_OUT_SLOTS
        return pltpu.make_async_copy(ybuf.at[slot], ys_ref.at[rows(g)], out_sems.at[slot])

    @pl.when(e == 0)
    def _():
        for g in range(MOE_AHEAD):
            @pl.when(g < total)
            def _(g=g):
                load(g).start()

    @pl.when(n > 0)
    def _():
        wg_bf[...] = wg_ref[0, 0].astype(BF16)
        wu_bf[...] = wu_ref[0, 0].astype(BF16)
        wd_bf[...] = wd_ref[0, 0].astype(BF16)

        def tile(g, carry):
            load(g).wait()

            @pl.when(g + MOE_AHEAD < total)
            def _():
                load(g + MOE_AHEAD).start()

            @pl.when(g >= MOE_OUT_SLOTS)
            def _():
                store(g - MOE_OUT_SLOTS).wait()

            x = xbuf[g % MOE_IN_SLOTS].astype(BF16)
            h = (_silu(_dot(x, wg_bf[...])) * _dot(x, wu_bf[...])).astype(BF16)
            ybuf[g % MOE_OUT_SLOTS] = _dot(h, wd_bf[...])
            store(g).start()
            return carry

        lax.fori_loop(first, first + n, tile, 0)

    @pl.when(e == N_EXPERTS - 1)
    def _():
        for back in range(MOE_OUT_SLOTS, 0, -1):
            @pl.when(total >= back)
            def _(back=back):
                store(total - back).wait()


def _experts(seg_start, seg_chunks, xs, w_gate, w_up, w_down, li):
    def weight(shape):
        return pl.BlockSpec((1, 1) + shape, lambda e, *_: (li, e, 0, 0))

    return pl.pallas_call(
        _experts_kernel,
        grid_spec=pltpu.PrefetchScalarGridSpec(
            num_scalar_prefetch=2,
            grid=(N_EXPERTS,),
            in_specs=[pl.BlockSpec(memory_space=pl.ANY),
                      weight((D, D_EXPERT)), weight((D, D_EXPERT)), weight((D_EXPERT, D))],
            out_specs=pl.BlockSpec(memory_space=pl.ANY),
            scratch_shapes=[pltpu.VMEM((D, D_EXPERT), BF16), pltpu.VMEM((D, D_EXPERT), BF16),
                            pltpu.VMEM((D_EXPERT, D), BF16),
                            pltpu.VMEM((MOE_IN_SLOTS, MOE_TM, D), F32),
                            pltpu.VMEM((MOE_OUT_SLOTS, MOE_TM, D), F32),
                            pltpu.SemaphoreType.DMA((MOE_IN_SLOTS,)),
                            pltpu.SemaphoreType.DMA((MOE_OUT_SLOTS,))],
        ),
        out_shape=jax.ShapeDtypeStruct((MOE_ROWS, D), F32),
        compiler_params=_cparams(("arbitrary",)),
        name="moe_experts",
    )(seg_start, seg_chunks, xs, w_gate, w_up, w_down)


def _combine_kernel(tot_ref, dstg_ref, ys_ref, x1_ref, meta_ref, srcv_ref, g2_ref,
                    lng_ref, lnb_ref, *rest, split, feeds_conv):
    if feeds_conv:
        (sh_ref, sc_ref, w1_ref, b1_ref), rest, w1_bf = rest[:4], rest[4:-1], rest[-1]
        _cast_weight_once(w1_ref, w1_bf)
    outs, (sorted_ref, sems) = rest[:-2], rest[-2:]
    i = pl.program_id(0)
    slot = i % 2

    def fetch(tile, slot):
        buf = sorted_ref.at[slot]

        def start(src, dst):
            pltpu.make_async_copy(ys_ref.at[pl.ds(dst, RUN_ALIGN)], buf.at[pl.ds(src, RUN_ALIGN)],
                                  sems.at[slot]).start()

        _for_each_row_group(tile, tot_ref, dstg_ref, start)

    @pl.when(i == 0)
    def _():
        sorted_ref[...] = jnp.zeros_like(sorted_ref)
        fetch(0, 0)

    @pl.when(i + 1 < N_TILES)
    def _():
        fetch(i + 1, 1 - slot)

    meta = meta_ref[...]
    pos1, pos2 = _sorted_positions(meta, srcv_ref[0])
    sel1 = _one_hot_rows(pos1).astype(BF16)
    sel2 = _one_hot_rows(pos2).astype(BF16)
    buf = sorted_ref.at[slot]
    _wait_rows(tot_ref[i], lambda rows: pltpu.make_async_copy(
        ys_ref.at[pl.ds(0, rows)], buf.at[pl.ds(0, rows)], sems.at[slot]))
    ysort = sorted_ref[slot].astype(BF16)
    f = (meta[:, META_W1:META_W1 + 1] * _dot(sel1, ysort)
         + meta[:, META_W2:META_W2 + 1] * _dot(sel2, ysort))
    y = _layer_norm(ALPHA * x1_ref[...] + g2_ref[0] * f, lng_ref[...], lnb_ref[...])
    if split:
        @pl.when(i < PROMPT_TILES)
        def _():
            outs[0][...] = y

        @pl.when(i >= PROMPT_TILES)
        def _():
            outs[1][...] = y
    else:
        outs[0][...] = y
    if feeds_conv:
        outs[-1][...] = _conv_glu(y, sh_ref, sc_ref, w1_bf, b1_ref)


def _combine(sched, ys, x1, meta, srcv, modr, ln_g, ln_b, li, split, conv_w1=None, conv_b1=None):
    feeds_conv = conv_w1 is not None
    tile = pl.BlockSpec((TM, D), lambda i, *_: (i, 0))
    if split:
        out_specs = [_prompt_tile_spec(D), _sample_tile_spec(D)]
        out_shape = [jax.ShapeDtypeStruct((T_PROMPT, D), F32), jax.ShapeDtypeStruct((T_SAMPLE, D), F32)]
    else:
        out_specs = [tile]
        out_shape = [jax.ShapeDtypeStruct((T, D), F32)]
    extra_specs, extra_args, extra_scratch = [], [], []
    if feeds_conv:
        extra_specs = [_mod_spec(li + 1, 0), _mod_spec(li + 1, 1),
                       _resident_f32_weight((D, 2 * D)), _row_spec((1, 2 * D))]
        extra_args = [modr, modr, conv_w1, conv_b1]
        extra_scratch = [pltpu.VMEM((D, 2 * D), BF16)]
        out_specs = out_specs + [tile]
        out_shape = out_shape + [jax.ShapeDtypeStruct((T, D), F32)]
    return pl.pallas_call(
        functools.partial(_combine_kernel, split=split, feeds_conv=feeds_conv),
        grid_spec=pltpu.PrefetchScalarGridSpec(
            num_scalar_prefetch=2,
            grid=(N_TILES,),
            in_specs=[pl.BlockSpec(memory_space=pl.ANY),
                      pl.BlockSpec((TM, D), lambda i, *_: (i, 0)),
                      pl.BlockSpec((TM, LANES), lambda i, *_: (i, 0)),
                      pl.BlockSpec((1, 1, LANES), lambda i, *_: (i, 0, 0)),
                      _mod_spec(li, 5), _row_spec((1, D)), _row_spec((1, D))] + extra_specs,
            out_specs=out_specs,
            scratch_shapes=[pltpu.VMEM((2, SORT_ROWS, D), F32), pltpu.SemaphoreType.DMA((2,))] + extra_scratch,
        ),
        out_shape=out_shape,
        compiler_params=_cparams(("arbitrary",)),
        name="moe_combine",
    )(*sched, ys, x1, meta, srcv, modr, ln_g, ln_b, *extra_args)


def _moe_schedule(tile_counts):
    n = (tile_counts + RUN_ALIGN - 1) // RUN_ALIGN * RUN_ALIGN
    src = jnp.cumsum(n, axis=1) - n
    per_expert = jnp.sum(n, axis=0)
    seg = (per_expert + MOE_TM - 1) // MOE_TM * MOE_TM
    seg_start = jnp.cumsum(seg) - seg
    dst = seg_start[None, :] + jnp.cumsum(n, axis=0) - n
    g_row = jnp.arange(SORT_GROUPS, dtype=jnp.int32) * RUN_ALIGN
    in_run = jnp.logical_and(src[:, None, :] <= g_row[None, :, None],
                             g_row[None, :, None] < (src + n)[:, None, :])
    dst_g = g_row[None, :] + jnp.sum(jnp.where(in_run, (dst - src)[:, None, :], 0), axis=2)
    step_rows = RUN_ALIGN * GROUP_UNROLL
    rows = jnp.sum(n, axis=1)
    issued = (rows + step_rows - 1) // step_rows * step_rows
    real = g_row[None, :] < rows[:, None]
    spill = (MOE_ROWS + jnp.arange(N_TILES, dtype=jnp.int32)[:, None] * step_rows
             + g_row[None, :] % step_rows)
    scatter_runs = (issued, jnp.where(real, dst_g, spill).reshape(-1))
    gather_runs = (issued, jnp.where(real, dst_g, dst_g[:, :1]).reshape(-1))
    srcv = jnp.pad(src.astype(F32), ((0, 0), (ROUTER_LANE0, LANES - ROUTER_LANE0 - N_EXPERTS)))
    return scatter_runs, gather_runs, srcv.reshape(N_TILES, 1, LANES), (seg_start, seg // MOE_TM)


def _moe(x1, u2, meta, cnt, modr, w_gate, w_up, w_down, ln_g, ln_b, li, split, **next_conv):
    tile_counts = cnt[:, 0, ROUTER_LANE0:ROUTER_LANE0 + N_EXPERTS].astype(jnp.int32)
    scatter_runs, gather_runs, srcv, (seg_start, seg_chunks) = _moe_schedule(tile_counts)
    xs = _dispatch(scatter_runs, u2, meta, srcv)
    ys = _experts(seg_start, seg_chunks, xs, w_gate, w_up, w_down, li)
    return _combine(gather_runs, ys, x1, meta, srcv, modr, ln_g, ln_b, li, split, **next_conv)


def _router_slab(wg, bg, we, be):
    w = jnp.concatenate([wg, we.transpose(1, 0, 2).reshape(D, N_EXPERTS)], axis=1)
    b = jnp.concatenate([bg, be.reshape(N_EXPERTS)])
    pad = LANES - w.shape[1]
    return jnp.pad(w, ((0, 0), (0, pad))), jnp.pad(b, (0, pad)).reshape(1, LANES)


def kernel(x_prompt, x_sample, cache_diff_k, cache_diff_v, state_ret_fwd, state_ret_bwd, c, c_ctx, mod_w, mod_b, ln1_g, ln1_b, ln2_g, ln2_b, mix_w_in, mix_w_out, ret_decay_fwd, ret_decay_bwd, diff_lq1, diff_lk1, diff_lq2, diff_lk2, diff_subln_g, conv_w1, conv_b1, conv_dw, conv_dw_b, conv_ln_g, conv_ln_b, conv_w2, conv_b2, router_g_w, router_g_b, router_e_w, router_e_b, moe_w_gate, moe_w_up, moe_w_down):
    xp = x_prompt.reshape(T_PROMPT, D)
    xs = x_sample.reshape(T_SAMPLE, D)
    cond = jnp.concatenate([c_ctx[None, :], c, jnp.zeros((MOD_ROWS - 1 - DEC_BATCH, D), F32)], axis=0)
    modr = _mod_vectors(cond, mod_w, mod_b).reshape(DEPTH * MOD_ROWS * 6, 1, D)
    cos, sin_signed = _rope_tables()

    def row(v):
        return v.reshape(1, -1)

    x = None
    caches = None
    for li in range(DEPTH):
        wr, br = _router_slab(router_g_w[li], router_g_b[li], router_e_w[li], router_e_b[li])
        if li % 2 == 0:
            assert li == 0, "the even mixer reads the kernel inputs directly"
            e = li // 2
            lam_init = 0.8 - 0.6 * math.exp(-0.3 * li)
            proj, ck, cv = _in_proj(xp, xs, modr, mix_w_in[e], li)
            dec = jnp.concatenate([ret_decay_fwd[e], ret_decay_bwd[e]])
            r_p, sf, sb = _retention(proj, dec, BATCH, SEQ, 0, HEADS, emit_state=True)
            (r_s,) = _retention(proj, dec, DEC_BATCH, DEC_SEQ, T_PROMPT // DEC_SEQ, 2,
                                s0f=state_ret_fwd, s0b=state_ret_bwd, e=e)
            lams = (row(diff_lq1[e]), row(diff_lk1[e]), row(diff_lq2[e]), row(diff_lk2[e]),
                    row(diff_subln_g[e]))
            o_p = _attn_prompt(proj, *lams, lam_init)
            o_s = _attn_sample(proj, cache_diff_k, cache_diff_v, cos, sin_signed, *lams, lam_init, e)
            x1, u2, meta, cnt = _out_proj_tail(r_p, r_s, o_p, o_s, mix_w_out[e], xp, xs, modr,
                                               row(ln1_g[li]), row(ln1_b[li]), wr, br, li)
            caches = (ck, cv, sf, sb)
        else:
            o = li // 2
            x1, u2, meta, cnt = _conv_tail(glu, conv_dw[o], row(conv_dw_b[o]), row(conv_ln_g[o]),
                                           row(conv_ln_b[o]), conv_w2[o], row(conv_b2[o]),
                                           x, modr, row(ln1_g[li]), row(ln1_b[li]), wr, br, li)
        next_conv = {}
        if li + 1 < DEPTH and (li + 1) % 2 == 1:
            next_conv = dict(conv_w1=conv_w1[(li + 1) // 2], conv_b1=row(conv_b1[(li + 1) // 2]))
        outs = _moe(x1, u2, meta, cnt, modr, moe_w_gate, moe_w_up, moe_w_down,
                    row(ln2_g[li]), row(ln2_b[li]), li, split=(li == DEPTH - 1), **next_conv)
        x, glu = outs[0], outs[-1]

    y_prompt = outs[0].reshape(BATCH, SEQ, D)
    y_sample = outs[1].reshape(DEC_BATCH, DEC_SEQ, D)
    return (y_prompt, y_sample) + caches
```

```python
import functools
import math

import numpy as np
import jax
import jax.numpy as jnp
from jax import lax
from jax.experimental import pallas as pl
from jax.experimental.pallas import tpu as pltpu

F32 = jnp.float32
BF16 = jnp.bfloat16

D = 1024
BATCH = 16
SEQ = 256
DEPTH = 2
DEC_BATCH = 2
DEC_SEQ = 2048
PAST_LEN = 512
GRID_W = 64
HEADS = 4
HEAD_W = 128
RET_CHUNK = 128
DIFF_DK = 64
ROPE_THETA = 10000.0
IN_W = 7 * HEADS * HEAD_W
CONV_K = 31
CONV_PAD = CONV_K // 2
N_GROUPS = 4
EXPERTS_PER_GROUP = 8
N_EXPERTS = N_GROUPS * EXPERTS_PER_GROUP
D_EXPERT = 512
ALPHA = (2.0 * DEPTH) ** 0.25
LN_EPS = 1e-5
GN_EPS = 1e-6

T_PROMPT = BATCH * SEQ
T_SAMPLE = DEC_BATCH * DEC_SEQ
T = T_PROMPT + T_SAMPLE
TM = 256
N_TILES = T // TM
PROMPT_TILES = T_PROMPT // TM
SAMPLE_TILES_PER_SEQ = DEC_SEQ // TM
MOD_ROWS = 8
MOE_TM = 256
LANES = 128
SUBLANES = 8
RUN_ALIGN = SUBLANES
SORT_ROWS = -(-(2 * TM + N_EXPERTS * (RUN_ALIGN - 1)) // TM) * TM
RUN_BITS = tuple(1 << b for b in range((2 * TM).bit_length() - 1, RUN_ALIGN.bit_length() - 2, -1))
MOE_MAX_TILES = -(-(2 * T + N_TILES * N_EXPERTS * (RUN_ALIGN - 1) + N_EXPERTS * (MOE_TM - RUN_ALIGN)) // MOE_TM)
MOE_ROWS = MOE_MAX_TILES * MOE_TM
SORT_GROUPS = SORT_ROWS // RUN_ALIGN
GROUP_UNROLL = 4
DISPATCH_ROWS = MOE_ROWS + N_TILES * RUN_ALIGN * GROUP_UNROLL
MOE_AHEAD = 8
MOE_IN_SLOTS = MOE_AHEAD + 1
MOE_OUT_SLOTS = 4
ROUTER_LANE0 = N_GROUPS
ROPE_GROUP = DIFF_DK // 2
ROPE_HALF = ROPE_GROUP // 2
V7X_VMEM_BYTES = 64 * 1024 * 1024
VMEM_LIMIT = V7X_VMEM_BYTES - 12 * 1024 * 1024


def _cparams(sem):
    return pltpu.CompilerParams(dimension_semantics=sem, vmem_limit_bytes=VMEM_LIMIT)


def _tile_cond_row(i, tm):
    return jnp.where(i < T_PROMPT // tm, 0, 1 + (i - T_PROMPT // tm) // (DEC_SEQ // tm))


def _mod_spec(li, k, tm=TM):
    return pl.BlockSpec((1, 1, D), lambda i, *_: ((li * MOD_ROWS + _tile_cond_row(i, tm)) * 6 + k, 0, 0))


def _row_spec(shape):
    return pl.BlockSpec(shape, lambda i, *_: (0,) * len(shape))


def _resident_f32_weight(shape):
    return pl.BlockSpec(shape, lambda i, *_: (0,) * len(shape), pipeline_mode=pl.Buffered(1))


def _cast_weight_once(w_ref, w_bf):
    @pl.when(pl.program_id(0) == 0)
    def _():
        w_bf[...] = w_ref[...].astype(BF16)


def _layer_norm(x, g, b):
    mu = jnp.mean(x, axis=-1, keepdims=True)
    xc = x - mu
    var = jnp.mean(xc * xc, axis=-1, keepdims=True)
    return xc * lax.rsqrt(var + LN_EPS) * g + b


def _silu(x):
    return x * jax.nn.sigmoid(x)


def _dot(a, b):
    return jnp.dot(a, b, preferred_element_type=F32)


def _dot_nt(a, b):
    return lax.dot_general(a, b, (((1,), (1,)), ((), ())), preferred_element_type=F32)


def _dot_tn(a, b):
    return lax.dot_general(a, b, (((0,), (0,)), ((), ())), preferred_element_type=F32)


MOD_TN = 1024
MOD_USED_ROWS = 1 + DEC_BATCH


def _mod_kernel(cond_t_ref, w_ref, b_ref, o_ref):
    s = _silu(cond_t_ref[...])
    w = w_ref[0]
    o_ref[0] = jnp.zeros((MOD_ROWS, MOD_TN), F32) + b_ref[0]
    for r in range(MOD_USED_ROWS):
        o_ref[0, r:r + 1, :] = jnp.sum(w * s[:, r:r + 1], axis=0, keepdims=True) + b_ref[0]


def _mod_vectors(cond, mod_w, mod_b):
    return pl.pallas_call(
        _mod_kernel,
        grid=(DEPTH, 6 * D // MOD_TN),
        in_specs=[
            pl.BlockSpec((D, MOD_ROWS), lambda l, j: (0, 0)),
            pl.BlockSpec((1, D, MOD_TN), lambda l, j: (l, 0, j)),
            pl.BlockSpec((1, 1, MOD_TN), lambda l, j: (l, 0, j)),
        ],
        out_specs=pl.BlockSpec((1, MOD_ROWS, MOD_TN), lambda l, j: (l, 0, j)),
        out_shape=jax.ShapeDtypeStruct((DEPTH, MOD_ROWS, 6 * D), F32),
        compiler_params=_cparams(("arbitrary", "arbitrary")),
        name="mod_vectors",
    )(cond.T, mod_w, mod_b.reshape(DEPTH, 1, 6 * D))


def _prompt_tile_spec(width, tm=TM):
    return pl.BlockSpec((tm, width), lambda i, *_: (jnp.minimum(i, T_PROMPT // tm - 1), 0))


def _sample_tile_spec(width, tm=TM):
    return pl.BlockSpec((tm, width), lambda i, *_: (jnp.maximum(i - T_PROMPT // tm, 0), 0))


def _pick_tile(prompt_ref, sample_ref, tm=TM):
    return jnp.where(pl.program_id(0) < T_PROMPT // tm, prompt_ref[...], sample_ref[...])


IN_TM = 512
IN_SEQS = IN_TM // SEQ


def _in_proj_kernel(xp_ref, xs_ref, sh_ref, sc_ref, w_ref, o_ref, ck_ref, cv_ref, w_bf):
    _cast_weight_once(w_ref, w_bf)
    u = _pick_tile(xp_ref, xs_ref, IN_TM) * (1.0 + sc_ref[0]) + sh_ref[0]
    proj = _dot(u.astype(BF16), w_bf[...])
    o_ref[...] = proj.astype(BF16)

    @pl.when(pl.program_id(0) < T_PROMPT // IN_TM)
    def _():
        for s in range(IN_SEQS):
            rows = slice(s * SEQ, (s + 1) * SEQ)
            for h in range(HEADS):
                ck_ref[s, 0, h] = proj[rows, (COL_KD + h) * HEAD_W:(COL_KD + h + 1) * HEAD_W]
                cv_ref[s, 0, h] = proj[rows, (COL_VD + h) * HEAD_W:(COL_VD + h + 1) * HEAD_W]


def _in_proj(x_prompt, x_sample, modr, w_in, li):
    cache_spec = pl.BlockSpec((IN_SEQS, 1, HEADS, SEQ, HEAD_W),
                              lambda i: (jnp.minimum(i, T_PROMPT // IN_TM - 1), 0, 0, 0, 0))
    cache_shape = jax.ShapeDtypeStruct((BATCH, 1, HEADS, SEQ, HEAD_W), F32)
    return pl.pallas_call(
        _in_proj_kernel,
        grid=(T // IN_TM,),
        in_specs=[
            _prompt_tile_spec(D, IN_TM), _sample_tile_spec(D, IN_TM),
            _mod_spec(li, 0, IN_TM),
            _mod_spec(li, 1, IN_TM),
            _resident_f32_weight((D, IN_W)),
        ],
        out_specs=[pl.BlockSpec((IN_TM, IN_W), lambda i: (i, 0)), cache_spec, cache_spec],
        out_shape=[jax.ShapeDtypeStruct((T, IN_W), BF16), cache_shape, cache_shape],
        scratch_shapes=[pltpu.VMEM((D, IN_W), BF16)],
        compiler_params=_cparams(("arbitrary",)),
        name="in_proj",
    )(x_prompt, x_sample, modr, modr, w_in)


COL_QR, COL_KR, COL_VR, COL_GR, COL_QD, COL_KD, COL_VD = (k * HEADS for k in range(7))


def _retention_kernel(dec_ref, q_ref, k_ref, v_ref, g_ref, *rest, n_chunks, n_heads, has_state, emit_state):
    rest = list(rest)
    if has_state:
        s0f_ref, s0b_ref = rest[:2]
        rest = rest[2:]
    r_ref = rest[0]
    rest = rest[1:]
    if emit_state:
        sf_ref, sb_ref = rest[:2]
        rest = rest[2:]
    of_ref = rest[0]

    head0 = pl.program_id(1) * n_heads
    C = RET_CHUNK
    ii = lax.broadcasted_iota(jnp.int32, (C, C), 0)
    jj = lax.broadcasted_iota(jnp.int32, (C, C), 1)
    rel = (ii - jj).astype(F32)
    idx = lax.broadcasted_iota(jnp.int32, (C, 1), 0).astype(F32)
    k_scale = HEAD_W ** -0.5

    def chunk(ref, c, h):
        return ref[c * C:(c + 1) * C, h * HEAD_W:(h + 1) * HEAD_W].astype(F32)

    def decays(direction, h):
        lg = -jnp.exp(jnp.full((1, 1), dec_ref[direction * HEADS + head0 + h], F32))
        if direction == 0:
            inner = jnp.where(rel >= 0, jnp.exp(jnp.maximum(rel, 0.0) * lg), 0.0)
            return inner, jnp.exp((idx + 1.0) * lg), jnp.exp((C - 1.0 - idx) * lg), jnp.exp(C * lg)
        inner = jnp.where(rel <= 0, jnp.exp(jnp.maximum(-rel, 0.0) * lg), 0.0)
        return inner, jnp.exp((C - idx) * lg), jnp.exp(idx * lg), jnp.exp(C * lg)

    def run(direction):
        dec = [decays(direction, h) for h in range(n_heads)]
        if has_state:
            s0_ref = s0f_ref if direction == 0 else s0b_ref
            states = [s0_ref[0, 0, h] for h in range(n_heads)]
        else:
            states = [jnp.zeros((HEAD_W, HEAD_W), F32) for _ in range(n_heads)]
        order = range(n_chunks) if direction == 0 else range(n_chunks - 1, -1, -1)
        for c in order:
            rows = slice(c * C, (c + 1) * C)
            for h in range(n_heads):
                inner, q_decay, k_decay, chunk_decay = dec[h]
                cols = slice(h * HEAD_W, (h + 1) * HEAD_W)
                s = states[h]
                qc = chunk(q_ref, c, h)
                kc = chunk(k_ref, c, h) * k_scale
                vc = chunk(v_ref, c, h).astype(BF16)
                scores = _dot_nt(qc.astype(BF16), kc.astype(BF16)) * inner
                o = _dot(scores.astype(BF16), vc) + _dot((qc * q_decay).astype(BF16), s.astype(BF16))
                states[h] = s * chunk_decay + _dot_tn((kc * k_decay).astype(BF16), vc)
                if direction == 0:
                    of_ref[rows, cols] = o
                else:
                    r = of_ref[rows, cols] + o
                    mu = jnp.mean(r, axis=-1, keepdims=True)
                    rc = r - mu
                    var = jnp.mean(rc * rc, axis=-1, keepdims=True)
                    rn = rc * lax.rsqrt(var + GN_EPS)
                    r_ref[rows, cols] = _silu(chunk(g_ref, c, h)) * rn
        return states

    sf = run(0)
    sb = run(1)
    if emit_state:
        for h in range(n_heads):
            sf_ref[0, 0, h] = sf[h]
            sb_ref[0, 0, h] = sb[h]


def _retention(proj, dec, n_seq, seq_len, row_block0, n_heads, s0f=None, s0b=None, e=0, emit_state=False):
    has_state = s0f is not None
    width = n_heads * HEAD_W

    def col(base):
        return pl.BlockSpec((seq_len, width), lambda b, h, *_: (row_block0 + b, base // n_heads + h))

    state_spec = pl.BlockSpec((1, 1, n_heads, HEAD_W, HEAD_W), lambda b, h, *_: (b, e, h, 0, 0))
    in_specs = [pl.BlockSpec(memory_space=pltpu.SMEM), col(COL_QR), col(COL_KR), col(COL_VR), col(COL_GR)]
    args = [dec, proj, proj, proj, proj]
    if has_state:
        in_specs += [state_spec, state_spec]
        args += [s0f, s0b]
    out_specs = [pl.BlockSpec((seq_len, width), lambda b, h, *_: (b, h))]
    out_shape = [jax.ShapeDtypeStruct((n_seq * seq_len, HEADS * HEAD_W), F32)]
    if emit_state:
        st = pl.BlockSpec((1, 1, n_heads, HEAD_W, HEAD_W), lambda b, h, *_: (b, 0, h, 0, 0))
        out_specs += [st, st]
        out_shape += [jax.ShapeDtypeStruct((n_seq, 1, HEADS, HEAD_W, HEAD_W), F32)] * 2
    return pl.pallas_call(
        functools.partial(_retention_kernel, n_chunks=seq_len // RET_CHUNK, n_heads=n_heads,
                          has_state=has_state, emit_state=emit_state),
        grid=(n_seq, HEADS // n_heads),
        in_specs=in_specs,
        out_specs=out_specs,
        out_shape=out_shape,
        scratch_shapes=[pltpu.VMEM((seq_len, width), F32)],
        compiler_params=_cparams(("arbitrary", "arbitrary")),
        name=f"retention_{seq_len}",
    )(*args)


def _diff_lambda(lq1_ref, lk1_ref, lq2_ref, lk2_ref, lam_init):
    a = jnp.sum(lq1_ref[...] * lk1_ref[...], axis=-1, keepdims=True)
    b = jnp.sum(lq2_ref[...] * lk2_ref[...], axis=-1, keepdims=True)
    return jnp.exp(a) - jnp.exp(b) + lam_init


LOG2E = 1.4426950408889634


def _diff_attend(q, k, v, lam, subln_g, lam_init):
    lane = lax.broadcasted_iota(jnp.int32, q.shape, 1)
    q1 = jnp.where(lane < DIFF_DK, q, 0.0).astype(BF16)
    q2 = jnp.where(lane >= DIFF_DK, q, 0.0).astype(BF16)

    def softmax_times_v(qz):
        s = _dot_nt(qz, k)
        p = jnp.exp2(s - jnp.max(s, axis=-1, keepdims=True))
        return _dot(p.astype(BF16), v) * (1.0 / jnp.sum(p, axis=-1, keepdims=True))

    o = softmax_times_v(q1) - lam * softmax_times_v(q2)
    o = o * lax.rsqrt(jnp.mean(o * o, axis=-1, keepdims=True) + LN_EPS)
    return o * subln_g * (1.0 - lam_init)


def _attn_prompt_kernel(q_ref, k_ref, v_ref, lq1, lk1, lq2, lk2, g_ref, o_ref, *, lam_init):
    lam = _diff_lambda(lq1, lk1, lq2, lk2, lam_init)
    scale = DIFF_DK ** -0.5 * LOG2E
    for h in range(HEADS):
        sl = slice(h * HEAD_W, (h + 1) * HEAD_W)
        o_ref[:, sl] = _diff_attend(q_ref[:, sl].astype(F32) * scale, k_ref[:, sl], v_ref[:, sl],
                                    lam, g_ref[...], lam_init)


def _attn_prompt(proj, lq1, lk1, lq2, lk2, subln_g, lam_init):
    W = HEADS * HEAD_W

    def slab(base):
        return pl.BlockSpec((SEQ, W), lambda b: (b, base // HEADS))

    small = _row_spec((1, DIFF_DK))
    return pl.pallas_call(
        functools.partial(_attn_prompt_kernel, lam_init=lam_init),
        grid=(BATCH,),
        in_specs=[slab(COL_QD), slab(COL_KD), slab(COL_VD), small, small, small, small,
                  _row_spec((1, HEAD_W))],
        out_specs=pl.BlockSpec((SEQ, W), lambda b: (b, 0)),
        out_shape=jax.ShapeDtypeStruct((T_PROMPT, W), F32),
        compiler_params=_cparams(("arbitrary",)),
        name="diff_attn_prompt",
    )(proj, proj, proj, lq1, lk1, lq2, lk2, subln_g)


def _rope(x, cos, sin_signed):
    lane = lax.broadcasted_iota(jnp.int32, x.shape, 1)
    partner = jnp.where((lane % ROPE_GROUP) < ROPE_HALF,
                        pltpu.roll(x, LANES - ROPE_HALF, 1), pltpu.roll(x, ROPE_HALF, 1))
    return x * cos + partner * sin_signed


def _attn_sample_kernel(q_ref, k_ref, v_ref, ck_ref, cv_ref, cosq_ref, sinq_ref, cos_ref, sin_ref,
                        lq1, lk1, lq2, lk2, g_ref, o_ref, kbuf, vbuf, *, lam_init):
    @pl.when(pl.program_id(2) == 0)
    def _():
        kbuf[0:DEC_SEQ, :] = _rope(k_ref[...].astype(F32), cos_ref[...], sin_ref[...]).astype(BF16)
        kbuf[DEC_SEQ:, :] = ck_ref[0, 0, 0].astype(BF16)
        vbuf[0:DEC_SEQ, :] = v_ref[...]
        vbuf[DEC_SEQ:, :] = cv_ref[0, 0, 0].astype(BF16)

    lam = _diff_lambda(lq1, lk1, lq2, lk2, lam_init)
    q = _rope(q_ref[...].astype(F32), cosq_ref[...], sinq_ref[...]) * (DIFF_DK ** -0.5 * LOG2E)
    o_ref[...] = _diff_attend(q, kbuf[...], vbuf[...], lam, g_ref[...], lam_init)


ATTN_TQ = 256


def _attn_sample(proj, cache_k, cache_v, cos, sin_signed, lq1, lk1, lq2, lk2, subln_g, lam_init, e):
    nq = DEC_SEQ // ATTN_TQ
    row0_q = T_PROMPT // ATTN_TQ
    row0_kv = T_PROMPT // DEC_SEQ
    small = pl.BlockSpec((1, DIFF_DK), lambda b, h, t: (0, 0))
    cache = pl.BlockSpec((1, 1, 1, PAST_LEN, HEAD_W), lambda b, h, t: (b, e, h, 0, 0))
    table_q = pl.BlockSpec((ATTN_TQ, HEAD_W), lambda b, h, t: (t, 0))
    table = pl.BlockSpec((DEC_SEQ, HEAD_W), lambda b, h, t: (0, 0))
    return pl.pallas_call(
        functools.partial(_attn_sample_kernel, lam_init=lam_init),
        grid=(DEC_BATCH, HEADS, nq),
        in_specs=[
            pl.BlockSpec((ATTN_TQ, HEAD_W), lambda b, h, t: (row0_q + b * nq + t, COL_QD + h)),
            pl.BlockSpec((DEC_SEQ, HEAD_W), lambda b, h, t: (row0_kv + b, COL_KD + h)),
            pl.BlockSpec((DEC_SEQ, HEAD_W), lambda b, h, t: (row0_kv + b, COL_VD + h)),
            cache, cache, table_q, table_q, table, table,
            small, small, small, small,
            pl.BlockSpec((1, HEAD_W), lambda b, h, t: (0, 0)),
        ],
        out_specs=pl.BlockSpec((ATTN_TQ, HEAD_W), lambda b, h, t: (b * nq + t, h)),
        out_shape=jax.ShapeDtypeStruct((T_SAMPLE, HEADS * HEAD_W), F32),
        scratch_shapes=[pltpu.VMEM((DEC_SEQ + PAST_LEN, HEAD_W), BF16),
                        pltpu.VMEM((DEC_SEQ + PAST_LEN, HEAD_W), BF16)],
        compiler_params=_cparams(("arbitrary", "arbitrary", "arbitrary")),
        name="diff_attn_sample",
    )(proj, proj, proj, cache_k, cache_v, cos, sin_signed, cos, sin_signed,
      lq1, lk1, lq2, lk2, subln_g)


def _rope_tables():
    t = np.arange(DEC_SEQ)
    row, colp = t // GRID_W, t % GRID_W
    lane = np.arange(LANES)
    pos = np.where(((lane // ROPE_GROUP) % 2 == 0)[None, :], row[:, None], colp[:, None]).astype(np.float64)
    half = ROPE_HALF
    inv = (np.float32(ROPE_THETA) ** (-(np.arange(half, dtype=np.float32)) / np.float32(half))).astype(np.float32)
    ang = pos.astype(np.float32) * inv[lane % half][None, :]
    cos = np.cos(ang.astype(np.float64)).astype(np.float32)
    sin = np.sin(ang.astype(np.float64)).astype(np.float32)
    sign = np.where((lane % ROPE_GROUP) < half, -1.0, 1.0).astype(np.float32)[None, :]
    return jnp.asarray(cos), jnp.asarray(sin * sign)


def _split_bf16(a):
    hi = a.astype(BF16)
    return hi, (a - hi.astype(F32)).astype(BF16)


def _mixer_tail(out, x, g1_ref, sc2_ref, sh2_ref, lng_ref, lnb_ref, wr_ref, br_ref,
                x1_ref, u2_ref, meta_ref, cnt_ref):
    x1 = _layer_norm(ALPHA * x + g1_ref[0] * out, lng_ref[...], lnb_ref[...])
    x1_ref[...] = x1
    u2 = x1 * (1.0 + sc2_ref[0]) + sh2_ref[0]
    u2_ref[...] = u2.astype(BF16)

    u_hi, u_lo = _split_bf16(u2)
    w_hi, w_lo = _split_bf16(wr_ref[...])
    logits = _dot(u_hi, w_hi) + (_dot(u_hi, w_lo) + _dot(u_lo, w_hi)) + br_ref[...]
    lane = lax.broadcasted_iota(jnp.int32, logits.shape, 1).astype(F32)
    neg = jnp.float32(-jnp.inf)
    is_g = lane < N_GROUPS
    gl = jnp.where(is_g, logits, neg)
    gmax = jnp.max(gl, axis=-1, keepdims=True)
    gsel = jnp.min(jnp.where(gl == gmax, lane, float(LANES)), axis=-1, keepdims=True)
    p_g = 1.0 / jnp.sum(jnp.where(is_g, jnp.exp(gl - gmax), 0.0), axis=-1, keepdims=True)
    lo = ROUTER_LANE0 + gsel * EXPERTS_PER_GROUP
    el = jnp.where((lane >= lo) & (lane < lo + EXPERTS_PER_GROUP), logits, neg)
    v1 = jnp.max(el, axis=-1, keepdims=True)
    i1 = jnp.min(jnp.where(el == v1, lane, float(LANES)), axis=-1, keepdims=True)
    el2 = jnp.where(lane == i1, neg, el)
    v2 = jnp.max(el2, axis=-1, keepdims=True)
    i2 = jnp.min(jnp.where(el2 == v2, lane, float(LANES)), axis=-1, keepdims=True)
    t = jnp.exp(v2 - v1)
    w1 = p_g / (1.0 + t)
    w2 = w1 * t

    oh1 = (lane == i1).astype(F32)
    oh2 = (lane == i2).astype(F32)
    oh = oh1 + oh2
    r_i = lax.broadcasted_iota(jnp.int32, (TM, TM), 0)
    c_i = lax.broadcasted_iota(jnp.int32, (TM, TM), 1)
    before = (c_i < r_i).astype(BF16)
    earlier = _dot(before, oh.astype(BF16))
    rank1 = jnp.sum(earlier * oh1, axis=-1, keepdims=True)
    rank2 = jnp.sum(earlier * oh2, axis=-1, keepdims=True)
    cnt_ref[0] = jnp.sum(oh, axis=0, keepdims=True)
    cols = (i1, i2, w1, w2, rank1, rank2)
    meta = jnp.zeros_like(logits)
    for k, col in enumerate(cols):
        meta = jnp.where(lane == k, col, meta)
    meta_ref[...] = meta


META_E1, META_E2, META_W1, META_W2, META_RANK1, META_RANK2 = range(6)

_TAIL_OUT_SHAPES = [
    jax.ShapeDtypeStruct((T, D), F32),
    jax.ShapeDtypeStruct((T, D), BF16),
    jax.ShapeDtypeStruct((T, LANES), F32),
    jax.ShapeDtypeStruct((N_TILES, 1, LANES), F32),
]


def _tail_out_specs():
    return [
        pl.BlockSpec((TM, D), lambda i: (i, 0)),
        pl.BlockSpec((TM, D), lambda i: (i, 0)),
        pl.BlockSpec((TM, LANES), lambda i: (i, 0)),
        pl.BlockSpec((1, 1, LANES), lambda i: (i, 0, 0)),
    ]


def _tail_in_specs(li):
    return [
        _mod_spec(li, 2), _mod_spec(li, 4), _mod_spec(li, 3),
        _row_spec((1, D)), _row_spec((1, D)),
        _row_spec((D, LANES)), _row_spec((1, LANES)),
    ]


def _out_proj_kernel(rp_ref, rs_ref, op_ref, os_ref, w_ref, xp_ref, xs_ref, *rest):
    tail_args, w_bf = rest[:-1], rest[-1]
    _cast_weight_once(w_ref, w_bf)
    half = HEADS * HEAD_W
    r = _pick_tile(rp_ref, rs_ref).astype(BF16)
    o = _pick_tile(op_ref, os_ref).astype(BF16)
    out = _dot(r, w_bf[0:half, :]) + _dot(o, w_bf[half:, :])
    _mixer_tail(out, _pick_tile(xp_ref, xs_ref), *tail_args)


def _out_proj_tail(r_p, r_s, o_p, o_s, w_out, x_prompt, x_sample, modr, ln_g, ln_b, wr, br, li):
    half = HEADS * HEAD_W
    return pl.pallas_call(
        _out_proj_kernel,
        grid=(N_TILES,),
        in_specs=[_prompt_tile_spec(half), _sample_tile_spec(half),
                  _prompt_tile_spec(half), _sample_tile_spec(half),
                  _resident_f32_weight((2 * half, D)),
                  _prompt_tile_spec(D), _sample_tile_spec(D)] + _tail_in_specs(li),
        out_specs=_tail_out_specs(),
        out_shape=_TAIL_OUT_SHAPES,
        scratch_shapes=[pltpu.VMEM((2 * half, D), BF16)],
        compiler_params=_cparams(("arbitrary",)),
        name="out_proj_tail",
    )(r_p, r_s, o_p, o_s, w_out, x_prompt, x_sample, modr, modr, modr, ln_g, ln_b, wr, br)


def _conv_glu(y, sh_ref, sc_ref, w_bf, b_ref):
    u = y * (1.0 + sc_ref[0]) + sh_ref[0]
    h = _dot(u.astype(BF16), w_bf[...]) + b_ref[...]
    return h[:, :D] * jax.nn.sigmoid(h[:, D:])


HALO = 16
CONV_ROWS = 64
CONV_COLS = 128


def _depthwise_conv(hp, dw_ref, conv):
    base = HALO - CONV_PAD
    for cb in range(D // CONV_COLS):
        cs = slice(cb * CONV_COLS, (cb + 1) * CONV_COLS)
        for rb in range(TM // CONV_ROWS):
            r0 = rb * CONV_ROWS
            acc = None
            for shift in range(SUBLANES):
                part = None
                for tap in range(CONV_K):
                    off = base + tap
                    if off % SUBLANES != shift:
                        continue
                    a0 = r0 + off - shift
                    term = hp[a0:a0 + CONV_ROWS + SUBLANES, cs] * dw_ref[tap:tap + 1, cs]
                    part = term if part is None else part + term
                part = part[shift:shift + CONV_ROWS, :]
                acc = part if acc is None else acc + part
            conv[r0:r0 + CONV_ROWS, cs] = acc


def _conv_tail_kernel(cur_ref, prev_ref, next_ref, dw_ref, dwb_ref, cg_ref, cb_ref, w2_ref, b2_ref,
                      x_ref, *rest):
    tail_args, (hp, conv, w2_bf) = rest[:-3], rest[-3:]
    _cast_weight_once(w2_ref, w2_bf)
    i = pl.program_id(0)
    k = (i - PROMPT_TILES) % SAMPLE_TILES_PER_SEQ
    in_sample = i >= PROMPT_TILES
    left_ok = jnp.logical_and(in_sample, k != 0)
    right_ok = jnp.logical_and(in_sample, k != SAMPLE_TILES_PER_SEQ - 1)
    hp[0:HALO, :] = jnp.where(left_ok, prev_ref[...], 0.0)
    hp[HALO:HALO + TM, :] = cur_ref[...]
    hp[HALO + TM:HALO + TM + HALO, :] = jnp.where(right_ok, next_ref[...], 0.0)
    _depthwise_conv(hp, dw_ref, conv)
    hc = _silu(_layer_norm(conv[...] + dwb_ref[...], cg_ref[...], cb_ref[...]))
    out = _dot(hc.astype(BF16), w2_bf[...]) + b2_ref[...]
    _mixer_tail(out, x_ref[...], *tail_args)


def _conv_tail(glu, dw, dwb, cg, cb, w2, b2, x, modr, ln_g, ln_b, wr, br, li):
    per = TM // HALO
    last = T // HALO - 1
    return pl.pallas_call(
        _conv_tail_kernel,
        grid=(N_TILES,),
        in_specs=[pl.BlockSpec((TM, D), lambda i: (i, 0)),
                  pl.BlockSpec((HALO, D), lambda i: (jnp.maximum(i * per - 1, 0), 0)),
                  pl.BlockSpec((HALO, D), lambda i: (jnp.minimum((i + 1) * per, last), 0)),
                  _row_spec((CONV_K, D)), _row_spec((1, D)), _row_spec((1, D)), _row_spec((1, D)),
                  _resident_f32_weight((D, D)), _row_spec((1, D)),
                  pl.BlockSpec((TM, D), lambda i: (i, 0))] + _tail_in_specs(li),
        out_specs=_tail_out_specs(),
        out_shape=_TAIL_OUT_SHAPES,
        scratch_shapes=[pltpu.VMEM((TM + 2 * HALO, D), F32), pltpu.VMEM((TM, D), F32),
                        pltpu.VMEM((D, D), BF16)],
        compiler_params=_cparams(("arbitrary",)),
        name="conv_tail",
    )(glu, glu, glu, dw, dwb, cg, cb, w2, b2, x, modr, modr, modr, ln_g, ln_b, wr, br)


def _sorted_positions(meta, srcv):
    lane = lax.broadcasted_iota(jnp.int32, meta.shape, 1).astype(F32)

    def pos(e_col, r_col):
        start = jnp.sum(jnp.where(lane == meta[:, e_col:e_col + 1], srcv, 0.0), axis=-1, keepdims=True)
        return start + meta[:, r_col:r_col + 1]

    return pos(META_E1, META_RANK1), pos(META_E2, META_RANK2)


def _one_hot_rows(pos):
    col = lax.broadcasted_iota(jnp.int32, (TM, SORT_ROWS), 1).astype(F32)
    return col == pos


def _for_each_row_group(tile, tot_ref, dstg_ref, fn):
    def body(k, carry):
        for j in range(GROUP_UNROLL):
            g = k * GROUP_UNROLL + j
            fn(pl.multiple_of(g * RUN_ALIGN, RUN_ALIGN),
               pl.multiple_of(dstg_ref[tile * SORT_GROUPS + g], RUN_ALIGN))
        return carry

    lax.fori_loop(0, tot_ref[tile] // (RUN_ALIGN * GROUP_UNROLL), body, 0)


def _wait_rows(total, make_copy):
    for bit in RUN_BITS:
        @pl.when((total & bit) != 0)
        def _(bit=bit):
            make_copy(bit).wait()


def _dispatch_kernel(tot_ref, dstg_ref, u_ref, meta_ref, srcv_ref, xs_ref, sorted_ref, sems):
    i = pl.program_id(0)
    slot = i % 2

    def wait_tile(tile, slot):
        buf = sorted_ref.at[slot]
        _wait_rows(tot_ref[tile], lambda rows: pltpu.make_async_copy(
            buf.at[pl.ds(0, rows)], xs_ref.at[pl.ds(0, rows)], sems.at[slot]))

    @pl.when(i >= 2)
    def _():
        wait_tile(i - 2, slot)

    pos1, pos2 = _sorted_positions(meta_ref[...], srcv_ref[0])
    select = jnp.logical_or(_one_hot_rows(pos1), _one_hot_rows(pos2)).astype(BF16)
    sorted_ref[slot] = _dot_tn(select, u_ref[...])
    buf = sorted_ref.at[slot]

    def start(src, dst):
        pltpu.make_async_copy(buf.at[pl.ds(src, RUN_ALIGN)], xs_ref.at[pl.ds(dst, RUN_ALIGN)],
                              sems.at[slot]).start()

    _for_each_row_group(i, tot_ref, dstg_ref, start)

    @pl.when(i == N_TILES - 1)
    def _():
        wait_tile(i - 1, 1 - slot)
        wait_tile(i, slot)


def _dispatch(sched, u2, meta, srcv):
    return pl.pallas_call(
        _dispatch_kernel,
        grid_spec=pltpu.PrefetchScalarGridSpec(
            num_scalar_prefetch=2,
            grid=(N_TILES,),
            in_specs=[pl.BlockSpec((TM, D), lambda i, *_: (i, 0)),
                      pl.BlockSpec((TM, LANES), lambda i, *_: (i, 0)),
                      pl.BlockSpec((1, 1, LANES), lambda i, *_: (i, 0, 0))],
            out_specs=pl.BlockSpec(memory_space=pl.ANY),
            scratch_shapes=[pltpu.VMEM((2, SORT_ROWS, D), F32), pltpu.SemaphoreType.DMA((2,))],
        ),
        out_shape=jax.ShapeDtypeStruct((DISPATCH_ROWS, D), F32),
        compiler_params=_cparams(("arbitrary",)),
        name="moe_dispatch",
    )(*sched, u2, meta, srcv)


def _experts_kernel(start_ref, chunks_ref, xs_ref, wg_ref, wu_ref, wd_ref, ys_ref,
                    wg_bf, wu_bf, wd_bf, xbuf, ybuf, in_sems, out_sems):
    e = pl.program_id(0)
    n = chunks_ref[e]
    first = start_ref[e] // MOE_TM
    total = start_ref[N_EXPERTS - 1] // MOE_TM + chunks_ref[N_EXPERTS - 1]

    def rows(g):
        return pl.ds(pl.multiple_of(g * MOE_TM, MOE_TM), MOE_TM)

    def load(g):
        slot = g % MOE_IN_SLOTS
        return pltpu.make_async_copy(xs_ref.at[rows(g)], xbuf.at[slot], in_sems.at[slot])

    def store(g):
        slot = g % MOE_OUT_SLOTS
        return pltpu.make_async_copy(ybuf.at[slot], ys_ref.at[rows(g)], out_sems.at[slot])

    @pl.when(e == 0)
    def _():
        for g in range(MOE_AHEAD):
            @pl.when(g < total)
            def _(g=g):
                load(g).start()

    @pl.when(n > 0)
    def _():
        wg_bf[...] = wg_ref[0, 0].astype(BF16)
        wu_bf[...] = wu_ref[0, 0].astype(BF16)
        wd_bf[...] = wd_ref[0, 0].astype(BF16)

        def tile(g, carry):
            load(g).wait()

            @pl.when(g + MOE_AHEAD < total)
            def _():
                load(g + MOE_AHEAD).start()

            @pl.when(g >= MOE_OUT_SLOTS)
            def _():
                store(g - MOE_OUT_SLOTS).wait()

            x = xbuf[g % MOE_IN_SLOTS].astype(BF16)
            h = (_silu(_dot(x, wg_bf[...])) * _dot(x, wu_bf[...])).astype(BF16)
            ybuf[g % MOE_OUT_SLOTS] = _dot(h, wd_bf[...])
            store(g).start()
            return carry

        lax.fori_loop(first, first + n, tile, 0)

    @pl.when(e == N_EXPERTS - 1)
    def _():
        for back in range(MOE_OUT_SLOTS, 0, -1):
            @pl.when(total >= back)
            def _(back=back):
                store(total - back).wait()


def _experts(seg_start, seg_chunks, xs, w_gate, w_up, w_down, li):
    def weight(shape):
        return pl.BlockSpec((1, 1) + shape, lambda e, *_: (li, e, 0, 0))

    return pl.pallas_call(
        _experts_kernel,
        grid_spec=pltpu.PrefetchScalarGridSpec(
            num_scalar_prefetch=2,
            grid=(N_EXPERTS,),
            in_specs=[pl.BlockSpec(memory_space=pl.ANY),
                      weight((D, D_EXPERT)), weight((D, D_EXPERT)), weight((D_EXPERT, D))],
            out_specs=pl.BlockSpec(memory_space=pl.ANY),
            scratch_shapes=[pltpu.VMEM((D, D_EXPERT), BF16), pltpu.VMEM((D, D_EXPERT), BF16),
                            pltpu.VMEM((D_EXPERT, D), BF16),
                            pltpu.VMEM((MOE_IN_SLOTS, MOE_TM, D), F32),
                            pltpu.VMEM((MOE_OUT_SLOTS, MOE_TM, D), F32),
                            pltpu.SemaphoreType.DMA((MOE_IN_SLOTS,)),
                            pltpu.SemaphoreType.DMA((MOE_OUT_SLOTS,))],
        ),
        out_shape=jax.ShapeDtypeStruct((MOE_ROWS, D), F32),
        compiler_params=_cparams(("arbitrary",)),
        name="moe_experts",
    )(seg_start, seg_chunks, xs, w_gate, w_up, w_down)


def _combine_kernel(tot_ref, dstg_ref, ys_ref, x1_ref, meta_ref, srcv_ref, g2_ref,
                    lng_ref, lnb_ref, *rest, split, feeds_conv):
    if feeds_conv:
        (sh_ref, sc_ref, w1_ref, b1_ref), rest, w1_bf = rest[:4], rest[4:-1], rest[-1]
        _cast_weight_once(w1_ref, w1_bf)
    outs, (sorted_ref, sems) = rest[:-2], rest[-2:]
    i = pl.program_id(0)
    slot = i % 2

    def fetch(tile, slot):
        buf = sorted_ref.at[slot]

        def start(src, dst):
            pltpu.make_async_copy(ys_ref.at[pl.ds(dst, RUN_ALIGN)], buf.at[pl.ds(src, RUN_ALIGN)],
                                  sems.at[slot]).start()

        _for_each_row_group(tile, tot_ref, dstg_ref, start)

    @pl.when(i == 0)
    def _():
        sorted_ref[...] = jnp.zeros_like(sorted_ref)
        fetch(0, 0)

    @pl.when(i + 1 < N_TILES)
    def _():
        fetch(i + 1, 1 - slot)

    meta = meta_ref[...]
    pos1, pos2 = _sorted_positions(meta, srcv_ref[0])
    sel1 = _one_hot_rows(pos1).astype(BF16)
    sel2 = _one_hot_rows(pos2).astype(BF16)
    buf = sorted_ref.at[slot]
    _wait_rows(tot_ref[i], lambda rows: pltpu.make_async_copy(
        ys_ref.at[pl.ds(0, rows)], buf.at[pl.ds(0, rows)], sems.at[slot]))
    ysort = sorted_ref[slot].astype(BF16)
    f = (meta[:, META_W1:META_W1 + 1] * _dot(sel1, ysort)
         + meta[:, META_W2:META_W2 + 1] * _dot(sel2, ysort))
    y = _layer_norm(ALPHA * x1_ref[...] + g2_ref[0] * f, lng_ref[...], lnb_ref[...])
    if split:
        @pl.when(i < PROMPT_TILES)
        def _():
            outs[0][...] = y

        @pl.when(i >= PROMPT_TILES)
        def _():
            outs[1][...] = y
    else:
        outs[0][...] = y
    if feeds_conv:
        outs[-1][...] = _conv_glu(y, sh_ref, sc_ref, w1_bf, b1_ref)


def _combine(sched, ys, x1, meta, srcv, modr, ln_g, ln_b, li, split, conv_w1=None, conv_b1=None):
    feeds_conv = conv_w1 is not None
    tile = pl.BlockSpec((TM, D), lambda i, *_: (i, 0))
    if split:
        out_specs = [_prompt_tile_spec(D), _sample_tile_spec(D)]
        out_shape = [jax.ShapeDtypeStruct((T_PROMPT, D), F32), jax.ShapeDtypeStruct((T_SAMPLE, D), F32)]
    else:
        out_specs = [tile]
        out_shape = [jax.ShapeDtypeStruct((T, D), F32)]
    extra_specs, extra_args, extra_scratch = [], [], []
    if feeds_conv:
        extra_specs = [_mod_spec(li + 1, 0), _mod_spec(li + 1, 1),
                       _resident_f32_weight((D, 2 * D)), _row_spec((1, 2 * D))]
        extra_args = [modr, modr, conv_w1, conv_b1]
        extra_scratch = [pltpu.VMEM((D, 2 * D), BF16)]
        out_specs = out_specs + [tile]
        out_shape = out_shape + [jax.ShapeDtypeStruct((T, D), F32)]
    return pl.pallas_call(
        functools.partial(_combine_kernel, split=split, feeds_conv=feeds_conv),
        grid_spec=pltpu.PrefetchScalarGridSpec(
            num_scalar_prefetch=2,
            grid=(N_TILES,),
            in_specs=[pl.BlockSpec(memory_space=pl.ANY),
                      pl.BlockSpec((TM, D), lambda i, *_: (i, 0)),
                      pl.BlockSpec((TM, LANES), lambda i, *_: (i, 0)),
                      pl.BlockSpec((1, 1, LANES), lambda i, *_: (i, 0, 0)),
                      _mod_spec(li, 5), _row_spec((1, D)), _row_spec((1, D))] + extra_specs,
            out_specs=out_specs,
            scratch_shapes=[pltpu.VMEM((2, SORT_ROWS, D), F32), pltpu.SemaphoreType.DMA((2,))] + extra_scratch,
        ),
        out_shape=out_shape,
        compiler_params=_cparams(("arbitrary",)),
        name="moe_combine",
    )(*sched, ys, x1, meta, srcv, modr, ln_g, ln_b, *extra_args)


def _moe_schedule(tile_counts):
    n = (tile_counts + RUN_ALIGN - 1) // RUN_ALIGN * RUN_ALIGN
    src = jnp.cumsum(n, axis=1) - n
    per_expert = jnp.sum(n, axis=0)
    seg = (per_expert + MOE_TM - 1) // MOE_TM * MOE_TM
    seg_start = jnp.cumsum(seg) - seg
    dst = seg_start[None, :] + jnp.cumsum(n, axis=0) - n
    g_row = jnp.arange(SORT_GROUPS, dtype=jnp.int32) * RUN_ALIGN
    in_run = jnp.logical_and(src[:, None, :] <= g_row[None, :, None],
                             g_row[None, :, None] < (src + n)[:, None, :])
    dst_g = g_row[None, :] + jnp.sum(jnp.where(in_run, (dst - src)[:, None, :], 0), axis=2)
    step_rows = RUN_ALIGN * GROUP_UNROLL
    rows = jnp.sum(n, axis=1)
    issued = (rows + step_rows - 1) // step_rows * step_rows
    real = g_row[None, :] < rows[:, None]
    spill = (MOE_ROWS + jnp.arange(N_TILES, dtype=jnp.int32)[:, None] * step_rows
             + g_row[None, :] % step_rows)
    scatter_runs = (issued, jnp.where(real, dst_g, spill).reshape(-1))
    gather_runs = (issued, jnp.where(real, dst_g, dst_g[:, :1]).reshape(-1))
    srcv = jnp.pad(src.astype(F32), ((0, 0), (ROUTER_LANE0, LANES - ROUTER_LANE0 - N_EXPERTS)))
    return scatter_runs, gather_runs, srcv.reshape(N_TILES, 1, LANES), (seg_start, seg // MOE_TM)


def _moe(x1, u2, meta, cnt, modr, w_gate, w_up, w_down, ln_g, ln_b, li, split, **next_conv):
    tile_counts = cnt[:, 0, ROUTER_LANE0:ROUTER_LANE0 + N_EXPERTS].astype(jnp.int32)
    scatter_runs, gather_runs, srcv, (seg_start, seg_chunks) = _moe_schedule(tile_counts)
    xs = _dispatch(scatter_runs, u2, meta, srcv)
    ys = _experts(seg_start, seg_chunks, xs, w_gate, w_up, w_down, li)
    return _combine(gather_runs, ys, x1, meta, srcv, modr, ln_g, ln_b, li, split, **next_conv)


def _router_slab(wg, bg, we, be):
    w = jnp.concatenate([wg, we.transpose(1, 0, 2).reshape(D, N_EXPERTS)], axis=1)
    b = jnp.concatenate([bg, be.reshape(N_EXPERTS)])
    pad = LANES - w.shape[1]
    return jnp.pad(w, ((0, 0), (0, pad))), jnp.pad(b, (0, pad)).reshape(1, LANES)


def kernel(x_prompt, x_sample, cache_diff_k, cache_diff_v, state_ret_fwd, state_ret_bwd, c, c_ctx, mod_w, mod_b, ln1_g, ln1_b, ln2_g, ln2_b, mix_w_in, mix_w_out, ret_decay_fwd, ret_decay_bwd, diff_lq1, diff_lk1, diff_lq2, diff_lk2, diff_subln_g, conv_w1, conv_b1, conv_dw, conv_dw_b, conv_ln_g, conv_ln_b, conv_w2, conv_b2, router_g_w, router_g_b, router_e_w, router_e_b, moe_w_gate, moe_w_up, moe_w_down):
    xp = x_prompt.reshape(T_PROMPT, D)
    xs = x_sample.reshape(T_SAMPLE, D)
    cond = jnp.concatenate([c_ctx[None, :], c, jnp.zeros((MOD_ROWS - 1 - DEC_BATCH, D), F32)], axis=0)
    modr = _mod_vectors(cond, mod_w, mod_b).reshape(DEPTH * MOD_ROWS * 6, 1, D)
    cos, sin_signed = _rope_tables()

    def row(v):
        return v.reshape(1, -1)

    x = None
    caches = None
    for li in range(DEPTH):
        wr, br = _router_slab(router_g_w[li], router_g_b[li], router_e_w[li], router_e_b[li])
        if li % 2 == 0:
            assert li == 0, "the even mixer reads the kernel inputs directly"
            e = li // 2
            lam_init = 0.8 - 0.6 * math.exp(-0.3 * li)
            proj, ck, cv = _in_proj(xp, xs, modr, mix_w_in[e], li)
            dec = jnp.concatenate([ret_decay_fwd[e], ret_decay_bwd[e]])
            r_p, sf, sb = _retention(proj, dec, BATCH, SEQ, 0, HEADS, emit_state=True)
            (r_s,) = _retention(proj, dec, DEC_BATCH, DEC_SEQ, T_PROMPT // DEC_SEQ, 2,
                                s0f=state_ret_fwd, s0b=state_ret_bwd, e=e)
            lams = (row(diff_lq1[e]), row(diff_lk1[e]), row(diff_lq2[e]), row(diff_lk2[e]),
                    row(diff_subln_g[e]))
            o_p = _attn_prompt(proj, *lams, lam_init)
            o_s = _attn_sample(proj, cache_diff_k, cache_diff_v, cos, sin_signed, *lams, lam_init, e)
            x1, u2, meta, cnt = _out_proj_tail(r_p, r_s, o_p, o_s, mix_w_out[e], xp, xs, modr,
                                               row(ln1_g[li]), row(ln1_b[li]), wr, br, li)
            caches = (ck, cv, sf, sb)
        else:
            o = li // 2
            x1, u2, meta, cnt = _conv_tail(glu, conv_dw[o], row(conv_dw_b[o]), row(conv_ln_g[o]),
                                           row(conv_ln_b[o]), conv_w2[o], row(conv_b2[o]),
                                           x, modr, row(ln1_g[li]), row(ln1_b[li]), wr, br, li)
        next_conv = {}
        if li + 1 < DEPTH and (li + 1) % 2 == 1:
            next_conv = dict(conv_w1=conv_w1[(li + 1) // 2], conv_b1=row(conv_b1[(li + 1) // 2]))
        outs = _moe(x1, u2, meta, cnt, modr, moe_w_gate, moe_w_up, moe_w_down,
                    row(ln2_g[li]), row(ln2_b[li]), li, split=(li == DEPTH - 1), **next_conv)
        x, glu = outs[0], outs[-1]

    y_prompt = outs[0].reshape(BATCH, SEQ, D)
    y_sample = outs[1].reshape(DEC_BATCH, DEC_SEQ, D)
    return (y_prompt, y_sample) + caches
```

```python
import functools
import math

import numpy as np
import jax
import jax.numpy as jnp
from jax import lax
from jax.experimental import pallas as pl
from jax.experimental.pallas import tpu as pltpu

F32 = jnp.float32
BF16 = jnp.bfloat16

D = 1024
BATCH = 16
SEQ = 256
DEPTH = 2
DEC_BATCH = 2
DEC_SEQ = 2048
PAST_LEN = 512
GRID_W = 64
HEADS = 4
HEAD_W = 128
RET_CHUNK = 128
DIFF_DK = 64
ROPE_THETA = 10000.0
IN_W = 7 * HEADS * HEAD_W
CONV_K = 31
CONV_PAD = CONV_K // 2
N_GROUPS = 4
EXPERTS_PER_GROUP = 8
N_EXPERTS = N_GROUPS * EXPERTS_PER_GROUP
D_EXPERT = 512
ALPHA = (2.0 * DEPTH) ** 0.25
LN_EPS = 1e-5
GN_EPS = 1e-6

T_PROMPT = BATCH * SEQ
T_SAMPLE = DEC_BATCH * DEC_SEQ
T = T_PROMPT + T_SAMPLE
TM = 256
N_TILES = T // TM
PROMPT_TILES = T_PROMPT // TM
SAMPLE_TILES_PER_SEQ = DEC_SEQ // TM
MOD_ROWS = 8
MOE_TM = 256
LANES = 128
SUBLANES = 8
RUN_ALIGN = SUBLANES
SORT_ROWS = -(-(2 * TM + N_EXPERTS * (RUN_ALIGN - 1)) // TM) * TM
RUN_BITS = tuple(1 << b for b in range((2 * TM).bit_length() - 1, RUN_ALIGN.bit_length() - 2, -1))
MOE_MAX_TILES = -(-(2 * T + N_TILES * N_EXPERTS * (RUN_ALIGN - 1) + N_EXPERTS * (MOE_TM - RUN_ALIGN)) // MOE_TM)
MOE_ROWS = MOE_MAX_TILES * MOE_TM
SORT_GROUPS = SORT_ROWS // RUN_ALIGN
GROUP_UNROLL = 8
DISPATCH_ROWS = MOE_ROWS + N_TILES * RUN_ALIGN * GROUP_UNROLL
MOE_AHEAD = 8
MOE_IN_SLOTS = MOE_AHEAD + 1
MOE_OUT_SLOTS = 4
ROUTER_LANE0 = N_GROUPS
ROPE_GROUP = DIFF_DK // 2
ROPE_HALF = ROPE_GROUP // 2
V7X_VMEM_BYTES = 64 * 1024 * 1024
VMEM_LIMIT = V7X_VMEM_BYTES - 12 * 1024 * 1024


def _cparams(sem):
    return pltpu.CompilerParams(dimension_semantics=sem, vmem_limit_bytes=VMEM_LIMIT)


def _tile_cond_row(i, tm):
    return jnp.where(i < T_PROMPT // tm, 0, 1 + (i - T_PROMPT // tm) // (DEC_SEQ // tm))


def _mod_spec(li, k, tm=TM):
    return pl.BlockSpec((1, 1, D), lambda i, *_: ((li * MOD_ROWS + _tile_cond_row(i, tm)) * 6 + k, 0, 0))


def _row_spec(shape):
    return pl.BlockSpec(shape, lambda i, *_: (0,) * len(shape))


def _resident_f32_weight(shape):
    return pl.BlockSpec(shape, lambda i, *_: (0,) * len(shape), pipeline_mode=pl.Buffered(1))


def _cast_weight_once(w_ref, w_bf):
    @pl.when(pl.program_id(0) == 0)
    def _():
        w_bf[...] = w_ref[...].astype(BF16)


def _layer_norm(x, g, b):
    mu = jnp.mean(x, axis=-1, keepdims=True)
    xc = x - mu
    var = jnp.mean(xc * xc, axis=-1, keepdims=True)
    return xc * lax.rsqrt(var + LN_EPS) * g + b


def _silu(x):
    return x * jax.nn.sigmoid(x)


def _dot(a, b):
    return jnp.dot(a, b, preferred_element_type=F32)


def _dot_nt(a, b):
    return lax.dot_general(a, b, (((1,), (1,)), ((), ())), preferred_element_type=F32)


def _dot_tn(a, b):
    return lax.dot_general(a, b, (((0,), (0,)), ((), ())), preferred_element_type=F32)


MOD_TN = 2048
MOD_USED_ROWS = 1 + DEC_BATCH


def _mod_kernel(cond_t_ref, w_ref, b_ref, o_ref):
    s = _silu(cond_t_ref[...])
    w = w_ref[0]
    o_ref[0] = jnp.zeros((MOD_ROWS, MOD_TN), F32) + b_ref[0]
    for r in range(MOD_USED_ROWS):
        o_ref[0, r:r + 1, :] = jnp.sum(w * s[:, r:r + 1], axis=0, keepdims=True) + b_ref[0]


def _mod_vectors(cond, mod_w, mod_b):
    return pl.pallas_call(
        _mod_kernel,
        grid=(DEPTH, 6 * D // MOD_TN),
        in_specs=[
            pl.BlockSpec((D, MOD_ROWS), lambda l, j: (0, 0)),
            pl.BlockSpec((1, D, MOD_TN), lambda l, j: (l, 0, j)),
            pl.BlockSpec((1, 1, MOD_TN), lambda l, j: (l, 0, j)),
        ],
        out_specs=pl.BlockSpec((1, MOD_ROWS, MOD_TN), lambda l, j: (l, 0, j)),
        out_shape=jax.ShapeDtypeStruct((DEPTH, MOD_ROWS, 6 * D), F32),
        compiler_params=_cparams(("arbitrary", "arbitrary")),
        name="mod_vectors",
    )(cond.T, mod_w, mod_b.reshape(DEPTH, 1, 6 * D))


def _prompt_tile_spec(width, tm=TM):
    return pl.BlockSpec((tm, width), lambda i, *_: (jnp.minimum(i, T_PROMPT // tm - 1), 0))


def _sample_tile_spec(width, tm=TM):
    return pl.BlockSpec((tm, width), lambda i, *_: (jnp.maximum(i - T_PROMPT // tm, 0), 0))


def _pick_tile(prompt_ref, sample_ref, tm=TM):
    return jnp.where(pl.program_id(0) < T_PROMPT // tm, prompt_ref[...], sample_ref[...])


IN_TM = 512
IN_SEQS = IN_TM // SEQ


def _in_proj_kernel(xp_ref, xs_ref, sh_ref, sc_ref, w_ref, o_ref, ck_ref, cv_ref, w_bf):
    _cast_weight_once(w_ref, w_bf)
    u = _pick_tile(xp_ref, xs_ref, IN_TM) * (1.0 + sc_ref[0]) + sh_ref[0]
    proj = _dot(u.astype(BF16), w_bf[...])
    o_ref[...] = proj.astype(BF16)

    @pl.when(pl.program_id(0) < T_PROMPT // IN_TM)
    def _():
        for s in range(IN_SEQS):
            rows = slice(s * SEQ, (s + 1) * SEQ)
            for h in range(HEADS):
                ck_ref[s, 0, h] = proj[rows, (COL_KD + h) * HEAD_W:(COL_KD + h + 1) * HEAD_W]
                cv_ref[s, 0, h] = proj[rows, (COL_VD + h) * HEAD_W:(COL_VD + h + 1) * HEAD_W]


def _in_proj(x_prompt, x_sample, modr, w_in, li):
    cache_spec = pl.BlockSpec((IN_SEQS, 1, HEADS, SEQ, HEAD_W),
                              lambda i: (jnp.minimum(i, T_PROMPT // IN_TM - 1), 0, 0, 0, 0))
    cache_shape = jax.ShapeDtypeStruct((BATCH, 1, HEADS, SEQ, HEAD_W), F32)
    return pl.pallas_call(
        _in_proj_kernel,
        grid=(T // IN_TM,),
        in_specs=[
            _prompt_tile_spec(D, IN_TM), _sample_tile_spec(D, IN_TM),
            _mod_spec(li, 0, IN_TM),
            _mod_spec(li, 1, IN_TM),
            _resident_f32_weight((D, IN_W)),
        ],
        out_specs=[pl.BlockSpec((IN_TM, IN_W), lambda i: (i, 0)), cache_spec, cache_spec],
        out_shape=[jax.ShapeDtypeStruct((T, IN_W), BF16), cache_shape, cache_shape],
        scratch_shapes=[pltpu.VMEM((D, IN_W), BF16)],
        compiler_params=_cparams(("arbitrary",)),
        name="in_proj",
    )(x_prompt, x_sample, modr, modr, w_in)


COL_QR, COL_KR, COL_VR, COL_GR, COL_QD, COL_KD, COL_VD = (k * HEADS for k in range(7))


def _retention_kernel(dec_ref, q_ref, k_ref, v_ref, g_ref, *rest, n_chunks, n_heads, has_state, emit_state):
    rest = list(rest)
    if has_state:
        s0f_ref, s0b_ref = rest[:2]
        rest = rest[2:]
    r_ref = rest[0]
    rest = rest[1:]
    if emit_state:
        sf_ref, sb_ref = rest[:2]
        rest = rest[2:]
    of_ref = rest[0]

    head0 = pl.program_id(1) * n_heads
    C = RET_CHUNK
    ii = lax.broadcasted_iota(jnp.int32, (C, C), 0)
    jj = lax.broadcasted_iota(jnp.int32, (C, C), 1)
    rel = (ii - jj).astype(F32)
    idx = lax.broadcasted_iota(jnp.int32, (C, 1), 0).astype(F32)
    k_scale = HEAD_W ** -0.5

    def chunk(ref, c, h):
        return ref[c * C:(c + 1) * C, h * HEAD_W:(h + 1) * HEAD_W].astype(F32)

    def decays(direction, h):
        lg = -jnp.exp(jnp.full((1, 1), dec_ref[direction * HEADS + head0 + h], F32))
        if direction == 0:
            inner = jnp.where(rel >= 0, jnp.exp(jnp.maximum(rel, 0.0) * lg), 0.0)
            return inner, jnp.exp((idx + 1.0) * lg), jnp.exp((C - 1.0 - idx) * lg), jnp.exp(C * lg)
        inner = jnp.where(rel <= 0, jnp.exp(jnp.maximum(-rel, 0.0) * lg), 0.0)
        return inner, jnp.exp((C - idx) * lg), jnp.exp(idx * lg), jnp.exp(C * lg)

    def run(direction):
        dec = [decays(direction, h) for h in range(n_heads)]
        if has_state:
            s0_ref = s0f_ref if direction == 0 else s0b_ref
            states = [s0_ref[0, 0, h] for h in range(n_heads)]
        else:
            states = [jnp.zeros((HEAD_W, HEAD_W), F32) for _ in range(n_heads)]
        order = range(n_chunks) if direction == 0 else range(n_chunks - 1, -1, -1)
        for c in order:
            rows = slice(c * C, (c + 1) * C)
            for h in range(n_heads):
                inner, q_decay, k_decay, chunk_decay = dec[h]
                cols = slice(h * HEAD_W, (h + 1) * HEAD_W)
                s = states[h]
                qc = chunk(q_ref, c, h)
                kc = chunk(k_ref, c, h) * k_scale
                vc = chunk(v_ref, c, h).astype(BF16)
                scores = _dot_nt(qc.astype(BF16), kc.astype(BF16)) * inner
                o = _dot(scores.astype(BF16), vc) + _dot((qc * q_decay).astype(BF16), s.astype(BF16))
                states[h] = s * chunk_decay + _dot_tn((kc * k_decay).astype(BF16), vc)
                if direction == 0:
                    of_ref[rows, cols] = o
                else:
                    r = of_ref[rows, cols] + o
                    mu = jnp.mean(r, axis=-1, keepdims=True)
                    rc = r - mu
                    var = jnp.mean(rc * rc, axis=-1, keepdims=True)
                    rn = rc * lax.rsqrt(var + GN_EPS)
                    r_ref[rows, cols] = _silu(chunk(g_ref, c, h)) * rn
        return states

    sf = run(0)
    sb = run(1)
    if emit_state:
        for h in range(n_heads):
            sf_ref[0, 0, h] = sf[h]
            sb_ref[0, 0, h] = sb[h]


def _retention(proj, dec, n_seq, seq_len, row_block0, n_heads, s0f=None, s0b=None, e=0, emit_state=False):
    has_state = s0f is not None
    width = n_heads * HEAD_W

    def col(base):
        return pl.BlockSpec((seq_len, width), lambda b, h, *_: (row_block0 + b, base // n_heads + h))

    state_spec = pl.BlockSpec((1, 1, n_heads, HEAD_W, HEAD_W), lambda b, h, *_: (b, e, h, 0, 0))
    in_specs = [pl.BlockSpec(memory_space=pltpu.SMEM), col(COL_QR), col(COL_KR), col(COL_VR), col(COL_GR)]
    args = [dec, proj, proj, proj, proj]
    if has_state:
        in_specs += [state_spec, state_spec]
        args += [s0f, s0b]
    out_specs = [pl.BlockSpec((seq_len, width), lambda b, h, *_: (b, h))]
    out_shape = [jax.ShapeDtypeStruct((n_seq * seq_len, HEADS * HEAD_W), F32)]
    if emit_state:
        st = pl.BlockSpec((1, 1, n_heads, HEAD_W, HEAD_W), lambda b, h, *_: (b, 0, h, 0, 0))
        out_specs += [st, st]
        out_shape += [jax.ShapeDtypeStruct((n_seq, 1, HEADS, HEAD_W, HEAD_W), F32)] * 2
    return pl.pallas_call(
        functools.partial(_retention_kernel, n_chunks=seq_len // RET_CHUNK, n_heads=n_heads,
                          has_state=has_state, emit_state=emit_state),
        grid=(n_seq, HEADS // n_heads),
        in_specs=in_specs,
        out_specs=out_specs,
        out_shape=out_shape,
        scratch_shapes=[pltpu.VMEM((seq_len, width), F32)],
        compiler_params=_cparams(("arbitrary", "arbitrary")),
        name=f"retention_{seq_len}",
    )(*args)


def _diff_lambda(lq1_ref, lk1_ref, lq2_ref, lk2_ref, lam_init):
    a = jnp.sum(lq1_ref[...] * lk1_ref[...], axis=-1, keepdims=True)
    b = jnp.sum(lq2_ref[...] * lk2_ref[...], axis=-1, keepdims=True)
    return jnp.exp(a) - jnp.exp(b) + lam_init


LOG2E = 1.4426950408889634


def _diff_attend(q, k, v, lam, subln_g, lam_init):
    lane = lax.broadcasted_iota(jnp.int32, q.shape, 1)
    q1 = jnp.where(lane < DIFF_DK, q, 0.0).astype(BF16)
    q2 = jnp.where(lane >= DIFF_DK, q, 0.0).astype(BF16)

    def softmax_times_v(qz):
        s = _dot_nt(qz, k)
        p = jnp.exp2(s - jnp.max(s, axis=-1, keepdims=True))
        return _dot(p.astype(BF16), v) * (1.0 / jnp.sum(p, axis=-1, keepdims=True))

    o = softmax_times_v(q1) - lam * softmax_times_v(q2)
    o = o * lax.rsqrt(jnp.mean(o * o, axis=-1, keepdims=True) + LN_EPS)
    return o * subln_g * (1.0 - lam_init)


def _attn_prompt_kernel(q_ref, k_ref, v_ref, lq1, lk1, lq2, lk2, g_ref, o_ref, *, lam_init):
    lam = _diff_lambda(lq1, lk1, lq2, lk2, lam_init)
    scale = DIFF_DK ** -0.5 * LOG2E
    for h in range(HEADS):
        sl = slice(h * HEAD_W, (h + 1) * HEAD_W)
        o_ref[:, sl] = _diff_attend(q_ref[:, sl].astype(F32) * scale, k_ref[:, sl], v_ref[:, sl],
                                    lam, g_ref[...], lam_init)


def _attn_prompt(proj, lq1, lk1, lq2, lk2, subln_g, lam_init):
    W = HEADS * HEAD_W

    def slab(base):
        return pl.BlockSpec((SEQ, W), lambda b: (b, base // HEADS))

    small = _row_spec((1, DIFF_DK))
    return pl.pallas_call(
        functools.partial(_attn_prompt_kernel, lam_init=lam_init),
        grid=(BATCH,),
        in_specs=[slab(COL_QD), slab(COL_KD), slab(COL_VD), small, small, small, small,
                  _row_spec((1, HEAD_W))],
        out_specs=pl.BlockSpec((SEQ, W), lambda b: (b, 0)),
        out_shape=jax.ShapeDtypeStruct((T_PROMPT, W), F32),
        compiler_params=_cparams(("arbitrary",)),
        name="diff_attn_prompt",
    )(proj, proj, proj, lq1, lk1, lq2, lk2, subln_g)


def _rope(x, cos, sin_signed):
    lane = lax.broadcasted_iota(jnp.int32, x.shape, 1)
    partner = jnp.where((lane % ROPE_GROUP) < ROPE_HALF,
                        pltpu.roll(x, LANES - ROPE_HALF, 1), pltpu.roll(x, ROPE_HALF, 1))
    return x * cos + partner * sin_signed


def _attn_sample_kernel(q_ref, k_ref, v_ref, ck_ref, cv_ref, cosq_ref, sinq_ref, cos_ref, sin_ref,
                        lq1, lk1, lq2, lk2, g_ref, o_ref, kbuf, vbuf, *, lam_init):
    @pl.when(pl.program_id(2) == 0)
    def _():
        kbuf[0:DEC_SEQ, :] = _rope(k_ref[...].astype(F32), cos_ref[...], sin_ref[...]).astype(BF16)
        kbuf[DEC_SEQ:, :] = ck_ref[0, 0, 0].astype(BF16)
        vbuf[0:DEC_SEQ, :] = v_ref[...]
        vbuf[DEC_SEQ:, :] = cv_ref[0, 0, 0].astype(BF16)

    lam = _diff_lambda(lq1, lk1, lq2, lk2, lam_init)
    q = _rope(q_ref[...].astype(F32), cosq_ref[...], sinq_ref[...]) * (DIFF_DK ** -0.5 * LOG2E)
    o_ref[...] = _diff_attend(q, kbuf[...], vbuf[...], lam, g_ref[...], lam_init)


ATTN_TQ = 256


def _attn_sample(proj, cache_k, cache_v, cos, sin_signed, lq1, lk1, lq2, lk2, subln_g, lam_init, e):
    nq = DEC_SEQ // ATTN_TQ
    row0_q = T_PROMPT // ATTN_TQ
    row0_kv = T_PROMPT // DEC_SEQ
    small = pl.BlockSpec((1, DIFF_DK), lambda b, h, t: (0, 0))
    cache = pl.BlockSpec((1, 1, 1, PAST_LEN, HEAD_W), lambda b, h, t: (b, e, h, 0, 0))
    table_q = pl.BlockSpec((ATTN_TQ, HEAD_W), lambda b, h, t: (t, 0))
    table = pl.BlockSpec((DEC_SEQ, HEAD_W), lambda b, h, t: (0, 0))
    return pl.pallas_call(
        functools.partial(_attn_sample_kernel, lam_init=lam_init),
        grid=(DEC_BATCH, HEADS, nq),
        in_specs=[
            pl.BlockSpec((ATTN_TQ, HEAD_W), lambda b, h, t: (row0_q + b * nq + t, COL_QD + h)),
            pl.BlockSpec((DEC_SEQ, HEAD_W), lambda b, h, t: (row0_kv + b, COL_KD + h)),
            pl.BlockSpec((DEC_SEQ, HEAD_W), lambda b, h, t: (row0_kv + b, COL_VD + h)),
            cache, cache, table_q, table_q, table, table,
            small, small, small, small,
            pl.BlockSpec((1, HEAD_W), lambda b, h, t: (0, 0)),
        ],
        out_specs=pl.BlockSpec((ATTN_TQ, HEAD_W), lambda b, h, t: (b * nq + t, h)),
        out_shape=jax.ShapeDtypeStruct((T_SAMPLE, HEADS * HEAD_W), F32),
        scratch_shapes=[pltpu.VMEM((DEC_SEQ + PAST_LEN, HEAD_W), BF16),
                        pltpu.VMEM((DEC_SEQ + PAST_LEN, HEAD_W), BF16)],
        compiler_params=_cparams(("arbitrary", "arbitrary", "arbitrary")),
        name="diff_attn_sample",
    )(proj, proj, proj, cache_k, cache_v, cos, sin_signed, cos, sin_signed,
      lq1, lk1, lq2, lk2, subln_g)


def _rope_tables():
    t = np.arange(DEC_SEQ)
    row, colp = t // GRID_W, t % GRID_W
    lane = np.arange(LANES)
    pos = np.where(((lane // ROPE_GROUP) % 2 == 0)[None, :], row[:, None], colp[:, None]).astype(np.float64)
    half = ROPE_HALF
    inv = (np.float32(ROPE_THETA) ** (-(np.arange(half, dtype=np.float32)) / np.float32(half))).astype(np.float32)
    ang = pos.astype(np.float32) * inv[lane % half][None, :]
    cos = np.cos(ang.astype(np.float64)).astype(np.float32)
    sin = np.sin(ang.astype(np.float64)).astype(np.float32)
    sign = np.where((lane % ROPE_GROUP) < half, -1.0, 1.0).astype(np.float32)[None, :]
    return jnp.asarray(cos), jnp.asarray(sin * sign)


def _split_bf16(a):
    hi = a.astype(BF16)
    return hi, (a - hi.astype(F32)).astype(BF16)


def _mixer_tail(out, x, g1_ref, sc2_ref, sh2_ref, lng_ref, lnb_ref, wr_ref, br_ref,
                x1_ref, u2_ref, meta_ref, cnt_ref):
    x1 = _layer_norm(ALPHA * x + g1_ref[0] * out, lng_ref[...], lnb_ref[...])
    x1_ref[...] = x1
    u2 = x1 * (1.0 + sc2_ref[0]) + sh2_ref[0]
    u2_ref[...] = u2.astype(BF16)

    u_hi, u_lo = _split_bf16(u2)
    w_hi, w_lo = _split_bf16(wr_ref[...])
    logits = _dot(u_hi, w_hi) + (_dot(u_hi, w_lo) + _dot(u_lo, w_hi)) + br_ref[...]
    lane = lax.broadcasted_iota(jnp.int32, logits.shape, 1).astype(F32)
    neg = jnp.float32(-jnp.inf)
    is_g = lane < N_GROUPS
    gl = jnp.where(is_g, logits, neg)
    gmax = jnp.max(gl, axis=-1, keepdims=True)
    gsel = jnp.min(jnp.where(gl == gmax, lane, float(LANES)), axis=-1, keepdims=True)
    p_g = 1.0 / jnp.sum(jnp.where(is_g, jnp.exp(gl - gmax), 0.0), axis=-1, keepdims=True)
    lo = ROUTER_LANE0 + gsel * EXPERTS_PER_GROUP
    el = jnp.where((lane >= lo) & (lane < lo + EXPERTS_PER_GROUP), logits, neg)
    v1 = jnp.max(el, axis=-1, keepdims=True)
    i1 = jnp.min(jnp.where(el == v1, lane, float(LANES)), axis=-1, keepdims=True)
    el2 = jnp.where(lane == i1, neg, el)
    v2 = jnp.max(el2, axis=-1, keepdims=True)
    i2 = jnp.min(jnp.where(el2 == v2, lane, float(LANES)), axis=-1, keepdims=True)
    t = jnp.exp(v2 - v1)
    w1 = p_g / (1.0 + t)
    w2 = w1 * t

    oh1 = (lane == i1).astype(F32)
    oh2 = (lane == i2).astype(F32)
    oh = oh1 + oh2
    r_i = lax.broadcasted_iota(jnp.int32, (TM, TM), 0)
    c_i = lax.broadcasted_iota(jnp.int32, (TM, TM), 1)
    before = (c_i < r_i).astype(BF16)
    earlier = _dot(before, oh.astype(BF16))
    rank1 = jnp.sum(earlier * oh1, axis=-1, keepdims=True)
    rank2 = jnp.sum(earlier * oh2, axis=-1, keepdims=True)
    cnt_ref[0] = jnp.sum(oh, axis=0, keepdims=True)
    cols = (i1, i2, w1, w2, rank1, rank2)
    meta = jnp.zeros_like(logits)
    for k, col in enumerate(cols):
        meta = jnp.where(lane == k, col, meta)
    meta_ref[...] = meta


META_E1, META_E2, META_W1, META_W2, META_RANK1, META_RANK2 = range(6)

_TAIL_OUT_SHAPES = [
    jax.ShapeDtypeStruct((T, D), F32),
    jax.ShapeDtypeStruct((T, D), BF16),
    jax.ShapeDtypeStruct((T, LANES), F32),
    jax.ShapeDtypeStruct((N_TILES, 1, LANES), F32),
]


def _tail_out_specs():
    return [
        pl.BlockSpec((TM, D), lambda i: (i, 0)),
        pl.BlockSpec((TM, D), lambda i: (i, 0)),
        pl.BlockSpec((TM, LANES), lambda i: (i, 0)),
        pl.BlockSpec((1, 1, LANES), lambda i: (i, 0, 0)),
    ]


def _tail_in_specs(li):
    return [
        _mod_spec(li, 2), _mod_spec(li, 4), _mod_spec(li, 3),
        _row_spec((1, D)), _row_spec((1, D)),
        _row_spec((D, LANES)), _row_spec((1, LANES)),
    ]


def _out_proj_kernel(rp_ref, rs_ref, op_ref, os_ref, w_ref, xp_ref, xs_ref, *rest):
    tail_args, w_bf = rest[:-1], rest[-1]
    _cast_weight_once(w_ref, w_bf)
    half = HEADS * HEAD_W
    r = _pick_tile(rp_ref, rs_ref).astype(BF16)
    o = _pick_tile(op_ref, os_ref).astype(BF16)
    out = _dot(r, w_bf[0:half, :]) + _dot(o, w_bf[half:, :])
    _mixer_tail(out, _pick_tile(xp_ref, xs_ref), *tail_args)


def _out_proj_tail(r_p, r_s, o_p, o_s, w_out, x_prompt, x_sample, modr, ln_g, ln_b, wr, br, li):
    half = HEADS * HEAD_W
    return pl.pallas_call(
        _out_proj_kernel,
        grid=(N_TILES,),
        in_specs=[_prompt_tile_spec(half), _sample_tile_spec(half),
                  _prompt_tile_spec(half), _sample_tile_spec(half),
                  _resident_f32_weight((2 * half, D)),
                  _prompt_tile_spec(D), _sample_tile_spec(D)] + _tail_in_specs(li),
        out_specs=_tail_out_specs(),
        out_shape=_TAIL_OUT_SHAPES,
        scratch_shapes=[pltpu.VMEM((2 * half, D), BF16)],
        compiler_params=_cparams(("arbitrary",)),
        name="out_proj_tail",
    )(r_p, r_s, o_p, o_s, w_out, x_prompt, x_sample, modr, modr, modr, ln_g, ln_b, wr, br)


def _conv_glu(y, sh_ref, sc_ref, w_bf, b_ref):
    u = y * (1.0 + sc_ref[0]) + sh_ref[0]
    h = _dot(u.astype(BF16), w_bf[...]) + b_ref[...]
    return h[:, :D] * jax.nn.sigmoid(h[:, D:])


HALO = 16
CONV_ROWS = 64
CONV_COLS = 128


def _depthwise_conv(hp, dw_ref, conv):
    base = HALO - CONV_PAD
    for cb in range(D // CONV_COLS):
        cs = slice(cb * CONV_COLS, (cb + 1) * CONV_COLS)
        for rb in range(TM // CONV_ROWS):
            r0 = rb * CONV_ROWS
            acc = None
            for shift in range(SUBLANES):
                part = None
                for tap in range(CONV_K):
                    off = base + tap
                    if off % SUBLANES != shift:
                        continue
                    a0 = r0 + off - shift
                    term = hp[a0:a0 + CONV_ROWS + SUBLANES, cs] * dw_ref[tap:tap + 1, cs]
                    part = term if part is None else part + term
                part = part[shift:shift + CONV_ROWS, :]
                acc = part if acc is None else acc + part
            conv[r0:r0 + CONV_ROWS, cs] = acc


def _conv_tail_kernel(cur_ref, prev_ref, next_ref, dw_ref, dwb_ref, cg_ref, cb_ref, w2_ref, b2_ref,
                      x_ref, *rest):
    tail_args, (hp, conv, w2_bf) = rest[:-3], rest[-3:]
    _cast_weight_once(w2_ref, w2_bf)
    i = pl.program_id(0)
    k = (i - PROMPT_TILES) % SAMPLE_TILES_PER_SEQ
    in_sample = i >= PROMPT_TILES
    left_ok = jnp.logical_and(in_sample, k != 0)
    right_ok = jnp.logical_and(in_sample, k != SAMPLE_TILES_PER_SEQ - 1)
    hp[0:HALO, :] = jnp.where(left_ok, prev_ref[...], 0.0)
    hp[HALO:HALO + TM, :] = cur_ref[...]
    hp[HALO + TM:HALO + TM + HALO, :] = jnp.where(right_ok, next_ref[...], 0.0)
    _depthwise_conv(hp, dw_ref, conv)
    hc = _silu(_layer_norm(conv[...] + dwb_ref[...], cg_ref[...], cb_ref[...]))
    out = _dot(hc.astype(BF16), w2_bf[...]) + b2_ref[...]
    _mixer_tail(out, x_ref[...], *tail_args)


def _conv_tail(glu, dw, dwb, cg, cb, w2, b2, x, modr, ln_g, ln_b, wr, br, li):
    per = TM // HALO
    last = T // HALO - 1
    return pl.pallas_call(
        _conv_tail_kernel,
        grid=(N_TILES,),
        in_specs=[pl.BlockSpec((TM, D), lambda i: (i, 0)),
                  pl.BlockSpec((HALO, D), lambda i: (jnp.maximum(i * per - 1, 0), 0)),
                  pl.BlockSpec((HALO, D), lambda i: (jnp.minimum((i + 1) * per, last), 0)),
                  _row_spec((CONV_K, D)), _row_spec((1, D)), _row_spec((1, D)), _row_spec((1, D)),
                  _resident_f32_weight((D, D)), _row_spec((1, D)),
                  pl.BlockSpec((TM, D), lambda i: (i, 0))] + _tail_in_specs(li),
        out_specs=_tail_out_specs(),
        out_shape=_TAIL_OUT_SHAPES,
        scratch_shapes=[pltpu.VMEM((TM + 2 * HALO, D), F32), pltpu.VMEM((TM, D), F32),
                        pltpu.VMEM((D, D), BF16)],
        compiler_params=_cparams(("arbitrary",)),
        name="conv_tail",
    )(glu, glu, glu, dw, dwb, cg, cb, w2, b2, x, modr, modr, modr, ln_g, ln_b, wr, br)


def _sorted_positions(meta, srcv):
    lane = lax.broadcasted_iota(jnp.int32, meta.shape, 1).astype(F32)

    def pos(e_col, r_col):
        start = jnp.sum(jnp.where(lane == meta[:, e_col:e_col + 1], srcv, 0.0), axis=-1, keepdims=True)
        return start + meta[:, r_col:r_col + 1]

    return pos(META_E1, META_RANK1), pos(META_E2, META_RANK2)


def _one_hot_rows(pos):
    col = lax.broadcasted_iota(jnp.int32, (TM, SORT_ROWS), 1).astype(F32)
    return col == pos


def _for_each_row_group(tile, tot_ref, dstg_ref, fn):
    def body(k, carry):
        for j in range(GROUP_UNROLL):
            g = k * GROUP_UNROLL + j
            fn(pl.multiple_of(g * RUN_ALIGN, RUN_ALIGN),
               pl.multiple_of(dstg_ref[tile * SORT_GROUPS + g], RUN_ALIGN))
        return carry

    lax.fori_loop(0, tot_ref[tile] // (RUN_ALIGN * GROUP_UNROLL), body, 0)


def _wait_rows(total, make_copy):
    for bit in RUN_BITS:
        @pl.when((total & bit) != 0)
        def _(bit=bit):
            make_copy(bit).wait()


def _dispatch_kernel(tot_ref, dstg_ref, u_ref, meta_ref, srcv_ref, xs_ref, sorted_ref, sems):
    i = pl.program_id(0)
    slot = i % 2

    def wait_tile(tile, slot):
        buf = sorted_ref.at[slot]
        _wait_rows(tot_ref[tile], lambda rows: pltpu.make_async_copy(
            buf.at[pl.ds(0, rows)], xs_ref.at[pl.ds(0, rows)], sems.at[slot]))

    @pl.when(i >= 2)
    def _():
        wait_tile(i - 2, slot)

    pos1, pos2 = _sorted_positions(meta_ref[...], srcv_ref[0])
    select = jnp.logical_or(_one_hot_rows(pos1), _one_hot_rows(pos2)).astype(BF16)
    sorted_ref[slot] = _dot_tn(select, u_ref[...])
    buf = sorted_ref.at[slot]

    def start(src, dst):
        pltpu.make_async_copy(buf.at[pl.ds(src, RUN_ALIGN)], xs_ref.at[pl.ds(dst, RUN_ALIGN)],
                              sems.at[slot]).start()

    _for_each_row_group(i, tot_ref, dstg_ref, start)

    @pl.when(i == N_TILES - 1)
    def _():
        wait_tile(i - 1, 1 - slot)
        wait_tile(i, slot)


def _dispatch(sched, u2, meta, srcv):
    return pl.pallas_call(
        _dispatch_kernel,
        grid_spec=pltpu.PrefetchScalarGridSpec(
            num_scalar_prefetch=2,
            grid=(N_TILES,),
            in_specs=[pl.BlockSpec((TM, D), lambda i, *_: (i, 0)),
                      pl.BlockSpec((TM, LANES), lambda i, *_: (i, 0)),
                      pl.BlockSpec((1, 1, LANES), lambda i, *_: (i, 0, 0))],
            out_specs=pl.BlockSpec(memory_space=pl.ANY),
            scratch_shapes=[pltpu.VMEM((2, SORT_ROWS, D), F32), pltpu.SemaphoreType.DMA((2,))],
        ),
        out_shape=jax.ShapeDtypeStruct((DISPATCH_ROWS, D), F32),
        compiler_params=_cparams(("arbitrary",)),
        name="moe_dispatch",
    )(*sched, u2, meta, srcv)


def _experts_kernel(start_ref, chunks_ref, xs_ref, wg_ref, wu_ref, wd_ref, ys_ref,
                    wg_bf, wu_bf, wd_bf, xbuf, ybuf, in_sems, out_sems):
    e = pl.program_id(0)
    n = chunks_ref[e]
    first = start_ref[e] // MOE_TM
    total = start_ref[N_EXPERTS - 1] // MOE_TM + chunks_ref[N_EXPERTS - 1]

    def rows(g):
        return pl.ds(pl.multiple_of(g * MOE_TM, MOE_TM), MOE_TM)

    def load(g):
        slot = g % MOE_IN_SLOTS
        return pltpu.make_async_copy(xs_ref.at[rows(g)], xbuf.at[slot], in_sems.at[slot])

    def store(g):
        slot = g % MOE_OUT_SLOTS
        return pltpu.make_async_copy(ybuf.at[slot], ys_ref.at[rows(g)], out_sems.at[slot])

    @pl.when(e == 0)
    def _():
        for g in range(MOE_AHEAD):
            @pl.when(g < total)
            def _(g=g):
                load(g).start()

    @pl.when(n > 0)
    def _():
        wg_bf[...] = wg_ref[0, 0].astype(BF16)
        wu_bf[...] = wu_ref[0, 0].astype(BF16)
        wd_bf[...] = wd_ref[0, 0].astype(BF16)

        def tile(g, carry):
            load(g).wait()

            @pl.when(g + MOE_AHEAD < total)
            def _():
                load(g + MOE_AHEAD).start()

            @pl.when(g >= MOE_OUT_SLOTS)
            def _():
                store(g - MOE_OUT_SLOTS).wait()

            x = xbuf[g % MOE_IN_SLOTS].astype(BF16)
            h = (_silu(_dot(x, wg_bf[...])) * _dot(x, wu_bf[...])).astype(BF16)
            ybuf[g % MOE_OUT_SLOTS] = _dot(h, wd_bf[...])
            store(g).start()
            return carry

        lax.fori_loop(first, first + n, tile, 0)

    @pl.when(e == N_EXPERTS - 1)
    def _():
        for back in range(MOE_OUT_SLOTS, 0, -1):
            @pl.when(total >= back)
            def _(back=back):
                store(total - back).wait()


def _experts(seg_start, seg_chunks, xs, w_gate, w_up, w_down, li):
    def weight(shape):
        return pl.BlockSpec((1, 1) + shape, lambda e, *_: (li, e, 0, 0))

    return pl.pallas_call(
        _experts_kernel,
        grid_spec=pltpu.PrefetchScalarGridSpec(
            num_scalar_prefetch=2,
            grid=(N_EXPERTS,),
            in_specs=[pl.BlockSpec(memory_space=pl.ANY),
                      weight((D, D_EXPERT)), weight((D, D_EXPERT)), weight((D_EXPERT, D))],
            out_specs=pl.BlockSpec(memory_space=pl.ANY),
            scratch_shapes=[pltpu.VMEM((D, D_EXPERT), BF16), pltpu.VMEM((D, D_EXPERT), BF16),
                            pltpu.VMEM((D_EXPERT, D), BF16),
                            pltpu.VMEM((MOE_IN_SLOTS, MOE_TM, D), F32),
                            pltpu.VMEM((MOE_OUT_SLOTS, MOE_TM, D), F32),
                            pltpu.SemaphoreType.DMA((MOE_IN_SLOTS,)),
                            pltpu.SemaphoreType.DMA((MOE_OUT_SLOTS,))],
        ),
        out_shape=jax.ShapeDtypeStruct((MOE_ROWS, D), F32),
        compiler_params=_cparams(("arbitrary",)),
        name="moe_experts",
    )(seg_start, seg_chunks, xs, w_gate, w_up, w_down)


def _combine_kernel(tot_ref, dstg_ref, ys_ref, x1_ref, meta_ref, srcv_ref, g2_ref,
                    lng_ref, lnb_ref, *rest, split, feeds_conv):
    if feeds_conv:
        (sh_ref, sc_ref, w1_ref, b1_ref), rest, w1_bf = rest[:4], rest[4:-1], rest[-1]
        _cast_weight_once(w1_ref, w1_bf)
    outs, (sorted_ref, sems) = rest[:-2], rest[-2:]
    i = pl.program_id(0)
    slot = i % 2

    def fetch(tile, slot):
        buf = sorted_ref.at[slot]

        def start(src, dst):
            pltpu.make_async_copy(ys_ref.at[pl.ds(dst, RUN_ALIGN)], buf.at[pl.ds(src, RUN_ALIGN)],
                                  sems.at[slot]).start()

        _for_each_row_group(tile, tot_ref, dstg_ref, start)

    @pl.when(i == 0)
    def _():
        sorted_ref[...] = jnp.zeros_like(sorted_ref)
        fetch(0, 0)

    @pl.when(i + 1 < N_TILES)
    def _():
        fetch(i + 1, 1 - slot)

    meta = meta_ref[...]
    pos1, pos2 = _sorted_positions(meta, srcv_ref[0])
    sel1 = _one_hot_rows(pos1).astype(BF16)
    sel2 = _one_hot_rows(pos2).astype(BF16)
    buf = sorted_ref.at[slot]
    _wait_rows(tot_ref[i], lambda rows: pltpu.make_async_copy(
        ys_ref.at[pl.ds(0, rows)], buf.at[pl.ds(0, rows)], sems.at[slot]))
    ysort = sorted_ref[slot].astype(BF16)
    f = (meta[:, META_W1:META_W1 + 1] * _dot(sel1, ysort)
         + meta[:, META_W2:META_W2 + 1] * _dot(sel2, ysort))
    y = _layer_norm(ALPHA * x1_ref[...] + g2_ref[0] * f, lng_ref[...], lnb_ref[...])
    if split:
        @pl.when(i < PROMPT_TILES)
        def _():
            outs[0][...] = y

        @pl.when(i >= PROMPT_TILES)
        def _():
            outs[1][...] = y
    else:
        outs[0][...] = y
    if feeds_conv:
        outs[-1][...] = _conv_glu(y, sh_ref, sc_ref, w1_bf, b1_ref)


def _combine(sched, ys, x1, meta, srcv, modr, ln_g, ln_b, li, split, conv_w1=None, conv_b1=None):
    feeds_conv = conv_w1 is not None
    tile = pl.BlockSpec((TM, D), lambda i, *_: (i, 0))
    if split:
        out_specs = [_prompt_tile_spec(D), _sample_tile_spec(D)]
        out_shape = [jax.ShapeDtypeStruct((T_PROMPT, D), F32), jax.ShapeDtypeStruct((T_SAMPLE, D), F32)]
    else:
        out_specs = [tile]
        out_shape = [jax.ShapeDtypeStruct((T, D), F32)]
    extra_specs, extra_args, extra_scratch = [], [], []
    if feeds_conv:
        extra_specs = [_mod_spec(li + 1, 0), _mod_spec(li + 1, 1),
                       _resident_f32_weight((D, 2 * D)), _row_spec((1, 2 * D))]
        extra_args = [modr, modr, conv_w1, conv_b1]
        extra_scratch = [pltpu.VMEM((D, 2 * D), BF16)]
        out_specs = out_specs + [tile]
        out_shape = out_shape + [jax.ShapeDtypeStruct((T, D), F32)]
    return pl.pallas_call(
        functools.partial(_combine_kernel, split=split, feeds_conv=feeds_conv),
        grid_spec=pltpu.PrefetchScalarGridSpec(
            num_scalar_prefetch=2,
            grid=(N_TILES,),
            in_specs=[pl.BlockSpec(memory_space=pl.ANY),
                      pl.BlockSpec((TM, D), lambda i, *_: (i, 0)),
                      pl.BlockSpec((TM, LANES), lambda i, *_: (i, 0)),
                      pl.BlockSpec((1, 1, LANES), lambda i, *_: (i, 0, 0)),
                      _mod_spec(li, 5), _row_spec((1, D)), _row_spec((1, D))] + extra_specs,
            out_specs=out_specs,
            scratch_shapes=[pltpu.VMEM((2, SORT_ROWS, D), F32), pltpu.SemaphoreType.DMA((2,))] + extra_scratch,
        ),
        out_shape=out_shape,
        compiler_params=_cparams(("arbitrary",)),
        name="moe_combine",
    )(*sched, ys, x1, meta, srcv, modr, ln_g, ln_b, *extra_args)


def _moe_schedule(tile_counts):
    n = (tile_counts + RUN_ALIGN - 1) // RUN_ALIGN * RUN_ALIGN
    src = jnp.cumsum(n, axis=1) - n
    per_expert = jnp.sum(n, axis=0)
    seg = (per_expert + MOE_TM - 1) // MOE_TM * MOE_TM
    seg_start = jnp.cumsum(seg) - seg
    dst = seg_start[None, :] + jnp.cumsum(n, axis=0) - n
    g_row = jnp.arange(SORT_GROUPS, dtype=jnp.int32) * RUN_ALIGN
    in_run = jnp.logical_and(src[:, None, :] <= g_row[None, :, None],
                             g_row[None, :, None] < (src + n)[:, None, :])
    dst_g = g_row[None, :] + jnp.sum(jnp.where(in_run, (dst - src)[:, None, :], 0), axis=2)
    step_rows = RUN_ALIGN * GROUP_UNROLL
    rows = jnp.sum(n, axis=1)
    issued = (rows + step_rows - 1) // step_rows * step_rows
    real = g_row[None, :] < rows[:, None]
    spill = (MOE_ROWS + jnp.arange(N_TILES, dtype=jnp.int32)[:, None] * step_rows
             + g_row[None, :] % step_rows)
    scatter_runs = (issued, jnp.where(real, dst_g, spill).reshape(-1))
    gather_runs = (issued, jnp.where(real, dst_g, dst_g[:, :1]).reshape(-1))
    srcv = jnp.pad(src.astype(F32), ((0, 0), (ROUTER_LANE0, LANES - ROUTER_LANE0 - N_EXPERTS)))
    return scatter_runs, gather_runs, srcv.reshape(N_TILES, 1, LANES), (seg_start, seg // MOE_TM)


def _moe(x1, u2, meta, cnt, modr, w_gate, w_up, w_down, ln_g, ln_b, li, split, **next_conv):
    tile_counts = cnt[:, 0, ROUTER_LANE0:ROUTER_LANE0 + N_EXPERTS].astype(jnp.int32)
    scatter_runs, gather_runs, srcv, (seg_start, seg_chunks) = _moe_schedule(tile_counts)
    xs = _dispatch(scatter_runs, u2, meta, srcv)
    ys = _experts(seg_start, seg_chunks, xs, w_gate, w_up, w_down, li)
    return _combine(gather_runs, ys, x1, meta, srcv, modr, ln_g, ln_b, li, split, **next_conv)


def _router_slab(wg, bg, we, be):
    w = jnp.concatenate([wg, we.transpose(1, 0, 2).reshape(D, N_EXPERTS)], axis=1)
    b = jnp.concatenate([bg, be.reshape(N_EXPERTS)])
    pad = LANES - w.shape[1]
    return jnp.pad(w, ((0, 0), (0, pad))), jnp.pad(b, (0, pad)).reshape(1, LANES)


def kernel(x_prompt, x_sample, cache_diff_k, cache_diff_v, state_ret_fwd, state_ret_bwd, c, c_ctx, mod_w, mod_b, ln1_g, ln1_b, ln2_g, ln2_b, mix_w_in, mix_w_out, ret_decay_fwd, ret_decay_bwd, diff_lq1, diff_lk1, diff_lq2, diff_lk2, diff_subln_g, conv_w1, conv_b1, conv_dw, conv_dw_b, conv_ln_g, conv_ln_b, conv_w2, conv_b2, router_g_w, router_g_b, router_e_w, router_e_b, moe_w_gate, moe_w_up, moe_w_down):
    xp = x_prompt.reshape(T_PROMPT, D)
    xs = x_sample.reshape(T_SAMPLE, D)
    cond = jnp.concatenate([c_ctx[None, :], c, jnp.zeros((MOD_ROWS - 1 - DEC_BATCH, D), F32)], axis=0)
    modr = _mod_vectors(cond, mod_w, mod_b).reshape(DEPTH * MOD_ROWS * 6, 1, D)
    cos, sin_signed = _rope_tables()

    def row(v):
        return v.reshape(1, -1)

    x = None
    caches = None
    for li in range(DEPTH):
        wr, br = _router_slab(router_g_w[li], router_g_b[li], router_e_w[li], router_e_b[li])
        if li % 2 == 0:
            assert li == 0, "the even mixer reads the kernel inputs directly"
            e = li // 2
            lam_init = 0.8 - 0.6 * math.exp(-0.3 * li)
            proj, ck, cv = _in_proj(xp, xs, modr, mix_w_in[e], li)
            dec = jnp.concatenate([ret_decay_fwd[e], ret_decay_bwd[e]])
            r_p, sf, sb = _retention(proj, dec, BATCH, SEQ, 0, HEADS, emit_state=True)
            (r_s,) = _retention(proj, dec, DEC_BATCH, DEC_SEQ, T_PROMPT // DEC_SEQ, 2,
                                s0f=state_ret_fwd, s0b=state_ret_bwd, e=e)
            lams = (row(diff_lq1[e]), row(diff_lk1[e]), row(diff_lq2[e]), row(diff_lk2[e]),
                    row(diff_subln_g[e]))
            o_p = _attn_prompt(proj, *lams, lam_init)
            o_s = _attn_sample(proj, cache_diff_k, cache_diff_v, cos, sin_signed, *lams, lam_init, e)
            x1, u2, meta, cnt = _out_proj_tail(r_p, r_s, o_p, o_s, mix_w_out[e], xp, xs, modr,
                                               row(ln1_g[li]), row(ln1_b[li]), wr, br, li)
            caches = (ck, cv, sf, sb)
        else:
            o = li // 2
            x1, u2, meta, cnt = _conv_tail(glu, conv_dw[o], row(conv_dw_b[o]), row(conv_ln_g[o]),
                                           row(conv_ln_b[o]), conv_w2[o], row(conv_b2[o]),
                                           x, modr, row(ln1_g[li]), row(ln1_b[li]), wr, br, li)
        next_conv = {}
        if li + 1 < DEPTH and (li + 1) % 2 == 1:
            next_conv = dict(conv_w1=conv_w1[(li + 1) // 2], conv_b1=row(conv_b1[(li + 1) // 2]))
        outs = _moe(x1, u2, meta, cnt, modr, moe_w_gate, moe_w_up, moe_w_down,
                    row(ln2_g[li]), row(ln2_b[li]), li, split=(li == DEPTH - 1), **next_conv)
        x, glu = outs[0], outs[-1]

    y_prompt = outs[0].reshape(BATCH, SEQ, D)
    y_sample = outs[1].reshape(DEC_BATCH, DEC_SEQ, D)
    return (y_prompt, y_sample) + caches
```

```python
import functools
import math

import numpy as np
import jax
import jax.numpy as jnp
from jax import lax
from jax.experimental import pallas as pl
from jax.experimental.pallas import tpu as pltpu

F32 = jnp.float32
BF16 = jnp.bfloat16

D = 1024
BATCH = 16
SEQ = 256
DEPTH = 2
DEC_BATCH = 2
DEC_SEQ = 2048
PAST_LEN = 512
GRID_W = 64
HEADS = 4
HEAD_W = 128
RET_CHUNK = 128
DIFF_DK = 64
ROPE_THETA = 10000.0
IN_W = 7 * HEADS * HEAD_W
CONV_K = 31
CONV_PAD = CONV_K // 2
N_GROUPS = 4
EXPERTS_PER_GROUP = 8
N_EXPERTS = N_GROUPS * EXPERTS_PER_GROUP
D_EXPERT = 512
ALPHA = (2.0 * DEPTH) ** 0.25
LN_EPS = 1e-5
GN_EPS = 1e-6

T_PROMPT = BATCH * SEQ
T_SAMPLE = DEC_BATCH * DEC_SEQ
T = T_PROMPT + T_SAMPLE
TM = 256
N_TILES = T // TM
PROMPT_TILES = T_PROMPT // TM
SAMPLE_TILES_PER_SEQ = DEC_SEQ // TM
MOD_ROWS = 8
MOE_TM = 256
LANES = 128
SUBLANES = 8
RUN_ALIGN = SUBLANES
SORT_ROWS = -(-(2 * TM + N_EXPERTS * (RUN_ALIGN - 1)) // TM) * TM
RUN_BITS = tuple(1 << b for b in range((2 * TM).bit_length() - 1, RUN_ALIGN.bit_length() - 2, -1))
MOE_MAX_TILES = -(-(2 * T + N_TILES * N_EXPERTS * (RUN_ALIGN - 1) + N_EXPERTS * (MOE_TM - RUN_ALIGN)) // MOE_TM)
MOE_ROWS = MOE_MAX_TILES * MOE_TM
SORT_GROUPS = SORT_ROWS // RUN_ALIGN
GROUP_UNROLL = 8
DISPATCH_ROWS = MOE_ROWS + N_TILES * RUN_ALIGN * GROUP_UNROLL
MOE_AHEAD = 8
MOE_IN_SLOTS = MOE_AHEAD + 1
MOE_OUT_SLOTS = 4
ROUTER_LANE0 = N_GROUPS
ROPE_GROUP = DIFF_DK // 2
ROPE_HALF = ROPE_GROUP // 2
V7X_VMEM_BYTES = 64 * 1024 * 1024
VMEM_LIMIT = V7X_VMEM_BYTES - 12 * 1024 * 1024


def _cparams(sem):
    return pltpu.CompilerParams(dimension_semantics=sem, vmem_limit_bytes=VMEM_LIMIT)


def _tile_cond_row(i, tm):
    return jnp.where(i < T_PROMPT // tm, 0, 1 + (i - T_PROMPT // tm) // (DEC_SEQ // tm))


def _mod_spec(li, k, tm=TM):
    return pl.BlockSpec((1, 1, D), lambda i, *_: ((li * MOD_ROWS + _tile_cond_row(i, tm)) * 6 + k, 0, 0))


def _row_spec(shape):
    return pl.BlockSpec(shape, lambda i, *_: (0,) * len(shape))


def _resident_f32_weight(shape):
    return pl.BlockSpec(shape, lambda i, *_: (0,) * len(shape), pipeline_mode=pl.Buffered(1))


def _cast_weight_once(w_ref, w_bf):
    @pl.when(pl.program_id(0) == 0)
    def _():
        w_bf[...] = w_ref[...].astype(BF16)


def _layer_norm(x, g, b):
    mu = jnp.mean(x, axis=-1, keepdims=True)
    xc = x - mu
    var = jnp.mean(xc * xc, axis=-1, keepdims=True)
    return xc * lax.rsqrt(var + LN_EPS) * g + b


def _silu(x):
    return x * jax.nn.sigmoid(x)


def _dot(a, b):
    return jnp.dot(a, b, preferred_element_type=F32)


def _dot_nt(a, b):
    return lax.dot_general(a, b, (((1,), (1,)), ((), ())), preferred_element_type=F32)


def _dot_tn(a, b):
    return lax.dot_general(a, b, (((0,), (0,)), ((), ())), preferred_element_type=F32)


MOD_TN = 2048
MOD_USED_ROWS = 1 + DEC_BATCH


def _mod_kernel(cond_t_ref, w_ref, b_ref, o_ref):
    s = _silu(cond_t_ref[...])
    w = w_ref[0]
    o_ref[0] = jnp.zeros((MOD_ROWS, MOD_TN), F32) + b_ref[0]
    for r in range(MOD_USED_ROWS):
        o_ref[0, r:r + 1, :] = jnp.sum(w * s[:, r:r + 1], axis=0, keepdims=True) + b_ref[0]


def _mod_vectors(cond, mod_w, mod_b):
    return pl.pallas_call(
        _mod_kernel,
        grid=(DEPTH, 6 * D // MOD_TN),
        in_specs=[
            pl.BlockSpec((D, MOD_ROWS), lambda l, j: (0, 0)),
            pl.BlockSpec((1, D, MOD_TN), lambda l, j: (l, 0, j)),
            pl.BlockSpec((1, 1, MOD_TN), lambda l, j: (l, 0, j)),
        ],
        out_specs=pl.BlockSpec((1, MOD_ROWS, MOD_TN), lambda l, j: (l, 0, j)),
        out_shape=jax.ShapeDtypeStruct((DEPTH, MOD_ROWS, 6 * D), F32),
        compiler_params=_cparams(("arbitrary", "arbitrary")),
        name="mod_vectors",
    )(cond.T, mod_w, mod_b.reshape(DEPTH, 1, 6 * D))


def _prompt_tile_spec(width, tm=TM):
    return pl.BlockSpec((tm, width), lambda i, *_: (jnp.minimum(i, T_PROMPT // tm - 1), 0))


def _sample_tile_spec(width, tm=TM):
    return pl.BlockSpec((tm, width), lambda i, *_: (jnp.maximum(i - T_PROMPT // tm, 0), 0))


def _pick_tile(prompt_ref, sample_ref, tm=TM):
    return jnp.where(pl.program_id(0) < T_PROMPT // tm, prompt_ref[...], sample_ref[...])


IN_TM = 512
IN_SEQS = IN_TM // SEQ


def _in_proj_kernel(xp_ref, xs_ref, sh_ref, sc_ref, w_ref, o_ref, ck_ref, cv_ref, w_bf):
    _cast_weight_once(w_ref, w_bf)
    u = _pick_tile(xp_ref, xs_ref, IN_TM) * (1.0 + sc_ref[0]) + sh_ref[0]
    proj = _dot(u.astype(BF16), w_bf[...])
    o_ref[...] = proj.astype(BF16)

    @pl.when(pl.program_id(0) < T_PROMPT // IN_TM)
    def _():
        for s in range(IN_SEQS):
            rows = slice(s * SEQ, (s + 1) * SEQ)
            for h in range(HEADS):
                ck_ref[s, 0, h] = proj[rows, (COL_KD + h) * HEAD_W:(COL_KD + h + 1) * HEAD_W]
                cv_ref[s, 0, h] = proj[rows, (COL_VD + h) * HEAD_W:(COL_VD + h + 1) * HEAD_W]


def _in_proj(x_prompt, x_sample, modr, w_in, li):
    cache_spec = pl.BlockSpec((IN_SEQS, 1, HEADS, SEQ, HEAD_W),
                              lambda i: (jnp.minimum(i, T_PROMPT // IN_TM - 1), 0, 0, 0, 0))
    cache_shape = jax.ShapeDtypeStruct((BATCH, 1, HEADS, SEQ, HEAD_W), F32)
    return pl.pallas_call(
        _in_proj_kernel,
        grid=(T // IN_TM,),
        in_specs=[
            _prompt_tile_spec(D, IN_TM), _sample_tile_spec(D, IN_TM),
            _mod_spec(li, 0, IN_TM),
            _mod_spec(li, 1, IN_TM),
            _resident_f32_weight((D, IN_W)),
        ],
        out_specs=[pl.BlockSpec((IN_TM, IN_W), lambda i: (i, 0)), cache_spec, cache_spec],
        out_shape=[jax.ShapeDtypeStruct((T, IN_W), BF16), cache_shape, cache_shape],
        scratch_shapes=[pltpu.VMEM((D, IN_W), BF16)],
        compiler_params=_cparams(("arbitrary",)),
        name="in_proj",
    )(x_prompt, x_sample, modr, modr, w_in)


COL_QR, COL_KR, COL_VR, COL_GR, COL_QD, COL_KD, COL_VD = (k * HEADS for k in range(7))


def _retention_kernel(dec_ref, q_ref, k_ref, v_ref, g_ref, *rest, n_chunks, n_heads, has_state, emit_state):
    rest = list(rest)
    if has_state:
        s0f_ref, s0b_ref = rest[:2]
        rest = rest[2:]
    r_ref = rest[0]
    rest = rest[1:]
    if emit_state:
        sf_ref, sb_ref = rest[:2]
        rest = rest[2:]
    of_ref = rest[0]

    head0 = pl.program_id(1) * n_heads
    C = RET_CHUNK
    ii = lax.broadcasted_iota(jnp.int32, (C, C), 0)
    jj = lax.broadcasted_iota(jnp.int32, (C, C), 1)
    rel = (ii - jj).astype(F32)
    idx = lax.broadcasted_iota(jnp.int32, (C, 1), 0).astype(F32)
    k_scale = HEAD_W ** -0.5

    def chunk(ref, c, h):
        return ref[c * C:(c + 1) * C, h * HEAD_W:(h + 1) * HEAD_W].astype(F32)

    def decays(direction, h):
        lg = -jnp.exp(jnp.full((1, 1), dec_ref[direction * HEADS + head0 + h], F32))
        if direction == 0:
            inner = jnp.where(rel >= 0, jnp.exp(jnp.maximum(rel, 0.0) * lg), 0.0)
            return inner, jnp.exp((idx + 1.0) * lg), jnp.exp((C - 1.0 - idx) * lg), jnp.exp(C * lg)
        inner = jnp.where(rel <= 0, jnp.exp(jnp.maximum(-rel, 0.0) * lg), 0.0)
        return inner, jnp.exp((C - idx) * lg), jnp.exp(idx * lg), jnp.exp(C * lg)

    def run(direction):
        dec = [decays(direction, h) for h in range(n_heads)]
        if has_state:
            s0_ref = s0f_ref if direction == 0 else s0b_ref
            states = [s0_ref[0, 0, h] for h in range(n_heads)]
        else:
            states = [jnp.zeros((HEAD_W, HEAD_W), F32) for _ in range(n_heads)]
        order = range(n_chunks) if direction == 0 else range(n_chunks - 1, -1, -1)
        for c in order:
            rows = slice(c * C, (c + 1) * C)
            for h in range(n_heads):
                inner, q_decay, k_decay, chunk_decay = dec[h]
                cols = slice(h * HEAD_W, (h + 1) * HEAD_W)
                s = states[h]
                qc = chunk(q_ref, c, h)
                kc = chunk(k_ref, c, h) * k_scale
                vc = chunk(v_ref, c, h).astype(BF16)
                scores = _dot_nt(qc.astype(BF16), kc.astype(BF16)) * inner
                o = _dot(scores.astype(BF16), vc) + _dot((qc * q_decay).astype(BF16), s.astype(BF16))
                states[h] = s * chunk_decay + _dot_tn((kc * k_decay).astype(BF16), vc)
                if direction == 0:
                    of_ref[rows, cols] = o
                else:
                    r = of_ref[rows, cols] + o
                    mu = jnp.mean(r, axis=-1, keepdims=True)
                    rc = r - mu
                    var = jnp.mean(rc * rc, axis=-1, keepdims=True)
                    rn = rc * lax.rsqrt(var + GN_EPS)
                    r_ref[rows, cols] = _silu(chunk(g_ref, c, h)) * rn
        return states

    sf = run(0)
    sb = run(1)
    if emit_state:
        for h in range(n_heads):
            sf_ref[0, 0, h] = sf[h]
            sb_ref[0, 0, h] = sb[h]


def _retention(proj, dec, n_seq, seq_len, row_block0, n_heads, s0f=None, s0b=None, e=0, emit_state=False):
    has_state = s0f is not None
    width = n_heads * HEAD_W

    def col(base):
        return pl.BlockSpec((seq_len, width), lambda b, h, *_: (row_block0 + b, base // n_heads + h))

    state_spec = pl.BlockSpec((1, 1, n_heads, HEAD_W, HEAD_W), lambda b, h, *_: (b, e, h, 0, 0))
    in_specs = [pl.BlockSpec(memory_space=pltpu.SMEM), col(COL_QR), col(COL_KR), col(COL_VR), col(COL_GR)]
    args = [dec, proj, proj, proj, proj]
    if has_state:
        in_specs += [state_spec, state_spec]
        args += [s0f, s0b]
    out_specs = [pl.BlockSpec((seq_len, width), lambda b, h, *_: (b, h))]
    out_shape = [jax.ShapeDtypeStruct((n_seq * seq_len, HEADS * HEAD_W), F32)]
    if emit_state:
        st = pl.BlockSpec((1, 1, n_heads, HEAD_W, HEAD_W), lambda b, h, *_: (b, 0, h, 0, 0))
        out_specs += [st, st]
        out_shape += [jax.ShapeDtypeStruct((n_seq, 1, HEADS, HEAD_W, HEAD_W), F32)] * 2
    return pl.pallas_call(
        functools.partial(_retention_kernel, n_chunks=seq_len // RET_CHUNK, n_heads=n_heads,
                          has_state=has_state, emit_state=emit_state),
        grid=(n_seq, HEADS // n_heads),
        in_specs=in_specs,
        out_specs=out_specs,
        out_shape=out_shape,
        scratch_shapes=[pltpu.VMEM((seq_len, width), F32)],
        compiler_params=_cparams(("arbitrary", "arbitrary")),
        name=f"retention_{seq_len}",
    )(*args)


def _diff_lambda(lq1_ref, lk1_ref, lq2_ref, lk2_ref, lam_init):
    a = jnp.sum(lq1_ref[...] * lk1_ref[...], axis=-1, keepdims=True)
    b = jnp.sum(lq2_ref[...] * lk2_ref[...], axis=-1, keepdims=True)
    return jnp.exp(a) - jnp.exp(b) + lam_init


LOG2E = 1.4426950408889634


def _diff_attend(q, k, v, lam, subln_g, lam_init):
    lane = lax.broadcasted_iota(jnp.int32, q.shape, 1)
    q1 = jnp.where(lane < DIFF_DK, q, 0.0).astype(BF16)
    q2 = jnp.where(lane >= DIFF_DK, q, 0.0).astype(BF16)

    def softmax_times_v(qz):
        s = _dot_nt(qz, k)
        p = jnp.exp2(s - jnp.max(s, axis=-1, keepdims=True))
        return _dot(p.astype(BF16), v) * (1.0 / jnp.sum(p, axis=-1, keepdims=True))

    o = softmax_times_v(q1) - lam * softmax_times_v(q2)
    o = o * lax.rsqrt(jnp.mean(o * o, axis=-1, keepdims=True) + LN_EPS)
    return o * subln_g * (1.0 - lam_init)


def _attn_prompt_kernel(q_ref, k_ref, v_ref, lq1, lk1, lq2, lk2, g_ref, o_ref, *, lam_init):
    lam = _diff_lambda(lq1, lk1, lq2, lk2, lam_init)
    scale = DIFF_DK ** -0.5 * LOG2E
    for h in range(HEADS):
        sl = slice(h * HEAD_W, (h + 1) * HEAD_W)
        o_ref[:, sl] = _diff_attend(q_ref[:, sl].astype(F32) * scale, k_ref[:, sl], v_ref[:, sl],
                                    lam, g_ref[...], lam_init)


def _attn_prompt(proj, lq1, lk1, lq2, lk2, subln_g, lam_init):
    W = HEADS * HEAD_W

    def slab(base):
        return pl.BlockSpec((SEQ, W), lambda b: (b, base // HEADS))

    small = _row_spec((1, DIFF_DK))
    return pl.pallas_call(
        functools.partial(_attn_prompt_kernel, lam_init=lam_init),
        grid=(BATCH,),
        in_specs=[slab(COL_QD), slab(COL_KD), slab(COL_VD), small, small, small, small,
                  _row_spec((1, HEAD_W))],
        out_specs=pl.BlockSpec((SEQ, W), lambda b: (b, 0)),
        out_shape=jax.ShapeDtypeStruct((T_PROMPT, W), F32),
        compiler_params=_cparams(("arbitrary",)),
        name="diff_attn_prompt",
    )(proj, proj, proj, lq1, lk1, lq2, lk2, subln_g)


def _rope(x, cos, sin_signed):
    lane = lax.broadcasted_iota(jnp.int32, x.shape, 1)
    partner = jnp.where((lane % ROPE_GROUP) < ROPE_HALF,
                        pltpu.roll(x, LANES - ROPE_HALF, 1), pltpu.roll(x, ROPE_HALF, 1))
    return x * cos + partner * sin_signed


def _attn_sample_kernel(q_ref, k_ref, v_ref, ck_ref, cv_ref, cosq_ref, sinq_ref, cos_ref, sin_ref,
                        lq1, lk1, lq2, lk2, g_ref, o_ref, kbuf, vbuf, *, lam_init):
    @pl.when(pl.program_id(2) == 0)
    def _():
        kbuf[0:DEC_SEQ, :] = _rope(k_ref[...].astype(F32), cos_ref[...], sin_ref[...]).astype(BF16)
        kbuf[DEC_SEQ:, :] = ck_ref[0, 0, 0].astype(BF16)
        vbuf[0:DEC_SEQ, :] = v_ref[...]
        vbuf[DEC_SEQ:, :] = cv_ref[0, 0, 0].astype(BF16)

    lam = _diff_lambda(lq1, lk1, lq2, lk2, lam_init)
    q = _rope(q_ref[...].astype(F32), cosq_ref[...], sinq_ref[...]) * (DIFF_DK ** -0.5 * LOG2E)
    o_ref[...] = _diff_attend(q, kbuf[...], vbuf[...], lam, g_ref[...], lam_init)


ATTN_TQ = 256


def _attn_sample(proj, cache_k, cache_v, cos, sin_signed, lq1, lk1, lq2, lk2, subln_g, lam_init, e):
    nq = DEC_SEQ // ATTN_TQ
    row0_q = T_PROMPT // ATTN_TQ
    row0_kv = T_PROMPT // DEC_SEQ
    small = pl.BlockSpec((1, DIFF_DK), lambda b, h, t: (0, 0))
    cache = pl.BlockSpec((1, 1, 1, PAST_LEN, HEAD_W), lambda b, h, t: (b, e, h, 0, 0))
    table_q = pl.BlockSpec((ATTN_TQ, HEAD_W), lambda b, h, t: (t, 0))
    table = pl.BlockSpec((DEC_SEQ, HEAD_W), lambda b, h, t: (0, 0))
    return pl.pallas_call(
        functools.partial(_attn_sample_kernel, lam_init=lam_init),
        grid=(DEC_BATCH, HEADS, nq),
        in_specs=[
            pl.BlockSpec((ATTN_TQ, HEAD_W), lambda b, h, t: (row0_q + b * nq + t, COL_QD + h)),
            pl.BlockSpec((DEC_SEQ, HEAD_W), lambda b, h, t: (row0_kv + b, COL_KD + h)),
            pl.BlockSpec((DEC_SEQ, HEAD_W), lambda b, h, t: (row0_kv + b, COL_VD + h)),
            cache, cache, table_q, table_q, table, table,
            small, small, small, small,
            pl.BlockSpec((1, HEAD_W), lambda b, h, t: (0, 0)),
        ],
        out_specs=pl.BlockSpec((ATTN_TQ, HEAD_W), lambda b, h, t: (b * nq + t, h)),
        out_shape=jax.ShapeDtypeStruct((T_SAMPLE, HEADS * HEAD_W), F32),
        scratch_shapes=[pltpu.VMEM((DEC_SEQ + PAST_LEN, HEAD_W), BF16),
                        pltpu.VMEM((DEC_SEQ + PAST_LEN, HEAD_W), BF16)],
        compiler_params=_cparams(("arbitrary", "arbitrary", "arbitrary")),
        name="diff_attn_sample",
    )(proj, proj, proj, cache_k, cache_v, cos, sin_signed, cos, sin_signed,
      lq1, lk1, lq2, lk2, subln_g)


def _rope_tables():
    t = np.arange(DEC_SEQ)
    row, colp = t // GRID_W, t % GRID_W
    lane = np.arange(LANES)
    pos = np.where(((lane // ROPE_GROUP) % 2 == 0)[None, :], row[:, None], colp[:, None]).astype(np.float64)
    half = ROPE_HALF
    inv = (np.float32(ROPE_THETA) ** (-(np.arange(half, dtype=np.float32)) / np.float32(half))).astype(np.float32)
    ang = pos.astype(np.float32) * inv[lane % half][None, :]
    cos = np.cos(ang.astype(np.float64)).astype(np.float32)
    sin = np.sin(ang.astype(np.float64)).astype(np.float32)
    sign = np.where((lane % ROPE_GROUP) < half, -1.0, 1.0).astype(np.float32)[None, :]
    return jnp.asarray(cos), jnp.asarray(sin * sign)


def _split_bf16(a):
    hi = a.astype(BF16)
    return hi, (a - hi.astype(F32)).astype(BF16)


def _mixer_tail(out, x, g1_ref, sc2_ref, sh2_ref, lng_ref, lnb_ref, wr_ref, br_ref,
                x1_ref, u2_ref, meta_ref, cnt_ref, wr_bf, before_bf):
    @pl.when(pl.program_id(0) == 0)
    def _():
        w_hi, w_lo = _split_bf16(wr_ref[...])
        wr_bf[:, 0:LANES] = w_hi
        wr_bf[:, LANES:] = w_lo
        r_i = lax.broadcasted_iota(jnp.int32, (TM, TM), 0)
        c_i = lax.broadcasted_iota(jnp.int32, (TM, TM), 1)
        before_bf[...] = (c_i < r_i).astype(BF16)

    x1 = _layer_norm(ALPHA * x + g1_ref[0] * out, lng_ref[...], lnb_ref[...])
    x1_ref[...] = x1
    u2 = x1 * (1.0 + sc2_ref[0]) + sh2_ref[0]
    u2_ref[...] = u2.astype(BF16)

    u_hi, u_lo = _split_bf16(u2)
    hi = _dot(u_hi, wr_bf[...])
    logits = hi[:, 0:LANES] + (hi[:, LANES:] + _dot(u_lo, wr_bf[:, 0:LANES])) + br_ref[...]
    lane = lax.broadcasted_iota(jnp.int32, logits.shape, 1).astype(F32)
    neg = jnp.float32(-jnp.inf)
    is_g = lane < N_GROUPS
    gl = jnp.where(is_g, logits, neg)
    gmax = jnp.max(gl, axis=-1, keepdims=True)
    gsel = jnp.min(jnp.where(gl == gmax, lane, float(LANES)), axis=-1, keepdims=True)
    p_g = 1.0 / jnp.sum(jnp.where(is_g, jnp.exp(gl - gmax), 0.0), axis=-1, keepdims=True)
    lo = ROUTER_LANE0 + gsel * EXPERTS_PER_GROUP
    el = jnp.where((lane >= lo) & (lane < lo + EXPERTS_PER_GROUP), logits, neg)
    v1 = jnp.max(el, axis=-1, keepdims=True)
    i1 = jnp.min(jnp.where(el == v1, lane, float(LANES)), axis=-1, keepdims=True)
    el2 = jnp.where(lane == i1, neg, el)
    v2 = jnp.max(el2, axis=-1, keepdims=True)
    i2 = jnp.min(jnp.where(el2 == v2, lane, float(LANES)), axis=-1, keepdims=True)
    t = jnp.exp(v2 - v1)
    w1 = p_g / (1.0 + t)
    w2 = w1 * t

    oh1 = (lane == i1).astype(F32)
    oh2 = (lane == i2).astype(F32)
    oh = oh1 + oh2
    earlier = _dot(before_bf[...], oh.astype(BF16))
    rank1 = jnp.sum(earlier * oh1, axis=-1, keepdims=True)
    rank2 = jnp.sum(earlier * oh2, axis=-1, keepdims=True)
    cnt_ref[0] = jnp.sum(oh, axis=0, keepdims=True)
    cols = (i1, i2, w1, w2, rank1, rank2)
    meta = jnp.zeros_like(logits)
    for k, col in enumerate(cols):
        meta = jnp.where(lane == k, col, meta)
    meta_ref[...] = meta


META_E1, META_E2, META_W1, META_W2, META_RANK1, META_RANK2 = range(6)

def _tail_scratch():
    return [pltpu.VMEM((D, 2 * LANES), BF16), pltpu.VMEM((TM, TM), BF16)]


_TAIL_OUT_SHAPES = [
    jax.ShapeDtypeStruct((T, D), F32),
    jax.ShapeDtypeStruct((T, D), BF16),
    jax.ShapeDtypeStruct((T, LANES), F32),
    jax.ShapeDtypeStruct((N_TILES, 1, LANES), F32),
]


def _tail_out_specs():
    return [
        pl.BlockSpec((TM, D), lambda i: (i, 0)),
        pl.BlockSpec((TM, D), lambda i: (i, 0)),
        pl.BlockSpec((TM, LANES), lambda i: (i, 0)),
        pl.BlockSpec((1, 1, LANES), lambda i: (i, 0, 0)),
    ]


def _tail_in_specs(li):
    return [
        _mod_spec(li, 2), _mod_spec(li, 4), _mod_spec(li, 3),
        _row_spec((1, D)), _row_spec((1, D)),
        _row_spec((D, LANES)), _row_spec((1, LANES)),
    ]


def _out_proj_kernel(rp_ref, rs_ref, op_ref, os_ref, w_ref, xp_ref, xs_ref, *rest):
    tail_args, w_bf = rest[:-3] + rest[-2:], rest[-3]
    _cast_weight_once(w_ref, w_bf)
    half = HEADS * HEAD_W
    r = _pick_tile(rp_ref, rs_ref).astype(BF16)
    o = _pick_tile(op_ref, os_ref).astype(BF16)
    out = _dot(r, w_bf[0:half, :]) + _dot(o, w_bf[half:, :])
    _mixer_tail(out, _pick_tile(xp_ref, xs_ref), *tail_args)


def _out_proj_tail(r_p, r_s, o_p, o_s, w_out, x_prompt, x_sample, modr, ln_g, ln_b, wr, br, li):
    half = HEADS * HEAD_W
    return pl.pallas_call(
        _out_proj_kernel,
        grid=(N_TILES,),
        in_specs=[_prompt_tile_spec(half), _sample_tile_spec(half),
                  _prompt_tile_spec(half), _sample_tile_spec(half),
                  _resident_f32_weight((2 * half, D)),
                  _prompt_tile_spec(D), _sample_tile_spec(D)] + _tail_in_specs(li),
        out_specs=_tail_out_specs(),
        out_shape=_TAIL_OUT_SHAPES,
        scratch_shapes=[pltpu.VMEM((2 * half, D), BF16)] + _tail_scratch(),
        compiler_params=_cparams(("arbitrary",)),
        name="out_proj_tail",
    )(r_p, r_s, o_p, o_s, w_out, x_prompt, x_sample, modr, modr, modr, ln_g, ln_b, wr, br)


def _conv_glu(y, sh_ref, sc_ref, w_bf, b_ref):
    u = y * (1.0 + sc_ref[0]) + sh_ref[0]
    h = _dot(u.astype(BF16), w_bf[...]) + b_ref[...]
    return h[:, :D] * jax.nn.sigmoid(h[:, D:])


HALO = 16
CONV_ROWS = 64
CONV_COLS = 128


def _depthwise_conv(hp, dw_ref, conv):
    base = HALO - CONV_PAD
    for cb in range(D // CONV_COLS):
        cs = slice(cb * CONV_COLS, (cb + 1) * CONV_COLS)
        for rb in range(TM // CONV_ROWS):
            r0 = rb * CONV_ROWS
            acc = None
            for shift in range(SUBLANES):
                part = None
                for tap in range(CONV_K):
                    off = base + tap
                    if off % SUBLANES != shift:
                        continue
                    a0 = r0 + off - shift
                    term = hp[a0:a0 + CONV_ROWS + SUBLANES, cs] * dw_ref[tap:tap + 1, cs]
                    part = term if part is None else part + term
                part = part[shift:shift + CONV_ROWS, :]
                acc = part if acc is None else acc + part
            conv[r0:r0 + CONV_ROWS, cs] = acc


def _conv_tail_kernel(cur_ref, prev_ref, next_ref, dw_ref, dwb_ref, cg_ref, cb_ref, w2_ref, b2_ref,
                      x_ref, *rest):
    tail_args, (hp, conv, w2_bf) = rest[:-5] + rest[-2:], rest[-5:-2]
    _cast_weight_once(w2_ref, w2_bf)
    i = pl.program_id(0)
    k = (i - PROMPT_TILES) % SAMPLE_TILES_PER_SEQ
    in_sample = i >= PROMPT_TILES
    left_ok = jnp.logical_and(in_sample, k != 0)
    right_ok = jnp.logical_and(in_sample, k != SAMPLE_TILES_PER_SEQ - 1)
    hp[0:HALO, :] = jnp.where(left_ok, prev_ref[...], 0.0)
    hp[HALO:HALO + TM, :] = cur_ref[...]
    hp[HALO + TM:HALO + TM + HALO, :] = jnp.where(right_ok, next_ref[...], 0.0)
    _depthwise_conv(hp, dw_ref, conv)
    hc = _silu(_layer_norm(conv[...] + dwb_ref[...], cg_ref[...], cb_ref[...]))
    out = _dot(hc.astype(BF16), w2_bf[...]) + b2_ref[...]
    _mixer_tail(out, x_ref[...], *tail_args)


def _conv_tail(glu, dw, dwb, cg, cb, w2, b2, x, modr, ln_g, ln_b, wr, br, li):
    per = TM // HALO
    last = T // HALO - 1
    return pl.pallas_call(
        _conv_tail_kernel,
        grid=(N_TILES,),
        in_specs=[pl.BlockSpec((TM, D), lambda i: (i, 0)),
                  pl.BlockSpec((HALO, D), lambda i: (jnp.maximum(i * per - 1, 0), 0)),
                  pl.BlockSpec((HALO, D), lambda i: (jnp.minimum((i + 1) * per, last), 0)),
                  _row_spec((CONV_K, D)), _row_spec((1, D)), _row_spec((1, D)), _row_spec((1, D)),
                  _resident_f32_weight((D, D)), _row_spec((1, D)),
                  pl.BlockSpec((TM, D), lambda i: (i, 0))] + _tail_in_specs(li),
        out_specs=_tail_out_specs(),
        out_shape=_TAIL_OUT_SHAPES,
        scratch_shapes=[pltpu.VMEM((TM + 2 * HALO, D), F32), pltpu.VMEM((TM, D), F32),
                        pltpu.VMEM((D, D), BF16)] + _tail_scratch(),
        compiler_params=_cparams(("arbitrary",)),
        name="conv_tail",
    )(glu, glu, glu, dw, dwb, cg, cb, w2, b2, x, modr, modr, modr, ln_g, ln_b, wr, br)


def _sorted_positions(meta, srcv):
    lane = lax.broadcasted_iota(jnp.int32, meta.shape, 1).astype(F32)

    def pos(e_col, r_col):
        start = jnp.sum(jnp.where(lane == meta[:, e_col:e_col + 1], srcv, 0.0), axis=-1, keepdims=True)
        return start + meta[:, r_col:r_col + 1]

    return pos(META_E1, META_RANK1), pos(META_E2, META_RANK2)


def _one_hot_rows(pos):
    col = lax.broadcasted_iota(jnp.int32, (TM, SORT_ROWS), 1).astype(F32)
    return col == pos


def _for_each_row_group(tile, tot_ref, dstg_ref, fn):
    def body(k, carry):
        for j in range(GROUP_UNROLL):
            g = k * GROUP_UNROLL + j
            fn(pl.multiple_of(g * RUN_ALIGN, RUN_ALIGN),
               pl.multiple_of(dstg_ref[tile * SORT_GROUPS + g], RUN_ALIGN))
        return carry

    lax.fori_loop(0, tot_ref[tile] // (RUN_ALIGN * GROUP_UNROLL), body, 0)


def _wait_rows(total, make_copy):
    for bit in RUN_BITS:
        @pl.when((total & bit) != 0)
        def _(bit=bit):
            make_copy(bit).wait()


def _dispatch_kernel(tot_ref, dstg_ref, u_ref, meta_ref, srcv_ref, xs_ref, sorted_ref, sems):
    i = pl.program_id(0)
    slot = i % 2

    def wait_tile(tile, slot):
        buf = sorted_ref.at[slot]
        _wait_rows(tot_ref[tile], lambda rows: pltpu.make_async_copy(
            buf.at[pl.ds(0, rows)], xs_ref.at[pl.ds(0, rows)], sems.at[slot]))

    @pl.when(i >= 2)
    def _():
        wait_tile(i - 2, slot)

    pos1, pos2 = _sorted_positions(meta_ref[...], srcv_ref[0])
    select = jnp.logical_or(_one_hot_rows(pos1), _one_hot_rows(pos2)).astype(BF16)
    sorted_ref[slot] = _dot_tn(select, u_ref[...])
    buf = sorted_ref.at[slot]

    def start(src, dst):
        pltpu.make_async_copy(buf.at[pl.ds(src, RUN_ALIGN)], xs_ref.at[pl.ds(dst, RUN_ALIGN)],
                              sems.at[slot]).start()

    _for_each_row_group(i, tot_ref, dstg_ref, start)

    @pl.when(i == N_TILES - 1)
    def _():
        wait_tile(i - 1, 1 - slot)
        wait_tile(i, slot)


def _dispatch(sched, u2, meta, srcv):
    return pl.pallas_call(
        _dispatch_kernel,
        grid_spec=pltpu.PrefetchScalarGridSpec(
            num_scalar_prefetch=2,
            grid=(N_TILES,),
            in_specs=[pl.BlockSpec((TM, D), lambda i, *_: (i, 0)),
                      pl.BlockSpec((TM, LANES), lambda i, *_: (i, 0)),
                      pl.BlockSpec((1, 1, LANES), lambda i, *_: (i, 0, 0))],
            out_specs=pl.BlockSpec(memory_space=pl.ANY),
            scratch_shapes=[pltpu.VMEM((2, SORT_ROWS, D), F32), pltpu.SemaphoreType.DMA((2,))],
        ),
        out_shape=jax.ShapeDtypeStruct((DISPATCH_ROWS, D), F32),
        compiler_params=_cparams(("arbitrary",)),
        name="moe_dispatch",
    )(*sched, u2, meta, srcv)


def _experts_kernel(start_ref, chunks_ref, xs_ref, wg_ref, wu_ref, wd_ref, ys_ref,
                    wg_bf, wu_bf, wd_bf, xbuf, ybuf, in_sems, out_sems):
    e = pl.program_id(0)
    n = chunks_ref[e]
    first = start_ref[e] // MOE_TM
    total = start_ref[N_EXPERTS - 1] // MOE_TM + chunks_ref[N_EXPERTS - 1]

    def rows(g):
        return pl.ds(pl.multiple_of(g * MOE_TM, MOE_TM), MOE_TM)

    def load(g):
        slot = g % MOE_IN_SLOTS
        return pltpu.make_async_copy(xs_ref.at[rows(g)], xbuf.at[slot], in_sems.at[slot])

    def store(g):
        slot = g % MOE_OUT_SLOTS
        return pltpu.make_async_copy(ybuf.at[slot], ys_ref.at[rows(g)], out_sems.at[slot])

    @pl.when(e == 0)
    def _():
        for g in range(MOE_AHEAD):
            @pl.when(g < total)
            def _(g=g):
                load(g).start()

    @pl.when(n > 0)
    def _():
        wg_bf[...] = wg_ref[0, 0].astype(BF16)
        wu_bf[...] = wu_ref[0, 0].astype(BF16)
        wd_bf[...] = wd_ref[0, 0].astype(BF16)

        def tile(g, carry):
            load(g).wait()

            @pl.when(g + MOE_AHEAD < total)
            def _():
                load(g + MOE_AHEAD).start()

            @pl.when(g >= MOE_OUT_SLOTS)
            def _():
                store(g - MOE_OUT_SLOTS).wait()

            x = xbuf[g % MOE_IN_SLOTS].astype(BF16)
            h = (_silu(_dot(x, wg_bf[...])) * _dot(x, wu_bf[...])).astype(BF16)
            ybuf[g % MOE_OUT_SLOTS] = _dot(h, wd_bf[...])
            store(g).start()
            return carry

        lax.fori_loop(first, first + n, tile, 0)

    @pl.when(e == N_EXPERTS - 1)
    def _():
        for back in range(MOE_OUT_SLOTS, 0, -1):
            @pl.when(total >= back)
            def _(back=back):
                store(total - back).wait()


def _experts(seg_start, seg_chunks, xs, w_gate, w_up, w_down, li):
    def weight(shape):
        return pl.BlockSpec((1, 1) + shape, lambda e, *_: (li, e, 0, 0))

    return pl.pallas_call(
        _experts_kernel,
        grid_spec=pltpu.PrefetchScalarGridSpec(
            num_scalar_prefetch=2,
            grid=(N_EXPERTS,),
            in_specs=[pl.BlockSpec(memory_space=pl.ANY),
                      weight((D, D_EXPERT)), weight((D, D_EXPERT)), weight((D_EXPERT, D))],
            out_specs=pl.BlockSpec(memory_space=pl.ANY),
            scratch_shapes=[pltpu.VMEM((D, D_EXPERT), BF16), pltpu.VMEM((D, D_EXPERT), BF16),
                            pltpu.VMEM((D_EXPERT, D), BF16),
                            pltpu.VMEM((MOE_IN_SLOTS, MOE_TM, D), F32),
                            pltpu.VMEM((MOE_OUT_SLOTS, MOE_TM, D), F32),
                            pltpu.SemaphoreType.DMA((MOE_IN_SLOTS,)),
                            pltpu.SemaphoreType.DMA((MOE_OUT_SLOTS,))],
        ),
        out_shape=jax.ShapeDtypeStruct((MOE_ROWS, D), F32),
        compiler_params=_cparams(("arbitrary",)),
        name="moe_experts",
    )(seg_start, seg_chunks, xs, w_gate, w_up, w_down)


def _combine_kernel(tot_ref, dstg_ref, ys_ref, x1_ref, meta_ref, srcv_ref, g2_ref,
                    lng_ref, lnb_ref, *rest, split, feeds_conv):
    if feeds_conv:
        (sh_ref, sc_ref, w1_ref, b1_ref), rest, w1_bf = rest[:4], rest[4:-1], rest[-1]
        _cast_weight_once(w1_ref, w1_bf)
    outs, (sorted_ref, sems) = rest[:-2], rest[-2:]
    i = pl.program_id(0)
    slot = i % 2

    def fetch(tile, slot):
        buf = sorted_ref.at[slot]

        def start(src, dst):
            pltpu.make_async_copy(ys_ref.at[pl.ds(dst, RUN_ALIGN)], buf.at[pl.ds(src, RUN_ALIGN)],
                                  sems.at[slot]).start()

        _for_each_row_group(tile, tot_ref, dstg_ref, start)

    @pl.when(i == 0)
    def _():
        sorted_ref[...] = jnp.zeros_like(sorted_ref)
        fetch(0, 0)

    @pl.when(i + 1 < N_TILES)
    def _():
        fetch(i + 1, 1 - slot)

    meta = meta_ref[...]
    pos1, pos2 = _sorted_positions(meta, srcv_ref[0])
    sel1 = _one_hot_rows(pos1).astype(BF16)
    sel2 = _one_hot_rows(pos2).astype(BF16)
    buf = sorted_ref.at[slot]
    _wait_rows(tot_ref[i], lambda rows: pltpu.make_async_copy(
        ys_ref.at[pl.ds(0, rows)], buf.at[pl.ds(0, rows)], sems.at[slot]))
    ysort = sorted_ref[slot].astype(BF16)
    f = (meta[:, META_W1:META_W1 + 1] * _dot(sel1, ysort)
         + meta[:, META_W2:META_W2 + 1] * _dot(sel2, ysort))
    y = _layer_norm(ALPHA * x1_ref[...] + g2_ref[0] * f, lng_ref[...], lnb_ref[...])
    if split:
        @pl.when(i < PROMPT_TILES)
        def _():
            outs[0][...] = y

        @pl.when(i >= PROMPT_TILES)
        def _():
            outs[1][...] = y
    else:
        outs[0][...] = y
    if feeds_conv:
        outs[-1][...] = _conv_glu(y, sh_ref, sc_ref, w1_bf, b1_ref)


def _combine(sched, ys, x1, meta, srcv, modr, ln_g, ln_b, li, split, conv_w1=None, conv_b1=None):
    feeds_conv = conv_w1 is not None
    tile = pl.BlockSpec((TM, D), lambda i, *_: (i, 0))
    if split:
        out_specs = [_prompt_tile_spec(D), _sample_tile_spec(D)]
        out_shape = [jax.ShapeDtypeStruct((T_PROMPT, D), F32), jax.ShapeDtypeStruct((T_SAMPLE, D), F32)]
    else:
        out_specs = [tile]
        out_shape = [jax.ShapeDtypeStruct((T, D), F32)]
    extra_specs, extra_args, extra_scratch = [], [], []
    if feeds_conv:
        extra_specs = [_mod_spec(li + 1, 0), _mod_spec(li + 1, 1),
                       _resident_f32_weight((D, 2 * D)), _row_spec((1, 2 * D))]
        extra_args = [modr, modr, conv_w1, conv_b1]
        extra_scratch = [pltpu.VMEM((D, 2 * D), BF16)]
        out_specs = out_specs + [tile]
        out_shape = out_shape + [jax.ShapeDtypeStruct((T, D), F32)]
    return pl.pallas_call(
        functools.partial(_combine_kernel, split=split, feeds_conv=feeds_conv),
        grid_spec=pltpu.PrefetchScalarGridSpec(
            num_scalar_prefetch=2,
            grid=(N_TILES,),
            in_specs=[pl.BlockSpec(memory_space=pl.ANY),
                      pl.BlockSpec((TM, D), lambda i, *_: (i, 0)),
                      pl.BlockSpec((TM, LANES), lambda i, *_: (i, 0)),
                      pl.BlockSpec((1, 1, LANES), lambda i, *_: (i, 0, 0)),
                      _mod_spec(li, 5), _row_spec((1, D)), _row_spec((1, D))] + extra_specs,
            out_specs=out_specs,
            scratch_shapes=[pltpu.VMEM((2, SORT_ROWS, D), F32), pltpu.SemaphoreType.DMA((2,))] + extra_scratch,
        ),
        out_shape=out_shape,
        compiler_params=_cparams(("arbitrary",)),
        name="moe_combine",
    )(*sched, ys, x1, meta, srcv, modr, ln_g, ln_b, *extra_args)


def _moe_schedule(tile_counts):
    n = (tile_counts + RUN_ALIGN - 1) // RUN_ALIGN * RUN_ALIGN
    src = jnp.cumsum(n, axis=1) - n
    per_expert = jnp.sum(n, axis=0)
    seg = (per_expert + MOE_TM - 1) // MOE_TM * MOE_TM
    seg_start = jnp.cumsum(seg) - seg
    dst = seg_start[None, :] + jnp.cumsum(n, axis=0) - n
    g_row = jnp.arange(SORT_GROUPS, dtype=jnp.int32) * RUN_ALIGN
    in_run = jnp.logical_and(src[:, None, :] <= g_row[None, :, None],
                             g_row[None, :, None] < (src + n)[:, None, :])
    dst_g = g_row[None, :] + jnp.sum(jnp.where(in_run, (dst - src)[:, None, :], 0), axis=2)
    step_rows = RUN_ALIGN * GROUP_UNROLL
    rows = jnp.sum(n, axis=1)
    issued = (rows + step_rows - 1) // step_rows * step_rows
    real = g_row[None, :] < rows[:, None]
    spill = (MOE_ROWS + jnp.arange(N_TILES, dtype=jnp.int32)[:, None] * step_rows
             + g_row[None, :] % step_rows)
    scatter_runs = (issued, jnp.where(real, dst_g, spill).reshape(-1))
    gather_runs = (issued, jnp.where(real, dst_g, dst_g[:, :1]).reshape(-1))
    srcv = jnp.pad(src.astype(F32), ((0, 0), (ROUTER_LANE0, LANES - ROUTER_LANE0 - N_EXPERTS)))
    return scatter_runs, gather_runs, srcv.reshape(N_TILES, 1, LANES), (seg_start, seg // MOE_TM)


def _moe(x1, u2, meta, cnt, modr, w_gate, w_up, w_down, ln_g, ln_b, li, split, **next_conv):
    tile_counts = cnt[:, 0, ROUTER_LANE0:ROUTER_LANE0 + N_EXPERTS].astype(jnp.int32)
    scatter_runs, gather_runs, srcv, (seg_start, seg_chunks) = _moe_schedule(tile_counts)
    xs = _dispatch(scatter_runs, u2, meta, srcv)
    ys = _experts(seg_start, seg_chunks, xs, w_gate, w_up, w_down, li)
    return _combine(gather_runs, ys, x1, meta, srcv, modr, ln_g, ln_b, li, split, **next_conv)


def _router_slab(wg, bg, we, be):
    w = jnp.concatenate([wg, we.transpose(1, 0, 2).reshape(D, N_EXPERTS)], axis=1)
    b = jnp.concatenate([bg, be.reshape(N_EXPERTS)])
    pad = LANES - w.shape[1]
    return jnp.pad(w, ((0, 0), (0, pad))), jnp.pad(b, (0, pad)).reshape(1, LANES)


def kernel(x_prompt, x_sample, cache_diff_k, cache_diff_v, state_ret_fwd, state_ret_bwd, c, c_ctx, mod_w, mod_b, ln1_g, ln1_b, ln2_g, ln2_b, mix_w_in, mix_w_out, ret_decay_fwd, ret_decay_bwd, diff_lq1, diff_lk1, diff_lq2, diff_lk2, diff_subln_g, conv_w1, conv_b1, conv_dw, conv_dw_b, conv_ln_g, conv_ln_b, conv_w2, conv_b2, router_g_w, router_g_b, router_e_w, router_e_b, moe_w_gate, moe_w_up, moe_w_down):
    xp = x_prompt.reshape(T_PROMPT, D)
    xs = x_sample.reshape(T_SAMPLE, D)
    cond = jnp.concatenate([c_ctx[None, :], c, jnp.zeros((MOD_ROWS - 1 - DEC_BATCH, D), F32)], axis=0)
    modr = _mod_vectors(cond, mod_w, mod_b).reshape(DEPTH * MOD_ROWS * 6, 1, D)
    cos, sin_signed = _rope_tables()

    def row(v):
        return v.reshape(1, -1)

    x = None
    caches = None
    for li in range(DEPTH):
        wr, br = _router_slab(router_g_w[li], router_g_b[li], router_e_w[li], router_e_b[li])
        if li % 2 == 0:
            assert li == 0, "the even mixer reads the kernel inputs directly"
            e = li // 2
            lam_init = 0.8 - 0.6 * math.exp(-0.3 * li)
            proj, ck, cv = _in_proj(xp, xs, modr, mix_w_in[e], li)
            dec = jnp.concatenate([ret_decay_fwd[e], ret_decay_bwd[e]])
            r_p, sf, sb = _retention(proj, dec, BATCH, SEQ, 0, HEADS, emit_state=True)
            (r_s,) = _retention(proj, dec, DEC_BATCH, DEC_SEQ, T_PROMPT // DEC_SEQ, 2,
                                s0f=state_ret_fwd, s0b=state_ret_bwd, e=e)
            lams = (row(diff_lq1[e]), row(diff_lk1[e]), row(diff_lq2[e]), row(diff_lk2[e]),
                    row(diff_subln_g[e]))
            o_p = _attn_prompt(proj, *lams, lam_init)
            o_s = _attn_sample(proj, cache_diff_k, cache_diff_v, cos, sin_signed, *lams, lam_init, e)
            x1, u2, meta, cnt = _out_proj_tail(r_p, r_s, o_p, o_s, mix_w_out[e], xp, xs, modr,
                                               row(ln1_g[li]), row(ln1_b[li]), wr, br, li)
            caches = (ck, cv, sf, sb)
        else:
            o = li // 2
            x1, u2, meta, cnt = _conv_tail(glu, conv_dw[o], row(conv_dw_b[o]), row(conv_ln_g[o]),
                                           row(conv_ln_b[o]), conv_w2[o], row(conv_b2[o]),
                                           x, modr, row(ln1_g[li]), row(ln1_b[li]), wr, br, li)
        next_conv = {}
        if li + 1 < DEPTH and (li + 1) % 2 == 1:
            next_conv = dict(conv_w1=conv_w1[(li + 1) // 2], conv_b1=row(conv_b1[(li + 1) // 2]))
        outs = _moe(x1, u2, meta, cnt, modr, moe_w_gate, moe_w_up, moe_w_down,
                    row(ln2_g[li]), row(ln2_b[li]), li, split=(li == DEPTH - 1), **next_conv)
        x, glu = outs[0], outs[-1]

    y_prompt = outs[0].reshape(BATCH, SEQ, D)
    y_sample = outs[1].reshape(DEC_BATCH, DEC_SEQ, D)
    return (y_prompt, y_sample) + caches
```

```python
import functools
import math

import numpy as np
import jax
import jax.numpy as jnp
from jax import lax
from jax.experimental import pallas as pl
from jax.experimental.pallas import tpu as pltpu

F32 = jnp.float32
BF16 = jnp.bfloat16

D = 1024
BATCH = 16
SEQ = 256
DEPTH = 2
DEC_BATCH = 2
DEC_SEQ = 2048
PAST_LEN = 512
GRID_W = 64
HEADS = 4
HEAD_W = 128
RET_CHUNK = 128
DIFF_DK = 64
ROPE_THETA = 10000.0
IN_W = 7 * HEADS * HEAD_W
CONV_K = 31
CONV_PAD = CONV_K // 2
N_GROUPS = 4
EXPERTS_PER_GROUP = 8
N_EXPERTS = N_GROUPS * EXPERTS_PER_GROUP
D_EXPERT = 512
ALPHA = (2.0 * DEPTH) ** 0.25
LN_EPS = 1e-5
GN_EPS = 1e-6

T_PROMPT = BATCH * SEQ
T_SAMPLE = DEC_BATCH * DEC_SEQ
T = T_PROMPT + T_SAMPLE
TM = 256
N_TILES = T // TM
PROMPT_TILES = T_PROMPT // TM
SAMPLE_TILES_PER_SEQ = DEC_SEQ // TM
MOD_ROWS = 8
MOE_TM = 256
LANES = 128
SUBLANES = 8
RUN_ALIGN = SUBLANES
SORT_ROWS = -(-(2 * TM + N_EXPERTS * (RUN_ALIGN - 1)) // TM) * TM
RUN_BITS = tuple(1 << b for b in range((2 * TM).bit_length() - 1, RUN_ALIGN.bit_length() - 2, -1))
MOE_MAX_TILES = -(-(2 * T + N_TILES * N_EXPERTS * (RUN_ALIGN - 1) + N_EXPERTS * (MOE_TM - RUN_ALIGN)) // MOE_TM)
MOE_ROWS = MOE_MAX_TILES * MOE_TM
SORT_GROUPS = SORT_ROWS // RUN_ALIGN
GROUP_UNROLL = 8
DISPATCH_ROWS = MOE_ROWS + N_TILES * RUN_ALIGN * GROUP_UNROLL
MOE_AHEAD = 8
MOE_IN_SLOTS = MOE_AHEAD + 1
MOE_OUT_SLOTS = 4
ROUTER_LANE0 = N_GROUPS
ROPE_GROUP = DIFF_DK // 2
ROPE_HALF = ROPE_GROUP // 2
V7X_VMEM_BYTES = 64 * 1024 * 1024
VMEM_LIMIT = V7X_VMEM_BYTES - 12 * 1024 * 1024


def _cparams(sem):
    return pltpu.CompilerParams(dimension_semantics=sem, vmem_limit_bytes=VMEM_LIMIT)


def _tile_cond_row(i, tm):
    return jnp.where(i < T_PROMPT // tm, 0, 1 + (i - T_PROMPT // tm) // (DEC_SEQ // tm))


def _mod_spec(li, k, tm=TM):
    return pl.BlockSpec((1, 1, D), lambda i, *_: ((li * MOD_ROWS + _tile_cond_row(i, tm)) * 6 + k, 0, 0))


def _row_spec(shape):
    return pl.BlockSpec(shape, lambda i, *_: (0,) * len(shape))


def _resident_f32_weight(shape):
    return pl.BlockSpec(shape, lambda i, *_: (0,) * len(shape), pipeline_mode=pl.Buffered(1))


def _cast_weight_once(w_ref, w_bf):
    @pl.when(pl.program_id(0) == 0)
    def _():
        w_bf[...] = w_ref[...].astype(BF16)


def _layer_norm(x, g, b):
    mu = jnp.mean(x, axis=-1, keepdims=True)
    xc = x - mu
    var = jnp.mean(xc * xc, axis=-1, keepdims=True)
    return xc * lax.rsqrt(var + LN_EPS) * g + b


def _silu(x):
    return x * jax.nn.sigmoid(x)


def _dot(a, b):
    return jnp.dot(a, b, preferred_element_type=F32)


def _dot_nt(a, b):
    return lax.dot_general(a, b, (((1,), (1,)), ((), ())), preferred_element_type=F32)


def _dot_tn(a, b):
    return lax.dot_general(a, b, (((0,), (0,)), ((), ())), preferred_element_type=F32)


MOD_TN = 2048
MOD_USED_ROWS = 1 + DEC_BATCH


def _mod_kernel(cond_t_ref, w_ref, b_ref, o_ref):
    s = _silu(cond_t_ref[...])
    w = w_ref[0]
    o_ref[0] = jnp.zeros((MOD_ROWS, MOD_TN), F32) + b_ref[0]
    for r in range(MOD_USED_ROWS):
        o_ref[0, r:r + 1, :] = jnp.sum(w * s[:, r:r + 1], axis=0, keepdims=True) + b_ref[0]


def _mod_vectors(cond, mod_w, mod_b):
    return pl.pallas_call(
        _mod_kernel,
        grid=(DEPTH, 6 * D // MOD_TN),
        in_specs=[
            pl.BlockSpec((D, MOD_ROWS), lambda l, j: (0, 0)),
            pl.BlockSpec((1, D, MOD_TN), lambda l, j: (l, 0, j)),
            pl.BlockSpec((1, 1, MOD_TN), lambda l, j: (l, 0, j)),
        ],
        out_specs=pl.BlockSpec((1, MOD_ROWS, MOD_TN), lambda l, j: (l, 0, j)),
        out_shape=jax.ShapeDtypeStruct((DEPTH, MOD_ROWS, 6 * D), F32),
        compiler_params=_cparams(("arbitrary", "arbitrary")),
        name="mod_vectors",
    )(cond.T, mod_w, mod_b.reshape(DEPTH, 1, 6 * D))


def _prompt_tile_spec(width, tm=TM):
    return pl.BlockSpec((tm, width), lambda i, *_: (jnp.minimum(i, T_PROMPT // tm - 1), 0))


def _sample_tile_spec(width, tm=TM):
    return pl.BlockSpec((tm, width), lambda i, *_: (jnp.maximum(i - T_PROMPT // tm, 0), 0))


def _pick_tile(prompt_ref, sample_ref, tm=TM):
    return jnp.where(pl.program_id(0) < T_PROMPT // tm, prompt_ref[...], sample_ref[...])


IN_TM = 512
IN_SEQS = IN_TM // SEQ


def _in_proj_kernel(xp_ref, xs_ref, sh_ref, sc_ref, w_ref, o_ref, ck_ref, cv_ref, w_bf):
    _cast_weight_once(w_ref, w_bf)
    u = _pick_tile(xp_ref, xs_ref, IN_TM) * (1.0 + sc_ref[0]) + sh_ref[0]
    proj = _dot(u.astype(BF16), w_bf[...])
    o_ref[...] = proj.astype(BF16)

    @pl.when(pl.program_id(0) < T_PROMPT // IN_TM)
    def _():
        for s in range(IN_SEQS):
            rows = slice(s * SEQ, (s + 1) * SEQ)
            for h in range(HEADS):
                ck_ref[s, 0, h] = proj[rows, (COL_KD + h) * HEAD_W:(COL_KD + h + 1) * HEAD_W]
                cv_ref[s, 0, h] = proj[rows, (COL_VD + h) * HEAD_W:(COL_VD + h + 1) * HEAD_W]


def _in_proj(x_prompt, x_sample, modr, w_in, li):
    cache_spec = pl.BlockSpec((IN_SEQS, 1, HEADS, SEQ, HEAD_W),
                              lambda i: (jnp.minimum(i, T_PROMPT // IN_TM - 1), 0, 0, 0, 0))
    cache_shape = jax.ShapeDtypeStruct((BATCH, 1, HEADS, SEQ, HEAD_W), F32)
    return pl.pallas_call(
        _in_proj_kernel,
        grid=(T // IN_TM,),
        in_specs=[
            _prompt_tile_spec(D, IN_TM), _sample_tile_spec(D, IN_TM),
            _mod_spec(li, 0, IN_TM),
            _mod_spec(li, 1, IN_TM),
            _resident_f32_weight((D, IN_W)),
        ],
        out_specs=[pl.BlockSpec((IN_TM, IN_W), lambda i: (i, 0)), cache_spec, cache_spec],
        out_shape=[jax.ShapeDtypeStruct((T, IN_W), BF16), cache_shape, cache_shape],
        scratch_shapes=[pltpu.VMEM((D, IN_W), BF16)],
        compiler_params=_cparams(("arbitrary",)),
        name="in_proj",
    )(x_prompt, x_sample, modr, modr, w_in)


COL_QR, COL_KR, COL_VR, COL_GR, COL_QD, COL_KD, COL_VD = (k * HEADS for k in range(7))


def _retention_kernel(dec_ref, q_ref, k_ref, v_ref, g_ref, *rest, chunk_len, n_chunks, n_heads, has_state,
                      emit_state):
    rest = list(rest)
    if has_state:
        s0f_ref, s0b_ref = rest[:2]
        rest = rest[2:]
    r_ref = rest[0]
    rest = rest[1:]
    if emit_state:
        sf_ref, sb_ref = rest[:2]
        rest = rest[2:]
    of_ref = rest[0]

    head0 = pl.program_id(1) * n_heads
    C = chunk_len
    ii = lax.broadcasted_iota(jnp.int32, (C, C), 0)
    jj = lax.broadcasted_iota(jnp.int32, (C, C), 1)
    rel = (ii - jj).astype(F32)
    idx = lax.broadcasted_iota(jnp.int32, (C, 1), 0).astype(F32)
    k_scale = HEAD_W ** -0.5

    def chunk(ref, c, h):
        return ref[c * C:(c + 1) * C, h * HEAD_W:(h + 1) * HEAD_W].astype(F32)

    def decays(direction, h):
        lg = -jnp.exp(jnp.full((1, 1), dec_ref[direction * HEADS + head0 + h], F32))
        if direction == 0:
            inner = jnp.where(rel >= 0, jnp.exp(jnp.maximum(rel, 0.0) * lg), 0.0)
            return inner, jnp.exp((idx + 1.0) * lg), jnp.exp((C - 1.0 - idx) * lg), jnp.exp(C * lg)
        inner = jnp.where(rel <= 0, jnp.exp(jnp.maximum(-rel, 0.0) * lg), 0.0)
        return inner, jnp.exp((C - idx) * lg), jnp.exp(idx * lg), jnp.exp(C * lg)

    def run(direction):
        dec = [decays(direction, h) for h in range(n_heads)]
        if has_state:
            s0_ref = s0f_ref if direction == 0 else s0b_ref
            states = [s0_ref[0, 0, h] for h in range(n_heads)]
        else:
            states = [None] * n_heads
        order = range(n_chunks) if direction == 0 else range(n_chunks - 1, -1, -1)
        for c in order:
            rows = slice(c * C, (c + 1) * C)
            for h in range(n_heads):
                inner, q_decay, k_decay, chunk_decay = dec[h]
                cols = slice(h * HEAD_W, (h + 1) * HEAD_W)
                s = states[h]
                qc = chunk(q_ref, c, h)
                kc = chunk(k_ref, c, h) * k_scale
                vc = chunk(v_ref, c, h).astype(BF16)
                scores = _dot_nt(qc.astype(BF16), kc.astype(BF16)) * inner
                o = _dot(scores.astype(BF16), vc)
                kv = _dot_tn((kc * k_decay).astype(BF16), vc)
                if s is not None:
                    o = o + _dot((qc * q_decay).astype(BF16), s.astype(BF16))
                    kv = s * chunk_decay + kv
                states[h] = kv
                if direction == 0:
                    of_ref[rows, cols] = o
                else:
                    r = of_ref[rows, cols] + o
                    mu = jnp.mean(r, axis=-1, keepdims=True)
                    rc = r - mu
                    var = jnp.mean(rc * rc, axis=-1, keepdims=True)
                    rn = rc * lax.rsqrt(var + GN_EPS)
                    r_ref[rows, cols] = _silu(chunk(g_ref, c, h)) * rn
        return states

    sf = run(0)
    sb = run(1)
    if emit_state:
        for h in range(n_heads):
            sf_ref[0, 0, h] = sf[h]
            sb_ref[0, 0, h] = sb[h]


def _retention(proj, dec, n_seq, seq_len, row_block0, n_heads, s0f=None, s0b=None, e=0, emit_state=False,
               chunk_len=RET_CHUNK):
    has_state = s0f is not None
    width = n_heads * HEAD_W

    def col(base):
        return pl.BlockSpec((seq_len, width), lambda b, h, *_: (row_block0 + b, base // n_heads + h))

    state_spec = pl.BlockSpec((1, 1, n_heads, HEAD_W, HEAD_W), lambda b, h, *_: (b, e, h, 0, 0))
    in_specs = [pl.BlockSpec(memory_space=pltpu.SMEM), col(COL_QR), col(COL_KR), col(COL_VR), col(COL_GR)]
    args = [dec, proj, proj, proj, proj]
    if has_state:
        in_specs += [state_spec, state_spec]
        args += [s0f, s0b]
    out_specs = [pl.BlockSpec((seq_len, width), lambda b, h, *_: (b, h))]
    out_shape = [jax.ShapeDtypeStruct((n_seq * seq_len, HEADS * HEAD_W), F32)]
    if emit_state:
        st = pl.BlockSpec((1, 1, n_heads, HEAD_W, HEAD_W), lambda b, h, *_: (b, 0, h, 0, 0))
        out_specs += [st, st]
        out_shape += [jax.ShapeDtypeStruct((n_seq, 1, HEADS, HEAD_W, HEAD_W), F32)] * 2
    return pl.pallas_call(
        functools.partial(_retention_kernel, chunk_len=chunk_len, n_chunks=seq_len // chunk_len, n_heads=n_heads,
                          has_state=has_state, emit_state=emit_state),
        grid=(n_seq, HEADS // n_heads),
        in_specs=in_specs,
        out_specs=out_specs,
        out_shape=out_shape,
        scratch_shapes=[pltpu.VMEM((seq_len, width), F32)],
        compiler_params=_cparams(("arbitrary", "arbitrary")),
        name=f"retention_{seq_len}",
    )(*args)


def _diff_lambda(lq1_ref, lk1_ref, lq2_ref, lk2_ref, lam_init):
    a = jnp.sum(lq1_ref[...] * lk1_ref[...], axis=-1, keepdims=True)
    b = jnp.sum(lq2_ref[...] * lk2_ref[...], axis=-1, keepdims=True)
    return jnp.exp(a) - jnp.exp(b) + lam_init


LOG2E = 1.4426950408889634


def _diff_attend(q, k, v, lam, subln_g, lam_init):
    lane = lax.broadcasted_iota(jnp.int32, q.shape, 1)
    q1 = jnp.where(lane < DIFF_DK, q, 0.0).astype(BF16)
    q2 = jnp.where(lane >= DIFF_DK, q, 0.0).astype(BF16)

    def softmax_times_v(qz):
        s = _dot_nt(qz, k)
        p = jnp.exp2(s - jnp.max(s, axis=-1, keepdims=True))
        return _dot(p.astype(BF16), v) * (1.0 / jnp.sum(p, axis=-1, keepdims=True))

    o = softmax_times_v(q1) - lam * softmax_times_v(q2)
    o = o * lax.rsqrt(jnp.mean(o * o, axis=-1, keepdims=True) + LN_EPS)
    return o * subln_g * (1.0 - lam_init)


def _attn_prompt_kernel(q_ref, k_ref, v_ref, lq1, lk1, lq2, lk2, g_ref, o_ref, *, lam_init):
    lam = _diff_lambda(lq1, lk1, lq2, lk2, lam_init)
    scale = DIFF_DK ** -0.5 * LOG2E
    for h in range(HEADS):
        sl = slice(h * HEAD_W, (h + 1) * HEAD_W)
        o_ref[:, sl] = _diff_attend(q_ref[:, sl].astype(F32) * scale, k_ref[:, sl], v_ref[:, sl],
                                    lam, g_ref[...], lam_init)


def _attn_prompt(proj, lq1, lk1, lq2, lk2, subln_g, lam_init):
    W = HEADS * HEAD_W

    def slab(base):
        return pl.BlockSpec((SEQ, W), lambda b: (b, base // HEADS))

    small = _row_spec((1, DIFF_DK))
    return pl.pallas_call(
        functools.partial(_attn_prompt_kernel, lam_init=lam_init),
        grid=(BATCH,),
        in_specs=[slab(COL_QD), slab(COL_KD), slab(COL_VD), small, small, small, small,
                  _row_spec((1, HEAD_W))],
        out_specs=pl.BlockSpec((SEQ, W), lambda b: (b, 0)),
        out_shape=jax.ShapeDtypeStruct((T_PROMPT, W), F32),
        compiler_params=_cparams(("arbitrary",)),
        name="diff_attn_prompt",
    )(proj, proj, proj, lq1, lk1, lq2, lk2, subln_g)


def _rope(x, cos, sin_signed):
    lane = lax.broadcasted_iota(jnp.int32, x.shape, 1)
    partner = jnp.where((lane % ROPE_GROUP) < ROPE_HALF,
                        pltpu.roll(x, LANES - ROPE_HALF, 1), pltpu.roll(x, ROPE_HALF, 1))
    return x * cos + partner * sin_signed


def _attn_sample_kernel(q_ref, k_ref, v_ref, ck_ref, cv_ref, cosq_ref, sinq_ref, cos_ref, sin_ref,
                        lq1, lk1, lq2, lk2, g_ref, o_ref, kbuf, vbuf, *, lam_init):
    @pl.when(pl.program_id(2) == 0)
    def _():
        kbuf[0:DEC_SEQ, :] = _rope(k_ref[...].astype(F32), cos_ref[...], sin_ref[...]).astype(BF16)
        kbuf[DEC_SEQ:, :] = ck_ref[0, 0, 0].astype(BF16)
        vbuf[0:DEC_SEQ, :] = v_ref[...]
        vbuf[DEC_SEQ:, :] = cv_ref[0, 0, 0].astype(BF16)

    lam = _diff_lambda(lq1, lk1, lq2, lk2, lam_init)
    q = _rope(q_ref[...].astype(F32), cosq_ref[...], sinq_ref[...]) * (DIFF_DK ** -0.5 * LOG2E)
    o_ref[...] = _diff_attend(q, kbuf[...], vbuf[...], lam, g_ref[...], lam_init)


ATTN_TQ = 256


def _attn_sample(proj, cache_k, cache_v, cos, sin_signed, lq1, lk1, lq2, lk2, subln_g, lam_init, e):
    nq = DEC_SEQ // ATTN_TQ
    row0_q = T_PROMPT // ATTN_TQ
    row0_kv = T_PROMPT // DEC_SEQ
    small = pl.BlockSpec((1, DIFF_DK), lambda b, h, t: (0, 0))
    cache = pl.BlockSpec((1, 1, 1, PAST_LEN, HEAD_W), lambda b, h, t: (b, e, h, 0, 0))
    table_q = pl.BlockSpec((ATTN_TQ, HEAD_W), lambda b, h, t: (t, 0))
    table = pl.BlockSpec((DEC_SEQ, HEAD_W), lambda b, h, t: (0, 0))
    return pl.pallas_call(
        functools.partial(_attn_sample_kernel, lam_init=lam_init),
        grid=(DEC_BATCH, HEADS, nq),
        in_specs=[
            pl.BlockSpec((ATTN_TQ, HEAD_W), lambda b, h, t: (row0_q + b * nq + t, COL_QD + h)),
            pl.BlockSpec((DEC_SEQ, HEAD_W), lambda b, h, t: (row0_kv + b, COL_KD + h)),
            pl.BlockSpec((DEC_SEQ, HEAD_W), lambda b, h, t: (row0_kv + b, COL_VD + h)),
            cache, cache, table_q, table_q, table, table,
            small, small, small, small,
            pl.BlockSpec((1, HEAD_W), lambda b, h, t: (0, 0)),
        ],
        out_specs=pl.BlockSpec((ATTN_TQ, HEAD_W), lambda b, h, t: (b * nq + t, h)),
        out_shape=jax.ShapeDtypeStruct((T_SAMPLE, HEADS * HEAD_W), F32),
        scratch_shapes=[pltpu.VMEM((DEC_SEQ + PAST_LEN, HEAD_W), BF16),
                        pltpu.VMEM((DEC_SEQ + PAST_LEN, HEAD_W), BF16)],
        compiler_params=_cparams(("arbitrary", "arbitrary", "arbitrary")),
        name="diff_attn_sample",
    )(proj, proj, proj, cache_k, cache_v, cos, sin_signed, cos, sin_signed,
      lq1, lk1, lq2, lk2, subln_g)


def _rope_tables():
    t = np.arange(DEC_SEQ)
    row, colp = t // GRID_W, t % GRID_W
    lane = np.arange(LANES)
    pos = np.where(((lane // ROPE_GROUP) % 2 == 0)[None, :], row[:, None], colp[:, None]).astype(np.float64)
    half = ROPE_HALF
    inv = (np.float32(ROPE_THETA) ** (-(np.arange(half, dtype=np.float32)) / np.float32(half))).astype(np.float32)
    ang = pos.astype(np.float32) * inv[lane % half][None, :]
    cos = np.cos(ang.astype(np.float64)).astype(np.float32)
    sin = np.sin(ang.astype(np.float64)).astype(np.float32)
    sign = np.where((lane % ROPE_GROUP) < half, -1.0, 1.0).astype(np.float32)[None, :]
    return jnp.asarray(cos), jnp.asarray(sin * sign)


def _split_bf16(a):
    hi = a.astype(BF16)
    return hi, (a - hi.astype(F32)).astype(BF16)


def _mixer_tail(out, x, g1_ref, sc2_ref, sh2_ref, lng_ref, lnb_ref, wr_ref, br_ref,
                x1_ref, u2_ref, meta_ref, cnt_ref, wr_bf, before_bf):
    @pl.when(pl.program_id(0) == 0)
    def _():
        w_hi, w_lo = _split_bf16(wr_ref[...])
        wr_bf[:, 0:LANES] = w_hi
        wr_bf[:, LANES:] = w_lo
        r_i = lax.broadcasted_iota(jnp.int32, (TM, TM), 0)
        c_i = lax.broadcasted_iota(jnp.int32, (TM, TM), 1)
        before_bf[...] = (c_i < r_i).astype(BF16)

    x1 = _layer_norm(ALPHA * x + g1_ref[0] * out, lng_ref[...], lnb_ref[...])
    x1_ref[...] = x1
    u2 = x1 * (1.0 + sc2_ref[0]) + sh2_ref[0]
    u2_ref[...] = u2.astype(BF16)

    u_hi, u_lo = _split_bf16(u2)
    hi = _dot(u_hi, wr_bf[...])
    logits = hi[:, 0:LANES] + (hi[:, LANES:] + _dot(u_lo, wr_bf[:, 0:LANES])) + br_ref[...]
    lane = lax.broadcasted_iota(jnp.int32, logits.shape, 1).astype(F32)
    neg = jnp.float32(-jnp.inf)
    is_g = lane < N_GROUPS
    gl = jnp.where(is_g, logits, neg)
    gmax = jnp.max(gl, axis=-1, keepdims=True)
    gsel = jnp.min(jnp.where(gl == gmax, lane, float(LANES)), axis=-1, keepdims=True)
    p_g = 1.0 / jnp.sum(jnp.where(is_g, jnp.exp(gl - gmax), 0.0), axis=-1, keepdims=True)
    lo = ROUTER_LANE0 + gsel * EXPERTS_PER_GROUP
    el = jnp.where((lane >= lo) & (lane < lo + EXPERTS_PER_GROUP), logits, neg)
    v1 = jnp.max(el, axis=-1, keepdims=True)
    i1 = jnp.min(jnp.where(el == v1, lane, float(LANES)), axis=-1, keepdims=True)
    el2 = jnp.where(lane == i1, neg, el)
    v2 = jnp.max(el2, axis=-1, keepdims=True)
    i2 = jnp.min(jnp.where(el2 == v2, lane, float(LANES)), axis=-1, keepdims=True)
    t = jnp.exp(v2 - v1)
    w1 = p_g / (1.0 + t)
    w2 = w1 * t

    oh1 = (lane == i1).astype(F32)
    oh2 = (lane == i2).astype(F32)
    oh = oh1 + oh2
    earlier = _dot(before_bf[...], oh.astype(BF16))
    rank1 = jnp.sum(earlier * oh1, axis=-1, keepdims=True)
    rank2 = jnp.sum(earlier * oh2, axis=-1, keepdims=True)
    cnt_ref[0] = jnp.sum(oh, axis=0, keepdims=True)
    cols = (i1, i2, w1, w2, rank1, rank2)
    meta = jnp.zeros_like(logits)
    for k, col in enumerate(cols):
        meta = jnp.where(lane == k, col, meta)
    meta_ref[...] = meta


META_E1, META_E2, META_W1, META_W2, META_RANK1, META_RANK2 = range(6)

def _tail_scratch():
    return [pltpu.VMEM((D, 2 * LANES), BF16), pltpu.VMEM((TM, TM), BF16)]


_TAIL_OUT_SHAPES = [
    jax.ShapeDtypeStruct((T, D), F32),
    jax.ShapeDtypeStruct((T, D), BF16),
    jax.ShapeDtypeStruct((T, LANES), F32),
    jax.ShapeDtypeStruct((N_TILES, 1, LANES), F32),
]


def _tail_out_specs():
    return [
        pl.BlockSpec((TM, D), lambda i: (i, 0)),
        pl.BlockSpec((TM, D), lambda i: (i, 0)),
        pl.BlockSpec((TM, LANES), lambda i: (i, 0)),
        pl.BlockSpec((1, 1, LANES), lambda i: (i, 0, 0)),
    ]


def _tail_in_specs(li):
    return [
        _mod_spec(li, 2), _mod_spec(li, 4), _mod_spec(li, 3),
        _row_spec((1, D)), _row_spec((1, D)),
        _row_spec((D, LANES)), _row_spec((1, LANES)),
    ]


def _out_proj_kernel(rp_ref, rs_ref, op_ref, os_ref, w_ref, xp_ref, xs_ref, *rest):
    tail_args, w_bf = rest[:-3] + rest[-2:], rest[-3]
    _cast_weight_once(w_ref, w_bf)
    half = HEADS * HEAD_W
    r = _pick_tile(rp_ref, rs_ref).astype(BF16)
    o = _pick_tile(op_ref, os_ref).astype(BF16)
    out = _dot(r, w_bf[0:half, :]) + _dot(o, w_bf[half:, :])
    _mixer_tail(out, _pick_tile(xp_ref, xs_ref), *tail_args)


def _out_proj_tail(r_p, r_s, o_p, o_s, w_out, x_prompt, x_sample, modr, ln_g, ln_b, wr, br, li):
    half = HEADS * HEAD_W
    return pl.pallas_call(
        _out_proj_kernel,
        grid=(N_TILES,),
        in_specs=[_prompt_tile_spec(half), _sample_tile_spec(half),
                  _prompt_tile_spec(half), _sample_tile_spec(half),
                  _resident_f32_weight((2 * half, D)),
                  _prompt_tile_spec(D), _sample_tile_spec(D)] + _tail_in_specs(li),
        out_specs=_tail_out_specs(),
        out_shape=_TAIL_OUT_SHAPES,
        scratch_shapes=[pltpu.VMEM((2 * half, D), BF16)] + _tail_scratch(),
        compiler_params=_cparams(("arbitrary",)),
        name="out_proj_tail",
    )(r_p, r_s, o_p, o_s, w_out, x_prompt, x_sample, modr, modr, modr, ln_g, ln_b, wr, br)


def _conv_glu(y, sh_ref, sc_ref, w_bf, b_ref):
    u = y * (1.0 + sc_ref[0]) + sh_ref[0]
    h = _dot(u.astype(BF16), w_bf[...]) + b_ref[...]
    return h[:, :D] * jax.nn.sigmoid(h[:, D:])


HALO = 16
CONV_ROWS = 64
CONV_COLS = 128


def _depthwise_conv(hp, dw_ref, conv):
    base = HALO - CONV_PAD
    for cb in range(D // CONV_COLS):
        cs = slice(cb * CONV_COLS, (cb + 1) * CONV_COLS)
        for rb in range(TM // CONV_ROWS):
            r0 = rb * CONV_ROWS
            acc = None
            for shift in range(SUBLANES):
                part = None
                for tap in range(CONV_K):
                    off = base + tap
                    if off % SUBLANES != shift:
                        continue
                    a0 = r0 + off - shift
                    term = hp[a0:a0 + CONV_ROWS + SUBLANES, cs] * dw_ref[tap:tap + 1, cs]
                    part = term if part is None else part + term
                part = part[shift:shift + CONV_ROWS, :]
                acc = part if acc is None else acc + part
            conv[r0:r0 + CONV_ROWS, cs] = acc


def _conv_tail_kernel(cur_ref, prev_ref, next_ref, dw_ref, dwb_ref, cg_ref, cb_ref, w2_ref, b2_ref,
                      x_ref, *rest):
    tail_args, (hp, conv, w2_bf) = rest[:-5] + rest[-2:], rest[-5:-2]
    _cast_weight_once(w2_ref, w2_bf)
    i = pl.program_id(0)
    k = (i - PROMPT_TILES) % SAMPLE_TILES_PER_SEQ
    in_sample = i >= PROMPT_TILES
    left_ok = jnp.logical_and(in_sample, k != 0)
    right_ok = jnp.logical_and(in_sample, k != SAMPLE_TILES_PER_SEQ - 1)
    hp[0:HALO, :] = jnp.where(left_ok, prev_ref[...], 0.0)
    hp[HALO:HALO + TM, :] = cur_ref[...]
    hp[HALO + TM:HALO + TM + HALO, :] = jnp.where(right_ok, next_ref[...], 0.0)
    _depthwise_conv(hp, dw_ref, conv)
    hc = _silu(_layer_norm(conv[...] + dwb_ref[...], cg_ref[...], cb_ref[...]))
    out = _dot(hc.astype(BF16), w2_bf[...]) + b2_ref[...]
    _mixer_tail(out, x_ref[...], *tail_args)


def _conv_tail(glu, dw, dwb, cg, cb, w2, b2, x, modr, ln_g, ln_b, wr, br, li):
    per = TM // HALO
    last = T // HALO - 1
    return pl.pallas_call(
        _conv_tail_kernel,
        grid=(N_TILES,),
        in_specs=[pl.BlockSpec((TM, D), lambda i: (i, 0)),
                  pl.BlockSpec((HALO, D), lambda i: (jnp.maximum(i * per - 1, 0), 0)),
                  pl.BlockSpec((HALO, D), lambda i: (jnp.minimum((i + 1) * per, last), 0)),
                  _row_spec((CONV_K, D)), _row_spec((1, D)), _row_spec((1, D)), _row_spec((1, D)),
                  _resident_f32_weight((D, D)), _row_spec((1, D)),
                  pl.BlockSpec((TM, D), lambda i: (i, 0))] + _tail_in_specs(li),
        out_specs=_tail_out_specs(),
        out_shape=_TAIL_OUT_SHAPES,
        scratch_shapes=[pltpu.VMEM((TM + 2 * HALO, D), F32), pltpu.VMEM((TM, D), F32),
                        pltpu.VMEM((D, D), BF16)] + _tail_scratch(),
        compiler_params=_cparams(("arbitrary",)),
        name="conv_tail",
    )(glu, glu, glu, dw, dwb, cg, cb, w2, b2, x, modr, modr, modr, ln_g, ln_b, wr, br)


def _sorted_positions(meta, srcv):
    lane = lax.broadcasted_iota(jnp.int32, meta.shape, 1).astype(F32)

    def pos(e_col, r_col):
        start = jnp.sum(jnp.where(lane == meta[:, e_col:e_col + 1], srcv, 0.0), axis=-1, keepdims=True)
        return start + meta[:, r_col:r_col + 1]

    return pos(META_E1, META_RANK1), pos(META_E2, META_RANK2)


def _one_hot_rows(pos):
    col = lax.broadcasted_iota(jnp.int32, (TM, SORT_ROWS), 1).astype(F32)
    return col == pos


def _for_each_row_group(tile, tot_ref, dstg_ref, fn):
    def body(k, carry):
        for j in range(GROUP_UNROLL):
            g = k * GROUP_UNROLL + j
            fn(pl.multiple_of(g * RUN_ALIGN, RUN_ALIGN),
               pl.multiple_of(dstg_ref[tile * SORT_GROUPS + g], RUN_ALIGN))
        return carry

    lax.fori_loop(0, tot_ref[tile] // (RUN_ALIGN * GROUP_UNROLL), body, 0)


def _wait_rows(total, make_copy):
    for bit in RUN_BITS:
        @pl.when((total & bit) != 0)
        def _(bit=bit):
            make_copy(bit).wait()


def _dispatch_kernel(tot_ref, dstg_ref, u_ref, meta_ref, srcv_ref, xs_ref, sorted_ref, sems):
    i = pl.program_id(0)
    slot = i % 2

    def wait_tile(tile, slot):
        buf = sorted_ref.at[slot]
        _wait_rows(tot_ref[tile], lambda rows: pltpu.make_async_copy(
            buf.at[pl.ds(0, rows)], xs_ref.at[pl.ds(0, rows)], sems.at[slot]))

    @pl.when(i >= 2)
    def _():
        wait_tile(i - 2, slot)

    pos1, pos2 = _sorted_positions(meta_ref[...], srcv_ref[0])
    select = jnp.logical_or(_one_hot_rows(pos1), _one_hot_rows(pos2)).astype(BF16)
    sorted_ref[slot] = _dot_tn(select, u_ref[...])
    buf = sorted_ref.at[slot]

    def start(src, dst):
        pltpu.make_async_copy(buf.at[pl.ds(src, RUN_ALIGN)], xs_ref.at[pl.ds(dst, RUN_ALIGN)],
                              sems.at[slot]).start()

    _for_each_row_group(i, tot_ref, dstg_ref, start)

    @pl.when(i == N_TILES - 1)
    def _():
        wait_tile(i - 1, 1 - slot)
        wait_tile(i, slot)


def _dispatch(sched, u2, meta, srcv):
    return pl.pallas_call(
        _dispatch_kernel,
        grid_spec=pltpu.PrefetchScalarGridSpec(
            num_scalar_prefetch=2,
            grid=(N_TILES,),
            in_specs=[pl.BlockSpec((TM, D), lambda i, *_: (i, 0)),
                      pl.BlockSpec((TM, LANES), lambda i, *_: (i, 0)),
                      pl.BlockSpec((1, 1, LANES), lambda i, *_: (i, 0, 0))],
            out_specs=pl.BlockSpec(memory_space=pl.ANY),
            scratch_shapes=[pltpu.VMEM((2, SORT_ROWS, D), F32), pltpu.SemaphoreType.DMA((2,))],
        ),
        out_shape=jax.ShapeDtypeStruct((DISPATCH_ROWS, D), F32),
        compiler_params=_cparams(("arbitrary",)),
        name="moe_dispatch",
    )(*sched, u2, meta, srcv)


def _experts_kernel(start_ref, chunks_ref, xs_ref, wg_ref, wu_ref, wd_ref, ys_ref,
                    wg_bf, wu_bf, wd_bf, xbuf, ybuf, in_sems, out_sems):
    e = pl.program_id(0)
    n = chunks_ref[e]
    first = start_ref[e] // MOE_TM
    total = start_ref[N_EXPERTS - 1] // MOE_TM + chunks_ref[N_EXPERTS - 1]

    def rows(g):
        return pl.ds(pl.multiple_of(g * MOE_TM, MOE_TM), MOE_TM)

    def load(g):
        slot = g % MOE_IN_SLOTS
        return pltpu.make_async_copy(xs_ref.at[rows(g)], xbuf.at[slot], in_sems.at[slot])

    def store(g):
        slot = g % MOE_OUT_SLOTS
        return pltpu.make_async_copy(ybuf.at[slot], ys_ref.at[rows(g)], out_sems.at[slot])

    @pl.when(e == 0)
    def _():
        for g in range(MOE_AHEAD):
            @pl.when(g < total)
            def _(g=g):
                load(g).start()

    @pl.when(n > 0)
    def _():
        wg_bf[...] = wg_ref[0, 0].astype(BF16)
        wu_bf[...] = wu_ref[0, 0].astype(BF16)
        wd_bf[...] = wd_ref[0, 0].astype(BF16)

        def tile(g, carry):
            load(g).wait()

            @pl.when(g + MOE_AHEAD < total)
            def _():
                load(g + MOE_AHEAD).start()

            @pl.when(g >= MOE_OUT_SLOTS)
            def _():
                store(g - MOE_OUT_SLOTS).wait()

            x = xbuf[g % MOE_IN_SLOTS].astype(BF16)
            h = (_silu(_dot(x, wg_bf[...])) * _dot(x, wu_bf[...])).astype(BF16)
            ybuf[g % MOE_OUT_SLOTS] = _dot(h, wd_bf[...])
            store(g).start()
            return carry

        lax.fori_loop(first, first + n, tile, 0)

    @pl.when(e == N_EXPERTS - 1)
    def _():
        for back in range(MOE_OUT_SLOTS, 0, -1):
            @pl.when(total >= back)
            def _(back=back):
                store(total - back).wait()


def _experts(seg_start, seg_chunks, xs, w_gate, w_up, w_down, li):
    def weight(shape):
        return pl.BlockSpec((1, 1) + shape, lambda e, *_: (li, e, 0, 0))

    return pl.pallas_call(
        _experts_kernel,
        grid_spec=pltpu.PrefetchScalarGridSpec(
            num_scalar_prefetch=2,
            grid=(N_EXPERTS,),
            in_specs=[pl.BlockSpec(memory_space=pl.ANY),
                      weight((D, D_EXPERT)), weight((D, D_EXPERT)), weight((D_EXPERT, D))],
            out_specs=pl.BlockSpec(memory_space=pl.ANY),
            scratch_shapes=[pltpu.VMEM((D, D_EXPERT), BF16), pltpu.VMEM((D, D_EXPERT), BF16),
                            pltpu.VMEM((D_EXPERT, D), BF16),
                            pltpu.VMEM((MOE_IN_SLOTS, MOE_TM, D), F32),
                            pltpu.VMEM((MOE_OUT_SLOTS, MOE_TM, D), F32),
                            pltpu.SemaphoreType.DMA((MOE_IN_SLOTS,)),
                            pltpu.SemaphoreType.DMA((MOE_OUT_SLOTS,))],
        ),
        out_shape=jax.ShapeDtypeStruct((MOE_ROWS, D), F32),
        compiler_params=_cparams(("arbitrary",)),
        name="moe_experts",
    )(seg_start, seg_chunks, xs, w_gate, w_up, w_down)


def _combine_kernel(tot_ref, dstg_ref, ys_ref, x1_ref, meta_ref, srcv_ref, g2_ref,
                    lng_ref, lnb_ref, *rest, split, feeds_conv):
    if feeds_conv:
        (sh_ref, sc_ref, w1_ref, b1_ref), rest, w1_bf = rest[:4], rest[4:-1], rest[-1]
        _cast_weight_once(w1_ref, w1_bf)
    outs, (sorted_ref, sems) = rest[:-2], rest[-2:]
    i = pl.program_id(0)
    slot = i % 2

    def fetch(tile, slot):
        buf = sorted_ref.at[slot]

        def start(src, dst):
            pltpu.make_async_copy(ys_ref.at[pl.ds(dst, RUN_ALIGN)], buf.at[pl.ds(src, RUN_ALIGN)],
                                  sems.at[slot]).start()

        _for_each_row_group(tile, tot_ref, dstg_ref, start)

    @pl.when(i == 0)
    def _():
        sorted_ref[...] = jnp.zeros_like(sorted_ref)
        fetch(0, 0)

    @pl.when(i + 1 < N_TILES)
    def _():
        fetch(i + 1, 1 - slot)

    meta = meta_ref[...]
    pos1, pos2 = _sorted_positions(meta, srcv_ref[0])
    sel1 = _one_hot_rows(pos1).astype(BF16)
    sel2 = _one_hot_rows(pos2).astype(BF16)
    buf = sorted_ref.at[slot]
    _wait_rows(tot_ref[i], lambda rows: pltpu.make_async_copy(
        ys_ref.at[pl.ds(0, rows)], buf.at[pl.ds(0, rows)], sems.at[slot]))
    ysort = sorted_ref[slot].astype(BF16)
    f = (meta[:, META_W1:META_W1 + 1] * _dot(sel1, ysort)
         + meta[:, META_W2:META_W2 + 1] * _dot(sel2, ysort))
    y = _layer_norm(ALPHA * x1_ref[...] + g2_ref[0] * f, lng_ref[...], lnb_ref[...])
    if split:
        @pl.when(i < PROMPT_TILES)
        def _():
            outs[0][...] = y

        @pl.when(i >= PROMPT_TILES)
        def _():
            outs[1][...] = y
    else:
        outs[0][...] = y
    if feeds_conv:
        outs[-1][...] = _conv_glu(y, sh_ref, sc_ref, w1_bf, b1_ref)


def _combine(sched, ys, x1, meta, srcv, modr, ln_g, ln_b, li, split, conv_w1=None, conv_b1=None):
    feeds_conv = conv_w1 is not None
    tile = pl.BlockSpec((TM, D), lambda i, *_: (i, 0))
    if split:
        out_specs = [_prompt_tile_spec(D), _sample_tile_spec(D)]
        out_shape = [jax.ShapeDtypeStruct((T_PROMPT, D), F32), jax.ShapeDtypeStruct((T_SAMPLE, D), F32)]
    else:
        out_specs = [tile]
        out_shape = [jax.ShapeDtypeStruct((T, D), F32)]
    extra_specs, extra_args, extra_scratch = [], [], []
    if feeds_conv:
        extra_specs = [_mod_spec(li + 1, 0), _mod_spec(li + 1, 1),
                       _resident_f32_weight((D, 2 * D)), _row_spec((1, 2 * D))]
        extra_args = [modr, modr, conv_w1, conv_b1]
        extra_scratch = [pltpu.VMEM((D, 2 * D), BF16)]
        out_specs = out_specs + [tile]
        out_shape = out_shape + [jax.ShapeDtypeStruct((T, D), F32)]
    return pl.pallas_call(
        functools.partial(_combine_kernel, split=split, feeds_conv=feeds_conv),
        grid_spec=pltpu.PrefetchScalarGridSpec(
            num_scalar_prefetch=2,
            grid=(N_TILES,),
            in_specs=[pl.BlockSpec(memory_space=pl.ANY),
                      pl.BlockSpec((TM, D), lambda i, *_: (i, 0)),
                      pl.BlockSpec((TM, LANES), lambda i, *_: (i, 0)),
                      pl.BlockSpec((1, 1, LANES), lambda i, *_: (i, 0, 0)),
                      _mod_spec(li, 5), _row_spec((1, D)), _row_spec((1, D))] + extra_specs,
            out_specs=out_specs,
            scratch_shapes=[pltpu.VMEM((2, SORT_ROWS, D), F32), pltpu.SemaphoreType.DMA((2,))] + extra_scratch,
        ),
        out_shape=out_shape,
        compiler_params=_cparams(("arbitrary",)),
        name="moe_combine",
    )(*sched, ys, x1, meta, srcv, modr, ln_g, ln_b, *extra_args)


def _moe_schedule(tile_counts):
    n = (tile_counts + RUN_ALIGN - 1) // RUN_ALIGN * RUN_ALIGN
    src = jnp.cumsum(n, axis=1) - n
    per_expert = jnp.sum(n, axis=0)
    seg = (per_expert + MOE_TM - 1) // MOE_TM * MOE_TM
    seg_start = jnp.cumsum(seg) - seg
    dst = seg_start[None, :] + jnp.cumsum(n, axis=0) - n
    g_row = jnp.arange(SORT_GROUPS, dtype=jnp.int32) * RUN_ALIGN
    in_run = jnp.logical_and(src[:, None, :] <= g_row[None, :, None],
                             g_row[None, :, None] < (src + n)[:, None, :])
    dst_g = g_row[None, :] + jnp.sum(jnp.where(in_run, (dst - src)[:, None, :], 0), axis=2)
    step_rows = RUN_ALIGN * GROUP_UNROLL
    rows = jnp.sum(n, axis=1)
    issued = (rows + step_rows - 1) // step_rows * step_rows
    real = g_row[None, :] < rows[:, None]
    spill = (MOE_ROWS + jnp.arange(N_TILES, dtype=jnp.int32)[:, None] * step_rows
             + g_row[None, :] % step_rows)
    scatter_runs = (issued, jnp.where(real, dst_g, spill).reshape(-1))
    gather_runs = (issued, jnp.where(real, dst_g, dst_g[:, :1]).reshape(-1))
    srcv = jnp.pad(src.astype(F32), ((0, 0), (ROUTER_LANE0, LANES - ROUTER_LANE0 - N_EXPERTS)))
    return scatter_runs, gather_runs, srcv.reshape(N_TILES, 1, LANES), (seg_start, seg // MOE_TM)


def _moe(x1, u2, meta, cnt, modr, w_gate, w_up, w_down, ln_g, ln_b, li, split, **next_conv):
    tile_counts = cnt[:, 0, ROUTER_LANE0:ROUTER_LANE0 + N_EXPERTS].astype(jnp.int32)
    scatter_runs, gather_runs, srcv, (seg_start, seg_chunks) = _moe_schedule(tile_counts)
    xs = _dispatch(scatter_runs, u2, meta, srcv)
    ys = _experts(seg_start, seg_chunks, xs, w_gate, w_up, w_down, li)
    return _combine(gather_runs, ys, x1, meta, srcv, modr, ln_g, ln_b, li, split, **next_conv)


def _router_slab(wg, bg, we, be):
    w = jnp.concatenate([wg, we.transpose(1, 0, 2).reshape(D, N_EXPERTS)], axis=1)
    b = jnp.concatenate([bg, be.reshape(N_EXPERTS)])
    pad = LANES - w.shape[1]
    return jnp.pad(w, ((0, 0), (0, pad))), jnp.pad(b, (0, pad)).reshape(1, LANES)


def kernel(x_prompt, x_sample, cache_diff_k, cache_diff_v, state_ret_fwd, state_ret_bwd, c, c_ctx, mod_w, mod_b, ln1_g, ln1_b, ln2_g, ln2_b, mix_w_in, mix_w_out, ret_decay_fwd, ret_decay_bwd, diff_lq1, diff_lk1, diff_lq2, diff_lk2, diff_subln_g, conv_w1, conv_b1, conv_dw, conv_dw_b, conv_ln_g, conv_ln_b, conv_w2, conv_b2, router_g_w, router_g_b, router_e_w, router_e_b, moe_w_gate, moe_w_up, moe_w_down):
    xp = x_prompt.reshape(T_PROMPT, D)
    xs = x_sample.reshape(T_SAMPLE, D)
    cond = jnp.concatenate([c_ctx[None, :], c, jnp.zeros((MOD_ROWS - 1 - DEC_BATCH, D), F32)], axis=0)
    modr = _mod_vectors(cond, mod_w, mod_b).reshape(DEPTH * MOD_ROWS * 6, 1, D)
    cos, sin_signed = _rope_tables()

    def row(v):
        return v.reshape(1, -1)

    x = None
    caches = None
    for li in range(DEPTH):
        wr, br = _router_slab(router_g_w[li], router_g_b[li], router_e_w[li], router_e_b[li])
        if li % 2 == 0:
            assert li == 0, "the even mixer reads the kernel inputs directly"
            e = li // 2
            lam_init = 0.8 - 0.6 * math.exp(-0.3 * li)
            proj, ck, cv = _in_proj(xp, xs, modr, mix_w_in[e], li)
            dec = jnp.concatenate([ret_decay_fwd[e], ret_decay_bwd[e]])
            r_p, sf, sb = _retention(proj, dec, BATCH, SEQ, 0, HEADS, emit_state=True, chunk_len=SEQ)
            (r_s,) = _retention(proj, dec, DEC_BATCH, DEC_SEQ, T_PROMPT // DEC_SEQ, 2,
                                s0f=state_ret_fwd, s0b=state_ret_bwd, e=e)
            lams = (row(diff_lq1[e]), row(diff_lk1[e]), row(diff_lq2[e]), row(diff_lk2[e]),
                    row(diff_subln_g[e]))
            o_p = _attn_prompt(proj, *lams, lam_init)
            o_s = _attn_sample(proj, cache_diff_k, cache_diff_v, cos, sin_signed, *lams, lam_init, e)
            x1, u2, meta, cnt = _out_proj_tail(r_p, r_s, o_p, o_s, mix_w_out[e], xp, xs, modr,
                                               row(ln1_g[li]), row(ln1_b[li]), wr, br, li)
            caches = (ck, cv, sf, sb)
        else:
            o = li // 2
            x1, u2, meta, cnt = _conv_tail(glu, conv_dw[o], row(conv_dw_b[o]), row(conv_ln_g[o]),
                                           row(conv_ln_b[o]), conv_w2[o], row(conv_b2[o]),
                                           x, modr, row(ln1_g[li]), row(ln1_b[li]), wr, br, li)
        next_conv = {}
        if li + 1 < DEPTH and (li + 1) % 2 == 1:
            next_conv = dict(conv_w1=conv_w1[(li + 1) // 2], conv_b1=row(conv_b1[(li + 1) // 2]))
        outs = _moe(x1, u2, meta, cnt, modr, moe_w_gate, moe_w_up, moe_w_down,
                    row(ln2_g[li]), row(ln2_b[li]), li, split=(li == DEPTH - 1), **next_conv)
        x, glu = outs[0], outs[-1]

    y_prompt = outs[0].reshape(BATCH, SEQ, D)
    y_sample = outs[1].reshape(DEC_BATCH, DEC_SEQ, D)
    return (y_prompt, y_sample) + caches
```

```python
import functools
import math

import numpy as np
import jax
import jax.numpy as jnp
from jax import lax
from jax.experimental import pallas as pl
from jax.experimental.pallas import tpu as pltpu

F32 = jnp.float32
BF16 = jnp.bfloat16

D = 1024
BATCH = 16
SEQ = 256
DEPTH = 2
DEC_BATCH = 2
DEC_SEQ = 2048
PAST_LEN = 512
GRID_W = 64
HEADS = 4
HEAD_W = 128
RET_CHUNK = 128
DIFF_DK = 64
ROPE_THETA = 10000.0
IN_W = 7 * HEADS * HEAD_W
CONV_K = 31
CONV_PAD = CONV_K // 2
N_GROUPS = 4
EXPERTS_PER_GROUP = 8
N_EXPERTS = N_GROUPS * EXPERTS_PER_GROUP
D_EXPERT = 512
ALPHA = (2.0 * DEPTH) ** 0.25
LN_EPS = 1e-5
GN_EPS = 1e-6

T_PROMPT = BATCH * SEQ
T_SAMPLE = DEC_BATCH * DEC_SEQ
T = T_PROMPT + T_SAMPLE
TM = 256
N_TILES = T // TM
PROMPT_TILES = T_PROMPT // TM
SAMPLE_TILES_PER_SEQ = DEC_SEQ // TM
MOD_ROWS = 8
MOE_TM = 256
LANES = 128
SUBLANES = 8
RUN_ALIGN = SUBLANES
SORT_ROWS = -(-(2 * TM + N_EXPERTS * (RUN_ALIGN - 1)) // TM) * TM
RUN_BITS = tuple(1 << b for b in range((2 * TM).bit_length() - 1, RUN_ALIGN.bit_length() - 2, -1))
MOE_MAX_TILES = -(-(2 * T + N_TILES * N_EXPERTS * (RUN_ALIGN - 1) + N_EXPERTS * (MOE_TM - RUN_ALIGN)) // MOE_TM)
MOE_ROWS = MOE_MAX_TILES * MOE_TM
SORT_GROUPS = SORT_ROWS // RUN_ALIGN
GROUP_UNROLL = 8
DISPATCH_ROWS = MOE_ROWS + N_TILES * RUN_ALIGN * GROUP_UNROLL
MOE_AHEAD = 8
MOE_IN_SLOTS = MOE_AHEAD + 1
MOE_OUT_SLOTS = 4
ROUTER_LANE0 = N_GROUPS
ROPE_GROUP = DIFF_DK // 2
ROPE_HALF = ROPE_GROUP // 2
V7X_VMEM_BYTES = 64 * 1024 * 1024
VMEM_LIMIT = V7X_VMEM_BYTES - 12 * 1024 * 1024


def _cparams(sem):
    return pltpu.CompilerParams(dimension_semantics=sem, vmem_limit_bytes=VMEM_LIMIT)


def _tile_cond_row(i, tm):
    return jnp.where(i < T_PROMPT // tm, 0, 1 + (i - T_PROMPT // tm) // (DEC_SEQ // tm))


def _mod_spec(li, k, tm=TM):
    return pl.BlockSpec((1, 1, D), lambda i, *_: ((li * MOD_ROWS + _tile_cond_row(i, tm)) * 6 + k, 0, 0))


def _row_spec(shape):
    return pl.BlockSpec(shape, lambda i, *_: (0,) * len(shape))


def _resident_f32_weight(shape):
    return pl.BlockSpec(shape, lambda i, *_: (0,) * len(shape), pipeline_mode=pl.Buffered(1))


def _cast_weight_once(w_ref, w_bf):
    @pl.when(pl.program_id(0) == 0)
    def _():
        w_bf[...] = w_ref[...].astype(BF16)


def _layer_norm(x, g, b):
    mu = jnp.mean(x, axis=-1, keepdims=True)
    xc = x - mu
    var = jnp.mean(xc * xc, axis=-1, keepdims=True)
    return xc * lax.rsqrt(var + LN_EPS) * g + b


def _silu(x):
    return x * jax.nn.sigmoid(x)


def _dot(a, b):
    return jnp.dot(a, b, preferred_element_type=F32)


def _dot_nt(a, b):
    return lax.dot_general(a, b, (((1,), (1,)), ((), ())), preferred_element_type=F32)


def _dot_tn(a, b):
    return lax.dot_general(a, b, (((0,), (0,)), ((), ())), preferred_element_type=F32)


MOD_TN = 2048
MOD_USED_ROWS = 1 + DEC_BATCH


def _mod_kernel(cond_t_ref, w_ref, b_ref, o_ref):
    s = _silu(cond_t_ref[...])
    w = w_ref[0]
    o_ref[0] = jnp.zeros((MOD_ROWS, MOD_TN), F32) + b_ref[0]
    for r in range(MOD_USED_ROWS):
        o_ref[0, r:r + 1, :] = jnp.sum(w * s[:, r:r + 1], axis=0, keepdims=True) + b_ref[0]


def _mod_vectors(cond, mod_w, mod_b):
    return pl.pallas_call(
        _mod_kernel,
        grid=(DEPTH, 6 * D // MOD_TN),
        in_specs=[
            pl.BlockSpec((D, MOD_ROWS), lambda l, j: (0, 0)),
            pl.BlockSpec((1, D, MOD_TN), lambda l, j: (l, 0, j)),
            pl.BlockSpec((1, 1, MOD_TN), lambda l, j: (l, 0, j)),
        ],
        out_specs=pl.BlockSpec((1, MOD_ROWS, MOD_TN), lambda l, j: (l, 0, j)),
        out_shape=jax.ShapeDtypeStruct((DEPTH, MOD_ROWS, 6 * D), F32),
        compiler_params=_cparams(("arbitrary", "arbitrary")),
        name="mod_vectors",
    )(cond.T, mod_w, mod_b.reshape(DEPTH, 1, 6 * D))


def _prompt_tile_spec(width, tm=TM):
    return pl.BlockSpec((tm, width), lambda i, *_: (jnp.minimum(i, T_PROMPT // tm - 1), 0))


def _sample_tile_spec(width, tm=TM):
    return pl.BlockSpec((tm, width), lambda i, *_: (jnp.maximum(i - T_PROMPT // tm, 0), 0))


def _pick_tile(prompt_ref, sample_ref, tm=TM):
    return jnp.where(pl.program_id(0) < T_PROMPT // tm, prompt_ref[...], sample_ref[...])


IN_TM = 512
IN_SEQS = IN_TM // SEQ


def _in_proj_kernel(xp_ref, xs_ref, sh_ref, sc_ref, w_ref, o_ref, ck_ref, cv_ref, w_bf):
    _cast_weight_once(w_ref, w_bf)
    u = _pick_tile(xp_ref, xs_ref, IN_TM) * (1.0 + sc_ref[0]) + sh_ref[0]
    proj = _dot(u.astype(BF16), w_bf[...])
    o_ref[...] = proj.astype(BF16)

    @pl.when(pl.program_id(0) < T_PROMPT // IN_TM)
    def _():
        for s in range(IN_SEQS):
            rows = slice(s * SEQ, (s + 1) * SEQ)
            for h in range(HEADS):
                ck_ref[s, 0, h] = proj[rows, (COL_KD + h) * HEAD_W:(COL_KD + h + 1) * HEAD_W]
                cv_ref[s, 0, h] = proj[rows, (COL_VD + h) * HEAD_W:(COL_VD + h + 1) * HEAD_W]


def _in_proj(x_prompt, x_sample, modr, w_in, li):
    cache_spec = pl.BlockSpec((IN_SEQS, 1, HEADS, SEQ, HEAD_W),
                              lambda i: (jnp.minimum(i, T_PROMPT // IN_TM - 1), 0, 0, 0, 0))
    cache_shape = jax.ShapeDtypeStruct((BATCH, 1, HEADS, SEQ, HEAD_W), F32)
    return pl.pallas_call(
        _in_proj_kernel,
        grid=(T // IN_TM,),
        in_specs=[
            _prompt_tile_spec(D, IN_TM), _sample_tile_spec(D, IN_TM),
            _mod_spec(li, 0, IN_TM),
            _mod_spec(li, 1, IN_TM),
            _resident_f32_weight((D, IN_W)),
        ],
        out_specs=[pl.BlockSpec((IN_TM, IN_W), lambda i: (i, 0)), cache_spec, cache_spec],
        out_shape=[jax.ShapeDtypeStruct((T, IN_W), BF16), cache_shape, cache_shape],
        scratch_shapes=[pltpu.VMEM((D, IN_W), BF16)],
        compiler_params=_cparams(("arbitrary",)),
        name="in_proj",
    )(x_prompt, x_sample, modr, modr, w_in)


COL_QR, COL_KR, COL_VR, COL_GR, COL_QD, COL_KD, COL_VD = (k * HEADS for k in range(7))


def _retention_kernel(dec_ref, q_ref, k_ref, v_ref, g_ref, *rest, chunk_len, n_chunks, n_heads, has_state,
                      emit_state):
    rest = list(rest)
    if has_state:
        s0f_ref, s0b_ref = rest[:2]
        rest = rest[2:]
    r_ref = rest[0]
    rest = rest[1:]
    if emit_state:
        sf_ref, sb_ref = rest[:2]
        rest = rest[2:]
    of_ref = rest[0]

    head0 = pl.program_id(1) * n_heads
    C = chunk_len
    ii = lax.broadcasted_iota(jnp.int32, (C, C), 0)
    jj = lax.broadcasted_iota(jnp.int32, (C, C), 1)
    rel = (ii - jj).astype(F32)
    idx = lax.broadcasted_iota(jnp.int32, (C, 1), 0).astype(F32)
    k_scale = HEAD_W ** -0.5

    def chunk(ref, c, h):
        return ref[c * C:(c + 1) * C, h * HEAD_W:(h + 1) * HEAD_W].astype(F32)

    def decays(direction, h):
        lg = -jnp.exp(jnp.full((1, 1), dec_ref[direction * HEADS + head0 + h], F32))
        if direction == 0:
            inner = jnp.where(rel >= 0, jnp.exp(jnp.maximum(rel, 0.0) * lg), 0.0)
            return inner, jnp.exp((idx + 1.0) * lg), jnp.exp((C - 1.0 - idx) * lg), jnp.exp(C * lg)
        inner = jnp.where(rel <= 0, jnp.exp(jnp.maximum(-rel, 0.0) * lg), 0.0)
        return inner, jnp.exp((C - idx) * lg), jnp.exp(idx * lg), jnp.exp(C * lg)

    def run(direction):
        dec = [decays(direction, h) for h in range(n_heads)]
        if has_state:
            s0_ref = s0f_ref if direction == 0 else s0b_ref
            states = [s0_ref[0, 0, h] for h in range(n_heads)]
        else:
            states = [None] * n_heads
        order = range(n_chunks) if direction == 0 else range(n_chunks - 1, -1, -1)
        for c in order:
            rows = slice(c * C, (c + 1) * C)
            for h in range(n_heads):
                inner, q_decay, k_decay, chunk_decay = dec[h]
                cols = slice(h * HEAD_W, (h + 1) * HEAD_W)
                s = states[h]
                qc = chunk(q_ref, c, h)
                kc = chunk(k_ref, c, h) * k_scale
                vc = chunk(v_ref, c, h).astype(BF16)
                scores = _dot_nt(qc.astype(BF16), kc.astype(BF16)) * inner
                o = _dot(scores.astype(BF16), vc)
                kv = _dot_tn((kc * k_decay).astype(BF16), vc)
                if s is not None:
                    o = o + _dot((qc * q_decay).astype(BF16), s.astype(BF16))
                    kv = s * chunk_decay + kv
                states[h] = kv
                if direction == 0:
                    of_ref[rows, cols] = o
                else:
                    r = of_ref[rows, cols] + o
                    mu = jnp.mean(r, axis=-1, keepdims=True)
                    rc = r - mu
                    var = jnp.mean(rc * rc, axis=-1, keepdims=True)
                    rn = rc * lax.rsqrt(var + GN_EPS)
                    r_ref[rows, cols] = _silu(chunk(g_ref, c, h)) * rn
        return states

    sf = run(0)
    sb = run(1)
    if emit_state:
        for h in range(n_heads):
            sf_ref[0, 0, h] = sf[h]
            sb_ref[0, 0, h] = sb[h]


def _retention(proj, dec, n_seq, seq_len, row_block0, n_heads, s0f=None, s0b=None, e=0, emit_state=False,
               chunk_len=RET_CHUNK):
    has_state = s0f is not None
    width = n_heads * HEAD_W

    def col(base):
        return pl.BlockSpec((seq_len, width), lambda b, h, *_: (row_block0 + b, base // n_heads + h))

    state_spec = pl.BlockSpec((1, 1, n_heads, HEAD_W, HEAD_W), lambda b, h, *_: (b, e, h, 0, 0))
    in_specs = [pl.BlockSpec(memory_space=pltpu.SMEM), col(COL_QR), col(COL_KR), col(COL_VR), col(COL_GR)]
    args = [dec, proj, proj, proj, proj]
    if has_state:
        in_specs += [state_spec, state_spec]
        args += [s0f, s0b]
    out_specs = [pl.BlockSpec((seq_len, width), lambda b, h, *_: (b, h))]
    out_shape = [jax.ShapeDtypeStruct((n_seq * seq_len, HEADS * HEAD_W), F32)]
    if emit_state:
        st = pl.BlockSpec((1, 1, n_heads, HEAD_W, HEAD_W), lambda b, h, *_: (b, 0, h, 0, 0))
        out_specs += [st, st]
        out_shape += [jax.ShapeDtypeStruct((n_seq, 1, HEADS, HEAD_W, HEAD_W), F32)] * 2
    return pl.pallas_call(
        functools.partial(_retention_kernel, chunk_len=chunk_len, n_chunks=seq_len // chunk_len, n_heads=n_heads,
                          has_state=has_state, emit_state=emit_state),
        grid=(n_seq, HEADS // n_heads),
        in_specs=in_specs,
        out_specs=out_specs,
        out_shape=out_shape,
        scratch_shapes=[pltpu.VMEM((seq_len, width), F32)],
        compiler_params=_cparams(("arbitrary", "arbitrary")),
        name=f"retention_{seq_len}",
    )(*args)


def _diff_lambda(lq1_ref, lk1_ref, lq2_ref, lk2_ref, lam_init):
    a = jnp.sum(lq1_ref[...] * lk1_ref[...], axis=-1, keepdims=True)
    b = jnp.sum(lq2_ref[...] * lk2_ref[...], axis=-1, keepdims=True)
    return jnp.exp(a) - jnp.exp(b) + lam_init


LOG2E = 1.4426950408889634


def _diff_attend(q, k, v, lam, subln_g, lam_init):
    lane = lax.broadcasted_iota(jnp.int32, q.shape, 1)
    q1 = jnp.where(lane < DIFF_DK, q, 0.0).astype(BF16)
    q2 = jnp.where(lane >= DIFF_DK, q, 0.0).astype(BF16)

    def softmax_times_v(qz):
        s = _dot_nt(qz, k)
        p = jnp.exp2(s - jnp.max(s, axis=-1, keepdims=True))
        return _dot(p.astype(BF16), v) * (1.0 / jnp.sum(p, axis=-1, keepdims=True))

    o = softmax_times_v(q1) - lam * softmax_times_v(q2)
    o = o * lax.rsqrt(jnp.mean(o * o, axis=-1, keepdims=True) + LN_EPS)
    return o * subln_g * (1.0 - lam_init)


def _attn_prompt_kernel(q_ref, k_ref, v_ref, lq1, lk1, lq2, lk2, g_ref, o_ref, *, lam_init):
    lam = _diff_lambda(lq1, lk1, lq2, lk2, lam_init)
    scale = DIFF_DK ** -0.5 * LOG2E
    for h in range(HEADS):
        sl = slice(h * HEAD_W, (h + 1) * HEAD_W)
        o_ref[:, sl] = _diff_attend(q_ref[:, sl].astype(F32) * scale, k_ref[:, sl], v_ref[:, sl],
                                    lam, g_ref[...], lam_init)


def _attn_prompt(proj, lq1, lk1, lq2, lk2, subln_g, lam_init):
    W = HEADS * HEAD_W

    def slab(base):
        return pl.BlockSpec((SEQ, W), lambda b: (b, base // HEADS))

    small = _row_spec((1, DIFF_DK))
    return pl.pallas_call(
        functools.partial(_attn_prompt_kernel, lam_init=lam_init),
        grid=(BATCH,),
        in_specs=[slab(COL_QD), slab(COL_KD), slab(COL_VD), small, small, small, small,
                  _row_spec((1, HEAD_W))],
        out_specs=pl.BlockSpec((SEQ, W), lambda b: (b, 0)),
        out_shape=jax.ShapeDtypeStruct((T_PROMPT, W), F32),
        compiler_params=_cparams(("arbitrary",)),
        name="diff_attn_prompt",
    )(proj, proj, proj, lq1, lk1, lq2, lk2, subln_g)


def _rope(x, cos, sin_signed):
    lane = lax.broadcasted_iota(jnp.int32, x.shape, 1)
    partner = jnp.where((lane % ROPE_GROUP) < ROPE_HALF,
                        pltpu.roll(x, LANES - ROPE_HALF, 1), pltpu.roll(x, ROPE_HALF, 1))
    return x * cos + partner * sin_signed


def _attn_sample_kernel(q_ref, k_ref, v_ref, ck_ref, cv_ref, cosq_ref, sinq_ref, cos_ref, sin_ref,
                        lq1, lk1, lq2, lk2, g_ref, o_ref, kbuf, vbuf, *, lam_init):
    @pl.when(pl.program_id(2) == 0)
    def _():
        kbuf[0:DEC_SEQ, :] = _rope(k_ref[...].astype(F32), cos_ref[...], sin_ref[...]).astype(BF16)
        kbuf[DEC_SEQ:, :] = ck_ref[0, 0, 0].astype(BF16)
        vbuf[0:DEC_SEQ, :] = v_ref[...]
        vbuf[DEC_SEQ:, :] = cv_ref[0, 0, 0].astype(BF16)

    lam = _diff_lambda(lq1, lk1, lq2, lk2, lam_init)
    q = _rope(q_ref[...].astype(F32), cosq_ref[...], sinq_ref[...]) * (DIFF_DK ** -0.5 * LOG2E)
    o_ref[...] = _diff_attend(q, kbuf[...], vbuf[...], lam, g_ref[...], lam_init)


ATTN_TQ = 256


def _attn_sample(proj, cache_k, cache_v, cos, sin_signed, lq1, lk1, lq2, lk2, subln_g, lam_init, e):
    nq = DEC_SEQ // ATTN_TQ
    row0_q = T_PROMPT // ATTN_TQ
    row0_kv = T_PROMPT // DEC_SEQ
    small = pl.BlockSpec((1, DIFF_DK), lambda b, h, t: (0, 0))
    cache = pl.BlockSpec((1, 1, 1, PAST_LEN, HEAD_W), lambda b, h, t: (b, e, h, 0, 0))
    table_q = pl.BlockSpec((ATTN_TQ, HEAD_W), lambda b, h, t: (t, 0))
    table = pl.BlockSpec((DEC_SEQ, HEAD_W), lambda b, h, t: (0, 0))
    return pl.pallas_call(
        functools.partial(_attn_sample_kernel, lam_init=lam_init),
        grid=(DEC_BATCH, HEADS, nq),
        in_specs=[
            pl.BlockSpec((ATTN_TQ, HEAD_W), lambda b, h, t: (row0_q + b * nq + t, COL_QD + h)),
            pl.BlockSpec((DEC_SEQ, HEAD_W), lambda b, h, t: (row0_kv + b, COL_KD + h)),
            pl.BlockSpec((DEC_SEQ, HEAD_W), lambda b, h, t: (row0_kv + b, COL_VD + h)),
            cache, cache, table_q, table_q, table, table,
            small, small, small, small,
            pl.BlockSpec((1, HEAD_W), lambda b, h, t: (0, 0)),
        ],
        out_specs=pl.BlockSpec((ATTN_TQ, HEAD_W), lambda b, h, t: (b * nq + t, h)),
        out_shape=jax.ShapeDtypeStruct((T_SAMPLE, HEADS * HEAD_W), F32),
        scratch_shapes=[pltpu.VMEM((DEC_SEQ + PAST_LEN, HEAD_W), BF16),
                        pltpu.VMEM((DEC_SEQ + PAST_LEN, HEAD_W), BF16)],
        compiler_params=_cparams(("arbitrary", "arbitrary", "arbitrary")),
        name="diff_attn_sample",
    )(proj, proj, proj, cache_k, cache_v, cos, sin_signed, cos, sin_signed,
      lq1, lk1, lq2, lk2, subln_g)


def _rope_tables():
    t = np.arange(DEC_SEQ)
    row, colp = t // GRID_W, t % GRID_W
    lane = np.arange(LANES)
    pos = np.where(((lane // ROPE_GROUP) % 2 == 0)[None, :], row[:, None], colp[:, None]).astype(np.float64)
    half = ROPE_HALF
    inv = (np.float32(ROPE_THETA) ** (-(np.arange(half, dtype=np.float32)) / np.float32(half))).astype(np.float32)
    ang = pos.astype(np.float32) * inv[lane % half][None, :]
    cos = np.cos(ang.astype(np.float64)).astype(np.float32)
    sin = np.sin(ang.astype(np.float64)).astype(np.float32)
    sign = np.where((lane % ROPE_GROUP) < half, -1.0, 1.0).astype(np.float32)[None, :]
    return jnp.asarray(cos), jnp.asarray(sin * sign)


def _split_bf16(a):
    hi = a.astype(BF16)
    return hi, (a - hi.astype(F32)).astype(BF16)


def _mixer_tail(out, x, g1_ref, sc2_ref, sh2_ref, lng_ref, lnb_ref, wr_ref, br_ref,
                x1_ref, u2_ref, meta_ref, cnt_ref, wr_bf, before_bf):
    @pl.when(pl.program_id(0) == 0)
    def _():
        w_hi, w_lo = _split_bf16(wr_ref[...])
        wr_bf[:, 0:LANES] = w_hi
        wr_bf[:, LANES:] = w_lo
        r_i = lax.broadcasted_iota(jnp.int32, (TM, TM), 0)
        c_i = lax.broadcasted_iota(jnp.int32, (TM, TM), 1)
        before_bf[...] = (c_i < r_i).astype(BF16)

    x1 = _layer_norm(ALPHA * x + g1_ref[0] * out, lng_ref[...], lnb_ref[...])
    x1_ref[...] = x1
    u2 = x1 * (1.0 + sc2_ref[0]) + sh2_ref[0]
    u2_ref[...] = u2.astype(BF16)

    u_hi, u_lo = _split_bf16(u2)
    hi = _dot(u_hi, wr_bf[...])
    logits = hi[:, 0:LANES] + (hi[:, LANES:] + _dot(u_lo, wr_bf[:, 0:LANES])) + br_ref[...]
    lane = lax.broadcasted_iota(jnp.int32, logits.shape, 1).astype(F32)
    neg = jnp.float32(-jnp.inf)
    is_g = lane < N_GROUPS
    gl = jnp.where(is_g, logits, neg)
    gmax = jnp.max(gl, axis=-1, keepdims=True)
    gsel = jnp.min(jnp.where(gl == gmax, lane, float(LANES)), axis=-1, keepdims=True)
    p_g = 1.0 / jnp.sum(jnp.where(is_g, jnp.exp(gl - gmax), 0.0), axis=-1, keepdims=True)
    lo = ROUTER_LANE0 + gsel * EXPERTS_PER_GROUP
    el = jnp.where((lane >= lo) & (lane < lo + EXPERTS_PER_GROUP), logits, neg)
    v1 = jnp.max(el, axis=-1, keepdims=True)
    i1 = jnp.min(jnp.where(el == v1, lane, float(LANES)), axis=-1, keepdims=True)
    el2 = jnp.where(lane == i1, neg, el)
    v2 = jnp.max(el2, axis=-1, keepdims=True)
    i2 = jnp.min(jnp.where(el2 == v2, lane, float(LANES)), axis=-1, keepdims=True)
    t = jnp.exp(v2 - v1)
    w1 = p_g / (1.0 + t)
    w2 = w1 * t

    oh1 = (lane == i1).astype(F32)
    oh2 = (lane == i2).astype(F32)
    oh = oh1 + oh2
    earlier = _dot(before_bf[...], oh.astype(BF16))
    rank1 = jnp.sum(earlier * oh1, axis=-1, keepdims=True)
    rank2 = jnp.sum(earlier * oh2, axis=-1, keepdims=True)
    cnt_ref[0] = jnp.sum(oh, axis=0, keepdims=True)
    cols = (i1, i2, w1, w2, rank1, rank2)
    meta = jnp.zeros_like(logits)
    for k, col in enumerate(cols):
        meta = jnp.where(lane == k, col, meta)
    meta_ref[...] = meta


META_E1, META_E2, META_W1, META_W2, META_RANK1, META_RANK2 = range(6)

def _tail_scratch():
    return [pltpu.VMEM((D, 2 * LANES), BF16), pltpu.VMEM((TM, TM), BF16)]


_TAIL_OUT_SHAPES = [
    jax.ShapeDtypeStruct((T, D), F32),
    jax.ShapeDtypeStruct((T, D), BF16),
    jax.ShapeDtypeStruct((T, LANES), F32),
    jax.ShapeDtypeStruct((N_TILES, 1, LANES), F32),
]


def _tail_out_specs():
    return [
        pl.BlockSpec((TM, D), lambda i: (i, 0)),
        pl.BlockSpec((TM, D), lambda i: (i, 0)),
        pl.BlockSpec((TM, LANES), lambda i: (i, 0)),
        pl.BlockSpec((1, 1, LANES), lambda i: (i, 0, 0)),
    ]


def _tail_in_specs(li):
    return [
        _mod_spec(li, 2), _mod_spec(li, 4), _mod_spec(li, 3),
        _row_spec((1, D)), _row_spec((1, D)),
        _row_spec((D, LANES)), _row_spec((1, LANES)),
    ]


def _out_proj_kernel(rp_ref, rs_ref, op_ref, os_ref, w_ref, xp_ref, xs_ref, *rest):
    tail_args, w_bf = rest[:-3] + rest[-2:], rest[-3]
    _cast_weight_once(w_ref, w_bf)
    half = HEADS * HEAD_W
    r = _pick_tile(rp_ref, rs_ref).astype(BF16)
    o = _pick_tile(op_ref, os_ref).astype(BF16)
    out = _dot(r, w_bf[0:half, :]) + _dot(o, w_bf[half:, :])
    _mixer_tail(out, _pick_tile(xp_ref, xs_ref), *tail_args)


def _out_proj_tail(r_p, r_s, o_p, o_s, w_out, x_prompt, x_sample, modr, ln_g, ln_b, wr, br, li):
    half = HEADS * HEAD_W
    return pl.pallas_call(
        _out_proj_kernel,
        grid=(N_TILES,),
        in_specs=[_prompt_tile_spec(half), _sample_tile_spec(half),
                  _prompt_tile_spec(half), _sample_tile_spec(half),
                  _resident_f32_weight((2 * half, D)),
                  _prompt_tile_spec(D), _sample_tile_spec(D)] + _tail_in_specs(li),
        out_specs=_tail_out_specs(),
        out_shape=_TAIL_OUT_SHAPES,
        scratch_shapes=[pltpu.VMEM((2 * half, D), BF16)] + _tail_scratch(),
        compiler_params=_cparams(("arbitrary",)),
        name="out_proj_tail",
    )(r_p, r_s, o_p, o_s, w_out, x_prompt, x_sample, modr, modr, modr, ln_g, ln_b, wr, br)


def _conv_glu(y, sh_ref, sc_ref, w_bf, b_ref):
    u = y * (1.0 + sc_ref[0]) + sh_ref[0]
    h = _dot(u.astype(BF16), w_bf[...]) + b_ref[...]
    return h[:, :D] * jax.nn.sigmoid(h[:, D:])


HALO = 16
CONV_ROWS = 64
CONV_COLS = 128


def _depthwise_conv(hp, dw_ref, conv):
    base = HALO - CONV_PAD
    for cb in range(D // CONV_COLS):
        cs = slice(cb * CONV_COLS, (cb + 1) * CONV_COLS)
        for rb in range(TM // CONV_ROWS):
            r0 = rb * CONV_ROWS
            acc = None
            for shift in range(SUBLANES):
                part = None
                for tap in range(CONV_K):
                    off = base + tap
                    if off % SUBLANES != shift:
                        continue
                    a0 = r0 + off - shift
                    term = hp[a0:a0 + CONV_ROWS + SUBLANES, cs] * dw_ref[tap:tap + 1, cs]
                    part = term if part is None else part + term
                part = part[shift:shift + CONV_ROWS, :]
                acc = part if acc is None else acc + part
            conv[r0:r0 + CONV_ROWS, cs] = acc


def _conv_tail_kernel(cur_ref, prev_ref, next_ref, dw_ref, dwb_ref, cg_ref, cb_ref, w2_ref, b2_ref,
                      x_ref, *rest):
    tail_args, (hp, conv, w2_bf) = rest[:-5] + rest[-2:], rest[-5:-2]
    _cast_weight_once(w2_ref, w2_bf)
    i = pl.program_id(0)
    k = (i - PROMPT_TILES) % SAMPLE_TILES_PER_SEQ
    in_sample = i >= PROMPT_TILES
    left_ok = jnp.logical_and(in_sample, k != 0)
    right_ok = jnp.logical_and(in_sample, k != SAMPLE_TILES_PER_SEQ - 1)
    hp[0:HALO, :] = jnp.where(left_ok, prev_ref[...], 0.0)
    hp[HALO:HALO + TM, :] = cur_ref[...]
    hp[HALO + TM:HALO + TM + HALO, :] = jnp.where(right_ok, next_ref[...], 0.0)
    _depthwise_conv(hp, dw_ref, conv)
    hc = _silu(_layer_norm(conv[...] + dwb_ref[...], cg_ref[...], cb_ref[...]))
    out = _dot(hc.astype(BF16), w2_bf[...]) + b2_ref[...]
    _mixer_tail(out, x_ref[...], *tail_args)


def _conv_tail(glu, dw, dwb, cg, cb, w2, b2, x, modr, ln_g, ln_b, wr, br, li):
    per = TM // HALO
    last = T // HALO - 1
    return pl.pallas_call(
        _conv_tail_kernel,
        grid=(N_TILES,),
        in_specs=[pl.BlockSpec((TM, D), lambda i: (i, 0)),
                  pl.BlockSpec((HALO, D), lambda i: (jnp.maximum(i * per - 1, 0), 0)),
                  pl.BlockSpec((HALO, D), lambda i: (jnp.minimum((i + 1) * per, last), 0)),
                  _row_spec((CONV_K, D)), _row_spec((1, D)), _row_spec((1, D)), _row_spec((1, D)),
                  _resident_f32_weight((D, D)), _row_spec((1, D)),
                  pl.BlockSpec((TM, D), lambda i: (i, 0))] + _tail_in_specs(li),
        out_specs=_tail_out_specs(),
        out_shape=_TAIL_OUT_SHAPES,
        scratch_shapes=[pltpu.VMEM((TM + 2 * HALO, D), F32), pltpu.VMEM((TM, D), F32),
                        pltpu.VMEM((D, D), BF16)] + _tail_scratch(),
        compiler_params=_cparams(("arbitrary",)),
        name="conv_tail",
    )(glu, glu, glu, dw, dwb, cg, cb, w2, b2, x, modr, modr, modr, ln_g, ln_b, wr, br)


def _sorted_positions(meta, srcv):
    lane = lax.broadcasted_iota(jnp.int32, meta.shape, 1).astype(F32)

    def pos(e_col, r_col):
        start = jnp.sum(jnp.where(lane == meta[:, e_col:e_col + 1], srcv, 0.0), axis=-1, keepdims=True)
        return start + meta[:, r_col:r_col + 1]

    return pos(META_E1, META_RANK1), pos(META_E2, META_RANK2)


def _one_hot_rows(pos):
    col = lax.broadcasted_iota(jnp.int32, (TM, SORT_ROWS), 1).astype(F32)
    return col == pos


def _for_each_row_group(tile, tot_ref, dstg_ref, fn):
    def body(k, carry):
        for j in range(GROUP_UNROLL):
            g = k * GROUP_UNROLL + j
            fn(pl.multiple_of(g * RUN_ALIGN, RUN_ALIGN),
               pl.multiple_of(dstg_ref[tile * SORT_GROUPS + g], RUN_ALIGN))
        return carry

    lax.fori_loop(0, tot_ref[tile] // (RUN_ALIGN * GROUP_UNROLL), body, 0)


def _wait_rows(total, make_copy):
    for bit in RUN_BITS:
        @pl.when((total & bit) != 0)
        def _(bit=bit):
            make_copy(bit).wait()


def _dispatch_kernel(tot_ref, dstg_ref, u_ref, meta_ref, srcv_ref, xs_ref, sorted_ref, sems):
    i = pl.program_id(0)
    slot = i % 2

    def wait_tile(tile, slot):
        buf = sorted_ref.at[slot]
        _wait_rows(tot_ref[tile], lambda rows: pltpu.make_async_copy(
            buf.at[pl.ds(0, rows)], xs_ref.at[pl.ds(0, rows)], sems.at[slot]))

    @pl.when(i >= 2)
    def _():
        wait_tile(i - 2, slot)

    pos1, pos2 = _sorted_positions(meta_ref[...], srcv_ref[0])
    select = jnp.logical_or(_one_hot_rows(pos1), _one_hot_rows(pos2)).astype(BF16)
    sorted_ref[slot] = _dot_tn(select, u_ref[...])
    buf = sorted_ref.at[slot]

    def start(src, dst):
        pltpu.make_async_copy(buf.at[pl.ds(src, RUN_ALIGN)], xs_ref.at[pl.ds(dst, RUN_ALIGN)],
                              sems.at[slot]).start()

    _for_each_row_group(i, tot_ref, dstg_ref, start)

    @pl.when(i == N_TILES - 1)
    def _():
        wait_tile(i - 1, 1 - slot)
        wait_tile(i, slot)


def _dispatch(sched, u2, meta, srcv):
    return pl.pallas_call(
        _dispatch_kernel,
        grid_spec=pltpu.PrefetchScalarGridSpec(
            num_scalar_prefetch=2,
            grid=(N_TILES,),
            in_specs=[pl.BlockSpec((TM, D), lambda i, *_: (i, 0)),
                      pl.BlockSpec((TM, LANES), lambda i, *_: (i, 0)),
                      pl.BlockSpec((1, 1, LANES), lambda i, *_: (i, 0, 0))],
            out_specs=pl.BlockSpec(memory_space=pl.ANY),
            scratch_shapes=[pltpu.VMEM((2, SORT_ROWS, D), F32), pltpu.SemaphoreType.DMA((2,))],
        ),
        out_shape=jax.ShapeDtypeStruct((DISPATCH_ROWS, D), F32),
        compiler_params=_cparams(("arbitrary",)),
        name="moe_dispatch",
    )(*sched, u2, meta, srcv)


def _experts_kernel(start_ref, chunks_ref, xs_ref, wg_ref, wu_ref, wd_ref, ys_ref,
                    wg_bf, wu_bf, wd_bf, xbuf, ybuf, in_sems, out_sems):
    e = pl.program_id(0)
    n = chunks_ref[e]
    first = start_ref[e] // MOE_TM
    total = start_ref[N_EXPERTS - 1] // MOE_TM + chunks_ref[N_EXPERTS - 1]

    def rows(g):
        return pl.ds(pl.multiple_of(g * MOE_TM, MOE_TM), MOE_TM)

    def load(g):
        slot = g % MOE_IN_SLOTS
        return pltpu.make_async_copy(xs_ref.at[rows(g)], xbuf.at[slot], in_sems.at[slot])

    def store(g):
        slot = g % MOE_OUT_SLOTS
        return pltpu.make_async_copy(ybuf.at[slot], ys_ref.at[rows(g)], out_sems.at[slot])

    @pl.when(e == 0)
    def _():
        for g in range(MOE_AHEAD):
            @pl.when(g < total)
            def _(g=g):
                load(g).start()

    @pl.when(n > 0)
    def _():
        wg_bf[...] = wg_ref[0, 0].astype(BF16)
        wu_bf[...] = wu_ref[0, 0].astype(BF16)
        wd_bf[...] = wd_ref[0, 0].astype(BF16)

        def tile(g, carry):
            load(g).wait()

            @pl.when(g + MOE_AHEAD < total)
            def _():
                load(g + MOE_AHEAD).start()

            @pl.when(g >= MOE_OUT_SLOTS)
            def _():
                store(g - MOE_OUT_SLOTS).wait()

            x = xbuf[g % MOE_IN_SLOTS].astype(BF16)
            h = (_silu(_dot(x, wg_bf[...])) * _dot(x, wu_bf[...])).astype(BF16)
            ybuf[g % MOE_OUT_SLOTS] = _dot(h, wd_bf[...])
            store(g).start()
            return carry

        lax.fori_loop(first, first + n, tile, 0)

    @pl.when(e == N_EXPERTS - 1)
    def _():
        for back in range(MOE_OUT_SLOTS, 0, -1):
            @pl.when(total >= back)
            def _(back=back):
                store(total - back).wait()


def _experts(seg_start, seg_chunks, xs, w_gate, w_up, w_down, li):
    def weight(shape):
        return pl.BlockSpec((1, 1) + shape, lambda e, *_: (li, e, 0, 0))

    return pl.pallas_call(
        _experts_kernel,
        grid_spec=pltpu.PrefetchScalarGridSpec(
            num_scalar_prefetch=2,
            grid=(N_EXPERTS,),
            in_specs=[pl.BlockSpec(memory_space=pl.ANY),
                      weight((D, D_EXPERT)), weight((D, D_EXPERT)), weight((D_EXPERT, D))],
            out_specs=pl.BlockSpec(memory_space=pl.ANY),
            scratch_shapes=[pltpu.VMEM((D, D_EXPERT), BF16), pltpu.VMEM((D, D_EXPERT), BF16),
                            pltpu.VMEM((D_EXPERT, D), BF16),
                            pltpu.VMEM((MOE_IN_SLOTS, MOE_TM, D), F32),
                            pltpu.VMEM((MOE_OUT_SLOTS, MOE_TM, D), F32),
                            pltpu.SemaphoreType.DMA((MOE_IN_SLOTS,)),
                            pltpu.SemaphoreType.DMA((MOE_OUT_SLOTS,))],
        ),
        out_shape=jax.ShapeDtypeStruct((MOE_ROWS, D), F32),
        compiler_params=_cparams(("arbitrary",)),
        name="moe_experts",
    )(seg_start, seg_chunks, xs, w_gate, w_up, w_down)


def _combine_kernel(tot_ref, dstg_ref, ys_ref, x1_ref, meta_ref, srcv_ref, g2_ref,
                    lng_ref, lnb_ref, *rest, split, feeds_conv):
    if feeds_conv:
        (sh_ref, sc_ref, w1_ref, b1_ref), rest, w1_bf = rest[:4], rest[4:-1], rest[-1]
        _cast_weight_once(w1_ref, w1_bf)
    outs, (sorted_ref, sems) = rest[:-2], rest[-2:]
    i = pl.program_id(0)
    slot = i % 2

    def fetch(tile, slot):
        buf = sorted_ref.at[slot]

        def start(src, dst):
            pltpu.make_async_copy(ys_ref.at[pl.ds(dst, RUN_ALIGN)], buf.at[pl.ds(src, RUN_ALIGN)],
                                  sems.at[slot]).start()

        _for_each_row_group(tile, tot_ref, dstg_ref, start)

    @pl.when(i == 0)
    def _():
        sorted_ref[...] = jnp.zeros_like(sorted_ref)
        fetch(0, 0)

    @pl.when(i + 1 < N_TILES)
    def _():
        fetch(i + 1, 1 - slot)

    meta = meta_ref[...]
    pos1, pos2 = _sorted_positions(meta, srcv_ref[0])
    sel1 = _one_hot_rows(pos1).astype(BF16)
    sel2 = _one_hot_rows(pos2).astype(BF16)
    buf = sorted_ref.at[slot]
    _wait_rows(tot_ref[i], lambda rows: pltpu.make_async_copy(
        ys_ref.at[pl.ds(0, rows)], buf.at[pl.ds(0, rows)], sems.at[slot]))
    ysort = sorted_ref[slot].astype(BF16)
    f = (meta[:, META_W1:META_W1 + 1] * _dot(sel1, ysort)
         + meta[:, META_W2:META_W2 + 1] * _dot(sel2, ysort))
    y = _layer_norm(ALPHA * x1_ref[...] + g2_ref[0] * f, lng_ref[...], lnb_ref[...])
    if split:
        @pl.when(i < PROMPT_TILES)
        def _():
            outs[0][...] = y

        @pl.when(i >= PROMPT_TILES)
        def _():
            outs[1][...] = y
    else:
        outs[0][...] = y
    if feeds_conv:
        outs[-1][...] = _conv_glu(y, sh_ref, sc_ref, w1_bf, b1_ref)


def _combine(sched, ys, x1, meta, srcv, modr, ln_g, ln_b, li, split, conv_w1=None, conv_b1=None):
    feeds_conv = conv_w1 is not None
    tile = pl.BlockSpec((TM, D), lambda i, *_: (i, 0))
    if split:
        out_specs = [_prompt_tile_spec(D), _sample_tile_spec(D)]
        out_shape = [jax.ShapeDtypeStruct((T_PROMPT, D), F32), jax.ShapeDtypeStruct((T_SAMPLE, D), F32)]
    else:
        out_specs = [tile]
        out_shape = [jax.ShapeDtypeStruct((T, D), F32)]
    extra_specs, extra_args, extra_scratch = [], [], []
    if feeds_conv:
        extra_specs = [_mod_spec(li + 1, 0), _mod_spec(li + 1, 1),
                       _resident_f32_weight((D, 2 * D)), _row_spec((1, 2 * D))]
        extra_args = [modr, modr, conv_w1, conv_b1]
        extra_scratch = [pltpu.VMEM((D, 2 * D), BF16)]
        out_specs = out_specs + [tile]
        out_shape = out_shape + [jax.ShapeDtypeStruct((T, D), F32)]
    return pl.pallas_call(
        functools.partial(_combine_kernel, split=split, feeds_conv=feeds_conv),
        grid_spec=pltpu.PrefetchScalarGridSpec(
            num_scalar_prefetch=2,
            grid=(N_TILES,),
            in_specs=[pl.BlockSpec(memory_space=pl.ANY),
                      pl.BlockSpec((TM, D), lambda i, *_: (i, 0)),
                      pl.BlockSpec((TM, LANES), lambda i, *_: (i, 0)),
                      pl.BlockSpec((1, 1, LANES), lambda i, *_: (i, 0, 0)),
                      _mod_spec(li, 5), _row_spec((1, D)), _row_spec((1, D))] + extra_specs,
            out_specs=out_specs,
            scratch_shapes=[pltpu.VMEM((2, SORT_ROWS, D), F32), pltpu.SemaphoreType.DMA((2,))] + extra_scratch,
        ),
        out_shape=out_shape,
        compiler_params=_cparams(("arbitrary",)),
        name="moe_combine",
    )(*sched, ys, x1, meta, srcv, modr, ln_g, ln_b, *extra_args)


def _moe_schedule(tile_counts):
    n = (tile_counts + RUN_ALIGN - 1) // RUN_ALIGN * RUN_ALIGN
    src = jnp.cumsum(n, axis=1) - n
    per_expert = jnp.sum(n, axis=0)
    seg = (per_expert + MOE_TM - 1) // MOE_TM * MOE_TM
    seg_start = jnp.cumsum(seg) - seg
    dst = seg_start[None, :] + jnp.cumsum(n, axis=0) - n
    g_row = jnp.arange(SORT_GROUPS, dtype=jnp.int32) * RUN_ALIGN
    in_run = jnp.logical_and(src[:, None, :] <= g_row[None, :, None],
                             g_row[None, :, None] < (src + n)[:, None, :])
    dst_g = g_row[None, :] + jnp.sum(jnp.where(in_run, (dst - src)[:, None, :], 0), axis=2)
    step_rows = RUN_ALIGN * GROUP_UNROLL
    rows = jnp.sum(n, axis=1)
    issued = (rows + step_rows - 1) // step_rows * step_rows
    real = g_row[None, :] < rows[:, None]
    spill = (MOE_ROWS + jnp.arange(N_TILES, dtype=jnp.int32)[:, None] * step_rows
             + g_row[None, :] % step_rows)
    scatter_runs = (issued, jnp.where(real, dst_g, spill).reshape(-1))
    gather_runs = (issued, jnp.where(real, dst_g, dst_g[:, :1]).reshape(-1))
    srcv = jnp.pad(src.astype(F32), ((0, 0), (ROUTER_LANE0, LANES - ROUTER_LANE0 - N_EXPERTS)))
    return scatter_runs, gather_runs, srcv.reshape(N_TILES, 1, LANES), (seg_start, seg // MOE_TM)


def _moe(x1, u2, meta, cnt, modr, w_gate, w_up, w_down, ln_g, ln_b, li, split, **next_conv):
    tile_counts = cnt[:, 0, ROUTER_LANE0:ROUTER_LANE0 + N_EXPERTS].astype(jnp.int32)
    scatter_runs, gather_runs, srcv, (seg_start, seg_chunks) = _moe_schedule(tile_counts)
    xs = _dispatch(scatter_runs, u2, meta, srcv)
    ys = _experts(seg_start, seg_chunks, xs, w_gate, w_up, w_down, li)
    return _combine(gather_runs, ys, x1, meta, srcv, modr, ln_g, ln_b, li, split, **next_conv)


def _router_slab(wg, bg, we, be):
    w = jnp.concatenate([wg, we.transpose(1, 0, 2).reshape(D, N_EXPERTS)], axis=1)
    b = jnp.concatenate([bg, be.reshape(N_EXPERTS)])
    pad = LANES - w.shape[1]
    return jnp.pad(w, ((0, 0), (0, pad))), jnp.pad(b, (0, pad)).reshape(1, LANES)


def kernel(x_prompt, x_sample, cache_diff_k, cache_diff_v, state_ret_fwd, state_ret_bwd, c, c_ctx, mod_w, mod_b, ln1_g, ln1_b, ln2_g, ln2_b, mix_w_in, mix_w_out, ret_decay_fwd, ret_decay_bwd, diff_lq1, diff_lk1, diff_lq2, diff_lk2, diff_subln_g, conv_w1, conv_b1, conv_dw, conv_dw_b, conv_ln_g, conv_ln_b, conv_w2, conv_b2, router_g_w, router_g_b, router_e_w, router_e_b, moe_w_gate, moe_w_up, moe_w_down):
    xp = x_prompt.reshape(T_PROMPT, D)
    xs = x_sample.reshape(T_SAMPLE, D)
    cond = jnp.concatenate([c_ctx[None, :], c, jnp.zeros((MOD_ROWS - 1 - DEC_BATCH, D), F32)], axis=0)
    modr = _mod_vectors(cond, mod_w, mod_b).reshape(DEPTH * MOD_ROWS * 6, 1, D)
    cos, sin_signed = _rope_tables()

    def row(v):
        return v.reshape(1, -1)

    x = None
    caches = None
    for li in range(DEPTH):
        wr, br = _router_slab(router_g_w[li], router_g_b[li], router_e_w[li], router_e_b[li])
        if li % 2 == 0:
            assert li == 0, "the even mixer reads the kernel inputs directly"
            e = li // 2
            lam_init = 0.8 - 0.6 * math.exp(-0.3 * li)
            proj, ck, cv = _in_proj(xp, xs, modr, mix_w_in[e], li)
            dec = jnp.concatenate([ret_decay_fwd[e], ret_decay_bwd[e]])
            r_p, sf, sb = _retention(proj, dec, BATCH, SEQ, 0, HEADS, emit_state=True, chunk_len=SEQ)
            (r_s,) = _retention(proj, dec, DEC_BATCH, DEC_SEQ, T_PROMPT // DEC_SEQ, 2,
                                s0f=state_ret_fwd, s0b=state_ret_bwd, e=e, chunk_len=2 * RET_CHUNK)
            lams = (row(diff_lq1[e]), row(diff_lk1[e]), row(diff_lq2[e]), row(diff_lk2[e]),
                    row(diff_subln_g[e]))
            o_p = _attn_prompt(proj, *lams, lam_init)
            o_s = _attn_sample(proj, cache_diff_k, cache_diff_v, cos, sin_signed, *lams, lam_init, e)
            x1, u2, meta, cnt = _out_proj_tail(r_p, r_s, o_p, o_s, mix_w_out[e], xp, xs, modr,
                                               row(ln1_g[li]), row(ln1_b[li]), wr, br, li)
            caches = (ck, cv, sf, sb)
        else:
            o = li // 2
            x1, u2, meta, cnt = _conv_tail(glu, conv_dw[o], row(conv_dw_b[o]), row(conv_ln_g[o]),
                                           row(conv_ln_b[o]), conv_w2[o], row(conv_b2[o]),
                                           x, modr, row(ln1_g[li]), row(ln1_b[li]), wr, br, li)
        next_conv = {}
        if li + 1 < DEPTH and (li + 1) % 2 == 1:
            next_conv = dict(conv_w1=conv_w1[(li + 1) // 2], conv_b1=row(conv_b1[(li + 1) // 2]))
        outs = _moe(x1, u2, meta, cnt, modr, moe_w_gate, moe_w_up, moe_w_down,
                    row(ln2_g[li]), row(ln2_b[li]), li, split=(li == DEPTH - 1), **next_conv)
        x, glu = outs[0], outs[-1]

    y_prompt = outs[0].reshape(BATCH, SEQ, D)
    y_sample = outs[1].reshape(DEC_BATCH, DEC_SEQ, D)
    return (y_prompt, y_sample) + caches
```

```python
import functools
import math

import numpy as np
import jax
import jax.numpy as jnp
from jax import lax
from jax.experimental import pallas as pl
from jax.experimental.pallas import tpu as pltpu

F32 = jnp.float32
BF16 = jnp.bfloat16

D = 1024
BATCH = 16
SEQ = 256
DEPTH = 2
DEC_BATCH = 2
DEC_SEQ = 2048
PAST_LEN = 512
GRID_W = 64
HEADS = 4
HEAD_W = 128
RET_CHUNK = 128
DIFF_DK = 64
ROPE_THETA = 10000.0
IN_W = 7 * HEADS * HEAD_W
CONV_K = 31
CONV_PAD = CONV_K // 2
N_GROUPS = 4
EXPERTS_PER_GROUP = 8
N_EXPERTS = N_GROUPS * EXPERTS_PER_GROUP
D_EXPERT = 512
ALPHA = (2.0 * DEPTH) ** 0.25
LN_EPS = 1e-5
GN_EPS = 1e-6

T_PROMPT = BATCH * SEQ
T_SAMPLE = DEC_BATCH * DEC_SEQ
T = T_PROMPT + T_SAMPLE
TM = 256
N_TILES = T // TM
PROMPT_TILES = T_PROMPT // TM
SAMPLE_TILES_PER_SEQ = DEC_SEQ // TM
MOD_ROWS = 8
MOE_TM = 256
LANES = 128
SUBLANES = 8
RUN_ALIGN = SUBLANES
SORT_ROWS = -(-(2 * TM + N_EXPERTS * (RUN_ALIGN - 1)) // TM) * TM
RUN_BITS = tuple(1 << b for b in range((2 * TM).bit_length() - 1, RUN_ALIGN.bit_length() - 2, -1))
MOE_MAX_TILES = -(-(2 * T + N_TILES * N_EXPERTS * (RUN_ALIGN - 1) + N_EXPERTS * (MOE_TM - RUN_ALIGN)) // MOE_TM)
MOE_ROWS = MOE_MAX_TILES * MOE_TM
SORT_GROUPS = SORT_ROWS // RUN_ALIGN
GROUP_UNROLL = 8
DISPATCH_ROWS = MOE_ROWS + N_TILES * RUN_ALIGN * GROUP_UNROLL
MOE_AHEAD = 8
MOE_IN_SLOTS = MOE_AHEAD + 1
MOE_OUT_SLOTS = 4
ROUTER_LANE0 = N_GROUPS
ROPE_GROUP = DIFF_DK // 2
ROPE_HALF = ROPE_GROUP // 2
V7X_VMEM_BYTES = 64 * 1024 * 1024
VMEM_LIMIT = V7X_VMEM_BYTES - 12 * 1024 * 1024


def _cparams(sem):
    return pltpu.CompilerParams(dimension_semantics=sem, vmem_limit_bytes=VMEM_LIMIT)


def _tile_cond_row(i, tm):
    return jnp.where(i < T_PROMPT // tm, 0, 1 + (i - T_PROMPT // tm) // (DEC_SEQ // tm))


def _mod_spec(li, k, tm=TM):
    return pl.BlockSpec((1, 1, D), lambda i, *_: ((li * MOD_ROWS + _tile_cond_row(i, tm)) * 6 + k, 0, 0))


def _row_spec(shape):
    return pl.BlockSpec(shape, lambda i, *_: (0,) * len(shape))


def _resident_f32_weight(shape):
    return pl.BlockSpec(shape, lambda i, *_: (0,) * len(shape), pipeline_mode=pl.Buffered(1))


def _cast_weight_once(w_ref, w_bf):
    @pl.when(pl.program_id(0) == 0)
    def _():
        w_bf[...] = w_ref[...].astype(BF16)


def _layer_norm(x, g, b):
    mu = jnp.mean(x, axis=-1, keepdims=True)
    xc = x - mu
    var = jnp.mean(xc * xc, axis=-1, keepdims=True)
    return xc * lax.rsqrt(var + LN_EPS) * g + b


def _silu(x):
    return x * jax.nn.sigmoid(x)


def _dot(a, b):
    return jnp.dot(a, b, preferred_element_type=F32)


def _dot_nt(a, b):
    return lax.dot_general(a, b, (((1,), (1,)), ((), ())), preferred_element_type=F32)


def _dot_tn(a, b):
    return lax.dot_general(a, b, (((0,), (0,)), ((), ())), preferred_element_type=F32)


MOD_TN = 2048
MOD_USED_ROWS = 1 + DEC_BATCH


def _mod_kernel(cond_t_ref, w_ref, b_ref, o_ref):
    s = _silu(cond_t_ref[...])
    w = w_ref[0]
    o_ref[0] = jnp.zeros((MOD_ROWS, MOD_TN), F32) + b_ref[0]
    for r in range(MOD_USED_ROWS):
        o_ref[0, r:r + 1, :] = jnp.sum(w * s[:, r:r + 1], axis=0, keepdims=True) + b_ref[0]


def _mod_vectors(cond, mod_w, mod_b):
    return pl.pallas_call(
        _mod_kernel,
        grid=(DEPTH, 6 * D // MOD_TN),
        in_specs=[
            pl.BlockSpec((D, MOD_ROWS), lambda l, j: (0, 0)),
            pl.BlockSpec((1, D, MOD_TN), lambda l, j: (l, 0, j)),
            pl.BlockSpec((1, 1, MOD_TN), lambda l, j: (l, 0, j)),
        ],
        out_specs=pl.BlockSpec((1, MOD_ROWS, MOD_TN), lambda l, j: (l, 0, j)),
        out_shape=jax.ShapeDtypeStruct((DEPTH, MOD_ROWS, 6 * D), F32),
        compiler_params=_cparams(("arbitrary", "arbitrary")),
        name="mod_vectors",
    )(cond.T, mod_w, mod_b.reshape(DEPTH, 1, 6 * D))


def _prompt_tile_spec(width, tm=TM):
    return pl.BlockSpec((tm, width), lambda i, *_: (jnp.minimum(i, T_PROMPT // tm - 1), 0))


def _sample_tile_spec(width, tm=TM):
    return pl.BlockSpec((tm, width), lambda i, *_: (jnp.maximum(i - T_PROMPT // tm, 0), 0))


def _pick_tile(prompt_ref, sample_ref, tm=TM):
    return jnp.where(pl.program_id(0) < T_PROMPT // tm, prompt_ref[...], sample_ref[...])


IN_TM = 512
IN_SEQS = IN_TM // SEQ


def _in_proj_kernel(xp_ref, xs_ref, sh_ref, sc_ref, w_ref, o_ref, ck_ref, cv_ref, w_bf):
    _cast_weight_once(w_ref, w_bf)
    u = _pick_tile(xp_ref, xs_ref, IN_TM) * (1.0 + sc_ref[0]) + sh_ref[0]
    proj = _dot(u.astype(BF16), w_bf[...])
    o_ref[...] = proj.astype(BF16)

    @pl.when(pl.program_id(0) < T_PROMPT // IN_TM)
    def _():
        for s in range(IN_SEQS):
            rows = slice(s * SEQ, (s + 1) * SEQ)
            for h in range(HEADS):
                ck_ref[s, 0, h] = proj[rows, (COL_KD + h) * HEAD_W:(COL_KD + h + 1) * HEAD_W]
                cv_ref[s, 0, h] = proj[rows, (COL_VD + h) * HEAD_W:(COL_VD + h + 1) * HEAD_W]


def _in_proj(x_prompt, x_sample, modr, w_in, li):
    cache_spec = pl.BlockSpec((IN_SEQS, 1, HEADS, SEQ, HEAD_W),
                              lambda i: (jnp.minimum(i, T_PROMPT // IN_TM - 1), 0, 0, 0, 0))
    cache_shape = jax.ShapeDtypeStruct((BATCH, 1, HEADS, SEQ, HEAD_W), F32)
    return pl.pallas_call(
        _in_proj_kernel,
        grid=(T // IN_TM,),
        in_specs=[
            _prompt_tile_spec(D, IN_TM), _sample_tile_spec(D, IN_TM),
            _mod_spec(li, 0, IN_TM),
            _mod_spec(li, 1, IN_TM),
            _resident_f32_weight((D, IN_W)),
        ],
        out_specs=[pl.BlockSpec((IN_TM, IN_W), lambda i: (i, 0)), cache_spec, cache_spec],
        out_shape=[jax.ShapeDtypeStruct((T, IN_W), BF16), cache_shape, cache_shape],
        scratch_shapes=[pltpu.VMEM((D, IN_W), BF16)],
        compiler_params=_cparams(("arbitrary",)),
        name="in_proj",
    )(x_prompt, x_sample, modr, modr, w_in)


COL_QR, COL_KR, COL_VR, COL_GR, COL_QD, COL_KD, COL_VD = (k * HEADS for k in range(7))


def _retention_kernel(dec_ref, q_ref, k_ref, v_ref, g_ref, *rest, chunk_len, n_chunks, n_heads, has_state,
                      emit_state):
    rest = list(rest)
    if has_state:
        s0f_ref, s0b_ref = rest[:2]
        rest = rest[2:]
    r_ref = rest[0]
    rest = rest[1:]
    if emit_state:
        sf_ref, sb_ref = rest[:2]
        rest = rest[2:]
    of_ref = rest[0]

    head0 = pl.program_id(1) * n_heads
    C = chunk_len
    ii = lax.broadcasted_iota(jnp.int32, (C, C), 0)
    jj = lax.broadcasted_iota(jnp.int32, (C, C), 1)
    rel = (ii - jj).astype(F32)
    idx = lax.broadcasted_iota(jnp.int32, (C, 1), 0).astype(F32)
    k_scale = HEAD_W ** -0.5

    def chunk(ref, c, h):
        return ref[c * C:(c + 1) * C, h * HEAD_W:(h + 1) * HEAD_W].astype(F32)

    def decays(direction, h):
        lg = -jnp.exp(jnp.full((1, 1), dec_ref[direction * HEADS + head0 + h], F32))
        if direction == 0:
            inner = jnp.where(rel >= 0, jnp.exp(jnp.maximum(rel, 0.0) * lg), 0.0)
            return inner, jnp.exp((idx + 1.0) * lg), jnp.exp((C - 1.0 - idx) * lg), jnp.exp(C * lg)
        inner = jnp.where(rel <= 0, jnp.exp(jnp.maximum(-rel, 0.0) * lg), 0.0)
        return inner, jnp.exp((C - idx) * lg), jnp.exp(idx * lg), jnp.exp(C * lg)

    def run(direction):
        dec = [decays(direction, h) for h in range(n_heads)]
        if has_state:
            s0_ref = s0f_ref if direction == 0 else s0b_ref
            states = [s0_ref[0, 0, h] for h in range(n_heads)]
        else:
            states = [jnp.zeros((HEAD_W, HEAD_W), F32) for _ in range(n_heads)]
        order = range(n_chunks) if direction == 0 else range(n_chunks - 1, -1, -1)
        for c in order:
            rows = slice(c * C, (c + 1) * C)
            for h in range(n_heads):
                inner, q_decay, k_decay, chunk_decay = dec[h]
                cols = slice(h * HEAD_W, (h + 1) * HEAD_W)
                s = states[h]
                qc = chunk(q_ref, c, h)
                kc = chunk(k_ref, c, h) * k_scale
                vc = chunk(v_ref, c, h).astype(BF16)
                scores = _dot_nt(qc.astype(BF16), kc.astype(BF16)) * inner
                o = _dot(scores.astype(BF16), vc) + _dot((qc * q_decay).astype(BF16), s.astype(BF16))
                states[h] = s * chunk_decay + _dot_tn((kc * k_decay).astype(BF16), vc)
                if direction == 0:
                    of_ref[rows, cols] = o
                else:
                    r = of_ref[rows, cols] + o
                    mu = jnp.mean(r, axis=-1, keepdims=True)
                    rc = r - mu
                    var = jnp.mean(rc * rc, axis=-1, keepdims=True)
                    rn = rc * lax.rsqrt(var + GN_EPS)
                    r_ref[rows, cols] = _silu(chunk(g_ref, c, h)) * rn
        return states

    sf = run(0)
    sb = run(1)
    if emit_state:
        for h in range(n_heads):
            sf_ref[0, 0, h] = sf[h]
            sb_ref[0, 0, h] = sb[h]


def _retention(proj, dec, n_seq, seq_len, row_block0, n_heads, s0f=None, s0b=None, e=0, emit_state=False,
               chunk_len=RET_CHUNK):
    has_state = s0f is not None
    width = n_heads * HEAD_W

    def col(base):
        return pl.BlockSpec((seq_len, width), lambda b, h, *_: (row_block0 + b, base // n_heads + h))

    state_spec = pl.BlockSpec((1, 1, n_heads, HEAD_W, HEAD_W), lambda b, h, *_: (b, e, h, 0, 0))
    in_specs = [pl.BlockSpec(memory_space=pltpu.SMEM), col(COL_QR), col(COL_KR), col(COL_VR), col(COL_GR)]
    args = [dec, proj, proj, proj, proj]
    if has_state:
        in_specs += [state_spec, state_spec]
        args += [s0f, s0b]
    out_specs = [pl.BlockSpec((seq_len, width), lambda b, h, *_: (b, h))]
    out_shape = [jax.ShapeDtypeStruct((n_seq * seq_len, HEADS * HEAD_W), F32)]
    if emit_state:
        st = pl.BlockSpec((1, 1, n_heads, HEAD_W, HEAD_W), lambda b, h, *_: (b, 0, h, 0, 0))
        out_specs += [st, st]
        out_shape += [jax.ShapeDtypeStruct((n_seq, 1, HEADS, HEAD_W, HEAD_W), F32)] * 2
    return pl.pallas_call(
        functools.partial(_retention_kernel, chunk_len=chunk_len, n_chunks=seq_len // chunk_len, n_heads=n_heads,
                          has_state=has_state, emit_state=emit_state),
        grid=(n_seq, HEADS // n_heads),
        in_specs=in_specs,
        out_specs=out_specs,
        out_shape=out_shape,
        scratch_shapes=[pltpu.VMEM((seq_len, width), F32)],
        compiler_params=_cparams(("arbitrary", "arbitrary")),
        name=f"retention_{seq_len}",
    )(*args)


def _diff_lambda(lq1_ref, lk1_ref, lq2_ref, lk2_ref, lam_init):
    a = jnp.sum(lq1_ref[...] * lk1_ref[...], axis=-1, keepdims=True)
    b = jnp.sum(lq2_ref[...] * lk2_ref[...], axis=-1, keepdims=True)
    return jnp.exp(a) - jnp.exp(b) + lam_init


LOG2E = 1.4426950408889634


def _diff_attend(q, k, v, lam, subln_g, lam_init):
    lane = lax.broadcasted_iota(jnp.int32, q.shape, 1)
    q1 = jnp.where(lane < DIFF_DK, q, 0.0).astype(BF16)
    q2 = jnp.where(lane >= DIFF_DK, q, 0.0).astype(BF16)

    def softmax_times_v(qz):
        s = _dot_nt(qz, k)
        p = jnp.exp2(s - jnp.max(s, axis=-1, keepdims=True))
        return _dot(p.astype(BF16), v) * (1.0 / jnp.sum(p, axis=-1, keepdims=True))

    o = softmax_times_v(q1) - lam * softmax_times_v(q2)
    o = o * lax.rsqrt(jnp.mean(o * o, axis=-1, keepdims=True) + LN_EPS)
    return o * subln_g * (1.0 - lam_init)


def _attn_prompt_kernel(q_ref, k_ref, v_ref, lq1, lk1, lq2, lk2, g_ref, o_ref, *, lam_init):
    lam = _diff_lambda(lq1, lk1, lq2, lk2, lam_init)
    scale = DIFF_DK ** -0.5 * LOG2E
    for h in range(HEADS):
        sl = slice(h * HEAD_W, (h + 1) * HEAD_W)
        o_ref[:, sl] = _diff_attend(q_ref[:, sl].astype(F32) * scale, k_ref[:, sl], v_ref[:, sl],
                                    lam, g_ref[...], lam_init)


def _attn_prompt(proj, lq1, lk1, lq2, lk2, subln_g, lam_init):
    W = HEADS * HEAD_W

    def slab(base):
        return pl.BlockSpec((SEQ, W), lambda b: (b, base // HEADS))

    small = _row_spec((1, DIFF_DK))
    return pl.pallas_call(
        functools.partial(_attn_prompt_kernel, lam_init=lam_init),
        grid=(BATCH,),
        in_specs=[slab(COL_QD), slab(COL_KD), slab(COL_VD), small, small, small, small,
                  _row_spec((1, HEAD_W))],
        out_specs=pl.BlockSpec((SEQ, W), lambda b: (b, 0)),
        out_shape=jax.ShapeDtypeStruct((T_PROMPT, W), F32),
        compiler_params=_cparams(("arbitrary",)),
        name="diff_attn_prompt",
    )(proj, proj, proj, lq1, lk1, lq2, lk2, subln_g)


def _rope(x, cos, sin_signed):
    lane = lax.broadcasted_iota(jnp.int32, x.shape, 1)
    partner = jnp.where((lane % ROPE_GROUP) < ROPE_HALF,
                        pltpu.roll(x, LANES - ROPE_HALF, 1), pltpu.roll(x, ROPE_HALF, 1))
    return x * cos + partner * sin_signed


def _attn_sample_kernel(q_ref, k_ref, v_ref, ck_ref, cv_ref, cosq_ref, sinq_ref, cos_ref, sin_ref,
                        lq1, lk1, lq2, lk2, g_ref, o_ref, kbuf, vbuf, *, lam_init):
    @pl.when(pl.program_id(2) == 0)
    def _():
        kbuf[0:DEC_SEQ, :] = _rope(k_ref[...].astype(F32), cos_ref[...], sin_ref[...]).astype(BF16)
        kbuf[DEC_SEQ:, :] = ck_ref[0, 0, 0].astype(BF16)
        vbuf[0:DEC_SEQ, :] = v_ref[...]
        vbuf[DEC_SEQ:, :] = cv_ref[0, 0, 0].astype(BF16)

    lam = _diff_lambda(lq1, lk1, lq2, lk2, lam_init)
    q = _rope(q_ref[...].astype(F32), cosq_ref[...], sinq_ref[...]) * (DIFF_DK ** -0.5 * LOG2E)
    o_ref[...] = _diff_attend(q, kbuf[...], vbuf[...], lam, g_ref[...], lam_init)


ATTN_TQ = 256


def _attn_sample(proj, cache_k, cache_v, cos, sin_signed, lq1, lk1, lq2, lk2, subln_g, lam_init, e):
    nq = DEC_SEQ // ATTN_TQ
    row0_q = T_PROMPT // ATTN_TQ
    row0_kv = T_PROMPT // DEC_SEQ
    small = pl.BlockSpec((1, DIFF_DK), lambda b, h, t: (0, 0))
    cache = pl.BlockSpec((1, 1, 1, PAST_LEN, HEAD_W), lambda b, h, t: (b, e, h, 0, 0))
    table_q = pl.BlockSpec((ATTN_TQ, HEAD_W), lambda b, h, t: (t, 0))
    table = pl.BlockSpec((DEC_SEQ, HEAD_W), lambda b, h, t: (0, 0))
    return pl.pallas_call(
        functools.partial(_attn_sample_kernel, lam_init=lam_init),
        grid=(DEC_BATCH, HEADS, nq),
        in_specs=[
            pl.BlockSpec((ATTN_TQ, HEAD_W), lambda b, h, t: (row0_q + b * nq + t, COL_QD + h)),
            pl.BlockSpec((DEC_SEQ, HEAD_W), lambda b, h, t: (row0_kv + b, COL_KD + h)),
            pl.BlockSpec((DEC_SEQ, HEAD_W), lambda b, h, t: (row0_kv + b, COL_VD + h)),
            cache, cache, table_q, table_q, table, table,
            small, small, small, small,
            pl.BlockSpec((1, HEAD_W), lambda b, h, t: (0, 0)),
        ],
        out_specs=pl.BlockSpec((ATTN_TQ, HEAD_W), lambda b, h, t: (b * nq + t, h)),
        out_shape=jax.ShapeDtypeStruct((T_SAMPLE, HEADS * HEAD_W), F32),
        scratch_shapes=[pltpu.VMEM((DEC_SEQ + PAST_LEN, HEAD_W), BF16),
                        pltpu.VMEM((DEC_SEQ + PAST_LEN, HEAD_W), BF16)],
        compiler_params=_cparams(("arbitrary", "arbitrary", "arbitrary")),
        name="diff_attn_sample",
    )(proj, proj, proj, cache_k, cache_v, cos, sin_signed, cos, sin_signed,
      lq1, lk1, lq2, lk2, subln_g)


def _rope_tables():
    t = np.arange(DEC_SEQ)
    row, colp = t // GRID_W, t % GRID_W
    lane = np.arange(LANES)
    pos = np.where(((lane // ROPE_GROUP) % 2 == 0)[None, :], row[:, None], colp[:, None]).astype(np.float64)
    half = ROPE_HALF
    inv = (np.float32(ROPE_THETA) ** (-(np.arange(half, dtype=np.float32)) / np.float32(half))).astype(np.float32)
    ang = pos.astype(np.float32) * inv[lane % half][None, :]
    cos = np.cos(ang.astype(np.float64)).astype(np.float32)
    sin = np.sin(ang.astype(np.float64)).astype(np.float32)
    sign = np.where((lane % ROPE_GROUP) < half, -1.0, 1.0).astype(np.float32)[None, :]
    return jnp.asarray(cos), jnp.asarray(sin * sign)


def _split_bf16(a):
    hi = a.astype(BF16)
    return hi, (a - hi.astype(F32)).astype(BF16)


def _mixer_tail(out, x, g1_ref, sc2_ref, sh2_ref, lng_ref, lnb_ref, wr_ref, br_ref,
                x1_ref, u2_ref, meta_ref, cnt_ref, wr_bf, before_bf):
    @pl.when(pl.program_id(0) == 0)
    def _():
        w_hi, w_lo = _split_bf16(wr_ref[...])
        wr_bf[:, 0:LANES] = w_hi
        wr_bf[:, LANES:] = w_lo
        r_i = lax.broadcasted_iota(jnp.int32, (TM, TM), 0)
        c_i = lax.broadcasted_iota(jnp.int32, (TM, TM), 1)
        before_bf[...] = (c_i < r_i).astype(BF16)

    x1 = _layer_norm(ALPHA * x + g1_ref[0] * out, lng_ref[...], lnb_ref[...])
    x1_ref[...] = x1
    u2 = x1 * (1.0 + sc2_ref[0]) + sh2_ref[0]
    u2_ref[...] = u2.astype(BF16)

    u_hi, u_lo = _split_bf16(u2)
    hi = _dot(u_hi, wr_bf[...])
    logits = hi[:, 0:LANES] + (hi[:, LANES:] + _dot(u_lo, wr_bf[:, 0:LANES])) + br_ref[...]
    lane = lax.broadcasted_iota(jnp.int32, logits.shape, 1).astype(F32)
    neg = jnp.float32(-jnp.inf)
    is_g = lane < N_GROUPS
    gl = jnp.where(is_g, logits, neg)
    gmax = jnp.max(gl, axis=-1, keepdims=True)
    gsel = jnp.min(jnp.where(gl == gmax, lane, float(LANES)), axis=-1, keepdims=True)
    p_g = 1.0 / jnp.sum(jnp.where(is_g, jnp.exp(gl - gmax), 0.0), axis=-1, keepdims=True)
    lo = ROUTER_LANE0 + gsel * EXPERTS_PER_GROUP
    el = jnp.where((lane >= lo) & (lane < lo + EXPERTS_PER_GROUP), logits, neg)
    v1 = jnp.max(el, axis=-1, keepdims=True)
    i1 = jnp.min(jnp.where(el == v1, lane, float(LANES)), axis=-1, keepdims=True)
    el2 = jnp.where(lane == i1, neg, el)
    v2 = jnp.max(el2, axis=-1, keepdims=True)
    i2 = jnp.min(jnp.where(el2 == v2, lane, float(LANES)), axis=-1, keepdims=True)
    t = jnp.exp(v2 - v1)
    w1 = p_g / (1.0 + t)
    w2 = w1 * t

    oh1 = (lane == i1).astype(F32)
    oh2 = (lane == i2).astype(F32)
    oh = oh1 + oh2
    earlier = _dot(before_bf[...], oh.astype(BF16))
    rank1 = jnp.sum(earlier * oh1, axis=-1, keepdims=True)
    rank2 = jnp.sum(earlier * oh2, axis=-1, keepdims=True)
    cnt_ref[0] = jnp.sum(oh, axis=0, keepdims=True)
    cols = (i1, i2, w1, w2, rank1, rank2)
    meta = jnp.zeros_like(logits)
    for k, col in enumerate(cols):
        meta = jnp.where(lane == k, col, meta)
    meta_ref[...] = meta


META_E1, META_E2, META_W1, META_W2, META_RANK1, META_RANK2 = range(6)

def _tail_scratch():
    return [pltpu.VMEM((D, 2 * LANES), BF16), pltpu.VMEM((TM, TM), BF16)]


_TAIL_OUT_SHAPES = [
    jax.ShapeDtypeStruct((T, D), F32),
    jax.ShapeDtypeStruct((T, D), BF16),
    jax.ShapeDtypeStruct((T, LANES), F32),
    jax.ShapeDtypeStruct((N_TILES, 1, LANES), F32),
]


def _tail_out_specs():
    return [
        pl.BlockSpec((TM, D), lambda i: (i, 0)),
        pl.BlockSpec((TM, D), lambda i: (i, 0)),
        pl.BlockSpec((TM, LANES), lambda i: (i, 0)),
        pl.BlockSpec((1, 1, LANES), lambda i: (i, 0, 0)),
    ]


def _tail_in_specs(li):
    return [
        _mod_spec(li, 2), _mod_spec(li, 4), _mod_spec(li, 3),
        _row_spec((1, D)), _row_spec((1, D)),
        _row_spec((D, LANES)), _row_spec((1, LANES)),
    ]


def _out_proj_kernel(rp_ref, rs_ref, op_ref, os_ref, w_ref, xp_ref, xs_ref, *rest):
    tail_args, w_bf = rest[:-3] + rest[-2:], rest[-3]
    _cast_weight_once(w_ref, w_bf)
    half = HEADS * HEAD_W
    r = _pick_tile(rp_ref, rs_ref).astype(BF16)
    o = _pick_tile(op_ref, os_ref).astype(BF16)
    out = _dot(r, w_bf[0:half, :]) + _dot(o, w_bf[half:, :])
    _mixer_tail(out, _pick_tile(xp_ref, xs_ref), *tail_args)


def _out_proj_tail(r_p, r_s, o_p, o_s, w_out, x_prompt, x_sample, modr, ln_g, ln_b, wr, br, li):
    half = HEADS * HEAD_W
    return pl.pallas_call(
        _out_proj_kernel,
        grid=(N_TILES,),
        in_specs=[_prompt_tile_spec(half), _sample_tile_spec(half),
                  _prompt_tile_spec(half), _sample_tile_spec(half),
                  _resident_f32_weight((2 * half, D)),
                  _prompt_tile_spec(D), _sample_tile_spec(D)] + _tail_in_specs(li),
        out_specs=_tail_out_specs(),
        out_shape=_TAIL_OUT_SHAPES,
        scratch_shapes=[pltpu.VMEM((2 * half, D), BF16)] + _tail_scratch(),
        compiler_params=_cparams(("arbitrary",)),
        name="out_proj_tail",
    )(r_p, r_s, o_p, o_s, w_out, x_prompt, x_sample, modr, modr, modr, ln_g, ln_b, wr, br)


def _conv_glu(y, sh_ref, sc_ref, w_bf, b_ref):
    u = y * (1.0 + sc_ref[0]) + sh_ref[0]
    h = _dot(u.astype(BF16), w_bf[...]) + b_ref[...]
    return h[:, :D] * jax.nn.sigmoid(h[:, D:])


HALO = 16
CONV_ROWS = 64
CONV_COLS = 128


def _depthwise_conv(hp, dw_ref, conv):
    base = HALO - CONV_PAD
    for cb in range(D // CONV_COLS):
        cs = slice(cb * CONV_COLS, (cb + 1) * CONV_COLS)
        for rb in range(TM // CONV_ROWS):
            r0 = rb * CONV_ROWS
            acc = None
            for shift in range(SUBLANES):
                part = None
                for tap in range(CONV_K):
                    off = base + tap
                    if off % SUBLANES != shift:
                        continue
                    a0 = r0 + off - shift
                    term = hp[a0:a0 + CONV_ROWS + SUBLANES, cs] * dw_ref[tap:tap + 1, cs]
                    part = term if part is None else part + term
                part = part[shift:shift + CONV_ROWS, :]
                acc = part if acc is None else acc + part
            conv[r0:r0 + CONV_ROWS, cs] = acc


def _conv_tail_kernel(cur_ref, prev_ref, next_ref, dw_ref, dwb_ref, cg_ref, cb_ref, w2_ref, b2_ref,
                      x_ref, *rest):
    tail_args, (hp, conv, w2_bf) = rest[:-5] + rest[-2:], rest[-5:-2]
    _cast_weight_once(w2_ref, w2_bf)
    i = pl.program_id(0)
    k = (i - PROMPT_TILES) % SAMPLE_TILES_PER_SEQ
    in_sample = i >= PROMPT_TILES
    left_ok = jnp.logical_and(in_sample, k != 0)
    right_ok = jnp.logical_and(in_sample, k != SAMPLE_TILES_PER_SEQ - 1)
    hp[0:HALO, :] = jnp.where(left_ok, prev_ref[...], 0.0)
    hp[HALO:HALO + TM, :] = cur_ref[...]
    hp[HALO + TM:HALO + TM + HALO, :] = jnp.where(right_ok, next_ref[...], 0.0)
    _depthwise_conv(hp, dw_ref, conv)
    hc = _silu(_layer_norm(conv[...] + dwb_ref[...], cg_ref[...], cb_ref[...]))
    out = _dot(hc.astype(BF16), w2_bf[...]) + b2_ref[...]
    _mixer_tail(out, x_ref[...], *tail_args)


def _conv_tail(glu, dw, dwb, cg, cb, w2, b2, x, modr, ln_g, ln_b, wr, br, li):
    per = TM // HALO
    last = T // HALO - 1
    return pl.pallas_call(
        _conv_tail_kernel,
        grid=(N_TILES,),
        in_specs=[pl.BlockSpec((TM, D), lambda i: (i, 0)),
                  pl.BlockSpec((HALO, D), lambda i: (jnp.maximum(i * per - 1, 0), 0)),
                  pl.BlockSpec((HALO, D), lambda i: (jnp.minimum((i + 1) * per, last), 0)),
                  _row_spec((CONV_K, D)), _row_spec((1, D)), _row_spec((1, D)), _row_spec((1, D)),
                  _resident_f32_weight((D, D)), _row_spec((1, D)),
                  pl.BlockSpec((TM, D), lambda i: (i, 0))] + _tail_in_specs(li),
        out_specs=_tail_out_specs(),
        out_shape=_TAIL_OUT_SHAPES,
        scratch_shapes=[pltpu.VMEM((TM + 2 * HALO, D), F32), pltpu.VMEM((TM, D), F32),
                        pltpu.VMEM((D, D), BF16)] + _tail_scratch(),
        compiler_params=_cparams(("arbitrary",)),
        name="conv_tail",
    )(glu, glu, glu, dw, dwb, cg, cb, w2, b2, x, modr, modr, modr, ln_g, ln_b, wr, br)


def _sorted_positions(meta, srcv):
    lane = lax.broadcasted_iota(jnp.int32, meta.shape, 1).astype(F32)

    def pos(e_col, r_col):
        start = jnp.sum(jnp.where(lane == meta[:, e_col:e_col + 1], srcv, 0.0), axis=-1, keepdims=True)
        return start + meta[:, r_col:r_col + 1]

    return pos(META_E1, META_RANK1), pos(META_E2, META_RANK2)


def _one_hot_rows(pos):
    col = lax.broadcasted_iota(jnp.int32, (TM, SORT_ROWS), 1).astype(F32)
    return col == pos


def _for_each_row_group(tile, tot_ref, dstg_ref, fn):
    def body(k, carry):
        for j in range(GROUP_UNROLL):
            g = k * GROUP_UNROLL + j
            fn(pl.multiple_of(g * RUN_ALIGN, RUN_ALIGN),
               pl.multiple_of(dstg_ref[tile * SORT_GROUPS + g], RUN_ALIGN))
        return carry

    lax.fori_loop(0, tot_ref[tile] // (RUN_ALIGN * GROUP_UNROLL), body, 0)


def _wait_rows(total, make_copy):
    for bit in RUN_BITS:
        @pl.when((total & bit) != 0)
        def _(bit=bit):
            make_copy(bit).wait()


def _dispatch_kernel(tot_ref, dstg_ref, u_ref, meta_ref, srcv_ref, xs_ref, sorted_ref, sems):
    i = pl.program_id(0)
    slot = i % 2

    def wait_tile(tile, slot):
        buf = sorted_ref.at[slot]
        _wait_rows(tot_ref[tile], lambda rows: pltpu.make_async_copy(
            buf.at[pl.ds(0, rows)], xs_ref.at[pl.ds(0, rows)], sems.at[slot]))

    @pl.when(i >= 2)
    def _():
        wait_tile(i - 2, slot)

    pos1, pos2 = _sorted_positions(meta_ref[...], srcv_ref[0])
    select = jnp.logical_or(_one_hot_rows(pos1), _one_hot_rows(pos2)).astype(BF16)
    sorted_ref[slot] = _dot_tn(select, u_ref[...])
    buf = sorted_ref.at[slot]

    def start(src, dst):
        pltpu.make_async_copy(buf.at[pl.ds(src, RUN_ALIGN)], xs_ref.at[pl.ds(dst, RUN_ALIGN)],
                              sems.at[slot]).start()

    _for_each_row_group(i, tot_ref, dstg_ref, start)

    @pl.when(i == N_TILES - 1)
    def _():
        wait_tile(i - 1, 1 - slot)
        wait_tile(i, slot)


def _dispatch(sched, u2, meta, srcv):
    return pl.pallas_call(
        _dispatch_kernel,
        grid_spec=pltpu.PrefetchScalarGridSpec(
            num_scalar_prefetch=2,
            grid=(N_TILES,),
            in_specs=[pl.BlockSpec((TM, D), lambda i, *_: (i, 0)),
                      pl.BlockSpec((TM, LANES), lambda i, *_: (i, 0)),
                      pl.BlockSpec((1, 1, LANES), lambda i, *_: (i, 0, 0))],
            out_specs=pl.BlockSpec(memory_space=pl.ANY),
            scratch_shapes=[pltpu.VMEM((2, SORT_ROWS, D), F32), pltpu.SemaphoreType.DMA((2,))],
        ),
        out_shape=jax.ShapeDtypeStruct((DISPATCH_ROWS, D), F32),
        compiler_params=_cparams(("arbitrary",)),
        name="moe_dispatch",
    )(*sched, u2, meta, srcv)


def _experts_kernel(start_ref, chunks_ref, xs_ref, wg_ref, wu_ref, wd_ref, ys_ref,
                    wg_bf, wu_bf, wd_bf, xbuf, ybuf, in_sems, out_sems):
    e = pl.program_id(0)
    n = chunks_ref[e]
    first = start_ref[e] // MOE_TM
    total = start_ref[N_EXPERTS - 1] // MOE_TM + chunks_ref[N_EXPERTS - 1]

    def rows(g):
        return pl.ds(pl.multiple_of(g * MOE_TM, MOE_TM), MOE_TM)

    def load(g):
        slot = g % MOE_IN_SLOTS
        return pltpu.make_async_copy(xs_ref.at[rows(g)], xbuf.at[slot], in_sems.at[slot])

    def store(g):
        slot = g % MOE_OUT_SLOTS
        return pltpu.make_async_copy(ybuf.at[slot], ys_ref.at[rows(g)], out_sems.at[slot])

    @pl.when(e == 0)
    def _():
        for g in range(MOE_AHEAD):
            @pl.when(g < total)
            def _(g=g):
                load(g).start()

    @pl.when(n > 0)
    def _():
        wg_bf[...] = wg_ref[0, 0].astype(BF16)
        wu_bf[...] = wu_ref[0, 0].astype(BF16)
        wd_bf[...] = wd_ref[0, 0].astype(BF16)

        def tile(g, carry):
            load(g).wait()

            @pl.when(g + MOE_AHEAD < total)
            def _():
                load(g + MOE_AHEAD).start()

            @pl.when(g >= MOE_OUT_SLOTS)
            def _():
                store(g - MOE_OUT_SLOTS).wait()

            x = xbuf[g % MOE_IN_SLOTS].astype(BF16)
            h = (_silu(_dot(x, wg_bf[...])) * _dot(x, wu_bf[...])).astype(BF16)
            ybuf[g % MOE_OUT_SLOTS] = _dot(h, wd_bf[...])
            store(g).start()
            return carry

        lax.fori_loop(first, first + n, tile, 0)

    @pl.when(e == N_EXPERTS - 1)
    def _():
        for back in range(MOE_OUT_SLOTS, 0, -1):
            @pl.when(total >= back)
            def _(back=back):
                store(total - back).wait()


def _experts(seg_start, seg_chunks, xs, w_gate, w_up, w_down, li):
    def weight(shape):
        return pl.BlockSpec((1, 1) + shape, lambda e, *_: (li, e, 0, 0))

    return pl.pallas_call(
        _experts_kernel,
        grid_spec=pltpu.PrefetchScalarGridSpec(
            num_scalar_prefetch=2,
            grid=(N_EXPERTS,),
            in_specs=[pl.BlockSpec(memory_space=pl.ANY),
                      weight((D, D_EXPERT)), weight((D, D_EXPERT)), weight((D_EXPERT, D))],
            out_specs=pl.BlockSpec(memory_space=pl.ANY),
            scratch_shapes=[pltpu.VMEM((D, D_EXPERT), BF16), pltpu.VMEM((D, D_EXPERT), BF16),
                            pltpu.VMEM((D_EXPERT, D), BF16),
                            pltpu.VMEM((MOE_IN_SLOTS, MOE_TM, D), F32),
                            pltpu.VMEM((MOE_OUT_SLOTS, MOE_TM, D), F32),
                            pltpu.SemaphoreType.DMA((MOE_IN_SLOTS,)),
                            pltpu.SemaphoreType.DMA((MOE_OUT_SLOTS,))],
        ),
        out_shape=jax.ShapeDtypeStruct((MOE_ROWS, D), F32),
        compiler_params=_cparams(("arbitrary",)),
        name="moe_experts",
    )(seg_start, seg_chunks, xs, w_gate, w_up, w_down)


def _combine_kernel(tot_ref, dstg_ref, ys_ref, x1_ref, meta_ref, srcv_ref, g2_ref,
                    lng_ref, lnb_ref, *rest, split, feeds_conv):
    if feeds_conv:
        (sh_ref, sc_ref, w1_ref, b1_ref), rest, w1_bf = rest[:4], rest[4:-1], rest[-1]
        _cast_weight_once(w1_ref, w1_bf)
    outs, (sorted_ref, sems) = rest[:-2], rest[-2:]
    i = pl.program_id(0)
    slot = i % 2

    def fetch(tile, slot):
        buf = sorted_ref.at[slot]

        def start(src, dst):
            pltpu.make_async_copy(ys_ref.at[pl.ds(dst, RUN_ALIGN)], buf.at[pl.ds(src, RUN_ALIGN)],
                                  sems.at[slot]).start()

        _for_each_row_group(tile, tot_ref, dstg_ref, start)

    @pl.when(i == 0)
    def _():
        sorted_ref[...] = jnp.zeros_like(sorted_ref)
        fetch(0, 0)

    @pl.when(i + 1 < N_TILES)
    def _():
        fetch(i + 1, 1 - slot)

    meta = meta_ref[...]
    pos1, pos2 = _sorted_positions(meta, srcv_ref[0])
    sel1 = _one_hot_rows(pos1).astype(BF16)
    sel2 = _one_hot_rows(pos2).astype(BF16)
    buf = sorted_ref.at[slot]
    _wait_rows(tot_ref[i], lambda rows: pltpu.make_async_copy(
        ys_ref.at[pl.ds(0, rows)], buf.at[pl.ds(0, rows)], sems.at[slot]))
    ysort = sorted_ref[slot].astype(BF16)
    f = (meta[:, META_W1:META_W1 + 1] * _dot(sel1, ysort)
         + meta[:, META_W2:META_W2 + 1] * _dot(sel2, ysort))
    y = _layer_norm(ALPHA * x1_ref[...] + g2_ref[0] * f, lng_ref[...], lnb_ref[...])
    if split:
        @pl.when(i < PROMPT_TILES)
        def _():
            outs[0][...] = y

        @pl.when(i >= PROMPT_TILES)
        def _():
            outs[1][...] = y
    else:
        outs[0][...] = y
    if feeds_conv:
        outs[-1][...] = _conv_glu(y, sh_ref, sc_ref, w1_bf, b1_ref)


def _combine(sched, ys, x1, meta, srcv, modr, ln_g, ln_b, li, split, conv_w1=None, conv_b1=None):
    feeds_conv = conv_w1 is not None
    tile = pl.BlockSpec((TM, D), lambda i, *_: (i, 0))
    if split:
        out_specs = [_prompt_tile_spec(D), _sample_tile_spec(D)]
        out_shape = [jax.ShapeDtypeStruct((T_PROMPT, D), F32), jax.ShapeDtypeStruct((T_SAMPLE, D), F32)]
    else:
        out_specs = [tile]
        out_shape = [jax.ShapeDtypeStruct((T, D), F32)]
    extra_specs, extra_args, extra_scratch = [], [], []
    if feeds_conv:
        extra_specs = [_mod_spec(li + 1, 0), _mod_spec(li + 1, 1),
                       _resident_f32_weight((D, 2 * D)), _row_spec((1, 2 * D))]
        extra_args = [modr, modr, conv_w1, conv_b1]
        extra_scratch = [pltpu.VMEM((D, 2 * D), BF16)]
        out_specs = out_specs + [tile]
        out_shape = out_shape + [jax.ShapeDtypeStruct((T, D), F32)]
    return pl.pallas_call(
        functools.partial(_combine_kernel, split=split, feeds_conv=feeds_conv),
        grid_spec=pltpu.PrefetchScalarGridSpec(
            num_scalar_prefetch=2,
            grid=(N_TILES,),
            in_specs=[pl.BlockSpec(memory_space=pl.ANY),
                      pl.BlockSpec((TM, D), lambda i, *_: (i, 0)),
                      pl.BlockSpec((TM, LANES), lambda i, *_: (i, 0)),
                      pl.BlockSpec((1, 1, LANES), lambda i, *_: (i, 0, 0)),
                      _mod_spec(li, 5), _row_spec((1, D)), _row_spec((1, D))] + extra_specs,
            out_specs=out_specs,
            scratch_shapes=[pltpu.VMEM((2, SORT_ROWS, D), F32), pltpu.SemaphoreType.DMA((2,))] + extra_scratch,
        ),
        out_shape=out_shape,
        compiler_params=_cparams(("arbitrary",)),
        name="moe_combine",
    )(*sched, ys, x1, meta, srcv, modr, ln_g, ln_b, *extra_args)


def _moe_schedule(tile_counts):
    n = (tile_counts + RUN_ALIGN - 1) // RUN_ALIGN * RUN_ALIGN
    src = jnp.cumsum(n, axis=1) - n
    per_expert = jnp.sum(n, axis=0)
    seg = (per_expert + MOE_TM - 1) // MOE_TM * MOE_TM
    seg_start = jnp.cumsum(seg) - seg
    dst = seg_start[None, :] + jnp.cumsum(n, axis=0) - n
    g_row = jnp.arange(SORT_GROUPS, dtype=jnp.int32) * RUN_ALIGN
    in_run = jnp.logical_and(src[:, None, :] <= g_row[None, :, None],
                             g_row[None, :, None] < (src + n)[:, None, :])
    dst_g = g_row[None, :] + jnp.sum(jnp.where(in_run, (dst - src)[:, None, :], 0), axis=2)
    step_rows = RUN_ALIGN * GROUP_UNROLL
    rows = jnp.sum(n, axis=1)
    issued = (rows + step_rows - 1) // step_rows * step_rows
    real = g_row[None, :] < rows[:, None]
    spill = (MOE_ROWS + jnp.arange(N_TILES, dtype=jnp.int32)[:, None] * step_rows
             + g_row[None, :] % step_rows)
    scatter_runs = (issued, jnp.where(real, dst_g, spill).reshape(-1))
    gather_runs = (issued, jnp.where(real, dst_g, dst_g[:, :1]).reshape(-1))
    srcv = jnp.pad(src.astype(F32), ((0, 0), (ROUTER_LANE0, LANES - ROUTER_LANE0 - N_EXPERTS)))
    return scatter_runs, gather_runs, srcv.reshape(N_TILES, 1, LANES), (seg_start, seg // MOE_TM)


def _moe(x1, u2, meta, cnt, modr, w_gate, w_up, w_down, ln_g, ln_b, li, split, **next_conv):
    tile_counts = cnt[:, 0, ROUTER_LANE0:ROUTER_LANE0 + N_EXPERTS].astype(jnp.int32)
    scatter_runs, gather_runs, srcv, (seg_start, seg_chunks) = _moe_schedule(tile_counts)
    xs = _dispatch(scatter_runs, u2, meta, srcv)
    ys = _experts(seg_start, seg_chunks, xs, w_gate, w_up, w_down, li)
    return _combine(gather_runs, ys, x1, meta, srcv, modr, ln_g, ln_b, li, split, **next_conv)


def _router_slab(wg, bg, we, be):
    w = jnp.concatenate([wg, we.transpose(1, 0, 2).reshape(D, N_EXPERTS)], axis=1)
    b = jnp.concatenate([bg, be.reshape(N_EXPERTS)])
    pad = LANES - w.shape[1]
    return jnp.pad(w, ((0, 0), (0, pad))), jnp.pad(b, (0, pad)).reshape(1, LANES)


def kernel(x_prompt, x_sample, cache_diff_k, cache_diff_v, state_ret_fwd, state_ret_bwd, c, c_ctx, mod_w, mod_b, ln1_g, ln1_b, ln2_g, ln2_b, mix_w_in, mix_w_out, ret_decay_fwd, ret_decay_bwd, diff_lq1, diff_lk1, diff_lq2, diff_lk2, diff_subln_g, conv_w1, conv_b1, conv_dw, conv_dw_b, conv_ln_g, conv_ln_b, conv_w2, conv_b2, router_g_w, router_g_b, router_e_w, router_e_b, moe_w_gate, moe_w_up, moe_w_down):
    xp = x_prompt.reshape(T_PROMPT, D)
    xs = x_sample.reshape(T_SAMPLE, D)
    cond = jnp.concatenate([c_ctx[None, :], c, jnp.zeros((MOD_ROWS - 1 - DEC_BATCH, D), F32)], axis=0)
    modr = _mod_vectors(cond, mod_w, mod_b).reshape(DEPTH * MOD_ROWS * 6, 1, D)
    cos, sin_signed = _rope_tables()

    def row(v):
        return v.reshape(1, -1)

    x = None
    caches = None
    for li in range(DEPTH):
        wr, br = _router_slab(router_g_w[li], router_g_b[li], router_e_w[li], router_e_b[li])
        if li % 2 == 0:
            assert li == 0, "the even mixer reads the kernel inputs directly"
            e = li // 2
            lam_init = 0.8 - 0.6 * math.exp(-0.3 * li)
            proj, ck, cv = _in_proj(xp, xs, modr, mix_w_in[e], li)
            dec = jnp.concatenate([ret_decay_fwd[e], ret_decay_bwd[e]])
            r_p, sf, sb = _retention(proj, dec, BATCH, SEQ, 0, HEADS, emit_state=True, chunk_len=SEQ)
            (r_s,) = _retention(proj, dec, DEC_BATCH, DEC_SEQ, T_PROMPT // DEC_SEQ, 2,
                                s0f=state_ret_fwd, s0b=state_ret_bwd, e=e, chunk_len=2 * RET_CHUNK)
            lams = (row(diff_lq1[e]), row(diff_lk1[e]), row(diff_lq2[e]), row(diff_lk2[e]),
                    row(diff_subln_g[e]))
            o_p = _attn_prompt(proj, *lams, lam_init)
            o_s = _attn_sample(proj, cache_diff_k, cache_diff_v, cos, sin_signed, *lams, lam_init, e)
            x1, u2, meta, cnt = _out_proj_tail(r_p, r_s, o_p, o_s, mix_w_out[e], xp, xs, modr,
                                               row(ln1_g[li]), row(ln1_b[li]), wr, br, li)
            caches = (ck, cv, sf, sb)
        else:
            o = li // 2
            x1, u2, meta, cnt = _conv_tail(glu, conv_dw[o], row(conv_dw_b[o]), row(conv_ln_g[o]),
                                           row(conv_ln_b[o]), conv_w2[o], row(conv_b2[o]),
                                           x, modr, row(ln1_g[li]), row(ln1_b[li]), wr, br, li)
        next_conv = {}
        if li + 1 < DEPTH and (li + 1) % 2 == 1:
            next_conv = dict(conv_w1=conv_w1[(li + 1) // 2], conv_b1=row(conv_b1[(li + 1) // 2]))
        outs = _moe(x1, u2, meta, cnt, modr, moe_w_gate, moe_w_up, moe_w_down,
                    row(ln2_g[li]), row(ln2_b[li]), li, split=(li == DEPTH - 1), **next_conv)
        x, glu = outs[0], outs[-1]

    y_prompt = outs[0].reshape(BATCH, SEQ, D)
    y_sample = outs[1].reshape(DEC_BATCH, DEC_SEQ, D)
    return (y_prompt, y_sample) + caches
```

```python
import functools
import math

import numpy as np
import jax
import jax.numpy as jnp
from jax import lax
from jax.experimental import pallas as pl
from jax.experimental.pallas import tpu as pltpu

F32 = jnp.float32
BF16 = jnp.bfloat16

D = 1024
BATCH = 16
SEQ = 256
DEPTH = 2
DEC_BATCH = 2
DEC_SEQ = 2048
PAST_LEN = 512
GRID_W = 64
HEADS = 4
HEAD_W = 128
RET_CHUNK = 128
DIFF_DK = 64
ROPE_THETA = 10000.0
IN_W = 7 * HEADS * HEAD_W
CONV_K = 31
CONV_PAD = CONV_K // 2
N_GROUPS = 4
EXPERTS_PER_GROUP = 8
N_EXPERTS = N_GROUPS * EXPERTS_PER_GROUP
D_EXPERT = 512
ALPHA = (2.0 * DEPTH) ** 0.25
LN_EPS = 1e-5
GN_EPS = 1e-6

T_PROMPT = BATCH * SEQ
T_SAMPLE = DEC_BATCH * DEC_SEQ
T = T_PROMPT + T_SAMPLE
TM = 256
N_TILES = T // TM
PROMPT_TILES = T_PROMPT // TM
SAMPLE_TILES_PER_SEQ = DEC_SEQ // TM
MOD_ROWS = 8
MOE_TM = 256
LANES = 128
SUBLANES = 8
RUN_ALIGN = SUBLANES
SORT_ROWS = -(-(2 * TM + N_EXPERTS * (RUN_ALIGN - 1)) // TM) * TM
RUN_BITS = tuple(1 << b for b in range((2 * TM).bit_length() - 1, RUN_ALIGN.bit_length() - 2, -1))
MOE_MAX_TILES = -(-(2 * T + N_TILES * N_EXPERTS * (RUN_ALIGN - 1) + N_EXPERTS * (MOE_TM - RUN_ALIGN)) // MOE_TM)
MOE_ROWS = MOE_MAX_TILES * MOE_TM
SORT_GROUPS = SORT_ROWS // RUN_ALIGN
GROUP_UNROLL = 8
DISPATCH_ROWS = MOE_ROWS + N_TILES * RUN_ALIGN * GROUP_UNROLL
MOE_AHEAD = 8
MOE_IN_SLOTS = MOE_AHEAD + 1
MOE_OUT_SLOTS = 4
ROUTER_LANE0 = N_GROUPS
ROPE_GROUP = DIFF_DK // 2
ROPE_HALF = ROPE_GROUP // 2
V7X_VMEM_BYTES = 64 * 1024 * 1024
VMEM_LIMIT = V7X_VMEM_BYTES - 12 * 1024 * 1024


def _cparams(sem):
    return pltpu.CompilerParams(dimension_semantics=sem, vmem_limit_bytes=VMEM_LIMIT)


def _tile_cond_row(i, tm):
    return jnp.where(i < T_PROMPT // tm, 0, 1 + (i - T_PROMPT // tm) // (DEC_SEQ // tm))


def _mod_spec(li, k, tm=TM):
    return pl.BlockSpec((1, 1, D), lambda i, *_: ((li * MOD_ROWS + _tile_cond_row(i, tm)) * 6 + k, 0, 0))


def _row_spec(shape):
    return pl.BlockSpec(shape, lambda i, *_: (0,) * len(shape))


def _resident_f32_weight(shape):
    return pl.BlockSpec(shape, lambda i, *_: (0,) * len(shape), pipeline_mode=pl.Buffered(1))


def _cast_weight_once(w_ref, w_bf):
    @pl.when(pl.program_id(0) == 0)
    def _():
        w_bf[...] = w_ref[...].astype(BF16)


def _layer_norm(x, g, b):
    mu = jnp.mean(x, axis=-1, keepdims=True)
    xc = x - mu
    var = jnp.mean(xc * xc, axis=-1, keepdims=True)
    return xc * lax.rsqrt(var + LN_EPS) * g + b


def _silu(x):
    return x * jax.nn.sigmoid(x)


def _dot(a, b):
    return jnp.dot(a, b, preferred_element_type=F32)


def _dot_nt(a, b):
    return lax.dot_general(a, b, (((1,), (1,)), ((), ())), preferred_element_type=F32)


def _dot_tn(a, b):
    return lax.dot_general(a, b, (((0,), (0,)), ((), ())), preferred_element_type=F32)


MOD_TN = 2048
MOD_USED_ROWS = 1 + DEC_BATCH


def _mod_kernel(cond_t_ref, w_ref, b_ref, o_ref):
    s = _silu(cond_t_ref[...])
    w = w_ref[0]
    o_ref[0] = jnp.zeros((MOD_ROWS, MOD_TN), F32) + b_ref[0]
    for r in range(MOD_USED_ROWS):
        o_ref[0, r:r + 1, :] = jnp.sum(w * s[:, r:r + 1], axis=0, keepdims=True) + b_ref[0]


def _mod_vectors(cond, mod_w, mod_b):
    return pl.pallas_call(
        _mod_kernel,
        grid=(DEPTH, 6 * D // MOD_TN),
        in_specs=[
            pl.BlockSpec((D, MOD_ROWS), lambda l, j: (0, 0)),
            pl.BlockSpec((1, D, MOD_TN), lambda l, j: (l, 0, j)),
            pl.BlockSpec((1, 1, MOD_TN), lambda l, j: (l, 0, j)),
        ],
        out_specs=pl.BlockSpec((1, MOD_ROWS, MOD_TN), lambda l, j: (l, 0, j)),
        out_shape=jax.ShapeDtypeStruct((DEPTH, MOD_ROWS, 6 * D), F32),
        compiler_params=_cparams(("arbitrary", "arbitrary")),
        name="mod_vectors",
    )(cond.T, mod_w, mod_b.reshape(DEPTH, 1, 6 * D))


def _prompt_tile_spec(width, tm=TM):
    return pl.BlockSpec((tm, width), lambda i, *_: (jnp.minimum(i, T_PROMPT // tm - 1), 0))


def _sample_tile_spec(width, tm=TM):
    return pl.BlockSpec((tm, width), lambda i, *_: (jnp.maximum(i - T_PROMPT // tm, 0), 0))


def _pick_tile(prompt_ref, sample_ref, tm=TM):
    return jnp.where(pl.program_id(0) < T_PROMPT // tm, prompt_ref[...], sample_ref[...])


IN_TM = 512
IN_SEQS = IN_TM // SEQ


def _in_proj_kernel(xp_ref, xs_ref, sh_ref, sc_ref, w_ref, o_ref, ck_ref, cv_ref, w_bf):
    _cast_weight_once(w_ref, w_bf)
    u = _pick_tile(xp_ref, xs_ref, IN_TM) * (1.0 + sc_ref[0]) + sh_ref[0]
    proj = _dot(u.astype(BF16), w_bf[...])
    o_ref[...] = proj.astype(BF16)

    @pl.when(pl.program_id(0) < T_PROMPT // IN_TM)
    def _():
        for s in range(IN_SEQS):
            rows = slice(s * SEQ, (s + 1) * SEQ)
            for h in range(HEADS):
                ck_ref[s, 0, h] = proj[rows, (COL_KD + h) * HEAD_W:(COL_KD + h + 1) * HEAD_W]
                cv_ref[s, 0, h] = proj[rows, (COL_VD + h) * HEAD_W:(COL_VD + h + 1) * HEAD_W]


def _in_proj(x_prompt, x_sample, modr, w_in, li):
    cache_spec = pl.BlockSpec((IN_SEQS, 1, HEADS, SEQ, HEAD_W),
                              lambda i: (jnp.minimum(i, T_PROMPT // IN_TM - 1), 0, 0, 0, 0))
    cache_shape = jax.ShapeDtypeStruct((BATCH, 1, HEADS, SEQ, HEAD_W), F32)
    return pl.pallas_call(
        _in_proj_kernel,
        grid=(T // IN_TM,),
        in_specs=[
            _prompt_tile_spec(D, IN_TM), _sample_tile_spec(D, IN_TM),
            _mod_spec(li, 0, IN_TM),
            _mod_spec(li, 1, IN_TM),
            _resident_f32_weight((D, IN_W)),
        ],
        out_specs=[pl.BlockSpec((IN_TM, IN_W), lambda i: (i, 0)), cache_spec, cache_spec],
        out_shape=[jax.ShapeDtypeStruct((T, IN_W), BF16), cache_shape, cache_shape],
        scratch_shapes=[pltpu.VMEM((D, IN_W), BF16)],
        compiler_params=_cparams(("arbitrary",)),
        name="in_proj",
    )(x_prompt, x_sample, modr, modr, w_in)


COL_QR, COL_KR, COL_VR, COL_GR, COL_QD, COL_KD, COL_VD = (k * HEADS for k in range(7))


def _retention_kernel(dec_ref, q_ref, k_ref, v_ref, g_ref, *rest, chunk_len, n_chunks, n_heads, has_state,
                      emit_state):
    rest = list(rest)
    if has_state:
        s0f_ref, s0b_ref = rest[:2]
        rest = rest[2:]
    r_ref = rest[0]
    rest = rest[1:]
    if emit_state:
        sf_ref, sb_ref = rest[:2]
        rest = rest[2:]
    of_ref = rest[0]

    head0 = pl.program_id(1) * n_heads
    C = chunk_len
    ii = lax.broadcasted_iota(jnp.int32, (C, C), 0)
    jj = lax.broadcasted_iota(jnp.int32, (C, C), 1)
    rel = (ii - jj).astype(F32)
    idx = lax.broadcasted_iota(jnp.int32, (C, 1), 0).astype(F32)
    k_scale = HEAD_W ** -0.5

    def chunk(ref, c, h):
        return ref[c * C:(c + 1) * C, h * HEAD_W:(h + 1) * HEAD_W].astype(F32)

    def decays(direction, h):
        lg = -jnp.exp(jnp.full((1, 1), dec_ref[direction * HEADS + head0 + h], F32))
        if direction == 0:
            inner = jnp.where(rel >= 0, jnp.exp(jnp.maximum(rel, 0.0) * lg), 0.0)
            return inner, jnp.exp((idx + 1.0) * lg), jnp.exp((C - 1.0 - idx) * lg), jnp.exp(C * lg)
        inner = jnp.where(rel <= 0, jnp.exp(jnp.maximum(-rel, 0.0) * lg), 0.0)
        return inner, jnp.exp((C - idx) * lg), jnp.exp(idx * lg), jnp.exp(C * lg)

    def run(direction):
        dec = [decays(direction, h) for h in range(n_heads)]
        if has_state:
            s0_ref = s0f_ref if direction == 0 else s0b_ref
            states = [s0_ref[0, 0, h] for h in range(n_heads)]
        else:
            states = [jnp.zeros((HEAD_W, HEAD_W), F32) for _ in range(n_heads)]
        order = range(n_chunks) if direction == 0 else range(n_chunks - 1, -1, -1)
        for c in order:
            rows = slice(c * C, (c + 1) * C)
            for h in range(n_heads):
                inner, q_decay, k_decay, chunk_decay = dec[h]
                cols = slice(h * HEAD_W, (h + 1) * HEAD_W)
                s = states[h]
                qc = chunk(q_ref, c, h)
                kc = chunk(k_ref, c, h) * k_scale
                vc = chunk(v_ref, c, h).astype(BF16)
                scores = _dot_nt(qc.astype(BF16), kc.astype(BF16)) * inner
                o = _dot(scores.astype(BF16), vc) + _dot((qc * q_decay).astype(BF16), s.astype(BF16))
                states[h] = s * chunk_decay + _dot_tn((kc * k_decay).astype(BF16), vc)
                if direction == 0:
                    of_ref[rows, cols] = o
                else:
                    r = of_ref[rows, cols] + o
                    mu = jnp.mean(r, axis=-1, keepdims=True)
                    rc = r - mu
                    var = jnp.mean(rc * rc, axis=-1, keepdims=True)
                    rn = rc * lax.rsqrt(var + GN_EPS)
                    r_ref[rows, cols] = _silu(chunk(g_ref, c, h)) * rn
        return states

    sf = run(0)
    sb = run(1)
    if emit_state:
        for h in range(n_heads):
            sf_ref[0, 0, h] = sf[h]
            sb_ref[0, 0, h] = sb[h]


def _retention(proj, dec, n_seq, seq_len, row_block0, n_heads, s0f=None, s0b=None, e=0, emit_state=False,
               chunk_len=RET_CHUNK):
    has_state = s0f is not None
    width = n_heads * HEAD_W

    def col(base):
        return pl.BlockSpec((seq_len, width), lambda b, h, *_: (row_block0 + b, base // n_heads + h))

    state_spec = pl.BlockSpec((1, 1, n_heads, HEAD_W, HEAD_W), lambda b, h, *_: (b, e, h, 0, 0))
    in_specs = [pl.BlockSpec(memory_space=pltpu.SMEM), col(COL_QR), col(COL_KR), col(COL_VR), col(COL_GR)]
    args = [dec, proj, proj, proj, proj]
    if has_state:
        in_specs += [state_spec, state_spec]
        args += [s0f, s0b]
    out_specs = [pl.BlockSpec((seq_len, width), lambda b, h, *_: (b, h))]
    out_shape = [jax.ShapeDtypeStruct((n_seq * seq_len, HEADS * HEAD_W), F32)]
    if emit_state:
        st = pl.BlockSpec((1, 1, n_heads, HEAD_W, HEAD_W), lambda b, h, *_: (b, 0, h, 0, 0))
        out_specs += [st, st]
        out_shape += [jax.ShapeDtypeStruct((n_seq, 1, HEADS, HEAD_W, HEAD_W), F32)] * 2
    return pl.pallas_call(
        functools.partial(_retention_kernel, chunk_len=chunk_len, n_chunks=seq_len // chunk_len, n_heads=n_heads,
                          has_state=has_state, emit_state=emit_state),
        grid=(n_seq, HEADS // n_heads),
        in_specs=in_specs,
        out_specs=out_specs,
        out_shape=out_shape,
        scratch_shapes=[pltpu.VMEM((seq_len, width), F32)],
        compiler_params=_cparams(("arbitrary", "arbitrary")),
        name=f"retention_{seq_len}",
    )(*args)


def _diff_lambda(lq1_ref, lk1_ref, lq2_ref, lk2_ref, lam_init):
    a = jnp.sum(lq1_ref[...] * lk1_ref[...], axis=-1, keepdims=True)
    b = jnp.sum(lq2_ref[...] * lk2_ref[...], axis=-1, keepdims=True)
    return jnp.exp(a) - jnp.exp(b) + lam_init


LOG2E = 1.4426950408889634


def _diff_attend(q, k, v, lam, subln_g, lam_init):
    lane = lax.broadcasted_iota(jnp.int32, q.shape, 1)
    q1 = jnp.where(lane < DIFF_DK, q, 0.0).astype(BF16)
    q2 = jnp.where(lane >= DIFF_DK, q, 0.0).astype(BF16)

    def softmax_times_v(qz):
        s = _dot_nt(qz, k)
        p = jnp.exp2(s - jnp.max(s, axis=-1, keepdims=True))
        return _dot(p.astype(BF16), v) * (1.0 / jnp.sum(p, axis=-1, keepdims=True))

    o = softmax_times_v(q1) - lam * softmax_times_v(q2)
    o = o * lax.rsqrt(jnp.mean(o * o, axis=-1, keepdims=True) + LN_EPS)
    return o * subln_g * (1.0 - lam_init)


def _attn_prompt_kernel(q_ref, k_ref, v_ref, lq1, lk1, lq2, lk2, g_ref, o_ref, *, lam_init):
    lam = _diff_lambda(lq1, lk1, lq2, lk2, lam_init)
    scale = DIFF_DK ** -0.5 * LOG2E
    for h in range(HEADS):
        sl = slice(h * HEAD_W, (h + 1) * HEAD_W)
        o_ref[:, sl] = _diff_attend(q_ref[:, sl].astype(F32) * scale, k_ref[:, sl], v_ref[:, sl],
                                    lam, g_ref[...], lam_init)


def _attn_prompt(proj, lq1, lk1, lq2, lk2, subln_g, lam_init):
    W = HEADS * HEAD_W

    def slab(base):
        return pl.BlockSpec((SEQ, W), lambda b: (b, base // HEADS))

    small = _row_spec((1, DIFF_DK))
    return pl.pallas_call(
        functools.partial(_attn_prompt_kernel, lam_init=lam_init),
        grid=(BATCH,),
        in_specs=[slab(COL_QD), slab(COL_KD), slab(COL_VD), small, small, small, small,
                  _row_spec((1, HEAD_W))],
        out_specs=pl.BlockSpec((SEQ, W), lambda b: (b, 0)),
        out_shape=jax.ShapeDtypeStruct((T_PROMPT, W), F32),
        compiler_params=_cparams(("arbitrary",)),
        name="diff_attn_prompt",
    )(proj, proj, proj, lq1, lk1, lq2, lk2, subln_g)


def _rope(x, cos, sin_signed):
    lane = lax.broadcasted_iota(jnp.int32, x.shape, 1)
    partner = jnp.where((lane % ROPE_GROUP) < ROPE_HALF,
                        pltpu.roll(x, LANES - ROPE_HALF, 1), pltpu.roll(x, ROPE_HALF, 1))
    return x * cos + partner * sin_signed


def _attn_sample_kernel(q_ref, k_ref, v_ref, ck_ref, cv_ref, cosq_ref, sinq_ref, cos_ref, sin_ref,
                        lq1, lk1, lq2, lk2, g_ref, o_ref, kbuf, vbuf, *, lam_init):
    @pl.when(pl.program_id(2) == 0)
    def _():
        kbuf[0:DEC_SEQ, :] = _rope(k_ref[...].astype(F32), cos_ref[...], sin_ref[...]).astype(BF16)
        kbuf[DEC_SEQ:, :] = ck_ref[0, 0, 0].astype(BF16)
        vbuf[0:DEC_SEQ, :] = v_ref[...]
        vbuf[DEC_SEQ:, :] = cv_ref[0, 0, 0].astype(BF16)

    lam = _diff_lambda(lq1, lk1, lq2, lk2, lam_init)
    q = _rope(q_ref[...].astype(F32), cosq_ref[...], sinq_ref[...]) * (DIFF_DK ** -0.5 * LOG2E)
    o_ref[...] = _diff_attend(q, kbuf[...], vbuf[...], lam, g_ref[...], lam_init)


ATTN_TQ = 256


def _attn_sample(proj, cache_k, cache_v, cos, sin_signed, lq1, lk1, lq2, lk2, subln_g, lam_init, e):
    nq = DEC_SEQ // ATTN_TQ
    row0_q = T_PROMPT // ATTN_TQ
    row0_kv = T_PROMPT // DEC_SEQ
    small = pl.BlockSpec((1, DIFF_DK), lambda b, h, t: (0, 0))
    cache = pl.BlockSpec((1, 1, 1, PAST_LEN, HEAD_W), lambda b, h, t: (b, e, h, 0, 0))
    table_q = pl.BlockSpec((ATTN_TQ, HEAD_W), lambda b, h, t: (t, 0))
    table = pl.BlockSpec((DEC_SEQ, HEAD_W), lambda b, h, t: (0, 0))
    return pl.pallas_call(
        functools.partial(_attn_sample_kernel, lam_init=lam_init),
        grid=(DEC_BATCH, HEADS, nq),
        in_specs=[
            pl.BlockSpec((ATTN_TQ, HEAD_W), lambda b, h, t: (row0_q + b * nq + t, COL_QD + h)),
            pl.BlockSpec((DEC_SEQ, HEAD_W), lambda b, h, t: (row0_kv + b, COL_KD + h)),
            pl.BlockSpec((DEC_SEQ, HEAD_W), lambda b, h, t: (row0_kv + b, COL_VD + h)),
            cache, cache, table_q, table_q, table, table,
            small, small, small, small,
            pl.BlockSpec((1, HEAD_W), lambda b, h, t: (0, 0)),
        ],
        out_specs=pl.BlockSpec((ATTN_TQ, HEAD_W), lambda b, h, t: (b * nq + t, h)),
        out_shape=jax.ShapeDtypeStruct((T_SAMPLE, HEADS * HEAD_W), F32),
        scratch_shapes=[pltpu.VMEM((DEC_SEQ + PAST_LEN, HEAD_W), BF16),
                        pltpu.VMEM((DEC_SEQ + PAST_LEN, HEAD_W), BF16)],
        compiler_params=_cparams(("arbitrary", "arbitrary", "arbitrary")),
        name="diff_attn_sample",
    )(proj, proj, proj, cache_k, cache_v, cos, sin_signed, cos, sin_signed,
      lq1, lk1, lq2, lk2, subln_g)


def _rope_tables():
    t = np.arange(DEC_SEQ)
    row, colp = t // GRID_W, t % GRID_W
    lane = np.arange(LANES)
    pos = np.where(((lane // ROPE_GROUP) % 2 == 0)[None, :], row[:, None], colp[:, None]).astype(np.float64)
    half = ROPE_HALF
    inv = (np.float32(ROPE_THETA) ** (-(np.arange(half, dtype=np.float32)) / np.float32(half))).astype(np.float32)
    ang = pos.astype(np.float32) * inv[lane % half][None, :]
    cos = np.cos(ang.astype(np.float64)).astype(np.float32)
    sin = np.sin(ang.astype(np.float64)).astype(np.float32)
    sign = np.where((lane % ROPE_GROUP) < half, -1.0, 1.0).astype(np.float32)[None, :]
    return jnp.asarray(cos), jnp.asarray(sin * sign)


def _split_bf16(a):
    hi = a.astype(BF16)
    return hi, (a - hi.astype(F32)).astype(BF16)


def _mixer_tail(out, x, g1_ref, sc2_ref, sh2_ref, lng_ref, lnb_ref, wr_ref, br_ref,
                x1_ref, u2_ref, meta_ref, cnt_ref, wr_bf, before_bf):
    @pl.when(pl.program_id(0) == 0)
    def _():
        w_hi, w_lo = _split_bf16(wr_ref[...])
        wr_bf[:, 0:LANES] = w_hi
        wr_bf[:, LANES:] = w_lo
        r_i = lax.broadcasted_iota(jnp.int32, (TM, TM), 0)
        c_i = lax.broadcasted_iota(jnp.int32, (TM, TM), 1)
        before_bf[...] = (c_i < r_i).astype(BF16)

    x1 = _layer_norm(ALPHA * x + g1_ref[0] * out, lng_ref[...], lnb_ref[...])
    x1_ref[...] = x1
    u2 = x1 * (1.0 + sc2_ref[0]) + sh2_ref[0]
    u2_ref[...] = u2.astype(BF16)

    u_hi, u_lo = _split_bf16(u2)
    hi = _dot(u_hi, wr_bf[...])
    logits = hi[:, 0:LANES] + (hi[:, LANES:] + _dot(u_lo, wr_bf[:, 0:LANES])) + br_ref[...]
    lane = lax.broadcasted_iota(jnp.int32, logits.shape, 1).astype(F32)
    neg = jnp.float32(-jnp.inf)
    is_g = lane < N_GROUPS
    gl = jnp.where(is_g, logits, neg)
    gmax = jnp.max(gl, axis=-1, keepdims=True)
    gsel = jnp.min(jnp.where(gl == gmax, lane, float(LANES)), axis=-1, keepdims=True)
    p_g = 1.0 / jnp.sum(jnp.where(is_g, jnp.exp(gl - gmax), 0.0), axis=-1, keepdims=True)
    lo = ROUTER_LANE0 + gsel * EXPERTS_PER_GROUP
    el = jnp.where((lane >= lo) & (lane < lo + EXPERTS_PER_GROUP), logits, neg)
    v1 = jnp.max(el, axis=-1, keepdims=True)
    i1 = jnp.min(jnp.where(el == v1, lane, float(LANES)), axis=-1, keepdims=True)
    el2 = jnp.where(lane == i1, neg, el)
    v2 = jnp.max(el2, axis=-1, keepdims=True)
    i2 = jnp.min(jnp.where(el2 == v2, lane, float(LANES)), axis=-1, keepdims=True)
    t = jnp.exp(v2 - v1)
    w1 = p_g / (1.0 + t)
    w2 = w1 * t

    oh1 = (lane == i1).astype(F32)
    oh2 = (lane == i2).astype(F32)
    oh = oh1 + oh2
    earlier = _dot(before_bf[...], oh.astype(BF16))
    rank1 = jnp.sum(earlier * oh1, axis=-1, keepdims=True)
    rank2 = jnp.sum(earlier * oh2, axis=-1, keepdims=True)
    cnt_ref[0] = jnp.sum(oh, axis=0, keepdims=True)
    cols = (i1, i2, w1, w2, rank1, rank2)
    meta = jnp.zeros_like(logits)
    for k, col in enumerate(cols):
        meta = jnp.where(lane == k, col, meta)
    meta_ref[...] = meta


META_E1, META_E2, META_W1, META_W2, META_RANK1, META_RANK2 = range(6)

def _tail_scratch():
    return [pltpu.VMEM((D, 2 * LANES), BF16), pltpu.VMEM((TM, TM), BF16)]


_TAIL_OUT_SHAPES = [
    jax.ShapeDtypeStruct((T, D), F32),
    jax.ShapeDtypeStruct((T, D), BF16),
    jax.ShapeDtypeStruct((T, LANES), F32),
    jax.ShapeDtypeStruct((N_TILES, 1, LANES), F32),
]


def _tail_out_specs():
    return [
        pl.BlockSpec((TM, D), lambda i: (i, 0)),
        pl.BlockSpec((TM, D), lambda i: (i, 0)),
        pl.BlockSpec((TM, LANES), lambda i: (i, 0)),
        pl.BlockSpec((1, 1, LANES), lambda i: (i, 0, 0)),
    ]


def _tail_in_specs(li):
    return [
        _mod_spec(li, 2), _mod_spec(li, 4), _mod_spec(li, 3),
        _row_spec((1, D)), _row_spec((1, D)),
        _row_spec((D, LANES)), _row_spec((1, LANES)),
    ]


def _out_proj_kernel(rp_ref, rs_ref, op_ref, os_ref, w_ref, xp_ref, xs_ref, *rest):
    tail_args, w_bf = rest[:-3] + rest[-2:], rest[-3]
    _cast_weight_once(w_ref, w_bf)
    half = HEADS * HEAD_W
    r = _pick_tile(rp_ref, rs_ref).astype(BF16)
    o = _pick_tile(op_ref, os_ref).astype(BF16)
    out = _dot(r, w_bf[0:half, :]) + _dot(o, w_bf[half:, :])
    _mixer_tail(out, _pick_tile(xp_ref, xs_ref), *tail_args)


def _out_proj_tail(r_p, r_s, o_p, o_s, w_out, x_prompt, x_sample, modr, ln_g, ln_b, wr, br, li):
    half = HEADS * HEAD_W
    return pl.pallas_call(
        _out_proj_kernel,
        grid=(N_TILES,),
        in_specs=[_prompt_tile_spec(half), _sample_tile_spec(half),
                  _prompt_tile_spec(half), _sample_tile_spec(half),
                  _resident_f32_weight((2 * half, D)),
                  _prompt_tile_spec(D), _sample_tile_spec(D)] + _tail_in_specs(li),
        out_specs=_tail_out_specs(),
        out_shape=_TAIL_OUT_SHAPES,
        scratch_shapes=[pltpu.VMEM((2 * half, D), BF16)] + _tail_scratch(),
        compiler_params=_cparams(("arbitrary",)),
        name="out_proj_tail",
    )(r_p, r_s, o_p, o_s, w_out, x_prompt, x_sample, modr, modr, modr, ln_g, ln_b, wr, br)


def _conv_glu(y, sh_ref, sc_ref, w_bf, b_ref):
    u = y * (1.0 + sc_ref[0]) + sh_ref[0]
    h = _dot(u.astype(BF16), w_bf[...]) + b_ref[...]
    return h[:, :D] * jax.nn.sigmoid(h[:, D:])


HALO = 16
CONV_ROWS = 64
CONV_COLS = 128


def _depthwise_conv(hp, dw_ref, conv):
    base = HALO - CONV_PAD
    for cb in range(D // CONV_COLS):
        cs = slice(cb * CONV_COLS, (cb + 1) * CONV_COLS)
        taps = [dw_ref[tap:tap + 1, cs] for tap in range(CONV_K)]
        for rb in range(TM // CONV_ROWS):
            r0 = rb * CONV_ROWS
            acc = None
            for shift in range(SUBLANES):
                part = None
                for tap in range(CONV_K):
                    off = base + tap
                    if off % SUBLANES != shift:
                        continue
                    a0 = r0 + off - shift
                    term = hp[a0:a0 + CONV_ROWS + SUBLANES, cs] * taps[tap]
                    part = term if part is None else part + term
                part = part[shift:shift + CONV_ROWS, :]
                acc = part if acc is None else acc + part
            conv[r0:r0 + CONV_ROWS, cs] = acc


def _conv_tail_kernel(cur_ref, prev_ref, next_ref, dw_ref, dwb_ref, cg_ref, cb_ref, w2_ref, b2_ref,
                      x_ref, *rest):
    tail_args, (hp, conv, w2_bf) = rest[:-5] + rest[-2:], rest[-5:-2]
    _cast_weight_once(w2_ref, w2_bf)
    i = pl.program_id(0)
    k = (i - PROMPT_TILES) % SAMPLE_TILES_PER_SEQ
    in_sample = i >= PROMPT_TILES
    left_ok = jnp.logical_and(in_sample, k != 0)
    right_ok = jnp.logical_and(in_sample, k != SAMPLE_TILES_PER_SEQ - 1)
    hp[0:HALO, :] = jnp.where(left_ok, prev_ref[...], 0.0)
    hp[HALO:HALO + TM, :] = cur_ref[...]
    hp[HALO + TM:HALO + TM + HALO, :] = jnp.where(right_ok, next_ref[...], 0.0)
    _depthwise_conv(hp, dw_ref, conv)
    hc = _silu(_layer_norm(conv[...] + dwb_ref[...], cg_ref[...], cb_ref[...]))
    out = _dot(hc.astype(BF16), w2_bf[...]) + b2_ref[...]
    _mixer_tail(out, x_ref[...], *tail_args)


def _conv_tail(glu, dw, dwb, cg, cb, w2, b2, x, modr, ln_g, ln_b, wr, br, li):
    per = TM // HALO
    last = T // HALO - 1
    return pl.pallas_call(
        _conv_tail_kernel,
        grid=(N_TILES,),
        in_specs=[pl.BlockSpec((TM, D), lambda i: (i, 0)),
                  pl.BlockSpec((HALO, D), lambda i: (jnp.maximum(i * per - 1, 0), 0)),
                  pl.BlockSpec((HALO, D), lambda i: (jnp.minimum((i + 1) * per, last), 0)),
                  _row_spec((CONV_K, D)), _row_spec((1, D)), _row_spec((1, D)), _row_spec((1, D)),
                  _resident_f32_weight((D, D)), _row_spec((1, D)),
                  pl.BlockSpec((TM, D), lambda i: (i, 0))] + _tail_in_specs(li),
        out_specs=_tail_out_specs(),
        out_shape=_TAIL_OUT_SHAPES,
        scratch_shapes=[pltpu.VMEM((TM + 2 * HALO, D), F32), pltpu.VMEM((TM, D), F32),
                        pltpu.VMEM((D, D), BF16)] + _tail_scratch(),
        compiler_params=_cparams(("arbitrary",)),
        name="conv_tail",
    )(glu, glu, glu, dw, dwb, cg, cb, w2, b2, x, modr, modr, modr, ln_g, ln_b, wr, br)


def _sorted_positions(meta, srcv):
    lane = lax.broadcasted_iota(jnp.int32, meta.shape, 1).astype(F32)

    def pos(e_col, r_col):
        start = jnp.sum(jnp.where(lane == meta[:, e_col:e_col + 1], srcv, 0.0), axis=-1, keepdims=True)
        return start + meta[:, r_col:r_col + 1]

    return pos(META_E1, META_RANK1), pos(META_E2, META_RANK2)


def _one_hot_rows(pos):
    col = lax.broadcasted_iota(jnp.int32, (TM, SORT_ROWS), 1).astype(F32)
    return col == pos


def _for_each_row_group(tile, tot_ref, dstg_ref, fn):
    def body(k, carry):
        for j in range(GROUP_UNROLL):
            g = k * GROUP_UNROLL + j
            fn(pl.multiple_of(g * RUN_ALIGN, RUN_ALIGN),
               pl.multiple_of(dstg_ref[tile * SORT_GROUPS + g], RUN_ALIGN), j % 2)
        return carry

    lax.fori_loop(0, tot_ref[tile] // (RUN_ALIGN * GROUP_UNROLL), body, 0)


def _wait_rows(total, make_copy):
    for bit in RUN_BITS:
        @pl.when((total & bit) != 0)
        def _(bit=bit):
            make_copy(bit).wait()


def _dispatch_kernel(tot_ref, dstg_ref, u_ref, meta_ref, srcv_ref, xs_ref, sorted_ref, sems):
    i = pl.program_id(0)
    slot = i % 2

    def wait_tile(tile, slot):
        buf = sorted_ref.at[slot]
        _wait_rows(tot_ref[tile], lambda rows: pltpu.make_async_copy(
            buf.at[pl.ds(0, rows)], xs_ref.at[pl.ds(0, rows)], sems.at[slot]))

    @pl.when(i >= 2)
    def _():
        wait_tile(i - 2, slot)

    pos1, pos2 = _sorted_positions(meta_ref[...], srcv_ref[0])
    select = jnp.logical_or(_one_hot_rows(pos1), _one_hot_rows(pos2)).astype(BF16)
    sorted_ref[slot] = _dot_tn(select, u_ref[...])
    buf = sorted_ref.at[slot]

    def start(src, dst, priority):
        pltpu.make_async_copy(buf.at[pl.ds(src, RUN_ALIGN)], xs_ref.at[pl.ds(dst, RUN_ALIGN)],
                              sems.at[slot]).start(priority=priority)

    _for_each_row_group(i, tot_ref, dstg_ref, start)

    @pl.when(i == N_TILES - 1)
    def _():
        wait_tile(i - 1, 1 - slot)
        wait_tile(i, slot)


def _dispatch(sched, u2, meta, srcv):
    return pl.pallas_call(
        _dispatch_kernel,
        grid_spec=pltpu.PrefetchScalarGridSpec(
            num_scalar_prefetch=2,
            grid=(N_TILES,),
            in_specs=[pl.BlockSpec((TM, D), lambda i, *_: (i, 0)),
                      pl.BlockSpec((TM, LANES), lambda i, *_: (i, 0)),
                      pl.BlockSpec((1, 1, LANES), lambda i, *_: (i, 0, 0))],
            out_specs=pl.BlockSpec(memory_space=pl.ANY),
            scratch_shapes=[pltpu.VMEM((2, SORT_ROWS, D), F32), pltpu.SemaphoreType.DMA((2,))],
        ),
        out_shape=jax.ShapeDtypeStruct((DISPATCH_ROWS, D), F32),
        compiler_params=_cparams(("arbitrary",)),
        name="moe_dispatch",
    )(*sched, u2, meta, srcv)


def _experts_kernel(start_ref, chunks_ref, xs_ref, wg_ref, wu_ref, wd_ref, ys_ref,
                    wg_bf, wu_bf, wd_bf, xbuf, ybuf, in_sems, out_sems):
    e = pl.program_id(0)
    n = chunks_ref[e]
    first = start_ref[e] // MOE_TM
    total = start_ref[N_EXPERTS - 1] // MOE_TM + chunks_ref[N_EXPERTS - 1]

    def rows(g):
        return pl.ds(pl.multiple_of(g * MOE_TM, MOE_TM), MOE_TM)

    def load(g):
        slot = g % MOE_IN_SLOTS
        return pltpu.make_async_copy(xs_ref.at[rows(g)], xbuf.at[slot], in_sems.at[slot])

    def store(g):
        slot = g % MOE_OUT_SLOTS
        return pltpu.make_async_copy(ybuf.at[slot], ys_ref.at[rows(g)], out_sems.at[slot])

    @pl.when(e == 0)
    def _():
        for g in range(MOE_AHEAD):
            @pl.when(g < total)
            def _(g=g):
                load(g).start()

    @pl.when(n > 0)
    def _():
        wg_bf[...] = wg_ref[0, 0].astype(BF16)
        wu_bf[...] = wu_ref[0, 0].astype(BF16)
        wd_bf[...] = wd_ref[0, 0].astype(BF16)

        def tile(g, carry):
            load(g).wait()

            @pl.when(g + MOE_AHEAD < total)
            def _():
                load(g + MOE_AHEAD).start()

            @pl.when(g >= MOE_OUT_SLOTS)
            def _():
                store(g - MOE_OUT_SLOTS).wait()

            x = xbuf[g % MOE_IN_SLOTS].astype(BF16)
            h = (_silu(_dot(x, wg_bf[...])) * _dot(x, wu_bf[...])).astype(BF16)
            ybuf[g % MOE_OUT_SLOTS] = _dot(h, wd_bf[...])
            store(g).start()
            return carry

        lax.fori_loop(first, first + n, tile, 0)

    @pl.when(e == N_EXPERTS - 1)
    def _():
        for back in range(MOE_OUT_SLOTS, 0, -1):
            @pl.when(total >= back)
            def _(back=back):
                store(total - back).wait()


def _experts(seg_start, seg_chunks, xs, w_gate, w_up, w_down, li):
    def weight(shape):
        return pl.BlockSpec((1, 1) + shape, lambda e, *_: (li, e, 0, 0))

    return pl.pallas_call(
        _experts_kernel,
        grid_spec=pltpu.PrefetchScalarGridSpec(
            num_scalar_prefetch=2,
            grid=(N_EXPERTS,),
            in_specs=[pl.BlockSpec(memory_space=pl.ANY),
                      weight((D, D_EXPERT)), weight((D, D_EXPERT)), weight((D_EXPERT, D))],
            out_specs=pl.BlockSpec(memory_space=pl.ANY),
            scratch_shapes=[pltpu.VMEM((D, D_EXPERT), BF16), pltpu.VMEM((D, D_EXPERT), BF16),
                            pltpu.VMEM((D_EXPERT, D), BF16),
                            pltpu.VMEM((MOE_IN_SLOTS, MOE_TM, D), F32),
                            pltpu.VMEM((MOE_OUT_SLOTS, MOE_TM, D), F32),
                            pltpu.SemaphoreType.DMA((MOE_IN_SLOTS,)),
                            pltpu.SemaphoreType.DMA((MOE_OUT_SLOTS,))],
        ),
        out_shape=jax.ShapeDtypeStruct((MOE_ROWS, D), F32),
        compiler_params=_cparams(("arbitrary",)),
        name="moe_experts",
    )(seg_start, seg_chunks, xs, w_gate, w_up, w_down)


def _combine_kernel(tot_ref, dstg_ref, ys_ref, x1_ref, meta_ref, srcv_ref, g2_ref,
                    lng_ref, lnb_ref, *rest, split, feeds_conv):
    if feeds_conv:
        (sh_ref, sc_ref, w1_ref, b1_ref), rest, w1_bf = rest[:4], rest[4:-1], rest[-1]
        _cast_weight_once(w1_ref, w1_bf)
    outs, (sorted_ref, sems) = rest[:-2], rest[-2:]
    i = pl.program_id(0)
    slot = i % 2

    def fetch(tile, slot):
        buf = sorted_ref.at[slot]

        def start(src, dst, priority):
            pltpu.make_async_copy(ys_ref.at[pl.ds(dst, RUN_ALIGN)], buf.at[pl.ds(src, RUN_ALIGN)],
                                  sems.at[slot]).start(priority=priority)

        _for_each_row_group(tile, tot_ref, dstg_ref, start)

    @pl.when(i == 0)
    def _():
        sorted_ref[...] = jnp.zeros_like(sorted_ref)
        fetch(0, 0)

    @pl.when(i + 1 < N_TILES)
    def _():
        fetch(i + 1, 1 - slot)

    meta = meta_ref[...]
    pos1, pos2 = _sorted_positions(meta, srcv_ref[0])
    sel1 = _one_hot_rows(pos1).astype(BF16)
    sel2 = _one_hot_rows(pos2).astype(BF16)
    buf = sorted_ref.at[slot]
    _wait_rows(tot_ref[i], lambda rows: pltpu.make_async_copy(
        ys_ref.at[pl.ds(0, rows)], buf.at[pl.ds(0, rows)], sems.at[slot]))
    ysort = sorted_ref[slot].astype(BF16)
    f = (meta[:, META_W1:META_W1 + 1] * _dot(sel1, ysort)
         + meta[:, META_W2:META_W2 + 1] * _dot(sel2, ysort))
    y = _layer_norm(ALPHA * x1_ref[...] + g2_ref[0] * f, lng_ref[...], lnb_ref[...])
    if split:
        @pl.when(i < PROMPT_TILES)
        def _():
            outs[0][...] = y

        @pl.when(i >= PROMPT_TILES)
        def _():
            outs[1][...] = y
    else:
        outs[0][...] = y
    if feeds_conv:
        outs[-1][...] = _conv_glu(y, sh_ref, sc_ref, w1_bf, b1_ref)


def _combine(sched, ys, x1, meta, srcv, modr, ln_g, ln_b, li, split, conv_w1=None, conv_b1=None):
    feeds_conv = conv_w1 is not None
    tile = pl.BlockSpec((TM, D), lambda i, *_: (i, 0))
    if split:
        out_specs = [_prompt_tile_spec(D), _sample_tile_spec(D)]
        out_shape = [jax.ShapeDtypeStruct((T_PROMPT, D), F32), jax.ShapeDtypeStruct((T_SAMPLE, D), F32)]
    else:
        out_specs = [tile]
        out_shape = [jax.ShapeDtypeStruct((T, D), F32)]
    extra_specs, extra_args, extra_scratch = [], [], []
    if feeds_conv:
        extra_specs = [_mod_spec(li + 1, 0), _mod_spec(li + 1, 1),
                       _resident_f32_weight((D, 2 * D)), _row_spec((1, 2 * D))]
        extra_args = [modr, modr, conv_w1, conv_b1]
        extra_scratch = [pltpu.VMEM((D, 2 * D), BF16)]
        out_specs = out_specs + [tile]
        out_shape = out_shape + [jax.ShapeDtypeStruct((T, D), F32)]
    return pl.pallas_call(
        functools.partial(_combine_kernel, split=split, feeds_conv=feeds_conv),
        grid_spec=pltpu.PrefetchScalarGridSpec(
            num_scalar_prefetch=2,
            grid=(N_TILES,),
            in_specs=[pl.BlockSpec(memory_space=pl.ANY),
                      pl.BlockSpec((TM, D), lambda i, *_: (i, 0)),
                      pl.BlockSpec((TM, LANES), lambda i, *_: (i, 0)),
                      pl.BlockSpec((1, 1, LANES), lambda i, *_: (i, 0, 0)),
                      _mod_spec(li, 5), _row_spec((1, D)), _row_spec((1, D))] + extra_specs,
            out_specs=out_specs,
            scratch_shapes=[pltpu.VMEM((2, SORT_ROWS, D), F32), pltpu.SemaphoreType.DMA((2,))] + extra_scratch,
        ),
        out_shape=out_shape,
        compiler_params=_cparams(("arbitrary",)),
        name="moe_combine",
    )(*sched, ys, x1, meta, srcv, modr, ln_g, ln_b, *extra_args)


def _moe_schedule(tile_counts):
    n = (tile_counts + RUN_ALIGN - 1) // RUN_ALIGN * RUN_ALIGN
    src = jnp.cumsum(n, axis=1) - n
    per_expert = jnp.sum(n, axis=0)
    seg = (per_expert + MOE_TM - 1) // MOE_TM * MOE_TM
    seg_start = jnp.cumsum(seg) - seg
    dst = seg_start[None, :] + jnp.cumsum(n, axis=0) - n
    g_row = jnp.arange(SORT_GROUPS, dtype=jnp.int32) * RUN_ALIGN
    in_run = jnp.logical_and(src[:, None, :] <= g_row[None, :, None],
                             g_row[None, :, None] < (src + n)[:, None, :])
    dst_g = g_row[None, :] + jnp.sum(jnp.where(in_run, (dst - src)[:, None, :], 0), axis=2)
    step_rows = RUN_ALIGN * GROUP_UNROLL
    rows = jnp.sum(n, axis=1)
    issued = (rows + step_rows - 1) // step_rows * step_rows
    real = g_row[None, :] < rows[:, None]
    spill = (MOE_ROWS + jnp.arange(N_TILES, dtype=jnp.int32)[:, None] * step_rows
             + g_row[None, :] % step_rows)
    scatter_runs = (issued, jnp.where(real, dst_g, spill).reshape(-1))
    gather_runs = (issued, jnp.where(real, dst_g, dst_g[:, :1]).reshape(-1))
    srcv = jnp.pad(src.astype(F32), ((0, 0), (ROUTER_LANE0, LANES - ROUTER_LANE0 - N_EXPERTS)))
    return scatter_runs, gather_runs, srcv.reshape(N_TILES, 1, LANES), (seg_start, seg // MOE_TM)


def _moe(x1, u2, meta, cnt, modr, w_gate, w_up, w_down, ln_g, ln_b, li, split, **next_conv):
    tile_counts = cnt[:, 0, ROUTER_LANE0:ROUTER_LANE0 + N_EXPERTS].astype(jnp.int32)
    scatter_runs, gather_runs, srcv, (seg_start, seg_chunks) = _moe_schedule(tile_counts)
    xs = _dispatch(scatter_runs, u2, meta, srcv)
    ys = _experts(seg_start, seg_chunks, xs, w_gate, w_up, w_down, li)
    return _combine(gather_runs, ys, x1, meta, srcv, modr, ln_g, ln_b, li, split, **next_conv)


def _router_slab(wg, bg, we, be):
    w = jnp.concatenate([wg, we.transpose(1, 0, 2).reshape(D, N_EXPERTS)], axis=1)
    b = jnp.concatenate([bg, be.reshape(N_EXPERTS)])
    pad = LANES - w.shape[1]
    return jnp.pad(w, ((0, 0), (0, pad))), jnp.pad(b, (0, pad)).reshape(1, LANES)


def kernel(x_prompt, x_sample, cache_diff_k, cache_diff_v, state_ret_fwd, state_ret_bwd, c, c_ctx, mod_w, mod_b, ln1_g, ln1_b, ln2_g, ln2_b, mix_w_in, mix_w_out, ret_decay_fwd, ret_decay_bwd, diff_lq1, diff_lk1, diff_lq2, diff_lk2, diff_subln_g, conv_w1, conv_b1, conv_dw, conv_dw_b, conv_ln_g, conv_ln_b, conv_w2, conv_b2, router_g_w, router_g_b, router_e_w, router_e_b, moe_w_gate, moe_w_up, moe_w_down):
    xp = x_prompt.reshape(T_PROMPT, D)
    xs = x_sample.reshape(T_SAMPLE, D)
    cond = jnp.concatenate([c_ctx[None, :], c, jnp.zeros((MOD_ROWS - 1 - DEC_BATCH, D), F32)], axis=0)
    modr = _mod_vectors(cond, mod_w, mod_b).reshape(DEPTH * MOD_ROWS * 6, 1, D)
    cos, sin_signed = _rope_tables()

    def row(v):
        return v.reshape(1, -1)

    x = None
    caches = None
    for li in range(DEPTH):
        wr, br = _router_slab(router_g_w[li], router_g_b[li], router_e_w[li], router_e_b[li])
        if li % 2 == 0:
            assert li == 0, "the even mixer reads the kernel inputs directly"
            e = li // 2
            lam_init = 0.8 - 0.6 * math.exp(-0.3 * li)
            proj, ck, cv = _in_proj(xp, xs, modr, mix_w_in[e], li)
            dec = jnp.concatenate([ret_decay_fwd[e], ret_decay_bwd[e]])
            r_p, sf, sb = _retention(proj, dec, BATCH, SEQ, 0, HEADS, emit_state=True, chunk_len=SEQ)
            (r_s,) = _retention(proj, dec, DEC_BATCH, DEC_SEQ, T_PROMPT // DEC_SEQ, 2,
                                s0f=state_ret_fwd, s0b=state_ret_bwd, e=e, chunk_len=2 * RET_CHUNK)
            lams = (row(diff_lq1[e]), row(diff_lk1[e]), row(diff_lq2[e]), row(diff_lk2[e]),
                    row(diff_subln_g[e]))
            o_p = _attn_prompt(proj, *lams, lam_init)
            o_s = _attn_sample(proj, cache_diff_k, cache_diff_v, cos, sin_signed, *lams, lam_init, e)
            x1, u2, meta, cnt = _out_proj_tail(r_p, r_s, o_p, o_s, mix_w_out[e], xp, xs, modr,
                                               row(ln1_g[li]), row(ln1_b[li]), wr, br, li)
            caches = (ck, cv, sf, sb)
        else:
            o = li // 2
            x1, u2, meta, cnt = _conv_tail(glu, conv_dw[o], row(conv_dw_b[o]), row(conv_ln_g[o]),
                                           row(conv_ln_b[o]), conv_w2[o], row(conv_b2[o]),
                                           x, modr, row(ln1_g[li]), row(ln1_b[li]), wr, br, li)
        next_conv = {}
        if li + 1 < DEPTH and (li + 1) % 2 == 1:
            next_conv = dict(conv_w1=conv_w1[(li + 1) // 2], conv_b1=row(conv_b1[(li + 1) // 2]))
        outs = _moe(x1, u2, meta, cnt, modr, moe_w_gate, moe_w_up, moe_w_down,
                    row(ln2_g[li]), row(ln2_b[li]), li, split=(li == DEPTH - 1), **next_conv)
        x, glu = outs[0], outs[-1]

    y_prompt = outs[0].reshape(BATCH, SEQ, D)
    y_sample = outs[1].reshape(DEC_BATCH, DEC_SEQ, D)
    return (y_prompt, y_sample) + caches
```
